```python
import math
import jax, jax.numpy as jnp
from jax import lax
import numpy as np

D_MODEL = 1024
BATCH = 8
SEQ = 8192
DEPTH = 1

MEM_LEN = 256
N_BRANCH = 3
CONV_WIDTH = D_MODEL // 2
CONV_TAPS = 3
SSM_WIDTH = D_MODEL // 2
SSM_GROUP = 16
SSM_GROUPS = SSM_WIDTH // SSM_GROUP
SSM_STATE = 64
XATTN_HEADS = 4
XATTN_HEAD_DIM = 128
XATTN_WIDTH = XATTN_HEADS * XATTN_HEAD_DIM
D_FF = 4 * D_MODEL
GATE_COLS = N_BRANCH * D_MODEL
IN_COLS = GATE_COLS + 3 * CONV_WIDTH + SSM_WIDTH + XATTN_WIDTH
ALPHA = (2.0 * DEPTH) ** 0.25
BETA = (8.0 * DEPTH) ** -0.25
LN_EPS = 1e-5
DT_MIN = 1e-3
DT_MAX = 1e-1

kernel_name = "hybrid_gated_conv_s5_xattn_deepnorm"


def layer_norm(x, g, b):
    xf = x.astype(jnp.float32)
    mu = jnp.mean(xf, axis=-1, keepdims=True)
    var = jnp.mean(jnp.square(xf - mu), axis=-1, keepdims=True)
    y = (xf - mu) * lax.rsqrt(var + LN_EPS) * g.astype(jnp.float32) + b.astype(jnp.float32)
    return y.astype(x.dtype)


def causal_dwconv(z, w):
    s = z.shape[1]
    zp = jnp.pad(z, ((0, 0), (CONV_TAPS - 1, 0), (0, 0)))
    y = w[0] * zp[:, 0:s]
    for k in range(1, CONV_TAPS):
        y = y + w[k] * zp[:, k:k + s]
    return y


def _complex_affine_combine(e1, e2):
    a1r, a1i, b1r, b1i = e1
    a2r, a2i, b2r, b2i = e2
    ar = a1r * a2r - a1i * a2i
    ai = a1r * a2i + a1i * a2r
    br = a2r * b1r - a2i * b1i + b2r
    bi = a2r * b1i + a2i * b1r + b2i
    return (ar, ai, br, bi)


def s5_mixer(u, lam_re, lam_im, log_dt, b_re, b_im, c_re, c_im, d_skip):
    bsz, s, _ = u.shape
    f32 = jnp.float32
    uf = u.astype(f32).reshape(bsz, s, SSM_GROUPS, SSM_GROUP)
    lr = lam_re.astype(f32)
    li = lam_im.astype(f32)
    dt = jnp.exp(log_dt.astype(f32))[:, None]
    mag = jnp.exp(lr * dt)
    abar_r = mag * jnp.cos(li * dt)
    abar_i = mag * jnp.sin(li * dt)
    den = lr * lr + li * li
    nr = abar_r - 1.0
    ni = abar_i
    kr = (nr * lr + ni * li) / den
    ki = (ni * lr - nr * li) / den
    br_ = b_re.astype(f32)
    bi_ = b_im.astype(f32)
    bbar_r = kr[..., None] * br_ - ki[..., None] * bi_
    bbar_i = kr[..., None] * bi_ + ki[..., None] * br_
    bu_r = jnp.einsum("bsgh,gph->bsgp", uf, bbar_r)
    bu_i = jnp.einsum("bsgh,gph->bsgp", uf, bbar_i)
    a_r = jnp.broadcast_to(abar_r, bu_r.shape)
    a_i = jnp.broadcast_to(abar_i, bu_r.shape)
    _, _, st_r, st_i = lax.associative_scan(_complex_affine_combine, (a_r, a_i, bu_r, bu_i), axis=1)
    y = (jnp.einsum("bsgp,ghp->bsgh", st_r, c_re.astype(f32))
         - jnp.einsum("bsgp,ghp->bsgh", st_i, c_im.astype(f32)))
    y = y.reshape(bsz, s, SSM_WIDTH) + d_skip.astype(f32) * uf.reshape(bsz, s, SSM_WIDTH)
    return y.astype(u.dtype)


def memory_cross_attention(q, mem, w_kv, w_xo):
    bsz, s, _ = q.shape
    kv = jnp.einsum("bmd,dn->bmn", mem, w_kv)
    k, v = jnp.split(kv, 2, axis=-1)
    qh = q.reshape(bsz, s, XATTN_HEADS, XATTN_HEAD_DIM)
    kh = k.reshape(bsz, -1, XATTN_HEADS, XATTN_HEAD_DIM)
    vh = v.reshape(bsz, -1, XATTN_HEADS, XATTN_HEAD_DIM)
    scores = jnp.einsum("bshd,bmhd->bhsm", qh, kh).astype(jnp.float32) * (XATTN_HEAD_DIM ** -0.5)
    probs = jax.nn.softmax(scores, axis=-1).astype(vh.dtype)
    o = jnp.einsum("bhsm,bmhd->bshd", probs, vh).reshape(bsz, s, XATTN_WIDTH)
    return jnp.einsum("bsc,cd->bsd", o, w_xo)


def _fwd_setup_inputs(seed: int = 0) -> dict:
    key = jax.random.key(seed)
    ks = jax.random.split(key, 32)
    f32 = jnp.float32
    L, D = DEPTH, D_MODEL

    def nrm(k, shape, std):
        return jax.random.normal(k, shape, f32) * std

    x = jax.random.normal(ks[0], (BATCH, SEQ, D), f32)
    mem = jax.random.normal(ks[1], (BATCH, MEM_LEN, D), f32)
    w_in = nrm(ks[2], (L, D, IN_COLS), D ** -0.5)
    b_gate = nrm(ks[3], (L, GATE_COLS), 0.02)
    conv_w = nrm(ks[4], (L, CONV_TAPS, CONV_WIDTH), CONV_TAPS ** -0.5)
    w_conv_out = nrm(ks[5], (L, CONV_WIDTH, D), CONV_WIDTH ** -0.5)
    n_idx = jnp.arange(SSM_STATE, dtype=f32)
    ssm_lam_re = -0.5 + nrm(ks[6], (L, SSM_GROUPS, SSM_STATE), 0.01)
    ssm_lam_im = math.pi * n_idx + nrm(ks[7], (L, SSM_GROUPS, SSM_STATE), 0.01)
    ssm_log_dt = jax.random.uniform(ks[8], (L, SSM_GROUPS), f32, math.log(DT_MIN), math.log(DT_MAX))
    bstd = (2.0 * SSM_GROUP) ** -0.5
    ssm_b_re = nrm(ks[9], (L, SSM_GROUPS, SSM_STATE, SSM_GROUP), bstd)
    ssm_b_im = nrm(ks[10], (L, SSM_GROUPS, SSM_STATE, SSM_GROUP), bstd)
    cstd = (2.0 * SSM_STATE) ** -0.5
    ssm_c_re = nrm(ks[11], (L, SSM_GROUPS, SSM_GROUP, SSM_STATE), cstd)
    ssm_c_im = nrm(ks[12], (L, SSM_GROUPS, SSM_GROUP, SSM_STATE), cstd)
    ssm_d = nrm(ks[13], (L, SSM_WIDTH), 1.0)
    w_glu = nrm(ks[14], (L, SSM_WIDTH, 2 * D), SSM_WIDTH ** -0.5)
    w_k = nrm(ks[15], (L, D, XATTN_WIDTH), D ** -0.5)
    w_v = nrm(ks[16], (L, D, XATTN_WIDTH), BETA * D ** -0.5)
    w_kv = jnp.concatenate([w_k, w_v], axis=-1)
    w_xattn_out = nrm(ks[17], (L, XATTN_WIDTH, D), XATTN_WIDTH ** -0.5)
    w_out = nrm(ks[18], (L, D, D), BETA * D ** -0.5)
    ln1_g = 1.0 + nrm(ks[19], (L, D), 0.02)
    ln1_b = nrm(ks[20], (L, D), 0.02)
    w_up = nrm(ks[21], (L, D, D_FF), BETA * D ** -0.5)
    b_up = nrm(ks[22], (L, D_FF), 0.02)
    w_down = nrm(ks[23], (L, D_FF, D), BETA * D_FF ** -0.5)
    b_down = nrm(ks[24], (L, D), 0.02)
    ln2_g = 1.0 + nrm(ks[25], (L, D), 0.02)
    ln2_b = nrm(ks[26], (L, D), 0.02)
    return {"x": x, "mem": mem, "w_in": w_in, "b_gate": b_gate, "conv_w": conv_w,
            "w_conv_out": w_conv_out, "ssm_lam_re": ssm_lam_re, "ssm_lam_im": ssm_lam_im,
            "ssm_log_dt": ssm_log_dt, "ssm_b_re": ssm_b_re, "ssm_b_im": ssm_b_im,
            "ssm_c_re": ssm_c_re, "ssm_c_im": ssm_c_im, "ssm_d": ssm_d, "w_glu": w_glu,
            "w_kv": w_kv, "w_xattn_out": w_xattn_out, "w_out": w_out, "ln1_g": ln1_g,
            "ln1_b": ln1_b, "w_up": w_up, "b_up": b_up, "w_down": w_down, "b_down": b_down,
            "ln2_g": ln2_g, "ln2_b": ln2_b}


def _fwd_reference(x, mem, w_in, b_gate, conv_w, w_conv_out, ssm_lam_re, ssm_lam_im, ssm_log_dt,
              ssm_b_re, ssm_b_im, ssm_c_re, ssm_c_im, ssm_d, w_glu, w_kv, w_xattn_out, w_out,
              ln1_g, ln1_b, w_up, b_up, w_down, b_down, ln2_g, ln2_b):
    bsz, s, d = x.shape
    splits = [GATE_COLS, GATE_COLS + 3 * CONV_WIDTH, GATE_COLS + 3 * CONV_WIDTH + SSM_WIDTH]
    for l in range(DEPTH):
        proj = jnp.einsum("bsd,dn->bsn", x, w_in[l])
        gate_pre, conv_in, u, q = jnp.split(proj, splits, axis=-1)
        gates = jax.nn.sigmoid(gate_pre + b_gate[l]).reshape(bsz, s, N_BRANCH, d)

        cb, cc, ch = jnp.split(conv_in, 3, axis=-1)
        y_a = jnp.einsum("bsc,cd->bsd", cb * causal_dwconv(cc * ch, conv_w[l]), w_conv_out[l])

        y_s = jax.nn.gelu(s5_mixer(u, ssm_lam_re[l], ssm_lam_im[l], ssm_log_dt[l], ssm_b_re[l],
                                   ssm_b_im[l], ssm_c_re[l], ssm_c_im[l], ssm_d[l]))
        glu_a, glu_b = jnp.split(jnp.einsum("bsc,cn->bsn", y_s, w_glu[l]), 2, axis=-1)
        y_b = glu_a * jax.nn.sigmoid(glu_b)

        y_c = memory_cross_attention(q, mem, w_kv[l], w_xattn_out[l])

        merged = gates[:, :, 0] * y_a + gates[:, :, 1] * y_b + gates[:, :, 2] * y_c
        x = layer_norm(ALPHA * x + jnp.einsum("bsd,de->bse", merged, w_out[l]), ln1_g[l], ln1_b[l])

        hdn = jnp.square(jax.nn.relu(jnp.einsum("bsd,df->bsf", x, w_up[l]) + b_up[l]))
        x = layer_norm(ALPHA * x + jnp.einsum("bsf,fd->bsd", hdn, w_down[l]) + b_down[l],
                       ln2_g[l], ln2_b[l])
    return x


import jax as _jax
import jax.numpy as _jnp

TWIN_FORMAT = 'train_step'
FWD_PARAMS = ['x', 'mem', 'w_in', 'b_gate', 'conv_w', 'w_conv_out', 'ssm_lam_re', 'ssm_lam_im', 'ssm_log_dt', 'ssm_b_re', 'ssm_b_im', 'ssm_c_re', 'ssm_c_im', 'ssm_d', 'w_glu', 'w_kv', 'w_xattn_out', 'w_out', 'ln1_g', 'ln1_b', 'w_up', 'b_up', 'w_down', 'b_down', 'ln2_g', 'ln2_b']
TWIN_WEIGHTS = ['w_in', 'b_gate', 'conv_w', 'w_conv_out', 'ssm_lam_re', 'ssm_lam_im', 'ssm_log_dt', 'ssm_b_re', 'ssm_b_im', 'ssm_c_re', 'ssm_c_im', 'ssm_d', 'w_glu', 'w_kv', 'w_xattn_out', 'w_out', 'ln1_g', 'ln1_b', 'w_up', 'b_up', 'w_down', 'b_down', 'ln2_g', 'ln2_b']
TWIN_DIFF_INPUT = 'x'
TWIN_INPUTS = ['x', 'mem', 'w_in', 'b_gate', 'conv_w', 'w_conv_out', 'ssm_lam_re', 'ssm_lam_im', 'ssm_log_dt', 'ssm_b_re', 'ssm_b_im', 'ssm_c_re', 'ssm_c_im', 'ssm_d', 'w_glu', 'w_kv', 'w_xattn_out', 'w_out', 'ln1_g', 'ln1_b', 'w_up', 'b_up', 'w_down', 'b_down', 'ln2_g', 'ln2_b', 'loss_target', 'm_w_in', 'm_b_gate', 'm_conv_w', 'm_w_conv_out', 'm_ssm_lam_re', 'm_ssm_lam_im', 'm_ssm_log_dt', 'm_ssm_b_re', 'm_ssm_b_im', 'm_ssm_c_re', 'm_ssm_c_im', 'm_ssm_d', 'm_w_glu', 'm_w_kv', 'm_w_xattn_out', 'm_w_out', 'm_ln1_g', 'm_ln1_b', 'm_w_up', 'm_b_up', 'm_w_down', 'm_b_down', 'm_ln2_g', 'm_ln2_b', 'v_w_in', 'v_b_gate', 'v_conv_w', 'v_w_conv_out', 'v_ssm_lam_re', 'v_ssm_lam_im', 'v_ssm_log_dt', 'v_ssm_b_re', 'v_ssm_b_im', 'v_ssm_c_re', 'v_ssm_c_im', 'v_ssm_d', 'v_w_glu', 'v_w_kv', 'v_w_xattn_out', 'v_w_out', 'v_ln1_g', 'v_ln1_b', 'v_w_up', 'v_b_up', 'v_w_down', 'v_b_down', 'v_ln2_g', 'v_ln2_b']
TWIN_OUTPUTS = ['loss', 'grad_x', 'grad_w_in', 'grad_b_gate', 'grad_conv_w', 'grad_w_conv_out', 'grad_ssm_lam_re', 'grad_ssm_lam_im', 'grad_ssm_log_dt', 'grad_ssm_b_re', 'grad_ssm_b_im', 'grad_ssm_c_re', 'grad_ssm_c_im', 'grad_ssm_d', 'grad_w_glu', 'grad_w_kv', 'grad_w_xattn_out', 'grad_w_out', 'grad_ln1_g', 'grad_ln1_b', 'grad_w_up', 'grad_b_up', 'grad_w_down', 'grad_b_down', 'grad_ln2_g', 'grad_ln2_b', 'delta_w_in', 'delta_b_gate', 'delta_conv_w', 'delta_w_conv_out', 'delta_ssm_lam_re', 'delta_ssm_lam_im', 'delta_ssm_log_dt', 'delta_ssm_b_re', 'delta_ssm_b_im', 'delta_ssm_c_re', 'delta_ssm_c_im', 'delta_ssm_d', 'delta_w_glu', 'delta_w_kv', 'delta_w_xattn_out', 'delta_w_out', 'delta_ln1_g', 'delta_ln1_b', 'delta_w_up', 'delta_b_up', 'delta_w_down', 'delta_b_down', 'delta_ln2_g', 'delta_ln2_b', 'new_m_w_in', 'new_m_b_gate', 'new_m_conv_w', 'new_m_w_conv_out', 'new_m_ssm_lam_re', 'new_m_ssm_lam_im', 'new_m_ssm_log_dt', 'new_m_ssm_b_re', 'new_m_ssm_b_im', 'new_m_ssm_c_re', 'new_m_ssm_c_im', 'new_m_ssm_d', 'new_m_w_glu', 'new_m_w_kv', 'new_m_w_xattn_out', 'new_m_w_out', 'new_m_ln1_g', 'new_m_ln1_b', 'new_m_w_up', 'new_m_b_up', 'new_m_w_down', 'new_m_b_down', 'new_m_ln2_g', 'new_m_ln2_b', 'new_v_w_in', 'new_v_b_gate', 'new_v_conv_w', 'new_v_w_conv_out', 'new_v_ssm_lam_re', 'new_v_ssm_lam_im', 'new_v_ssm_log_dt', 'new_v_ssm_b_re', 'new_v_ssm_b_im', 'new_v_ssm_c_re', 'new_v_ssm_c_im', 'new_v_ssm_d', 'new_v_w_glu', 'new_v_w_kv', 'new_v_w_xattn_out', 'new_v_w_out', 'new_v_ln1_g', 'new_v_ln1_b', 'new_v_w_up', 'new_v_b_up', 'new_v_w_down', 'new_v_b_down', 'new_v_ln2_g', 'new_v_ln2_b']
TWIN_LEAF_KINDS = {'loss': 'loss', 'grad_x': 'grad_x', 'grad_w_in': 'grad_w', 'grad_b_gate': 'grad_w', 'grad_conv_w': 'grad_w', 'grad_w_conv_out': 'grad_w', 'grad_ssm_lam_re': 'grad_w', 'grad_ssm_lam_im': 'grad_w', 'grad_ssm_log_dt': 'grad_w', 'grad_ssm_b_re': 'grad_w', 'grad_ssm_b_im': 'grad_w', 'grad_ssm_c_re': 'grad_w', 'grad_ssm_c_im': 'grad_w', 'grad_ssm_d': 'grad_w', 'grad_w_glu': 'grad_w', 'grad_w_kv': 'grad_w', 'grad_w_xattn_out': 'grad_w', 'grad_w_out': 'grad_w', 'grad_ln1_g': 'grad_w', 'grad_ln1_b': 'grad_w', 'grad_w_up': 'grad_w', 'grad_b_up': 'grad_w', 'grad_w_down': 'grad_w', 'grad_b_down': 'grad_w', 'grad_ln2_g': 'grad_w', 'grad_ln2_b': 'grad_w', 'delta_w_in': 'delta_w', 'delta_b_gate': 'delta_w', 'delta_conv_w': 'delta_w', 'delta_w_conv_out': 'delta_w', 'delta_ssm_lam_re': 'delta_w', 'delta_ssm_lam_im': 'delta_w', 'delta_ssm_log_dt': 'delta_w', 'delta_ssm_b_re': 'delta_w', 'delta_ssm_b_im': 'delta_w', 'delta_ssm_c_re': 'delta_w', 'delta_ssm_c_im': 'delta_w', 'delta_ssm_d': 'delta_w', 'delta_w_glu': 'delta_w', 'delta_w_kv': 'delta_w', 'delta_w_xattn_out': 'delta_w', 'delta_w_out': 'delta_w', 'delta_ln1_g': 'delta_w', 'delta_ln1_b': 'delta_w', 'delta_w_up': 'delta_w', 'delta_b_up': 'delta_w', 'delta_w_down': 'delta_w', 'delta_b_down': 'delta_w', 'delta_ln2_g': 'delta_w', 'delta_ln2_b': 'delta_w', 'new_m_w_in': 'new_m', 'new_m_b_gate': 'new_m', 'new_m_conv_w': 'new_m', 'new_m_w_conv_out': 'new_m', 'new_m_ssm_lam_re': 'new_m', 'new_m_ssm_lam_im': 'new_m', 'new_m_ssm_log_dt': 'new_m', 'new_m_ssm_b_re': 'new_m', 'new_m_ssm_b_im': 'new_m', 'new_m_ssm_c_re': 'new_m', 'new_m_ssm_c_im': 'new_m', 'new_m_ssm_d': 'new_m', 'new_m_w_glu': 'new_m', 'new_m_w_kv': 'new_m', 'new_m_w_xattn_out': 'new_m', 'new_m_w_out': 'new_m', 'new_m_ln1_g': 'new_m', 'new_m_ln1_b': 'new_m', 'new_m_w_up': 'new_m', 'new_m_b_up': 'new_m', 'new_m_w_down': 'new_m', 'new_m_b_down': 'new_m', 'new_m_ln2_g': 'new_m', 'new_m_ln2_b': 'new_m', 'new_v_w_in': 'new_v', 'new_v_b_gate': 'new_v', 'new_v_conv_w': 'new_v', 'new_v_w_conv_out': 'new_v', 'new_v_ssm_lam_re': 'new_v', 'new_v_ssm_lam_im': 'new_v', 'new_v_ssm_log_dt': 'new_v', 'new_v_ssm_b_re': 'new_v', 'new_v_ssm_b_im': 'new_v', 'new_v_ssm_c_re': 'new_v', 'new_v_ssm_c_im': 'new_v', 'new_v_ssm_d': 'new_v', 'new_v_w_glu': 'new_v', 'new_v_w_kv': 'new_v', 'new_v_w_xattn_out': 'new_v', 'new_v_w_out': 'new_v', 'new_v_ln1_g': 'new_v', 'new_v_ln1_b': 'new_v', 'new_v_w_up': 'new_v', 'new_v_b_up': 'new_v', 'new_v_w_down': 'new_v', 'new_v_b_down': 'new_v', 'new_v_ln2_g': 'new_v', 'new_v_ln2_b': 'new_v'}


def _forward(args):
    return _fwd_reference(*[args[k] for k in FWD_PARAMS])


def _output_shape():
    def fwd():
        inp = _fwd_setup_inputs(0)
        return _fwd_reference(*[inp[k] for k in FWD_PARAMS])
    out = _jax.eval_shape(fwd)
    return out.shape, out.dtype

N_MICROBATCH = 1
ADAM_LR = 0.001
ADAM_B1 = 0.9
ADAM_B2 = 0.999
ADAM_EPS = 1e-08
ADAM_WD = 0.01
ADAM_STEP = 10
PER_EXAMPLE_BATCH_AXIS = {'x': 0, 'mem': 0, 'loss_target': 0}
SHARED_INPUTS = []
_WEIGHT_DTYPES = {'w_in': _jnp.float32, 'b_gate': _jnp.float32, 'conv_w': _jnp.float32, 'w_conv_out': _jnp.float32, 'ssm_lam_re': _jnp.float32, 'ssm_lam_im': _jnp.float32, 'ssm_log_dt': _jnp.float32, 'ssm_b_re': _jnp.float32, 'ssm_b_im': _jnp.float32, 'ssm_c_re': _jnp.float32, 'ssm_c_im': _jnp.float32, 'ssm_d': _jnp.float32, 'w_glu': _jnp.float32, 'w_kv': _jnp.float32, 'w_xattn_out': _jnp.float32, 'w_out': _jnp.float32, 'ln1_g': _jnp.float32, 'ln1_b': _jnp.float32, 'w_up': _jnp.float32, 'b_up': _jnp.float32, 'w_down': _jnp.float32, 'b_down': _jnp.float32, 'ln2_g': _jnp.float32, 'ln2_b': _jnp.float32}
MOMENT_SCALE = {'w_in': 5.065542e-02, 'b_gate': 1.656905e-02, 'conv_w': 9.763336e-02, 'w_conv_out': 6.490996e-02, 'ssm_lam_re': 1.560222e-03, 'ssm_lam_im': 1.704593e-03, 'ssm_log_dt': 1.823225e+00, 'ssm_b_re': 1.129068e-03, 'ssm_b_im': 1.096102e-03, 'ssm_c_re': 2.176240e-03, 'ssm_c_im': 2.208188e-03, 'ssm_d': 5.928632e-02, 'w_glu': 2.954075e-02, 'w_kv': 8.922514e-03, 'w_xattn_out': 4.736112e-03, 'w_out': 1.263135e-01, 'ln1_g': 2.257888e+00, 'ln1_b': 1.036711e+00, 'w_up': 5.177943e-02, 'b_up': 1.155212e-01, 'w_down': 1.450804e-01, 'b_down': 8.697393e-01, 'ln2_g': 6.404039e+01, 'ln2_b': 6.250480e+00}


def _to_microbatches(a, axis):
    t = _jnp.moveaxis(a, axis, 0)
    t = t.reshape((N_MICROBATCH, t.shape[0] // N_MICROBATCH) + t.shape[1:])
    return _jnp.moveaxis(t, 1, axis + 1)


def setup_inputs(seed: int = 0) -> dict:
    inp = _fwd_setup_inputs(seed)
    key = _jax.random.fold_in(_jax.random.key(seed), 7919)
    shape, _ = _output_shape()
    out = dict(inp)
    out["loss_target"] = _jax.random.normal(_jax.random.fold_in(key, 0), shape, _jnp.float32)
    for i, name in enumerate(TWIN_WEIGHTS):
        w = inp[name].astype(_jnp.float32)
        if MOMENT_SCALE is None:
            s = _jnp.sqrt(_jnp.mean(_jnp.square(w)) + 1e-30)
        else:
            s = MOMENT_SCALE[name]
        km, kv = _jax.random.split(_jax.random.fold_in(key, i + 1))
        out[name] = w
        out["m_" + name] = s * _jax.random.normal(km, w.shape, _jnp.float32)
        out["v_" + name] = (s * s) * _jax.random.uniform(kv, w.shape, _jnp.float32, 0.5, 1.5)
    if N_MICROBATCH > 1:
        for name, axis in PER_EXAMPLE_BATCH_AXIS.items():
            out[name] = _to_microbatches(out[name], axis)
    return {'x': out['x'], 'mem': out['mem'], 'w_in': out['w_in'], 'b_gate': out['b_gate'], 'conv_w': out['conv_w'], 'w_conv_out': out['w_conv_out'], 'ssm_lam_re': out['ssm_lam_re'], 'ssm_lam_im': out['ssm_lam_im'], 'ssm_log_dt': out['ssm_log_dt'], 'ssm_b_re': out['ssm_b_re'], 'ssm_b_im': out['ssm_b_im'], 'ssm_c_re': out['ssm_c_re'], 'ssm_c_im': out['ssm_c_im'], 'ssm_d': out['ssm_d'], 'w_glu': out['w_glu'], 'w_kv': out['w_kv'], 'w_xattn_out': out['w_xattn_out'], 'w_out': out['w_out'], 'ln1_g': out['ln1_g'], 'ln1_b': out['ln1_b'], 'w_up': out['w_up'], 'b_up': out['b_up'], 'w_down': out['w_down'], 'b_down': out['b_down'], 'ln2_g': out['ln2_g'], 'ln2_b': out['ln2_b'], 'loss_target': out['loss_target'], 'm_w_in': out['m_w_in'], 'm_b_gate': out['m_b_gate'], 'm_conv_w': out['m_conv_w'], 'm_w_conv_out': out['m_w_conv_out'], 'm_ssm_lam_re': out['m_ssm_lam_re'], 'm_ssm_lam_im': out['m_ssm_lam_im'], 'm_ssm_log_dt': out['m_ssm_log_dt'], 'm_ssm_b_re': out['m_ssm_b_re'], 'm_ssm_b_im': out['m_ssm_b_im'], 'm_ssm_c_re': out['m_ssm_c_re'], 'm_ssm_c_im': out['m_ssm_c_im'], 'm_ssm_d': out['m_ssm_d'], 'm_w_glu': out['m_w_glu'], 'm_w_kv': out['m_w_kv'], 'm_w_xattn_out': out['m_w_xattn_out'], 'm_w_out': out['m_w_out'], 'm_ln1_g': out['m_ln1_g'], 'm_ln1_b': out['m_ln1_b'], 'm_w_up': out['m_w_up'], 'm_b_up': out['m_b_up'], 'm_w_down': out['m_w_down'], 'm_b_down': out['m_b_down'], 'm_ln2_g': out['m_ln2_g'], 'm_ln2_b': out['m_ln2_b'], 'v_w_in': out['v_w_in'], 'v_b_gate': out['v_b_gate'], 'v_conv_w': out['v_conv_w'], 'v_w_conv_out': out['v_w_conv_out'], 'v_ssm_lam_re': out['v_ssm_lam_re'], 'v_ssm_lam_im': out['v_ssm_lam_im'], 'v_ssm_log_dt': out['v_ssm_log_dt'], 'v_ssm_b_re': out['v_ssm_b_re'], 'v_ssm_b_im': out['v_ssm_b_im'], 'v_ssm_c_re': out['v_ssm_c_re'], 'v_ssm_c_im': out['v_ssm_c_im'], 'v_ssm_d': out['v_ssm_d'], 'v_w_glu': out['v_w_glu'], 'v_w_kv': out['v_w_kv'], 'v_w_xattn_out': out['v_w_xattn_out'], 'v_w_out': out['v_w_out'], 'v_ln1_g': out['v_ln1_g'], 'v_ln1_b': out['v_ln1_b'], 'v_w_up': out['v_w_up'], 'v_b_up': out['v_b_up'], 'v_w_down': out['v_w_down'], 'v_b_down': out['v_b_down'], 'v_ln2_g': out['v_ln2_g'], 'v_ln2_b': out['v_ln2_b']}


def _loss(weights, diff, rest, loss_target):
    with _jax.named_scope("forward"):
        args = {**rest, TWIN_DIFF_INPUT: diff, **{k: w.astype(_WEIGHT_DTYPES[k]) for k, w in weights.items()}}
        y = _forward(args)
    with _jax.named_scope("loss_head"):
        err = _jnp.square(y.astype(_jnp.float32) - loss_target)
        return 0.5 * _jnp.sum(_jnp.mean(err, axis=-1)) if err.ndim else 0.5 * err


def _adamw(w, g, m, v):
    m = ADAM_B1 * m + (1.0 - ADAM_B1) * g
    v = ADAM_B2 * v + (1.0 - ADAM_B2) * _jnp.square(g)
    m_hat = m / (1.0 - ADAM_B1 ** ADAM_STEP)
    v_hat = v / (1.0 - ADAM_B2 ** ADAM_STEP)
    delta = -ADAM_LR * (m_hat / (_jnp.sqrt(v_hat) + ADAM_EPS) + ADAM_WD * w)
    return delta, m, v


def reference(x, mem, w_in, b_gate, conv_w, w_conv_out, ssm_lam_re, ssm_lam_im, ssm_log_dt, ssm_b_re, ssm_b_im, ssm_c_re, ssm_c_im, ssm_d, w_glu, w_kv, w_xattn_out, w_out, ln1_g, ln1_b, w_up, b_up, w_down, b_down, ln2_g, ln2_b, loss_target, m_w_in, m_b_gate, m_conv_w, m_w_conv_out, m_ssm_lam_re, m_ssm_lam_im, m_ssm_log_dt, m_ssm_b_re, m_ssm_b_im, m_ssm_c_re, m_ssm_c_im, m_ssm_d, m_w_glu, m_w_kv, m_w_xattn_out, m_w_out, m_ln1_g, m_ln1_b, m_w_up, m_b_up, m_w_down, m_b_down, m_ln2_g, m_ln2_b, v_w_in, v_b_gate, v_conv_w, v_w_conv_out, v_ssm_lam_re, v_ssm_lam_im, v_ssm_log_dt, v_ssm_b_re, v_ssm_b_im, v_ssm_c_re, v_ssm_c_im, v_ssm_d, v_w_glu, v_w_kv, v_w_xattn_out, v_w_out, v_ln1_g, v_ln1_b, v_w_up, v_b_up, v_w_down, v_b_down, v_ln2_g, v_ln2_b):
    given = dict(x=x, mem=mem, w_in=w_in, b_gate=b_gate, conv_w=conv_w, w_conv_out=w_conv_out, ssm_lam_re=ssm_lam_re, ssm_lam_im=ssm_lam_im, ssm_log_dt=ssm_log_dt, ssm_b_re=ssm_b_re, ssm_b_im=ssm_b_im, ssm_c_re=ssm_c_re, ssm_c_im=ssm_c_im, ssm_d=ssm_d, w_glu=w_glu, w_kv=w_kv, w_xattn_out=w_xattn_out, w_out=w_out, ln1_g=ln1_g, ln1_b=ln1_b, w_up=w_up, b_up=b_up, w_down=w_down, b_down=b_down, ln2_g=ln2_g, ln2_b=ln2_b, loss_target=loss_target, m_w_in=m_w_in, m_b_gate=m_b_gate, m_conv_w=m_conv_w, m_w_conv_out=m_w_conv_out, m_ssm_lam_re=m_ssm_lam_re, m_ssm_lam_im=m_ssm_lam_im, m_ssm_log_dt=m_ssm_log_dt, m_ssm_b_re=m_ssm_b_re, m_ssm_b_im=m_ssm_b_im, m_ssm_c_re=m_ssm_c_re, m_ssm_c_im=m_ssm_c_im, m_ssm_d=m_ssm_d, m_w_glu=m_w_glu, m_w_kv=m_w_kv, m_w_xattn_out=m_w_xattn_out, m_w_out=m_w_out, m_ln1_g=m_ln1_g, m_ln1_b=m_ln1_b, m_w_up=m_w_up, m_b_up=m_b_up, m_w_down=m_w_down, m_b_down=m_b_down, m_ln2_g=m_ln2_g, m_ln2_b=m_ln2_b, v_w_in=v_w_in, v_b_gate=v_b_gate, v_conv_w=v_conv_w, v_w_conv_out=v_w_conv_out, v_ssm_lam_re=v_ssm_lam_re, v_ssm_lam_im=v_ssm_lam_im, v_ssm_log_dt=v_ssm_log_dt, v_ssm_b_re=v_ssm_b_re, v_ssm_b_im=v_ssm_b_im, v_ssm_c_re=v_ssm_c_re, v_ssm_c_im=v_ssm_c_im, v_ssm_d=v_ssm_d, v_w_glu=v_w_glu, v_w_kv=v_w_kv, v_w_xattn_out=v_w_xattn_out, v_w_out=v_w_out, v_ln1_g=v_ln1_g, v_ln1_b=v_ln1_b, v_w_up=v_w_up, v_b_up=v_b_up, v_w_down=v_w_down, v_b_down=v_b_down, v_ln2_g=v_ln2_g, v_ln2_b=v_ln2_b)
    weights = {n: given[n] for n in TWIN_WEIGHTS}
    shared = {n: given[n] for n in SHARED_INPUTS}
    per_example = {n: given[n] for n in ['x', 'mem']}
    grad_fn = _jax.value_and_grad(_loss, argnums=(0, 1))

    def one_microbatch(ex, loss_target):
        ex = dict(ex)
        diff = ex.pop(TWIN_DIFF_INPUT)
        return grad_fn(weights, diff, {**shared, **ex}, loss_target)

    if N_MICROBATCH == 1:
        loss, (grad_w, grad_x) = one_microbatch(per_example, given["loss_target"])
    else:
        def body(carry, xs):
            loss_sum, grad_sum = carry
            l_k, (gw_k, gx_k) = one_microbatch(xs[0], xs[1])
            with _jax.named_scope("update"):
                return (loss_sum + l_k, _jax.tree.map(_jnp.add, grad_sum, gw_k)), gx_k

        init = (_jnp.zeros((), _jnp.float32), _jax.tree.map(_jnp.zeros_like, weights))
        (loss, grad_w), grad_x = _jax.lax.scan(body, init, (per_example, given["loss_target"]))
    with _jax.named_scope("update"):
        delta_w, new_m, new_v = {}, {}, {}
        for n in TWIN_WEIGHTS:
            delta_w[n], new_m[n], new_v[n] = _adamw(weights[n], grad_w[n], given["m_" + n], given["v_" + n])
    return (loss, grad_x, *[grad_w[n] for n in TWIN_WEIGHTS], *[delta_w[n] for n in TWIN_WEIGHTS],
            *[new_m[n] for n in TWIN_WEIGHTS], *[new_v[n] for n in TWIN_WEIGHTS])
```

```python
import functools
import math

import jax
import jax.numpy as jnp
from jax import lax
from jax.experimental import pallas as pl
from jax.experimental.pallas import tpu as pltpu

f32 = jnp.float32
bf16 = jnp.bfloat16

D_MODEL = 1024
MEM_LEN = 256
GATE_COLS = 3 * D_MODEL
CONV_W = 512
SSM_W = 512
XATTN_W = 512
HEADS = 4
HEAD_DIM = 128
D_FF = 4096
IN_COLS = GATE_COLS + 3 * CONV_W + SSM_W + XATTN_W
SSM_GROUPS = 32
SSM_GROUP = 16
SSM_STATE = 64
N_STATE = SSM_GROUPS * SSM_STATE
ALPHA = 2.0 ** 0.25
LN_EPS = 1e-5
N_DEV = 8

ADAM_LR = 0.001
ADAM_B1 = 0.9
ADAM_B2 = 0.999
ADAM_EPS = 1e-08
ADAM_WD = 0.01
ADAM_STEP = 10

VMEM_LIMIT_V7X = 56 * 2 ** 20
SUBLANES = 8

TOKEN_TILE = 256
SSM_BLOCK = 256
SSM_SEG = SSM_BLOCK // SUBLANES
LANE_CHUNK = 512

NT = (((1,), (1,)), ((), ()))
TN = (((0,), (0,)), ((), ()))
NN = (((1,), (0,)), ((), ()))


def _dot(a, b, dims=NN):
    return lax.dot_general(a, b, dims, preferred_element_type=f32)


def _cparams(sem=None):
    return pltpu.CompilerParams(dimension_semantics=sem, vmem_limit_bytes=VMEM_LIMIT_V7X)


def _row_spec(tm, cols, rev_n=None):
    if rev_n is None:
        return pl.BlockSpec((tm, cols), lambda i: (i, 0))
    return pl.BlockSpec((tm, cols), lambda i: (rev_n - 1 - i, 0))


def _const_spec(shape):
    nd = len(shape)
    return pl.BlockSpec(shape, lambda *_: (0,) * nd, pipeline_mode=pl.Buffered(1))


def _acc_spec(shape):
    nd = len(shape)
    return pl.BlockSpec(shape, lambda *_: (0,) * nd)


def _sds(shape, dtype):
    return jax.ShapeDtypeStruct(shape, dtype)


def _gelu(x):
    c = math.sqrt(2.0 / math.pi)
    return 0.5 * x * (1.0 + jnp.tanh(c * (x + 0.044715 * x * x * x)))


def _gelu_grad(x):
    c = math.sqrt(2.0 / math.pi)
    t = jnp.tanh(c * (x + 0.044715 * x * x * x))
    return 0.5 * (1.0 + t) + 0.5 * x * (1.0 - t * t) * c * (1.0 + 3.0 * 0.044715 * x * x)


def _colsum(a):
    return jnp.sum(a, axis=0, keepdims=True)


def _mesh_place():
    x, y, c = lax.axis_index("x"), lax.axis_index("y"), lax.axis_index("c")
    return x, y, c


def _slot(p):
    return 4 * p[0] + 2 * p[1] + p[2]


def _all_gather(blocks, name):
    n = len(blocks)

    def body(*refs):
        ins, outs = refs[:n], refs[n:2 * n]
        send_sems, recv_sems, local_sems = refs[2 * n:]
        x, y, c = _mesh_place()
        me, sibling = (x, y, c), (x, y, 1 - c)
        chips = [(1 - x, y), (x, 1 - y), (1 - x, 1 - y)]

        def copy(a, k, block, to, src=None):
            rows = outs[a].at[_slot(block)]
            return pltpu.make_async_remote_copy(
                src_ref=rows if src is None else src, dst_ref=rows,
                send_sem=send_sems.at[a, k], recv_sem=recv_sems.at[a, k],
                device_id=to, device_id_type=pl.DeviceIdType.MESH)

        mine = [pltpu.make_async_copy(ins[a], outs[a].at[_slot(me)], local_sems.at[a]) for a in range(n)]
        for cp in mine:
            cp.start()
        first = []
        for a in range(n):
            first.append(copy(a, 0, me, sibling, src=ins[a]))
            first += [copy(a, 1 + j, me, (*chip, c), src=ins[a]) for j, chip in enumerate(chips)]
        for cp in first:
            cp.start()
        passed = []
        for a in range(n):
            for j, chip in enumerate(chips):
                copy(a, 1 + j, (*chip, c), me).wait_recv()
                fwd = copy(a, 4 + j, (*chip, c), sibling)
                fwd.start()
                passed.append(fwd)
        for a in range(n):
            copy(a, 0, sibling, me).wait_recv()
            for j, chip in enumerate(chips):
                copy(a, 4 + j, (*chip, 1 - c), me).wait_recv()
        for cp in first + passed:
            cp.wait_send()
        for cp in mine:
            cp.wait()

    any_spec = pl.BlockSpec(memory_space=pl.ANY)
    return pl.pallas_call(
        body, name=name,
        out_shape=[_sds((N_DEV,) + b.shape, b.dtype) for b in blocks],
        in_specs=[any_spec] * n, out_specs=[any_spec] * n,
        scratch_shapes=[pltpu.SemaphoreType.DMA((n, 7)), pltpu.SemaphoreType.DMA((n, 7)),
                        pltpu.SemaphoreType.DMA((n,))],
    )(*blocks)


def _all_to_all(stacks, name):
    n = len(stacks)

    def body(*refs):
        ins, outs = refs[:n], refs[n:2 * n]
        send_sems, recv_sems, local_sems = refs[2 * n:]
        x, y, c = _mesh_place()
        me = (x, y, c)
        flip = lambda v, d: 1 - v if d else v
        peers = [(flip(x, dx), flip(y, dy), flip(c, dc)) for dx in (0, 1) for dy in (0, 1) for dc in (0, 1)][1:]

        def copy(a, k, frm, to):
            return pltpu.make_async_remote_copy(
                src_ref=ins[a].at[_slot(to)], dst_ref=outs[a].at[_slot(frm)],
                send_sem=send_sems.at[a, k], recv_sem=recv_sems.at[a, k],
                device_id=to, device_id_type=pl.DeviceIdType.MESH)

        mine = [pltpu.make_async_copy(ins[a].at[_slot(me)], outs[a].at[_slot(me)], local_sems.at[a])
                for a in range(n)]
        for cp in mine:
            cp.start()
        sends = [copy(a, k, me, peer) for a in range(n) for k, peer in enumerate(peers)]
        for cp in sends:
            cp.start()
        for a in range(n):
            for k, peer in enumerate(peers):
                copy(a, k, peer, me).wait_recv()
        for cp in sends:
            cp.wait_send()
        for cp in mine:
            cp.wait()

    any_spec = pl.BlockSpec(memory_space=pl.ANY)
    return pl.pallas_call(
        body, name=name,
        out_shape=[_sds(s.shape, s.dtype) for s in stacks],
        in_specs=[any_spec] * n, out_specs=[any_spec] * n,
        scratch_shapes=[pltpu.SemaphoreType.DMA((n, 7)), pltpu.SemaphoreType.DMA((n, 7)),
                        pltpu.SemaphoreType.DMA((n,))],
    )(*stacks)


def _kv_proj(mem, w_kv):
    def body(mem_ref, w_ref, kv_ref, memb_ref):
        mb = mem_ref[...].astype(bf16)
        memb_ref[...] = mb
        kv_ref[...] = _dot(mb, w_ref[...]).astype(bf16)

    return pl.pallas_call(
        body, name="kv_proj",
        out_shape=[_sds((MEM_LEN, 2 * XATTN_W), bf16), _sds((MEM_LEN, D_MODEL), bf16)],
        compiler_params=_cparams(),
    )(mem, w_kv)


def _attention_probs(qb, kv_ref, h):
    kh = kv_ref[:, h * HEAD_DIM:(h + 1) * HEAD_DIM]
    s = _dot(qb[:, h * HEAD_DIM:(h + 1) * HEAD_DIM], kh, NT) * (HEAD_DIM ** -0.5)
    e = jnp.exp(s - jnp.max(s, axis=-1, keepdims=True))
    return e / jnp.sum(e, axis=-1, keepdims=True)


def _in_proj(x, w_in_t, b_gate, conv_w, w_co_t, kv, w_xo_t):
    s_len = x.shape[0]
    tm = TOKEN_TILE
    n = s_len // tm

    def body(x_ref, win_ref, bg_ref, cw_ref, wco_ref, kv_ref, wxo_ref,
             xb_ref, g_ref, cin_ref, u_ref, q_ref, ya_ref, yc_ref, ain_ref, o_ref, zs_ref):
        i = pl.program_id(0)
        xb = x_ref[...].astype(bf16)
        xb_ref[...] = xb
        proj = _dot(xb, win_ref[...], NT)
        g_ref[...] = jax.nn.sigmoid(proj[:, :GATE_COLS] + bg_ref[...])
        cin = proj[:, GATE_COLS:GATE_COLS + 3 * CONV_W]
        cin_ref[...] = cin
        u_ref[...] = proj[:, GATE_COLS + 3 * CONV_W:GATE_COLS + 3 * CONV_W + SSM_W]
        qb = proj[:, IN_COLS - XATTN_W:].astype(bf16)
        q_ref[...] = qb

        cb, cc, ch = cin[:, :CONV_W], cin[:, CONV_W:2 * CONV_W], cin[:, 2 * CONV_W:]
        z = cc * ch

        @pl.when(i == 0)
        def _():
            zs_ref[0:8, :] = jnp.zeros((8, CONV_W), f32)

        zs_ref[8:8 + tm, :] = z
        z1 = zs_ref[pl.ds(7, tm), :]
        z2 = zs_ref[pl.ds(6, tm), :]
        cw = cw_ref[...]
        cz = cw[0:1] * z2 + cw[1:2] * z1 + cw[2:3] * z
        zs_ref[0:8, :] = zs_ref[tm:tm + 8, :]
        ain = (cb * cz).astype(bf16)
        ain_ref[...] = ain
        ya_ref[...] = _dot(ain, wco_ref[...], NT)

        outs = []
        for h in range(HEADS):
            p = _attention_probs(qb, kv_ref, h)
            vh = kv_ref[:, XATTN_W + h * HEAD_DIM:XATTN_W + (h + 1) * HEAD_DIM]
            outs.append(_dot(p.astype(bf16), vh))
        ob = jnp.concatenate(outs, axis=1).astype(bf16)
        o_ref[...] = ob
        yc_ref[...] = _dot(ob, wxo_ref[...], NT)

    out_cols = [(D_MODEL, bf16), (GATE_COLS, f32), (3 * CONV_W, f32), (SSM_W, f32), (XATTN_W, bf16),
                (D_MODEL, f32), (D_MODEL, f32), (CONV_W, bf16), (XATTN_W, bf16)]
    return pl.pallas_call(
        body, name="in_proj", grid=(n,),
        in_specs=[_row_spec(tm, D_MODEL), _const_spec((IN_COLS, D_MODEL)), _const_spec((1, GATE_COLS)),
                  _const_spec((3, CONV_W)), _const_spec((D_MODEL, CONV_W)), _const_spec((MEM_LEN, 2 * XATTN_W)),
                  _const_spec((D_MODEL, XATTN_W))],
        out_specs=[_row_spec(tm, c) for c, _ in out_cols],
        out_shape=[_sds((s_len, c), dt) for c, dt in out_cols],
        scratch_shapes=[pltpu.VMEM((tm + 8, CONV_W), f32)],
        compiler_params=_cparams(("arbitrary",)),
    )(x, w_in_t, b_gate, conv_w, w_co_t, kv, w_xo_t)


def _ssm_local_scan(s_ref, ar_ref, ai_ref, reverse):
    for c in range(N_STATE // LANE_CHUNK):
        lo = c * LANE_CHUNK
        re = slice(lo, lo + LANE_CHUNK)
        im = slice(N_STATE + lo, N_STATE + lo + LANE_CHUNK)
        ar = jnp.broadcast_to(ar_ref[:, re], (SUBLANES, LANE_CHUNK))
        ai = jnp.broadcast_to(ai_ref[:, re], (SUBLANES, LANE_CHUNK))
        if reverse:
            ai = -ai

        def step(j, carry, re=re, im=im, ar=ar, ai=ai):
            sr, si = carry
            k = (SSM_SEG - 1 - j) if reverse else j
            r0 = pl.multiple_of(k * SUBLANES, SUBLANES)
            nr = ar * sr - ai * si + s_ref[pl.ds(r0, SUBLANES), re]
            ni = ar * si + ai * sr + s_ref[pl.ds(r0, SUBLANES), im]
            s_ref[pl.ds(r0, SUBLANES), re] = nr
            s_ref[pl.ds(r0, SUBLANES), im] = ni
            return nr, ni

        zero = jnp.zeros((SUBLANES, LANE_CHUNK), f32)
        lax.fori_loop(0, SSM_SEG, step, (zero, zero), unroll=4)


def _ssm_add_carry(s_ref, pw_ref, cm_ref, reverse):
    for c in range(N_STATE // LANE_CHUNK):
        lo = c * LANE_CHUNK
        re = slice(lo, lo + LANE_CHUNK)
        im = slice(N_STATE + lo, N_STATE + lo + LANE_CHUNK)
        cr, ci = cm_ref[:, re], cm_ref[:, im]
        for k in range(SSM_SEG):
            pk = (SSM_SEG - 1 - k) if reverse else k
            pr = pw_ref[pk:pk + 1, re]
            pi = pw_ref[pk:pk + 1, im]
            if reverse:
                pi = -pi
            rows = slice(k * SUBLANES, (k + 1) * SUBLANES)
            s_ref[rows, re] = s_ref[rows, re] + (pr * cr - pi * ci)
            s_ref[rows, im] = s_ref[rows, im] + (pr * ci + pi * cr)


def _ssm_fwd(u_perm, b_blk, c_blk, a_re, a_im, pw, d_skip):
    s_len = u_perm.shape[0]
    tb = SSM_BLOCK
    n = s_len // tb

    def body(u_ref, b_ref, c_ref, ar_ref, ai_ref, pw_ref, d_ref, y_ref, cm_ref, s_ref, carry_ref):
        i = pl.program_id(0)

        @pl.when(i == 0)
        def _():
            carry_ref[...] = jnp.zeros_like(carry_ref)

        u = u_ref[...]
        s_ref[...] = _dot(u.astype(bf16), b_ref[...])
        _ssm_local_scan(s_ref, ar_ref, ai_ref, reverse=False)

        a_r, a_i = pw_ref[SSM_SEG - 1:SSM_SEG, :N_STATE], pw_ref[SSM_SEG - 1:SSM_SEG, N_STATE:]
        cr, ci = carry_ref[0:1, :N_STATE], carry_ref[0:1, N_STATE:]
        last = tb - SUBLANES
        for seg in range(SUBLANES):
            cm_ref[seg:seg + 1, :N_STATE] = cr
            cm_ref[seg:seg + 1, N_STATE:] = ci
            er = s_ref[last + seg:last + seg + 1, :N_STATE]
            ei = s_ref[last + seg:last + seg + 1, N_STATE:]
            cr, ci = a_r * cr - a_i * ci + er, a_r * ci + a_i * cr + ei
        carry_ref[0:1, :N_STATE] = cr
        carry_ref[0:1, N_STATE:] = ci

        _ssm_add_carry(s_ref, pw_ref, cm_ref, reverse=False)
        y_ref[...] = _dot(s_ref[...].astype(bf16), c_ref[...]) + d_ref[...] * u

    return pl.pallas_call(
        body, name="ssm_fwd", grid=(n,),
        in_specs=[_row_spec(tb, SSM_W), _const_spec((SSM_W, 2 * N_STATE)), _const_spec((2 * N_STATE, SSM_W)),
                  _const_spec((1, N_STATE)), _const_spec((1, N_STATE)), _const_spec((SSM_SEG, 2 * N_STATE)),
                  _const_spec((1, SSM_W))],
        out_specs=[_row_spec(tb, SSM_W), _row_spec(SUBLANES, 2 * N_STATE)],
        out_shape=[_sds((s_len, SSM_W), f32), _sds((n * SUBLANES, 2 * N_STATE), f32)],
        scratch_shapes=[pltpu.VMEM((tb, 2 * N_STATE), f32), pltpu.VMEM((SUBLANES, 2 * N_STATE), f32)],
        compiler_params=_cparams(("arbitrary",)),
    )(u_perm, b_blk, c_blk, a_re, a_im, pw, d_skip)


def _layer_norm_fwd(r, g, b):
    mu = jnp.mean(r, axis=-1, keepdims=True)
    var = jnp.mean(jnp.square(r - mu), axis=-1, keepdims=True)
    rstd = lax.rsqrt(var + LN_EPS)
    xhat = (r - mu) * rstd
    return xhat, rstd, xhat * g + b


def _layer_norm_bwd(dy, xhat, rstd, g):
    dxh = dy * g
    m1 = jnp.mean(dxh, axis=-1, keepdims=True)
    m2 = jnp.mean(dxh * xhat, axis=-1, keepdims=True)
    return rstd * (dxh - m1 - xhat * m2)


def _mid_fwd(y_ssm, g, ya, yc, x, w_glu_t, w_out, ln1_g, ln1_b):
    s_len = x.shape[0]
    tm = TOKEN_TILE
    n = s_len // tm

    def body(ys_ref, g_ref, ya_ref, yc_ref, x_ref, wglu_ref, wout_ref, lg_ref, lb_ref,
             ysb_ref, glu_ref, mb_ref, xhat_ref, rstd_ref):
        ysb = _gelu(ys_ref[...]).astype(bf16)
        ysb_ref[...] = ysb
        glu = _dot(ysb, wglu_ref[...], NT)
        glu_ref[...] = glu
        yb = glu[:, :D_MODEL] * jax.nn.sigmoid(glu[:, D_MODEL:])
        gt = g_ref[...]
        merged = (gt[:, :D_MODEL] * ya_ref[...] + gt[:, D_MODEL:2 * D_MODEL] * yb
                  + gt[:, 2 * D_MODEL:] * yc_ref[...])
        mb = merged.astype(bf16)
        mb_ref[...] = mb
        r1 = ALPHA * x_ref[...] + _dot(mb, wout_ref[...])
        xhat, rstd, _ = _layer_norm_fwd(r1, lg_ref[...], lb_ref[...])
        xhat_ref[...] = xhat
        rstd_ref[...] = rstd

    out_cols = [(SSM_W, bf16), (2 * D_MODEL, f32), (D_MODEL, bf16), (D_MODEL, f32), (1, f32)]
    return pl.pallas_call(
        body, name="mid_fwd", grid=(n,),
        in_specs=[_row_spec(tm, SSM_W), _row_spec(tm, GATE_COLS), _row_spec(tm, D_MODEL), _row_spec(tm, D_MODEL),
                  _row_spec(tm, D_MODEL), _const_spec((2 * D_MODEL, SSM_W)), _const_spec((D_MODEL, D_MODEL)),
                  _const_spec((1, D_MODEL)), _const_spec((1, D_MODEL))],
        out_specs=[_row_spec(tm, c) for c, _ in out_cols],
        out_shape=[_sds((s_len, c), dt) for c, dt in out_cols],
        compiler_params=_cparams(("parallel",)),
    )(y_ssm, g, ya, yc, x, w_glu_t, w_out, ln1_g, ln1_b)


def _mlp_fwd_bwd(xhat1, tgt, ln1_g, ln1_b, w_up_t, b_up, w_down, b_down, ln2_g, ln2_b):
    s_len = xhat1.shape[0]
    tm = TOKEN_TILE
    n = s_len // tm
    fc = 1024
    nfc = D_FF // fc

    def body(xh_ref, t_ref, l1g_ref, l1b_ref, wup_ref, bup_ref, wdn_ref, bdn_ref, l2g_ref, l2b_ref,
             x1b_ref, hdn_ref, dr2b_ref, dpre_ref, dx1_ref,
             loss_ref, dl2g_ref, dl2b_ref, dbdn_ref, dbup_ref, rl_ref):
        i = pl.program_id(0)

        @pl.when(i == 0)
        def _():
            loss_ref[...] = jnp.zeros_like(loss_ref)
            dl2g_ref[...] = jnp.zeros_like(dl2g_ref)
            dl2b_ref[...] = jnp.zeros_like(dl2b_ref)
            dbdn_ref[...] = jnp.zeros_like(dbdn_ref)
            dbup_ref[...] = jnp.zeros_like(dbup_ref)

        x1 = xh_ref[...] * l1g_ref[...] + l1b_ref[...]
        x1b = x1.astype(bf16)
        x1b_ref[...] = x1b
        acc = jnp.zeros((tm, D_MODEL), f32)
        for c in range(nfc):
            cols = slice(c * fc, (c + 1) * fc)
            pre = _dot(x1b, wup_ref[cols, :], NT) + bup_ref[:, cols]
            rl = jnp.maximum(pre, 0.0)
            rl_ref[:, cols] = rl
            hb = (rl * rl).astype(bf16)
            hdn_ref[:, cols] = hb
            acc = acc + _dot(hb, wdn_ref[cols, :])
        r2 = ALPHA * x1 + acc + bdn_ref[...]
        xhat2, rstd2, y = _layer_norm_fwd(r2, l2g_ref[...], l2b_ref[...])
        err = y - t_ref[...]
        loss_ref[...] += jnp.sum(jnp.sum(err * err, axis=1, keepdims=True), axis=0, keepdims=True) * (0.5 / D_MODEL)
        dy = err * (1.0 / D_MODEL)
        dl2g_ref[...] += _colsum(dy * xhat2)
        dl2b_ref[...] += _colsum(dy)
        dr2 = _layer_norm_bwd(dy, xhat2, rstd2, l2g_ref[...])
        dbdn_ref[...] += _colsum(dr2)
        dr2b = dr2.astype(bf16)
        dr2b_ref[...] = dr2b
        dacc = jnp.zeros((tm, D_MODEL), f32)
        for c in range(nfc):
            cols = slice(c * fc, (c + 1) * fc)
            dh = _dot(dr2b, wdn_ref[cols, :], NT)
            dpre = dh * (2.0 * rl_ref[:, cols])
            dbup_ref[:, cols] += _colsum(dpre)
            dpb = dpre.astype(bf16)
            dpre_ref[:, cols] = dpb
            dacc = dacc + _dot(dpb, wup_ref[cols, :])
        dx1_ref[...] = ALPHA * dr2 + dacc

    row_cols = [(D_MODEL, bf16), (D_FF, bf16), (D_MODEL, bf16), (D_FF, bf16), (D_MODEL, f32)]
    acc_shapes = [(1, 128), (1, D_MODEL), (1, D_MODEL), (1, D_MODEL), (1, D_FF)]
    return pl.pallas_call(
        body, name="mlp_fwd_bwd", grid=(n,),
        in_specs=[_row_spec(tm, D_MODEL), _row_spec(tm, D_MODEL), _const_spec((1, D_MODEL)), _const_spec((1, D_MODEL)),
                  _const_spec((D_FF, D_MODEL)), _const_spec((1, D_FF)), _const_spec((D_FF, D_MODEL)),
                  _const_spec((1, D_MODEL)), _const_spec((1, D_MODEL)), _const_spec((1, D_MODEL))],
        out_specs=[_row_spec(tm, c) for c, _ in row_cols] + [_acc_spec(s) for s in acc_shapes],
        out_shape=[_sds((s_len, c), dt) for c, dt in row_cols] + [_sds(s, f32) for s in acc_shapes],
        scratch_shapes=[pltpu.VMEM((tm, D_FF), f32)],
        compiler_params=_cparams(("arbitrary",)),
    )(xhat1, tgt, ln1_g, ln1_b, w_up_t, b_up, w_down, b_down, ln2_g, ln2_b)


def _mid_bwd(dx1, xhat1, rstd1, g, ya, yc, glu, y_ssm, ln1_g, w_out, w_glu_t):
    s_len = dx1.shape[0]
    tm = TOKEN_TILE
    n = s_len // tm

    def body(dx1_ref, xh_ref, rs_ref, g_ref, ya_ref, yc_ref, glu_ref, ys_ref, lg_ref, wout_ref, wglu_ref,
             dxp_ref, dr1b_ref, dgp_ref, dya_ref, dyc_ref, dglu_ref, dyssm_ref,
             dl1g_ref, dl1b_ref, dbg_ref):
        i = pl.program_id(0)

        @pl.when(i == 0)
        def _():
            dl1g_ref[...] = jnp.zeros_like(dl1g_ref)
            dl1b_ref[...] = jnp.zeros_like(dl1b_ref)
            dbg_ref[...] = jnp.zeros_like(dbg_ref)

        dx1 = dx1_ref[...]
        xhat = xh_ref[...]
        dl1g_ref[...] += _colsum(dx1 * xhat)
        dl1b_ref[...] += _colsum(dx1)
        dr1 = _layer_norm_bwd(dx1, xhat, rs_ref[...], lg_ref[...])
        dxp_ref[...] = ALPHA * dr1
        dr1b = dr1.astype(bf16)
        dr1b_ref[...] = dr1b
        dm = _dot(dr1b, wout_ref[...], NT)

        glu = glu_ref[...]
        ga, sb = glu[:, :D_MODEL], jax.nn.sigmoid(glu[:, D_MODEL:])
        yb = ga * sb
        gt = g_ref[...]
        branch = (ya_ref[...], yb, yc_ref[...])
        for j in range(3):
            cols = slice(j * D_MODEL, (j + 1) * D_MODEL)
            gj = gt[:, cols]
            dgp = dm * branch[j] * gj * (1.0 - gj)
            dbg_ref[:, cols] += _colsum(dgp)
            dgp_ref[:, cols] = dgp.astype(bf16)
        dya_ref[...] = (dm * gt[:, :D_MODEL]).astype(bf16)
        dyc_ref[...] = (dm * gt[:, 2 * D_MODEL:]).astype(bf16)
        dyb = dm * gt[:, D_MODEL:2 * D_MODEL]
        dga = (dyb * sb).astype(bf16)
        dgb = (dyb * ga * sb * (1.0 - sb)).astype(bf16)
        dglu_ref[:, :D_MODEL] = dga
        dglu_ref[:, D_MODEL:] = dgb
        dys = _dot(dga, wglu_ref[:D_MODEL, :]) + _dot(dgb, wglu_ref[D_MODEL:, :])
        dyssm_ref[...] = dys * _gelu_grad(ys_ref[...])

    row_cols = [(D_MODEL, f32), (D_MODEL, bf16), (GATE_COLS, bf16), (D_MODEL, bf16), (D_MODEL, bf16),
                (2 * D_MODEL, bf16), (SSM_W, f32)]
    acc_shapes = [(1, D_MODEL), (1, D_MODEL), (1, GATE_COLS)]
    return pl.pallas_call(
        body, name="mid_bwd", grid=(n,),
        in_specs=[_row_spec(tm, D_MODEL), _row_spec(tm, D_MODEL), _row_spec(tm, 1), _row_spec(tm, GATE_COLS),
                  _row_spec(tm, D_MODEL), _row_spec(tm, D_MODEL), _row_spec(tm, 2 * D_MODEL), _row_spec(tm, SSM_W),
                  _const_spec((1, D_MODEL)), _const_spec((D_MODEL, D_MODEL)), _const_spec((2 * D_MODEL, SSM_W))],
        out_specs=[_row_spec(tm, c) for c, _ in row_cols] + [_acc_spec(s) for s in acc_shapes],
        out_shape=[_sds((s_len, c), dt) for c, dt in row_cols] + [_sds(s, f32) for s in acc_shapes],
        compiler_params=_cparams(("arbitrary",)),
    )(dx1, xhat1, rstd1, g, ya, yc, glu, y_ssm, ln1_g, w_out, w_glu_t)


def _ssm_bwd(u_perm, dy_perm, cm_all, b_blk, c_blk, a_re, a_im, pw, d_skip):
    s_len = u_perm.shape[0]
    tb = SSM_BLOCK
    n = s_len // tb

    def body(u_ref, dy_ref, cm_ref, b_ref, c_ref, ar_ref, ai_ref, pw_ref, d_ref,
             du_ref, db_hbm, dc_hbm, da_ref, dd_ref, s_ref, g_ref, gcarry_ref, gcm_ref, db_ref, dc_ref):
        i = pl.program_id(0)

        @pl.when(i == 0)
        def _():
            gcarry_ref[...] = jnp.zeros_like(gcarry_ref)
            db_ref[...] = jnp.zeros_like(db_ref)
            dc_ref[...] = jnp.zeros_like(dc_ref)
            da_ref[...] = jnp.zeros_like(da_ref)
            dd_ref[...] = jnp.zeros_like(dd_ref)

        u = u_ref[...]
        ub = u.astype(bf16)
        dy = dy_ref[...]
        dyb = dy.astype(bf16)
        dd_ref[...] += _colsum(dy * u)

        s_ref[...] = _dot(ub, b_ref[...])
        _ssm_local_scan(s_ref, ar_ref, ai_ref, reverse=False)
        _ssm_add_carry(s_ref, pw_ref, cm_ref, reverse=False)

        g_ref[...] = _dot(dyb, c_ref[...], NT)
        _ssm_local_scan(g_ref, ar_ref, ai_ref, reverse=True)
        a_r, a_i = pw_ref[SSM_SEG - 1:SSM_SEG, :N_STATE], -pw_ref[SSM_SEG - 1:SSM_SEG, N_STATE:]
        cr, ci = gcarry_ref[0:1, :N_STATE], gcarry_ref[0:1, N_STATE:]
        for seg in range(SUBLANES - 1, -1, -1):
            gcm_ref[seg:seg + 1, :N_STATE] = cr
            gcm_ref[seg:seg + 1, N_STATE:] = ci
            er = g_ref[seg:seg + 1, :N_STATE]
            ei = g_ref[seg:seg + 1, N_STATE:]
            cr, ci = a_r * cr - a_i * ci + er, a_r * ci + a_i * cr + ei
        gcarry_ref[0:1, :N_STATE] = cr
        gcarry_ref[0:1, N_STATE:] = ci
        _ssm_add_carry(g_ref, pw_ref, gcm_ref, reverse=True)

        gb = g_ref[...].astype(bf16)
        du_ref[...] = _dot(gb, b_ref[...], NT) + d_ref[...] * dy
        db_ref[...] += _dot(ub, gb, TN)
        dc_ref[...] += _dot(s_ref[...].astype(bf16), dyb, TN)

        for c in range(N_STATE // LANE_CHUNK):
            lo = c * LANE_CHUNK
            re = slice(lo, lo + LANE_CHUNK)
            im = slice(N_STATE + lo, N_STATE + lo + LANE_CHUNK)
            acc_r = da_ref[:, re]
            acc_i = da_ref[:, im]
            for k in range(SSM_SEG):
                rows = slice(k * SUBLANES, (k + 1) * SUBLANES)
                if k == 0:
                    pr, pi = cm_ref[:, re], cm_ref[:, im]
                else:
                    prev = slice((k - 1) * SUBLANES, k * SUBLANES)
                    pr, pi = s_ref[prev, re], s_ref[prev, im]
                gr, gi = g_ref[rows, re], g_ref[rows, im]
                acc_r = acc_r + (gr * pr + gi * pi)
                acc_i = acc_i + (gi * pr - gr * pi)
            da_ref[:, re] = acc_r
            da_ref[:, im] = acc_i

        @pl.when(i == n - 1)
        def _():
            pltpu.sync_copy(db_ref, db_hbm)
            pltpu.sync_copy(dc_ref, dc_hbm)

    rev = functools.partial(_row_spec, rev_n=n)
    any_spec = pl.BlockSpec(memory_space=pl.ANY)
    return pl.pallas_call(
        body, name="ssm_bwd", grid=(n,),
        in_specs=[rev(tb, SSM_W), rev(tb, SSM_W), rev(SUBLANES, 2 * N_STATE),
                  _const_spec((SSM_W, 2 * N_STATE)), _const_spec((2 * N_STATE, SSM_W)),
                  _const_spec((1, N_STATE)), _const_spec((1, N_STATE)), _const_spec((SSM_SEG, 2 * N_STATE)),
                  _const_spec((1, SSM_W))],
        out_specs=[rev(tb, SSM_W), any_spec, any_spec,
                   _acc_spec((SUBLANES, 2 * N_STATE)), _acc_spec((1, SSM_W))],
        out_shape=[_sds((s_len, SSM_W), f32), _sds((SSM_W, 2 * N_STATE), f32), _sds((2 * N_STATE, SSM_W), f32),
                   _sds((SUBLANES, 2 * N_STATE), f32), _sds((1, SSM_W), f32)],
        scratch_shapes=[pltpu.VMEM((tb, 2 * N_STATE), f32), pltpu.VMEM((tb, 2 * N_STATE), f32),
                        pltpu.VMEM((SUBLANES, 2 * N_STATE), f32), pltpu.VMEM((SUBLANES, 2 * N_STATE), f32),
                        pltpu.VMEM((SSM_W, 2 * N_STATE), f32), pltpu.VMEM((2 * N_STATE, SSM_W), f32)],
        compiler_params=_cparams(("arbitrary",)),
    )(u_perm, dy_perm, cm_all, b_blk, c_blk, a_re, a_im, pw, d_skip)


def _branch_bwd(dya, dyc, cin, q, kv, conv_w, w_co_t, w_xo_t):
    s_len = dya.shape[0]
    tm = TOKEN_TILE
    n = s_len // tm
    halo_blocks = tm // 8

    def body(dya_ref, dyc_ref, cin_ref, cprev_ref, q_ref, kv_ref, cw_ref, wco_ref, wxo_ref,
             dconv_ref, dq_ref, dcw_ref, dkv_ref, zs_ref, dczs_ref):
        i = pl.program_id(0)
        tile = n - 1 - i

        @pl.when(i == 0)
        def _():
            dcw_ref[...] = jnp.zeros_like(dcw_ref)
            dkv_ref[...] = jnp.zeros_like(dkv_ref)
            dczs_ref[tm:tm + 8, :] = jnp.zeros((8, CONV_W), f32)

        cin = cin_ref[...]
        cb, cc, ch = cin[:, :CONV_W], cin[:, CONV_W:2 * CONV_W], cin[:, 2 * CONV_W:]
        z = cc * ch
        cprev = cprev_ref[...]
        zprev = cprev[:, CONV_W:2 * CONV_W] * cprev[:, 2 * CONV_W:]
        zs_ref[0:8, :] = jnp.where(tile == 0, 0.0, zprev)
        zs_ref[8:8 + tm, :] = z
        z1 = zs_ref[pl.ds(7, tm), :]
        z2 = zs_ref[pl.ds(6, tm), :]
        cw = cw_ref[...]
        cz = cw[0:1] * z2 + cw[1:2] * z1 + cw[2:3] * z

        dain = _dot(dya_ref[...], wco_ref[...])
        dcb = dain * cz
        dcz = dain * cb
        dczs_ref[0:tm, :] = dcz
        dcz1 = dczs_ref[pl.ds(1, tm), :]
        dcz2 = dczs_ref[pl.ds(2, tm), :]
        dz = cw[2:3] * dcz + cw[1:2] * dcz1 + cw[0:1] * dcz2
        dczs_ref[tm:tm + 8, :] = dczs_ref[0:8, :]
        dcw_ref[0:1, :] += _colsum(dcz * z2)
        dcw_ref[1:2, :] += _colsum(dcz * z1)
        dcw_ref[2:3, :] += _colsum(dcz * z)
        dconv_ref[:, :CONV_W] = dcb.astype(bf16)
        dconv_ref[:, CONV_W:2 * CONV_W] = (dz * ch).astype(bf16)
        dconv_ref[:, 2 * CONV_W:] = (dz * cc).astype(bf16)

        qb = q_ref[...]
        do = _dot(dyc_ref[...], wxo_ref[...])
        for h in range(HEADS):
            hc = slice(h * HEAD_DIM, (h + 1) * HEAD_DIM)
            vc = slice(XATTN_W + h * HEAD_DIM, XATTN_W + (h + 1) * HEAD_DIM)
            p = _attention_probs(qb, kv_ref, h)
            dob = do[:, hc].astype(bf16)
            dp = _dot(dob, kv_ref[:, vc], NT)
            dkv_ref[:, vc] += _dot(p.astype(bf16), dob, TN)
            ds = p * (dp - jnp.sum(dp * p, axis=-1, keepdims=True)) * (HEAD_DIM ** -0.5)
            dsb = ds.astype(bf16)
            dq_ref[:, hc] = _dot(dsb, kv_ref[:, hc]).astype(bf16)
            dkv_ref[:, hc] += _dot(dsb, qb[:, hc], TN)

    rev = functools.partial(_row_spec, rev_n=n)
    prev_spec = pl.BlockSpec((8, 3 * CONV_W), lambda i: (jnp.maximum((n - 1 - i) * halo_blocks - 1, 0), 0))
    return pl.pallas_call(
        body, name="branch_bwd", grid=(n,),
        in_specs=[rev(tm, D_MODEL), rev(tm, D_MODEL), rev(tm, 3 * CONV_W), prev_spec, rev(tm, XATTN_W),
                  _const_spec((MEM_LEN, 2 * XATTN_W)), _const_spec((3, CONV_W)), _const_spec((D_MODEL, CONV_W)),
                  _const_spec((D_MODEL, XATTN_W))],
        out_specs=[rev(tm, 3 * CONV_W), rev(tm, XATTN_W), _acc_spec((8, CONV_W)), _acc_spec((MEM_LEN, 2 * XATTN_W))],
        out_shape=[_sds((s_len, 3 * CONV_W), bf16), _sds((s_len, XATTN_W), bf16), _sds((8, CONV_W), f32),
                   _sds((MEM_LEN, 2 * XATTN_W), f32)],
        scratch_shapes=[pltpu.VMEM((tm + 8, CONV_W), f32), pltpu.VMEM((tm + 8, CONV_W), f32)],
        compiler_params=_cparams(("arbitrary",)),
    )(dya, dyc, cin, cin, q, kv, conv_w, w_co_t, w_xo_t)


def _in_proj_bwd(dgp, dconv, du, dq, dxp, w_in_t):
    s_len = dgp.shape[0]
    tm = TOKEN_TILE
    n = s_len // tm

    def body(dgp_ref, dconv_ref, du_ref, dq_ref, dxp_ref, win_ref, dx_ref, dproj_ref):
        dproj = jnp.concatenate([dgp_ref[...], dconv_ref[...], du_ref[...].astype(bf16), dq_ref[...]], axis=1)
        dproj_ref[...] = dproj
        dx_ref[...] = dxp_ref[...] + _dot(dproj, win_ref[...])

    return pl.pallas_call(
        body, name="in_proj_bwd", grid=(n,),
        in_specs=[_row_spec(tm, GATE_COLS), _row_spec(tm, 3 * CONV_W), _row_spec(tm, SSM_W), _row_spec(tm, XATTN_W),
                  _row_spec(tm, D_MODEL), _const_spec((IN_COLS, D_MODEL))],
        out_specs=[_row_spec(tm, D_MODEL), _row_spec(tm, IN_COLS)],
        out_shape=[_sds((s_len, D_MODEL), f32), _sds((s_len, IN_COLS), bf16)],
        compiler_params=_cparams(("parallel",)),
    )(dgp, dconv, du, dq, dxp, w_in_t)


def _weight_grad(a, b, name, col_shards=None, tm=512, tt=512):
    s_len, m = a.shape
    n_cols = b.shape[1]
    tm = min(tm, m)
    nm, nt = m // tm, s_len // tt

    def body(a_ref, b_ref, o_ref, acc_ref):
        t = pl.program_id(1)

        @pl.when(t == 0)
        def _():
            acc_ref[...] = jnp.zeros_like(acc_ref)

        acc_ref[...] += _dot(a_ref[...], b_ref[...], TN)

        @pl.when(t == nt - 1)
        def _():
            if col_shards is None:
                o_ref[...] = acc_ref[...].astype(bf16)
            else:
                w = n_cols // col_shards
                for j in range(col_shards):
                    o_ref[j] = acc_ref[:, j * w:(j + 1) * w].astype(bf16)

    if col_shards is None:
        out_spec = pl.BlockSpec((tm, n_cols), lambda i, t: (i, 0))
        out_shape = _sds((m, n_cols), bf16)
    else:
        w = n_cols // col_shards
        out_spec = pl.BlockSpec((col_shards, tm, w), lambda i, t: (0, i, 0))
        out_shape = _sds((col_shards, m, w), bf16)
    return pl.pallas_call(
        body, name=name, grid=(nm, nt),
        in_specs=[pl.BlockSpec((tt, tm), lambda i, t: (t, i)), pl.BlockSpec((tt, n_cols), lambda i, t: (t, 0))],
        out_specs=out_spec, out_shape=out_shape,
        scratch_shapes=[pltpu.VMEM((tm, n_cols), f32)],
        compiler_params=_cparams(("parallel", "arbitrary")),
    )(a, b)


def _adamw(w, g, m, v):
    m = ADAM_B1 * m + (1.0 - ADAM_B1) * g
    v = ADAM_B2 * v + (1.0 - ADAM_B2) * jnp.square(g)
    m_hat = m / (1.0 - ADAM_B1 ** ADAM_STEP)
    v_hat = v / (1.0 - ADAM_B2 ** ADAM_STEP)
    delta = -ADAM_LR * (m_hat / (jnp.sqrt(v_hat) + ADAM_EPS) + ADAM_WD * w)
    return delta, m, v


def _row_tile(rows):
    for t in (256, 128, 64, 32, 16, 8):
        if rows % t == 0:
            return t
    return rows


def _sum_parts(parts, name):
    _, rows, cols = parts.shape
    tr = _row_tile(rows)

    def body(p_ref, g_ref):
        g = p_ref[0].astype(f32)
        for j in range(1, N_DEV):
            g = g + p_ref[j].astype(f32)
        g_ref[...] = g

    return pl.pallas_call(
        body, name=name, grid=(rows // tr,),
        in_specs=[pl.BlockSpec((N_DEV, tr, cols), lambda i: (0, i, 0))],
        out_specs=pl.BlockSpec((tr, cols), lambda i: (i, 0)),
        out_shape=_sds((rows, cols), f32),
        compiler_params=_cparams(("parallel",)),
    )(parts)


def _adamw_update(w, g, m, v, name, parts=None):
    rows, cols = w.shape
    tr = _row_tile(rows)
    spec = pl.BlockSpec((tr, cols), lambda i: (i, 0))

    def body(*refs):
        if parts is None:
            w_ref, g_in_ref, m_ref, v_ref, g_ref, d_ref, nm_ref, nv_ref = refs
            g = g_in_ref[...]
        else:
            w_ref, p_ref, m_ref, v_ref, g_ref, d_ref, nm_ref, nv_ref = refs
            g = p_ref[0].astype(f32)
            for j in range(1, N_DEV):
                g = g + p_ref[j].astype(f32)
        g_ref[...] = g
        d_ref[...], nm_ref[...], nv_ref[...] = _adamw(w_ref[...], g, m_ref[...], v_ref[...])

    g_spec = spec if parts is None else pl.BlockSpec((N_DEV, tr, cols), lambda i: (0, i, 0))
    return pl.pallas_call(
        body, name=name, grid=(rows // tr,),
        in_specs=[spec, g_spec, spec, spec], out_specs=[spec] * 4,
        out_shape=[_sds((rows, cols), f32)] * 4,
        compiler_params=_cparams(("parallel",)),
    )(w, g if parts is None else parts, m, v)


def _ssm_discretize(lam_re, lam_im, log_dt, b_re, b_im):
    dt = jnp.exp(log_dt)[:, None]
    mag = jnp.exp(lam_re * dt)
    abar_r = mag * jnp.cos(lam_im * dt)
    abar_i = mag * jnp.sin(lam_im * dt)
    den = lam_re * lam_re + lam_im * lam_im
    nr = abar_r - 1.0
    ni = abar_i
    kr = (nr * lam_re + ni * lam_im) / den
    ki = (ni * lam_re - nr * lam_im) / den
    bbar_r = kr[..., None] * b_re - ki[..., None] * b_im
    bbar_i = kr[..., None] * b_im + ki[..., None] * b_re
    return abar_r, abar_i, bbar_r, bbar_i


def _abar_powers(abar_r, abar_i):
    def step(carry, _):
        pr, pi = carry
        nr, ni = pr * abar_r - pi * abar_i, pr * abar_i + pi * abar_r
        return (nr, ni), (pr, pi)

    _, (prs, pis) = lax.scan(step, (abar_r, abar_i), None, length=SSM_SEG)
    return jnp.concatenate([prs.reshape(SSM_SEG, N_STATE), pis.reshape(SSM_SEG, N_STATE)], axis=1)


def _block_diag_b(bbar):
    eye = jnp.eye(SSM_GROUPS, dtype=bbar.dtype)
    return jnp.einsum("gph,gk->ghkp", bbar, eye).reshape(SSM_W, N_STATE)


def _block_diag_c(c):
    eye = jnp.eye(SSM_GROUPS, dtype=c.dtype)
    return jnp.einsum("ghp,gk->gpkh", c, eye).reshape(N_STATE, SSM_W)


def _diag_blocks(mat, rows_per, cols_per):
    m4 = mat.reshape(SSM_GROUPS, rows_per, SSM_GROUPS, cols_per)
    idx = jnp.arange(SSM_GROUPS)
    return m4[idx, :, idx, :]


def _to_segments(a):
    s_len, cols = a.shape
    return a.reshape(s_len // SSM_BLOCK, SUBLANES, SSM_SEG, cols).transpose(0, 2, 1, 3).reshape(s_len, cols)


def _from_segments(a):
    s_len, cols = a.shape
    return a.reshape(s_len // SSM_BLOCK, SSM_SEG, SUBLANES, cols).transpose(0, 2, 1, 3).reshape(s_len, cols)


def _local_step(x, mem, tgt, wts, small):
    (w_in_t, w_co_t, w_glu_t, w_xo_t, w_up_t, w_kv, w_out, w_down) = wts
    lam_re, lam_im, log_dt = small["ssm_lam_re"], small["ssm_lam_im"], small["ssm_log_dt"]

    disc, disc_vjp = jax.vjp(_ssm_discretize, lam_re, lam_im, log_dt, small["ssm_b_re"], small["ssm_b_im"])
    abar_r, abar_i, bbar_r, bbar_i = disc
    pw = _abar_powers(abar_r, abar_i)
    a_re, a_im = abar_r.reshape(1, N_STATE), abar_i.reshape(1, N_STATE)
    b_blk = jnp.concatenate([_block_diag_b(bbar_r), _block_diag_b(bbar_i)], axis=1).astype(bf16)
    c_blk = jnp.concatenate([_block_diag_c(small["ssm_c_re"]), -_block_diag_c(small["ssm_c_im"])], axis=0).astype(bf16)
    d_skip = small["ssm_d"].reshape(1, SSM_W)

    kv, memb = _kv_proj(mem, w_kv)
    xb, g, cin, u, q, ya, yc, ain, ob = _in_proj(
        x, w_in_t, small["b_gate"].reshape(1, GATE_COLS), small["conv_w_full"], w_co_t, kv, w_xo_t)
    u_perm = _to_segments(u)
    y_perm, cm_all = _ssm_fwd(u_perm, b_blk, c_blk, a_re, a_im, pw, d_skip)
    y_ssm = _from_segments(y_perm)
    ln1_g, ln1_b = small["ln1_g"].reshape(1, D_MODEL), small["ln1_b"].reshape(1, D_MODEL)
    ysb, glu, mb, xhat1, rstd1 = _mid_fwd(y_ssm, g, ya, yc, x, w_glu_t, w_out, ln1_g, ln1_b)
    (x1b, hdn, dr2b, dpre, dx1, loss, dl2g, dl2b, dbdn, dbup) = _mlp_fwd_bwd(
        xhat1, tgt, ln1_g, ln1_b, w_up_t, small["b_up"].reshape(1, D_FF), w_down,
        small["b_down"].reshape(1, D_MODEL), small["ln2_g"].reshape(1, D_MODEL), small["ln2_b"].reshape(1, D_MODEL))
    (dxp, dr1b, dgp, dya, dyc, dglu, dyssm, dl1g, dl1b, dbg) = _mid_bwd(
        dx1, xhat1, rstd1, g, ya, yc, glu, y_ssm, ln1_g, w_out, w_glu_t)
    du_perm, db_blk, dc_blk, da8, dd = _ssm_bwd(
        u_perm, _to_segments(dyssm), cm_all, b_blk, c_blk, a_re, a_im, pw, d_skip)
    du = _from_segments(du_perm)
    dconv, dq, dcw8, dkv = _branch_bwd(dya, dyc, cin, q, kv, small["conv_w_full"], w_co_t, w_xo_t)
    dx, dproj = _in_proj_bwd(dgp, dconv, du, dq, dxp, w_in_t)

    send = {
        "w_in": _weight_grad(dproj, xb, "dw_in").reshape(N_DEV, IN_COLS // N_DEV, D_MODEL),
        "w_conv_out": _weight_grad(ain, dya, "dw_conv_out", col_shards=N_DEV),
        "w_glu": _weight_grad(ysb, dglu, "dw_glu", col_shards=N_DEV),
        "w_xattn_out": _weight_grad(ob, dyc, "dw_xattn_out", col_shards=N_DEV),
        "w_up": _weight_grad(x1b, dpre, "dw_up", col_shards=N_DEV),
        "w_kv": _weight_grad(memb, dkv.astype(bf16), "dw_kv", tt=MEM_LEN).reshape(N_DEV, D_MODEL // N_DEV, D_MODEL),
        "w_out": _weight_grad(mb, dr1b, "dw_out").reshape(N_DEV, D_MODEL // N_DEV, D_MODEL),
        "w_down": _weight_grad(hdn, dr2b, "dw_down").reshape(N_DEV, D_FF // N_DEV, D_MODEL),
    }

    da = jnp.sum(da8, axis=0)
    dabar_r = da[:N_STATE].reshape(SSM_GROUPS, SSM_STATE)
    dabar_i = da[N_STATE:].reshape(SSM_GROUPS, SSM_STATE)
    dbbar_r = _diag_blocks(db_blk[:, :N_STATE], SSM_GROUP, SSM_STATE).transpose(0, 2, 1)
    dbbar_i = _diag_blocks(db_blk[:, N_STATE:], SSM_GROUP, SSM_STATE).transpose(0, 2, 1)
    dlam_re, dlam_im, dlog_dt, db_re, db_im = disc_vjp((dabar_r, dabar_i, dbbar_r, dbbar_i))
    dc_re = _diag_blocks(dc_blk[:N_STATE], SSM_STATE, SSM_GROUP).transpose(0, 2, 1)
    dc_im = -_diag_blocks(dc_blk[N_STATE:], SSM_STATE, SSM_GROUP).transpose(0, 2, 1)

    small_grads = {
        "b_gate": dbg, "conv_w_full": dcw8[0:3], "ssm_lam_re": dlam_re, "ssm_lam_im": dlam_im, "ssm_log_dt": dlog_dt,
        "ssm_b_re": db_re, "ssm_b_im": db_im, "ssm_c_re": dc_re, "ssm_c_im": dc_im, "ssm_d": dd,
        "ln1_g": dl1g, "ln1_b": dl1b, "b_up": dbup, "b_down": dbdn, "ln2_g": dl2g, "ln2_b": dl2b,
    }
    return loss[0, 0], dx, send, small_grads


BIG = ["w_in", "w_conv_out", "w_glu", "w_kv", "w_xattn_out", "w_out", "w_up", "w_down"]
COL_SHARDED = ["w_in", "w_conv_out", "w_glu", "w_xattn_out", "w_up"]
SMALL = ["b_gate", "conv_w_full", "ssm_lam_re", "ssm_lam_im", "ssm_log_dt", "ssm_b_re", "ssm_b_im", "ssm_c_re",
         "ssm_c_im", "ssm_d", "ln1_g", "ln1_b", "b_up", "b_down", "ln2_g", "ln2_b"]
WEIGHTS = ["w_in", "b_gate", "conv_w", "w_conv_out", "ssm_lam_re", "ssm_lam_im", "ssm_log_dt", "ssm_b_re", "ssm_b_im",
           "ssm_c_re", "ssm_c_im", "ssm_d", "w_glu", "w_kv", "w_xattn_out", "w_out", "ln1_g", "ln1_b", "w_up", "b_up",
           "w_down", "b_down", "ln2_g", "ln2_b"]
PACK_LANES = 128


def _pack(arrays):
    flat = jnp.concatenate([a.reshape(-1).astype(f32) for a in arrays])
    pad = (-flat.shape[0]) % (8 * PACK_LANES)
    return jnp.pad(flat, (0, pad)).reshape(-1, PACK_LANES)


def _unpack(packed, shapes):
    flat = packed.reshape(-1)
    out, off = [], 0
    for shp in shapes:
        size = math.prod(shp)
        out.append(flat[off:off + size].reshape(shp))
        off += size
    return out


def kernel(x, mem, w_in, b_gate, conv_w, w_conv_out, ssm_lam_re, ssm_lam_im, ssm_log_dt, ssm_b_re, ssm_b_im, ssm_c_re, ssm_c_im, ssm_d, w_glu, w_kv, w_xattn_out, w_out, ln1_g, ln1_b, w_up, b_up, w_down, b_down, ln2_g, ln2_b, loss_target, m_w_in, m_b_gate, m_conv_w, m_w_conv_out, m_ssm_lam_re, m_ssm_lam_im, m_ssm_log_dt, m_ssm_b_re, m_ssm_b_im, m_ssm_c_re, m_ssm_c_im, m_ssm_d, m_w_glu, m_w_kv, m_w_xattn_out, m_w_out, m_ln1_g, m_ln1_b, m_w_up, m_b_up, m_w_down, m_b_down, m_ln2_g, m_ln2_b, v_w_in, v_b_gate, v_conv_w, v_w_conv_out, v_ssm_lam_re, v_ssm_lam_im, v_ssm_log_dt, v_ssm_b_re, v_ssm_b_im, v_ssm_c_re, v_ssm_c_im, v_ssm_d, v_w_glu, v_w_kv, v_w_xattn_out, v_w_out, v_ln1_g, v_ln1_b, v_w_up, v_b_up, v_w_down, v_b_down, v_ln2_g, v_ln2_b):
    w = dict(w_in=w_in, b_gate=b_gate, conv_w=conv_w, w_conv_out=w_conv_out, ssm_lam_re=ssm_lam_re,
             ssm_lam_im=ssm_lam_im, ssm_log_dt=ssm_log_dt, ssm_b_re=ssm_b_re, ssm_b_im=ssm_b_im, ssm_c_re=ssm_c_re,
             ssm_c_im=ssm_c_im, ssm_d=ssm_d, w_glu=w_glu, w_kv=w_kv, w_xattn_out=w_xattn_out, w_out=w_out,
             ln1_g=ln1_g, ln1_b=ln1_b, w_up=w_up, b_up=b_up, w_down=w_down, b_down=b_down, ln2_g=ln2_g, ln2_b=ln2_b)
    m = dict(w_in=m_w_in, b_gate=m_b_gate, conv_w=m_conv_w, w_conv_out=m_w_conv_out, ssm_lam_re=m_ssm_lam_re,
             ssm_lam_im=m_ssm_lam_im, ssm_log_dt=m_ssm_log_dt, ssm_b_re=m_ssm_b_re, ssm_b_im=m_ssm_b_im,
             ssm_c_re=m_ssm_c_re, ssm_c_im=m_ssm_c_im, ssm_d=m_ssm_d, w_glu=m_w_glu, w_kv=m_w_kv,
             w_xattn_out=m_w_xattn_out, w_out=m_w_out, ln1_g=m_ln1_g, ln1_b=m_ln1_b, w_up=m_w_up, b_up=m_b_up,
             w_down=m_w_down, b_down=m_b_down, ln2_g=m_ln2_g, ln2_b=m_ln2_b)
    v = dict(w_in=v_w_in, b_gate=v_b_gate, conv_w=v_conv_w, w_conv_out=v_w_conv_out, ssm_lam_re=v_ssm_lam_re,
             ssm_lam_im=v_ssm_lam_im, ssm_log_dt=v_ssm_log_dt, ssm_b_re=v_ssm_b_re, ssm_b_im=v_ssm_b_im,
             ssm_c_re=v_ssm_c_re, ssm_c_im=v_ssm_c_im, ssm_d=v_ssm_d, w_glu=v_w_glu, w_kv=v_w_kv,
             w_xattn_out=v_w_xattn_out, w_out=v_w_out, ln1_g=v_ln1_g, ln1_b=v_ln1_b, w_up=v_w_up, b_up=v_b_up,
             w_down=v_w_down, b_down=v_b_down, ln2_g=v_ln2_g, ln2_b=v_ln2_b)
    w, m, v = ({k: a[0] for k, a in d.items()} for d in (w, m, v))

    shards = [w[k].T.astype(bf16) if k in COL_SHARDED else w[k].astype(bf16) for k in BIG]
    conv_pad = jnp.pad(w["conv_w"], ((0, 5), (0, PACK_LANES - CONV_W // N_DEV)))
    gathered = _all_gather(shards + [conv_pad], "gather_weights")
    full = {k: a.reshape(-1, a.shape[-1]) for k, a in zip(BIG, gathered[:-1])}
    conv_full = gathered[-1][:, :3, :CONV_W // N_DEV].transpose(1, 0, 2).reshape(3, CONV_W)
    wts = (full["w_in"], full["w_conv_out"], full["w_glu"], full["w_xattn_out"], full["w_up"],
           full["w_kv"], full["w_out"], full["w_down"])
    small = {k: w[k] for k in SMALL if k != "conv_w_full"}
    small["conv_w_full"] = conv_full

    loss, dx, send, small_grads = _local_step(x[0], mem[0], loss_target[0], wts, small)

    recv = dict(zip(BIG, _all_to_all([send[k] for k in BIG], "scatter_grads")))
    small_shapes = [small[k].shape for k in SMALL]
    small_parts = _all_gather([_pack([small_grads[k].reshape(small[k].shape) for k in SMALL])], "gather_small_grads")[0]

    grads, deltas, new_m, new_v = {}, {}, {}, {}
    for k in BIG:
        if k == "w_in":
            g_t = _sum_parts(recv[k], "sum_w_in")
            res = _adamw_update(w[k], g_t.T, m[k], v[k], "adamw_" + k)
        else:
            res = _adamw_update(w[k], None, m[k], v[k], "adamw_" + k, parts=recv[k])
        grads[k], deltas[k], new_m[k], new_v[k] = res

    dev = _slot(_mesh_place())
    my_conv = lambda a: lax.dynamic_slice_in_dim(a, dev * (CONV_W // N_DEV), CONV_W // N_DEV, axis=1)
    sw, sm, sv = {}, {}, {}
    for k in SMALL:
        if k == "conv_w_full":
            full_like = lambda a: jnp.tile(a, (1, N_DEV))
            sw[k], sm[k], sv[k] = conv_full, full_like(m["conv_w"]), full_like(v["conv_w"])
        else:
            sw[k], sm[k], sv[k] = w[k], m[k], v[k]
    res = _adamw_update(_pack([sw[k] for k in SMALL]), None, _pack([sm[k] for k in SMALL]),
                        _pack([sv[k] for k in SMALL]), "adamw_small",
                        parts=small_parts)
    for name, packed in zip((grads, deltas, new_m, new_v), res):
        for k, a in zip(SMALL, _unpack(packed, small_shapes)):
            name["conv_w" if k == "conv_w_full" else k] = my_conv(a) if k == "conv_w_full" else a

    loss = lax.psum(loss, ("x", "y", "c"))
    outs = [loss, dx[None]]
    for d in (grads, deltas, new_m, new_v):
        outs += [d[k][None] for k in WEIGHTS]
    return tuple(outs)
```

```python
import functools
import math

import jax
import jax.numpy as jnp
from jax import lax
from jax.experimental import pallas as pl
from jax.experimental.pallas import tpu as pltpu

f32 = jnp.float32
bf16 = jnp.bfloat16

D_MODEL = 1024
MEM_LEN = 256
GATE_COLS = 3 * D_MODEL
CONV_W = 512
SSM_W = 512
XATTN_W = 512
HEADS = 4
HEAD_DIM = 128
D_FF = 4096
IN_COLS = GATE_COLS + 3 * CONV_W + SSM_W + XATTN_W
SSM_GROUPS = 32
SSM_GROUP = 16
SSM_STATE = 64
N_STATE = SSM_GROUPS * SSM_STATE
ALPHA = 2.0 ** 0.25
LN_EPS = 1e-5
N_DEV = 8

ADAM_LR = 0.001
ADAM_B1 = 0.9
ADAM_B2 = 0.999
ADAM_EPS = 1e-08
ADAM_WD = 0.01
ADAM_STEP = 10

VMEM_LIMIT_V7X = 56 * 2 ** 20
SUBLANES = 8
LANES = 128

TOKEN_TILE = 256
SSM_BLOCK = 256
SSM_SEG = SSM_BLOCK // SUBLANES
LANE_CHUNK = 512
N_HALF = 2
HALF_W = SSM_W // N_HALF
HALF_STATE = N_STATE // N_HALF
HALF_COLS = 2 * HALF_STATE

NT = (((1,), (1,)), ((), ()))
TN = (((0,), (0,)), ((), ()))
NN = (((1,), (0,)), ((), ()))


def _dot(a, b, dims=NN):
    return lax.dot_general(a, b, dims, preferred_element_type=f32)


def _cparams(sem=None):
    return pltpu.CompilerParams(dimension_semantics=sem, vmem_limit_bytes=VMEM_LIMIT_V7X)


def _row_spec(tm, cols, rev_n=None):
    if rev_n is None:
        return pl.BlockSpec((tm, cols), lambda i: (i, 0))
    return pl.BlockSpec((tm, cols), lambda i: (rev_n - 1 - i, 0))


def _col_spec(rows, tm):
    return pl.BlockSpec((rows, tm), lambda i: (0, i))


def _const_spec(shape):
    nd = len(shape)
    return pl.BlockSpec(shape, lambda *_: (0,) * nd, pipeline_mode=pl.Buffered(1))


def _acc_spec(shape):
    nd = len(shape)
    return pl.BlockSpec(shape, lambda *_: (0,) * nd)


def _sds(shape, dtype):
    return jax.ShapeDtypeStruct(shape, dtype)


def _gelu(x):
    c = math.sqrt(2.0 / math.pi)
    return 0.5 * x * (1.0 + jnp.tanh(c * (x + 0.044715 * x * x * x)))


def _gelu_grad(x):
    c = math.sqrt(2.0 / math.pi)
    t = jnp.tanh(c * (x + 0.044715 * x * x * x))
    return 0.5 * (1.0 + t) + 0.5 * x * (1.0 - t * t) * c * (1.0 + 3.0 * 0.044715 * x * x)


def _colsum(a):
    return jnp.sum(a, axis=0, keepdims=True)


def _mesh_place():
    return lax.axis_index("x"), lax.axis_index("y"), lax.axis_index("c")


def _slot(p):
    return 4 * p[0] + 2 * p[1] + p[2]


def _other_devices(me):
    x, y, c = me
    flip = lambda v, d: 1 - v if d else v
    return [(flip(x, dx), flip(y, dy), flip(c, dc)) for dx in (0, 1) for dy in (0, 1) for dc in (0, 1)][1:]


def _all_gather(blocks, name):
    n = len(blocks)

    def body(*refs):
        ins, outs = refs[:n], refs[n:2 * n]
        send_sems, recv_sems, local_sems = refs[2 * n:]
        x, y, c = _mesh_place()
        me, sibling = (x, y, c), (x, y, 1 - c)
        chips = [(1 - x, y), (x, 1 - y), (1 - x, 1 - y)]

        def copy(a, k, block, to, src=None):
            rows = outs[a].at[_slot(block)]
            return pltpu.make_async_remote_copy(
                src_ref=rows if src is None else src, dst_ref=rows,
                send_sem=send_sems.at[a, k], recv_sem=recv_sems.at[a, k],
                device_id=to, device_id_type=pl.DeviceIdType.MESH)

        mine = [pltpu.make_async_copy(ins[a], outs[a].at[_slot(me)], local_sems.at[a]) for a in range(n)]
        for cp in mine:
            cp.start()
        first = []
        for a in range(n):
            first.append(copy(a, 0, me, sibling, src=ins[a]))
            first += [copy(a, 1 + j, me, (*chip, c), src=ins[a]) for j, chip in enumerate(chips)]
        for cp in first:
            cp.start()
        passed = []
        for a in range(n):
            for j, chip in enumerate(chips):
                copy(a, 1 + j, (*chip, c), me).wait_recv()
                fwd = copy(a, 4 + j, (*chip, c), sibling)
                fwd.start()
                passed.append(fwd)
        for a in range(n):
            copy(a, 0, sibling, me).wait_recv()
            for j, chip in enumerate(chips):
                copy(a, 4 + j, (*chip, 1 - c), me).wait_recv()
        for cp in first + passed:
            cp.wait_send()
        for cp in mine:
            cp.wait()

    any_spec = pl.BlockSpec(memory_space=pl.ANY)
    return pl.pallas_call(
        body, name=name,
        out_shape=[_sds((N_DEV,) + b.shape, b.dtype) for b in blocks],
        in_specs=[any_spec] * n, out_specs=[any_spec] * n,
        scratch_shapes=[pltpu.SemaphoreType.DMA((n, 7)), pltpu.SemaphoreType.DMA((n, 7)),
                        pltpu.SemaphoreType.DMA((n,))],
    )(*blocks)


def _kv_proj(mem, w_kv):
    def body(mem_ref, w_ref, kv_ref, memb_ref):
        mb = mem_ref[...].astype(bf16)
        memb_ref[...] = mb
        kv_ref[...] = _dot(mb, w_ref[...]).astype(bf16)

    return pl.pallas_call(
        body, name="kv_proj",
        out_shape=[_sds((MEM_LEN, 2 * XATTN_W), bf16), _sds((MEM_LEN, D_MODEL), bf16)],
        compiler_params=_cparams(),
    )(mem, w_kv)


def _attention_probs(qb, kv_ref, h):
    kh = kv_ref[:, h * HEAD_DIM:(h + 1) * HEAD_DIM]
    s = _dot(qb[:, h * HEAD_DIM:(h + 1) * HEAD_DIM], kh, NT) * (HEAD_DIM ** -0.5)
    e = jnp.exp(s - jnp.max(s, axis=-1, keepdims=True))
    return e / jnp.sum(e, axis=-1, keepdims=True)


def _in_proj(x, w_in_t, b_gate, conv_w, w_co_t, kv, w_xo_t):
    s_len = x.shape[0]
    tm = TOKEN_TILE
    n = s_len // tm

    def body(x_ref, win_ref, bg_ref, cw_ref, wco_ref, kv_ref, wxo_ref,
             xbt_ref, g_ref, cin_ref, u_ref, q_ref, ya_ref, yc_ref, aint_ref, ot_ref, zs_ref):
        i = pl.program_id(0)
        xb = x_ref[...].astype(bf16)
        xbt_ref[...] = xb.T
        proj = _dot(xb, win_ref[...], NT)
        g_ref[...] = jax.nn.sigmoid(proj[:, :GATE_COLS] + bg_ref[...])
        cin = proj[:, GATE_COLS:GATE_COLS + 3 * CONV_W]
        cin_ref[...] = cin
        u_ref[...] = proj[:, GATE_COLS + 3 * CONV_W:GATE_COLS + 3 * CONV_W + SSM_W]
        qb = proj[:, IN_COLS - XATTN_W:].astype(bf16)
        q_ref[...] = qb

        cb, cc, ch = cin[:, :CONV_W], cin[:, CONV_W:2 * CONV_W], cin[:, 2 * CONV_W:]
        z = cc * ch

        @pl.when(i == 0)
        def _():
            zs_ref[0:8, :] = jnp.zeros((8, CONV_W), f32)

        zs_ref[8:8 + tm, :] = z
        z1 = zs_ref[pl.ds(7, tm), :]
        z2 = zs_ref[pl.ds(6, tm), :]
        cw = cw_ref[...]
        cz = cw[0:1] * z2 + cw[1:2] * z1 + cw[2:3] * z
        zs_ref[0:8, :] = zs_ref[tm:tm + 8, :]
        ain = (cb * cz).astype(bf16)
        aint_ref[...] = ain.T
        ya_ref[...] = _dot(ain, wco_ref[...], NT)

        outs = []
        for h in range(HEADS):
            p = _attention_probs(qb, kv_ref, h)
            vh = kv_ref[:, XATTN_W + h * HEAD_DIM:XATTN_W + (h + 1) * HEAD_DIM]
            outs.append(_dot(p.astype(bf16), vh))
        ob = jnp.concatenate(outs, axis=1).astype(bf16)
        ot_ref[...] = ob.T
        yc_ref[...] = _dot(ob, wxo_ref[...], NT)

    row_cols = [(GATE_COLS, f32), (3 * CONV_W, f32), (SSM_W, f32), (XATTN_W, bf16), (D_MODEL, f32), (D_MODEL, f32)]
    t_rows = [D_MODEL, CONV_W, XATTN_W]
    outs = pl.pallas_call(
        body, name="in_proj", grid=(n,),
        in_specs=[_row_spec(tm, D_MODEL), _const_spec((IN_COLS, D_MODEL)), _const_spec((1, GATE_COLS)),
                  _const_spec((3, CONV_W)), _const_spec((D_MODEL, CONV_W)), _const_spec((MEM_LEN, 2 * XATTN_W)),
                  _const_spec((D_MODEL, XATTN_W))],
        out_specs=([_col_spec(t_rows[0], tm)] + [_row_spec(tm, c) for c, _ in row_cols]
                   + [_col_spec(t_rows[1], tm), _col_spec(t_rows[2], tm)]),
        out_shape=([_sds((t_rows[0], s_len), bf16)] + [_sds((s_len, c), dt) for c, dt in row_cols]
                   + [_sds((t_rows[1], s_len), bf16), _sds((t_rows[2], s_len), bf16)]),
        scratch_shapes=[pltpu.VMEM((tm + 8, CONV_W), f32)],
        compiler_params=_cparams(("arbitrary",)),
    )(x, w_in_t, b_gate, conv_w, w_co_t, kv, w_xo_t)
    return outs


def _state_cols(chunk):
    half, off = divmod(chunk * LANE_CHUNK, HALF_STATE)
    lo = half * HALF_COLS + off
    return slice(lo, lo + LANE_CHUNK), slice(lo + HALF_STATE, lo + HALF_STATE + LANE_CHUNK)


def _half_cols(half):
    lo = half * HALF_COLS
    return slice(lo, lo + HALF_STATE), slice(lo + HALF_STATE, lo + HALF_COLS)


def _rows_to_segments(src_ref, stage_ref, dst_ref):
    nc = SSM_W // LANES
    for c in range(nc):
        stage_ref[c] = src_ref[:, c * LANES:(c + 1) * LANES]
    for c in range(nc):
        for k in range(SSM_SEG):
            dst_ref[k * SUBLANES:(k + 1) * SUBLANES, c * LANES:(c + 1) * LANES] = (
                stage_ref[c, pl.ds(k, SUBLANES, stride=SSM_SEG), :])


def _rows_from_segments(src_ref, stage_ref, dst_ref):
    nc = SSM_W // LANES
    for c in range(nc):
        for k in range(SSM_SEG):
            stage_ref[c, pl.ds(k, SUBLANES, stride=SSM_SEG), :] = (
                src_ref[k * SUBLANES:(k + 1) * SUBLANES, c * LANES:(c + 1) * LANES])
    for c in range(nc):
        dst_ref[:, c * LANES:(c + 1) * LANES] = stage_ref[c]


def _ssm_scan(s_ref, pw_ref, init_ref, reverse):
    for chunk in range(N_STATE // LANE_CHUNK):
        re, im = _state_cols(chunk)
        ar = jnp.broadcast_to(pw_ref[0:1, re], (SUBLANES, LANE_CHUNK))
        ai = jnp.broadcast_to(pw_ref[0:1, im], (SUBLANES, LANE_CHUNK))
        if reverse:
            ai = -ai

        def step(j, carry, re=re, im=im, ar=ar, ai=ai):
            sr, si = carry
            k = (SSM_SEG - 1 - j) if reverse else j
            r0 = pl.multiple_of(k * SUBLANES, SUBLANES)
            nr = ar * sr - ai * si + s_ref[pl.ds(r0, SUBLANES), re]
            ni = ar * si + ai * sr + s_ref[pl.ds(r0, SUBLANES), im]
            s_ref[pl.ds(r0, SUBLANES), re] = nr
            s_ref[pl.ds(r0, SUBLANES), im] = ni
            return nr, ni

        if init_ref is None:
            init = (jnp.zeros((SUBLANES, LANE_CHUNK), f32),) * 2
        else:
            init = (init_ref[:, re], init_ref[:, im])
        lax.fori_loop(0, SSM_SEG, step, init, unroll=4)


def _ssm_add_carry(s_ref, pw_ref, cm_ref, reverse):
    for chunk in range(N_STATE // LANE_CHUNK):
        re, im = _state_cols(chunk)
        cr, ci = cm_ref[:, re], cm_ref[:, im]
        for k in range(SSM_SEG):
            pk = (SSM_SEG - 1 - k) if reverse else k
            pr = pw_ref[pk:pk + 1, re]
            pi = pw_ref[pk:pk + 1, im]
            if reverse:
                pi = -pi
            rows = slice(k * SUBLANES, (k + 1) * SUBLANES)
            s_ref[rows, re] = s_ref[rows, re] + (pr * cr - pi * ci)
            s_ref[rows, im] = s_ref[rows, im] + (pr * ci + pi * cr)


def _ssm_carries(first_row, s_ref, pw_ref, carry_ref, cm_ref, reverse):
    order = range(SUBLANES - 1, -1, -1) if reverse else range(SUBLANES)
    for half in range(N_HALF):
        re, im = _half_cols(half)
        a_r, a_i = pw_ref[SSM_SEG - 1:SSM_SEG, re], pw_ref[SSM_SEG - 1:SSM_SEG, im]
        if reverse:
            a_i = -a_i
        cr, ci = carry_ref[0:1, re], carry_ref[0:1, im]
        for seg in order:
            cm_ref[seg:seg + 1, re] = cr
            cm_ref[seg:seg + 1, im] = ci
            er = s_ref[first_row + seg:first_row + seg + 1, re]
            ei = s_ref[first_row + seg:first_row + seg + 1, im]
            cr, ci = a_r * cr - a_i * ci + er, a_r * ci + a_i * cr + ei
        carry_ref[0:1, re] = cr
        carry_ref[0:1, im] = ci


def _ssm_fwd(u, b_half, c_half, pw, d_skip):
    s_len = u.shape[0]
    tb = SSM_BLOCK
    n = s_len // tb

    def body(u_ref, b_ref, c_ref, pw_ref, d_ref, y_ref, cm_ref, s_ref, carry_ref, up_ref, yp_ref, stage_ref):
        i = pl.program_id(0)

        @pl.when(i == 0)
        def _():
            carry_ref[...] = jnp.zeros_like(carry_ref)

        _rows_to_segments(u_ref, stage_ref, up_ref)
        u = up_ref[...]
        ub = u.astype(bf16)
        for half in range(N_HALF):
            s_ref[:, half * HALF_COLS:(half + 1) * HALF_COLS] = _dot(ub[:, half * HALF_W:(half + 1) * HALF_W], b_ref[half])
        _ssm_scan(s_ref, pw_ref, None, reverse=False)
        _ssm_carries(tb - SUBLANES, s_ref, pw_ref, carry_ref, cm_ref, reverse=False)
        _ssm_add_carry(s_ref, pw_ref, cm_ref, reverse=False)
        for half in range(N_HALF):
            cols = slice(half * HALF_W, (half + 1) * HALF_W)
            sb = s_ref[:, half * HALF_COLS:(half + 1) * HALF_COLS].astype(bf16)
            yp_ref[:, cols] = _dot(sb, c_ref[half]) + d_ref[:, cols] * u[:, cols]
        _rows_from_segments(yp_ref, stage_ref, y_ref)

    return pl.pallas_call(
        body, name="ssm_fwd", grid=(n,),
        in_specs=[_row_spec(tb, SSM_W), _const_spec((N_HALF, HALF_W, HALF_COLS)), _const_spec((N_HALF, HALF_COLS, HALF_W)),
                  _const_spec((SSM_SEG, 2 * N_STATE)), _const_spec((1, SSM_W))],
        out_specs=[_row_spec(tb, SSM_W), _row_spec(SUBLANES, 2 * N_STATE)],
        out_shape=[_sds((s_len, SSM_W), f32), _sds((n * SUBLANES, 2 * N_STATE), f32)],
        scratch_shapes=[pltpu.VMEM((tb, 2 * N_STATE), f32), pltpu.VMEM((SUBLANES, 2 * N_STATE), f32),
                        pltpu.VMEM((tb, SSM_W), f32), pltpu.VMEM((tb, SSM_W), f32),
                        pltpu.VMEM((SSM_W // LANES, tb, LANES), f32)],
        compiler_params=_cparams(("arbitrary",)),
    )(u, b_half, c_half, pw, d_skip)


def _layer_norm_fwd(r, g, b):
    mu = jnp.mean(r, axis=-1, keepdims=True)
    var = jnp.mean(jnp.square(r - mu), axis=-1, keepdims=True)
    rstd = lax.rsqrt(var + LN_EPS)
    xhat = (r - mu) * rstd
    return xhat, rstd, xhat * g + b


def _layer_norm_bwd(dy, xhat, rstd, g):
    dxh = dy * g
    m1 = jnp.mean(dxh, axis=-1, keepdims=True)
    m2 = jnp.mean(dxh * xhat, axis=-1, keepdims=True)
    return rstd * (dxh - m1 - xhat * m2)


def _mid_fwd(y_ssm, g, ya, yc, x, w_glu_t, w_out, ln1_g, ln1_b):
    s_len = x.shape[0]
    tm = TOKEN_TILE
    n = s_len // tm

    def body(ys_ref, g_ref, ya_ref, yc_ref, x_ref, wglu_ref, wout_ref, lg_ref, lb_ref,
             ysbt_ref, glu_ref, mb_ref, xhat_ref, rstd_ref):
        ysb = _gelu(ys_ref[...]).astype(bf16)
        ysbt_ref[...] = ysb.T
        glu = _dot(ysb, wglu_ref[...], NT)
        glu_ref[...] = glu
        yb = glu[:, :D_MODEL] * jax.nn.sigmoid(glu[:, D_MODEL:])
        gt = g_ref[...]
        merged = (gt[:, :D_MODEL] * ya_ref[...] + gt[:, D_MODEL:2 * D_MODEL] * yb
                  + gt[:, 2 * D_MODEL:] * yc_ref[...])
        mb = merged.astype(bf16)
        mb_ref[...] = mb
        r1 = ALPHA * x_ref[...] + _dot(mb, wout_ref[...])
        xhat, rstd, _ = _layer_norm_fwd(r1, lg_ref[...], lb_ref[...])
        xhat_ref[...] = xhat
        rstd_ref[...] = rstd

    row_cols = [(2 * D_MODEL, f32), (D_MODEL, bf16), (D_MODEL, f32), (1, f32)]
    return pl.pallas_call(
        body, name="mid_fwd", grid=(n,),
        in_specs=[_row_spec(tm, SSM_W), _row_spec(tm, GATE_COLS), _row_spec(tm, D_MODEL), _row_spec(tm, D_MODEL),
                  _row_spec(tm, D_MODEL), _const_spec((2 * D_MODEL, SSM_W)), _const_spec((D_MODEL, D_MODEL)),
                  _const_spec((1, D_MODEL)), _const_spec((1, D_MODEL))],
        out_specs=[_col_spec(SSM_W, tm)] + [_row_spec(tm, c) for c, _ in row_cols],
        out_shape=[_sds((SSM_W, s_len), bf16)] + [_sds((s_len, c), dt) for c, dt in row_cols],
        compiler_params=_cparams(("parallel",)),
    )(y_ssm, g, ya, yc, x, w_glu_t, w_out, ln1_g, ln1_b)


def _mlp_fwd_bwd(xhat1, tgt, ln1_g, ln1_b, w_up_t, b_up, w_down, b_down, ln2_g, ln2_b):
    s_len = xhat1.shape[0]
    tm = TOKEN_TILE
    n = s_len // tm
    fc = 1024
    nfc = D_FF // fc

    def body(xh_ref, t_ref, l1g_ref, l1b_ref, wup_ref, bup_ref, wdn_ref, bdn_ref, l2g_ref, l2b_ref,
             x1bt_ref, hdn_ref, dr2bt_ref, dpre_ref, dx1_ref,
             loss_ref, dl2g_ref, dl2b_ref, dbdn_ref, dbup_ref, rl_ref):
        i = pl.program_id(0)

        @pl.when(i == 0)
        def _():
            loss_ref[...] = jnp.zeros_like(loss_ref)
            dl2g_ref[...] = jnp.zeros_like(dl2g_ref)
            dl2b_ref[...] = jnp.zeros_like(dl2b_ref)
            dbdn_ref[...] = jnp.zeros_like(dbdn_ref)
            dbup_ref[...] = jnp.zeros_like(dbup_ref)

        x1 = xh_ref[...] * l1g_ref[...] + l1b_ref[...]
        x1b = x1.astype(bf16)
        x1bt_ref[...] = x1b.T
        acc = jnp.zeros((tm, D_MODEL), f32)
        for c in range(nfc):
            cols = slice(c * fc, (c + 1) * fc)
            pre = _dot(x1b, wup_ref[cols, :], NT) + bup_ref[:, cols]
            rl = jnp.maximum(pre, 0.0)
            rl_ref[:, cols] = rl
            hb = (rl * rl).astype(bf16)
            hdn_ref[:, cols] = hb
            acc = acc + _dot(hb, wdn_ref[cols, :])
        r2 = ALPHA * x1 + acc + bdn_ref[...]
        xhat2, rstd2, y = _layer_norm_fwd(r2, l2g_ref[...], l2b_ref[...])
        err = y - t_ref[...]
        loss_ref[...] += jnp.sum(jnp.sum(err * err, axis=1, keepdims=True), axis=0, keepdims=True) * (0.5 / D_MODEL)
        dy = err * (1.0 / D_MODEL)
        dl2g_ref[...] += _colsum(dy * xhat2)
        dl2b_ref[...] += _colsum(dy)
        dr2 = _layer_norm_bwd(dy, xhat2, rstd2, l2g_ref[...])
        dbdn_ref[...] += _colsum(dr2)
        dr2b = dr2.astype(bf16)
        dr2bt_ref[...] = dr2b.T
        dacc = jnp.zeros((tm, D_MODEL), f32)
        for c in range(nfc):
            cols = slice(c * fc, (c + 1) * fc)
            dh = _dot(dr2b, wdn_ref[cols, :], NT)
            dpre = dh * (2.0 * rl_ref[:, cols])
            dbup_ref[:, cols] += _colsum(dpre)
            dpb = dpre.astype(bf16)
            dpre_ref[:, cols] = dpb
            dacc = dacc + _dot(dpb, wup_ref[cols, :])
        dx1_ref[...] = ALPHA * dr2 + dacc

    acc_shapes = [(1, LANES), (1, D_MODEL), (1, D_MODEL), (1, D_MODEL), (1, D_FF)]
    return pl.pallas_call(
        body, name="mlp_fwd_bwd", grid=(n,),
        in_specs=[_row_spec(tm, D_MODEL), _row_spec(tm, D_MODEL), _const_spec((1, D_MODEL)), _const_spec((1, D_MODEL)),
                  _const_spec((D_FF, D_MODEL)), _const_spec((1, D_FF)), _const_spec((D_FF, D_MODEL)),
                  _const_spec((1, D_MODEL)), _const_spec((1, D_MODEL)), _const_spec((1, D_MODEL))],
        out_specs=([_col_spec(D_MODEL, tm), _row_spec(tm, D_FF), _col_spec(D_MODEL, tm), _row_spec(tm, D_FF),
                    _row_spec(tm, D_MODEL)] + [_acc_spec(s) for s in acc_shapes]),
        out_shape=([_sds((D_MODEL, s_len), bf16), _sds((s_len, D_FF), bf16), _sds((D_MODEL, s_len), bf16),
                    _sds((s_len, D_FF), bf16), _sds((s_len, D_MODEL), f32)] + [_sds(s, f32) for s in acc_shapes]),
        scratch_shapes=[pltpu.VMEM((tm, D_FF), f32)],
        compiler_params=_cparams(("arbitrary",)),
    )(xhat1, tgt, ln1_g, ln1_b, w_up_t, b_up, w_down, b_down, ln2_g, ln2_b)


def _mid_bwd(dx1, xhat1, rstd1, g, ya, yc, glu, y_ssm, ln1_g, w_out, w_glu_t):
    s_len = dx1.shape[0]
    tm = TOKEN_TILE
    n = s_len // tm

    def body(dx1_ref, xh_ref, rs_ref, g_ref, ya_ref, yc_ref, glu_ref, ys_ref, lg_ref, wout_ref, wglu_ref,
             dxp_ref, dr1bt_ref, dgp_ref, dya_ref, dyc_ref, dglu_ref, dyssm_ref,
             dl1g_ref, dl1b_ref, dbg_ref):
        i = pl.program_id(0)

        @pl.when(i == 0)
        def _():
            dl1g_ref[...] = jnp.zeros_like(dl1g_ref)
            dl1b_ref[...] = jnp.zeros_like(dl1b_ref)
            dbg_ref[...] = jnp.zeros_like(dbg_ref)

        dx1 = dx1_ref[...]
        xhat = xh_ref[...]
        dl1g_ref[...] += _colsum(dx1 * xhat)
        dl1b_ref[...] += _colsum(dx1)
        dr1 = _layer_norm_bwd(dx1, xhat, rs_ref[...], lg_ref[...])
        dxp_ref[...] = ALPHA * dr1
        dr1b = dr1.astype(bf16)
        dr1bt_ref[...] = dr1b.T
        dm = _dot(dr1b, wout_ref[...], NT)

        glu = glu_ref[...]
        ga, sb = glu[:, :D_MODEL], jax.nn.sigmoid(glu[:, D_MODEL:])
        yb = ga * sb
        gt = g_ref[...]
        branch = (ya_ref[...], yb, yc_ref[...])
        for j in range(3):
            cols = slice(j * D_MODEL, (j + 1) * D_MODEL)
            gj = gt[:, cols]
            dgp = dm * branch[j] * gj * (1.0 - gj)
            dbg_ref[:, cols] += _colsum(dgp)
            dgp_ref[:, cols] = dgp.astype(bf16)
        dya_ref[...] = (dm * gt[:, :D_MODEL]).astype(bf16)
        dyc_ref[...] = (dm * gt[:, 2 * D_MODEL:]).astype(bf16)
        dyb = dm * gt[:, D_MODEL:2 * D_MODEL]
        dga = (dyb * sb).astype(bf16)
        dgb = (dyb * ga * sb * (1.0 - sb)).astype(bf16)
        dglu_ref[:, :D_MODEL] = dga
        dglu_ref[:, D_MODEL:] = dgb
        dys = _dot(dga, wglu_ref[:D_MODEL, :]) + _dot(dgb, wglu_ref[D_MODEL:, :])
        dyssm_ref[...] = dys * _gelu_grad(ys_ref[...])

    row_cols = [(GATE_COLS, bf16), (D_MODEL, bf16), (D_MODEL, bf16), (2 * D_MODEL, bf16), (SSM_W, f32)]
    acc_shapes = [(1, D_MODEL), (1, D_MODEL), (1, GATE_COLS)]
    return pl.pallas_call(
        body, name="mid_bwd", grid=(n,),
        in_specs=[_row_spec(tm, D_MODEL), _row_spec(tm, D_MODEL), _row_spec(tm, 1), _row_spec(tm, GATE_COLS),
                  _row_spec(tm, D_MODEL), _row_spec(tm, D_MODEL), _row_spec(tm, 2 * D_MODEL), _row_spec(tm, SSM_W),
                  _const_spec((1, D_MODEL)), _const_spec((D_MODEL, D_MODEL)), _const_spec((2 * D_MODEL, SSM_W))],
        out_specs=([_row_spec(tm, D_MODEL), _col_spec(D_MODEL, tm)] + [_row_spec(tm, c) for c, _ in row_cols]
                   + [_acc_spec(s) for s in acc_shapes]),
        out_shape=([_sds((s_len, D_MODEL), f32), _sds((D_MODEL, s_len), bf16)]
                   + [_sds((s_len, c), dt) for c, dt in row_cols] + [_sds(s, f32) for s in acc_shapes]),
        compiler_params=_cparams(("arbitrary",)),
    )(dx1, xhat1, rstd1, g, ya, yc, glu, y_ssm, ln1_g, w_out, w_glu_t)


def _ssm_bwd(u, dy, cm_all, b_half, c_half, pw, d_skip):
    s_len = u.shape[0]
    tb = SSM_BLOCK
    n = s_len // tb

    def body(u_ref, dy_ref, cm_ref, b_ref, c_ref, pw_ref, d_ref,
             du_ref, db_hbm, dc_hbm, da_ref, dd_ref,
             s_ref, g_ref, gcarry_ref, gcm_ref, db_ref, dc_ref, up_ref, dyp_ref, dup_ref, stage_ref):
        i = pl.program_id(0)

        @pl.when(i == 0)
        def _():
            gcarry_ref[...] = jnp.zeros_like(gcarry_ref)
            db_ref[...] = jnp.zeros_like(db_ref)
            dc_ref[...] = jnp.zeros_like(dc_ref)
            da_ref[...] = jnp.zeros_like(da_ref)
            dd_ref[...] = jnp.zeros_like(dd_ref)

        _rows_to_segments(u_ref, stage_ref, up_ref)
        _rows_to_segments(dy_ref, stage_ref, dyp_ref)
        u = up_ref[...]
        ub = u.astype(bf16)
        dy = dyp_ref[...]
        dyb = dy.astype(bf16)
        dd_ref[...] += _colsum(dy * u)

        for half in range(N_HALF):
            s_ref[:, half * HALF_COLS:(half + 1) * HALF_COLS] = _dot(ub[:, half * HALF_W:(half + 1) * HALF_W], b_ref[half])
        _ssm_scan(s_ref, pw_ref, cm_ref, reverse=False)

        for half in range(N_HALF):
            g_ref[:, half * HALF_COLS:(half + 1) * HALF_COLS] = _dot(dyb[:, half * HALF_W:(half + 1) * HALF_W], c_ref[half], NT)
        _ssm_scan(g_ref, pw_ref, None, reverse=True)
        _ssm_carries(0, g_ref, pw_ref, gcarry_ref, gcm_ref, reverse=True)
        _ssm_add_carry(g_ref, pw_ref, gcm_ref, reverse=True)

        for half in range(N_HALF):
            cols = slice(half * HALF_W, (half + 1) * HALF_W)
            scols = slice(half * HALF_COLS, (half + 1) * HALF_COLS)
            gb = g_ref[:, scols].astype(bf16)
            dup_ref[:, cols] = _dot(gb, b_ref[half], NT) + d_ref[:, cols] * dy[:, cols]
            db_ref[half] += _dot(ub[:, cols], gb, TN)
            dc_ref[half] += _dot(s_ref[:, scols].astype(bf16), dyb[:, cols], TN)
        _rows_from_segments(dup_ref, stage_ref, du_ref)

        for chunk in range(N_STATE // LANE_CHUNK):
            re, im = _state_cols(chunk)
            acc_r = da_ref[:, re]
            acc_i = da_ref[:, im]
            for k in range(SSM_SEG):
                rows = slice(k * SUBLANES, (k + 1) * SUBLANES)
                if k == 0:
                    pr, pi = cm_ref[:, re], cm_ref[:, im]
                else:
                    prev = slice((k - 1) * SUBLANES, k * SUBLANES)
                    pr, pi = s_ref[prev, re], s_ref[prev, im]
                gr, gi = g_ref[rows, re], g_ref[rows, im]
                acc_r = acc_r + (gr * pr + gi * pi)
                acc_i = acc_i + (gi * pr - gr * pi)
            da_ref[:, re] = acc_r
            da_ref[:, im] = acc_i

        @pl.when(i == n - 1)
        def _():
            pltpu.sync_copy(db_ref, db_hbm)
            pltpu.sync_copy(dc_ref, dc_hbm)

    rev = functools.partial(_row_spec, rev_n=n)
    any_spec = pl.BlockSpec(memory_space=pl.ANY)
    state_rows = pltpu.VMEM((tb, 2 * N_STATE), f32)
    seg_rows = pltpu.VMEM((SUBLANES, 2 * N_STATE), f32)
    tok_rows = pltpu.VMEM((tb, SSM_W), f32)
    return pl.pallas_call(
        body, name="ssm_bwd", grid=(n,),
        in_specs=[rev(tb, SSM_W), rev(tb, SSM_W), rev(SUBLANES, 2 * N_STATE),
                  _const_spec((N_HALF, HALF_W, HALF_COLS)), _const_spec((N_HALF, HALF_COLS, HALF_W)),
                  _const_spec((SSM_SEG, 2 * N_STATE)), _const_spec((1, SSM_W))],
        out_specs=[rev(tb, SSM_W), any_spec, any_spec, _acc_spec((SUBLANES, 2 * N_STATE)), _acc_spec((1, SSM_W))],
        out_shape=[_sds((s_len, SSM_W), f32), _sds((N_HALF, HALF_W, HALF_COLS), f32),
                   _sds((N_HALF, HALF_COLS, HALF_W), f32), _sds((SUBLANES, 2 * N_STATE), f32), _sds((1, SSM_W), f32)],
        scratch_shapes=[state_rows, state_rows, seg_rows, seg_rows,
                        pltpu.VMEM((N_HALF, HALF_W, HALF_COLS), f32), pltpu.VMEM((N_HALF, HALF_COLS, HALF_W), f32),
                        tok_rows, tok_rows, tok_rows, pltpu.VMEM((SSM_W // LANES, tb, LANES), f32)],
        compiler_params=_cparams(("arbitrary",)),
    )(u, dy, cm_all, b_half, c_half, pw, d_skip)


def _branch_bwd(dya, dyc, cin, q, kv, conv_w, w_co_t, w_xo_t):
    s_len = dya.shape[0]
    tm = TOKEN_TILE
    n = s_len // tm
    halo_blocks = tm // 8

    def body(dya_ref, dyc_ref, cin_ref, cprev_ref, q_ref, kv_ref, cw_ref, wco_ref, wxo_ref,
             dconv_ref, dq_ref, dcw_ref, dkv_ref, zs_ref, dczs_ref):
        i = pl.program_id(0)
        tile = n - 1 - i

        @pl.when(i == 0)
        def _():
            dcw_ref[...] = jnp.zeros_like(dcw_ref)
            dkv_ref[...] = jnp.zeros_like(dkv_ref)
            dczs_ref[tm:tm + 8, :] = jnp.zeros((8, CONV_W), f32)

        cin = cin_ref[...]
        cb, cc, ch = cin[:, :CONV_W], cin[:, CONV_W:2 * CONV_W], cin[:, 2 * CONV_W:]
        z = cc * ch
        cprev = cprev_ref[...]
        zprev = cprev[:, CONV_W:2 * CONV_W] * cprev[:, 2 * CONV_W:]
        zs_ref[0:8, :] = jnp.where(tile == 0, 0.0, zprev)
        zs_ref[8:8 + tm, :] = z
        z1 = zs_ref[pl.ds(7, tm), :]
        z2 = zs_ref[pl.ds(6, tm), :]
        cw = cw_ref[...]
        cz = cw[0:1] * z2 + cw[1:2] * z1 + cw[2:3] * z

        dain = _dot(dya_ref[...], wco_ref[...])
        dcb = dain * cz
        dcz = dain * cb
        dczs_ref[0:tm, :] = dcz
        dcz1 = dczs_ref[pl.ds(1, tm), :]
        dcz2 = dczs_ref[pl.ds(2, tm), :]
        dz = cw[2:3] * dcz + cw[1:2] * dcz1 + cw[0:1] * dcz2
        dczs_ref[tm:tm + 8, :] = dczs_ref[0:8, :]
        dcw_ref[0:1, :] += _colsum(dcz * z2)
        dcw_ref[1:2, :] += _colsum(dcz * z1)
        dcw_ref[2:3, :] += _colsum(dcz * z)
        dconv_ref[:, :CONV_W] = dcb.astype(bf16)
        dconv_ref[:, CONV_W:2 * CONV_W] = (dz * ch).astype(bf16)
        dconv_ref[:, 2 * CONV_W:] = (dz * cc).astype(bf16)

        qb = q_ref[...]
        do = _dot(dyc_ref[...], wxo_ref[...])
        for h in range(HEADS):
            hc = slice(h * HEAD_DIM, (h + 1) * HEAD_DIM)
            vc = slice(XATTN_W + h * HEAD_DIM, XATTN_W + (h + 1) * HEAD_DIM)
            p = _attention_probs(qb, kv_ref, h)
            dob = do[:, hc].astype(bf16)
            dp = _dot(dob, kv_ref[:, vc], NT)
            dkv_ref[:, vc] += _dot(p.astype(bf16), dob, TN)
            ds = p * (dp - jnp.sum(dp * p, axis=-1, keepdims=True)) * (HEAD_DIM ** -0.5)
            dsb = ds.astype(bf16)
            dq_ref[:, hc] = _dot(dsb, kv_ref[:, hc]).astype(bf16)
            dkv_ref[:, hc] += _dot(dsb, qb[:, hc], TN)

    rev = functools.partial(_row_spec, rev_n=n)
    prev_spec = pl.BlockSpec((8, 3 * CONV_W), lambda i: (jnp.maximum((n - 1 - i) * halo_blocks - 1, 0), 0))
    return pl.pallas_call(
        body, name="branch_bwd", grid=(n,),
        in_specs=[rev(tm, D_MODEL), rev(tm, D_MODEL), rev(tm, 3 * CONV_W), prev_spec, rev(tm, XATTN_W),
                  _const_spec((MEM_LEN, 2 * XATTN_W)), _const_spec((3, CONV_W)), _const_spec((D_MODEL, CONV_W)),
                  _const_spec((D_MODEL, XATTN_W))],
        out_specs=[rev(tm, 3 * CONV_W), rev(tm, XATTN_W), _acc_spec((8, CONV_W)), _acc_spec((MEM_LEN, 2 * XATTN_W))],
        out_shape=[_sds((s_len, 3 * CONV_W), bf16), _sds((s_len, XATTN_W), bf16), _sds((8, CONV_W), f32),
                   _sds((MEM_LEN, 2 * XATTN_W), f32)],
        scratch_shapes=[pltpu.VMEM((tm + 8, CONV_W), f32), pltpu.VMEM((tm + 8, CONV_W), f32)],
        compiler_params=_cparams(("arbitrary",)),
    )(dya, dyc, cin, cin, q, kv, conv_w, w_co_t, w_xo_t)


def _in_proj_bwd(dgp, dconv, du, dq, dxp, w_in_t):
    s_len = dgp.shape[0]
    tm = TOKEN_TILE
    n = s_len // tm

    def body(dgp_ref, dconv_ref, du_ref, dq_ref, dxp_ref, win_ref, dx_ref, dproj_ref):
        dproj = jnp.concatenate([dgp_ref[...], dconv_ref[...], du_ref[...].astype(bf16), dq_ref[...]], axis=1)
        dproj_ref[...] = dproj
        dx_ref[...] = dxp_ref[...] + _dot(dproj, win_ref[...])

    return pl.pallas_call(
        body, name="in_proj_bwd", grid=(n,),
        in_specs=[_row_spec(tm, GATE_COLS), _row_spec(tm, 3 * CONV_W), _row_spec(tm, SSM_W), _row_spec(tm, XATTN_W),
                  _row_spec(tm, D_MODEL), _const_spec((IN_COLS, D_MODEL))],
        out_specs=[_row_spec(tm, D_MODEL), _row_spec(tm, IN_COLS)],
        out_shape=[_sds((s_len, D_MODEL), f32), _sds((s_len, IN_COLS), bf16)],
        compiler_params=_cparams(("parallel",)),
    )(dgp, dconv, du, dq, dxp, w_in_t)


def _weight_grad_scatter(a_t, b, name, tm, tn, tt):
    m, s_len = a_t.shape
    n_cols = b.shape[1]
    w = n_cols // N_DEV
    tm, tn, tt = min(tm, m), min(tn, n_cols), min(tt, s_len)
    nn, nm, nt = n_cols // tn, m // tm, s_len // tt
    own = tn // w
    assert tn % w == 0 and m % tm == 0 and s_len % tt == 0

    def body(a_ref, b_ref, recv_ref, acc_ref, send_ref, send_sems, recv_sems, local_sem):
        jn, im, t = pl.program_id(0), pl.program_id(1), pl.program_id(2)
        me = _mesh_place()
        my_slot = _slot(me)
        peers = _other_devices(me)

        @pl.when(t == 0)
        def _():
            acc_ref[...] = jnp.zeros_like(acc_ref)

        acc_ref[...] += _dot(a_ref[...], b_ref[...])

        @pl.when(t == nt - 1)
        def _():
            r0 = pl.multiple_of(im * tm, tm)
            rows = pl.ds(r0, tm)
            for o in range(own):
                send_ref[jn * own + o, rows, :] = acc_ref[:, o * w:(o + 1) * w].astype(bf16)

            @pl.when(my_slot // own == jn)
            def _():
                pltpu.make_async_copy(send_ref.at[my_slot, rows, :], recv_ref.at[my_slot, rows, :], local_sem).start()

            for k, peer in enumerate(peers):
                @pl.when(_slot(peer) // own == jn)
                def _(k=k, peer=peer):
                    pltpu.make_async_remote_copy(
                        src_ref=send_ref.at[_slot(peer), rows, :], dst_ref=recv_ref.at[my_slot, rows, :],
                        send_sem=send_sems.at[k], recv_sem=recv_sems.at[k],
                        device_id=peer, device_id_type=pl.DeviceIdType.MESH).start()

        @pl.when((jn == nn - 1) & (im == nm - 1) & (t == nt - 1))
        def _():
            for k, peer in enumerate(peers):
                pltpu.make_async_remote_copy(
                    src_ref=send_ref.at[_slot(peer)], dst_ref=recv_ref.at[_slot(peer)],
                    send_sem=send_sems.at[k], recv_sem=recv_sems.at[k],
                    device_id=peer, device_id_type=pl.DeviceIdType.MESH).wait()
            pltpu.make_async_copy(send_ref.at[my_slot], recv_ref.at[my_slot], local_sem).wait()

    return pl.pallas_call(
        body, name=name, grid=(nn, nm, nt),
        in_specs=[pl.BlockSpec((tm, tt), lambda jn, im, t: (im, t)), pl.BlockSpec((tt, tn), lambda jn, im, t: (t, jn))],
        out_specs=pl.BlockSpec(memory_space=pl.ANY),
        out_shape=_sds((N_DEV, m, w), bf16),
        scratch_shapes=[pltpu.VMEM((tm, tn), f32), pltpu.VMEM((N_DEV, m, w), bf16),
                        pltpu.SemaphoreType.DMA((N_DEV - 1,)), pltpu.SemaphoreType.DMA((N_DEV - 1,)),
                        pltpu.SemaphoreType.DMA],
        compiler_params=_cparams(("arbitrary", "arbitrary", "arbitrary")),
    )(a_t, b)


def _adamw(w, g, m, v):
    m = ADAM_B1 * m + (1.0 - ADAM_B1) * g
    v = ADAM_B2 * v + (1.0 - ADAM_B2) * jnp.square(g)
    m_hat = m / (1.0 - ADAM_B1 ** ADAM_STEP)
    v_hat = v / (1.0 - ADAM_B2 ** ADAM_STEP)
    delta = -ADAM_LR * (m_hat / (jnp.sqrt(v_hat) + ADAM_EPS) + ADAM_WD * w)
    return delta, m, v


def _sum_parts(p_ref):
    g = p_ref[0].astype(f32)
    for j in range(1, N_DEV):
        g = g + p_ref[j].astype(f32)
    return g


def _adamw_update(w, m, v, parts, name, transposed):
    rows, cols = w.shape
    if transposed:
        tr = LANES
        p_spec = pl.BlockSpec((N_DEV, cols, tr), lambda i: (0, 0, i))
    else:
        tr = next(t for t in (256, 128, 64, 32, 16, 8) if rows % t == 0)
        p_spec = pl.BlockSpec((N_DEV, tr, cols), lambda i: (0, i, 0))
    spec = pl.BlockSpec((tr, cols), lambda i: (i, 0))

    def body(w_ref, p_ref, m_ref, v_ref, g_ref, d_ref, nm_ref, nv_ref):
        g = _sum_parts(p_ref)
        if transposed:
            g = g.T
        g_ref[...] = g
        d_ref[...], nm_ref[...], nv_ref[...] = _adamw(w_ref[...], g, m_ref[...], v_ref[...])

    return pl.pallas_call(
        body, name=name, grid=(rows // tr,),
        in_specs=[spec, p_spec, spec, spec], out_specs=[spec] * 4,
        out_shape=[_sds((rows, cols), f32)] * 4,
        compiler_params=_cparams(("parallel",)),
    )(w, parts, m, v)


def _adamw_small(ws, ms, vs, parts):
    n = len(ws)

    def body(*refs):
        w_refs, m_refs, v_refs, p_refs = (refs[j * n:(j + 1) * n] for j in range(4))
        out_refs = refs[4 * n:]
        for a in range(n):
            g = _sum_parts(p_refs[a])
            d, nm, nv = _adamw(w_refs[a][...], g, m_refs[a][...], v_refs[a][...])
            for j, val in enumerate((g, d, nm, nv)):
                out_refs[j * n + a][...] = val

    return pl.pallas_call(
        body, name="adamw_small",
        out_shape=[_sds(w.shape, f32) for _ in range(4) for w in ws],
        compiler_params=_cparams(),
    )(*ws, *ms, *vs, *parts)


def _ssm_discretize(lam_re, lam_im, log_dt, b_re, b_im):
    dt = jnp.exp(log_dt)[:, None]
    mag = jnp.exp(lam_re * dt)
    abar_r = mag * jnp.cos(lam_im * dt)
    abar_i = mag * jnp.sin(lam_im * dt)
    den = lam_re * lam_re + lam_im * lam_im
    nr = abar_r - 1.0
    ni = abar_i
    kr = (nr * lam_re + ni * lam_im) / den
    ki = (ni * lam_re - nr * lam_im) / den
    bbar_r = kr[..., None] * b_re - ki[..., None] * b_im
    bbar_i = kr[..., None] * b_im + ki[..., None] * b_re
    return abar_r, abar_i, bbar_r, bbar_i


def _state_layout(re, im):
    parts = []
    for half in range(N_HALF):
        cols = slice(half * HALF_STATE, (half + 1) * HALF_STATE)
        parts += [re[..., cols], im[..., cols]]
    return jnp.concatenate(parts, axis=-1)


def _state_unlayout(a):
    re = jnp.concatenate([a[..., _half_cols(h)[0]] for h in range(N_HALF)], axis=-1)
    im = jnp.concatenate([a[..., _half_cols(h)[1]] for h in range(N_HALF)], axis=-1)
    return re, im


def _abar_powers(abar_r, abar_i):
    pr, pi = abar_r.reshape(1, N_STATE), abar_i.reshape(1, N_STATE)
    while pr.shape[0] < SSM_SEG:
        tr, ti = pr[-1:], pi[-1:]
        pr, pi = (jnp.concatenate([pr, pr * tr - pi * ti], axis=0), jnp.concatenate([pi, pr * ti + pi * tr], axis=0))
    return _state_layout(pr, pi)


HALF_GROUPS = SSM_GROUPS // N_HALF


def _half_block_diag(blocks):
    _, r, c = blocks.shape
    eye = jnp.eye(HALF_GROUPS, dtype=blocks.dtype)
    b4 = blocks.reshape(N_HALF, HALF_GROUPS, r, c)
    return jnp.einsum("ngrc,gk->ngrkc", b4, eye).reshape(N_HALF, HALF_GROUPS * r, HALF_GROUPS * c)


def _half_diag_blocks(mat, r, c):
    eye = jnp.eye(HALF_GROUPS, dtype=mat.dtype)
    m5 = mat.reshape(N_HALF, HALF_GROUPS, r, HALF_GROUPS, c)
    return jnp.einsum("ngrkc,gk->ngrc", m5, eye).reshape(SSM_GROUPS, r, c)


BIG = ["w_in", "w_conv_out", "w_glu", "w_kv", "w_xattn_out", "w_out", "w_up", "w_down"]
COL_SHARDED = ["w_in", "w_conv_out", "w_glu", "w_xattn_out", "w_up"]
SMALL = {"b_gate": (1, GATE_COLS), "conv_w": (3, CONV_W), "ssm_lam_re": (SSM_GROUPS, SSM_STATE),
         "ssm_lam_im": (SSM_GROUPS, SSM_STATE), "ssm_log_dt": (1, SSM_GROUPS),
         "ssm_b_re": (N_STATE, SSM_GROUP), "ssm_b_im": (N_STATE, SSM_GROUP),
         "ssm_c_re": (SSM_W, SSM_STATE), "ssm_c_im": (SSM_W, SSM_STATE), "ssm_d": (1, SSM_W),
         "ln1_g": (1, D_MODEL), "ln1_b": (1, D_MODEL), "b_up": (1, D_FF), "b_down": (1, D_MODEL),
         "ln2_g": (1, D_MODEL), "ln2_b": (1, D_MODEL)}
WEIGHTS = ["w_in", "b_gate", "conv_w", "w_conv_out", "ssm_lam_re", "ssm_lam_im", "ssm_log_dt", "ssm_b_re", "ssm_b_im",
           "ssm_c_re", "ssm_c_im", "ssm_d", "w_glu", "w_kv", "w_xattn_out", "w_out", "ln1_g", "ln1_b", "w_up", "b_up",
           "w_down", "b_down", "ln2_g", "ln2_b"]


def _local_step(x, mem, tgt, full, small):
    lam_re, lam_im, log_dt = small["ssm_lam_re"], small["ssm_lam_im"], small["ssm_log_dt"].reshape(SSM_GROUPS)
    b_shape = (SSM_GROUPS, SSM_STATE, SSM_GROUP)
    c_shape = (SSM_GROUPS, SSM_GROUP, SSM_STATE)
    disc, disc_vjp = jax.vjp(_ssm_discretize, lam_re, lam_im, log_dt,
                             small["ssm_b_re"].reshape(b_shape), small["ssm_b_im"].reshape(b_shape))
    abar_r, abar_i, bbar_r, bbar_i = disc
    pw = _abar_powers(abar_r, abar_i)
    c_re, c_im = small["ssm_c_re"].reshape(c_shape), small["ssm_c_im"].reshape(c_shape)
    b_half = jnp.concatenate([_half_block_diag(bbar_r.transpose(0, 2, 1)), _half_block_diag(bbar_i.transpose(0, 2, 1))],
                             axis=2).astype(bf16)
    c_half = jnp.concatenate([_half_block_diag(c_re.transpose(0, 2, 1)), -_half_block_diag(c_im.transpose(0, 2, 1))],
                             axis=1).astype(bf16)

    kv, memb = _kv_proj(mem, full["w_kv"])
    xbt, g, cin, u, q, ya, yc, aint, obt = _in_proj(
        x, full["w_in"], small["b_gate"], small["conv_w"], full["w_conv_out"], kv, full["w_xattn_out"])
    y_ssm, cm_all = _ssm_fwd(u, b_half, c_half, pw, small["ssm_d"])
    ysbt, glu, mb, xhat1, rstd1 = _mid_fwd(y_ssm, g, ya, yc, x, full["w_glu"], full["w_out"],
                                           small["ln1_g"], small["ln1_b"])
    (x1bt, hdn, dr2bt, dpre, dx1, loss, dl2g, dl2b, dbdn, dbup) = _mlp_fwd_bwd(
        xhat1, tgt, small["ln1_g"], small["ln1_b"], full["w_up"], small["b_up"], full["w_down"],
        small["b_down"], small["ln2_g"], small["ln2_b"])
    recv = {}
    recv["w_down"] = _weight_grad_scatter(dr2bt, hdn, "dw_down", tm=512, tn=2048, tt=1024)
    recv["w_up"] = _weight_grad_scatter(x1bt, dpre, "dw_up", tm=512, tn=2048, tt=1024)
    (dxp, dr1bt, dgp, dya, dyc, dglu, dyssm, dl1g, dl1b, dbg) = _mid_bwd(
        dx1, xhat1, rstd1, g, ya, yc, glu, y_ssm, small["ln1_g"], full["w_out"], full["w_glu"])
    recv["w_out"] = _weight_grad_scatter(dr1bt, mb, "dw_out", tm=512, tn=1024, tt=2048)
    recv["w_glu"] = _weight_grad_scatter(ysbt, dglu, "dw_glu", tm=512, tn=2048, tt=2048)
    du, db_half, dc_half, da8, dd = _ssm_bwd(u, dyssm, cm_all, b_half, c_half, pw, small["ssm_d"])
    dconv, dq, dcw8, dkv = _branch_bwd(dya, dyc, cin, q, kv, small["conv_w"], full["w_conv_out"], full["w_xattn_out"])
    recv["w_conv_out"] = _weight_grad_scatter(aint, dya, "dw_conv_out", tm=512, tn=1024, tt=2048)
    recv["w_xattn_out"] = _weight_grad_scatter(obt, dyc, "dw_xattn_out", tm=512, tn=1024, tt=2048)
    recv["w_kv"] = _weight_grad_scatter(dkv.T.astype(bf16), memb, "dw_kv", tm=512, tn=1024, tt=MEM_LEN)
    dx, dproj = _in_proj_bwd(dgp, dconv, du, dq, dxp, full["w_in"])
    recv["w_in"] = _weight_grad_scatter(xbt, dproj, "dw_in", tm=512, tn=IN_COLS // 2, tt=1024)

    dabar_r, dabar_i = _state_unlayout(jnp.sum(da8, axis=0))
    dbbar_r = _half_diag_blocks(db_half[:, :, :HALF_STATE], SSM_GROUP, SSM_STATE).transpose(0, 2, 1)
    dbbar_i = _half_diag_blocks(db_half[:, :, HALF_STATE:], SSM_GROUP, SSM_STATE).transpose(0, 2, 1)
    g_shape = (SSM_GROUPS, SSM_STATE)
    dlam_re, dlam_im, dlog_dt, db_re, db_im = disc_vjp(
        (dabar_r.reshape(g_shape), dabar_i.reshape(g_shape), dbbar_r, dbbar_i))
    dc_re = _half_diag_blocks(dc_half[:, :HALF_STATE, :], SSM_STATE, SSM_GROUP).transpose(0, 2, 1)
    dc_im = -_half_diag_blocks(dc_half[:, HALF_STATE:, :], SSM_STATE, SSM_GROUP).transpose(0, 2, 1)

    small_grads = {
        "b_gate": dbg, "conv_w": dcw8[0:3], "ssm_lam_re": dlam_re, "ssm_lam_im": dlam_im, "ssm_log_dt": dlog_dt,
        "ssm_b_re": db_re, "ssm_b_im": db_im, "ssm_c_re": dc_re, "ssm_c_im": dc_im, "ssm_d": dd,
        "ln1_g": dl1g, "ln1_b": dl1b, "b_up": dbup, "b_down": dbdn, "ln2_g": dl2g, "ln2_b": dl2b,
    }
    small_grads = {k: a.reshape(SMALL[k]) for k, a in small_grads.items()}
    return loss[0, 0], dx, recv, small_grads


def kernel(x, mem, w_in, b_gate, conv_w, w_conv_out, ssm_lam_re, ssm_lam_im, ssm_log_dt, ssm_b_re, ssm_b_im, ssm_c_re, ssm_c_im, ssm_d, w_glu, w_kv, w_xattn_out, w_out, ln1_g, ln1_b, w_up, b_up, w_down, b_down, ln2_g, ln2_b, loss_target, m_w_in, m_b_gate, m_conv_w, m_w_conv_out, m_ssm_lam_re, m_ssm_lam_im, m_ssm_log_dt, m_ssm_b_re, m_ssm_b_im, m_ssm_c_re, m_ssm_c_im, m_ssm_d, m_w_glu, m_w_kv, m_w_xattn_out, m_w_out, m_ln1_g, m_ln1_b, m_w_up, m_b_up, m_w_down, m_b_down, m_ln2_g, m_ln2_b, v_w_in, v_b_gate, v_conv_w, v_w_conv_out, v_ssm_lam_re, v_ssm_lam_im, v_ssm_log_dt, v_ssm_b_re, v_ssm_b_im, v_ssm_c_re, v_ssm_c_im, v_ssm_d, v_w_glu, v_w_kv, v_w_xattn_out, v_w_out, v_ln1_g, v_ln1_b, v_w_up, v_b_up, v_w_down, v_b_down, v_ln2_g, v_ln2_b):
    w = dict(w_in=w_in, b_gate=b_gate, conv_w=conv_w, w_conv_out=w_conv_out, ssm_lam_re=ssm_lam_re,
             ssm_lam_im=ssm_lam_im, ssm_log_dt=ssm_log_dt, ssm_b_re=ssm_b_re, ssm_b_im=ssm_b_im, ssm_c_re=ssm_c_re,
             ssm_c_im=ssm_c_im, ssm_d=ssm_d, w_glu=w_glu, w_kv=w_kv, w_xattn_out=w_xattn_out, w_out=w_out,
             ln1_g=ln1_g, ln1_b=ln1_b, w_up=w_up, b_up=b_up, w_down=w_down, b_down=b_down, ln2_g=ln2_g, ln2_b=ln2_b)
    m = dict(w_in=m_w_in, b_gate=m_b_gate, conv_w=m_conv_w, w_conv_out=m_w_conv_out, ssm_lam_re=m_ssm_lam_re,
             ssm_lam_im=m_ssm_lam_im, ssm_log_dt=m_ssm_log_dt, ssm_b_re=m_ssm_b_re, ssm_b_im=m_ssm_b_im,
             ssm_c_re=m_ssm_c_re, ssm_c_im=m_ssm_c_im, ssm_d=m_ssm_d, w_glu=m_w_glu, w_kv=m_w_kv,
             w_xattn_out=m_w_xattn_out, w_out=m_w_out, ln1_g=m_ln1_g, ln1_b=m_ln1_b, w_up=m_w_up, b_up=m_b_up,
             w_down=m_w_down, b_down=m_b_down, ln2_g=m_ln2_g, ln2_b=m_ln2_b)
    v = dict(w_in=v_w_in, b_gate=v_b_gate, conv_w=v_conv_w, w_conv_out=v_w_conv_out, ssm_lam_re=v_ssm_lam_re,
             ssm_lam_im=v_ssm_lam_im, ssm_log_dt=v_ssm_log_dt, ssm_b_re=v_ssm_b_re, ssm_b_im=v_ssm_b_im,
             ssm_c_re=v_ssm_c_re, ssm_c_im=v_ssm_c_im, ssm_d=v_ssm_d, w_glu=v_w_glu, w_kv=v_w_kv,
             w_xattn_out=v_w_xattn_out, w_out=v_w_out, ln1_g=v_ln1_g, ln1_b=v_ln1_b, w_up=v_w_up, b_up=v_b_up,
             w_down=v_w_down, b_down=v_b_down, ln2_g=v_ln2_g, ln2_b=v_ln2_b)
    out_shapes = {k: a.shape for k, a in w.items()}
    shard2d = lambda k, a: a.reshape((3, CONV_W // N_DEV) if k == "conv_w" else SMALL[k]) if k in SMALL else a[0]
    w, m, v = ({k: shard2d(k, a) for k, a in d.items()} for d in (w, m, v))

    shards = [w[k].T.astype(bf16) if k in COL_SHARDED else w[k].astype(bf16) for k in BIG]
    conv_pad = jnp.pad(w["conv_w"], ((0, 5), (0, LANES - CONV_W // N_DEV)))
    gathered = _all_gather(shards + [conv_pad], "gather_weights")
    full = {k: a.reshape(-1, a.shape[-1]) for k, a in zip(BIG, gathered[:-1])}
    conv_full = gathered[-1][:, :3, :CONV_W // N_DEV].transpose(1, 0, 2).reshape(3, CONV_W)
    small = {k: (conv_full if k == "conv_w" else w[k]) for k in SMALL}

    loss, dx, recv, small_grads = _local_step(x[0], mem[0], loss_target[0], full, small)

    grads, deltas, new_m, new_v = {}, {}, {}, {}
    for k in BIG:
        res = _adamw_update(w[k], m[k], v[k], recv[k], "adamw_" + k, transposed=k not in COL_SHARDED)
        grads[k], deltas[k], new_m[k], new_v[k] = res

    names = list(SMALL)
    small_parts = _all_gather([small_grads[k] for k in names], "gather_small_grads")
    widen = lambda k, a: jnp.tile(a, (1, N_DEV)) if k == "conv_w" else a
    res = _adamw_small([small[k] for k in names], [widen(k, m[k]) for k in names], [widen(k, v[k]) for k in names],
                       small_parts)
    dev = _slot(_mesh_place())
    for j, d in enumerate((grads, deltas, new_m, new_v)):
        for k, a in zip(names, res[j * len(names):(j + 1) * len(names)]):
            if k == "conv_w":
                a = lax.dynamic_slice_in_dim(a, dev * (CONV_W // N_DEV), CONV_W // N_DEV, axis=1)
            d[k] = a

    loss = lax.psum(loss, ("x", "y", "c"))
    outs = [loss, dx[None]]
    for d in (grads, deltas, new_m, new_v):
        outs += [d[k].reshape(out_shapes[k]) for k in WEIGHTS]
    return tuple(outs)
```

```python
import functools
import math

import jax
import jax.numpy as jnp
from jax import lax
from jax.experimental import pallas as pl
from jax.experimental.pallas import tpu as pltpu

f32 = jnp.float32
bf16 = jnp.bfloat16

D_MODEL = 1024
MEM_LEN = 256
GATE_COLS = 3 * D_MODEL
CONV_W = 512
SSM_W = 512
XATTN_W = 512
HEADS = 4
HEAD_DIM = 128
D_FF = 4096
IN_COLS = GATE_COLS + 3 * CONV_W + SSM_W + XATTN_W
SSM_GROUPS = 32
SSM_GROUP = 16
SSM_STATE = 64
N_STATE = SSM_GROUPS * SSM_STATE
ALPHA = 2.0 ** 0.25
LN_EPS = 1e-5
N_DEV = 8

ADAM_LR = 0.001
ADAM_B1 = 0.9
ADAM_B2 = 0.999
ADAM_EPS = 1e-08
ADAM_WD = 0.01
ADAM_STEP = 10

VMEM_LIMIT_V7X = 56 * 2 ** 20
SUBLANES = 8
LANES = 128

TOKEN_TILE = 256
SSM_BLOCK = 256
SSM_SEG = SSM_BLOCK // SUBLANES
LANE_CHUNK = 512
N_HALF = 2
HALF_W = SSM_W // N_HALF
HALF_STATE = N_STATE // N_HALF
HALF_COLS = 2 * HALF_STATE

NT = (((1,), (1,)), ((), ()))
TN = (((0,), (0,)), ((), ()))
NN = (((1,), (0,)), ((), ()))


def _dot(a, b, dims=NN):
    return lax.dot_general(a, b, dims, preferred_element_type=f32)


def _cparams(sem=None):
    return pltpu.CompilerParams(dimension_semantics=sem, vmem_limit_bytes=VMEM_LIMIT_V7X)


def _row_spec(tm, cols, rev_n=None):
    if rev_n is None:
        return pl.BlockSpec((tm, cols), lambda i: (i, 0))
    return pl.BlockSpec((tm, cols), lambda i: (rev_n - 1 - i, 0))


def _col_spec(rows, tm):
    return pl.BlockSpec((rows, tm), lambda i: (0, i))


def _const_spec(shape):
    nd = len(shape)
    return pl.BlockSpec(shape, lambda *_: (0,) * nd, pipeline_mode=pl.Buffered(1))


def _acc_spec(shape):
    nd = len(shape)
    return pl.BlockSpec(shape, lambda *_: (0,) * nd)


def _sds(shape, dtype):
    return jax.ShapeDtypeStruct(shape, dtype)


def _gelu(x):
    c = math.sqrt(2.0 / math.pi)
    return 0.5 * x * (1.0 + jnp.tanh(c * (x + 0.044715 * x * x * x)))


def _gelu_grad(x):
    c = math.sqrt(2.0 / math.pi)
    t = jnp.tanh(c * (x + 0.044715 * x * x * x))
    return 0.5 * (1.0 + t) + 0.5 * x * (1.0 - t * t) * c * (1.0 + 3.0 * 0.044715 * x * x)


def _colsum(a):
    return jnp.sum(a, axis=0, keepdims=True)


def _mesh_place():
    return lax.axis_index("x"), lax.axis_index("y"), lax.axis_index("c")


def _slot(p):
    return 4 * p[0] + 2 * p[1] + p[2]


def _other_devices(me):
    x, y, c = me
    flip = lambda v, d: 1 - v if d else v
    return [(flip(x, dx), flip(y, dy), flip(c, dc)) for dx in (0, 1) for dy in (0, 1) for dc in (0, 1)][1:]


def _all_gather(blocks, name):
    n = len(blocks)

    def body(*refs):
        ins, outs = refs[:n], refs[n:2 * n]
        send_sems, recv_sems, local_sems = refs[2 * n:]
        x, y, c = _mesh_place()
        me, sibling = (x, y, c), (x, y, 1 - c)
        chips = [(1 - x, y), (x, 1 - y), (1 - x, 1 - y)]

        def copy(a, k, block, to, src=None):
            rows = outs[a].at[_slot(block)]
            return pltpu.make_async_remote_copy(
                src_ref=rows if src is None else src, dst_ref=rows,
                send_sem=send_sems.at[a, k], recv_sem=recv_sems.at[a, k],
                device_id=to, device_id_type=pl.DeviceIdType.MESH)

        mine = [pltpu.make_async_copy(ins[a], outs[a].at[_slot(me)], local_sems.at[a]) for a in range(n)]
        for cp in mine:
            cp.start()
        first = []
        for a in range(n):
            first.append(copy(a, 0, me, sibling, src=ins[a]))
            first += [copy(a, 1 + j, me, (*chip, c), src=ins[a]) for j, chip in enumerate(chips)]
        for cp in first:
            cp.start()
        passed = []
        for a in range(n):
            for j, chip in enumerate(chips):
                copy(a, 1 + j, (*chip, c), me).wait_recv()
                fwd = copy(a, 4 + j, (*chip, c), sibling)
                fwd.start()
                passed.append(fwd)
        for a in range(n):
            copy(a, 0, sibling, me).wait_recv()
            for j, chip in enumerate(chips):
                copy(a, 4 + j, (*chip, 1 - c), me).wait_recv()
        for cp in first + passed:
            cp.wait_send()
        for cp in mine:
            cp.wait()

    any_spec = pl.BlockSpec(memory_space=pl.ANY)
    return pl.pallas_call(
        body, name=name,
        out_shape=[_sds((N_DEV,) + b.shape, b.dtype) for b in blocks],
        in_specs=[any_spec] * n, out_specs=[any_spec] * n,
        scratch_shapes=[pltpu.SemaphoreType.DMA((n, 7)), pltpu.SemaphoreType.DMA((n, 7)),
                        pltpu.SemaphoreType.DMA((n,))],
    )(*blocks)


def _side_gather_copies(ins, outs, send_sems, recv_sems, local_sems):
    me = _mesh_place()
    copies = []
    for a, (src, dst) in enumerate(zip(ins, outs)):
        copies.append(pltpu.make_async_copy(src, dst.at[_slot(me)], local_sems.at[a]))
        for k, peer in enumerate(_other_devices(me)):
            copies.append(pltpu.make_async_remote_copy(
                src_ref=src, dst_ref=dst.at[_slot(me)], send_sem=send_sems.at[a, k], recv_sem=recv_sems.at[a, k],
                device_id=peer, device_id_type=pl.DeviceIdType.MESH))
    return copies


def _side_gather_specs(blocks):
    n = len(blocks)
    any_spec = pl.BlockSpec(memory_space=pl.ANY)
    return ([any_spec] * n, [_sds((N_DEV,) + b.shape, b.dtype) for b in blocks],
            [pltpu.SemaphoreType.DMA((n, N_DEV - 1)), pltpu.SemaphoreType.DMA((n, N_DEV - 1)),
             pltpu.SemaphoreType.DMA((n,))])


def _kv_proj(mem, w_kv):
    def body(mem_ref, w_ref, kv_ref, memb_ref):
        mb = mem_ref[...].astype(bf16)
        memb_ref[...] = mb
        kv_ref[...] = _dot(mb, w_ref[...]).astype(bf16)

    return pl.pallas_call(
        body, name="kv_proj",
        out_shape=[_sds((MEM_LEN, 2 * XATTN_W), bf16), _sds((MEM_LEN, D_MODEL), bf16)],
        compiler_params=_cparams(),
    )(mem, w_kv)


def _attention_probs(qb, kv_ref, h):
    kh = kv_ref[:, h * HEAD_DIM:(h + 1) * HEAD_DIM]
    s = _dot(qb[:, h * HEAD_DIM:(h + 1) * HEAD_DIM], kh, NT) * (HEAD_DIM ** -0.5)
    e = jnp.exp(s - jnp.max(s, axis=-1, keepdims=True))
    return e / jnp.sum(e, axis=-1, keepdims=True)


def _in_proj(x, w_in_t, b_gate, conv_w, w_co_t, kv, w_xo_t, side_blocks):
    s_len = x.shape[0]
    tm = TOKEN_TILE
    n = s_len // tm
    ns = len(side_blocks)
    side_in_specs, side_shapes, side_sems = _side_gather_specs(side_blocks)

    def body(*refs):
        (x_ref, win_ref, bg_ref, cw_ref, wco_ref, kv_ref, wxo_ref) = refs[:7]
        side_ins = refs[7:7 + ns]
        (xbt_ref, g_ref, cin_ref, u_ref, q_ref, ya_ref, yc_ref, aint_ref, ot_ref) = refs[7 + ns:16 + ns]
        side_outs = refs[16 + ns:16 + 2 * ns]
        zs_ref = refs[16 + 2 * ns]
        side = _side_gather_copies(side_ins, side_outs, *refs[17 + 2 * ns:])
        i = pl.program_id(0)

        @pl.when(i == 0)
        def _():
            for cp in side:
                cp.start()

        xb = x_ref[...].astype(bf16)
        xbt_ref[...] = xb.T
        proj = _dot(xb, win_ref[...], NT)
        g_ref[...] = jax.nn.sigmoid(proj[:, :GATE_COLS] + bg_ref[...])
        cin = proj[:, GATE_COLS:GATE_COLS + 3 * CONV_W]
        cin_ref[...] = cin
        u_ref[...] = proj[:, GATE_COLS + 3 * CONV_W:GATE_COLS + 3 * CONV_W + SSM_W]
        qb = proj[:, IN_COLS - XATTN_W:].astype(bf16)
        q_ref[...] = qb

        cb, cc, ch = cin[:, :CONV_W], cin[:, CONV_W:2 * CONV_W], cin[:, 2 * CONV_W:]
        z = cc * ch

        @pl.when(i == 0)
        def _():
            zs_ref[0:8, :] = jnp.zeros((8, CONV_W), f32)

        zs_ref[8:8 + tm, :] = z
        z1 = zs_ref[pl.ds(7, tm), :]
        z2 = zs_ref[pl.ds(6, tm), :]
        cw = cw_ref[...]
        cz = cw[0:1] * z2 + cw[1:2] * z1 + cw[2:3] * z
        zs_ref[0:8, :] = zs_ref[tm:tm + 8, :]
        ain = (cb * cz).astype(bf16)
        aint_ref[...] = ain.T
        ya_ref[...] = _dot(ain, wco_ref[...], NT)

        outs = []
        for h in range(HEADS):
            p = _attention_probs(qb, kv_ref, h)
            vh = kv_ref[:, XATTN_W + h * HEAD_DIM:XATTN_W + (h + 1) * HEAD_DIM]
            outs.append(_dot(p.astype(bf16), vh))
        ob = jnp.concatenate(outs, axis=1).astype(bf16)
        ot_ref[...] = ob.T
        yc_ref[...] = _dot(ob, wxo_ref[...], NT)

        @pl.when(i == n - 1)
        def _():
            for cp in side:
                cp.wait()

    row_cols = [(GATE_COLS, f32), (3 * CONV_W, f32), (SSM_W, f32), (XATTN_W, bf16), (D_MODEL, f32), (D_MODEL, f32)]
    t_rows = [D_MODEL, CONV_W, XATTN_W]
    outs = pl.pallas_call(
        body, name="in_proj", grid=(n,),
        in_specs=[_row_spec(tm, D_MODEL), _const_spec((IN_COLS, D_MODEL)), _const_spec((1, GATE_COLS)),
                  _const_spec((3, CONV_W)), _const_spec((D_MODEL, CONV_W)), _const_spec((MEM_LEN, 2 * XATTN_W)),
                  _const_spec((D_MODEL, XATTN_W))] + side_in_specs,
        out_specs=([_col_spec(t_rows[0], tm)] + [_row_spec(tm, c) for c, _ in row_cols]
                   + [_col_spec(t_rows[1], tm), _col_spec(t_rows[2], tm)] + side_in_specs),
        out_shape=([_sds((t_rows[0], s_len), bf16)] + [_sds((s_len, c), dt) for c, dt in row_cols]
                   + [_sds((t_rows[1], s_len), bf16), _sds((t_rows[2], s_len), bf16)] + side_shapes),
        scratch_shapes=[pltpu.VMEM((tm + 8, CONV_W), f32)] + side_sems,
        compiler_params=_cparams(("arbitrary",)),
    )(x, w_in_t, b_gate, conv_w, w_co_t, kv, w_xo_t, *side_blocks)
    return outs[:9], outs[9:]


def _state_cols(chunk):
    half, off = divmod(chunk * LANE_CHUNK, HALF_STATE)
    lo = half * HALF_COLS + off
    return slice(lo, lo + LANE_CHUNK), slice(lo + HALF_STATE, lo + HALF_STATE + LANE_CHUNK)


def _half_cols(half):
    lo = half * HALF_COLS
    return slice(lo, lo + HALF_STATE), slice(lo + HALF_STATE, lo + HALF_COLS)


def _rows_to_segments(src_ref, stage_ref, dst_ref):
    nc = SSM_W // LANES
    for c in range(nc):
        stage_ref[c] = src_ref[:, c * LANES:(c + 1) * LANES]
    for c in range(nc):
        for k in range(SSM_SEG):
            dst_ref[k * SUBLANES:(k + 1) * SUBLANES, c * LANES:(c + 1) * LANES] = (
                stage_ref[c, pl.ds(k, SUBLANES, stride=SSM_SEG), :])


def _rows_from_segments(src_ref, stage_ref, dst_ref):
    nc = SSM_W // LANES
    for c in range(nc):
        for k in range(SSM_SEG):
            stage_ref[c, pl.ds(k, SUBLANES, stride=SSM_SEG), :] = (
                src_ref[k * SUBLANES:(k + 1) * SUBLANES, c * LANES:(c + 1) * LANES])
    for c in range(nc):
        dst_ref[:, c * LANES:(c + 1) * LANES] = stage_ref[c]


def _ssm_scan(s_ref, pw_ref, init_ref, reverse):
    for chunk in range(N_STATE // LANE_CHUNK):
        re, im = _state_cols(chunk)
        ar = jnp.broadcast_to(pw_ref[0:1, re], (SUBLANES, LANE_CHUNK))
        ai = jnp.broadcast_to(pw_ref[0:1, im], (SUBLANES, LANE_CHUNK))
        if reverse:
            ai = -ai

        def step(j, carry, re=re, im=im, ar=ar, ai=ai):
            sr, si = carry
            k = (SSM_SEG - 1 - j) if reverse else j
            r0 = pl.multiple_of(k * SUBLANES, SUBLANES)
            nr = ar * sr - ai * si + s_ref[pl.ds(r0, SUBLANES), re]
            ni = ar * si + ai * sr + s_ref[pl.ds(r0, SUBLANES), im]
            s_ref[pl.ds(r0, SUBLANES), re] = nr
            s_ref[pl.ds(r0, SUBLANES), im] = ni
            return nr, ni

        if init_ref is None:
            init = (jnp.zeros((SUBLANES, LANE_CHUNK), f32),) * 2
        else:
            init = (init_ref[:, re], init_ref[:, im])
        lax.fori_loop(0, SSM_SEG, step, init, unroll=4)


def _ssm_add_carry(s_ref, pw_ref, cm_ref, reverse):
    for chunk in range(N_STATE // LANE_CHUNK):
        re, im = _state_cols(chunk)
        cr, ci = cm_ref[:, re], cm_ref[:, im]
        for k in range(SSM_SEG):
            pk = (SSM_SEG - 1 - k) if reverse else k
            pr = pw_ref[pk:pk + 1, re]
            pi = pw_ref[pk:pk + 1, im]
            if reverse:
                pi = -pi
            rows = slice(k * SUBLANES, (k + 1) * SUBLANES)
            s_ref[rows, re] = s_ref[rows, re] + (pr * cr - pi * ci)
            s_ref[rows, im] = s_ref[rows, im] + (pr * ci + pi * cr)


def _ssm_carries(first_row, s_ref, pw_ref, carry_ref, cm_ref, reverse):
    order = range(SUBLANES - 1, -1, -1) if reverse else range(SUBLANES)
    for half in range(N_HALF):
        re, im = _half_cols(half)
        a_r, a_i = pw_ref[SSM_SEG - 1:SSM_SEG, re], pw_ref[SSM_SEG - 1:SSM_SEG, im]
        if reverse:
            a_i = -a_i
        cr, ci = carry_ref[0:1, re], carry_ref[0:1, im]
        for seg in order:
            cm_ref[seg:seg + 1, re] = cr
            cm_ref[seg:seg + 1, im] = ci
            er = s_ref[first_row + seg:first_row + seg + 1, re]
            ei = s_ref[first_row + seg:first_row + seg + 1, im]
            cr, ci = a_r * cr - a_i * ci + er, a_r * ci + a_i * cr + ei
        carry_ref[0:1, re] = cr
        carry_ref[0:1, im] = ci


def _ssm_fwd(u, b_half, c_half, pw, d_skip, side_blocks):
    s_len = u.shape[0]
    tb = SSM_BLOCK
    n = s_len // tb
    ns = len(side_blocks)
    side_in_specs, side_shapes, side_sems = _side_gather_specs(side_blocks)

    def body(*refs):
        u_ref, b_ref, c_ref, pw_ref, d_ref = refs[:5]
        side_ins = refs[5:5 + ns]
        y_ref, cm_ref = refs[5 + ns:7 + ns]
        side_outs = refs[7 + ns:7 + 2 * ns]
        s_ref, carry_ref, up_ref, yp_ref, stage_ref = refs[7 + 2 * ns:12 + 2 * ns]
        side = _side_gather_copies(side_ins, side_outs, *refs[12 + 2 * ns:])
        i = pl.program_id(0)

        @pl.when(i == 0)
        def _():
            carry_ref[...] = jnp.zeros_like(carry_ref)
            for cp in side:
                cp.start()

        _rows_to_segments(u_ref, stage_ref, up_ref)
        u = up_ref[...]
        ub = u.astype(bf16)
        for half in range(N_HALF):
            s_ref[:, half * HALF_COLS:(half + 1) * HALF_COLS] = _dot(ub[:, half * HALF_W:(half + 1) * HALF_W], b_ref[half])
        _ssm_scan(s_ref, pw_ref, None, reverse=False)
        _ssm_carries(tb - SUBLANES, s_ref, pw_ref, carry_ref, cm_ref, reverse=False)
        _ssm_add_carry(s_ref, pw_ref, cm_ref, reverse=False)
        for half in range(N_HALF):
            cols = slice(half * HALF_W, (half + 1) * HALF_W)
            sb = s_ref[:, half * HALF_COLS:(half + 1) * HALF_COLS].astype(bf16)
            yp_ref[:, cols] = _dot(sb, c_ref[half]) + d_ref[:, cols] * u[:, cols]
        _rows_from_segments(yp_ref, stage_ref, y_ref)

        @pl.when(i == n - 1)
        def _():
            for cp in side:
                cp.wait()

    outs = pl.pallas_call(
        body, name="ssm_fwd", grid=(n,),
        in_specs=[_row_spec(tb, SSM_W), _const_spec((N_HALF, HALF_W, HALF_COLS)), _const_spec((N_HALF, HALF_COLS, HALF_W)),
                  _const_spec((SSM_SEG, 2 * N_STATE)), _const_spec((1, SSM_W))] + side_in_specs,
        out_specs=[_row_spec(tb, SSM_W), _row_spec(SUBLANES, 2 * N_STATE)] + side_in_specs,
        out_shape=[_sds((s_len, SSM_W), f32), _sds((n * SUBLANES, 2 * N_STATE), f32)] + side_shapes,
        scratch_shapes=[pltpu.VMEM((tb, 2 * N_STATE), f32), pltpu.VMEM((SUBLANES, 2 * N_STATE), f32),
                        pltpu.VMEM((tb, SSM_W), f32), pltpu.VMEM((tb, SSM_W), f32),
                        pltpu.VMEM((SSM_W // LANES, tb, LANES), f32)] + side_sems,
        compiler_params=_cparams(("arbitrary",)),
    )(u, b_half, c_half, pw, d_skip, *side_blocks)
    return outs[0], outs[1], outs[2:]


def _layer_norm_fwd(r, g, b):
    mu = jnp.mean(r, axis=-1, keepdims=True)
    var = jnp.mean(jnp.square(r - mu), axis=-1, keepdims=True)
    rstd = lax.rsqrt(var + LN_EPS)
    xhat = (r - mu) * rstd
    return xhat, rstd, xhat * g + b


def _layer_norm_bwd(dy, xhat, rstd, g):
    dxh = dy * g
    m1 = jnp.mean(dxh, axis=-1, keepdims=True)
    m2 = jnp.mean(dxh * xhat, axis=-1, keepdims=True)
    return rstd * (dxh - m1 - xhat * m2)


def _mid_fwd(y_ssm, g, ya, yc, x, w_glu_t, w_out, ln1_g, ln1_b):
    s_len = x.shape[0]
    tm = TOKEN_TILE
    n = s_len // tm

    def body(ys_ref, g_ref, ya_ref, yc_ref, x_ref, wglu_ref, wout_ref, lg_ref, lb_ref,
             ysbt_ref, glu_ref, mb_ref, xhat_ref, rstd_ref):
        ysb = _gelu(ys_ref[...]).astype(bf16)
        ysbt_ref[...] = ysb.T
        glu = _dot(ysb, wglu_ref[...], NT)
        glu_ref[...] = glu
        yb = glu[:, :D_MODEL] * jax.nn.sigmoid(glu[:, D_MODEL:])
        gt = g_ref[...]
        merged = (gt[:, :D_MODEL] * ya_ref[...] + gt[:, D_MODEL:2 * D_MODEL] * yb
                  + gt[:, 2 * D_MODEL:] * yc_ref[...])
        mb = merged.astype(bf16)
        mb_ref[...] = mb
        r1 = ALPHA * x_ref[...] + _dot(mb, wout_ref[...])
        xhat, rstd, _ = _layer_norm_fwd(r1, lg_ref[...], lb_ref[...])
        xhat_ref[...] = xhat
        rstd_ref[...] = rstd

    row_cols = [(2 * D_MODEL, f32), (D_MODEL, bf16), (D_MODEL, f32), (1, f32)]
    return pl.pallas_call(
        body, name="mid_fwd", grid=(n,),
        in_specs=[_row_spec(tm, SSM_W), _row_spec(tm, GATE_COLS), _row_spec(tm, D_MODEL), _row_spec(tm, D_MODEL),
                  _row_spec(tm, D_MODEL), _const_spec((2 * D_MODEL, SSM_W)), _const_spec((D_MODEL, D_MODEL)),
                  _const_spec((1, D_MODEL)), _const_spec((1, D_MODEL))],
        out_specs=[_col_spec(SSM_W, tm)] + [_row_spec(tm, c) for c, _ in row_cols],
        out_shape=[_sds((SSM_W, s_len), bf16)] + [_sds((s_len, c), dt) for c, dt in row_cols],
        compiler_params=_cparams(("parallel",)),
    )(y_ssm, g, ya, yc, x, w_glu_t, w_out, ln1_g, ln1_b)


def _mlp_fwd_bwd(xhat1, tgt, ln1_g, ln1_b, w_up_t, b_up, w_down, b_down, ln2_g, ln2_b):
    s_len = xhat1.shape[0]
    tm = TOKEN_TILE
    n = s_len // tm
    fc = 1024
    nfc = D_FF // fc

    def body(xh_ref, t_ref, l1g_ref, l1b_ref, wup_ref, bup_ref, wdn_ref, bdn_ref, l2g_ref, l2b_ref,
             x1bt_ref, hdn_ref, dr2bt_ref, dpre_ref, dx1_ref,
             loss_ref, dl2g_ref, dl2b_ref, dbdn_ref, dbup_ref, rl_ref):
        i = pl.program_id(0)

        @pl.when(i == 0)
        def _():
            loss_ref[...] = jnp.zeros_like(loss_ref)
            dl2g_ref[...] = jnp.zeros_like(dl2g_ref)
            dl2b_ref[...] = jnp.zeros_like(dl2b_ref)
            dbdn_ref[...] = jnp.zeros_like(dbdn_ref)
            dbup_ref[...] = jnp.zeros_like(dbup_ref)

        x1 = xh_ref[...] * l1g_ref[...] + l1b_ref[...]
        x1b = x1.astype(bf16)
        x1bt_ref[...] = x1b.T
        acc = jnp.zeros((tm, D_MODEL), f32)
        for c in range(nfc):
            cols = slice(c * fc, (c + 1) * fc)
            pre = _dot(x1b, wup_ref[cols, :], NT) + bup_ref[:, cols]
            rl = jnp.maximum(pre, 0.0)
            rl_ref[:, cols] = rl
            hb = (rl * rl).astype(bf16)
            hdn_ref[:, cols] = hb
            acc = acc + _dot(hb, wdn_ref[cols, :])
        r2 = ALPHA * x1 + acc + bdn_ref[...]
        xhat2, rstd2, y = _layer_norm_fwd(r2, l2g_ref[...], l2b_ref[...])
        err = y - t_ref[...]
        loss_ref[...] += jnp.sum(jnp.sum(err * err, axis=1, keepdims=True), axis=0, keepdims=True) * (0.5 / D_MODEL)
        dy = err * (1.0 / D_MODEL)
        dl2g_ref[...] += _colsum(dy * xhat2)
        dl2b_ref[...] += _colsum(dy)
        dr2 = _layer_norm_bwd(dy, xhat2, rstd2, l2g_ref[...])
        dbdn_ref[...] += _colsum(dr2)
        dr2b = dr2.astype(bf16)
        dr2bt_ref[...] = dr2b.T
        dacc = jnp.zeros((tm, D_MODEL), f32)
        for c in range(nfc):
            cols = slice(c * fc, (c + 1) * fc)
            dh = _dot(dr2b, wdn_ref[cols, :], NT)
            dpre = dh * (2.0 * rl_ref[:, cols])
            dbup_ref[:, cols] += _colsum(dpre)
            dpb = dpre.astype(bf16)
            dpre_ref[:, cols] = dpb
            dacc = dacc + _dot(dpb, wup_ref[cols, :])
        dx1_ref[...] = ALPHA * dr2 + dacc

    acc_shapes = [(1, LANES), (1, D_MODEL), (1, D_MODEL), (1, D_MODEL), (1, D_FF)]
    return pl.pallas_call(
        body, name="mlp_fwd_bwd", grid=(n,),
        in_specs=[_row_spec(tm, D_MODEL), _row_spec(tm, D_MODEL), _const_spec((1, D_MODEL)), _const_spec((1, D_MODEL)),
                  _const_spec((D_FF, D_MODEL)), _const_spec((1, D_FF)), _const_spec((D_FF, D_MODEL)),
                  _const_spec((1, D_MODEL)), _const_spec((1, D_MODEL)), _const_spec((1, D_MODEL))],
        out_specs=([_col_spec(D_MODEL, tm), _row_spec(tm, D_FF), _col_spec(D_MODEL, tm), _row_spec(tm, D_FF),
                    _row_spec(tm, D_MODEL)] + [_acc_spec(s) for s in acc_shapes]),
        out_shape=([_sds((D_MODEL, s_len), bf16), _sds((s_len, D_FF), bf16), _sds((D_MODEL, s_len), bf16),
                    _sds((s_len, D_FF), bf16), _sds((s_len, D_MODEL), f32)] + [_sds(s, f32) for s in acc_shapes]),
        scratch_shapes=[pltpu.VMEM((tm, D_FF), f32)],
        compiler_params=_cparams(("arbitrary",)),
    )(xhat1, tgt, ln1_g, ln1_b, w_up_t, b_up, w_down, b_down, ln2_g, ln2_b)


def _mid_bwd(dx1, xhat1, rstd1, g, ya, yc, glu, y_ssm, ln1_g, w_out, w_glu_t):
    s_len = dx1.shape[0]
    tm = TOKEN_TILE
    n = s_len // tm

    def body(dx1_ref, xh_ref, rs_ref, g_ref, ya_ref, yc_ref, glu_ref, ys_ref, lg_ref, wout_ref, wglu_ref,
             dxp_ref, dr1bt_ref, dgp_ref, dya_ref, dyc_ref, dglu_ref, dyssm_ref,
             dl1g_ref, dl1b_ref, dbg_ref):
        i = pl.program_id(0)

        @pl.when(i == 0)
        def _():
            dl1g_ref[...] = jnp.zeros_like(dl1g_ref)
            dl1b_ref[...] = jnp.zeros_like(dl1b_ref)
            dbg_ref[...] = jnp.zeros_like(dbg_ref)

        dx1 = dx1_ref[...]
        xhat = xh_ref[...]
        dl1g_ref[...] += _colsum(dx1 * xhat)
        dl1b_ref[...] += _colsum(dx1)
        dr1 = _layer_norm_bwd(dx1, xhat, rs_ref[...], lg_ref[...])
        dxp_ref[...] = ALPHA * dr1
        dr1b = dr1.astype(bf16)
        dr1bt_ref[...] = dr1b.T
        dm = _dot(dr1b, wout_ref[...], NT)

        glu = glu_ref[...]
        ga, sb = glu[:, :D_MODEL], jax.nn.sigmoid(glu[:, D_MODEL:])
        yb = ga * sb
        gt = g_ref[...]
        branch = (ya_ref[...], yb, yc_ref[...])
        for j in range(3):
            cols = slice(j * D_MODEL, (j + 1) * D_MODEL)
            gj = gt[:, cols]
            dgp = dm * branch[j] * gj * (1.0 - gj)
            dbg_ref[:, cols] += _colsum(dgp)
            dgp_ref[:, cols] = dgp.astype(bf16)
        dya_ref[...] = (dm * gt[:, :D_MODEL]).astype(bf16)
        dyc_ref[...] = (dm * gt[:, 2 * D_MODEL:]).astype(bf16)
        dyb = dm * gt[:, D_MODEL:2 * D_MODEL]
        dga = (dyb * sb).astype(bf16)
        dgb = (dyb * ga * sb * (1.0 - sb)).astype(bf16)
        dglu_ref[:, :D_MODEL] = dga
        dglu_ref[:, D_MODEL:] = dgb
        dys = _dot(dga, wglu_ref[:D_MODEL, :]) + _dot(dgb, wglu_ref[D_MODEL:, :])
        dyssm_ref[...] = dys * _gelu_grad(ys_ref[...])

    row_cols = [(GATE_COLS, bf16), (D_MODEL, bf16), (D_MODEL, bf16), (2 * D_MODEL, bf16), (SSM_W, f32)]
    acc_shapes = [(1, D_MODEL), (1, D_MODEL), (1, GATE_COLS)]
    return pl.pallas_call(
        body, name="mid_bwd", grid=(n,),
        in_specs=[_row_spec(tm, D_MODEL), _row_spec(tm, D_MODEL), _row_spec(tm, 1), _row_spec(tm, GATE_COLS),
                  _row_spec(tm, D_MODEL), _row_spec(tm, D_MODEL), _row_spec(tm, 2 * D_MODEL), _row_spec(tm, SSM_W),
                  _const_spec((1, D_MODEL)), _const_spec((D_MODEL, D_MODEL)), _const_spec((2 * D_MODEL, SSM_W))],
        out_specs=([_row_spec(tm, D_MODEL), _col_spec(D_MODEL, tm)] + [_row_spec(tm, c) for c, _ in row_cols]
                   + [_acc_spec(s) for s in acc_shapes]),
        out_shape=([_sds((s_len, D_MODEL), f32), _sds((D_MODEL, s_len), bf16)]
                   + [_sds((s_len, c), dt) for c, dt in row_cols] + [_sds(s, f32) for s in acc_shapes]),
        compiler_params=_cparams(("arbitrary",)),
    )(dx1, xhat1, rstd1, g, ya, yc, glu, y_ssm, ln1_g, w_out, w_glu_t)


def _ssm_bwd(u, dy, cm_all, b_half, c_half, pw, d_skip):
    s_len = u.shape[0]
    tb = SSM_BLOCK
    n = s_len // tb

    def body(u_ref, dy_ref, cm_ref, b_ref, c_ref, pw_ref, d_ref,
             du_ref, db_hbm, dc_hbm, da_ref, dd_ref,
             s_ref, g_ref, gcarry_ref, gcm_ref, db_ref, dc_ref, up_ref, dyp_ref, dup_ref, stage_ref):
        i = pl.program_id(0)

        @pl.when(i == 0)
        def _():
            gcarry_ref[...] = jnp.zeros_like(gcarry_ref)
            db_ref[...] = jnp.zeros_like(db_ref)
            dc_ref[...] = jnp.zeros_like(dc_ref)
            da_ref[...] = jnp.zeros_like(da_ref)
            dd_ref[...] = jnp.zeros_like(dd_ref)

        _rows_to_segments(u_ref, stage_ref, up_ref)
        _rows_to_segments(dy_ref, stage_ref, dyp_ref)
        u = up_ref[...]
        ub = u.astype(bf16)
        dy = dyp_ref[...]
        dyb = dy.astype(bf16)
        dd_ref[...] += _colsum(dy * u)

        for half in range(N_HALF):
            s_ref[:, half * HALF_COLS:(half + 1) * HALF_COLS] = _dot(ub[:, half * HALF_W:(half + 1) * HALF_W], b_ref[half])
        _ssm_scan(s_ref, pw_ref, cm_ref, reverse=False)

        for half in range(N_HALF):
            g_ref[:, half * HALF_COLS:(half + 1) * HALF_COLS] = _dot(dyb[:, half * HALF_W:(half + 1) * HALF_W], c_ref[half], NT)
        _ssm_scan(g_ref, pw_ref, None, reverse=True)
        _ssm_carries(0, g_ref, pw_ref, gcarry_ref, gcm_ref, reverse=True)
        _ssm_add_carry(g_ref, pw_ref, gcm_ref, reverse=True)

        for half in range(N_HALF):
            cols = slice(half * HALF_W, (half + 1) * HALF_W)
            scols = slice(half * HALF_COLS, (half + 1) * HALF_COLS)
            gb = g_ref[:, scols].astype(bf16)
            dup_ref[:, cols] = _dot(gb, b_ref[half], NT) + d_ref[:, cols] * dy[:, cols]
            db_ref[half] += _dot(ub[:, cols], gb, TN)
            dc_ref[half] += _dot(s_ref[:, scols].astype(bf16), dyb[:, cols], TN)
        _rows_from_segments(dup_ref, stage_ref, du_ref)

        for chunk in range(N_STATE // LANE_CHUNK):
            re, im = _state_cols(chunk)
            acc_r = da_ref[:, re]
            acc_i = da_ref[:, im]
            for k in range(SSM_SEG):
                rows = slice(k * SUBLANES, (k + 1) * SUBLANES)
                if k == 0:
                    pr, pi = cm_ref[:, re], cm_ref[:, im]
                else:
                    prev = slice((k - 1) * SUBLANES, k * SUBLANES)
                    pr, pi = s_ref[prev, re], s_ref[prev, im]
                gr, gi = g_ref[rows, re], g_ref[rows, im]
                acc_r = acc_r + (gr * pr + gi * pi)
                acc_i = acc_i + (gi * pr - gr * pi)
            da_ref[:, re] = acc_r
            da_ref[:, im] = acc_i

        @pl.when(i == n - 1)
        def _():
            pltpu.sync_copy(db_ref, db_hbm)
            pltpu.sync_copy(dc_ref, dc_hbm)

    rev = functools.partial(_row_spec, rev_n=n)
    any_spec = pl.BlockSpec(memory_space=pl.ANY)
    state_rows = pltpu.VMEM((tb, 2 * N_STATE), f32)
    seg_rows = pltpu.VMEM((SUBLANES, 2 * N_STATE), f32)
    tok_rows = pltpu.VMEM((tb, SSM_W), f32)
    return pl.pallas_call(
        body, name="ssm_bwd", grid=(n,),
        in_specs=[rev(tb, SSM_W), rev(tb, SSM_W), rev(SUBLANES, 2 * N_STATE),
                  _const_spec((N_HALF, HALF_W, HALF_COLS)), _const_spec((N_HALF, HALF_COLS, HALF_W)),
                  _const_spec((SSM_SEG, 2 * N_STATE)), _const_spec((1, SSM_W))],
        out_specs=[rev(tb, SSM_W), any_spec, any_spec, _acc_spec((SUBLANES, 2 * N_STATE)), _acc_spec((1, SSM_W))],
        out_shape=[_sds((s_len, SSM_W), f32), _sds((N_HALF, HALF_W, HALF_COLS), f32),
                   _sds((N_HALF, HALF_COLS, HALF_W), f32), _sds((SUBLANES, 2 * N_STATE), f32), _sds((1, SSM_W), f32)],
        scratch_shapes=[state_rows, state_rows, seg_rows, seg_rows,
                        pltpu.VMEM((N_HALF, HALF_W, HALF_COLS), f32), pltpu.VMEM((N_HALF, HALF_COLS, HALF_W), f32),
                        tok_rows, tok_rows, tok_rows, pltpu.VMEM((SSM_W // LANES, tb, LANES), f32)],
        compiler_params=_cparams(("arbitrary",)),
    )(u, dy, cm_all, b_half, c_half, pw, d_skip)


def _branch_bwd(dya, dyc, cin, q, kv, conv_w, w_co_t, w_xo_t):
    s_len = dya.shape[0]
    tm = TOKEN_TILE
    n = s_len // tm
    halo_blocks = tm // 8

    def body(dya_ref, dyc_ref, cin_ref, cprev_ref, q_ref, kv_ref, cw_ref, wco_ref, wxo_ref,
             dconv_ref, dq_ref, dcw_ref, dkv_ref, zs_ref, dczs_ref):
        i = pl.program_id(0)
        tile = n - 1 - i

        @pl.when(i == 0)
        def _():
            dcw_ref[...] = jnp.zeros_like(dcw_ref)
            dkv_ref[...] = jnp.zeros_like(dkv_ref)
            dczs_ref[tm:tm + 8, :] = jnp.zeros((8, CONV_W), f32)

        cin = cin_ref[...]
        cb, cc, ch = cin[:, :CONV_W], cin[:, CONV_W:2 * CONV_W], cin[:, 2 * CONV_W:]
        z = cc * ch
        cprev = cprev_ref[...]
        zprev = cprev[:, CONV_W:2 * CONV_W] * cprev[:, 2 * CONV_W:]
        zs_ref[0:8, :] = jnp.where(tile == 0, 0.0, zprev)
        zs_ref[8:8 + tm, :] = z
        z1 = zs_ref[pl.ds(7, tm), :]
        z2 = zs_ref[pl.ds(6, tm), :]
        cw = cw_ref[...]
        cz = cw[0:1] * z2 + cw[1:2] * z1 + cw[2:3] * z

        dain = _dot(dya_ref[...], wco_ref[...])
        dcb = dain * cz
        dcz = dain * cb
        dczs_ref[0:tm, :] = dcz
        dcz1 = dczs_ref[pl.ds(1, tm), :]
        dcz2 = dczs_ref[pl.ds(2, tm), :]
        dz = cw[2:3] * dcz + cw[1:2] * dcz1 + cw[0:1] * dcz2
        dczs_ref[tm:tm + 8, :] = dczs_ref[0:8, :]
        dcw_ref[0:1, :] += _colsum(dcz * z2)
        dcw_ref[1:2, :] += _colsum(dcz * z1)
        dcw_ref[2:3, :] += _colsum(dcz * z)
        dconv_ref[:, :CONV_W] = dcb.astype(bf16)
        dconv_ref[:, CONV_W:2 * CONV_W] = (dz * ch).astype(bf16)
        dconv_ref[:, 2 * CONV_W:] = (dz * cc).astype(bf16)

        qb = q_ref[...]
        do = _dot(dyc_ref[...], wxo_ref[...])
        for h in range(HEADS):
            hc = slice(h * HEAD_DIM, (h + 1) * HEAD_DIM)
            vc = slice(XATTN_W + h * HEAD_DIM, XATTN_W + (h + 1) * HEAD_DIM)
            p = _attention_probs(qb, kv_ref, h)
            dob = do[:, hc].astype(bf16)
            dp = _dot(dob, kv_ref[:, vc], NT)
            dkv_ref[:, vc] += _dot(p.astype(bf16), dob, TN)
            ds = p * (dp - jnp.sum(dp * p, axis=-1, keepdims=True)) * (HEAD_DIM ** -0.5)
            dsb = ds.astype(bf16)
            dq_ref[:, hc] = _dot(dsb, kv_ref[:, hc]).astype(bf16)
            dkv_ref[:, hc] += _dot(dsb, qb[:, hc], TN)

    rev = functools.partial(_row_spec, rev_n=n)
    prev_spec = pl.BlockSpec((8, 3 * CONV_W), lambda i: (jnp.maximum((n - 1 - i) * halo_blocks - 1, 0), 0))
    return pl.pallas_call(
        body, name="branch_bwd", grid=(n,),
        in_specs=[rev(tm, D_MODEL), rev(tm, D_MODEL), rev(tm, 3 * CONV_W), prev_spec, rev(tm, XATTN_W),
                  _const_spec((MEM_LEN, 2 * XATTN_W)), _const_spec((3, CONV_W)), _const_spec((D_MODEL, CONV_W)),
                  _const_spec((D_MODEL, XATTN_W))],
        out_specs=[rev(tm, 3 * CONV_W), rev(tm, XATTN_W), _acc_spec((8, CONV_W)), _acc_spec((MEM_LEN, 2 * XATTN_W))],
        out_shape=[_sds((s_len, 3 * CONV_W), bf16), _sds((s_len, XATTN_W), bf16), _sds((8, CONV_W), f32),
                   _sds((MEM_LEN, 2 * XATTN_W), f32)],
        scratch_shapes=[pltpu.VMEM((tm + 8, CONV_W), f32), pltpu.VMEM((tm + 8, CONV_W), f32)],
        compiler_params=_cparams(("arbitrary",)),
    )(dya, dyc, cin, cin, q, kv, conv_w, w_co_t, w_xo_t)


def _in_proj_bwd(dgp, dconv, du, dq, dxp, w_in_t):
    s_len = dgp.shape[0]
    tm = TOKEN_TILE
    n = s_len // tm

    def body(dgp_ref, dconv_ref, du_ref, dq_ref, dxp_ref, win_ref, dx_ref, dproj_ref):
        dproj = jnp.concatenate([dgp_ref[...], dconv_ref[...], du_ref[...].astype(bf16), dq_ref[...]], axis=1)
        dproj_ref[...] = dproj
        dx_ref[...] = dxp_ref[...] + _dot(dproj, win_ref[...])

    return pl.pallas_call(
        body, name="in_proj_bwd", grid=(n,),
        in_specs=[_row_spec(tm, GATE_COLS), _row_spec(tm, 3 * CONV_W), _row_spec(tm, SSM_W), _row_spec(tm, XATTN_W),
                  _row_spec(tm, D_MODEL), _const_spec((IN_COLS, D_MODEL))],
        out_specs=[_row_spec(tm, D_MODEL), _row_spec(tm, IN_COLS)],
        out_shape=[_sds((s_len, D_MODEL), f32), _sds((s_len, IN_COLS), bf16)],
        compiler_params=_cparams(("parallel",)),
    )(dgp, dconv, du, dq, dxp, w_in_t)


N_CHIP = 4
CHIP_STEPS = [(1, 1), (1, 0), (0, 1), (0, 0)]


def _flip(v, d):
    return 1 - v if d else v


def _chip_order():
    x, y, _ = _mesh_place()
    return jnp.stack([2 * _flip(x, dx) + _flip(y, dy) for dx, dy in CHIP_STEPS]).astype(jnp.int32)


def _weight_grad_scatter(a_t, b, name, tm, tt):
    m, s_len = a_t.shape
    n_cols = b.shape[1]
    w = n_cols // N_DEV
    tn = 2 * w
    tm, tt = min(tm, m), min(tt, s_len)
    nm, nt = m // tm, s_len // tt
    assert m % tm == 0 and s_len % tt == 0

    def body(order_ref, a_ref, b_ref, recv_ref, acc_ref, send_ref, sib_ref,
             d2d_send, d2d_recv, ici_send, ici_recv, local_sem):
        del order_ref
        q, im, t = pl.program_id(0), pl.program_id(1), pl.program_id(2)
        x, y, c = _mesh_place()
        mesh_id = pl.DeviceIdType.MESH

        @pl.when(t == 0)
        def _():
            acc_ref[...] = jnp.zeros_like(acc_ref)

        acc_ref[...] += _dot(a_ref[...], b_ref[...])

        @pl.when(t == nt - 1)
        def _():
            rows = pl.ds(pl.multiple_of(im * tm, tm), tm)
            to_sibling = pltpu.make_async_remote_copy(
                src_ref=send_ref.at[q, 0, rows, :], dst_ref=sib_ref.at[q, rows, :],
                send_sem=d2d_send.at[q], recv_sem=d2d_recv.at[q, im],
                device_id=(x, y, 1 - c), device_id_type=mesh_id)
            for core in (0, 1):
                @pl.when(c == core)
                def _(core=core):
                    other = 1 - core
                    send_ref[q, 0, rows, :] = acc_ref[:, other * w:(other + 1) * w].astype(bf16)
                    to_sibling.start()
                    to_sibling.wait_recv()
                    both = acc_ref[:, core * w:(core + 1) * w] + sib_ref[q, rows, :].astype(f32)
                    send_ref[q, 1, rows, :] = both.astype(bf16)

            for step, (dx, dy) in enumerate(CHIP_STEPS):
                @pl.when(q == step)
                def _(step=step, dx=dx, dy=dy):
                    src, dst = send_ref.at[step, 1, rows, :], recv_ref.at[step, rows, :]
                    if dx or dy:
                        pltpu.make_async_remote_copy(
                            src_ref=src, dst_ref=dst, send_sem=ici_send.at[step], recv_sem=ici_recv.at[step],
                            device_id=(_flip(x, dx), _flip(y, dy), c), device_id_type=mesh_id).start()
                    else:
                        pltpu.make_async_copy(src, dst, local_sem).start()

        @pl.when((q == N_CHIP - 1) & (im == nm - 1) & (t == nt - 1))
        def _():
            for step, (dx, dy) in enumerate(CHIP_STEPS):
                pltpu.make_async_remote_copy(
                    src_ref=send_ref.at[step, 0], dst_ref=sib_ref.at[step],
                    send_sem=d2d_send.at[step], recv_sem=d2d_recv.at[step, 0],
                    device_id=(x, y, 1 - c), device_id_type=mesh_id).wait_send()
                src, dst = send_ref.at[step, 1], recv_ref.at[step]
                if dx or dy:
                    pltpu.make_async_remote_copy(
                        src_ref=src, dst_ref=dst, send_sem=ici_send.at[step], recv_sem=ici_recv.at[step],
                        device_id=(_flip(x, dx), _flip(y, dy), c), device_id_type=mesh_id).wait()
                else:
                    pltpu.make_async_copy(src, dst, local_sem).wait()

    grid_spec = pltpu.PrefetchScalarGridSpec(
        num_scalar_prefetch=1, grid=(N_CHIP, nm, nt),
        in_specs=[pl.BlockSpec((tm, tt), lambda q, im, t, order: (im, t)),
                  pl.BlockSpec((tt, tn), lambda q, im, t, order: (t, order[q]))],
        out_specs=pl.BlockSpec(memory_space=pl.ANY),
        scratch_shapes=[pltpu.VMEM((tm, tn), f32), pltpu.VMEM((N_CHIP, 2, m, w), bf16), pltpu.VMEM((N_CHIP, m, w), bf16),
                        pltpu.SemaphoreType.DMA((N_CHIP,)), pltpu.SemaphoreType.DMA((N_CHIP, nm)),
                        pltpu.SemaphoreType.DMA((N_CHIP - 1,)), pltpu.SemaphoreType.DMA((N_CHIP - 1,)),
                        pltpu.SemaphoreType.DMA])
    return pl.pallas_call(
        body, name=name, grid_spec=grid_spec,
        out_shape=_sds((N_CHIP, m, w), bf16),
        compiler_params=_cparams(("arbitrary", "arbitrary", "arbitrary")),
    )(_chip_order(), a_t, b)


def _adamw(w, g, m, v):
    m = ADAM_B1 * m + (1.0 - ADAM_B1) * g
    v = ADAM_B2 * v + (1.0 - ADAM_B2) * jnp.square(g)
    m_hat = m / (1.0 - ADAM_B1 ** ADAM_STEP)
    v_hat = v / (1.0 - ADAM_B2 ** ADAM_STEP)
    delta = -ADAM_LR * (m_hat / (jnp.sqrt(v_hat) + ADAM_EPS) + ADAM_WD * w)
    return delta, m, v


def _sum_parts(p_ref):
    g = p_ref[0].astype(f32)
    for j in range(1, p_ref.shape[0]):
        g = g + p_ref[j].astype(f32)
    return g


def _adamw_update(w, m, v, parts, name, transposed):
    rows, cols = w.shape
    n_parts = parts.shape[0]
    if transposed:
        tr = LANES
        p_spec = pl.BlockSpec((n_parts, cols, tr), lambda i: (0, 0, i))
    else:
        tr = next(t for t in (256, 128, 64, 32, 16, 8) if rows % t == 0)
        p_spec = pl.BlockSpec((n_parts, tr, cols), lambda i: (0, i, 0))
    spec = pl.BlockSpec((tr, cols), lambda i: (i, 0))

    def body(w_ref, p_ref, m_ref, v_ref, g_ref, d_ref, nm_ref, nv_ref):
        g = _sum_parts(p_ref)
        if transposed:
            g = g.T
        g_ref[...] = g
        d_ref[...], nm_ref[...], nv_ref[...] = _adamw(w_ref[...], g, m_ref[...], v_ref[...])

    return pl.pallas_call(
        body, name=name, grid=(rows // tr,),
        in_specs=[spec, p_spec, spec, spec], out_specs=[spec] * 4,
        out_shape=[_sds((rows, cols), f32)] * 4,
        compiler_params=_cparams(("parallel",)),
    )(w, parts, m, v)


SMALL_GROUPS = [
    (["b_gate", "ln1_g", "ln1_b", "b_up", "b_down", "ln2_g", "ln2_b", "ssm_d"], 1),
    (["ssm_lam_re", "ssm_lam_im", "ssm_c_re", "ssm_c_im"], 0),
    (["ssm_b_re", "ssm_b_im"], 0),
    (["conv_w"], 0),
    (["ssm_log_dt"], 0),
]


def _adamw_small(ws, ms, vs, group_parts):
    names = [k for group, _ in SMALL_GROUPS for k in group]
    n = len(names)

    def body(*refs):
        w_refs, m_refs, v_refs = (dict(zip(names, refs[j * n:(j + 1) * n])) for j in range(3))
        p_refs = refs[3 * n:3 * n + len(SMALL_GROUPS)]
        out_refs = [dict(zip(names, refs[3 * n + len(SMALL_GROUPS) + j * n:][:n])) for j in range(4)]
        for (group, axis), p_ref in zip(SMALL_GROUPS, p_refs):
            total = _sum_parts(p_ref)
            off = 0
            for k in group:
                size = SMALL[k][axis]
                g = total[:, off:off + size] if axis == 1 else total[off:off + size, :]
                off += size
                d, nm, nv = _adamw(w_refs[k][...], g, m_refs[k][...], v_refs[k][...])
                for j, val in enumerate((g, d, nm, nv)):
                    out_refs[j][k][...] = val

    res = pl.pallas_call(
        body, name="adamw_small",
        out_shape=[_sds(SMALL[k], f32) for _ in range(4) for k in names],
        compiler_params=_cparams(),
    )(*[ws[k] for k in names], *[ms[k] for k in names], *[vs[k] for k in names], *group_parts)
    return [dict(zip(names, res[j * n:(j + 1) * n])) for j in range(4)]


def _ssm_discretize(lam_re, lam_im, log_dt, b_re, b_im):
    dt = jnp.exp(log_dt)[:, None]
    mag = jnp.exp(lam_re * dt)
    abar_r = mag * jnp.cos(lam_im * dt)
    abar_i = mag * jnp.sin(lam_im * dt)
    den = lam_re * lam_re + lam_im * lam_im
    nr = abar_r - 1.0
    ni = abar_i
    kr = (nr * lam_re + ni * lam_im) / den
    ki = (ni * lam_re - nr * lam_im) / den
    bbar_r = kr[..., None] * b_re - ki[..., None] * b_im
    bbar_i = kr[..., None] * b_im + ki[..., None] * b_re
    return abar_r, abar_i, bbar_r, bbar_i


def _state_layout(re, im):
    parts = []
    for half in range(N_HALF):
        cols = slice(half * HALF_STATE, (half + 1) * HALF_STATE)
        parts += [re[..., cols], im[..., cols]]
    return jnp.concatenate(parts, axis=-1)


def _state_unlayout(a):
    re = jnp.concatenate([a[..., _half_cols(h)[0]] for h in range(N_HALF)], axis=-1)
    im = jnp.concatenate([a[..., _half_cols(h)[1]] for h in range(N_HALF)], axis=-1)
    return re, im


def _abar_powers(abar_r, abar_i):
    pr, pi = abar_r.reshape(1, N_STATE), abar_i.reshape(1, N_STATE)
    while pr.shape[0] < SSM_SEG:
        tr, ti = pr[-1:], pi[-1:]
        pr, pi = (jnp.concatenate([pr, pr * tr - pi * ti], axis=0), jnp.concatenate([pi, pr * ti + pi * tr], axis=0))
    return _state_layout(pr, pi)


HALF_GROUPS = SSM_GROUPS // N_HALF


def _half_block_diag(blocks):
    _, r, c = blocks.shape
    eye = jnp.eye(HALF_GROUPS, dtype=blocks.dtype)
    b4 = blocks.reshape(N_HALF, HALF_GROUPS, r, c)
    return jnp.einsum("ngrc,gk->ngrkc", b4, eye).reshape(N_HALF, HALF_GROUPS * r, HALF_GROUPS * c)


def _half_diag_blocks(mat, r, c):
    eye = jnp.eye(HALF_GROUPS, dtype=mat.dtype)
    m5 = mat.reshape(N_HALF, HALF_GROUPS, r, HALF_GROUPS, c)
    return jnp.einsum("ngrkc,gk->ngrc", m5, eye).reshape(SSM_GROUPS, r, c)


BIG = ["w_in", "w_conv_out", "w_glu", "w_kv", "w_xattn_out", "w_out", "w_up", "w_down"]
COL_SHARDED = ["w_in", "w_conv_out", "w_glu", "w_xattn_out", "w_up"]
SMALL = {"b_gate": (1, GATE_COLS), "conv_w": (3, CONV_W), "ssm_lam_re": (SSM_GROUPS, SSM_STATE),
         "ssm_lam_im": (SSM_GROUPS, SSM_STATE), "ssm_log_dt": (1, SSM_GROUPS),
         "ssm_b_re": (N_STATE, SSM_GROUP), "ssm_b_im": (N_STATE, SSM_GROUP),
         "ssm_c_re": (SSM_W, SSM_STATE), "ssm_c_im": (SSM_W, SSM_STATE), "ssm_d": (1, SSM_W),
         "ln1_g": (1, D_MODEL), "ln1_b": (1, D_MODEL), "b_up": (1, D_FF), "b_down": (1, D_MODEL),
         "ln2_g": (1, D_MODEL), "ln2_b": (1, D_MODEL)}
WEIGHTS = ["w_in", "b_gate", "conv_w", "w_conv_out", "ssm_lam_re", "ssm_lam_im", "ssm_log_dt", "ssm_b_re", "ssm_b_im",
           "ssm_c_re", "ssm_c_im", "ssm_d", "w_glu", "w_kv", "w_xattn_out", "w_out", "ln1_g", "ln1_b", "w_up", "b_up",
           "w_down", "b_down", "ln2_g", "ln2_b"]


def _local_step(x, mem, tgt, full, late, small):
    lam_re, lam_im, log_dt = small["ssm_lam_re"], small["ssm_lam_im"], small["ssm_log_dt"].reshape(SSM_GROUPS)
    b_shape = (SSM_GROUPS, SSM_STATE, SSM_GROUP)
    c_shape = (SSM_GROUPS, SSM_GROUP, SSM_STATE)
    disc, disc_vjp = jax.vjp(_ssm_discretize, lam_re, lam_im, log_dt,
                             small["ssm_b_re"].reshape(b_shape), small["ssm_b_im"].reshape(b_shape))
    abar_r, abar_i, bbar_r, bbar_i = disc
    pw = _abar_powers(abar_r, abar_i)
    c_re, c_im = small["ssm_c_re"].reshape(c_shape), small["ssm_c_im"].reshape(c_shape)
    b_half = jnp.concatenate([_half_block_diag(bbar_r.transpose(0, 2, 1)), _half_block_diag(bbar_i.transpose(0, 2, 1))],
                             axis=2).astype(bf16)
    c_half = jnp.concatenate([_half_block_diag(c_re.transpose(0, 2, 1)), -_half_block_diag(c_im.transpose(0, 2, 1))],
                             axis=1).astype(bf16)

    stack = lambda a: a.reshape(-1, a.shape[-1])
    kv, memb = _kv_proj(mem, full["w_kv"])
    (xbt, g, cin, u, q, ya, yc, aint, obt), side = _in_proj(
        x, full["w_in"], small["b_gate"], small["conv_w"], full["w_conv_out"], kv, full["w_xattn_out"],
        [late["w_glu"], late["w_out"], late["w_up"]])
    w_glu_t, w_out, w_up_t = (stack(a) for a in side)
    y_ssm, cm_all, side = _ssm_fwd(u, b_half, c_half, pw, small["ssm_d"], [late["w_down"]])
    w_down = stack(side[0])
    ysbt, glu, mb, xhat1, rstd1 = _mid_fwd(y_ssm, g, ya, yc, x, w_glu_t, w_out, small["ln1_g"], small["ln1_b"])
    (x1bt, hdn, dr2bt, dpre, dx1, loss, dl2g, dl2b, dbdn, dbup) = _mlp_fwd_bwd(
        xhat1, tgt, small["ln1_g"], small["ln1_b"], w_up_t, small["b_up"], w_down,
        small["b_down"], small["ln2_g"], small["ln2_b"])
    recv = {}
    recv["w_down"] = _weight_grad_scatter(dr2bt, hdn, "dw_down", tm=512, tt=1024)
    recv["w_up"] = _weight_grad_scatter(x1bt, dpre, "dw_up", tm=512, tt=1024)
    (dxp, dr1bt, dgp, dya, dyc, dglu, dyssm, dl1g, dl1b, dbg) = _mid_bwd(
        dx1, xhat1, rstd1, g, ya, yc, glu, y_ssm, small["ln1_g"], w_out, w_glu_t)
    recv["w_out"] = _weight_grad_scatter(dr1bt, mb, "dw_out", tm=512, tt=2048)
    recv["w_glu"] = _weight_grad_scatter(ysbt, dglu, "dw_glu", tm=512, tt=2048)
    du, db_half, dc_half, da8, dd = _ssm_bwd(u, dyssm, cm_all, b_half, c_half, pw, small["ssm_d"])
    dconv, dq, dcw8, dkv = _branch_bwd(dya, dyc, cin, q, kv, small["conv_w"], full["w_conv_out"], full["w_xattn_out"])
    recv["w_conv_out"] = _weight_grad_scatter(aint, dya, "dw_conv_out", tm=512, tt=2048)
    recv["w_xattn_out"] = _weight_grad_scatter(obt, dyc, "dw_xattn_out", tm=512, tt=2048)
    recv["w_kv"] = _weight_grad_scatter(dkv.T.astype(bf16), memb, "dw_kv", tm=512, tt=MEM_LEN)
    dx, dproj = _in_proj_bwd(dgp, dconv, du, dq, dxp, full["w_in"])
    recv["w_in"] = _weight_grad_scatter(xbt, dproj, "dw_in", tm=512, tt=1024)

    dabar_r, dabar_i = _state_unlayout(jnp.sum(da8, axis=0))
    dbbar_r = _half_diag_blocks(db_half[:, :, :HALF_STATE], SSM_GROUP, SSM_STATE).transpose(0, 2, 1)
    dbbar_i = _half_diag_blocks(db_half[:, :, HALF_STATE:], SSM_GROUP, SSM_STATE).transpose(0, 2, 1)
    g_shape = (SSM_GROUPS, SSM_STATE)
    dlam_re, dlam_im, dlog_dt, db_re, db_im = disc_vjp(
        (dabar_r.reshape(g_shape), dabar_i.reshape(g_shape), dbbar_r, dbbar_i))
    dc_re = _half_diag_blocks(dc_half[:, :HALF_STATE, :], SSM_STATE, SSM_GROUP).transpose(0, 2, 1)
    dc_im = -_half_diag_blocks(dc_half[:, HALF_STATE:, :], SSM_STATE, SSM_GROUP).transpose(0, 2, 1)

    small_grads = {
        "b_gate": dbg, "conv_w": dcw8[0:3], "ssm_lam_re": dlam_re, "ssm_lam_im": dlam_im, "ssm_log_dt": dlog_dt,
        "ssm_b_re": db_re, "ssm_b_im": db_im, "ssm_c_re": dc_re, "ssm_c_im": dc_im, "ssm_d": dd,
        "ln1_g": dl1g, "ln1_b": dl1b, "b_up": dbup, "b_down": dbdn, "ln2_g": dl2g, "ln2_b": dl2b,
    }
    small_grads = {k: a.reshape(SMALL[k]) for k, a in small_grads.items()}
    return loss[0, 0], dx, recv, small_grads


def kernel(x, mem, w_in, b_gate, conv_w, w_conv_out, ssm_lam_re, ssm_lam_im, ssm_log_dt, ssm_b_re, ssm_b_im, ssm_c_re, ssm_c_im, ssm_d, w_glu, w_kv, w_xattn_out, w_out, ln1_g, ln1_b, w_up, b_up, w_down, b_down, ln2_g, ln2_b, loss_target, m_w_in, m_b_gate, m_conv_w, m_w_conv_out, m_ssm_lam_re, m_ssm_lam_im, m_ssm_log_dt, m_ssm_b_re, m_ssm_b_im, m_ssm_c_re, m_ssm_c_im, m_ssm_d, m_w_glu, m_w_kv, m_w_xattn_out, m_w_out, m_ln1_g, m_ln1_b, m_w_up, m_b_up, m_w_down, m_b_down, m_ln2_g, m_ln2_b, v_w_in, v_b_gate, v_conv_w, v_w_conv_out, v_ssm_lam_re, v_ssm_lam_im, v_ssm_log_dt, v_ssm_b_re, v_ssm_b_im, v_ssm_c_re, v_ssm_c_im, v_ssm_d, v_w_glu, v_w_kv, v_w_xattn_out, v_w_out, v_ln1_g, v_ln1_b, v_w_up, v_b_up, v_w_down, v_b_down, v_ln2_g, v_ln2_b):
    w = dict(w_in=w_in, b_gate=b_gate, conv_w=conv_w, w_conv_out=w_conv_out, ssm_lam_re=ssm_lam_re,
             ssm_lam_im=ssm_lam_im, ssm_log_dt=ssm_log_dt, ssm_b_re=ssm_b_re, ssm_b_im=ssm_b_im, ssm_c_re=ssm_c_re,
             ssm_c_im=ssm_c_im, ssm_d=ssm_d, w_glu=w_glu, w_kv=w_kv, w_xattn_out=w_xattn_out, w_out=w_out,
             ln1_g=ln1_g, ln1_b=ln1_b, w_up=w_up, b_up=b_up, w_down=w_down, b_down=b_down, ln2_g=ln2_g, ln2_b=ln2_b)
    m = dict(w_in=m_w_in, b_gate=m_b_gate, conv_w=m_conv_w, w_conv_out=m_w_conv_out, ssm_lam_re=m_ssm_lam_re,
             ssm_lam_im=m_ssm_lam_im, ssm_log_dt=m_ssm_log_dt, ssm_b_re=m_ssm_b_re, ssm_b_im=m_ssm_b_im,
             ssm_c_re=m_ssm_c_re, ssm_c_im=m_ssm_c_im, ssm_d=m_ssm_d, w_glu=m_w_glu, w_kv=m_w_kv,
             w_xattn_out=m_w_xattn_out, w_out=m_w_out, ln1_g=m_ln1_g, ln1_b=m_ln1_b, w_up=m_w_up, b_up=m_b_up,
             w_down=m_w_down, b_down=m_b_down, ln2_g=m_ln2_g, ln2_b=m_ln2_b)
    v = dict(w_in=v_w_in, b_gate=v_b_gate, conv_w=v_conv_w, w_conv_out=v_w_conv_out, ssm_lam_re=v_ssm_lam_re,
             ssm_lam_im=v_ssm_lam_im, ssm_log_dt=v_ssm_log_dt, ssm_b_re=v_ssm_b_re, ssm_b_im=v_ssm_b_im,
             ssm_c_re=v_ssm_c_re, ssm_c_im=v_ssm_c_im, ssm_d=v_ssm_d, w_glu=v_w_glu, w_kv=v_w_kv,
             w_xattn_out=v_w_xattn_out, w_out=v_w_out, ln1_g=v_ln1_g, ln1_b=v_ln1_b, w_up=v_w_up, b_up=v_b_up,
             w_down=v_w_down, b_down=v_b_down, ln2_g=v_ln2_g, ln2_b=v_ln2_b)
    out_shapes = {k: a.shape for k, a in w.items()}
    shard2d = lambda k, a: a.reshape((3, CONV_W // N_DEV) if k == "conv_w" else SMALL[k]) if k in SMALL else a[0]
    w, m, v = ({k: shard2d(k, a) for k, a in d.items()} for d in (w, m, v))

    shards = {k: w[k].T.astype(bf16) if k in COL_SHARDED else w[k].astype(bf16) for k in BIG}
    conv_pad = jnp.pad(w["conv_w"], ((0, 5), (0, LANES - CONV_W // N_DEV)))
    early = ["w_in", "w_conv_out", "w_kv", "w_xattn_out"]
    gathered = _all_gather([shards[k] for k in early] + [conv_pad], "gather_weights")
    full = {k: a.reshape(-1, a.shape[-1]) for k, a in zip(early, gathered[:-1])}
    late = {k: shards[k] for k in BIG if k not in early}
    conv_full = gathered[-1][:, :3, :CONV_W // N_DEV].transpose(1, 0, 2).reshape(3, CONV_W)
    small = {k: (conv_full if k == "conv_w" else w[k]) for k in SMALL}

    loss, dx, recv, small_grads = _local_step(x[0], mem[0], loss_target[0], full, late, small)

    grads, deltas, new_m, new_v = {}, {}, {}, {}
    for k in BIG:
        res = _adamw_update(w[k], m[k], v[k], recv[k], "adamw_" + k, transposed=k not in COL_SHARDED)
        grads[k], deltas[k], new_m[k], new_v[k] = res

    stacks = [jnp.concatenate([small_grads[k] for k in group], axis=axis) if len(group) > 1 else small_grads[group[0]]
              for group, axis in SMALL_GROUPS]
    group_parts = _all_gather(stacks, "gather_small_grads")
    widen = lambda k, a: jnp.tile(a, (1, N_DEV)) if k == "conv_w" else a
    res = _adamw_small(small, {k: widen(k, m[k]) for k in SMALL}, {k: widen(k, v[k]) for k in SMALL}, group_parts)
    dev = _slot(_mesh_place())
    for d, small_res in zip((grads, deltas, new_m, new_v), res):
        for k, a in small_res.items():
            if k == "conv_w":
                a = lax.dynamic_slice_in_dim(a, dev * (CONV_W // N_DEV), CONV_W // N_DEV, axis=1)
            d[k] = a

    loss = lax.psum(loss, ("x", "y", "c"))
    outs = [loss, dx[None]]
    for d in (grads, deltas, new_m, new_v):
        outs += [d[k].reshape(out_shapes[k]) for k in WEIGHTS]
    return tuple(outs)
```

```python
import functools
import math

import jax
import jax.numpy as jnp
from jax import lax
from jax.experimental import pallas as pl
from jax.experimental.pallas import tpu as pltpu

f32 = jnp.float32
bf16 = jnp.bfloat16

D_MODEL = 1024
MEM_LEN = 256
GATE_COLS = 3 * D_MODEL
CONV_W = 512
SSM_W = 512
XATTN_W = 512
HEADS = 4
HEAD_DIM = 128
D_FF = 4096
IN_COLS = GATE_COLS + 3 * CONV_W + SSM_W + XATTN_W
SSM_GROUPS = 32
SSM_GROUP = 16
SSM_STATE = 64
N_STATE = SSM_GROUPS * SSM_STATE
ALPHA = 2.0 ** 0.25
LN_EPS = 1e-5
N_DEV = 8

ADAM_LR = 0.001
ADAM_B1 = 0.9
ADAM_B2 = 0.999
ADAM_EPS = 1e-08
ADAM_WD = 0.01
ADAM_STEP = 10

VMEM_LIMIT_V7X = 56 * 2 ** 20
SUBLANES = 8
LANES = 128

TOKEN_TILE = 256
SSM_BLOCK = 256
SSM_SEG = SSM_BLOCK // SUBLANES
LANE_CHUNK = 512
N_HALF = 2
HALF_W = SSM_W // N_HALF
HALF_STATE = N_STATE // N_HALF
HALF_COLS = 2 * HALF_STATE

NT = (((1,), (1,)), ((), ()))
TN = (((0,), (0,)), ((), ()))
NN = (((1,), (0,)), ((), ()))


def _dot(a, b, dims=NN):
    return lax.dot_general(a, b, dims, preferred_element_type=f32)


def _cparams(sem=None):
    return pltpu.CompilerParams(dimension_semantics=sem, vmem_limit_bytes=VMEM_LIMIT_V7X)


def _row_spec(tm, cols, rev_n=None):
    if rev_n is None:
        return pl.BlockSpec((tm, cols), lambda i: (i, 0))
    return pl.BlockSpec((tm, cols), lambda i: (rev_n - 1 - i, 0))


def _col_spec(rows, tm):
    return pl.BlockSpec((rows, tm), lambda i: (0, i))


def _const_spec(shape):
    nd = len(shape)
    return pl.BlockSpec(shape, lambda *_: (0,) * nd, pipeline_mode=pl.Buffered(1))


def _acc_spec(shape):
    nd = len(shape)
    return pl.BlockSpec(shape, lambda *_: (0,) * nd)


def _sds(shape, dtype):
    return jax.ShapeDtypeStruct(shape, dtype)


def _gelu(x):
    c = math.sqrt(2.0 / math.pi)
    return 0.5 * x * (1.0 + jnp.tanh(c * (x + 0.044715 * x * x * x)))


def _gelu_grad(x):
    c = math.sqrt(2.0 / math.pi)
    t = jnp.tanh(c * (x + 0.044715 * x * x * x))
    return 0.5 * (1.0 + t) + 0.5 * x * (1.0 - t * t) * c * (1.0 + 3.0 * 0.044715 * x * x)


def _colsum(a):
    return jnp.sum(a, axis=0, keepdims=True)


def _mesh_place():
    return lax.axis_index("x"), lax.axis_index("y"), lax.axis_index("c")


def _slot(p):
    return 4 * p[0] + 2 * p[1] + p[2]


def _other_devices(me):
    x, y, c = me
    flip = lambda v, d: 1 - v if d else v
    return [(flip(x, dx), flip(y, dy), flip(c, dc)) for dx in (0, 1) for dy in (0, 1) for dc in (0, 1)][1:]


def _all_gather(blocks, name):
    n = len(blocks)

    def body(*refs):
        ins, outs = refs[:n], refs[n:2 * n]
        send_sems, recv_sems, local_sems = refs[2 * n:]
        x, y, c = _mesh_place()
        me, sibling = (x, y, c), (x, y, 1 - c)
        chips = [(1 - x, y), (x, 1 - y), (1 - x, 1 - y)]

        def copy(a, k, block, to, src=None):
            rows = outs[a].at[_slot(block)]
            return pltpu.make_async_remote_copy(
                src_ref=rows if src is None else src, dst_ref=rows,
                send_sem=send_sems.at[a, k], recv_sem=recv_sems.at[a, k],
                device_id=to, device_id_type=pl.DeviceIdType.MESH)

        mine = [pltpu.make_async_copy(ins[a], outs[a].at[_slot(me)], local_sems.at[a]) for a in range(n)]
        for cp in mine:
            cp.start()
        first = []
        for a in range(n):
            first.append(copy(a, 0, me, sibling, src=ins[a]))
            first += [copy(a, 1 + j, me, (*chip, c), src=ins[a]) for j, chip in enumerate(chips)]
        for cp in first:
            cp.start()
        passed = []
        for a in range(n):
            for j, chip in enumerate(chips):
                copy(a, 1 + j, (*chip, c), me).wait_recv()
                fwd = copy(a, 4 + j, (*chip, c), sibling)
                fwd.start()
                passed.append(fwd)
        for a in range(n):
            copy(a, 0, sibling, me).wait_recv()
            for j, chip in enumerate(chips):
                copy(a, 4 + j, (*chip, 1 - c), me).wait_recv()
        for cp in first + passed:
            cp.wait_send()
        for cp in mine:
            cp.wait()

    any_spec = pl.BlockSpec(memory_space=pl.ANY)
    return pl.pallas_call(
        body, name=name,
        out_shape=[_sds((N_DEV,) + b.shape, b.dtype) for b in blocks],
        in_specs=[any_spec] * n, out_specs=[any_spec] * n,
        scratch_shapes=[pltpu.SemaphoreType.DMA((n, 7)), pltpu.SemaphoreType.DMA((n, 7)),
                        pltpu.SemaphoreType.DMA((n,))],
    )(*blocks)


def _side_gather_copies(ins, outs, send_sems, recv_sems, local_sems):
    me = _mesh_place()
    copies = []
    for a, (src, dst) in enumerate(zip(ins, outs)):
        copies.append(pltpu.make_async_copy(src, dst.at[_slot(me)], local_sems.at[a]))
        for k, peer in enumerate(_other_devices(me)):
            copies.append(pltpu.make_async_remote_copy(
                src_ref=src, dst_ref=dst.at[_slot(me)], send_sem=send_sems.at[a, k], recv_sem=recv_sems.at[a, k],
                device_id=peer, device_id_type=pl.DeviceIdType.MESH))
    return copies


def _side_gather_specs(blocks):
    n = len(blocks)
    any_spec = pl.BlockSpec(memory_space=pl.ANY)
    return ([any_spec] * n, [_sds((N_DEV,) + b.shape, b.dtype) for b in blocks],
            [pltpu.SemaphoreType.DMA((n, N_DEV - 1)), pltpu.SemaphoreType.DMA((n, N_DEV - 1)),
             pltpu.SemaphoreType.DMA((n,))])


def _kv_proj(mem, w_kv):
    def body(mem_ref, w_ref, kv_ref, memb_ref):
        mb = mem_ref[...].astype(bf16)
        memb_ref[...] = mb
        kv_ref[...] = _dot(mb, w_ref[...]).astype(bf16)

    return pl.pallas_call(
        body, name="kv_proj",
        out_shape=[_sds((MEM_LEN, 2 * XATTN_W), bf16), _sds((MEM_LEN, D_MODEL), bf16)],
        compiler_params=_cparams(),
    )(mem, w_kv)


def _attention_probs(qb, kv_ref, h):
    kh = kv_ref[:, h * HEAD_DIM:(h + 1) * HEAD_DIM]
    s = _dot(qb[:, h * HEAD_DIM:(h + 1) * HEAD_DIM], kh, NT) * (HEAD_DIM ** -0.5)
    e = jnp.exp(s - jnp.max(s, axis=-1, keepdims=True))
    return e / jnp.sum(e, axis=-1, keepdims=True)


def _in_proj(x, w_in_t, b_gate, conv_w, kv, side_blocks):
    s_len = x.shape[0]
    tm = TOKEN_TILE
    n = s_len // tm
    ns = len(side_blocks)
    side_in_specs, side_shapes, side_sems = _side_gather_specs(side_blocks)

    def body(*refs):
        (x_ref, win_ref, bg_ref, cw_ref, kv_ref) = refs[:5]
        side_ins = refs[5:5 + ns]
        (xbt_ref, g_ref, cin_ref, u_ref, q_ref, ain_ref, o_ref, aint_ref, ot_ref) = refs[5 + ns:14 + ns]
        side_outs = refs[14 + ns:14 + 2 * ns]
        zs_ref = refs[14 + 2 * ns]
        side = _side_gather_copies(side_ins, side_outs, *refs[15 + 2 * ns:])
        i = pl.program_id(0)

        @pl.when(i == 0)
        def _():
            for cp in side:
                cp.start()

        xb = x_ref[...].astype(bf16)
        xbt_ref[...] = xb.T
        proj = _dot(xb, win_ref[...], NT)
        g_ref[...] = jax.nn.sigmoid(proj[:, :GATE_COLS] + bg_ref[...])
        cin = proj[:, GATE_COLS:GATE_COLS + 3 * CONV_W]
        cin_ref[...] = cin
        u_ref[...] = proj[:, GATE_COLS + 3 * CONV_W:GATE_COLS + 3 * CONV_W + SSM_W]
        qb = proj[:, IN_COLS - XATTN_W:].astype(bf16)
        q_ref[...] = qb

        cb, cc, ch = cin[:, :CONV_W], cin[:, CONV_W:2 * CONV_W], cin[:, 2 * CONV_W:]
        z = cc * ch

        @pl.when(i == 0)
        def _():
            zs_ref[0:8, :] = jnp.zeros((8, CONV_W), f32)

        zs_ref[8:8 + tm, :] = z
        z1 = zs_ref[pl.ds(7, tm), :]
        z2 = zs_ref[pl.ds(6, tm), :]
        cw = cw_ref[...]
        cz = cw[0:1] * z2 + cw[1:2] * z1 + cw[2:3] * z
        zs_ref[0:8, :] = zs_ref[tm:tm + 8, :]
        ain = (cb * cz).astype(bf16)
        ain_ref[...] = ain
        aint_ref[...] = ain.T

        outs = []
        for h in range(HEADS):
            p = _attention_probs(qb, kv_ref, h)
            vh = kv_ref[:, XATTN_W + h * HEAD_DIM:XATTN_W + (h + 1) * HEAD_DIM]
            outs.append(_dot(p.astype(bf16), vh))
        ob = jnp.concatenate(outs, axis=1).astype(bf16)
        o_ref[...] = ob
        ot_ref[...] = ob.T

        @pl.when(i == n - 1)
        def _():
            for cp in side:
                cp.wait()

    row_cols = [(GATE_COLS, f32), (3 * CONV_W, f32), (SSM_W, f32), (XATTN_W, bf16), (CONV_W, bf16), (XATTN_W, bf16)]
    t_rows = [D_MODEL, CONV_W, XATTN_W]
    outs = pl.pallas_call(
        body, name="in_proj", grid=(n,),
        in_specs=[_row_spec(tm, D_MODEL), _const_spec((IN_COLS, D_MODEL)), _const_spec((1, GATE_COLS)),
                  _const_spec((3, CONV_W)), _const_spec((MEM_LEN, 2 * XATTN_W))] + side_in_specs,
        out_specs=([_col_spec(t_rows[0], tm)] + [_row_spec(tm, c) for c, _ in row_cols]
                   + [_col_spec(t_rows[1], tm), _col_spec(t_rows[2], tm)] + side_in_specs),
        out_shape=([_sds((t_rows[0], s_len), bf16)] + [_sds((s_len, c), dt) for c, dt in row_cols]
                   + [_sds((t_rows[1], s_len), bf16), _sds((t_rows[2], s_len), bf16)] + side_shapes),
        scratch_shapes=[pltpu.VMEM((tm + 8, CONV_W), f32)] + side_sems,
        compiler_params=_cparams(("arbitrary",)),
    )(x, w_in_t, b_gate, conv_w, kv, *side_blocks)
    return outs[:9], outs[9:]


def _state_cols(chunk):
    half, off = divmod(chunk * LANE_CHUNK, HALF_STATE)
    lo = half * HALF_COLS + off
    return slice(lo, lo + LANE_CHUNK), slice(lo + HALF_STATE, lo + HALF_STATE + LANE_CHUNK)


def _half_cols(half):
    lo = half * HALF_COLS
    return slice(lo, lo + HALF_STATE), slice(lo + HALF_STATE, lo + HALF_COLS)


def _rows_to_segments(src_ref, stage_ref, dst_ref):
    nc = SSM_W // LANES
    for c in range(nc):
        stage_ref[c] = src_ref[:, c * LANES:(c + 1) * LANES]
    for c in range(nc):
        for k in range(SSM_SEG):
            dst_ref[k * SUBLANES:(k + 1) * SUBLANES, c * LANES:(c + 1) * LANES] = (
                stage_ref[c, pl.ds(k, SUBLANES, stride=SSM_SEG), :])


def _rows_from_segments(src_ref, stage_ref, dst_ref):
    nc = SSM_W // LANES
    for c in range(nc):
        for k in range(SSM_SEG):
            stage_ref[c, pl.ds(k, SUBLANES, stride=SSM_SEG), :] = (
                src_ref[k * SUBLANES:(k + 1) * SUBLANES, c * LANES:(c + 1) * LANES])
    for c in range(nc):
        dst_ref[:, c * LANES:(c + 1) * LANES] = stage_ref[c]


def _ssm_scan(s_ref, pw_ref, init_ref, reverse):
    for chunk in range(N_STATE // LANE_CHUNK):
        re, im = _state_cols(chunk)
        ar = jnp.broadcast_to(pw_ref[0:1, re], (SUBLANES, LANE_CHUNK))
        ai = jnp.broadcast_to(pw_ref[0:1, im], (SUBLANES, LANE_CHUNK))
        if reverse:
            ai = -ai

        def step(j, carry, re=re, im=im, ar=ar, ai=ai):
            sr, si = carry
            k = (SSM_SEG - 1 - j) if reverse else j
            r0 = pl.multiple_of(k * SUBLANES, SUBLANES)
            nr = ar * sr - ai * si + s_ref[pl.ds(r0, SUBLANES), re]
            ni = ar * si + ai * sr + s_ref[pl.ds(r0, SUBLANES), im]
            s_ref[pl.ds(r0, SUBLANES), re] = nr
            s_ref[pl.ds(r0, SUBLANES), im] = ni
            return nr, ni

        if init_ref is None:
            init = (jnp.zeros((SUBLANES, LANE_CHUNK), f32),) * 2
        else:
            init = (init_ref[:, re], init_ref[:, im])
        lax.fori_loop(0, SSM_SEG, step, init, unroll=True)


def _ssm_add_carry(s_ref, pw_ref, cm_ref, reverse):
    for chunk in range(N_STATE // LANE_CHUNK):
        re, im = _state_cols(chunk)
        cr, ci = cm_ref[:, re], cm_ref[:, im]
        for k in range(SSM_SEG):
            pk = (SSM_SEG - 1 - k) if reverse else k
            pr = pw_ref[pk:pk + 1, re]
            pi = pw_ref[pk:pk + 1, im]
            if reverse:
                pi = -pi
            rows = slice(k * SUBLANES, (k + 1) * SUBLANES)
            s_ref[rows, re] = s_ref[rows, re] + (pr * cr - pi * ci)
            s_ref[rows, im] = s_ref[rows, im] + (pr * ci + pi * cr)


def _ssm_carries(first_row, s_ref, pw_ref, carry_ref, cm_ref, reverse):
    order = range(SUBLANES - 1, -1, -1) if reverse else range(SUBLANES)
    for half in range(N_HALF):
        re, im = _half_cols(half)
        a_r, a_i = pw_ref[SSM_SEG - 1:SSM_SEG, re], pw_ref[SSM_SEG - 1:SSM_SEG, im]
        if reverse:
            a_i = -a_i
        cr, ci = carry_ref[0:1, re], carry_ref[0:1, im]
        for seg in order:
            cm_ref[seg:seg + 1, re] = cr
            cm_ref[seg:seg + 1, im] = ci
            er = s_ref[first_row + seg:first_row + seg + 1, re]
            ei = s_ref[first_row + seg:first_row + seg + 1, im]
            cr, ci = a_r * cr - a_i * ci + er, a_r * ci + a_i * cr + ei
        carry_ref[0:1, re] = cr
        carry_ref[0:1, im] = ci


def _ssm_fwd(u, b_half, c_half, pw, d_skip, side_blocks):
    s_len = u.shape[0]
    tb = SSM_BLOCK
    n = s_len // tb
    ns = len(side_blocks)
    side_in_specs, side_shapes, side_sems = _side_gather_specs(side_blocks)

    def body(*refs):
        u_ref, b_ref, c_ref, pw_ref, d_ref = refs[:5]
        side_ins = refs[5:5 + ns]
        y_ref, cm_ref = refs[5 + ns:7 + ns]
        side_outs = refs[7 + ns:7 + 2 * ns]
        s_ref, carry_ref, up_ref, yp_ref, stage_ref = refs[7 + 2 * ns:12 + 2 * ns]
        side = _side_gather_copies(side_ins, side_outs, *refs[12 + 2 * ns:])
        i = pl.program_id(0)

        @pl.when(i == 0)
        def _():
            carry_ref[...] = jnp.zeros_like(carry_ref)
            for cp in side:
                cp.start()

        _rows_to_segments(u_ref, stage_ref, up_ref)
        u = up_ref[...]
        ub = u.astype(bf16)
        for half in range(N_HALF):
            s_ref[:, half * HALF_COLS:(half + 1) * HALF_COLS] = _dot(ub[:, half * HALF_W:(half + 1) * HALF_W], b_ref[half])
        _ssm_scan(s_ref, pw_ref, None, reverse=False)
        _ssm_carries(tb - SUBLANES, s_ref, pw_ref, carry_ref, cm_ref, reverse=False)
        _ssm_add_carry(s_ref, pw_ref, cm_ref, reverse=False)
        for half in range(N_HALF):
            cols = slice(half * HALF_W, (half + 1) * HALF_W)
            sb = s_ref[:, half * HALF_COLS:(half + 1) * HALF_COLS].astype(bf16)
            yp_ref[:, cols] = _dot(sb, c_ref[half]) + d_ref[:, cols] * u[:, cols]
        _rows_from_segments(yp_ref, stage_ref, y_ref)

        @pl.when(i == n - 1)
        def _():
            for cp in side:
                cp.wait()

    outs = pl.pallas_call(
        body, name="ssm_fwd", grid=(n,),
        in_specs=[_row_spec(tb, SSM_W), _const_spec((N_HALF, HALF_W, HALF_COLS)), _const_spec((N_HALF, HALF_COLS, HALF_W)),
                  _const_spec((SSM_SEG, 2 * N_STATE)), _const_spec((1, SSM_W))] + side_in_specs,
        out_specs=[_row_spec(tb, SSM_W), _row_spec(SUBLANES, 2 * N_STATE)] + side_in_specs,
        out_shape=[_sds((s_len, SSM_W), f32), _sds((n * SUBLANES, 2 * N_STATE), f32)] + side_shapes,
        scratch_shapes=[pltpu.VMEM((tb, 2 * N_STATE), f32), pltpu.VMEM((SUBLANES, 2 * N_STATE), f32),
                        pltpu.VMEM((tb, SSM_W), f32), pltpu.VMEM((tb, SSM_W), f32),
                        pltpu.VMEM((SSM_W // LANES, tb, LANES), f32)] + side_sems,
        compiler_params=_cparams(("arbitrary",)),
    )(u, b_half, c_half, pw, d_skip, *side_blocks)
    return outs[0], outs[1], outs[2:]


def _layer_norm_fwd(r, g, b):
    mu = jnp.mean(r, axis=-1, keepdims=True)
    var = jnp.mean(jnp.square(r - mu), axis=-1, keepdims=True)
    rstd = lax.rsqrt(var + LN_EPS)
    xhat = (r - mu) * rstd
    return xhat, rstd, xhat * g + b


def _layer_norm_bwd(dy, xhat, rstd, g):
    dxh = dy * g
    m1 = jnp.mean(dxh, axis=-1, keepdims=True)
    m2 = jnp.mean(dxh * xhat, axis=-1, keepdims=True)
    return rstd * (dxh - m1 - xhat * m2)


def _branch_outputs(ys_ref, ain_ref, o_ref, wglu_ref, wco_ref, wxo_ref):
    ysb = _gelu(ys_ref[...]).astype(bf16)
    glu = _dot(ysb, wglu_ref[...], NT)
    ga, sb = glu[:, :D_MODEL], jax.nn.sigmoid(glu[:, D_MODEL:])
    ya = _dot(ain_ref[...], wco_ref[...], NT)
    yc = _dot(o_ref[...], wxo_ref[...], NT)
    return ysb, ga, sb, ya, ga * sb, yc


def _mid_fwd(y_ssm, g, ain, ob, x, w_glu_t, w_co_t, w_xo_t, w_out, ln1_g, ln1_b):
    s_len = x.shape[0]
    tm = TOKEN_TILE
    n = s_len // tm

    def body(ys_ref, g_ref, ain_ref, o_ref, x_ref, wglu_ref, wco_ref, wxo_ref, wout_ref, lg_ref, lb_ref,
             ysbt_ref, mb_ref, xhat_ref, rstd_ref):
        ysb, _, _, ya, yb, yc = _branch_outputs(ys_ref, ain_ref, o_ref, wglu_ref, wco_ref, wxo_ref)
        ysbt_ref[...] = ysb.T
        gt = g_ref[...]
        merged = gt[:, :D_MODEL] * ya + gt[:, D_MODEL:2 * D_MODEL] * yb + gt[:, 2 * D_MODEL:] * yc
        mb = merged.astype(bf16)
        mb_ref[...] = mb
        r1 = ALPHA * x_ref[...] + _dot(mb, wout_ref[...])
        xhat, rstd, _ = _layer_norm_fwd(r1, lg_ref[...], lb_ref[...])
        xhat_ref[...] = xhat
        rstd_ref[...] = rstd

    row_cols = [(D_MODEL, bf16), (D_MODEL, f32), (1, f32)]
    return pl.pallas_call(
        body, name="mid_fwd", grid=(n,),
        in_specs=[_row_spec(tm, SSM_W), _row_spec(tm, GATE_COLS), _row_spec(tm, CONV_W), _row_spec(tm, XATTN_W),
                  _row_spec(tm, D_MODEL), _const_spec((2 * D_MODEL, SSM_W)), _const_spec((D_MODEL, CONV_W)),
                  _const_spec((D_MODEL, XATTN_W)), _const_spec((D_MODEL, D_MODEL)),
                  _const_spec((1, D_MODEL)), _const_spec((1, D_MODEL))],
        out_specs=[_col_spec(SSM_W, tm)] + [_row_spec(tm, c) for c, _ in row_cols],
        out_shape=[_sds((SSM_W, s_len), bf16)] + [_sds((s_len, c), dt) for c, dt in row_cols],
        compiler_params=_cparams(("parallel",)),
    )(y_ssm, g, ain, ob, x, w_glu_t, w_co_t, w_xo_t, w_out, ln1_g, ln1_b)


def _mlp_fwd_bwd(xhat1, tgt, ln1_g, ln1_b, w_up_t, b_up, w_down, b_down, ln2_g, ln2_b):
    s_len = xhat1.shape[0]
    tm = TOKEN_TILE
    n = s_len // tm
    fc = 1024
    nfc = D_FF // fc

    def body(xh_ref, t_ref, l1g_ref, l1b_ref, wup_ref, bup_ref, wdn_ref, bdn_ref, l2g_ref, l2b_ref,
             x1bt_ref, hdn_ref, dr2bt_ref, dpre_ref, dx1_ref,
             loss_ref, dl2g_ref, dl2b_ref, dbdn_ref, dbup_ref, rl_ref):
        i = pl.program_id(0)

        @pl.when(i == 0)
        def _():
            loss_ref[...] = jnp.zeros_like(loss_ref)
            dl2g_ref[...] = jnp.zeros_like(dl2g_ref)
            dl2b_ref[...] = jnp.zeros_like(dl2b_ref)
            dbdn_ref[...] = jnp.zeros_like(dbdn_ref)
            dbup_ref[...] = jnp.zeros_like(dbup_ref)

        x1 = xh_ref[...] * l1g_ref[...] + l1b_ref[...]
        x1b = x1.astype(bf16)
        x1bt_ref[...] = x1b.T
        acc = jnp.zeros((tm, D_MODEL), f32)
        for c in range(nfc):
            cols = slice(c * fc, (c + 1) * fc)
            pre = _dot(x1b, wup_ref[cols, :], NT) + bup_ref[:, cols]
            rl = jnp.maximum(pre, 0.0)
            rl_ref[:, cols] = rl
            hb = (rl * rl).astype(bf16)
            hdn_ref[:, cols] = hb
            acc = acc + _dot(hb, wdn_ref[cols, :])
        r2 = ALPHA * x1 + acc + bdn_ref[...]
        xhat2, rstd2, y = _layer_norm_fwd(r2, l2g_ref[...], l2b_ref[...])
        err = y - t_ref[...]
        loss_ref[...] += jnp.sum(jnp.sum(err * err, axis=1, keepdims=True), axis=0, keepdims=True) * (0.5 / D_MODEL)
        dy = err * (1.0 / D_MODEL)
        dl2g_ref[...] += _colsum(dy * xhat2)
        dl2b_ref[...] += _colsum(dy)
        dr2 = _layer_norm_bwd(dy, xhat2, rstd2, l2g_ref[...])
        dbdn_ref[...] += _colsum(dr2)
        dr2b = dr2.astype(bf16)
        dr2bt_ref[...] = dr2b.T
        dacc = jnp.zeros((tm, D_MODEL), f32)
        for c in range(nfc):
            cols = slice(c * fc, (c + 1) * fc)
            dh = _dot(dr2b, wdn_ref[cols, :], NT)
            dpre = dh * (2.0 * rl_ref[:, cols])
            dbup_ref[:, cols] += _colsum(dpre)
            dpb = dpre.astype(bf16)
            dpre_ref[:, cols] = dpb
            dacc = dacc + _dot(dpb, wup_ref[cols, :])
        dx1_ref[...] = ALPHA * dr2 + dacc

    acc_shapes = [(1, LANES), (1, D_MODEL), (1, D_MODEL), (1, D_MODEL), (1, D_FF)]
    return pl.pallas_call(
        body, name="mlp_fwd_bwd", grid=(n,),
        in_specs=[_row_spec(tm, D_MODEL), _row_spec(tm, D_MODEL), _const_spec((1, D_MODEL)), _const_spec((1, D_MODEL)),
                  _const_spec((D_FF, D_MODEL)), _const_spec((1, D_FF)), _const_spec((D_FF, D_MODEL)),
                  _const_spec((1, D_MODEL)), _const_spec((1, D_MODEL)), _const_spec((1, D_MODEL))],
        out_specs=([_col_spec(D_MODEL, tm), _row_spec(tm, D_FF), _col_spec(D_MODEL, tm), _row_spec(tm, D_FF),
                    _row_spec(tm, D_MODEL)] + [_acc_spec(s) for s in acc_shapes]),
        out_shape=([_sds((D_MODEL, s_len), bf16), _sds((s_len, D_FF), bf16), _sds((D_MODEL, s_len), bf16),
                    _sds((s_len, D_FF), bf16), _sds((s_len, D_MODEL), f32)] + [_sds(s, f32) for s in acc_shapes]),
        scratch_shapes=[pltpu.VMEM((tm, D_FF), f32)],
        compiler_params=_cparams(("arbitrary",)),
    )(xhat1, tgt, ln1_g, ln1_b, w_up_t, b_up, w_down, b_down, ln2_g, ln2_b)


def _mid_bwd(dx1, xhat1, rstd1, g, ain, ob, y_ssm, ln1_g, w_out, w_glu_t, w_co_t, w_xo_t):
    s_len = dx1.shape[0]
    tm = TOKEN_TILE
    n = s_len // tm

    def body(dx1_ref, xh_ref, rs_ref, g_ref, ain_ref, o_ref, ys_ref, lg_ref, wout_ref, wglu_ref, wco_ref, wxo_ref,
             dxp_ref, dr1bt_ref, dgp_ref, dya_ref, dyc_ref, dglu_ref, dyssm_ref,
             dl1g_ref, dl1b_ref, dbg_ref):
        i = pl.program_id(0)

        @pl.when(i == 0)
        def _():
            dl1g_ref[...] = jnp.zeros_like(dl1g_ref)
            dl1b_ref[...] = jnp.zeros_like(dl1b_ref)
            dbg_ref[...] = jnp.zeros_like(dbg_ref)

        dx1 = dx1_ref[...]
        xhat = xh_ref[...]
        dl1g_ref[...] += _colsum(dx1 * xhat)
        dl1b_ref[...] += _colsum(dx1)
        dr1 = _layer_norm_bwd(dx1, xhat, rs_ref[...], lg_ref[...])
        dxp_ref[...] = ALPHA * dr1
        dr1b = dr1.astype(bf16)
        dr1bt_ref[...] = dr1b.T
        dm = _dot(dr1b, wout_ref[...], NT)

        _, ga, sb, ya, yb, yc = _branch_outputs(ys_ref, ain_ref, o_ref, wglu_ref, wco_ref, wxo_ref)
        gt = g_ref[...]
        branch = (ya, yb, yc)
        for j in range(3):
            cols = slice(j * D_MODEL, (j + 1) * D_MODEL)
            gj = gt[:, cols]
            dgp = dm * branch[j] * gj * (1.0 - gj)
            dbg_ref[:, cols] += _colsum(dgp)
            dgp_ref[:, cols] = dgp.astype(bf16)
        dya_ref[...] = (dm * gt[:, :D_MODEL]).astype(bf16)
        dyc_ref[...] = (dm * gt[:, 2 * D_MODEL:]).astype(bf16)
        dyb = dm * gt[:, D_MODEL:2 * D_MODEL]
        dga = (dyb * sb).astype(bf16)
        dgb = (dyb * ga * sb * (1.0 - sb)).astype(bf16)
        dglu_ref[:, :D_MODEL] = dga
        dglu_ref[:, D_MODEL:] = dgb
        dys = _dot(dga, wglu_ref[:D_MODEL, :]) + _dot(dgb, wglu_ref[D_MODEL:, :])
        dyssm_ref[...] = dys * _gelu_grad(ys_ref[...])

    row_cols = [(GATE_COLS, bf16), (D_MODEL, bf16), (D_MODEL, bf16), (2 * D_MODEL, bf16), (SSM_W, f32)]
    acc_shapes = [(1, D_MODEL), (1, D_MODEL), (1, GATE_COLS)]
    return pl.pallas_call(
        body, name="mid_bwd", grid=(n,),
        in_specs=[_row_spec(tm, D_MODEL), _row_spec(tm, D_MODEL), _row_spec(tm, 1), _row_spec(tm, GATE_COLS),
                  _row_spec(tm, CONV_W), _row_spec(tm, XATTN_W), _row_spec(tm, SSM_W),
                  _const_spec((1, D_MODEL)), _const_spec((D_MODEL, D_MODEL)), _const_spec((2 * D_MODEL, SSM_W)),
                  _const_spec((D_MODEL, CONV_W)), _const_spec((D_MODEL, XATTN_W))],
        out_specs=([_row_spec(tm, D_MODEL), _col_spec(D_MODEL, tm)] + [_row_spec(tm, c) for c, _ in row_cols]
                   + [_acc_spec(s) for s in acc_shapes]),
        out_shape=([_sds((s_len, D_MODEL), f32), _sds((D_MODEL, s_len), bf16)]
                   + [_sds((s_len, c), dt) for c, dt in row_cols] + [_sds(s, f32) for s in acc_shapes]),
        compiler_params=_cparams(("arbitrary",)),
    )(dx1, xhat1, rstd1, g, ain, ob, y_ssm, ln1_g, w_out, w_glu_t, w_co_t, w_xo_t)


def _ssm_bwd(u, dy, cm_all, b_half, c_half, pw, d_skip):
    s_len = u.shape[0]
    tb = SSM_BLOCK
    n = s_len // tb

    def body(u_ref, dy_ref, cm_ref, b_ref, c_ref, pw_ref, d_ref,
             du_ref, db_hbm, dc_hbm, da_ref, dd_ref,
             s_ref, g_ref, gcarry_ref, gcm_ref, db_ref, dc_ref, up_ref, dyp_ref, dup_ref, stage_ref):
        i = pl.program_id(0)

        @pl.when(i == 0)
        def _():
            gcarry_ref[...] = jnp.zeros_like(gcarry_ref)
            db_ref[...] = jnp.zeros_like(db_ref)
            dc_ref[...] = jnp.zeros_like(dc_ref)
            da_ref[...] = jnp.zeros_like(da_ref)
            dd_ref[...] = jnp.zeros_like(dd_ref)

        _rows_to_segments(u_ref, stage_ref, up_ref)
        _rows_to_segments(dy_ref, stage_ref, dyp_ref)
        u = up_ref[...]
        ub = u.astype(bf16)
        dy = dyp_ref[...]
        dyb = dy.astype(bf16)
        dd_ref[...] += _colsum(dy * u)

        for half in range(N_HALF):
            s_ref[:, half * HALF_COLS:(half + 1) * HALF_COLS] = _dot(ub[:, half * HALF_W:(half + 1) * HALF_W], b_ref[half])
        _ssm_scan(s_ref, pw_ref, cm_ref, reverse=False)

        for half in range(N_HALF):
            g_ref[:, half * HALF_COLS:(half + 1) * HALF_COLS] = _dot(dyb[:, half * HALF_W:(half + 1) * HALF_W], c_ref[half], NT)
        _ssm_scan(g_ref, pw_ref, None, reverse=True)
        _ssm_carries(0, g_ref, pw_ref, gcarry_ref, gcm_ref, reverse=True)
        _ssm_add_carry(g_ref, pw_ref, gcm_ref, reverse=True)

        for half in range(N_HALF):
            cols = slice(half * HALF_W, (half + 1) * HALF_W)
            scols = slice(half * HALF_COLS, (half + 1) * HALF_COLS)
            gb = g_ref[:, scols].astype(bf16)
            dup_ref[:, cols] = _dot(gb, b_ref[half], NT) + d_ref[:, cols] * dy[:, cols]
            db_ref[half] += _dot(ub[:, cols], gb, TN)
            dc_ref[half] += _dot(s_ref[:, scols].astype(bf16), dyb[:, cols], TN)
        _rows_from_segments(dup_ref, stage_ref, du_ref)

        for chunk in range(N_STATE // LANE_CHUNK):
            re, im = _state_cols(chunk)
            acc_r = da_ref[:, re]
            acc_i = da_ref[:, im]
            for k in range(SSM_SEG):
                rows = slice(k * SUBLANES, (k + 1) * SUBLANES)
                if k == 0:
                    pr, pi = cm_ref[:, re], cm_ref[:, im]
                else:
                    prev = slice((k - 1) * SUBLANES, k * SUBLANES)
                    pr, pi = s_ref[prev, re], s_ref[prev, im]
                gr, gi = g_ref[rows, re], g_ref[rows, im]
                acc_r = acc_r + (gr * pr + gi * pi)
                acc_i = acc_i + (gi * pr - gr * pi)
            da_ref[:, re] = acc_r
            da_ref[:, im] = acc_i

        @pl.when(i == n - 1)
        def _():
            pltpu.sync_copy(db_ref, db_hbm)
            pltpu.sync_copy(dc_ref, dc_hbm)

    rev = functools.partial(_row_spec, rev_n=n)
    any_spec = pl.BlockSpec(memory_space=pl.ANY)
    state_rows = pltpu.VMEM((tb, 2 * N_STATE), f32)
    seg_rows = pltpu.VMEM((SUBLANES, 2 * N_STATE), f32)
    tok_rows = pltpu.VMEM((tb, SSM_W), f32)
    return pl.pallas_call(
        body, name="ssm_bwd", grid=(n,),
        in_specs=[rev(tb, SSM_W), rev(tb, SSM_W), rev(SUBLANES, 2 * N_STATE),
                  _const_spec((N_HALF, HALF_W, HALF_COLS)), _const_spec((N_HALF, HALF_COLS, HALF_W)),
                  _const_spec((SSM_SEG, 2 * N_STATE)), _const_spec((1, SSM_W))],
        out_specs=[rev(tb, SSM_W), any_spec, any_spec, _acc_spec((SUBLANES, 2 * N_STATE)), _acc_spec((1, SSM_W))],
        out_shape=[_sds((s_len, SSM_W), f32), _sds((N_HALF, HALF_W, HALF_COLS), f32),
                   _sds((N_HALF, HALF_COLS, HALF_W), f32), _sds((SUBLANES, 2 * N_STATE), f32), _sds((1, SSM_W), f32)],
        scratch_shapes=[state_rows, state_rows, seg_rows, seg_rows,
                        pltpu.VMEM((N_HALF, HALF_W, HALF_COLS), f32), pltpu.VMEM((N_HALF, HALF_COLS, HALF_W), f32),
                        tok_rows, tok_rows, tok_rows, pltpu.VMEM((SSM_W // LANES, tb, LANES), f32)],
        compiler_params=_cparams(("arbitrary",)),
    )(u, dy, cm_all, b_half, c_half, pw, d_skip)


def _branch_bwd(dya, dyc, cin, q, kv, conv_w, w_co_t, w_xo_t):
    s_len = dya.shape[0]
    tm = TOKEN_TILE
    n = s_len // tm
    halo_blocks = tm // 8

    def body(dya_ref, dyc_ref, cin_ref, cprev_ref, q_ref, kv_ref, cw_ref, wco_ref, wxo_ref,
             dconv_ref, dq_ref, dcw_ref, dkv_ref, zs_ref, dczs_ref):
        i = pl.program_id(0)
        tile = n - 1 - i

        @pl.when(i == 0)
        def _():
            dcw_ref[...] = jnp.zeros_like(dcw_ref)
            dkv_ref[...] = jnp.zeros_like(dkv_ref)
            dczs_ref[tm:tm + 8, :] = jnp.zeros((8, CONV_W), f32)

        cin = cin_ref[...]
        cb, cc, ch = cin[:, :CONV_W], cin[:, CONV_W:2 * CONV_W], cin[:, 2 * CONV_W:]
        z = cc * ch
        cprev = cprev_ref[...]
        zprev = cprev[:, CONV_W:2 * CONV_W] * cprev[:, 2 * CONV_W:]
        zs_ref[0:8, :] = jnp.where(tile == 0, 0.0, zprev)
        zs_ref[8:8 + tm, :] = z
        z1 = zs_ref[pl.ds(7, tm), :]
        z2 = zs_ref[pl.ds(6, tm), :]
        cw = cw_ref[...]
        cz = cw[0:1] * z2 + cw[1:2] * z1 + cw[2:3] * z

        dain = _dot(dya_ref[...], wco_ref[...])
        dcb = dain * cz
        dcz = dain * cb
        dczs_ref[0:tm, :] = dcz
        dcz1 = dczs_ref[pl.ds(1, tm), :]
        dcz2 = dczs_ref[pl.ds(2, tm), :]
        dz = cw[2:3] * dcz + cw[1:2] * dcz1 + cw[0:1] * dcz2
        dczs_ref[tm:tm + 8, :] = dczs_ref[0:8, :]
        dcw_ref[0:1, :] += _colsum(dcz * z2)
        dcw_ref[1:2, :] += _colsum(dcz * z1)
        dcw_ref[2:3, :] += _colsum(dcz * z)
        dconv_ref[:, :CONV_W] = dcb.astype(bf16)
        dconv_ref[:, CONV_W:2 * CONV_W] = (dz * ch).astype(bf16)
        dconv_ref[:, 2 * CONV_W:] = (dz * cc).astype(bf16)

        qb = q_ref[...]
        do = _dot(dyc_ref[...], wxo_ref[...])
        for h in range(HEADS):
            hc = slice(h * HEAD_DIM, (h + 1) * HEAD_DIM)
            vc = slice(XATTN_W + h * HEAD_DIM, XATTN_W + (h + 1) * HEAD_DIM)
            p = _attention_probs(qb, kv_ref, h)
            dob = do[:, hc].astype(bf16)
            dp = _dot(dob, kv_ref[:, vc], NT)
            dkv_ref[:, vc] += _dot(p.astype(bf16), dob, TN)
            ds = p * (dp - jnp.sum(dp * p, axis=-1, keepdims=True)) * (HEAD_DIM ** -0.5)
            dsb = ds.astype(bf16)
            dq_ref[:, hc] = _dot(dsb, kv_ref[:, hc]).astype(bf16)
            dkv_ref[:, hc] += _dot(dsb, qb[:, hc], TN)

    rev = functools.partial(_row_spec, rev_n=n)
    prev_spec = pl.BlockSpec((8, 3 * CONV_W), lambda i: (jnp.maximum((n - 1 - i) * halo_blocks - 1, 0), 0))
    return pl.pallas_call(
        body, name="branch_bwd", grid=(n,),
        in_specs=[rev(tm, D_MODEL), rev(tm, D_MODEL), rev(tm, 3 * CONV_W), prev_spec, rev(tm, XATTN_W),
                  _const_spec((MEM_LEN, 2 * XATTN_W)), _const_spec((3, CONV_W)), _const_spec((D_MODEL, CONV_W)),
                  _const_spec((D_MODEL, XATTN_W))],
        out_specs=[rev(tm, 3 * CONV_W), rev(tm, XATTN_W), _acc_spec((8, CONV_W)), _acc_spec((MEM_LEN, 2 * XATTN_W))],
        out_shape=[_sds((s_len, 3 * CONV_W), bf16), _sds((s_len, XATTN_W), bf16), _sds((8, CONV_W), f32),
                   _sds((MEM_LEN, 2 * XATTN_W), f32)],
        scratch_shapes=[pltpu.VMEM((tm + 8, CONV_W), f32), pltpu.VMEM((tm + 8, CONV_W), f32)],
        compiler_params=_cparams(("arbitrary",)),
    )(dya, dyc, cin, cin, q, kv, conv_w, w_co_t, w_xo_t)


def _in_proj_bwd(dgp, dconv, du, dq, dxp, w_in_t):
    s_len = dgp.shape[0]
    tm = TOKEN_TILE
    n = s_len // tm

    def body(dgp_ref, dconv_ref, du_ref, dq_ref, dxp_ref, win_ref, dx_ref, dproj_ref):
        dproj = jnp.concatenate([dgp_ref[...], dconv_ref[...], du_ref[...].astype(bf16), dq_ref[...]], axis=1)
        dproj_ref[...] = dproj
        dx_ref[...] = dxp_ref[...] + _dot(dproj, win_ref[...])

    return pl.pallas_call(
        body, name="in_proj_bwd", grid=(n,),
        in_specs=[_row_spec(tm, GATE_COLS), _row_spec(tm, 3 * CONV_W), _row_spec(tm, SSM_W), _row_spec(tm, XATTN_W),
                  _row_spec(tm, D_MODEL), _const_spec((IN_COLS, D_MODEL))],
        out_specs=[_row_spec(tm, D_MODEL), _row_spec(tm, IN_COLS)],
        out_shape=[_sds((s_len, D_MODEL), f32), _sds((s_len, IN_COLS), bf16)],
        compiler_params=_cparams(("parallel",)),
    )(dgp, dconv, du, dq, dxp, w_in_t)


N_CHIP = 4
CHIP_STEPS = [(1, 1), (1, 0), (0, 1), (0, 0)]


def _flip(v, d):
    return 1 - v if d else v


def _chip_order():
    x, y, _ = _mesh_place()
    return jnp.stack([2 * _flip(x, dx) + _flip(y, dy) for dx, dy in CHIP_STEPS]).astype(jnp.int32)


def _weight_grad_scatter(a_t, b, name, tm, tt):
    m, s_len = a_t.shape
    n_cols = b.shape[1]
    w = n_cols // N_DEV
    tn = 2 * w
    tm, tt = min(tm, m), min(tt, s_len)
    nm, nt = m // tm, s_len // tt
    assert m % tm == 0 and s_len % tt == 0

    def body(order_ref, a_ref, b_ref, recv_ref, acc_ref, send_ref, sib_ref,
             d2d_send, d2d_recv, ici_send, ici_recv, local_sem):
        del order_ref
        q, im, t = pl.program_id(0), pl.program_id(1), pl.program_id(2)
        x, y, c = _mesh_place()
        mesh_id = pl.DeviceIdType.MESH

        @pl.when(t == 0)
        def _():
            acc_ref[...] = jnp.zeros_like(acc_ref)

        acc_ref[...] += _dot(a_ref[...], b_ref[...])

        @pl.when(t == nt - 1)
        def _():
            rows = pl.ds(pl.multiple_of(im * tm, tm), tm)
            to_sibling = pltpu.make_async_remote_copy(
                src_ref=send_ref.at[q, 0, rows, :], dst_ref=sib_ref.at[q, rows, :],
                send_sem=d2d_send.at[q], recv_sem=d2d_recv.at[q, im],
                device_id=(x, y, 1 - c), device_id_type=mesh_id)
            for core in (0, 1):
                @pl.when(c == core)
                def _(core=core):
                    other = 1 - core
                    send_ref[q, 0, rows, :] = acc_ref[:, other * w:(other + 1) * w].astype(bf16)
                    to_sibling.start()
                    to_sibling.wait_recv()
                    both = acc_ref[:, core * w:(core + 1) * w] + sib_ref[q, rows, :].astype(f32)
                    send_ref[q, 1, rows, :] = both.astype(bf16)

            for step, (dx, dy) in enumerate(CHIP_STEPS):
                @pl.when(q == step)
                def _(step=step, dx=dx, dy=dy):
                    src, dst = send_ref.at[step, 1, rows, :], recv_ref.at[step, rows, :]
                    if dx or dy:
                        pltpu.make_async_remote_copy(
                            src_ref=src, dst_ref=dst, send_sem=ici_send.at[step], recv_sem=ici_recv.at[step],
                            device_id=(_flip(x, dx), _flip(y, dy), c), device_id_type=mesh_id).start()
                    else:
                        pltpu.make_async_copy(src, dst, local_sem).start()

        @pl.when((q == N_CHIP - 1) & (im == nm - 1) & (t == nt - 1))
        def _():
            for step, (dx, dy) in enumerate(CHIP_STEPS):
                pltpu.make_async_remote_copy(
                    src_ref=send_ref.at[step, 0], dst_ref=sib_ref.at[step],
                    send_sem=d2d_send.at[step], recv_sem=d2d_recv.at[step, 0],
                    device_id=(x, y, 1 - c), device_id_type=mesh_id).wait_send()
                src, dst = send_ref.at[step, 1], recv_ref.at[step]
                if dx or dy:
                    pltpu.make_async_remote_copy(
                        src_ref=src, dst_ref=dst, send_sem=ici_send.at[step], recv_sem=ici_recv.at[step],
                        device_id=(_flip(x, dx), _flip(y, dy), c), device_id_type=mesh_id).wait()
                else:
                    pltpu.make_async_copy(src, dst, local_sem).wait()

    grid_spec = pltpu.PrefetchScalarGridSpec(
        num_scalar_prefetch=1, grid=(N_CHIP, nm, nt),
        in_specs=[pl.BlockSpec((tm, tt), lambda q, im, t, order: (im, t)),
                  pl.BlockSpec((tt, tn), lambda q, im, t, order: (t, order[q]))],
        out_specs=pl.BlockSpec(memory_space=pl.ANY),
        scratch_shapes=[pltpu.VMEM((tm, tn), f32), pltpu.VMEM((N_CHIP, 2, m, w), bf16), pltpu.VMEM((N_CHIP, m, w), bf16),
                        pltpu.SemaphoreType.DMA((N_CHIP,)), pltpu.SemaphoreType.DMA((N_CHIP, nm)),
                        pltpu.SemaphoreType.DMA((N_CHIP - 1,)), pltpu.SemaphoreType.DMA((N_CHIP - 1,)),
                        pltpu.SemaphoreType.DMA])
    return pl.pallas_call(
        body, name=name, grid_spec=grid_spec,
        out_shape=_sds((N_CHIP, m, w), bf16),
        compiler_params=_cparams(("arbitrary", "arbitrary", "arbitrary")),
    )(_chip_order(), a_t, b)


def _adamw(w, g, m, v):
    m = ADAM_B1 * m + (1.0 - ADAM_B1) * g
    v = ADAM_B2 * v + (1.0 - ADAM_B2) * jnp.square(g)
    m_hat = m / (1.0 - ADAM_B1 ** ADAM_STEP)
    v_hat = v / (1.0 - ADAM_B2 ** ADAM_STEP)
    delta = -ADAM_LR * (m_hat / (jnp.sqrt(v_hat) + ADAM_EPS) + ADAM_WD * w)
    return delta, m, v


def _sum_parts(p_ref):
    g = p_ref[0].astype(f32)
    for j in range(1, p_ref.shape[0]):
        g = g + p_ref[j].astype(f32)
    return g


def _adamw_update(w, m, v, parts, name, transposed):
    rows, cols = w.shape
    n_parts = parts.shape[0]
    if transposed:
        tr = LANES
        p_spec = pl.BlockSpec((n_parts, cols, tr), lambda i: (0, 0, i))
    else:
        tr = next(t for t in (256, 128, 64, 32, 16, 8) if rows % t == 0)
        p_spec = pl.BlockSpec((n_parts, tr, cols), lambda i: (0, i, 0))
    spec = pl.BlockSpec((tr, cols), lambda i: (i, 0))

    def body(w_ref, p_ref, m_ref, v_ref, g_ref, d_ref, nm_ref, nv_ref):
        g = _sum_parts(p_ref)
        if transposed:
            g = g.T
        g_ref[...] = g
        d_ref[...], nm_ref[...], nv_ref[...] = _adamw(w_ref[...], g, m_ref[...], v_ref[...])

    return pl.pallas_call(
        body, name=name, grid=(rows // tr,),
        in_specs=[spec, p_spec, spec, spec], out_specs=[spec] * 4,
        out_shape=[_sds((rows, cols), f32)] * 4,
        compiler_params=_cparams(("parallel",)),
    )(w, parts, m, v)


SMALL_GROUPS = [
    (["b_gate", "ln1_g", "ln1_b", "b_up", "b_down", "ln2_g", "ln2_b", "ssm_d"], 1),
    (["ssm_lam_re", "ssm_lam_im", "ssm_c_re", "ssm_c_im"], 0),
    (["ssm_b_re", "ssm_b_im"], 0),
    (["conv_w"], 0),
    (["ssm_log_dt"], 0),
]


def _sum_small(group_parts):
    def body(*refs):
        n = len(refs) // 2
        for p_ref, o_ref in zip(refs[:n], refs[n:]):
            o_ref[...] = _sum_parts(p_ref)

    return pl.pallas_call(
        body, name="sum_small",
        out_shape=[_sds(p.shape[1:], f32) for p in group_parts],
        compiler_params=_cparams(),
    )(*group_parts)


def _adamw_small(ws, ms, vs, group_sums):
    names = [k for group, _ in SMALL_GROUPS for k in group]
    n = len(names)

    def body(*refs):
        w_refs, m_refs, v_refs = (dict(zip(names, refs[j * n:(j + 1) * n])) for j in range(3))
        p_refs = refs[3 * n:3 * n + len(SMALL_GROUPS)]
        out_refs = [dict(zip(names, refs[3 * n + len(SMALL_GROUPS) + j * n:][:n])) for j in range(4)]
        for (group, axis), p_ref in zip(SMALL_GROUPS, p_refs):
            total = p_ref[...]
            off = 0
            for k in group:
                size = SMALL[k][axis]
                g = total[:, off:off + size] if axis == 1 else total[off:off + size, :]
                off += size
                d, nm, nv = _adamw(w_refs[k][...], g, m_refs[k][...], v_refs[k][...])
                for j, val in enumerate((g, d, nm, nv)):
                    out_refs[j][k][...] = val

    res = pl.pallas_call(
        body, name="adamw_small",
        out_shape=[_sds(SMALL[k], f32) for _ in range(4) for k in names],
        compiler_params=_cparams(),
    )(*[ws[k] for k in names], *[ms[k] for k in names], *[vs[k] for k in names], *group_sums)
    return [dict(zip(names, res[j * n:(j + 1) * n])) for j in range(4)]


def _ssm_discretize(lam_re, lam_im, log_dt, b_re, b_im):
    dt = jnp.exp(log_dt)[:, None]
    mag = jnp.exp(lam_re * dt)
    abar_r = mag * jnp.cos(lam_im * dt)
    abar_i = mag * jnp.sin(lam_im * dt)
    den = lam_re * lam_re + lam_im * lam_im
    nr = abar_r - 1.0
    ni = abar_i
    kr = (nr * lam_re + ni * lam_im) / den
    ki = (ni * lam_re - nr * lam_im) / den
    bbar_r = kr[..., None] * b_re - ki[..., None] * b_im
    bbar_i = kr[..., None] * b_im + ki[..., None] * b_re
    return abar_r, abar_i, bbar_r, bbar_i


def _state_layout(re, im):
    parts = []
    for half in range(N_HALF):
        cols = slice(half * HALF_STATE, (half + 1) * HALF_STATE)
        parts += [re[..., cols], im[..., cols]]
    return jnp.concatenate(parts, axis=-1)


def _state_unlayout(a):
    re = jnp.concatenate([a[..., _half_cols(h)[0]] for h in range(N_HALF)], axis=-1)
    im = jnp.concatenate([a[..., _half_cols(h)[1]] for h in range(N_HALF)], axis=-1)
    return re, im


def _abar_powers(abar_r, abar_i):
    pr, pi = abar_r.reshape(1, N_STATE), abar_i.reshape(1, N_STATE)
    while pr.shape[0] < SSM_SEG:
        tr, ti = pr[-1:], pi[-1:]
        pr, pi = (jnp.concatenate([pr, pr * tr - pi * ti], axis=0), jnp.concatenate([pi, pr * ti + pi * tr], axis=0))
    return _state_layout(pr, pi)


HALF_GROUPS = SSM_GROUPS // N_HALF


def _half_block_diag(blocks):
    _, r, c = blocks.shape
    eye = jnp.eye(HALF_GROUPS, dtype=blocks.dtype)
    b4 = blocks.reshape(N_HALF, HALF_GROUPS, r, c)
    return jnp.einsum("ngrc,gk->ngrkc", b4, eye).reshape(N_HALF, HALF_GROUPS * r, HALF_GROUPS * c)


def _half_diag_blocks(mat, r, c):
    eye = jnp.eye(HALF_GROUPS, dtype=mat.dtype)
    m5 = mat.reshape(N_HALF, HALF_GROUPS, r, HALF_GROUPS, c)
    return jnp.einsum("ngrkc,gk->ngrc", m5, eye).reshape(SSM_GROUPS, r, c)


BIG = ["w_in", "w_conv_out", "w_glu", "w_kv", "w_xattn_out", "w_out", "w_up", "w_down"]
COL_SHARDED = ["w_in", "w_conv_out", "w_glu", "w_xattn_out", "w_up"]
SMALL = {"b_gate": (1, GATE_COLS), "conv_w": (3, CONV_W), "ssm_lam_re": (SSM_GROUPS, SSM_STATE),
         "ssm_lam_im": (SSM_GROUPS, SSM_STATE), "ssm_log_dt": (1, SSM_GROUPS),
         "ssm_b_re": (N_STATE, SSM_GROUP), "ssm_b_im": (N_STATE, SSM_GROUP),
         "ssm_c_re": (SSM_W, SSM_STATE), "ssm_c_im": (SSM_W, SSM_STATE), "ssm_d": (1, SSM_W),
         "ln1_g": (1, D_MODEL), "ln1_b": (1, D_MODEL), "b_up": (1, D_FF), "b_down": (1, D_MODEL),
         "ln2_g": (1, D_MODEL), "ln2_b": (1, D_MODEL)}
WEIGHTS = ["w_in", "b_gate", "conv_w", "w_conv_out", "ssm_lam_re", "ssm_lam_im", "ssm_log_dt", "ssm_b_re", "ssm_b_im",
           "ssm_c_re", "ssm_c_im", "ssm_d", "w_glu", "w_kv", "w_xattn_out", "w_out", "ln1_g", "ln1_b", "w_up", "b_up",
           "w_down", "b_down", "ln2_g", "ln2_b"]


def _local_step(x, mem, tgt, full, late, small):
    lam_re, lam_im, log_dt = small["ssm_lam_re"], small["ssm_lam_im"], small["ssm_log_dt"].reshape(SSM_GROUPS)
    b_shape = (SSM_GROUPS, SSM_STATE, SSM_GROUP)
    c_shape = (SSM_GROUPS, SSM_GROUP, SSM_STATE)
    disc, disc_vjp = jax.vjp(_ssm_discretize, lam_re, lam_im, log_dt,
                             small["ssm_b_re"].reshape(b_shape), small["ssm_b_im"].reshape(b_shape))
    abar_r, abar_i, bbar_r, bbar_i = disc
    pw = _abar_powers(abar_r, abar_i)
    c_re, c_im = small["ssm_c_re"].reshape(c_shape), small["ssm_c_im"].reshape(c_shape)
    b_half = jnp.concatenate([_half_block_diag(bbar_r.transpose(0, 2, 1)), _half_block_diag(bbar_i.transpose(0, 2, 1))],
                             axis=2).astype(bf16)
    c_half = jnp.concatenate([_half_block_diag(c_re.transpose(0, 2, 1)), -_half_block_diag(c_im.transpose(0, 2, 1))],
                             axis=1).astype(bf16)

    stack = lambda a: a.reshape(-1, a.shape[-1])
    kv, memb = _kv_proj(mem, full["w_kv"])
    (xbt, g, cin, u, q, ain, ob, aint, obt), side = _in_proj(
        x, full["w_in"], small["b_gate"], small["conv_w"], kv,
        [late[k] for k in ("w_glu", "w_conv_out", "w_xattn_out", "w_out", "w_up")])
    w_glu_t, w_co_t, w_xo_t, w_out, w_up_t = (stack(a) for a in side)
    y_ssm, cm_all, side = _ssm_fwd(u, b_half, c_half, pw, small["ssm_d"], [late["w_down"]])
    w_down = stack(side[0])
    ysbt, mb, xhat1, rstd1 = _mid_fwd(y_ssm, g, ain, ob, x, w_glu_t, w_co_t, w_xo_t, w_out,
                                      small["ln1_g"], small["ln1_b"])
    (x1bt, hdn, dr2bt, dpre, dx1, loss, dl2g, dl2b, dbdn, dbup) = _mlp_fwd_bwd(
        xhat1, tgt, small["ln1_g"], small["ln1_b"], w_up_t, small["b_up"], w_down,
        small["b_down"], small["ln2_g"], small["ln2_b"])
    recv = {}
    recv["w_down"] = _weight_grad_scatter(dr2bt, hdn, "dw_down", tm=512, tt=1024)
    recv["w_up"] = _weight_grad_scatter(x1bt, dpre, "dw_up", tm=512, tt=1024)
    (dxp, dr1bt, dgp, dya, dyc, dglu, dyssm, dl1g, dl1b, dbg) = _mid_bwd(
        dx1, xhat1, rstd1, g, ain, ob, y_ssm, small["ln1_g"], w_out, w_glu_t, w_co_t, w_xo_t)
    recv["w_out"] = _weight_grad_scatter(dr1bt, mb, "dw_out", tm=512, tt=2048)
    recv["w_glu"] = _weight_grad_scatter(ysbt, dglu, "dw_glu", tm=512, tt=2048)
    du, db_half, dc_half, da8, dd = _ssm_bwd(u, dyssm, cm_all, b_half, c_half, pw, small["ssm_d"])
    dconv, dq, dcw8, dkv = _branch_bwd(dya, dyc, cin, q, kv, small["conv_w"], w_co_t, w_xo_t)
    recv["w_conv_out"] = _weight_grad_scatter(aint, dya, "dw_conv_out", tm=512, tt=2048)
    recv["w_xattn_out"] = _weight_grad_scatter(obt, dyc, "dw_xattn_out", tm=512, tt=2048)
    recv["w_kv"] = _weight_grad_scatter(dkv.T.astype(bf16), memb, "dw_kv", tm=512, tt=MEM_LEN)
    dx, dproj = _in_proj_bwd(dgp, dconv, du, dq, dxp, full["w_in"])
    recv["w_in"] = _weight_grad_scatter(xbt, dproj, "dw_in", tm=512, tt=1024)

    dabar_r, dabar_i = _state_unlayout(jnp.sum(da8, axis=0))
    dbbar_r = _half_diag_blocks(db_half[:, :, :HALF_STATE], SSM_GROUP, SSM_STATE).transpose(0, 2, 1)
    dbbar_i = _half_diag_blocks(db_half[:, :, HALF_STATE:], SSM_GROUP, SSM_STATE).transpose(0, 2, 1)
    g_shape = (SSM_GROUPS, SSM_STATE)
    dlam_re, dlam_im, dlog_dt, db_re, db_im = disc_vjp(
        (dabar_r.reshape(g_shape), dabar_i.reshape(g_shape), dbbar_r, dbbar_i))
    dc_re = _half_diag_blocks(dc_half[:, :HALF_STATE, :], SSM_STATE, SSM_GROUP).transpose(0, 2, 1)
    dc_im = -_half_diag_blocks(dc_half[:, HALF_STATE:, :], SSM_STATE, SSM_GROUP).transpose(0, 2, 1)

    small_grads = {
        "b_gate": dbg, "conv_w": dcw8[0:3], "ssm_lam_re": dlam_re, "ssm_lam_im": dlam_im, "ssm_log_dt": dlog_dt,
        "ssm_b_re": db_re, "ssm_b_im": db_im, "ssm_c_re": dc_re, "ssm_c_im": dc_im, "ssm_d": dd,
        "ln1_g": dl1g, "ln1_b": dl1b, "b_up": dbup, "b_down": dbdn, "ln2_g": dl2g, "ln2_b": dl2b,
    }
    small_grads = {k: a.reshape(SMALL[k]) for k, a in small_grads.items()}
    return loss[0, 0], dx, recv, small_grads


def kernel(x, mem, w_in, b_gate, conv_w, w_conv_out, ssm_lam_re, ssm_lam_im, ssm_log_dt, ssm_b_re, ssm_b_im, ssm_c_re, ssm_c_im, ssm_d, w_glu, w_kv, w_xattn_out, w_out, ln1_g, ln1_b, w_up, b_up, w_down, b_down, ln2_g, ln2_b, loss_target, m_w_in, m_b_gate, m_conv_w, m_w_conv_out, m_ssm_lam_re, m_ssm_lam_im, m_ssm_log_dt, m_ssm_b_re, m_ssm_b_im, m_ssm_c_re, m_ssm_c_im, m_ssm_d, m_w_glu, m_w_kv, m_w_xattn_out, m_w_out, m_ln1_g, m_ln1_b, m_w_up, m_b_up, m_w_down, m_b_down, m_ln2_g, m_ln2_b, v_w_in, v_b_gate, v_conv_w, v_w_conv_out, v_ssm_lam_re, v_ssm_lam_im, v_ssm_log_dt, v_ssm_b_re, v_ssm_b_im, v_ssm_c_re, v_ssm_c_im, v_ssm_d, v_w_glu, v_w_kv, v_w_xattn_out, v_w_out, v_ln1_g, v_ln1_b, v_w_up, v_b_up, v_w_down, v_b_down, v_ln2_g, v_ln2_b):
    w = dict(w_in=w_in, b_gate=b_gate, conv_w=conv_w, w_conv_out=w_conv_out, ssm_lam_re=ssm_lam_re,
             ssm_lam_im=ssm_lam_im, ssm_log_dt=ssm_log_dt, ssm_b_re=ssm_b_re, ssm_b_im=ssm_b_im, ssm_c_re=ssm_c_re,
             ssm_c_im=ssm_c_im, ssm_d=ssm_d, w_glu=w_glu, w_kv=w_kv, w_xattn_out=w_xattn_out, w_out=w_out,
             ln1_g=ln1_g, ln1_b=ln1_b, w_up=w_up, b_up=b_up, w_down=w_down, b_down=b_down, ln2_g=ln2_g, ln2_b=ln2_b)
    m = dict(w_in=m_w_in, b_gate=m_b_gate, conv_w=m_conv_w, w_conv_out=m_w_conv_out, ssm_lam_re=m_ssm_lam_re,
             ssm_lam_im=m_ssm_lam_im, ssm_log_dt=m_ssm_log_dt, ssm_b_re=m_ssm_b_re, ssm_b_im=m_ssm_b_im,
             ssm_c_re=m_ssm_c_re, ssm_c_im=m_ssm_c_im, ssm_d=m_ssm_d, w_glu=m_w_glu, w_kv=m_w_kv,
             w_xattn_out=m_w_xattn_out, w_out=m_w_out, ln1_g=m_ln1_g, ln1_b=m_ln1_b, w_up=m_w_up, b_up=m_b_up,
             w_down=m_w_down, b_down=m_b_down, ln2_g=m_ln2_g, ln2_b=m_ln2_b)
    v = dict(w_in=v_w_in, b_gate=v_b_gate, conv_w=v_conv_w, w_conv_out=v_w_conv_out, ssm_lam_re=v_ssm_lam_re,
             ssm_lam_im=v_ssm_lam_im, ssm_log_dt=v_ssm_log_dt, ssm_b_re=v_ssm_b_re, ssm_b_im=v_ssm_b_im,
             ssm_c_re=v_ssm_c_re, ssm_c_im=v_ssm_c_im, ssm_d=v_ssm_d, w_glu=v_w_glu, w_kv=v_w_kv,
             w_xattn_out=v_w_xattn_out, w_out=v_w_out, ln1_g=v_ln1_g, ln1_b=v_ln1_b, w_up=v_w_up, b_up=v_b_up,
             w_down=v_w_down, b_down=v_b_down, ln2_g=v_ln2_g, ln2_b=v_ln2_b)
    out_shapes = {k: a.shape for k, a in w.items()}
    shard2d = lambda k, a: a.reshape((3, CONV_W // N_DEV) if k == "conv_w" else SMALL[k]) if k in SMALL else a[0]
    w, m, v = ({k: shard2d(k, a) for k, a in d.items()} for d in (w, m, v))

    shards = {k: w[k].T.astype(bf16) if k in COL_SHARDED else w[k].astype(bf16) for k in BIG}
    conv_pad = jnp.pad(w["conv_w"], ((0, 5), (0, LANES - CONV_W // N_DEV)))
    early = ["w_in", "w_kv"]
    gathered = _all_gather([shards[k] for k in early] + [conv_pad], "gather_weights")
    full = {k: a.reshape(-1, a.shape[-1]) for k, a in zip(early, gathered[:-1])}
    late = {k: shards[k] for k in BIG if k not in early}
    conv_full = gathered[-1][:, :3, :CONV_W // N_DEV].transpose(1, 0, 2).reshape(3, CONV_W)
    small = {k: (conv_full if k == "conv_w" else w[k]) for k in SMALL}

    loss, dx, recv, small_grads = _local_step(x[0], mem[0], loss_target[0], full, late, small)

    grads, deltas, new_m, new_v = {}, {}, {}, {}
    for k in BIG:
        res = _adamw_update(w[k], m[k], v[k], recv[k], "adamw_" + k, transposed=k not in COL_SHARDED)
        grads[k], deltas[k], new_m[k], new_v[k] = res

    stacks = [jnp.concatenate([small_grads[k] for k in group], axis=axis) if len(group) > 1 else small_grads[group[0]]
              for group, axis in SMALL_GROUPS]
    dense = lambda a: a.reshape(-1, LANES) if a.shape[1] < LANES and a.size % LANES == 0 else a
    group_sums = _sum_small(_all_gather([dense(a) for a in stacks], "gather_small_grads"))
    group_sums = [s.reshape(a.shape) for s, a in zip(group_sums, stacks)]
    widen = lambda k, a: jnp.tile(a, (1, N_DEV)) if k == "conv_w" else a
    res = _adamw_small(small, {k: widen(k, m[k]) for k in SMALL}, {k: widen(k, v[k]) for k in SMALL}, group_sums)
    dev = _slot(_mesh_place())
    for d, small_res in zip((grads, deltas, new_m, new_v), res):
        for k, a in small_res.items():
            if k == "conv_w":
                a = lax.dynamic_slice_in_dim(a, dev * (CONV_W // N_DEV), CONV_W // N_DEV, axis=1)
            d[k] = a

    loss = lax.psum(loss, ("x", "y", "c"))
    outs = [loss, dx[None]]
    for d in (grads, deltas, new_m, new_v):
        outs += [d[k].reshape(out_shapes[k]) for k in WEIGHTS]
    return tuple(outs)
```

```python
import functools
import math

import jax
import jax.numpy as jnp
from jax import lax
from jax.experimental import pallas as pl
from jax.experimental.pallas import tpu as pltpu

f32 = jnp.float32
bf16 = jnp.bfloat16

D_MODEL = 1024
MEM_LEN = 256
GATE_COLS = 3 * D_MODEL
CONV_W = 512
SSM_W = 512
XATTN_W = 512
HEADS = 4
HEAD_DIM = 128
D_FF = 4096
IN_COLS = GATE_COLS + 3 * CONV_W + SSM_W + XATTN_W
SSM_GROUPS = 32
SSM_GROUP = 16
SSM_STATE = 64
N_STATE = SSM_GROUPS * SSM_STATE
ALPHA = 2.0 ** 0.25
LN_EPS = 1e-5
N_DEV = 8

ADAM_LR = 0.001
ADAM_B1 = 0.9
ADAM_B2 = 0.999
ADAM_EPS = 1e-08
ADAM_WD = 0.01
ADAM_STEP = 10

VMEM_LIMIT_V7X = 56 * 2 ** 20
SUBLANES = 8
LANES = 128

TOKEN_TILE = 256
SSM_BLOCK = 256
SSM_SEG = SSM_BLOCK // SUBLANES
LANE_CHUNK = 512
N_HALF = 2
HALF_W = SSM_W // N_HALF
HALF_STATE = N_STATE // N_HALF
HALF_COLS = 2 * HALF_STATE

NT = (((1,), (1,)), ((), ()))
TN = (((0,), (0,)), ((), ()))
NN = (((1,), (0,)), ((), ()))


def _dot(a, b, dims=NN):
    return lax.dot_general(a, b, dims, preferred_element_type=f32)


def _cparams(sem=None):
    return pltpu.CompilerParams(dimension_semantics=sem, vmem_limit_bytes=VMEM_LIMIT_V7X)


def _row_spec(tm, cols, rev_n=None):
    if rev_n is None:
        return pl.BlockSpec((tm, cols), lambda i: (i, 0))
    return pl.BlockSpec((tm, cols), lambda i: (rev_n - 1 - i, 0))


def _col_spec(rows, tm):
    return pl.BlockSpec((rows, tm), lambda i: (0, i))


def _const_spec(shape):
    nd = len(shape)
    return pl.BlockSpec(shape, lambda *_: (0,) * nd, pipeline_mode=pl.Buffered(1))


def _acc_spec(shape):
    nd = len(shape)
    return pl.BlockSpec(shape, lambda *_: (0,) * nd)


def _sds(shape, dtype):
    return jax.ShapeDtypeStruct(shape, dtype)


def _gelu(x):
    c = math.sqrt(2.0 / math.pi)
    return 0.5 * x * (1.0 + jnp.tanh(c * (x + 0.044715 * x * x * x)))


def _gelu_grad(x):
    c = math.sqrt(2.0 / math.pi)
    t = jnp.tanh(c * (x + 0.044715 * x * x * x))
    return 0.5 * (1.0 + t) + 0.5 * x * (1.0 - t * t) * c * (1.0 + 3.0 * 0.044715 * x * x)


def _colsum(a):
    return jnp.sum(a, axis=0, keepdims=True)


def _mesh_place():
    return lax.axis_index("x"), lax.axis_index("y"), lax.axis_index("c")


def _slot(p):
    return 4 * p[0] + 2 * p[1] + p[2]


def _other_devices(me):
    x, y, c = me
    flip = lambda v, d: 1 - v if d else v
    return [(flip(x, dx), flip(y, dy), flip(c, dc)) for dx in (0, 1) for dy in (0, 1) for dc in (0, 1)][1:]


def _all_gather(blocks, name):
    n = len(blocks)

    def body(*refs):
        ins, outs = refs[:n], refs[n:2 * n]
        send_sems, recv_sems, local_sems = refs[2 * n:]
        x, y, c = _mesh_place()
        me, sibling = (x, y, c), (x, y, 1 - c)
        chips = [(1 - x, y), (x, 1 - y), (1 - x, 1 - y)]

        def copy(a, k, block, to, src=None):
            rows = outs[a].at[_slot(block)]
            return pltpu.make_async_remote_copy(
                src_ref=rows if src is None else src, dst_ref=rows,
                send_sem=send_sems.at[a, k], recv_sem=recv_sems.at[a, k],
                device_id=to, device_id_type=pl.DeviceIdType.MESH)

        mine = [pltpu.make_async_copy(ins[a], outs[a].at[_slot(me)], local_sems.at[a]) for a in range(n)]
        for cp in mine:
            cp.start()
        first = []
        for a in range(n):
            first.append(copy(a, 0, me, sibling, src=ins[a]))
            first += [copy(a, 1 + j, me, (*chip, c), src=ins[a]) for j, chip in enumerate(chips)]
        for cp in first:
            cp.start()
        passed = []
        for a in range(n):
            for j, chip in enumerate(chips):
                copy(a, 1 + j, (*chip, c), me).wait_recv()
                fwd = copy(a, 4 + j, (*chip, c), sibling)
                fwd.start()
                passed.append(fwd)
        for a in range(n):
            copy(a, 0, sibling, me).wait_recv()
            for j, chip in enumerate(chips):
                copy(a, 4 + j, (*chip, 1 - c), me).wait_recv()
        for cp in first + passed:
            cp.wait_send()
        for cp in mine:
            cp.wait()

    any_spec = pl.BlockSpec(memory_space=pl.ANY)
    return pl.pallas_call(
        body, name=name,
        out_shape=[_sds((N_DEV,) + b.shape, b.dtype) for b in blocks],
        in_specs=[any_spec] * n, out_specs=[any_spec] * n,
        scratch_shapes=[pltpu.SemaphoreType.DMA((n, 7)), pltpu.SemaphoreType.DMA((n, 7)),
                        pltpu.SemaphoreType.DMA((n,))],
    )(*blocks)


def _side_gather_copies(ins, outs, send_sems, recv_sems, local_sems):
    me = _mesh_place()
    copies = []
    for a, (src, dst) in enumerate(zip(ins, outs)):
        copies.append(pltpu.make_async_copy(src, dst.at[_slot(me)], local_sems.at[a]))
        for k, peer in enumerate(_other_devices(me)):
            copies.append(pltpu.make_async_remote_copy(
                src_ref=src, dst_ref=dst.at[_slot(me)], send_sem=send_sems.at[a, k], recv_sem=recv_sems.at[a, k],
                device_id=peer, device_id_type=pl.DeviceIdType.MESH))
    return copies


def _side_gather_specs(blocks):
    n = len(blocks)
    any_spec = pl.BlockSpec(memory_space=pl.ANY)
    return ([any_spec] * n, [_sds((N_DEV,) + b.shape, b.dtype) for b in blocks],
            [pltpu.SemaphoreType.DMA((n, N_DEV - 1)), pltpu.SemaphoreType.DMA((n, N_DEV - 1)),
             pltpu.SemaphoreType.DMA((n,))])


def _kv_proj(mem, w_kv):
    def body(mem_ref, w_ref, kv_ref, memb_ref):
        mb = mem_ref[...].astype(bf16)
        memb_ref[...] = mb
        kv_ref[...] = _dot(mb, w_ref[...]).astype(bf16)

    return pl.pallas_call(
        body, name="kv_proj",
        out_shape=[_sds((MEM_LEN, 2 * XATTN_W), bf16), _sds((MEM_LEN, D_MODEL), bf16)],
        compiler_params=_cparams(),
    )(mem, w_kv)


def _attention_probs(qb, kv_ref, h):
    kh = kv_ref[:, h * HEAD_DIM:(h + 1) * HEAD_DIM]
    s = _dot(qb[:, h * HEAD_DIM:(h + 1) * HEAD_DIM], kh, NT) * (HEAD_DIM ** -0.5)
    e = jnp.exp(s - jnp.max(s, axis=-1, keepdims=True))
    return e / jnp.sum(e, axis=-1, keepdims=True)


def _in_proj(x, w_in_t, b_gate, conv_w, kv, side_blocks):
    s_len = x.shape[0]
    tm = TOKEN_TILE
    n = s_len // tm
    ns = len(side_blocks)
    side_in_specs, side_shapes, side_sems = _side_gather_specs(side_blocks)

    def body(*refs):
        (x_ref, win_ref, bg_ref, cw_ref, kv_ref) = refs[:5]
        side_ins = refs[5:5 + ns]
        (xbt_ref, g_ref, cin_ref, u_ref, q_ref, ain_ref, o_ref, aint_ref, ot_ref) = refs[5 + ns:14 + ns]
        side_outs = refs[14 + ns:14 + 2 * ns]
        zs_ref = refs[14 + 2 * ns]
        side = _side_gather_copies(side_ins, side_outs, *refs[15 + 2 * ns:])
        i = pl.program_id(0)

        @pl.when(i == 0)
        def _():
            for cp in side:
                cp.start()

        xb = x_ref[...].astype(bf16)
        xbt_ref[...] = xb.T
        proj = _dot(xb, win_ref[...], NT)
        g_ref[...] = jax.nn.sigmoid(proj[:, :GATE_COLS] + bg_ref[...])
        cin = proj[:, GATE_COLS:GATE_COLS + 3 * CONV_W]
        cin_ref[...] = cin
        u_ref[...] = proj[:, GATE_COLS + 3 * CONV_W:GATE_COLS + 3 * CONV_W + SSM_W]
        qb = proj[:, IN_COLS - XATTN_W:].astype(bf16)
        q_ref[...] = qb

        cb, cc, ch = cin[:, :CONV_W], cin[:, CONV_W:2 * CONV_W], cin[:, 2 * CONV_W:]
        z = cc * ch

        @pl.when(i == 0)
        def _():
            zs_ref[0:8, :] = jnp.zeros((8, CONV_W), f32)

        zs_ref[8:8 + tm, :] = z
        z1 = zs_ref[pl.ds(7, tm), :]
        z2 = zs_ref[pl.ds(6, tm), :]
        cw = cw_ref[...]
        cz = cw[0:1] * z2 + cw[1:2] * z1 + cw[2:3] * z
        zs_ref[0:8, :] = zs_ref[tm:tm + 8, :]
        ain = (cb * cz).astype(bf16)
        ain_ref[...] = ain
        aint_ref[...] = ain.T

        outs = []
        for h in range(HEADS):
            p = _attention_probs(qb, kv_ref, h)
            vh = kv_ref[:, XATTN_W + h * HEAD_DIM:XATTN_W + (h + 1) * HEAD_DIM]
            outs.append(_dot(p.astype(bf16), vh))
        ob = jnp.concatenate(outs, axis=1).astype(bf16)
        o_ref[...] = ob
        ot_ref[...] = ob.T

        @pl.when(i == n - 1)
        def _():
            for cp in side:
                cp.wait()

    row_cols = [(GATE_COLS, f32), (3 * CONV_W, f32), (SSM_W, f32), (XATTN_W, bf16), (CONV_W, bf16), (XATTN_W, bf16)]
    t_rows = [D_MODEL, CONV_W, XATTN_W]
    outs = pl.pallas_call(
        body, name="in_proj", grid=(n,),
        in_specs=[_row_spec(tm, D_MODEL), _const_spec((IN_COLS, D_MODEL)), _const_spec((1, GATE_COLS)),
                  _const_spec((3, CONV_W)), _const_spec((MEM_LEN, 2 * XATTN_W))] + side_in_specs,
        out_specs=([_col_spec(t_rows[0], tm)] + [_row_spec(tm, c) for c, _ in row_cols]
                   + [_col_spec(t_rows[1], tm), _col_spec(t_rows[2], tm)] + side_in_specs),
        out_shape=([_sds((t_rows[0], s_len), bf16)] + [_sds((s_len, c), dt) for c, dt in row_cols]
                   + [_sds((t_rows[1], s_len), bf16), _sds((t_rows[2], s_len), bf16)] + side_shapes),
        scratch_shapes=[pltpu.VMEM((tm + 8, CONV_W), f32)] + side_sems,
        compiler_params=_cparams(("arbitrary",)),
    )(x, w_in_t, b_gate, conv_w, kv, *side_blocks)
    return outs[:9], outs[9:]


def _state_cols(chunk):
    half, off = divmod(chunk * LANE_CHUNK, HALF_STATE)
    lo = half * HALF_COLS + off
    return slice(lo, lo + LANE_CHUNK), slice(lo + HALF_STATE, lo + HALF_STATE + LANE_CHUNK)


def _half_cols(half):
    lo = half * HALF_COLS
    return slice(lo, lo + HALF_STATE), slice(lo + HALF_STATE, lo + HALF_COLS)


def _rows_to_segments(src_ref, stage_ref, dst_ref):
    nc = SSM_W // LANES
    for c in range(nc):
        stage_ref[c] = src_ref[:, c * LANES:(c + 1) * LANES]
    for c in range(nc):
        for k in range(SSM_SEG):
            dst_ref[k * SUBLANES:(k + 1) * SUBLANES, c * LANES:(c + 1) * LANES] = (
                stage_ref[c, pl.ds(k, SUBLANES, stride=SSM_SEG), :])


def _rows_from_segments(src_ref, stage_ref, dst_ref):
    nc = SSM_W // LANES
    for c in range(nc):
        for k in range(SSM_SEG):
            stage_ref[c, pl.ds(k, SUBLANES, stride=SSM_SEG), :] = (
                src_ref[k * SUBLANES:(k + 1) * SUBLANES, c * LANES:(c + 1) * LANES])
    for c in range(nc):
        dst_ref[:, c * LANES:(c + 1) * LANES] = stage_ref[c]


def _ssm_scan(s_ref, pw_ref, init_ref, reverse, unroll):
    for chunk in range(N_STATE // LANE_CHUNK):
        re, im = _state_cols(chunk)
        ar = jnp.broadcast_to(pw_ref[0:1, re], (SUBLANES, LANE_CHUNK))
        ai = jnp.broadcast_to(pw_ref[0:1, im], (SUBLANES, LANE_CHUNK))
        if reverse:
            ai = -ai

        def step(j, carry, re=re, im=im, ar=ar, ai=ai):
            sr, si = carry
            k = (SSM_SEG - 1 - j) if reverse else j
            r0 = pl.multiple_of(k * SUBLANES, SUBLANES)
            nr = ar * sr - ai * si + s_ref[pl.ds(r0, SUBLANES), re]
            ni = ar * si + ai * sr + s_ref[pl.ds(r0, SUBLANES), im]
            s_ref[pl.ds(r0, SUBLANES), re] = nr
            s_ref[pl.ds(r0, SUBLANES), im] = ni
            return nr, ni

        if init_ref is None:
            init = (jnp.zeros((SUBLANES, LANE_CHUNK), f32),) * 2
        else:
            init = (init_ref[:, re], init_ref[:, im])
        lax.fori_loop(0, SSM_SEG, step, init, unroll=unroll)


def _ssm_add_carry(s_ref, pw_ref, cm_ref, reverse):
    for chunk in range(N_STATE // LANE_CHUNK):
        re, im = _state_cols(chunk)
        cr, ci = cm_ref[:, re], cm_ref[:, im]
        for k in range(SSM_SEG):
            pk = (SSM_SEG - 1 - k) if reverse else k
            pr = pw_ref[pk:pk + 1, re]
            pi = pw_ref[pk:pk + 1, im]
            if reverse:
                pi = -pi
            rows = slice(k * SUBLANES, (k + 1) * SUBLANES)
            s_ref[rows, re] = s_ref[rows, re] + (pr * cr - pi * ci)
            s_ref[rows, im] = s_ref[rows, im] + (pr * ci + pi * cr)


def _ssm_carries(first_row, s_ref, pw_ref, carry_ref, cm_ref, reverse):
    order = range(SUBLANES - 1, -1, -1) if reverse else range(SUBLANES)
    for half in range(N_HALF):
        re, im = _half_cols(half)
        a_r, a_i = pw_ref[SSM_SEG - 1:SSM_SEG, re], pw_ref[SSM_SEG - 1:SSM_SEG, im]
        if reverse:
            a_i = -a_i
        cr, ci = carry_ref[0:1, re], carry_ref[0:1, im]
        for seg in order:
            cm_ref[seg:seg + 1, re] = cr
            cm_ref[seg:seg + 1, im] = ci
            er = s_ref[first_row + seg:first_row + seg + 1, re]
            ei = s_ref[first_row + seg:first_row + seg + 1, im]
            cr, ci = a_r * cr - a_i * ci + er, a_r * ci + a_i * cr + ei
        carry_ref[0:1, re] = cr
        carry_ref[0:1, im] = ci


def _ssm_fwd(u, b_half, c_half, pw, d_skip, side_blocks):
    s_len = u.shape[0]
    tb = SSM_BLOCK
    n = s_len // tb
    ns = len(side_blocks)
    side_in_specs, side_shapes, side_sems = _side_gather_specs(side_blocks)

    def body(*refs):
        u_ref, b_ref, c_ref, pw_ref, d_ref = refs[:5]
        side_ins = refs[5:5 + ns]
        y_ref, cm_ref = refs[5 + ns:7 + ns]
        side_outs = refs[7 + ns:7 + 2 * ns]
        s_ref, carry_ref, up_ref, yp_ref, stage_ref = refs[7 + 2 * ns:12 + 2 * ns]
        side = _side_gather_copies(side_ins, side_outs, *refs[12 + 2 * ns:])
        i = pl.program_id(0)

        @pl.when(i == 0)
        def _():
            carry_ref[...] = jnp.zeros_like(carry_ref)
            for cp in side:
                cp.start()

        _rows_to_segments(u_ref, stage_ref, up_ref)
        u = up_ref[...]
        ub = u.astype(bf16)
        for half in range(N_HALF):
            s_ref[:, half * HALF_COLS:(half + 1) * HALF_COLS] = _dot(ub[:, half * HALF_W:(half + 1) * HALF_W], b_ref[half])
        _ssm_scan(s_ref, pw_ref, None, reverse=False, unroll=4)
        _ssm_carries(tb - SUBLANES, s_ref, pw_ref, carry_ref, cm_ref, reverse=False)
        _ssm_add_carry(s_ref, pw_ref, cm_ref, reverse=False)
        for half in range(N_HALF):
            cols = slice(half * HALF_W, (half + 1) * HALF_W)
            sb = s_ref[:, half * HALF_COLS:(half + 1) * HALF_COLS].astype(bf16)
            yp_ref[:, cols] = _dot(sb, c_ref[half]) + d_ref[:, cols] * u[:, cols]
        _rows_from_segments(yp_ref, stage_ref, y_ref)

        @pl.when(i == n - 1)
        def _():
            for cp in side:
                cp.wait()

    outs = pl.pallas_call(
        body, name="ssm_fwd", grid=(n,),
        in_specs=[_row_spec(tb, SSM_W), _const_spec((N_HALF, HALF_W, HALF_COLS)), _const_spec((N_HALF, HALF_COLS, HALF_W)),
                  _const_spec((SSM_SEG, 2 * N_STATE)), _const_spec((1, SSM_W))] + side_in_specs,
        out_specs=[_row_spec(tb, SSM_W), _row_spec(SUBLANES, 2 * N_STATE)] + side_in_specs,
        out_shape=[_sds((s_len, SSM_W), f32), _sds((n * SUBLANES, 2 * N_STATE), f32)] + side_shapes,
        scratch_shapes=[pltpu.VMEM((tb, 2 * N_STATE), f32), pltpu.VMEM((SUBLANES, 2 * N_STATE), f32),
                        pltpu.VMEM((tb, SSM_W), f32), pltpu.VMEM((tb, SSM_W), f32),
                        pltpu.VMEM((SSM_W // LANES, tb, LANES), f32)] + side_sems,
        compiler_params=_cparams(("arbitrary",)),
    )(u, b_half, c_half, pw, d_skip, *side_blocks)
    return outs[0], outs[1], outs[2:]


def _layer_norm_fwd(r, g, b):
    mu = jnp.mean(r, axis=-1, keepdims=True)
    var = jnp.mean(jnp.square(r - mu), axis=-1, keepdims=True)
    rstd = lax.rsqrt(var + LN_EPS)
    xhat = (r - mu) * rstd
    return xhat, rstd, xhat * g + b


def _layer_norm_bwd(dy, xhat, rstd, g):
    dxh = dy * g
    m1 = jnp.mean(dxh, axis=-1, keepdims=True)
    m2 = jnp.mean(dxh * xhat, axis=-1, keepdims=True)
    return rstd * (dxh - m1 - xhat * m2)


def _branch_outputs(ys_ref, ain_ref, o_ref, wglu_ref, wco_ref, wxo_ref):
    ysb = _gelu(ys_ref[...]).astype(bf16)
    glu = _dot(ysb, wglu_ref[...], NT)
    ga, sb = glu[:, :D_MODEL], jax.nn.sigmoid(glu[:, D_MODEL:])
    ya = _dot(ain_ref[...], wco_ref[...], NT)
    yc = _dot(o_ref[...], wxo_ref[...], NT)
    return ysb, ga, sb, ya, ga * sb, yc


def _mid_fwd(y_ssm, g, ain, ob, x, w_glu_t, w_co_t, w_xo_t, w_out, ln1_g, ln1_b):
    s_len = x.shape[0]
    tm = TOKEN_TILE
    n = s_len // tm

    def body(ys_ref, g_ref, ain_ref, o_ref, x_ref, wglu_ref, wco_ref, wxo_ref, wout_ref, lg_ref, lb_ref,
             ysbt_ref, mb_ref, xhat_ref, rstd_ref):
        ysb, _, _, ya, yb, yc = _branch_outputs(ys_ref, ain_ref, o_ref, wglu_ref, wco_ref, wxo_ref)
        ysbt_ref[...] = ysb.T
        gt = g_ref[...]
        merged = gt[:, :D_MODEL] * ya + gt[:, D_MODEL:2 * D_MODEL] * yb + gt[:, 2 * D_MODEL:] * yc
        mb = merged.astype(bf16)
        mb_ref[...] = mb
        r1 = ALPHA * x_ref[...] + _dot(mb, wout_ref[...])
        xhat, rstd, _ = _layer_norm_fwd(r1, lg_ref[...], lb_ref[...])
        xhat_ref[...] = xhat
        rstd_ref[...] = rstd

    row_cols = [(D_MODEL, bf16), (D_MODEL, f32), (1, f32)]
    return pl.pallas_call(
        body, name="mid_fwd", grid=(n,),
        in_specs=[_row_spec(tm, SSM_W), _row_spec(tm, GATE_COLS), _row_spec(tm, CONV_W), _row_spec(tm, XATTN_W),
                  _row_spec(tm, D_MODEL), _const_spec((2 * D_MODEL, SSM_W)), _const_spec((D_MODEL, CONV_W)),
                  _const_spec((D_MODEL, XATTN_W)), _const_spec((D_MODEL, D_MODEL)),
                  _const_spec((1, D_MODEL)), _const_spec((1, D_MODEL))],
        out_specs=[_col_spec(SSM_W, tm)] + [_row_spec(tm, c) for c, _ in row_cols],
        out_shape=[_sds((SSM_W, s_len), bf16)] + [_sds((s_len, c), dt) for c, dt in row_cols],
        compiler_params=_cparams(("parallel",)),
    )(y_ssm, g, ain, ob, x, w_glu_t, w_co_t, w_xo_t, w_out, ln1_g, ln1_b)


def _mlp_fwd_bwd(xhat1, tgt, ln1_g, ln1_b, w_up_t, b_up, w_down, b_down, ln2_g, ln2_b):
    s_len = xhat1.shape[0]
    tm = TOKEN_TILE
    n = s_len // tm
    fc = 1024
    nfc = D_FF // fc

    def body(xh_ref, t_ref, l1g_ref, l1b_ref, wup_ref, bup_ref, wdn_ref, bdn_ref, l2g_ref, l2b_ref,
             x1bt_ref, hdn_ref, dr2bt_ref, dpre_ref, dx1_ref,
             loss_ref, dl2g_ref, dl2b_ref, dbdn_ref, dbup_ref, rl_ref):
        i = pl.program_id(0)

        @pl.when(i == 0)
        def _():
            loss_ref[...] = jnp.zeros_like(loss_ref)
            dl2g_ref[...] = jnp.zeros_like(dl2g_ref)
            dl2b_ref[...] = jnp.zeros_like(dl2b_ref)
            dbdn_ref[...] = jnp.zeros_like(dbdn_ref)
            dbup_ref[...] = jnp.zeros_like(dbup_ref)

        x1 = xh_ref[...] * l1g_ref[...] + l1b_ref[...]
        x1b = x1.astype(bf16)
        x1bt_ref[...] = x1b.T
        acc = jnp.zeros((tm, D_MODEL), f32)
        for c in range(nfc):
            cols = slice(c * fc, (c + 1) * fc)
            pre = _dot(x1b, wup_ref[cols, :], NT) + bup_ref[:, cols]
            rl = jnp.maximum(pre, 0.0)
            rl_ref[:, cols] = rl
            hb = (rl * rl).astype(bf16)
            hdn_ref[:, cols] = hb
            acc = acc + _dot(hb, wdn_ref[cols, :])
        r2 = ALPHA * x1 + acc + bdn_ref[...]
        xhat2, rstd2, y = _layer_norm_fwd(r2, l2g_ref[...], l2b_ref[...])
        err = y - t_ref[...]
        loss_ref[...] += jnp.sum(jnp.sum(err * err, axis=1, keepdims=True), axis=0, keepdims=True) * (0.5 / D_MODEL)
        dy = err * (1.0 / D_MODEL)
        dl2g_ref[...] += _colsum(dy * xhat2)
        dl2b_ref[...] += _colsum(dy)
        dr2 = _layer_norm_bwd(dy, xhat2, rstd2, l2g_ref[...])
        dbdn_ref[...] += _colsum(dr2)
        dr2b = dr2.astype(bf16)
        dr2bt_ref[...] = dr2b.T
        dacc = jnp.zeros((tm, D_MODEL), f32)
        for c in range(nfc):
            cols = slice(c * fc, (c + 1) * fc)
            dh = _dot(dr2b, wdn_ref[cols, :], NT)
            dpre = dh * (2.0 * rl_ref[:, cols])
            dbup_ref[:, cols] += _colsum(dpre)
            dpb = dpre.astype(bf16)
            dpre_ref[:, cols] = dpb
            dacc = dacc + _dot(dpb, wup_ref[cols, :])
        dx1_ref[...] = ALPHA * dr2 + dacc

    acc_shapes = [(1, LANES), (1, D_MODEL), (1, D_MODEL), (1, D_MODEL), (1, D_FF)]
    return pl.pallas_call(
        body, name="mlp_fwd_bwd", grid=(n,),
        in_specs=[_row_spec(tm, D_MODEL), _row_spec(tm, D_MODEL), _const_spec((1, D_MODEL)), _const_spec((1, D_MODEL)),
                  _const_spec((D_FF, D_MODEL)), _const_spec((1, D_FF)), _const_spec((D_FF, D_MODEL)),
                  _const_spec((1, D_MODEL)), _const_spec((1, D_MODEL)), _const_spec((1, D_MODEL))],
        out_specs=([_col_spec(D_MODEL, tm), _row_spec(tm, D_FF), _col_spec(D_MODEL, tm), _row_spec(tm, D_FF),
                    _row_spec(tm, D_MODEL)] + [_acc_spec(s) for s in acc_shapes]),
        out_shape=([_sds((D_MODEL, s_len), bf16), _sds((s_len, D_FF), bf16), _sds((D_MODEL, s_len), bf16),
                    _sds((s_len, D_FF), bf16), _sds((s_len, D_MODEL), f32)] + [_sds(s, f32) for s in acc_shapes]),
        scratch_shapes=[pltpu.VMEM((tm, D_FF), f32)],
        compiler_params=_cparams(("arbitrary",)),
    )(xhat1, tgt, ln1_g, ln1_b, w_up_t, b_up, w_down, b_down, ln2_g, ln2_b)


def _mid_bwd(dx1, xhat1, rstd1, g, ain, ob, y_ssm, ln1_g, w_out, w_glu_t, w_co_t, w_xo_t):
    s_len = dx1.shape[0]
    tm = TOKEN_TILE
    n = s_len // tm

    def body(dx1_ref, xh_ref, rs_ref, g_ref, ain_ref, o_ref, ys_ref, lg_ref, wout_ref, wglu_ref, wco_ref, wxo_ref,
             dxp_ref, dr1bt_ref, dgp_ref, dya_ref, dyc_ref, dglu_ref, dyssm_ref,
             dl1g_ref, dl1b_ref, dbg_ref):
        i = pl.program_id(0)

        @pl.when(i == 0)
        def _():
            dl1g_ref[...] = jnp.zeros_like(dl1g_ref)
            dl1b_ref[...] = jnp.zeros_like(dl1b_ref)
            dbg_ref[...] = jnp.zeros_like(dbg_ref)

        dx1 = dx1_ref[...]
        xhat = xh_ref[...]
        dl1g_ref[...] += _colsum(dx1 * xhat)
        dl1b_ref[...] += _colsum(dx1)
        dr1 = _layer_norm_bwd(dx1, xhat, rs_ref[...], lg_ref[...])
        dxp_ref[...] = ALPHA * dr1
        dr1b = dr1.astype(bf16)
        dr1bt_ref[...] = dr1b.T
        dm = _dot(dr1b, wout_ref[...], NT)

        _, ga, sb, ya, yb, yc = _branch_outputs(ys_ref, ain_ref, o_ref, wglu_ref, wco_ref, wxo_ref)
        gt = g_ref[...]
        branch = (ya, yb, yc)
        for j in range(3):
            cols = slice(j * D_MODEL, (j + 1) * D_MODEL)
            gj = gt[:, cols]
            dgp = dm * branch[j] * gj * (1.0 - gj)
            dbg_ref[:, cols] += _colsum(dgp)
            dgp_ref[:, cols] = dgp.astype(bf16)
        dya_ref[...] = (dm * gt[:, :D_MODEL]).astype(bf16)
        dyc_ref[...] = (dm * gt[:, 2 * D_MODEL:]).astype(bf16)
        dyb = dm * gt[:, D_MODEL:2 * D_MODEL]
        dga = (dyb * sb).astype(bf16)
        dgb = (dyb * ga * sb * (1.0 - sb)).astype(bf16)
        dglu_ref[:, :D_MODEL] = dga
        dglu_ref[:, D_MODEL:] = dgb
        dys = _dot(dga, wglu_ref[:D_MODEL, :]) + _dot(dgb, wglu_ref[D_MODEL:, :])
        dyssm_ref[...] = dys * _gelu_grad(ys_ref[...])

    row_cols = [(GATE_COLS, bf16), (D_MODEL, bf16), (D_MODEL, bf16), (2 * D_MODEL, bf16), (SSM_W, f32)]
    acc_shapes = [(1, D_MODEL), (1, D_MODEL), (1, GATE_COLS)]
    return pl.pallas_call(
        body, name="mid_bwd", grid=(n,),
        in_specs=[_row_spec(tm, D_MODEL), _row_spec(tm, D_MODEL), _row_spec(tm, 1), _row_spec(tm, GATE_COLS),
                  _row_spec(tm, CONV_W), _row_spec(tm, XATTN_W), _row_spec(tm, SSM_W),
                  _const_spec((1, D_MODEL)), _const_spec((D_MODEL, D_MODEL)), _const_spec((2 * D_MODEL, SSM_W)),
                  _const_spec((D_MODEL, CONV_W)), _const_spec((D_MODEL, XATTN_W))],
        out_specs=([_row_spec(tm, D_MODEL), _col_spec(D_MODEL, tm)] + [_row_spec(tm, c) for c, _ in row_cols]
                   + [_acc_spec(s) for s in acc_shapes]),
        out_shape=([_sds((s_len, D_MODEL), f32), _sds((D_MODEL, s_len), bf16)]
                   + [_sds((s_len, c), dt) for c, dt in row_cols] + [_sds(s, f32) for s in acc_shapes]),
        compiler_params=_cparams(("arbitrary",)),
    )(dx1, xhat1, rstd1, g, ain, ob, y_ssm, ln1_g, w_out, w_glu_t, w_co_t, w_xo_t)


def _ssm_bwd(u, dy, cm_all, b_half, c_half, pw, d_skip):
    s_len = u.shape[0]
    tb = SSM_BLOCK
    n = s_len // tb

    def body(u_ref, dy_ref, cm_ref, b_ref, c_ref, pw_ref, d_ref,
             du_ref, db_hbm, dc_hbm, da_ref, dd_ref,
             s_ref, g_ref, gcarry_ref, gcm_ref, db_ref, dc_ref, up_ref, dyp_ref, dup_ref, stage_ref):
        i = pl.program_id(0)

        @pl.when(i == 0)
        def _():
            gcarry_ref[...] = jnp.zeros_like(gcarry_ref)
            db_ref[...] = jnp.zeros_like(db_ref)
            dc_ref[...] = jnp.zeros_like(dc_ref)
            da_ref[...] = jnp.zeros_like(da_ref)
            dd_ref[...] = jnp.zeros_like(dd_ref)

        _rows_to_segments(u_ref, stage_ref, up_ref)
        _rows_to_segments(dy_ref, stage_ref, dyp_ref)
        u = up_ref[...]
        ub = u.astype(bf16)
        dy = dyp_ref[...]
        dyb = dy.astype(bf16)
        dd_ref[...] += _colsum(dy * u)

        for half in range(N_HALF):
            s_ref[:, half * HALF_COLS:(half + 1) * HALF_COLS] = _dot(ub[:, half * HALF_W:(half + 1) * HALF_W], b_ref[half])
        _ssm_scan(s_ref, pw_ref, cm_ref, reverse=False, unroll=True)

        for half in range(N_HALF):
            g_ref[:, half * HALF_COLS:(half + 1) * HALF_COLS] = _dot(dyb[:, half * HALF_W:(half + 1) * HALF_W], c_ref[half], NT)
        _ssm_scan(g_ref, pw_ref, None, reverse=True, unroll=True)
        _ssm_carries(0, g_ref, pw_ref, gcarry_ref, gcm_ref, reverse=True)
        _ssm_add_carry(g_ref, pw_ref, gcm_ref, reverse=True)

        for half in range(N_HALF):
            cols = slice(half * HALF_W, (half + 1) * HALF_W)
            scols = slice(half * HALF_COLS, (half + 1) * HALF_COLS)
            gb = g_ref[:, scols].astype(bf16)
            dup_ref[:, cols] = _dot(gb, b_ref[half], NT) + d_ref[:, cols] * dy[:, cols]
            db_ref[half] += _dot(ub[:, cols], gb, TN)
            dc_ref[half] += _dot(s_ref[:, scols].astype(bf16), dyb[:, cols], TN)
        _rows_from_segments(dup_ref, stage_ref, du_ref)

        for chunk in range(N_STATE // LANE_CHUNK):
            re, im = _state_cols(chunk)
            acc_r = da_ref[:, re]
            acc_i = da_ref[:, im]
            for k in range(SSM_SEG):
                rows = slice(k * SUBLANES, (k + 1) * SUBLANES)
                if k == 0:
                    pr, pi = cm_ref[:, re], cm_ref[:, im]
                else:
                    prev = slice((k - 1) * SUBLANES, k * SUBLANES)
                    pr, pi = s_ref[prev, re], s_ref[prev, im]
                gr, gi = g_ref[rows, re], g_ref[rows, im]
                acc_r = acc_r + (gr * pr + gi * pi)
                acc_i = acc_i + (gi * pr - gr * pi)
            da_ref[:, re] = acc_r
            da_ref[:, im] = acc_i

        @pl.when(i == n - 1)
        def _():
            pltpu.sync_copy(db_ref, db_hbm)
            pltpu.sync_copy(dc_ref, dc_hbm)

    rev = functools.partial(_row_spec, rev_n=n)
    any_spec = pl.BlockSpec(memory_space=pl.ANY)
    state_rows = pltpu.VMEM((tb, 2 * N_STATE), f32)
    seg_rows = pltpu.VMEM((SUBLANES, 2 * N_STATE), f32)
    tok_rows = pltpu.VMEM((tb, SSM_W), f32)
    return pl.pallas_call(
        body, name="ssm_bwd", grid=(n,),
        in_specs=[rev(tb, SSM_W), rev(tb, SSM_W), rev(SUBLANES, 2 * N_STATE),
                  _const_spec((N_HALF, HALF_W, HALF_COLS)), _const_spec((N_HALF, HALF_COLS, HALF_W)),
                  _const_spec((SSM_SEG, 2 * N_STATE)), _const_spec((1, SSM_W))],
        out_specs=[rev(tb, SSM_W), any_spec, any_spec, _acc_spec((SUBLANES, 2 * N_STATE)), _acc_spec((1, SSM_W))],
        out_shape=[_sds((s_len, SSM_W), f32), _sds((N_HALF, HALF_W, HALF_COLS), f32),
                   _sds((N_HALF, HALF_COLS, HALF_W), f32), _sds((SUBLANES, 2 * N_STATE), f32), _sds((1, SSM_W), f32)],
        scratch_shapes=[state_rows, state_rows, seg_rows, seg_rows,
                        pltpu.VMEM((N_HALF, HALF_W, HALF_COLS), f32), pltpu.VMEM((N_HALF, HALF_COLS, HALF_W), f32),
                        tok_rows, tok_rows, tok_rows, pltpu.VMEM((SSM_W // LANES, tb, LANES), f32)],
        compiler_params=_cparams(("arbitrary",)),
    )(u, dy, cm_all, b_half, c_half, pw, d_skip)


def _branch_bwd(dya, dyc, cin, q, kv, conv_w, w_co_t, w_xo_t):
    s_len = dya.shape[0]
    tm = TOKEN_TILE
    n = s_len // tm
    halo_blocks = tm // 8

    def body(dya_ref, dyc_ref, cin_ref, cprev_ref, q_ref, kv_ref, cw_ref, wco_ref, wxo_ref,
             dconv_ref, dq_ref, dcw_ref, dkv_ref, zs_ref, dczs_ref):
        i = pl.program_id(0)
        tile = n - 1 - i

        @pl.when(i == 0)
        def _():
            dcw_ref[...] = jnp.zeros_like(dcw_ref)
            dkv_ref[...] = jnp.zeros_like(dkv_ref)
            dczs_ref[tm:tm + 8, :] = jnp.zeros((8, CONV_W), f32)

        cin = cin_ref[...]
        cb, cc, ch = cin[:, :CONV_W], cin[:, CONV_W:2 * CONV_W], cin[:, 2 * CONV_W:]
        z = cc * ch
        cprev = cprev_ref[...]
        zprev = cprev[:, CONV_W:2 * CONV_W] * cprev[:, 2 * CONV_W:]
        zs_ref[0:8, :] = jnp.where(tile == 0, 0.0, zprev)
        zs_ref[8:8 + tm, :] = z
        z1 = zs_ref[pl.ds(7, tm), :]
        z2 = zs_ref[pl.ds(6, tm), :]
        cw = cw_ref[...]
        cz = cw[0:1] * z2 + cw[1:2] * z1 + cw[2:3] * z

        dain = _dot(dya_ref[...], wco_ref[...])
        dcb = dain * cz
        dcz = dain * cb
        dczs_ref[0:tm, :] = dcz
        dcz1 = dczs_ref[pl.ds(1, tm), :]
        dcz2 = dczs_ref[pl.ds(2, tm), :]
        dz = cw[2:3] * dcz + cw[1:2] * dcz1 + cw[0:1] * dcz2
        dczs_ref[tm:tm + 8, :] = dczs_ref[0:8, :]
        dcw_ref[0:1, :] += _colsum(dcz * z2)
        dcw_ref[1:2, :] += _colsum(dcz * z1)
        dcw_ref[2:3, :] += _colsum(dcz * z)
        dconv_ref[:, :CONV_W] = dcb.astype(bf16)
        dconv_ref[:, CONV_W:2 * CONV_W] = (dz * ch).astype(bf16)
        dconv_ref[:, 2 * CONV_W:] = (dz * cc).astype(bf16)

        qb = q_ref[...]
        do = _dot(dyc_ref[...], wxo_ref[...])
        for h in range(HEADS):
            hc = slice(h * HEAD_DIM, (h + 1) * HEAD_DIM)
            vc = slice(XATTN_W + h * HEAD_DIM, XATTN_W + (h + 1) * HEAD_DIM)
            p = _attention_probs(qb, kv_ref, h)
            dob = do[:, hc].astype(bf16)
            dp = _dot(dob, kv_ref[:, vc], NT)
            dkv_ref[:, vc] += _dot(p.astype(bf16), dob, TN)
            ds = p * (dp - jnp.sum(dp * p, axis=-1, keepdims=True)) * (HEAD_DIM ** -0.5)
            dsb = ds.astype(bf16)
            dq_ref[:, hc] = _dot(dsb, kv_ref[:, hc]).astype(bf16)
            dkv_ref[:, hc] += _dot(dsb, qb[:, hc], TN)

    rev = functools.partial(_row_spec, rev_n=n)
    prev_spec = pl.BlockSpec((8, 3 * CONV_W), lambda i: (jnp.maximum((n - 1 - i) * halo_blocks - 1, 0), 0))
    return pl.pallas_call(
        body, name="branch_bwd", grid=(n,),
        in_specs=[rev(tm, D_MODEL), rev(tm, D_MODEL), rev(tm, 3 * CONV_W), prev_spec, rev(tm, XATTN_W),
                  _const_spec((MEM_LEN, 2 * XATTN_W)), _const_spec((3, CONV_W)), _const_spec((D_MODEL, CONV_W)),
                  _const_spec((D_MODEL, XATTN_W))],
        out_specs=[rev(tm, 3 * CONV_W), rev(tm, XATTN_W), _acc_spec((8, CONV_W)), _acc_spec((MEM_LEN, 2 * XATTN_W))],
        out_shape=[_sds((s_len, 3 * CONV_W), bf16), _sds((s_len, XATTN_W), bf16), _sds((8, CONV_W), f32),
                   _sds((MEM_LEN, 2 * XATTN_W), f32)],
        scratch_shapes=[pltpu.VMEM((tm + 8, CONV_W), f32), pltpu.VMEM((tm + 8, CONV_W), f32)],
        compiler_params=_cparams(("arbitrary",)),
    )(dya, dyc, cin, cin, q, kv, conv_w, w_co_t, w_xo_t)


def _in_proj_bwd(dgp, dconv, du, dq, dxp, w_in_t):
    s_len = dgp.shape[0]
    tm = TOKEN_TILE
    n = s_len // tm

    def body(dgp_ref, dconv_ref, du_ref, dq_ref, dxp_ref, win_ref, dx_ref, dproj_ref):
        dproj = jnp.concatenate([dgp_ref[...], dconv_ref[...], du_ref[...].astype(bf16), dq_ref[...]], axis=1)
        dproj_ref[...] = dproj
        dx_ref[...] = dxp_ref[...] + _dot(dproj, win_ref[...])

    return pl.pallas_call(
        body, name="in_proj_bwd", grid=(n,),
        in_specs=[_row_spec(tm, GATE_COLS), _row_spec(tm, 3 * CONV_W), _row_spec(tm, SSM_W), _row_spec(tm, XATTN_W),
                  _row_spec(tm, D_MODEL), _const_spec((IN_COLS, D_MODEL))],
        out_specs=[_row_spec(tm, D_MODEL), _row_spec(tm, IN_COLS)],
        out_shape=[_sds((s_len, D_MODEL), f32), _sds((s_len, IN_COLS), bf16)],
        compiler_params=_cparams(("parallel",)),
    )(dgp, dconv, du, dq, dxp, w_in_t)


N_CHIP = 4
CHIP_STEPS = [(1, 1), (1, 0), (0, 1), (0, 0)]


def _flip(v, d):
    return 1 - v if d else v


def _chip_order():
    x, y, _ = _mesh_place()
    return jnp.stack([2 * _flip(x, dx) + _flip(y, dy) for dx, dy in CHIP_STEPS]).astype(jnp.int32)


def _weight_grad_scatter(a_t, b, name, tm, tt):
    m, s_len = a_t.shape
    n_cols = b.shape[1]
    w = n_cols // N_DEV
    tn = 2 * w
    tm, tt = min(tm, m), min(tt, s_len)
    nm, nt = m // tm, s_len // tt
    assert m % tm == 0 and s_len % tt == 0

    def body(order_ref, a_ref, b_ref, recv_ref, acc_ref, send_ref, sib_ref, stash_ref,
             d2d_send, d2d_recv, ici_send, ici_recv, local_sem):
        del order_ref
        q, im, t = pl.program_id(0), pl.program_id(1), pl.program_id(2)
        x, y, c = _mesh_place()
        mesh_id = pl.DeviceIdType.MESH

        @pl.when(t == 0)
        def _():
            acc_ref[...] = jnp.zeros_like(acc_ref)

        acc_ref[...] += _dot(a_ref[...], b_ref[...])

        def to_sibling(qq, imm):
            rows = pl.ds(pl.multiple_of(imm * tm, tm), tm)
            return pltpu.make_async_remote_copy(
                src_ref=send_ref.at[qq, 0, rows, :], dst_ref=sib_ref.at[qq, rows, :],
                send_sem=d2d_send.at[qq], recv_sem=d2d_recv.at[qq, imm],
                device_id=(x, y, 1 - c), device_id_type=mesh_id)

        def finish_tile(qq, imm):
            rows = pl.ds(pl.multiple_of(imm * tm, tm), tm)
            to_sibling(qq, imm).wait_recv()
            both = stash_ref[...] + sib_ref[qq, rows, :].astype(f32)
            send_ref[qq, 1, rows, :] = both.astype(bf16)
            for step, (dx, dy) in enumerate(CHIP_STEPS):
                @pl.when(qq == step)
                def _(step=step, dx=dx, dy=dy):
                    src, dst = send_ref.at[step, 1, rows, :], recv_ref.at[step, rows, :]
                    if dx or dy:
                        pltpu.make_async_remote_copy(
                            src_ref=src, dst_ref=dst, send_sem=ici_send.at[step], recv_sem=ici_recv.at[step],
                            device_id=(_flip(x, dx), _flip(y, dy), c), device_id_type=mesh_id).start()
                    else:
                        pltpu.make_async_copy(src, dst, local_sem).start()

        @pl.when(t == nt - 1)
        def _():
            tile = q * nm + im

            @pl.when(tile > 0)
            def _():
                finish_tile((tile - 1) // nm, (tile - 1) % nm)

            rows = pl.ds(pl.multiple_of(im * tm, tm), tm)
            for core in (0, 1):
                @pl.when(c == core)
                def _(core=core):
                    other = 1 - core
                    send_ref[q, 0, rows, :] = acc_ref[:, other * w:(other + 1) * w].astype(bf16)
                    stash_ref[...] = acc_ref[:, core * w:(core + 1) * w]
            to_sibling(q, im).start()

            @pl.when(tile == N_CHIP * nm - 1)
            def _():
                finish_tile(q, im)

        @pl.when((q == N_CHIP - 1) & (im == nm - 1) & (t == nt - 1))
        def _():
            for step, (dx, dy) in enumerate(CHIP_STEPS):
                pltpu.make_async_remote_copy(
                    src_ref=send_ref.at[step, 0], dst_ref=sib_ref.at[step],
                    send_sem=d2d_send.at[step], recv_sem=d2d_recv.at[step, 0],
                    device_id=(x, y, 1 - c), device_id_type=mesh_id).wait_send()
                src, dst = send_ref.at[step, 1], recv_ref.at[step]
                if dx or dy:
                    pltpu.make_async_remote_copy(
                        src_ref=src, dst_ref=dst, send_sem=ici_send.at[step], recv_sem=ici_recv.at[step],
                        device_id=(_flip(x, dx), _flip(y, dy), c), device_id_type=mesh_id).wait()
                else:
                    pltpu.make_async_copy(src, dst, local_sem).wait()

    grid_spec = pltpu.PrefetchScalarGridSpec(
        num_scalar_prefetch=1, grid=(N_CHIP, nm, nt),
        in_specs=[pl.BlockSpec((tm, tt), lambda q, im, t, order: (im, t)),
                  pl.BlockSpec((tt, tn), lambda q, im, t, order: (t, order[q]))],
        out_specs=pl.BlockSpec(memory_space=pl.ANY),
        scratch_shapes=[pltpu.VMEM((tm, tn), f32), pltpu.VMEM((N_CHIP, 2, m, w), bf16), pltpu.VMEM((N_CHIP, m, w), bf16),
                        pltpu.VMEM((tm, w), f32),
                        pltpu.SemaphoreType.DMA((N_CHIP,)), pltpu.SemaphoreType.DMA((N_CHIP, nm)),
                        pltpu.SemaphoreType.DMA((N_CHIP - 1,)), pltpu.SemaphoreType.DMA((N_CHIP - 1,)),
                        pltpu.SemaphoreType.DMA])
    return pl.pallas_call(
        body, name=name, grid_spec=grid_spec,
        out_shape=_sds((N_CHIP, m, w), bf16),
        compiler_params=_cparams(("arbitrary", "arbitrary", "arbitrary")),
    )(_chip_order(), a_t, b)


def _adamw(w, g, m, v):
    m = ADAM_B1 * m + (1.0 - ADAM_B1) * g
    v = ADAM_B2 * v + (1.0 - ADAM_B2) * jnp.square(g)
    m_hat = m / (1.0 - ADAM_B1 ** ADAM_STEP)
    v_hat = v / (1.0 - ADAM_B2 ** ADAM_STEP)
    delta = -ADAM_LR * (m_hat / (jnp.sqrt(v_hat) + ADAM_EPS) + ADAM_WD * w)
    return delta, m, v


def _sum_parts(p_ref):
    g = p_ref[0].astype(f32)
    for j in range(1, p_ref.shape[0]):
        g = g + p_ref[j].astype(f32)
    return g


def _adamw_update(w, m, v, parts, name, transposed):
    rows, cols = w.shape
    n_parts = parts.shape[0]
    if transposed:
        tr = LANES
        p_spec = pl.BlockSpec((n_parts, cols, tr), lambda i: (0, 0, i))
    else:
        tr = next(t for t in (256, 128, 64, 32, 16, 8) if rows % t == 0)
        p_spec = pl.BlockSpec((n_parts, tr, cols), lambda i: (0, i, 0))
    spec = pl.BlockSpec((tr, cols), lambda i: (i, 0))

    def body(w_ref, p_ref, m_ref, v_ref, g_ref, d_ref, nm_ref, nv_ref):
        g = _sum_parts(p_ref)
        if transposed:
            g = g.T
        g_ref[...] = g
        d_ref[...], nm_ref[...], nv_ref[...] = _adamw(w_ref[...], g, m_ref[...], v_ref[...])

    return pl.pallas_call(
        body, name=name, grid=(rows // tr,),
        in_specs=[spec, p_spec, spec, spec], out_specs=[spec] * 4,
        out_shape=[_sds((rows, cols), f32)] * 4,
        compiler_params=_cparams(("parallel",)),
    )(w, parts, m, v)


SMALL_GROUPS = [
    (["b_gate", "ln1_g", "ln1_b", "b_up", "b_down", "ln2_g", "ln2_b", "ssm_d"], 1),
    (["ssm_lam_re", "ssm_lam_im", "ssm_c_re", "ssm_c_im"], 0),
    (["ssm_b_re", "ssm_b_im"], 0),
    (["conv_w"], 0),
    (["ssm_log_dt"], 0),
]


def _sum_small(group_parts):
    def body(*refs):
        n = len(refs) // 2
        for p_ref, o_ref in zip(refs[:n], refs[n:]):
            o_ref[...] = _sum_parts(p_ref)

    return pl.pallas_call(
        body, name="sum_small",
        out_shape=[_sds(p.shape[1:], f32) for p in group_parts],
        compiler_params=_cparams(),
    )(*group_parts)


def _adamw_small(ws, ms, vs, group_sums):
    names = [k for group, _ in SMALL_GROUPS for k in group]
    n = len(names)

    def body(*refs):
        w_refs, m_refs, v_refs = (dict(zip(names, refs[j * n:(j + 1) * n])) for j in range(3))
        p_refs = refs[3 * n:3 * n + len(SMALL_GROUPS)]
        out_refs = [dict(zip(names, refs[3 * n + len(SMALL_GROUPS) + j * n:][:n])) for j in range(4)]
        for (group, axis), p_ref in zip(SMALL_GROUPS, p_refs):
            total = p_ref[...]
            off = 0
            for k in group:
                size = SMALL[k][axis]
                g = total[:, off:off + size] if axis == 1 else total[off:off + size, :]
                off += size
                d, nm, nv = _adamw(w_refs[k][...], g, m_refs[k][...], v_refs[k][...])
                for j, val in enumerate((g, d, nm, nv)):
                    out_refs[j][k][...] = val

    res = pl.pallas_call(
        body, name="adamw_small",
        out_shape=[_sds(SMALL[k], f32) for _ in range(4) for k in names],
        compiler_params=_cparams(),
    )(*[ws[k] for k in names], *[ms[k] for k in names], *[vs[k] for k in names], *group_sums)
    return [dict(zip(names, res[j * n:(j + 1) * n])) for j in range(4)]


def _ssm_discretize(lam_re, lam_im, log_dt, b_re, b_im):
    dt = jnp.exp(log_dt)[:, None]
    mag = jnp.exp(lam_re * dt)
    abar_r = mag * jnp.cos(lam_im * dt)
    abar_i = mag * jnp.sin(lam_im * dt)
    den = lam_re * lam_re + lam_im * lam_im
    nr = abar_r - 1.0
    ni = abar_i
    kr = (nr * lam_re + ni * lam_im) / den
    ki = (ni * lam_re - nr * lam_im) / den
    bbar_r = kr[..., None] * b_re - ki[..., None] * b_im
    bbar_i = kr[..., None] * b_im + ki[..., None] * b_re
    return abar_r, abar_i, bbar_r, bbar_i


def _state_layout(re, im):
    parts = []
    for half in range(N_HALF):
        cols = slice(half * HALF_STATE, (half + 1) * HALF_STATE)
        parts += [re[..., cols], im[..., cols]]
    return jnp.concatenate(parts, axis=-1)


def _state_unlayout(a):
    re = jnp.concatenate([a[..., _half_cols(h)[0]] for h in range(N_HALF)], axis=-1)
    im = jnp.concatenate([a[..., _half_cols(h)[1]] for h in range(N_HALF)], axis=-1)
    return re, im


def _abar_powers(abar_r, abar_i):
    pr, pi = abar_r.reshape(1, N_STATE), abar_i.reshape(1, N_STATE)
    while pr.shape[0] < SSM_SEG:
        tr, ti = pr[-1:], pi[-1:]
        pr, pi = (jnp.concatenate([pr, pr * tr - pi * ti], axis=0), jnp.concatenate([pi, pr * ti + pi * tr], axis=0))
    return _state_layout(pr, pi)


HALF_GROUPS = SSM_GROUPS // N_HALF


def _half_block_diag(blocks):
    _, r, c = blocks.shape
    eye = jnp.eye(HALF_GROUPS, dtype=blocks.dtype)
    b4 = blocks.reshape(N_HALF, HALF_GROUPS, r, c)
    return jnp.einsum("ngrc,gk->ngrkc", b4, eye).reshape(N_HALF, HALF_GROUPS * r, HALF_GROUPS * c)


def _half_diag_blocks(mat, r, c):
    eye = jnp.eye(HALF_GROUPS, dtype=mat.dtype)
    m5 = mat.reshape(N_HALF, HALF_GROUPS, r, HALF_GROUPS, c)
    return jnp.einsum("ngrkc,gk->ngrc", m5, eye).reshape(SSM_GROUPS, r, c)


BIG = ["w_in", "w_conv_out", "w_glu", "w_kv", "w_xattn_out", "w_out", "w_up", "w_down"]
COL_SHARDED = ["w_in", "w_conv_out", "w_glu", "w_xattn_out", "w_up"]
SMALL = {"b_gate": (1, GATE_COLS), "conv_w": (3, CONV_W), "ssm_lam_re": (SSM_GROUPS, SSM_STATE),
         "ssm_lam_im": (SSM_GROUPS, SSM_STATE), "ssm_log_dt": (1, SSM_GROUPS),
         "ssm_b_re": (N_STATE, SSM_GROUP), "ssm_b_im": (N_STATE, SSM_GROUP),
         "ssm_c_re": (SSM_W, SSM_STATE), "ssm_c_im": (SSM_W, SSM_STATE), "ssm_d": (1, SSM_W),
         "ln1_g": (1, D_MODEL), "ln1_b": (1, D_MODEL), "b_up": (1, D_FF), "b_down": (1, D_MODEL),
         "ln2_g": (1, D_MODEL), "ln2_b": (1, D_MODEL)}
WEIGHTS = ["w_in", "b_gate", "conv_w", "w_conv_out", "ssm_lam_re", "ssm_lam_im", "ssm_log_dt", "ssm_b_re", "ssm_b_im",
           "ssm_c_re", "ssm_c_im", "ssm_d", "w_glu", "w_kv", "w_xattn_out", "w_out", "ln1_g", "ln1_b", "w_up", "b_up",
           "w_down", "b_down", "ln2_g", "ln2_b"]


def _local_step(x, mem, tgt, full, late, small):
    lam_re, lam_im, log_dt = small["ssm_lam_re"], small["ssm_lam_im"], small["ssm_log_dt"].reshape(SSM_GROUPS)
    b_shape = (SSM_GROUPS, SSM_STATE, SSM_GROUP)
    c_shape = (SSM_GROUPS, SSM_GROUP, SSM_STATE)
    disc, disc_vjp = jax.vjp(_ssm_discretize, lam_re, lam_im, log_dt,
                             small["ssm_b_re"].reshape(b_shape), small["ssm_b_im"].reshape(b_shape))
    abar_r, abar_i, bbar_r, bbar_i = disc
    pw = _abar_powers(abar_r, abar_i)
    c_re, c_im = small["ssm_c_re"].reshape(c_shape), small["ssm_c_im"].reshape(c_shape)
    b_half = jnp.concatenate([_half_block_diag(bbar_r.transpose(0, 2, 1)), _half_block_diag(bbar_i.transpose(0, 2, 1))],
                             axis=2).astype(bf16)
    c_half = jnp.concatenate([_half_block_diag(c_re.transpose(0, 2, 1)), -_half_block_diag(c_im.transpose(0, 2, 1))],
                             axis=1).astype(bf16)

    stack = lambda a: a.reshape(-1, a.shape[-1])
    kv, memb = _kv_proj(mem, full["w_kv"])
    (xbt, g, cin, u, q, ain, ob, aint, obt), side = _in_proj(
        x, full["w_in"], small["b_gate"], small["conv_w"], kv,
        [late[k] for k in ("w_glu", "w_conv_out", "w_xattn_out", "w_out", "w_up")])
    w_glu_t, w_co_t, w_xo_t, w_out, w_up_t = (stack(a) for a in side)
    y_ssm, cm_all, side = _ssm_fwd(u, b_half, c_half, pw, small["ssm_d"], [late["w_down"]])
    w_down = stack(side[0])
    ysbt, mb, xhat1, rstd1 = _mid_fwd(y_ssm, g, ain, ob, x, w_glu_t, w_co_t, w_xo_t, w_out,
                                      small["ln1_g"], small["ln1_b"])
    (x1bt, hdn, dr2bt, dpre, dx1, loss, dl2g, dl2b, dbdn, dbup) = _mlp_fwd_bwd(
        xhat1, tgt, small["ln1_g"], small["ln1_b"], w_up_t, small["b_up"], w_down,
        small["b_down"], small["ln2_g"], small["ln2_b"])
    recv = {}
    recv["w_down"] = _weight_grad_scatter(dr2bt, hdn, "dw_down", tm=512, tt=1024)
    recv["w_up"] = _weight_grad_scatter(x1bt, dpre, "dw_up", tm=512, tt=1024)
    (dxp, dr1bt, dgp, dya, dyc, dglu, dyssm, dl1g, dl1b, dbg) = _mid_bwd(
        dx1, xhat1, rstd1, g, ain, ob, y_ssm, small["ln1_g"], w_out, w_glu_t, w_co_t, w_xo_t)
    recv["w_out"] = _weight_grad_scatter(dr1bt, mb, "dw_out", tm=512, tt=2048)
    recv["w_glu"] = _weight_grad_scatter(ysbt, dglu, "dw_glu", tm=512, tt=2048)
    du, db_half, dc_half, da8, dd = _ssm_bwd(u, dyssm, cm_all, b_half, c_half, pw, small["ssm_d"])
    dconv, dq, dcw8, dkv = _branch_bwd(dya, dyc, cin, q, kv, small["conv_w"], w_co_t, w_xo_t)
    recv["w_conv_out"] = _weight_grad_scatter(aint, dya, "dw_conv_out", tm=512, tt=2048)
    recv["w_xattn_out"] = _weight_grad_scatter(obt, dyc, "dw_xattn_out", tm=512, tt=2048)
    recv["w_kv"] = _weight_grad_scatter(dkv.T.astype(bf16), memb, "dw_kv", tm=512, tt=MEM_LEN)
    dx, dproj = _in_proj_bwd(dgp, dconv, du, dq, dxp, full["w_in"])
    recv["w_in"] = _weight_grad_scatter(xbt, dproj, "dw_in", tm=512, tt=1024)

    dabar_r, dabar_i = _state_unlayout(jnp.sum(da8, axis=0))
    dbbar_r = _half_diag_blocks(db_half[:, :, :HALF_STATE], SSM_GROUP, SSM_STATE).transpose(0, 2, 1)
    dbbar_i = _half_diag_blocks(db_half[:, :, HALF_STATE:], SSM_GROUP, SSM_STATE).transpose(0, 2, 1)
    g_shape = (SSM_GROUPS, SSM_STATE)
    dlam_re, dlam_im, dlog_dt, db_re, db_im = disc_vjp(
        (dabar_r.reshape(g_shape), dabar_i.reshape(g_shape), dbbar_r, dbbar_i))
    dc_re = _half_diag_blocks(dc_half[:, :HALF_STATE, :], SSM_STATE, SSM_GROUP).transpose(0, 2, 1)
    dc_im = -_half_diag_blocks(dc_half[:, HALF_STATE:, :], SSM_STATE, SSM_GROUP).transpose(0, 2, 1)

    small_grads = {
        "b_gate": dbg, "conv_w": dcw8[0:3], "ssm_lam_re": dlam_re, "ssm_lam_im": dlam_im, "ssm_log_dt": dlog_dt,
        "ssm_b_re": db_re, "ssm_b_im": db_im, "ssm_c_re": dc_re, "ssm_c_im": dc_im, "ssm_d": dd,
        "ln1_g": dl1g, "ln1_b": dl1b, "b_up": dbup, "b_down": dbdn, "ln2_g": dl2g, "ln2_b": dl2b,
    }
    small_grads = {k: a.reshape(SMALL[k]) for k, a in small_grads.items()}
    return loss[0, 0], dx, recv, small_grads


def kernel(x, mem, w_in, b_gate, conv_w, w_conv_out, ssm_lam_re, ssm_lam_im, ssm_log_dt, ssm_b_re, ssm_b_im, ssm_c_re, ssm_c_im, ssm_d, w_glu, w_kv, w_xattn_out, w_out, ln1_g, ln1_b, w_up, b_up, w_down, b_down, ln2_g, ln2_b, loss_target, m_w_in, m_b_gate, m_conv_w, m_w_conv_out, m_ssm_lam_re, m_ssm_lam_im, m_ssm_log_dt, m_ssm_b_re, m_ssm_b_im, m_ssm_c_re, m_ssm_c_im, m_ssm_d, m_w_glu, m_w_kv, m_w_xattn_out, m_w_out, m_ln1_g, m_ln1_b, m_w_up, m_b_up, m_w_down, m_b_down, m_ln2_g, m_ln2_b, v_w_in, v_b_gate, v_conv_w, v_w_conv_out, v_ssm_lam_re, v_ssm_lam_im, v_ssm_log_dt, v_ssm_b_re, v_ssm_b_im, v_ssm_c_re, v_ssm_c_im, v_ssm_d, v_w_glu, v_w_kv, v_w_xattn_out, v_w_out, v_ln1_g, v_ln1_b, v_w_up, v_b_up, v_w_down, v_b_down, v_ln2_g, v_ln2_b):
    w = dict(w_in=w_in, b_gate=b_gate, conv_w=conv_w, w_conv_out=w_conv_out, ssm_lam_re=ssm_lam_re,
             ssm_lam_im=ssm_lam_im, ssm_log_dt=ssm_log_dt, ssm_b_re=ssm_b_re, ssm_b_im=ssm_b_im, ssm_c_re=ssm_c_re,
             ssm_c_im=ssm_c_im, ssm_d=ssm_d, w_glu=w_glu, w_kv=w_kv, w_xattn_out=w_xattn_out, w_out=w_out,
             ln1_g=ln1_g, ln1_b=ln1_b, w_up=w_up, b_up=b_up, w_down=w_down, b_down=b_down, ln2_g=ln2_g, ln2_b=ln2_b)
    m = dict(w_in=m_w_in, b_gate=m_b_gate, conv_w=m_conv_w, w_conv_out=m_w_conv_out, ssm_lam_re=m_ssm_lam_re,
             ssm_lam_im=m_ssm_lam_im, ssm_log_dt=m_ssm_log_dt, ssm_b_re=m_ssm_b_re, ssm_b_im=m_ssm_b_im,
             ssm_c_re=m_ssm_c_re, ssm_c_im=m_ssm_c_im, ssm_d=m_ssm_d, w_glu=m_w_glu, w_kv=m_w_kv,
             w_xattn_out=m_w_xattn_out, w_out=m_w_out, ln1_g=m_ln1_g, ln1_b=m_ln1_b, w_up=m_w_up, b_up=m_b_up,
             w_down=m_w_down, b_down=m_b_down, ln2_g=m_ln2_g, ln2_b=m_ln2_b)
    v = dict(w_in=v_w_in, b_gate=v_b_gate, conv_w=v_conv_w, w_conv_out=v_w_conv_out, ssm_lam_re=v_ssm_lam_re,
             ssm_lam_im=v_ssm_lam_im, ssm_log_dt=v_ssm_log_dt, ssm_b_re=v_ssm_b_re, ssm_b_im=v_ssm_b_im,
             ssm_c_re=v_ssm_c_re, ssm_c_im=v_ssm_c_im, ssm_d=v_ssm_d, w_glu=v_w_glu, w_kv=v_w_kv,
             w_xattn_out=v_w_xattn_out, w_out=v_w_out, ln1_g=v_ln1_g, ln1_b=v_ln1_b, w_up=v_w_up, b_up=v_b_up,
             w_down=v_w_down, b_down=v_b_down, ln2_g=v_ln2_g, ln2_b=v_ln2_b)
    out_shapes = {k: a.shape for k, a in w.items()}
    shard2d = lambda k, a: a.reshape((3, CONV_W // N_DEV) if k == "conv_w" else SMALL[k]) if k in SMALL else a[0]
    w, m, v = ({k: shard2d(k, a) for k, a in d.items()} for d in (w, m, v))

    shards = {k: w[k].T.astype(bf16) if k in COL_SHARDED else w[k].astype(bf16) for k in BIG}
    conv_pad = jnp.pad(w["conv_w"], ((0, 5), (0, LANES - CONV_W // N_DEV)))
    early = ["w_in", "w_kv"]
    gathered = _all_gather([shards[k] for k in early] + [conv_pad], "gather_weights")
    full = {k: a.reshape(-1, a.shape[-1]) for k, a in zip(early, gathered[:-1])}
    late = {k: shards[k] for k in BIG if k not in early}
    conv_full = gathered[-1][:, :3, :CONV_W // N_DEV].transpose(1, 0, 2).reshape(3, CONV_W)
    small = {k: (conv_full if k == "conv_w" else w[k]) for k in SMALL}

    loss, dx, recv, small_grads = _local_step(x[0], mem[0], loss_target[0], full, late, small)

    grads, deltas, new_m, new_v = {}, {}, {}, {}
    for k in BIG:
        res = _adamw_update(w[k], m[k], v[k], recv[k], "adamw_" + k, transposed=k not in COL_SHARDED)
        grads[k], deltas[k], new_m[k], new_v[k] = res

    stacks = [jnp.concatenate([small_grads[k] for k in group], axis=axis) if len(group) > 1 else small_grads[group[0]]
              for group, axis in SMALL_GROUPS]
    dense = lambda a: a.reshape(-1, LANES) if a.shape[1] < LANES and a.size % LANES == 0 else a
    group_sums = _sum_small(_all_gather([dense(a) for a in stacks], "gather_small_grads"))
    group_sums = [s.reshape(a.shape) for s, a in zip(group_sums, stacks)]
    widen = lambda k, a: jnp.tile(a, (1, N_DEV)) if k == "conv_w" else a
    res = _adamw_small(small, {k: widen(k, m[k]) for k in SMALL}, {k: widen(k, v[k]) for k in SMALL}, group_sums)
    dev = _slot(_mesh_place())
    for d, small_res in zip((grads, deltas, new_m, new_v), res):
        for k, a in small_res.items():
            if k == "conv_w":
                a = lax.dynamic_slice_in_dim(a, dev * (CONV_W // N_DEV), CONV_W // N_DEV, axis=1)
            d[k] = a

    loss = lax.psum(loss, ("x", "y", "c"))
    outs = [loss, dx[None]]
    for d in (grads, deltas, new_m, new_v):
        outs += [d[k].reshape(out_shapes[k]) for k in WEIGHTS]
    return tuple(outs)
```

```python
import functools
import math

import jax
import jax.numpy as jnp
from jax import lax
from jax.experimental import pallas as pl
from jax.experimental.pallas import tpu as pltpu

f32 = jnp.float32
bf16 = jnp.bfloat16

D_MODEL = 1024
MEM_LEN = 256
GATE_COLS = 3 * D_MODEL
CONV_W = 512
SSM_W = 512
XATTN_W = 512
HEADS = 4
HEAD_DIM = 128
D_FF = 4096
IN_COLS = GATE_COLS + 3 * CONV_W + SSM_W + XATTN_W
SSM_GROUPS = 32
SSM_GROUP = 16
SSM_STATE = 64
N_STATE = SSM_GROUPS * SSM_STATE
ALPHA = 2.0 ** 0.25
LN_EPS = 1e-5
N_DEV = 8

ADAM_LR = 0.001
ADAM_B1 = 0.9
ADAM_B2 = 0.999
ADAM_EPS = 1e-08
ADAM_WD = 0.01
ADAM_STEP = 10

VMEM_LIMIT_V7X = 56 * 2 ** 20
SUBLANES = 8
LANES = 128

TOKEN_TILE = 256
SSM_BLOCK = 256
SSM_SEG = SSM_BLOCK // SUBLANES
LANE_CHUNK = 512
N_HALF = 2
HALF_W = SSM_W // N_HALF
HALF_STATE = N_STATE // N_HALF
HALF_COLS = 2 * HALF_STATE

NT = (((1,), (1,)), ((), ()))
TN = (((0,), (0,)), ((), ()))
NN = (((1,), (0,)), ((), ()))


def _dot(a, b, dims=NN):
    return lax.dot_general(a, b, dims, preferred_element_type=f32)


def _cparams(sem=None):
    return pltpu.CompilerParams(dimension_semantics=sem, vmem_limit_bytes=VMEM_LIMIT_V7X)


def _row_spec(tm, cols, rev_n=None):
    if rev_n is None:
        return pl.BlockSpec((tm, cols), lambda i: (i, 0))
    return pl.BlockSpec((tm, cols), lambda i: (rev_n - 1 - i, 0))


def _col_spec(rows, tm):
    return pl.BlockSpec((rows, tm), lambda i: (0, i))


def _const_spec(shape):
    nd = len(shape)
    return pl.BlockSpec(shape, lambda *_: (0,) * nd, pipeline_mode=pl.Buffered(1))


def _acc_spec(shape):
    nd = len(shape)
    return pl.BlockSpec(shape, lambda *_: (0,) * nd)


def _sds(shape, dtype):
    return jax.ShapeDtypeStruct(shape, dtype)


def _gelu(x):
    c = math.sqrt(2.0 / math.pi)
    return 0.5 * x * (1.0 + jnp.tanh(c * (x + 0.044715 * x * x * x)))


def _gelu_grad(x):
    c = math.sqrt(2.0 / math.pi)
    t = jnp.tanh(c * (x + 0.044715 * x * x * x))
    return 0.5 * (1.0 + t) + 0.5 * x * (1.0 - t * t) * c * (1.0 + 3.0 * 0.044715 * x * x)


def _colsum(a):
    return jnp.sum(a, axis=0, keepdims=True)


def _mesh_place():
    return lax.axis_index("x"), lax.axis_index("y"), lax.axis_index("c")


def _slot(p):
    return 4 * p[0] + 2 * p[1] + p[2]


def _other_devices(me):
    x, y, c = me
    flip = lambda v, d: 1 - v if d else v
    return [(flip(x, dx), flip(y, dy), flip(c, dc)) for dx in (0, 1) for dy in (0, 1) for dc in (0, 1)][1:]


def _all_gather(blocks, name):
    n = len(blocks)

    def body(*refs):
        ins, outs = refs[:n], refs[n:2 * n]
        send_sems, recv_sems, local_sems = refs[2 * n:]
        x, y, c = _mesh_place()
        me, sibling = (x, y, c), (x, y, 1 - c)
        chips = [(1 - x, y), (x, 1 - y), (1 - x, 1 - y)]

        def copy(a, k, block, to, src=None):
            rows = outs[a].at[_slot(block)]
            return pltpu.make_async_remote_copy(
                src_ref=rows if src is None else src, dst_ref=rows,
                send_sem=send_sems.at[a, k], recv_sem=recv_sems.at[a, k],
                device_id=to, device_id_type=pl.DeviceIdType.MESH)

        mine = [pltpu.make_async_copy(ins[a], outs[a].at[_slot(me)], local_sems.at[a]) for a in range(n)]
        for cp in mine:
            cp.start()
        first = []
        for a in range(n):
            first.append(copy(a, 0, me, sibling, src=ins[a]))
            first += [copy(a, 1 + j, me, (*chip, c), src=ins[a]) for j, chip in enumerate(chips)]
        for cp in first:
            cp.start()
        passed = []
        for a in range(n):
            for j, chip in enumerate(chips):
                copy(a, 1 + j, (*chip, c), me).wait_recv()
                fwd = copy(a, 4 + j, (*chip, c), sibling)
                fwd.start()
                passed.append(fwd)
        for a in range(n):
            copy(a, 0, sibling, me).wait_recv()
            for j, chip in enumerate(chips):
                copy(a, 4 + j, (*chip, 1 - c), me).wait_recv()
        for cp in first + passed:
            cp.wait_send()
        for cp in mine:
            cp.wait()

    any_spec = pl.BlockSpec(memory_space=pl.ANY)
    return pl.pallas_call(
        body, name=name,
        out_shape=[_sds((N_DEV,) + b.shape, b.dtype) for b in blocks],
        in_specs=[any_spec] * n, out_specs=[any_spec] * n,
        scratch_shapes=[pltpu.SemaphoreType.DMA((n, 7)), pltpu.SemaphoreType.DMA((n, 7)),
                        pltpu.SemaphoreType.DMA((n,))],
    )(*blocks)


def _side_gather_copies(ins, outs, send_sems, recv_sems, local_sems):
    me = _mesh_place()
    copies = []
    for a, (src, dst) in enumerate(zip(ins, outs)):
        copies.append(pltpu.make_async_copy(src, dst.at[_slot(me)], local_sems.at[a]))
        for k, peer in enumerate(_other_devices(me)):
            copies.append(pltpu.make_async_remote_copy(
                src_ref=src, dst_ref=dst.at[_slot(me)], send_sem=send_sems.at[a, k], recv_sem=recv_sems.at[a, k],
                device_id=peer, device_id_type=pl.DeviceIdType.MESH))
    return copies


def _side_gather_specs(blocks):
    n = len(blocks)
    any_spec = pl.BlockSpec(memory_space=pl.ANY)
    return ([any_spec] * n, [_sds((N_DEV,) + b.shape, b.dtype) for b in blocks],
            [pltpu.SemaphoreType.DMA((n, N_DEV - 1)), pltpu.SemaphoreType.DMA((n, N_DEV - 1)),
             pltpu.SemaphoreType.DMA((n,))])


def _kv_proj(mem, w_kv):
    def body(mem_ref, w_ref, kv_ref, memb_ref):
        mb = mem_ref[...].astype(bf16)
        memb_ref[...] = mb
        kv_ref[...] = _dot(mb, w_ref[...]).astype(bf16)

    return pl.pallas_call(
        body, name="kv_proj",
        out_shape=[_sds((MEM_LEN, 2 * XATTN_W), bf16), _sds((MEM_LEN, D_MODEL), bf16)],
        compiler_params=_cparams(),
    )(mem, w_kv)


def _attention_probs(qb, kv_ref, h):
    kh = kv_ref[:, h * HEAD_DIM:(h + 1) * HEAD_DIM]
    s = _dot(qb[:, h * HEAD_DIM:(h + 1) * HEAD_DIM], kh, NT) * (HEAD_DIM ** -0.5)
    e = jnp.exp(s - jnp.max(s, axis=-1, keepdims=True))
    return e / jnp.sum(e, axis=-1, keepdims=True)


def _in_proj(x, w_in_t, b_gate, conv_w, kv, side_blocks):
    s_len = x.shape[0]
    tm = TOKEN_TILE
    n = s_len // tm
    ns = len(side_blocks)
    side_in_specs, side_shapes, side_sems = _side_gather_specs(side_blocks)

    def body(*refs):
        (x_ref, win_ref, bg_ref, cw_ref, kv_ref) = refs[:5]
        side_ins = refs[5:5 + ns]
        (xbt_ref, g_ref, cin_ref, u_ref, q_ref, ain_ref, o_ref, aint_ref, ot_ref) = refs[5 + ns:14 + ns]
        side_outs = refs[14 + ns:14 + 2 * ns]
        zs_ref = refs[14 + 2 * ns]
        side = _side_gather_copies(side_ins, side_outs, *refs[15 + 2 * ns:])
        i = pl.program_id(0)

        @pl.when(i == 0)
        def _():
            for cp in side:
                cp.start()

        xb = x_ref[...].astype(bf16)
        xbt_ref[...] = xb.T
        proj = _dot(xb, win_ref[...], NT)
        g_ref[...] = jax.nn.sigmoid(proj[:, :GATE_COLS] + bg_ref[...])
        cin = proj[:, GATE_COLS:GATE_COLS + 3 * CONV_W]
        cin_ref[...] = cin
        u_ref[...] = proj[:, GATE_COLS + 3 * CONV_W:GATE_COLS + 3 * CONV_W + SSM_W]
        qb = proj[:, IN_COLS - XATTN_W:].astype(bf16)
        q_ref[...] = qb

        cb, cc, ch = cin[:, :CONV_W], cin[:, CONV_W:2 * CONV_W], cin[:, 2 * CONV_W:]
        z = cc * ch

        @pl.when(i == 0)
        def _():
            zs_ref[0:8, :] = jnp.zeros((8, CONV_W), f32)

        zs_ref[8:8 + tm, :] = z
        z1 = zs_ref[pl.ds(7, tm), :]
        z2 = zs_ref[pl.ds(6, tm), :]
        cw = cw_ref[...]
        cz = cw[0:1] * z2 + cw[1:2] * z1 + cw[2:3] * z
        zs_ref[0:8, :] = zs_ref[tm:tm + 8, :]
        ain = (cb * cz).astype(bf16)
        ain_ref[...] = ain
        aint_ref[...] = ain.T

        outs = []
        for h in range(HEADS):
            p = _attention_probs(qb, kv_ref, h)
            vh = kv_ref[:, XATTN_W + h * HEAD_DIM:XATTN_W + (h + 1) * HEAD_DIM]
            outs.append(_dot(p.astype(bf16), vh))
        ob = jnp.concatenate(outs, axis=1).astype(bf16)
        o_ref[...] = ob
        ot_ref[...] = ob.T

        @pl.when(i == n - 1)
        def _():
            for cp in side:
                cp.wait()

    row_cols = [(GATE_COLS, f32), (3 * CONV_W, f32), (SSM_W, f32), (XATTN_W, bf16), (CONV_W, bf16), (XATTN_W, bf16)]
    t_rows = [D_MODEL, CONV_W, XATTN_W]
    outs = pl.pallas_call(
        body, name="in_proj", grid=(n,),
        in_specs=[_row_spec(tm, D_MODEL), _const_spec((IN_COLS, D_MODEL)), _const_spec((1, GATE_COLS)),
                  _const_spec((3, CONV_W)), _const_spec((MEM_LEN, 2 * XATTN_W))] + side_in_specs,
        out_specs=([_col_spec(t_rows[0], tm)] + [_row_spec(tm, c) for c, _ in row_cols]
                   + [_col_spec(t_rows[1], tm), _col_spec(t_rows[2], tm)] + side_in_specs),
        out_shape=([_sds((t_rows[0], s_len), bf16)] + [_sds((s_len, c), dt) for c, dt in row_cols]
                   + [_sds((t_rows[1], s_len), bf16), _sds((t_rows[2], s_len), bf16)] + side_shapes),
        scratch_shapes=[pltpu.VMEM((tm + 8, CONV_W), f32)] + side_sems,
        compiler_params=_cparams(("arbitrary",)),
    )(x, w_in_t, b_gate, conv_w, kv, *side_blocks)
    return outs[:9], outs[9:]


def _state_cols(chunk):
    half, off = divmod(chunk * LANE_CHUNK, HALF_STATE)
    lo = half * HALF_COLS + off
    return slice(lo, lo + LANE_CHUNK), slice(lo + HALF_STATE, lo + HALF_STATE + LANE_CHUNK)


def _half_cols(half):
    lo = half * HALF_COLS
    return slice(lo, lo + HALF_STATE), slice(lo + HALF_STATE, lo + HALF_COLS)


def _rows_to_segments(src_ref, stage_ref, dst_ref):
    nc = SSM_W // LANES
    for c in range(nc):
        stage_ref[c] = src_ref[:, c * LANES:(c + 1) * LANES]
    for c in range(nc):
        for k in range(SSM_SEG):
            dst_ref[k * SUBLANES:(k + 1) * SUBLANES, c * LANES:(c + 1) * LANES] = (
                stage_ref[c, pl.ds(k, SUBLANES, stride=SSM_SEG), :])


def _rows_from_segments(src_ref, stage_ref, dst_ref):
    nc = SSM_W // LANES
    for c in range(nc):
        for k in range(SSM_SEG):
            stage_ref[c, pl.ds(k, SUBLANES, stride=SSM_SEG), :] = (
                src_ref[k * SUBLANES:(k + 1) * SUBLANES, c * LANES:(c + 1) * LANES])
    for c in range(nc):
        dst_ref[:, c * LANES:(c + 1) * LANES] = stage_ref[c]


def _ssm_scan(s_ref, pw_ref, init_ref, reverse, unroll):
    for chunk in range(N_STATE // LANE_CHUNK):
        re, im = _state_cols(chunk)
        ar = jnp.broadcast_to(pw_ref[0:1, re], (SUBLANES, LANE_CHUNK))
        ai = jnp.broadcast_to(pw_ref[0:1, im], (SUBLANES, LANE_CHUNK))
        if reverse:
            ai = -ai

        def step(j, carry, re=re, im=im, ar=ar, ai=ai):
            sr, si = carry
            k = (SSM_SEG - 1 - j) if reverse else j
            r0 = pl.multiple_of(k * SUBLANES, SUBLANES)
            nr = ar * sr - ai * si + s_ref[pl.ds(r0, SUBLANES), re]
            ni = ar * si + ai * sr + s_ref[pl.ds(r0, SUBLANES), im]
            s_ref[pl.ds(r0, SUBLANES), re] = nr
            s_ref[pl.ds(r0, SUBLANES), im] = ni
            return nr, ni

        if init_ref is None:
            init = (jnp.zeros((SUBLANES, LANE_CHUNK), f32),) * 2
        else:
            init = (init_ref[:, re], init_ref[:, im])
        lax.fori_loop(0, SSM_SEG, step, init, unroll=unroll)


def _ssm_add_carry(s_ref, pw_ref, cm_ref, reverse):
    for chunk in range(N_STATE // LANE_CHUNK):
        re, im = _state_cols(chunk)
        cr, ci = cm_ref[:, re], cm_ref[:, im]
        for k in range(SSM_SEG):
            pk = (SSM_SEG - 1 - k) if reverse else k
            pr = pw_ref[pk:pk + 1, re]
            pi = pw_ref[pk:pk + 1, im]
            if reverse:
                pi = -pi
            rows = slice(k * SUBLANES, (k + 1) * SUBLANES)
            s_ref[rows, re] = s_ref[rows, re] + (pr * cr - pi * ci)
            s_ref[rows, im] = s_ref[rows, im] + (pr * ci + pi * cr)


def _ssm_carries(first_row, s_ref, pw_ref, carry_ref, cm_ref, reverse):
    order = range(SUBLANES - 1, -1, -1) if reverse else range(SUBLANES)
    for half in range(N_HALF):
        re, im = _half_cols(half)
        a_r, a_i = pw_ref[SSM_SEG - 1:SSM_SEG, re], pw_ref[SSM_SEG - 1:SSM_SEG, im]
        if reverse:
            a_i = -a_i
        cr, ci = carry_ref[0:1, re], carry_ref[0:1, im]
        for seg in order:
            cm_ref[seg:seg + 1, re] = cr
            cm_ref[seg:seg + 1, im] = ci
            er = s_ref[first_row + seg:first_row + seg + 1, re]
            ei = s_ref[first_row + seg:first_row + seg + 1, im]
            cr, ci = a_r * cr - a_i * ci + er, a_r * ci + a_i * cr + ei
        carry_ref[0:1, re] = cr
        carry_ref[0:1, im] = ci


def _ssm_fwd(u, b_half, c_half, pw, d_skip, side_blocks):
    s_len = u.shape[0]
    tb = SSM_BLOCK
    n = s_len // tb
    ns = len(side_blocks)
    side_in_specs, side_shapes, side_sems = _side_gather_specs(side_blocks)

    def body(*refs):
        u_ref, b_ref, c_ref, pw_ref, d_ref = refs[:5]
        side_ins = refs[5:5 + ns]
        y_ref, cm_ref = refs[5 + ns:7 + ns]
        side_outs = refs[7 + ns:7 + 2 * ns]
        s_ref, carry_ref, up_ref, yp_ref, stage_ref = refs[7 + 2 * ns:12 + 2 * ns]
        side = _side_gather_copies(side_ins, side_outs, *refs[12 + 2 * ns:])
        i = pl.program_id(0)

        @pl.when(i == 0)
        def _():
            carry_ref[...] = jnp.zeros_like(carry_ref)
            for cp in side:
                cp.start()

        _rows_to_segments(u_ref, stage_ref, up_ref)
        u = up_ref[...]
        ub = u.astype(bf16)
        for half in range(N_HALF):
            s_ref[:, half * HALF_COLS:(half + 1) * HALF_COLS] = _dot(ub[:, half * HALF_W:(half + 1) * HALF_W], b_ref[half])
        _ssm_scan(s_ref, pw_ref, None, reverse=False, unroll=4)
        _ssm_carries(tb - SUBLANES, s_ref, pw_ref, carry_ref, cm_ref, reverse=False)
        _ssm_add_carry(s_ref, pw_ref, cm_ref, reverse=False)
        for half in range(N_HALF):
            cols = slice(half * HALF_W, (half + 1) * HALF_W)
            sb = s_ref[:, half * HALF_COLS:(half + 1) * HALF_COLS].astype(bf16)
            yp_ref[:, cols] = _dot(sb, c_ref[half]) + d_ref[:, cols] * u[:, cols]
        _rows_from_segments(yp_ref, stage_ref, y_ref)

        @pl.when(i == n - 1)
        def _():
            for cp in side:
                cp.wait()

    outs = pl.pallas_call(
        body, name="ssm_fwd", grid=(n,),
        in_specs=[_row_spec(tb, SSM_W), _const_spec((N_HALF, HALF_W, HALF_COLS)), _const_spec((N_HALF, HALF_COLS, HALF_W)),
                  _const_spec((SSM_SEG, 2 * N_STATE)), _const_spec((1, SSM_W))] + side_in_specs,
        out_specs=[_row_spec(tb, SSM_W), _row_spec(SUBLANES, 2 * N_STATE)] + side_in_specs,
        out_shape=[_sds((s_len, SSM_W), f32), _sds((n * SUBLANES, 2 * N_STATE), f32)] + side_shapes,
        scratch_shapes=[pltpu.VMEM((tb, 2 * N_STATE), f32), pltpu.VMEM((SUBLANES, 2 * N_STATE), f32),
                        pltpu.VMEM((tb, SSM_W), f32), pltpu.VMEM((tb, SSM_W), f32),
                        pltpu.VMEM((SSM_W // LANES, tb, LANES), f32)] + side_sems,
        compiler_params=_cparams(("arbitrary",)),
    )(u, b_half, c_half, pw, d_skip, *side_blocks)
    return outs[0], outs[1], outs[2:]


def _layer_norm_fwd(r, g, b):
    mu = jnp.mean(r, axis=-1, keepdims=True)
    var = jnp.mean(jnp.square(r - mu), axis=-1, keepdims=True)
    rstd = lax.rsqrt(var + LN_EPS)
    xhat = (r - mu) * rstd
    return xhat, rstd, xhat * g + b


def _layer_norm_bwd(dy, xhat, rstd, g):
    dxh = dy * g
    m1 = jnp.mean(dxh, axis=-1, keepdims=True)
    m2 = jnp.mean(dxh * xhat, axis=-1, keepdims=True)
    return rstd * (dxh - m1 - xhat * m2)


def _branch_outputs(ys_ref, ain_ref, o_ref, wglu_ref, wco_ref, wxo_ref):
    ysb = _gelu(ys_ref[...]).astype(bf16)
    glu = _dot(ysb, wglu_ref[...], NT)
    ga, sb = glu[:, :D_MODEL], jax.nn.sigmoid(glu[:, D_MODEL:])
    ya = _dot(ain_ref[...], wco_ref[...], NT)
    yc = _dot(o_ref[...], wxo_ref[...], NT)
    return ysb, ga, sb, ya, ga * sb, yc


def _mid_fwd(y_ssm, g, ain, ob, x, w_glu_t, w_co_t, w_xo_t, w_out, ln1_g, ln1_b):
    s_len = x.shape[0]
    tm = TOKEN_TILE
    n = s_len // tm

    def body(ys_ref, g_ref, ain_ref, o_ref, x_ref, wglu_ref, wco_ref, wxo_ref, wout_ref, lg_ref, lb_ref,
             ysbt_ref, mb_ref, xhat_ref, rstd_ref):
        ysb, _, _, ya, yb, yc = _branch_outputs(ys_ref, ain_ref, o_ref, wglu_ref, wco_ref, wxo_ref)
        ysbt_ref[...] = ysb.T
        gt = g_ref[...]
        merged = gt[:, :D_MODEL] * ya + gt[:, D_MODEL:2 * D_MODEL] * yb + gt[:, 2 * D_MODEL:] * yc
        mb = merged.astype(bf16)
        mb_ref[...] = mb
        r1 = ALPHA * x_ref[...] + _dot(mb, wout_ref[...])
        xhat, rstd, _ = _layer_norm_fwd(r1, lg_ref[...], lb_ref[...])
        xhat_ref[...] = xhat
        rstd_ref[...] = rstd

    row_cols = [(D_MODEL, bf16), (D_MODEL, f32), (1, f32)]
    return pl.pallas_call(
        body, name="mid_fwd", grid=(n,),
        in_specs=[_row_spec(tm, SSM_W), _row_spec(tm, GATE_COLS), _row_spec(tm, CONV_W), _row_spec(tm, XATTN_W),
                  _row_spec(tm, D_MODEL), _const_spec((2 * D_MODEL, SSM_W)), _const_spec((D_MODEL, CONV_W)),
                  _const_spec((D_MODEL, XATTN_W)), _const_spec((D_MODEL, D_MODEL)),
                  _const_spec((1, D_MODEL)), _const_spec((1, D_MODEL))],
        out_specs=[_col_spec(SSM_W, tm)] + [_row_spec(tm, c) for c, _ in row_cols],
        out_shape=[_sds((SSM_W, s_len), bf16)] + [_sds((s_len, c), dt) for c, dt in row_cols],
        compiler_params=_cparams(("parallel",)),
    )(y_ssm, g, ain, ob, x, w_glu_t, w_co_t, w_xo_t, w_out, ln1_g, ln1_b)


def _mlp_fwd_bwd(xhat1, tgt, ln1_g, ln1_b, w_up_t, b_up, w_down, b_down, ln2_g, ln2_b):
    s_len = xhat1.shape[0]
    tm = TOKEN_TILE
    n = s_len // tm
    fc = 1024
    nfc = D_FF // fc

    def body(xh_ref, t_ref, l1g_ref, l1b_ref, wup_ref, bup_ref, wdn_ref, bdn_ref, l2g_ref, l2b_ref,
             x1bt_ref, hdn_ref, dr2bt_ref, dpre_ref, dx1_ref,
             loss_ref, dl2g_ref, dl2b_ref, dbdn_ref, dbup_ref, rl_ref):
        i = pl.program_id(0)

        @pl.when(i == 0)
        def _():
            loss_ref[...] = jnp.zeros_like(loss_ref)
            dl2g_ref[...] = jnp.zeros_like(dl2g_ref)
            dl2b_ref[...] = jnp.zeros_like(dl2b_ref)
            dbdn_ref[...] = jnp.zeros_like(dbdn_ref)
            dbup_ref[...] = jnp.zeros_like(dbup_ref)

        x1 = xh_ref[...] * l1g_ref[...] + l1b_ref[...]
        x1b = x1.astype(bf16)
        x1bt_ref[...] = x1b.T
        acc = jnp.zeros((tm, D_MODEL), f32)
        for c in range(nfc):
            cols = slice(c * fc, (c + 1) * fc)
            pre = _dot(x1b, wup_ref[cols, :], NT) + bup_ref[:, cols]
            rl = jnp.maximum(pre, 0.0)
            rl_ref[:, cols] = rl
            hb = (rl * rl).astype(bf16)
            hdn_ref[:, cols] = hb
            acc = acc + _dot(hb, wdn_ref[cols, :])
        r2 = ALPHA * x1 + acc + bdn_ref[...]
        xhat2, rstd2, y = _layer_norm_fwd(r2, l2g_ref[...], l2b_ref[...])
        err = y - t_ref[...]
        loss_ref[...] += jnp.sum(jnp.sum(err * err, axis=1, keepdims=True), axis=0, keepdims=True) * (0.5 / D_MODEL)
        dy = err * (1.0 / D_MODEL)
        dl2g_ref[...] += _colsum(dy * xhat2)
        dl2b_ref[...] += _colsum(dy)
        dr2 = _layer_norm_bwd(dy, xhat2, rstd2, l2g_ref[...])
        dbdn_ref[...] += _colsum(dr2)
        dr2b = dr2.astype(bf16)
        dr2bt_ref[...] = dr2b.T
        dacc = jnp.zeros((tm, D_MODEL), f32)
        for c in range(nfc):
            cols = slice(c * fc, (c + 1) * fc)
            dh = _dot(dr2b, wdn_ref[cols, :], NT)
            dpre = dh * (2.0 * rl_ref[:, cols])
            dbup_ref[:, cols] += _colsum(dpre)
            dpb = dpre.astype(bf16)
            dpre_ref[:, cols] = dpb
            dacc = dacc + _dot(dpb, wup_ref[cols, :])
        dx1_ref[...] = ALPHA * dr2 + dacc

    acc_shapes = [(1, LANES), (1, D_MODEL), (1, D_MODEL), (1, D_MODEL), (1, D_FF)]
    return pl.pallas_call(
        body, name="mlp_fwd_bwd", grid=(n,),
        in_specs=[_row_spec(tm, D_MODEL), _row_spec(tm, D_MODEL), _const_spec((1, D_MODEL)), _const_spec((1, D_MODEL)),
                  _const_spec((D_FF, D_MODEL)), _const_spec((1, D_FF)), _const_spec((D_FF, D_MODEL)),
                  _const_spec((1, D_MODEL)), _const_spec((1, D_MODEL)), _const_spec((1, D_MODEL))],
        out_specs=([_col_spec(D_MODEL, tm), _row_spec(tm, D_FF), _col_spec(D_MODEL, tm), _row_spec(tm, D_FF),
                    _row_spec(tm, D_MODEL)] + [_acc_spec(s) for s in acc_shapes]),
        out_shape=([_sds((D_MODEL, s_len), bf16), _sds((s_len, D_FF), bf16), _sds((D_MODEL, s_len), bf16),
                    _sds((s_len, D_FF), bf16), _sds((s_len, D_MODEL), f32)] + [_sds(s, f32) for s in acc_shapes]),
        scratch_shapes=[pltpu.VMEM((tm, D_FF), f32)],
        compiler_params=_cparams(("arbitrary",)),
    )(xhat1, tgt, ln1_g, ln1_b, w_up_t, b_up, w_down, b_down, ln2_g, ln2_b)


def _mid_bwd(dx1, xhat1, rstd1, g, ain, ob, y_ssm, ln1_g, w_out, w_glu_t, w_co_t, w_xo_t):
    s_len = dx1.shape[0]
    tm = TOKEN_TILE
    n = s_len // tm

    def body(dx1_ref, xh_ref, rs_ref, g_ref, ain_ref, o_ref, ys_ref, lg_ref, wout_ref, wglu_ref, wco_ref, wxo_ref,
             dxp_ref, dr1bt_ref, dgp_ref, dya_ref, dyc_ref, dglu_ref, dyssm_ref,
             dl1g_ref, dl1b_ref, dbg_ref):
        i = pl.program_id(0)

        @pl.when(i == 0)
        def _():
            dl1g_ref[...] = jnp.zeros_like(dl1g_ref)
            dl1b_ref[...] = jnp.zeros_like(dl1b_ref)
            dbg_ref[...] = jnp.zeros_like(dbg_ref)

        dx1 = dx1_ref[...]
        xhat = xh_ref[...]
        dl1g_ref[...] += _colsum(dx1 * xhat)
        dl1b_ref[...] += _colsum(dx1)
        dr1 = _layer_norm_bwd(dx1, xhat, rs_ref[...], lg_ref[...])
        dxp_ref[...] = ALPHA * dr1
        dr1b = dr1.astype(bf16)
        dr1bt_ref[...] = dr1b.T
        dm = _dot(dr1b, wout_ref[...], NT)

        _, ga, sb, ya, yb, yc = _branch_outputs(ys_ref, ain_ref, o_ref, wglu_ref, wco_ref, wxo_ref)
        gt = g_ref[...]
        branch = (ya, yb, yc)
        for j in range(3):
            cols = slice(j * D_MODEL, (j + 1) * D_MODEL)
            gj = gt[:, cols]
            dgp = dm * branch[j] * gj * (1.0 - gj)
            dbg_ref[:, cols] += _colsum(dgp)
            dgp_ref[:, cols] = dgp.astype(bf16)
        dya_ref[...] = (dm * gt[:, :D_MODEL]).astype(bf16)
        dyc_ref[...] = (dm * gt[:, 2 * D_MODEL:]).astype(bf16)
        dyb = dm * gt[:, D_MODEL:2 * D_MODEL]
        dga = (dyb * sb).astype(bf16)
        dgb = (dyb * ga * sb * (1.0 - sb)).astype(bf16)
        dglu_ref[:, :D_MODEL] = dga
        dglu_ref[:, D_MODEL:] = dgb
        dys = _dot(dga, wglu_ref[:D_MODEL, :]) + _dot(dgb, wglu_ref[D_MODEL:, :])
        dyssm_ref[...] = dys * _gelu_grad(ys_ref[...])

    row_cols = [(GATE_COLS, bf16), (D_MODEL, bf16), (D_MODEL, bf16), (2 * D_MODEL, bf16), (SSM_W, f32)]
    acc_shapes = [(1, D_MODEL), (1, D_MODEL), (1, GATE_COLS)]
    return pl.pallas_call(
        body, name="mid_bwd", grid=(n,),
        in_specs=[_row_spec(tm, D_MODEL), _row_spec(tm, D_MODEL), _row_spec(tm, 1), _row_spec(tm, GATE_COLS),
                  _row_spec(tm, CONV_W), _row_spec(tm, XATTN_W), _row_spec(tm, SSM_W),
                  _const_spec((1, D_MODEL)), _const_spec((D_MODEL, D_MODEL)), _const_spec((2 * D_MODEL, SSM_W)),
                  _const_spec((D_MODEL, CONV_W)), _const_spec((D_MODEL, XATTN_W))],
        out_specs=([_row_spec(tm, D_MODEL), _col_spec(D_MODEL, tm)] + [_row_spec(tm, c) for c, _ in row_cols]
                   + [_acc_spec(s) for s in acc_shapes]),
        out_shape=([_sds((s_len, D_MODEL), f32), _sds((D_MODEL, s_len), bf16)]
                   + [_sds((s_len, c), dt) for c, dt in row_cols] + [_sds(s, f32) for s in acc_shapes]),
        compiler_params=_cparams(("arbitrary",)),
    )(dx1, xhat1, rstd1, g, ain, ob, y_ssm, ln1_g, w_out, w_glu_t, w_co_t, w_xo_t)


def _ssm_bwd(u, dy, cm_all, b_half, c_half, pw, d_skip):
    s_len = u.shape[0]
    tb = SSM_BLOCK
    n = s_len // tb

    def body(u_ref, dy_ref, cm_ref, b_ref, c_ref, pw_ref, d_ref,
             du_ref, db_hbm, dc_hbm, da_ref, dd_ref,
             s_ref, g_ref, gcarry_ref, gcm_ref, db_ref, dc_ref, up_ref, dyp_ref, dup_ref, stage_ref):
        i = pl.program_id(0)

        @pl.when(i == 0)
        def _():
            gcarry_ref[...] = jnp.zeros_like(gcarry_ref)
            db_ref[...] = jnp.zeros_like(db_ref)
            dc_ref[...] = jnp.zeros_like(dc_ref)
            da_ref[...] = jnp.zeros_like(da_ref)
            dd_ref[...] = jnp.zeros_like(dd_ref)

        _rows_to_segments(u_ref, stage_ref, up_ref)
        _rows_to_segments(dy_ref, stage_ref, dyp_ref)
        u = up_ref[...]
        ub = u.astype(bf16)
        dy = dyp_ref[...]
        dyb = dy.astype(bf16)
        dd_ref[...] += _colsum(dy * u)

        for half in range(N_HALF):
            s_ref[:, half * HALF_COLS:(half + 1) * HALF_COLS] = _dot(ub[:, half * HALF_W:(half + 1) * HALF_W], b_ref[half])
        _ssm_scan(s_ref, pw_ref, cm_ref, reverse=False, unroll=True)

        for half in range(N_HALF):
            g_ref[:, half * HALF_COLS:(half + 1) * HALF_COLS] = _dot(dyb[:, half * HALF_W:(half + 1) * HALF_W], c_ref[half], NT)
        _ssm_scan(g_ref, pw_ref, None, reverse=True, unroll=True)
        _ssm_carries(0, g_ref, pw_ref, gcarry_ref, gcm_ref, reverse=True)
        _ssm_add_carry(g_ref, pw_ref, gcm_ref, reverse=True)

        for half in range(N_HALF):
            cols = slice(half * HALF_W, (half + 1) * HALF_W)
            scols = slice(half * HALF_COLS, (half + 1) * HALF_COLS)
            gb = g_ref[:, scols].astype(bf16)
            dup_ref[:, cols] = _dot(gb, b_ref[half], NT) + d_ref[:, cols] * dy[:, cols]
            db_ref[half] += _dot(ub[:, cols], gb, TN)
            dc_ref[half] += _dot(s_ref[:, scols].astype(bf16), dyb[:, cols], TN)
        _rows_from_segments(dup_ref, stage_ref, du_ref)

        for chunk in range(N_STATE // LANE_CHUNK):
            re, im = _state_cols(chunk)
            acc_r = da_ref[:, re]
            acc_i = da_ref[:, im]
            for k in range(SSM_SEG):
                rows = slice(k * SUBLANES, (k + 1) * SUBLANES)
                if k == 0:
                    pr, pi = cm_ref[:, re], cm_ref[:, im]
                else:
                    prev = slice((k - 1) * SUBLANES, k * SUBLANES)
                    pr, pi = s_ref[prev, re], s_ref[prev, im]
                gr, gi = g_ref[rows, re], g_ref[rows, im]
                acc_r = acc_r + (gr * pr + gi * pi)
                acc_i = acc_i + (gi * pr - gr * pi)
            da_ref[:, re] = acc_r
            da_ref[:, im] = acc_i

        @pl.when(i == n - 1)
        def _():
            pltpu.sync_copy(db_ref, db_hbm)
            pltpu.sync_copy(dc_ref, dc_hbm)

    rev = functools.partial(_row_spec, rev_n=n)
    any_spec = pl.BlockSpec(memory_space=pl.ANY)
    state_rows = pltpu.VMEM((tb, 2 * N_STATE), f32)
    seg_rows = pltpu.VMEM((SUBLANES, 2 * N_STATE), f32)
    tok_rows = pltpu.VMEM((tb, SSM_W), f32)
    return pl.pallas_call(
        body, name="ssm_bwd", grid=(n,),
        in_specs=[rev(tb, SSM_W), rev(tb, SSM_W), rev(SUBLANES, 2 * N_STATE),
                  _const_spec((N_HALF, HALF_W, HALF_COLS)), _const_spec((N_HALF, HALF_COLS, HALF_W)),
                  _const_spec((SSM_SEG, 2 * N_STATE)), _const_spec((1, SSM_W))],
        out_specs=[rev(tb, SSM_W), any_spec, any_spec, _acc_spec((SUBLANES, 2 * N_STATE)), _acc_spec((1, SSM_W))],
        out_shape=[_sds((s_len, SSM_W), f32), _sds((N_HALF, HALF_W, HALF_COLS), f32),
                   _sds((N_HALF, HALF_COLS, HALF_W), f32), _sds((SUBLANES, 2 * N_STATE), f32), _sds((1, SSM_W), f32)],
        scratch_shapes=[state_rows, state_rows, seg_rows, seg_rows,
                        pltpu.VMEM((N_HALF, HALF_W, HALF_COLS), f32), pltpu.VMEM((N_HALF, HALF_COLS, HALF_W), f32),
                        tok_rows, tok_rows, tok_rows, pltpu.VMEM((SSM_W // LANES, tb, LANES), f32)],
        compiler_params=_cparams(("arbitrary",)),
    )(u, dy, cm_all, b_half, c_half, pw, d_skip)


def _branch_bwd(dya, dyc, cin, q, kv, conv_w, w_co_t, w_xo_t):
    s_len = dya.shape[0]
    tm = TOKEN_TILE
    n = s_len // tm
    halo_blocks = tm // 8

    def body(dya_ref, dyc_ref, cin_ref, cprev_ref, q_ref, kv_ref, cw_ref, wco_ref, wxo_ref,
             dconv_ref, dq_ref, dcw_ref, dkv_ref, zs_ref, dczs_ref):
        i = pl.program_id(0)
        tile = n - 1 - i

        @pl.when(i == 0)
        def _():
            dcw_ref[...] = jnp.zeros_like(dcw_ref)
            dkv_ref[...] = jnp.zeros_like(dkv_ref)
            dczs_ref[tm:tm + 8, :] = jnp.zeros((8, CONV_W), f32)

        cin = cin_ref[...]
        cb, cc, ch = cin[:, :CONV_W], cin[:, CONV_W:2 * CONV_W], cin[:, 2 * CONV_W:]
        z = cc * ch
        cprev = cprev_ref[...]
        zprev = cprev[:, CONV_W:2 * CONV_W] * cprev[:, 2 * CONV_W:]
        zs_ref[0:8, :] = jnp.where(tile == 0, 0.0, zprev)
        zs_ref[8:8 + tm, :] = z
        z1 = zs_ref[pl.ds(7, tm), :]
        z2 = zs_ref[pl.ds(6, tm), :]
        cw = cw_ref[...]
        cz = cw[0:1] * z2 + cw[1:2] * z1 + cw[2:3] * z

        dain = _dot(dya_ref[...], wco_ref[...])
        dcb = dain * cz
        dcz = dain * cb
        dczs_ref[0:tm, :] = dcz
        dcz1 = dczs_ref[pl.ds(1, tm), :]
        dcz2 = dczs_ref[pl.ds(2, tm), :]
        dz = cw[2:3] * dcz + cw[1:2] * dcz1 + cw[0:1] * dcz2
        dczs_ref[tm:tm + 8, :] = dczs_ref[0:8, :]
        dcw_ref[0:1, :] += _colsum(dcz * z2)
        dcw_ref[1:2, :] += _colsum(dcz * z1)
        dcw_ref[2:3, :] += _colsum(dcz * z)
        dconv_ref[:, :CONV_W] = dcb.astype(bf16)
        dconv_ref[:, CONV_W:2 * CONV_W] = (dz * ch).astype(bf16)
        dconv_ref[:, 2 * CONV_W:] = (dz * cc).astype(bf16)

        qb = q_ref[...]
        do = _dot(dyc_ref[...], wxo_ref[...])
        for h in range(HEADS):
            hc = slice(h * HEAD_DIM, (h + 1) * HEAD_DIM)
            vc = slice(XATTN_W + h * HEAD_DIM, XATTN_W + (h + 1) * HEAD_DIM)
            p = _attention_probs(qb, kv_ref, h)
            dob = do[:, hc].astype(bf16)
            dp = _dot(dob, kv_ref[:, vc], NT)
            dkv_ref[:, vc] += _dot(p.astype(bf16), dob, TN)
            ds = p * (dp - jnp.sum(dp * p, axis=-1, keepdims=True)) * (HEAD_DIM ** -0.5)
            dsb = ds.astype(bf16)
            dq_ref[:, hc] = _dot(dsb, kv_ref[:, hc]).astype(bf16)
            dkv_ref[:, hc] += _dot(dsb, qb[:, hc], TN)

    rev = functools.partial(_row_spec, rev_n=n)
    prev_spec = pl.BlockSpec((8, 3 * CONV_W), lambda i: (jnp.maximum((n - 1 - i) * halo_blocks - 1, 0), 0))
    return pl.pallas_call(
        body, name="branch_bwd", grid=(n,),
        in_specs=[rev(tm, D_MODEL), rev(tm, D_MODEL), rev(tm, 3 * CONV_W), prev_spec, rev(tm, XATTN_W),
                  _const_spec((MEM_LEN, 2 * XATTN_W)), _const_spec((3, CONV_W)), _const_spec((D_MODEL, CONV_W)),
                  _const_spec((D_MODEL, XATTN_W))],
        out_specs=[rev(tm, 3 * CONV_W), rev(tm, XATTN_W), _acc_spec((8, CONV_W)), _acc_spec((MEM_LEN, 2 * XATTN_W))],
        out_shape=[_sds((s_len, 3 * CONV_W), bf16), _sds((s_len, XATTN_W), bf16), _sds((8, CONV_W), f32),
                   _sds((MEM_LEN, 2 * XATTN_W), f32)],
        scratch_shapes=[pltpu.VMEM((tm + 8, CONV_W), f32), pltpu.VMEM((tm + 8, CONV_W), f32)],
        compiler_params=_cparams(("arbitrary",)),
    )(dya, dyc, cin, cin, q, kv, conv_w, w_co_t, w_xo_t)


def _in_proj_bwd(dgp, dconv, du, dq, dxp, w_in_t, side_blocks):
    s_len = dgp.shape[0]
    tm = TOKEN_TILE
    n = s_len // tm
    ns = len(side_blocks)
    side_in_specs, side_shapes, side_sems = _side_gather_specs(side_blocks)

    def body(*refs):
        dgp_ref, dconv_ref, du_ref, dq_ref, dxp_ref, win_ref = refs[:6]
        side_ins = refs[6:6 + ns]
        dx_ref, dproj_ref = refs[6 + ns:8 + ns]
        side_outs = refs[8 + ns:8 + 2 * ns]
        side = _side_gather_copies(side_ins, side_outs, *refs[8 + 2 * ns:])
        i = pl.program_id(0)

        @pl.when(i == 0)
        def _():
            for cp in side:
                cp.start()

        dproj = jnp.concatenate([dgp_ref[...], dconv_ref[...], du_ref[...].astype(bf16), dq_ref[...]], axis=1)
        dproj_ref[...] = dproj
        dx_ref[...] = dxp_ref[...] + _dot(dproj, win_ref[...])

        @pl.when(i == n - 1)
        def _():
            for cp in side:
                cp.wait()

    outs = pl.pallas_call(
        body, name="in_proj_bwd", grid=(n,),
        in_specs=[_row_spec(tm, GATE_COLS), _row_spec(tm, 3 * CONV_W), _row_spec(tm, SSM_W), _row_spec(tm, XATTN_W),
                  _row_spec(tm, D_MODEL), _const_spec((IN_COLS, D_MODEL))] + side_in_specs,
        out_specs=[_row_spec(tm, D_MODEL), _row_spec(tm, IN_COLS)] + side_in_specs,
        out_shape=[_sds((s_len, D_MODEL), f32), _sds((s_len, IN_COLS), bf16)] + side_shapes,
        scratch_shapes=side_sems,
        compiler_params=_cparams(("arbitrary",)),
    )(dgp, dconv, du, dq, dxp, w_in_t, *side_blocks)
    return outs[0], outs[1], outs[2:]


N_CHIP = 4
CHIP_STEPS = [(1, 1), (1, 0), (0, 1), (0, 0)]


def _flip(v, d):
    return 1 - v if d else v


def _chip_order():
    x, y, _ = _mesh_place()
    return jnp.stack([2 * _flip(x, dx) + _flip(y, dy) for dx, dy in CHIP_STEPS]).astype(jnp.int32)


def _weight_grad_scatter(a_t, b, name, tm, tt):
    m, s_len = a_t.shape
    n_cols = b.shape[1]
    w = n_cols // N_DEV
    tn = 2 * w
    tm, tt = min(tm, m), min(tt, s_len)
    nm, nt = m // tm, s_len // tt
    assert m % tm == 0 and s_len % tt == 0

    def body(order_ref, a_ref, b_ref, recv_ref, acc_ref, send_ref, sib_ref, stash_ref,
             d2d_send, d2d_recv, ici_send, ici_recv, local_sem):
        del order_ref
        q, im, t = pl.program_id(0), pl.program_id(1), pl.program_id(2)
        x, y, c = _mesh_place()
        mesh_id = pl.DeviceIdType.MESH

        @pl.when(t == 0)
        def _():
            acc_ref[...] = jnp.zeros_like(acc_ref)

        acc_ref[...] += _dot(a_ref[...], b_ref[...])

        def to_sibling(qq, imm):
            rows = pl.ds(pl.multiple_of(imm * tm, tm), tm)
            return pltpu.make_async_remote_copy(
                src_ref=send_ref.at[qq, 0, rows, :], dst_ref=sib_ref.at[qq, rows, :],
                send_sem=d2d_send.at[qq], recv_sem=d2d_recv.at[qq, imm],
                device_id=(x, y, 1 - c), device_id_type=mesh_id)

        def finish_tile(qq, imm):
            rows = pl.ds(pl.multiple_of(imm * tm, tm), tm)
            to_sibling(qq, imm).wait_recv()
            both = stash_ref[...] + sib_ref[qq, rows, :].astype(f32)
            send_ref[qq, 1, rows, :] = both.astype(bf16)
            for step, (dx, dy) in enumerate(CHIP_STEPS):
                @pl.when(qq == step)
                def _(step=step, dx=dx, dy=dy):
                    src, dst = send_ref.at[step, 1, rows, :], recv_ref.at[step, rows, :]
                    if dx or dy:
                        pltpu.make_async_remote_copy(
                            src_ref=src, dst_ref=dst, send_sem=ici_send.at[step], recv_sem=ici_recv.at[step],
                            device_id=(_flip(x, dx), _flip(y, dy), c), device_id_type=mesh_id).start()
                    else:
                        pltpu.make_async_copy(src, dst, local_sem).start()

        @pl.when(t == nt - 1)
        def _():
            tile = q * nm + im

            @pl.when(tile > 0)
            def _():
                finish_tile((tile - 1) // nm, (tile - 1) % nm)

            rows = pl.ds(pl.multiple_of(im * tm, tm), tm)
            for core in (0, 1):
                @pl.when(c == core)
                def _(core=core):
                    other = 1 - core
                    send_ref[q, 0, rows, :] = acc_ref[:, other * w:(other + 1) * w].astype(bf16)
                    stash_ref[...] = acc_ref[:, core * w:(core + 1) * w]
            to_sibling(q, im).start()

            @pl.when(tile == N_CHIP * nm - 1)
            def _():
                finish_tile(q, im)

        @pl.when((q == N_CHIP - 1) & (im == nm - 1) & (t == nt - 1))
        def _():
            for step, (dx, dy) in enumerate(CHIP_STEPS):
                pltpu.make_async_remote_copy(
                    src_ref=send_ref.at[step, 0], dst_ref=sib_ref.at[step],
                    send_sem=d2d_send.at[step], recv_sem=d2d_recv.at[step, 0],
                    device_id=(x, y, 1 - c), device_id_type=mesh_id).wait_send()
                src, dst = send_ref.at[step, 1], recv_ref.at[step]
                if dx or dy:
                    pltpu.make_async_remote_copy(
                        src_ref=src, dst_ref=dst, send_sem=ici_send.at[step], recv_sem=ici_recv.at[step],
                        device_id=(_flip(x, dx), _flip(y, dy), c), device_id_type=mesh_id).wait()
                else:
                    pltpu.make_async_copy(src, dst, local_sem).wait()

    grid_spec = pltpu.PrefetchScalarGridSpec(
        num_scalar_prefetch=1, grid=(N_CHIP, nm, nt),
        in_specs=[pl.BlockSpec((tm, tt), lambda q, im, t, order: (im, t)),
                  pl.BlockSpec((tt, tn), lambda q, im, t, order: (t, order[q]))],
        out_specs=pl.BlockSpec(memory_space=pl.ANY),
        scratch_shapes=[pltpu.VMEM((tm, tn), f32), pltpu.VMEM((N_CHIP, 2, m, w), bf16), pltpu.VMEM((N_CHIP, m, w), bf16),
                        pltpu.VMEM((tm, w), f32),
                        pltpu.SemaphoreType.DMA((N_CHIP,)), pltpu.SemaphoreType.DMA((N_CHIP, nm)),
                        pltpu.SemaphoreType.DMA((N_CHIP - 1,)), pltpu.SemaphoreType.DMA((N_CHIP - 1,)),
                        pltpu.SemaphoreType.DMA])
    return pl.pallas_call(
        body, name=name, grid_spec=grid_spec,
        out_shape=_sds((N_CHIP, m, w), bf16),
        compiler_params=_cparams(("arbitrary", "arbitrary", "arbitrary")),
    )(_chip_order(), a_t, b)


def _adamw(w, g, m, v):
    m = ADAM_B1 * m + (1.0 - ADAM_B1) * g
    v = ADAM_B2 * v + (1.0 - ADAM_B2) * jnp.square(g)
    m_hat = m / (1.0 - ADAM_B1 ** ADAM_STEP)
    v_hat = v / (1.0 - ADAM_B2 ** ADAM_STEP)
    delta = -ADAM_LR * (m_hat / (jnp.sqrt(v_hat) + ADAM_EPS) + ADAM_WD * w)
    return delta, m, v


def _sum_parts(p_ref):
    g = p_ref[0].astype(f32)
    for j in range(1, p_ref.shape[0]):
        g = g + p_ref[j].astype(f32)
    return g


def _adamw_update(w, m, v, parts, name, transposed):
    rows, cols = w.shape
    n_parts = parts.shape[0]
    if transposed:
        tr = LANES
        p_spec = pl.BlockSpec((n_parts, cols, tr), lambda i: (0, 0, i))
    else:
        tr = next(t for t in (256, 128, 64, 32, 16, 8) if rows % t == 0)
        p_spec = pl.BlockSpec((n_parts, tr, cols), lambda i: (0, i, 0))
    spec = pl.BlockSpec((tr, cols), lambda i: (i, 0))

    def body(w_ref, p_ref, m_ref, v_ref, g_ref, d_ref, nm_ref, nv_ref):
        g = _sum_parts(p_ref)
        if transposed:
            g = g.T
        g_ref[...] = g
        d_ref[...], nm_ref[...], nv_ref[...] = _adamw(w_ref[...], g, m_ref[...], v_ref[...])

    return pl.pallas_call(
        body, name=name, grid=(rows // tr,),
        in_specs=[spec, p_spec, spec, spec], out_specs=[spec] * 4,
        out_shape=[_sds((rows, cols), f32)] * 4,
        compiler_params=_cparams(("parallel",)),
    )(w, parts, m, v)


SMALL_GROUPS = [
    (["b_gate", "ln1_g", "ln1_b", "b_up", "b_down", "ln2_g", "ln2_b", "ssm_d"], 1),
    (["ssm_lam_re", "ssm_lam_im", "ssm_c_re", "ssm_c_im"], 0),
    (["ssm_b_re", "ssm_b_im"], 0),
    (["conv_w"], 0),
    (["ssm_log_dt"], 0),
]


def _sum_small(group_parts):
    def body(*refs):
        n = len(refs) // 2
        for p_ref, o_ref in zip(refs[:n], refs[n:]):
            o_ref[...] = _sum_parts(p_ref)

    return pl.pallas_call(
        body, name="sum_small",
        out_shape=[_sds(p.shape[1:], f32) for p in group_parts],
        compiler_params=_cparams(),
    )(*group_parts)


def _adamw_small(ws, ms, vs, group_sums):
    names = [k for group, _ in SMALL_GROUPS for k in group]
    n = len(names)

    def body(*refs):
        w_refs, m_refs, v_refs = (dict(zip(names, refs[j * n:(j + 1) * n])) for j in range(3))
        p_refs = refs[3 * n:3 * n + len(SMALL_GROUPS)]
        out_refs = [dict(zip(names, refs[3 * n + len(SMALL_GROUPS) + j * n:][:n])) for j in range(4)]
        for (group, axis), p_ref in zip(SMALL_GROUPS, p_refs):
            total = p_ref[...]
            off = 0
            for k in group:
                size = SMALL[k][axis]
                g = total[:, off:off + size] if axis == 1 else total[off:off + size, :]
                off += size
                d, nm, nv = _adamw(w_refs[k][...], g, m_refs[k][...], v_refs[k][...])
                for j, val in enumerate((g, d, nm, nv)):
                    out_refs[j][k][...] = val

    res = pl.pallas_call(
        body, name="adamw_small",
        out_shape=[_sds(SMALL[k], f32) for _ in range(4) for k in names],
        compiler_params=_cparams(),
    )(*[ws[k] for k in names], *[ms[k] for k in names], *[vs[k] for k in names], *group_sums)
    return [dict(zip(names, res[j * n:(j + 1) * n])) for j in range(4)]


def _ssm_discretize(lam_re, lam_im, log_dt, b_re, b_im):
    dt = jnp.exp(log_dt)[:, None]
    mag = jnp.exp(lam_re * dt)
    abar_r = mag * jnp.cos(lam_im * dt)
    abar_i = mag * jnp.sin(lam_im * dt)
    den = lam_re * lam_re + lam_im * lam_im
    nr = abar_r - 1.0
    ni = abar_i
    kr = (nr * lam_re + ni * lam_im) / den
    ki = (ni * lam_re - nr * lam_im) / den
    bbar_r = kr[..., None] * b_re - ki[..., None] * b_im
    bbar_i = kr[..., None] * b_im + ki[..., None] * b_re
    return abar_r, abar_i, bbar_r, bbar_i


def _state_layout(re, im):
    parts = []
    for half in range(N_HALF):
        cols = slice(half * HALF_STATE, (half + 1) * HALF_STATE)
        parts += [re[..., cols], im[..., cols]]
    return jnp.concatenate(parts, axis=-1)


def _state_unlayout(a):
    re = jnp.concatenate([a[..., _half_cols(h)[0]] for h in range(N_HALF)], axis=-1)
    im = jnp.concatenate([a[..., _half_cols(h)[1]] for h in range(N_HALF)], axis=-1)
    return re, im


def _abar_powers(abar_r, abar_i):
    pr, pi = abar_r.reshape(1, N_STATE), abar_i.reshape(1, N_STATE)
    while pr.shape[0] < SSM_SEG:
        tr, ti = pr[-1:], pi[-1:]
        pr, pi = (jnp.concatenate([pr, pr * tr - pi * ti], axis=0), jnp.concatenate([pi, pr * ti + pi * tr], axis=0))
    return _state_layout(pr, pi)


HALF_GROUPS = SSM_GROUPS // N_HALF


def _half_block_diag(blocks):
    _, r, c = blocks.shape
    eye = jnp.eye(HALF_GROUPS, dtype=blocks.dtype)
    b4 = blocks.reshape(N_HALF, HALF_GROUPS, r, c)
    return jnp.einsum("ngrc,gk->ngrkc", b4, eye).reshape(N_HALF, HALF_GROUPS * r, HALF_GROUPS * c)


def _half_diag_blocks(mat, r, c):
    eye = jnp.eye(HALF_GROUPS, dtype=mat.dtype)
    m5 = mat.reshape(N_HALF, HALF_GROUPS, r, HALF_GROUPS, c)
    return jnp.einsum("ngrkc,gk->ngrc", m5, eye).reshape(SSM_GROUPS, r, c)


BIG = ["w_in", "w_conv_out", "w_glu", "w_kv", "w_xattn_out", "w_out", "w_up", "w_down"]
COL_SHARDED = ["w_in", "w_conv_out", "w_glu", "w_xattn_out", "w_up"]
SMALL = {"b_gate": (1, GATE_COLS), "conv_w": (3, CONV_W), "ssm_lam_re": (SSM_GROUPS, SSM_STATE),
         "ssm_lam_im": (SSM_GROUPS, SSM_STATE), "ssm_log_dt": (1, SSM_GROUPS),
         "ssm_b_re": (N_STATE, SSM_GROUP), "ssm_b_im": (N_STATE, SSM_GROUP),
         "ssm_c_re": (SSM_W, SSM_STATE), "ssm_c_im": (SSM_W, SSM_STATE), "ssm_d": (1, SSM_W),
         "ln1_g": (1, D_MODEL), "ln1_b": (1, D_MODEL), "b_up": (1, D_FF), "b_down": (1, D_MODEL),
         "ln2_g": (1, D_MODEL), "ln2_b": (1, D_MODEL)}
WEIGHTS = ["w_in", "b_gate", "conv_w", "w_conv_out", "ssm_lam_re", "ssm_lam_im", "ssm_log_dt", "ssm_b_re", "ssm_b_im",
           "ssm_c_re", "ssm_c_im", "ssm_d", "w_glu", "w_kv", "w_xattn_out", "w_out", "ln1_g", "ln1_b", "w_up", "b_up",
           "w_down", "b_down", "ln2_g", "ln2_b"]


def _local_step(x, mem, tgt, full, late, small):
    lam_re, lam_im, log_dt = small["ssm_lam_re"], small["ssm_lam_im"], small["ssm_log_dt"].reshape(SSM_GROUPS)
    b_shape = (SSM_GROUPS, SSM_STATE, SSM_GROUP)
    c_shape = (SSM_GROUPS, SSM_GROUP, SSM_STATE)
    disc, disc_vjp = jax.vjp(_ssm_discretize, lam_re, lam_im, log_dt,
                             small["ssm_b_re"].reshape(b_shape), small["ssm_b_im"].reshape(b_shape))
    abar_r, abar_i, bbar_r, bbar_i = disc
    pw = _abar_powers(abar_r, abar_i)
    c_re, c_im = small["ssm_c_re"].reshape(c_shape), small["ssm_c_im"].reshape(c_shape)
    b_half = jnp.concatenate([_half_block_diag(bbar_r.transpose(0, 2, 1)), _half_block_diag(bbar_i.transpose(0, 2, 1))],
                             axis=2).astype(bf16)
    c_half = jnp.concatenate([_half_block_diag(c_re.transpose(0, 2, 1)), -_half_block_diag(c_im.transpose(0, 2, 1))],
                             axis=1).astype(bf16)

    s_len = x.shape[0]
    stack = lambda a: a.reshape(-1, a.shape[-1])
    kv, memb = _kv_proj(mem, full["w_kv"])
    (xbt, g, cin, u, q, ain, ob, aint, obt), side = _in_proj(
        x, full["w_in"], small["b_gate"], small["conv_w"], kv,
        [late[k] for k in ("w_glu", "w_conv_out", "w_xattn_out", "w_out", "w_up")])
    w_glu_t, w_co_t, w_xo_t, w_out, w_up_t = (stack(a) for a in side)
    y_ssm, cm_all, side = _ssm_fwd(u, b_half, c_half, pw, small["ssm_d"], [late["w_down"]])
    w_down = stack(side[0])
    ysbt, mb, xhat1, rstd1 = _mid_fwd(y_ssm, g, ain, ob, x, w_glu_t, w_co_t, w_xo_t, w_out,
                                      small["ln1_g"], small["ln1_b"])
    (x1bt, hdn, dr2bt, dpre, dx1, loss, dl2g, dl2b, dbdn, dbup) = _mlp_fwd_bwd(
        xhat1, tgt, small["ln1_g"], small["ln1_b"], w_up_t, small["b_up"], w_down,
        small["b_down"], small["ln2_g"], small["ln2_b"])
    recv = {}
    recv["w_down"] = _weight_grad_scatter(dr2bt, hdn, "dw_down", tm=512, tt=2048)
    recv["w_up"] = _weight_grad_scatter(x1bt, dpre, "dw_up", tm=512, tt=2048)
    (dxp, dr1bt, dgp, dya, dyc, dglu, dyssm, dl1g, dl1b, dbg) = _mid_bwd(
        dx1, xhat1, rstd1, g, ain, ob, y_ssm, small["ln1_g"], w_out, w_glu_t, w_co_t, w_xo_t)
    recv["w_out"] = _weight_grad_scatter(dr1bt, mb, "dw_out", tm=512, tt=s_len)
    recv["w_glu"] = _weight_grad_scatter(ysbt, dglu, "dw_glu", tm=512, tt=s_len)
    du, db_half, dc_half, da8, dd = _ssm_bwd(u, dyssm, cm_all, b_half, c_half, pw, small["ssm_d"])
    dconv, dq, dcw8, dkv = _branch_bwd(dya, dyc, cin, q, kv, small["conv_w"], w_co_t, w_xo_t)
    recv["w_conv_out"] = _weight_grad_scatter(aint, dya, "dw_conv_out", tm=512, tt=s_len)
    recv["w_xattn_out"] = _weight_grad_scatter(obt, dyc, "dw_xattn_out", tm=512, tt=s_len)
    recv["w_kv"] = _weight_grad_scatter(dkv.T.astype(bf16), memb, "dw_kv", tm=D_MODEL, tt=MEM_LEN)
    dabar_r, dabar_i = _state_unlayout(jnp.sum(da8, axis=0))
    dbbar_r = _half_diag_blocks(db_half[:, :, :HALF_STATE], SSM_GROUP, SSM_STATE).transpose(0, 2, 1)
    dbbar_i = _half_diag_blocks(db_half[:, :, HALF_STATE:], SSM_GROUP, SSM_STATE).transpose(0, 2, 1)
    g_shape = (SSM_GROUPS, SSM_STATE)
    dlam_re, dlam_im, dlog_dt, db_re, db_im = disc_vjp(
        (dabar_r.reshape(g_shape), dabar_i.reshape(g_shape), dbbar_r, dbbar_i))
    dc_re = _half_diag_blocks(dc_half[:, :HALF_STATE, :], SSM_STATE, SSM_GROUP).transpose(0, 2, 1)
    dc_im = -_half_diag_blocks(dc_half[:, HALF_STATE:, :], SSM_STATE, SSM_GROUP).transpose(0, 2, 1)

    small_grads = {
        "b_gate": dbg, "conv_w": dcw8[0:3], "ssm_lam_re": dlam_re, "ssm_lam_im": dlam_im, "ssm_log_dt": dlog_dt,
        "ssm_b_re": db_re, "ssm_b_im": db_im, "ssm_c_re": dc_re, "ssm_c_im": dc_im, "ssm_d": dd,
        "ln1_g": dl1g, "ln1_b": dl1b, "b_up": dbup, "b_down": dbdn, "ln2_g": dl2g, "ln2_b": dl2b,
    }
    small_grads = {k: a.reshape(SMALL[k]) for k, a in small_grads.items()}
    stacks = [jnp.concatenate([small_grads[k] for k in group], axis=axis) if len(group) > 1 else small_grads[group[0]]
              for group, axis in SMALL_GROUPS]
    dense = lambda a: a.reshape(-1, LANES) if a.size % LANES == 0 else a
    dx, dproj, group_parts = _in_proj_bwd(dgp, dconv, du, dq, dxp, full["w_in"], [dense(a) for a in stacks])
    recv["w_in"] = _weight_grad_scatter(xbt, dproj, "dw_in", tm=512, tt=2048)
    group_sums = [s.reshape(a.shape) for s, a in zip(_sum_small(group_parts), stacks)]
    return loss[0, 0], dx, recv, group_sums


def kernel(x, mem, w_in, b_gate, conv_w, w_conv_out, ssm_lam_re, ssm_lam_im, ssm_log_dt, ssm_b_re, ssm_b_im, ssm_c_re, ssm_c_im, ssm_d, w_glu, w_kv, w_xattn_out, w_out, ln1_g, ln1_b, w_up, b_up, w_down, b_down, ln2_g, ln2_b, loss_target, m_w_in, m_b_gate, m_conv_w, m_w_conv_out, m_ssm_lam_re, m_ssm_lam_im, m_ssm_log_dt, m_ssm_b_re, m_ssm_b_im, m_ssm_c_re, m_ssm_c_im, m_ssm_d, m_w_glu, m_w_kv, m_w_xattn_out, m_w_out, m_ln1_g, m_ln1_b, m_w_up, m_b_up, m_w_down, m_b_down, m_ln2_g, m_ln2_b, v_w_in, v_b_gate, v_conv_w, v_w_conv_out, v_ssm_lam_re, v_ssm_lam_im, v_ssm_log_dt, v_ssm_b_re, v_ssm_b_im, v_ssm_c_re, v_ssm_c_im, v_ssm_d, v_w_glu, v_w_kv, v_w_xattn_out, v_w_out, v_ln1_g, v_ln1_b, v_w_up, v_b_up, v_w_down, v_b_down, v_ln2_g, v_ln2_b):
    w = dict(w_in=w_in, b_gate=b_gate, conv_w=conv_w, w_conv_out=w_conv_out, ssm_lam_re=ssm_lam_re,
             ssm_lam_im=ssm_lam_im, ssm_log_dt=ssm_log_dt, ssm_b_re=ssm_b_re, ssm_b_im=ssm_b_im, ssm_c_re=ssm_c_re,
             ssm_c_im=ssm_c_im, ssm_d=ssm_d, w_glu=w_glu, w_kv=w_kv, w_xattn_out=w_xattn_out, w_out=w_out,
             ln1_g=ln1_g, ln1_b=ln1_b, w_up=w_up, b_up=b_up, w_down=w_down, b_down=b_down, ln2_g=ln2_g, ln2_b=ln2_b)
    m = dict(w_in=m_w_in, b_gate=m_b_gate, conv_w=m_conv_w, w_conv_out=m_w_conv_out, ssm_lam_re=m_ssm_lam_re,
             ssm_lam_im=m_ssm_lam_im, ssm_log_dt=m_ssm_log_dt, ssm_b_re=m_ssm_b_re, ssm_b_im=m_ssm_b_im,
             ssm_c_re=m_ssm_c_re, ssm_c_im=m_ssm_c_im, ssm_d=m_ssm_d, w_glu=m_w_glu, w_kv=m_w_kv,
             w_xattn_out=m_w_xattn_out, w_out=m_w_out, ln1_g=m_ln1_g, ln1_b=m_ln1_b, w_up=m_w_up, b_up=m_b_up,
             w_down=m_w_down, b_down=m_b_down, ln2_g=m_ln2_g, ln2_b=m_ln2_b)
    v = dict(w_in=v_w_in, b_gate=v_b_gate, conv_w=v_conv_w, w_conv_out=v_w_conv_out, ssm_lam_re=v_ssm_lam_re,
             ssm_lam_im=v_ssm_lam_im, ssm_log_dt=v_ssm_log_dt, ssm_b_re=v_ssm_b_re, ssm_b_im=v_ssm_b_im,
             ssm_c_re=v_ssm_c_re, ssm_c_im=v_ssm_c_im, ssm_d=v_ssm_d, w_glu=v_w_glu, w_kv=v_w_kv,
             w_xattn_out=v_w_xattn_out, w_out=v_w_out, ln1_g=v_ln1_g, ln1_b=v_ln1_b, w_up=v_w_up, b_up=v_b_up,
             w_down=v_w_down, b_down=v_b_down, ln2_g=v_ln2_g, ln2_b=v_ln2_b)
    out_shapes = {k: a.shape for k, a in w.items()}
    shard2d = lambda k, a: a.reshape((3, CONV_W // N_DEV) if k == "conv_w" else SMALL[k]) if k in SMALL else a[0]
    w, m, v = ({k: shard2d(k, a) for k, a in d.items()} for d in (w, m, v))

    shards = {k: w[k].T.astype(bf16) if k in COL_SHARDED else w[k].astype(bf16) for k in BIG}
    conv_pad = jnp.pad(w["conv_w"], ((0, 5), (0, LANES - CONV_W // N_DEV)))
    early = ["w_in", "w_kv"]
    gathered = _all_gather([shards[k] for k in early] + [conv_pad], "gather_weights")
    full = {k: a.reshape(-1, a.shape[-1]) for k, a in zip(early, gathered[:-1])}
    late = {k: shards[k] for k in BIG if k not in early}
    conv_full = gathered[-1][:, :3, :CONV_W // N_DEV].transpose(1, 0, 2).reshape(3, CONV_W)
    small = {k: (conv_full if k == "conv_w" else w[k]) for k in SMALL}

    loss, dx, recv, group_sums = _local_step(x[0], mem[0], loss_target[0], full, late, small)

    grads, deltas, new_m, new_v = {}, {}, {}, {}
    for k in BIG:
        res = _adamw_update(w[k], m[k], v[k], recv[k], "adamw_" + k, transposed=k not in COL_SHARDED)
        grads[k], deltas[k], new_m[k], new_v[k] = res

    widen = lambda k, a: jnp.tile(a, (1, N_DEV)) if k == "conv_w" else a
    res = _adamw_small(small, {k: widen(k, m[k]) for k in SMALL}, {k: widen(k, v[k]) for k in SMALL}, group_sums)
    dev = _slot(_mesh_place())
    for d, small_res in zip((grads, deltas, new_m, new_v), res):
        for k, a in small_res.items():
            if k == "conv_w":
                a = lax.dynamic_slice_in_dim(a, dev * (CONV_W // N_DEV), CONV_W // N_DEV, axis=1)
            d[k] = a

    loss = lax.psum(loss, ("x", "y", "c"))
    outs = [loss, dx[None]]
    for d in (grads, deltas, new_m, new_v):
        outs += [d[k].reshape(out_shapes[k]) for k in WEIGHTS]
    return tuple(outs)
```

```python
import functools
import math

import jax
import jax.numpy as jnp
from jax import lax
from jax.experimental import pallas as pl
from jax.experimental.pallas import tpu as pltpu

f32 = jnp.float32
bf16 = jnp.bfloat16

D_MODEL = 1024
MEM_LEN = 256
GATE_COLS = 3 * D_MODEL
CONV_W = 512
SSM_W = 512
XATTN_W = 512
HEADS = 4
HEAD_DIM = 128
D_FF = 4096
IN_COLS = GATE_COLS + 3 * CONV_W + SSM_W + XATTN_W
SSM_GROUPS = 32
SSM_GROUP = 16
SSM_STATE = 64
N_STATE = SSM_GROUPS * SSM_STATE
ALPHA = 2.0 ** 0.25
LN_EPS = 1e-5
N_DEV = 8

ADAM_LR = 0.001
ADAM_B1 = 0.9
ADAM_B2 = 0.999
ADAM_EPS = 1e-08
ADAM_WD = 0.01
ADAM_STEP = 10

VMEM_LIMIT_V7X = 56 * 2 ** 20
SUBLANES = 8
LANES = 128

TOKEN_TILE = 256
SSM_BLOCK = 256
SSM_SEG = SSM_BLOCK // SUBLANES
LANE_CHUNK = 512
N_HALF = 2
HALF_W = SSM_W // N_HALF
HALF_STATE = N_STATE // N_HALF
HALF_COLS = 2 * HALF_STATE

NT = (((1,), (1,)), ((), ()))
TN = (((0,), (0,)), ((), ()))
NN = (((1,), (0,)), ((), ()))


def _dot(a, b, dims=NN):
    return lax.dot_general(a, b, dims, preferred_element_type=f32)


def _cparams(sem=None):
    return pltpu.CompilerParams(dimension_semantics=sem, vmem_limit_bytes=VMEM_LIMIT_V7X)


def _row_spec(tm, cols, rev_n=None):
    if rev_n is None:
        return pl.BlockSpec((tm, cols), lambda i: (i, 0))
    return pl.BlockSpec((tm, cols), lambda i: (rev_n - 1 - i, 0))


def _col_spec(rows, tm):
    return pl.BlockSpec((rows, tm), lambda i: (0, i))


def _const_spec(shape):
    nd = len(shape)
    return pl.BlockSpec(shape, lambda *_: (0,) * nd, pipeline_mode=pl.Buffered(1))


def _acc_spec(shape):
    nd = len(shape)
    return pl.BlockSpec(shape, lambda *_: (0,) * nd)


def _sds(shape, dtype):
    return jax.ShapeDtypeStruct(shape, dtype)


def _gelu(x):
    c = math.sqrt(2.0 / math.pi)
    return 0.5 * x * (1.0 + jnp.tanh(c * (x + 0.044715 * x * x * x)))


def _gelu_grad(x):
    c = math.sqrt(2.0 / math.pi)
    t = jnp.tanh(c * (x + 0.044715 * x * x * x))
    return 0.5 * (1.0 + t) + 0.5 * x * (1.0 - t * t) * c * (1.0 + 3.0 * 0.044715 * x * x)


def _colsum(a):
    return jnp.sum(a, axis=0, keepdims=True)


def _mesh_place():
    return lax.axis_index("x"), lax.axis_index("y"), lax.axis_index("c")


def _slot(p):
    return 4 * p[0] + 2 * p[1] + p[2]


def _other_devices(me):
    x, y, c = me
    flip = lambda v, d: 1 - v if d else v
    return [(flip(x, dx), flip(y, dy), flip(c, dc)) for dx in (0, 1) for dy in (0, 1) for dc in (0, 1)][1:]


def _all_gather(blocks, name):
    n = len(blocks)

    def body(*refs):
        ins, outs = refs[:n], refs[n:2 * n]
        send_sems, recv_sems, local_sems = refs[2 * n:]
        x, y, c = _mesh_place()
        me, sibling = (x, y, c), (x, y, 1 - c)
        chips = [(1 - x, y), (x, 1 - y), (1 - x, 1 - y)]

        def copy(a, k, block, to, src=None):
            rows = outs[a].at[_slot(block)]
            return pltpu.make_async_remote_copy(
                src_ref=rows if src is None else src, dst_ref=rows,
                send_sem=send_sems.at[a, k], recv_sem=recv_sems.at[a, k],
                device_id=to, device_id_type=pl.DeviceIdType.MESH)

        mine = [pltpu.make_async_copy(ins[a], outs[a].at[_slot(me)], local_sems.at[a]) for a in range(n)]
        for cp in mine:
            cp.start()
        first = []
        for a in range(n):
            first.append(copy(a, 0, me, sibling, src=ins[a]))
            first += [copy(a, 1 + j, me, (*chip, c), src=ins[a]) for j, chip in enumerate(chips)]
        for cp in first:
            cp.start()
        passed = []
        for a in range(n):
            for j, chip in enumerate(chips):
                copy(a, 1 + j, (*chip, c), me).wait_recv()
                fwd = copy(a, 4 + j, (*chip, c), sibling)
                fwd.start()
                passed.append(fwd)
        for a in range(n):
            copy(a, 0, sibling, me).wait_recv()
            for j, chip in enumerate(chips):
                copy(a, 4 + j, (*chip, 1 - c), me).wait_recv()
        for cp in first + passed:
            cp.wait_send()
        for cp in mine:
            cp.wait()

    any_spec = pl.BlockSpec(memory_space=pl.ANY)
    return pl.pallas_call(
        body, name=name,
        out_shape=[_sds((N_DEV,) + b.shape, b.dtype) for b in blocks],
        in_specs=[any_spec] * n, out_specs=[any_spec] * n,
        scratch_shapes=[pltpu.SemaphoreType.DMA((n, 7)), pltpu.SemaphoreType.DMA((n, 7)),
                        pltpu.SemaphoreType.DMA((n,))],
    )(*blocks)


def _side_gather_copies(ins, outs, send_sems, recv_sems, local_sems):
    me = _mesh_place()
    copies = []
    for a, (src, dst) in enumerate(zip(ins, outs)):
        copies.append(pltpu.make_async_copy(src, dst.at[_slot(me)], local_sems.at[a]))
        for k, peer in enumerate(_other_devices(me)):
            copies.append(pltpu.make_async_remote_copy(
                src_ref=src, dst_ref=dst.at[_slot(me)], send_sem=send_sems.at[a, k], recv_sem=recv_sems.at[a, k],
                device_id=peer, device_id_type=pl.DeviceIdType.MESH))
    return copies


def _side_gather_specs(blocks):
    n = len(blocks)
    any_spec = pl.BlockSpec(memory_space=pl.ANY)
    return ([any_spec] * n, [_sds((N_DEV,) + b.shape, b.dtype) for b in blocks],
            [pltpu.SemaphoreType.DMA((n, N_DEV - 1)), pltpu.SemaphoreType.DMA((n, N_DEV - 1)),
             pltpu.SemaphoreType.DMA((n,))])


def _kv_proj(mem, w_kv):
    def body(mem_ref, w_ref, kv_ref, memb_ref):
        mb = mem_ref[...].astype(bf16)
        memb_ref[...] = mb
        kv_ref[...] = _dot(mb, w_ref[...]).astype(bf16)

    return pl.pallas_call(
        body, name="kv_proj",
        out_shape=[_sds((MEM_LEN, 2 * XATTN_W), bf16), _sds((MEM_LEN, D_MODEL), bf16)],
        compiler_params=_cparams(),
    )(mem, w_kv)


def _attention_probs(qb, kv_ref, h):
    kh = kv_ref[:, h * HEAD_DIM:(h + 1) * HEAD_DIM]
    s = _dot(qb[:, h * HEAD_DIM:(h + 1) * HEAD_DIM], kh, NT) * (HEAD_DIM ** -0.5)
    e = jnp.exp(s - jnp.max(s, axis=-1, keepdims=True))
    return e / jnp.sum(e, axis=-1, keepdims=True)


def _in_proj(x, w_in_t, b_gate, conv_w, kv, side_blocks):
    s_len = x.shape[0]
    tm = TOKEN_TILE
    n = s_len // tm
    ns = len(side_blocks)
    side_in_specs, side_shapes, side_sems = _side_gather_specs(side_blocks)

    def body(*refs):
        (x_ref, win_ref, bg_ref, cw_ref, kv_ref) = refs[:5]
        side_ins = refs[5:5 + ns]
        (xbt_ref, g_ref, cin_ref, u_ref, q_ref, ain_ref, o_ref, aint_ref, ot_ref) = refs[5 + ns:14 + ns]
        side_outs = refs[14 + ns:14 + 2 * ns]
        zs_ref = refs[14 + 2 * ns]
        side = _side_gather_copies(side_ins, side_outs, *refs[15 + 2 * ns:])
        i = pl.program_id(0)

        @pl.when(i == 0)
        def _():
            for cp in side:
                cp.start()

        xb = x_ref[...].astype(bf16)
        xbt_ref[...] = xb.T
        proj = _dot(xb, win_ref[...], NT)
        g_ref[...] = jax.nn.sigmoid(proj[:, :GATE_COLS] + bg_ref[...]).astype(bf16)
        cin = proj[:, GATE_COLS:GATE_COLS + 3 * CONV_W]
        cin_ref[...] = cin
        u_ref[...] = proj[:, GATE_COLS + 3 * CONV_W:GATE_COLS + 3 * CONV_W + SSM_W]
        qb = proj[:, IN_COLS - XATTN_W:].astype(bf16)
        q_ref[...] = qb

        cb, cc, ch = cin[:, :CONV_W], cin[:, CONV_W:2 * CONV_W], cin[:, 2 * CONV_W:]
        z = cc * ch

        @pl.when(i == 0)
        def _():
            zs_ref[0:8, :] = jnp.zeros((8, CONV_W), f32)

        zs_ref[8:8 + tm, :] = z
        z1 = zs_ref[pl.ds(7, tm), :]
        z2 = zs_ref[pl.ds(6, tm), :]
        cw = cw_ref[...]
        cz = cw[0:1] * z2 + cw[1:2] * z1 + cw[2:3] * z
        zs_ref[0:8, :] = zs_ref[tm:tm + 8, :]
        ain = (cb * cz).astype(bf16)
        ain_ref[...] = ain
        aint_ref[...] = ain.T

        outs = []
        for h in range(HEADS):
            p = _attention_probs(qb, kv_ref, h)
            vh = kv_ref[:, XATTN_W + h * HEAD_DIM:XATTN_W + (h + 1) * HEAD_DIM]
            outs.append(_dot(p.astype(bf16), vh))
        ob = jnp.concatenate(outs, axis=1).astype(bf16)
        o_ref[...] = ob
        ot_ref[...] = ob.T

        @pl.when(i == n - 1)
        def _():
            for cp in side:
                cp.wait()

    row_cols = [(GATE_COLS, bf16), (3 * CONV_W, f32), (SSM_W, f32), (XATTN_W, bf16), (CONV_W, bf16), (XATTN_W, bf16)]
    t_rows = [D_MODEL, CONV_W, XATTN_W]
    outs = pl.pallas_call(
        body, name="in_proj", grid=(n,),
        in_specs=[_row_spec(tm, D_MODEL), _const_spec((IN_COLS, D_MODEL)), _const_spec((1, GATE_COLS)),
                  _const_spec((3, CONV_W)), _const_spec((MEM_LEN, 2 * XATTN_W))] + side_in_specs,
        out_specs=([_col_spec(t_rows[0], tm)] + [_row_spec(tm, c) for c, _ in row_cols]
                   + [_col_spec(t_rows[1], tm), _col_spec(t_rows[2], tm)] + side_in_specs),
        out_shape=([_sds((t_rows[0], s_len), bf16)] + [_sds((s_len, c), dt) for c, dt in row_cols]
                   + [_sds((t_rows[1], s_len), bf16), _sds((t_rows[2], s_len), bf16)] + side_shapes),
        scratch_shapes=[pltpu.VMEM((tm + 8, CONV_W), f32)] + side_sems,
        compiler_params=_cparams(("arbitrary",)),
    )(x, w_in_t, b_gate, conv_w, kv, *side_blocks)
    return outs[:9], outs[9:]


def _state_cols(chunk):
    half, off = divmod(chunk * LANE_CHUNK, HALF_STATE)
    lo = half * HALF_COLS + off
    return slice(lo, lo + LANE_CHUNK), slice(lo + HALF_STATE, lo + HALF_STATE + LANE_CHUNK)


def _half_cols(half):
    lo = half * HALF_COLS
    return slice(lo, lo + HALF_STATE), slice(lo + HALF_STATE, lo + HALF_COLS)


def _rows_to_segments(src_ref, stage_ref, dst_ref):
    nc = SSM_W // LANES
    for c in range(nc):
        stage_ref[c] = src_ref[:, c * LANES:(c + 1) * LANES]
    for c in range(nc):
        for k in range(SSM_SEG):
            dst_ref[k * SUBLANES:(k + 1) * SUBLANES, c * LANES:(c + 1) * LANES] = (
                stage_ref[c, pl.ds(k, SUBLANES, stride=SSM_SEG), :])


def _rows_from_segments(src_ref, stage_ref, dst_ref):
    nc = SSM_W // LANES
    for c in range(nc):
        for k in range(SSM_SEG):
            stage_ref[c, pl.ds(k, SUBLANES, stride=SSM_SEG), :] = (
                src_ref[k * SUBLANES:(k + 1) * SUBLANES, c * LANES:(c + 1) * LANES])
    for c in range(nc):
        dst_ref[:, c * LANES:(c + 1) * LANES] = stage_ref[c]


def _ssm_scan(s_ref, pw_ref, init_ref, reverse, unroll):
    for chunk in range(N_STATE // LANE_CHUNK):
        re, im = _state_cols(chunk)
        ar = jnp.broadcast_to(pw_ref[0:1, re], (SUBLANES, LANE_CHUNK))
        ai = jnp.broadcast_to(pw_ref[0:1, im], (SUBLANES, LANE_CHUNK))
        if reverse:
            ai = -ai

        def step(j, carry, re=re, im=im, ar=ar, ai=ai):
            sr, si = carry
            k = (SSM_SEG - 1 - j) if reverse else j
            r0 = pl.multiple_of(k * SUBLANES, SUBLANES)
            nr = ar * sr - ai * si + s_ref[pl.ds(r0, SUBLANES), re]
            ni = ar * si + ai * sr + s_ref[pl.ds(r0, SUBLANES), im]
            s_ref[pl.ds(r0, SUBLANES), re] = nr
            s_ref[pl.ds(r0, SUBLANES), im] = ni
            return nr, ni

        if init_ref is None:
            init = (jnp.zeros((SUBLANES, LANE_CHUNK), f32),) * 2
        else:
            init = (init_ref[:, re], init_ref[:, im])
        lax.fori_loop(0, SSM_SEG, step, init, unroll=unroll)


def _ssm_add_carry(s_ref, pw_ref, cm_ref, reverse):
    for chunk in range(N_STATE // LANE_CHUNK):
        re, im = _state_cols(chunk)
        cr, ci = cm_ref[:, re], cm_ref[:, im]
        for k in range(SSM_SEG):
            pk = (SSM_SEG - 1 - k) if reverse else k
            pr = pw_ref[pk:pk + 1, re]
            pi = pw_ref[pk:pk + 1, im]
            if reverse:
                pi = -pi
            rows = slice(k * SUBLANES, (k + 1) * SUBLANES)
            s_ref[rows, re] = s_ref[rows, re] + (pr * cr - pi * ci)
            s_ref[rows, im] = s_ref[rows, im] + (pr * ci + pi * cr)


def _ssm_carries(first_row, s_ref, pw_ref, carry_ref, cm_ref, reverse):
    order = range(SUBLANES - 1, -1, -1) if reverse else range(SUBLANES)
    for half in range(N_HALF):
        re, im = _half_cols(half)
        a_r, a_i = pw_ref[SSM_SEG - 1:SSM_SEG, re], pw_ref[SSM_SEG - 1:SSM_SEG, im]
        if reverse:
            a_i = -a_i
        cr, ci = carry_ref[0:1, re], carry_ref[0:1, im]
        for seg in order:
            cm_ref[seg:seg + 1, re] = cr
            cm_ref[seg:seg + 1, im] = ci
            er = s_ref[first_row + seg:first_row + seg + 1, re]
            ei = s_ref[first_row + seg:first_row + seg + 1, im]
            cr, ci = a_r * cr - a_i * ci + er, a_r * ci + a_i * cr + ei
        carry_ref[0:1, re] = cr
        carry_ref[0:1, im] = ci


def _ssm_fwd(u, b_half, c_half, pw, d_skip, side_blocks):
    s_len = u.shape[0]
    tb = SSM_BLOCK
    n = s_len // tb
    ns = len(side_blocks)
    side_in_specs, side_shapes, side_sems = _side_gather_specs(side_blocks)

    def body(*refs):
        u_ref, b_ref, c_ref, pw_ref, d_ref = refs[:5]
        side_ins = refs[5:5 + ns]
        y_ref, cm_ref = refs[5 + ns:7 + ns]
        side_outs = refs[7 + ns:7 + 2 * ns]
        s_ref, carry_ref, up_ref, yp_ref, stage_ref = refs[7 + 2 * ns:12 + 2 * ns]
        side = _side_gather_copies(side_ins, side_outs, *refs[12 + 2 * ns:])
        i = pl.program_id(0)

        @pl.when(i == 0)
        def _():
            carry_ref[...] = jnp.zeros_like(carry_ref)
            for cp in side:
                cp.start()

        _rows_to_segments(u_ref, stage_ref, up_ref)
        u = up_ref[...]
        ub = u.astype(bf16)
        for half in range(N_HALF):
            s_ref[:, half * HALF_COLS:(half + 1) * HALF_COLS] = _dot(ub[:, half * HALF_W:(half + 1) * HALF_W], b_ref[half])
        _ssm_scan(s_ref, pw_ref, None, reverse=False, unroll=4)
        _ssm_carries(tb - SUBLANES, s_ref, pw_ref, carry_ref, cm_ref, reverse=False)
        _ssm_add_carry(s_ref, pw_ref, cm_ref, reverse=False)
        for half in range(N_HALF):
            cols = slice(half * HALF_W, (half + 1) * HALF_W)
            sb = s_ref[:, half * HALF_COLS:(half + 1) * HALF_COLS].astype(bf16)
            yp_ref[:, cols] = _dot(sb, c_ref[half]) + d_ref[:, cols] * u[:, cols]
        _rows_from_segments(yp_ref, stage_ref, y_ref)

        @pl.when(i == n - 1)
        def _():
            for cp in side:
                cp.wait()

    outs = pl.pallas_call(
        body, name="ssm_fwd", grid=(n,),
        in_specs=[_row_spec(tb, SSM_W), _const_spec((N_HALF, HALF_W, HALF_COLS)), _const_spec((N_HALF, HALF_COLS, HALF_W)),
                  _const_spec((SSM_SEG, 2 * N_STATE)), _const_spec((1, SSM_W))] + side_in_specs,
        out_specs=[_row_spec(tb, SSM_W), _row_spec(SUBLANES, 2 * N_STATE)] + side_in_specs,
        out_shape=[_sds((s_len, SSM_W), f32), _sds((n * SUBLANES, 2 * N_STATE), f32)] + side_shapes,
        scratch_shapes=[pltpu.VMEM((tb, 2 * N_STATE), f32), pltpu.VMEM((SUBLANES, 2 * N_STATE), f32),
                        pltpu.VMEM((tb, SSM_W), f32), pltpu.VMEM((tb, SSM_W), f32),
                        pltpu.VMEM((SSM_W // LANES, tb, LANES), f32)] + side_sems,
        compiler_params=_cparams(("arbitrary",)),
    )(u, b_half, c_half, pw, d_skip, *side_blocks)
    return outs[0], outs[1], outs[2:]


def _layer_norm_fwd(r, g, b):
    mu = jnp.mean(r, axis=-1, keepdims=True)
    var = jnp.mean(jnp.square(r - mu), axis=-1, keepdims=True)
    rstd = lax.rsqrt(var + LN_EPS)
    xhat = (r - mu) * rstd
    return xhat, rstd, xhat * g + b


def _layer_norm_bwd(dy, xhat, rstd, g):
    dxh = dy * g
    m1 = jnp.mean(dxh, axis=-1, keepdims=True)
    m2 = jnp.mean(dxh * xhat, axis=-1, keepdims=True)
    return rstd * (dxh - m1 - xhat * m2)


def _branch_outputs(ys_ref, ain_ref, o_ref, wglu_ref, wco_ref, wxo_ref):
    ysb = _gelu(ys_ref[...]).astype(bf16)
    glu = _dot(ysb, wglu_ref[...], NT)
    ga, sb = glu[:, :D_MODEL], jax.nn.sigmoid(glu[:, D_MODEL:])
    ya = _dot(ain_ref[...], wco_ref[...], NT)
    yc = _dot(o_ref[...], wxo_ref[...], NT)
    return ysb, ga, sb, ya, ga * sb, yc


def _mid_fwd(y_ssm, g, ain, ob, x, w_glu_t, w_co_t, w_xo_t, w_out, ln1_g, ln1_b):
    s_len = x.shape[0]
    tm = TOKEN_TILE
    n = s_len // tm

    def body(ys_ref, g_ref, ain_ref, o_ref, x_ref, wglu_ref, wco_ref, wxo_ref, wout_ref, lg_ref, lb_ref,
             ysbt_ref, mb_ref, xhat_ref, rstd_ref):
        ysb, _, _, ya, yb, yc = _branch_outputs(ys_ref, ain_ref, o_ref, wglu_ref, wco_ref, wxo_ref)
        ysbt_ref[...] = ysb.T
        gt = g_ref[...].astype(f32)
        merged = gt[:, :D_MODEL] * ya + gt[:, D_MODEL:2 * D_MODEL] * yb + gt[:, 2 * D_MODEL:] * yc
        mb = merged.astype(bf16)
        mb_ref[...] = mb
        r1 = ALPHA * x_ref[...] + _dot(mb, wout_ref[...])
        xhat, rstd, _ = _layer_norm_fwd(r1, lg_ref[...], lb_ref[...])
        xhat_ref[...] = xhat
        rstd_ref[...] = rstd

    row_cols = [(D_MODEL, bf16), (D_MODEL, f32), (1, f32)]
    return pl.pallas_call(
        body, name="mid_fwd", grid=(n,),
        in_specs=[_row_spec(tm, SSM_W), _row_spec(tm, GATE_COLS), _row_spec(tm, CONV_W), _row_spec(tm, XATTN_W),
                  _row_spec(tm, D_MODEL), _const_spec((2 * D_MODEL, SSM_W)), _const_spec((D_MODEL, CONV_W)),
                  _const_spec((D_MODEL, XATTN_W)), _const_spec((D_MODEL, D_MODEL)),
                  _const_spec((1, D_MODEL)), _const_spec((1, D_MODEL))],
        out_specs=[_col_spec(SSM_W, tm)] + [_row_spec(tm, c) for c, _ in row_cols],
        out_shape=[_sds((SSM_W, s_len), bf16)] + [_sds((s_len, c), dt) for c, dt in row_cols],
        compiler_params=_cparams(("parallel",)),
    )(y_ssm, g, ain, ob, x, w_glu_t, w_co_t, w_xo_t, w_out, ln1_g, ln1_b)


def _mlp_fwd_bwd(xhat1, tgt, ln1_g, ln1_b, w_up_t, b_up, w_down, b_down, ln2_g, ln2_b):
    s_len = xhat1.shape[0]
    tm = TOKEN_TILE
    n = s_len // tm
    fc = 1024
    nfc = D_FF // fc

    def body(xh_ref, t_ref, l1g_ref, l1b_ref, wup_ref, bup_ref, wdn_ref, bdn_ref, l2g_ref, l2b_ref,
             x1bt_ref, hdn_ref, dr2bt_ref, dpre_ref, dx1_ref,
             loss_ref, dl2g_ref, dl2b_ref, dbdn_ref, dbup_ref, rl_ref):
        i = pl.program_id(0)

        @pl.when(i == 0)
        def _():
            loss_ref[...] = jnp.zeros_like(loss_ref)
            dl2g_ref[...] = jnp.zeros_like(dl2g_ref)
            dl2b_ref[...] = jnp.zeros_like(dl2b_ref)
            dbdn_ref[...] = jnp.zeros_like(dbdn_ref)
            dbup_ref[...] = jnp.zeros_like(dbup_ref)

        x1 = xh_ref[...] * l1g_ref[...] + l1b_ref[...]
        x1b = x1.astype(bf16)
        x1bt_ref[...] = x1b.T
        acc = jnp.zeros((tm, D_MODEL), f32)
        for c in range(nfc):
            cols = slice(c * fc, (c + 1) * fc)
            pre = _dot(x1b, wup_ref[cols, :], NT) + bup_ref[:, cols]
            rl = jnp.maximum(pre, 0.0)
            rl_ref[:, cols] = rl
            hb = (rl * rl).astype(bf16)
            hdn_ref[:, cols] = hb
            acc = acc + _dot(hb, wdn_ref[cols, :])
        r2 = ALPHA * x1 + acc + bdn_ref[...]
        xhat2, rstd2, y = _layer_norm_fwd(r2, l2g_ref[...], l2b_ref[...])
        err = y - t_ref[...]
        loss_ref[...] += jnp.sum(jnp.sum(err * err, axis=1, keepdims=True), axis=0, keepdims=True) * (0.5 / D_MODEL)
        dy = err * (1.0 / D_MODEL)
        dl2g_ref[...] += _colsum(dy * xhat2)
        dl2b_ref[...] += _colsum(dy)
        dr2 = _layer_norm_bwd(dy, xhat2, rstd2, l2g_ref[...])
        dbdn_ref[...] += _colsum(dr2)
        dr2b = dr2.astype(bf16)
        dr2bt_ref[...] = dr2b.T
        dacc = jnp.zeros((tm, D_MODEL), f32)
        for c in range(nfc):
            cols = slice(c * fc, (c + 1) * fc)
            dh = _dot(dr2b, wdn_ref[cols, :], NT)
            dpre = dh * (2.0 * rl_ref[:, cols])
            dbup_ref[:, cols] += _colsum(dpre)
            dpb = dpre.astype(bf16)
            dpre_ref[:, cols] = dpb
            dacc = dacc + _dot(dpb, wup_ref[cols, :])
        dx1_ref[...] = ALPHA * dr2 + dacc

    acc_shapes = [(1, LANES), (1, D_MODEL), (1, D_MODEL), (1, D_MODEL), (1, D_FF)]
    return pl.pallas_call(
        body, name="mlp_fwd_bwd", grid=(n,),
        in_specs=[_row_spec(tm, D_MODEL), _row_spec(tm, D_MODEL), _const_spec((1, D_MODEL)), _const_spec((1, D_MODEL)),
                  _const_spec((D_FF, D_MODEL)), _const_spec((1, D_FF)), _const_spec((D_FF, D_MODEL)),
                  _const_spec((1, D_MODEL)), _const_spec((1, D_MODEL)), _const_spec((1, D_MODEL))],
        out_specs=([_col_spec(D_MODEL, tm), _row_spec(tm, D_FF), _col_spec(D_MODEL, tm), _row_spec(tm, D_FF),
                    _row_spec(tm, D_MODEL)] + [_acc_spec(s) for s in acc_shapes]),
        out_shape=([_sds((D_MODEL, s_len), bf16), _sds((s_len, D_FF), bf16), _sds((D_MODEL, s_len), bf16),
                    _sds((s_len, D_FF), bf16), _sds((s_len, D_MODEL), f32)] + [_sds(s, f32) for s in acc_shapes]),
        scratch_shapes=[pltpu.VMEM((tm, D_FF), f32)],
        compiler_params=_cparams(("arbitrary",)),
    )(xhat1, tgt, ln1_g, ln1_b, w_up_t, b_up, w_down, b_down, ln2_g, ln2_b)


def _mid_bwd(dx1, xhat1, rstd1, g, ain, ob, y_ssm, ln1_g, w_out, w_glu_t, w_co_t, w_xo_t):
    s_len = dx1.shape[0]
    tm = TOKEN_TILE
    n = s_len // tm

    def body(dx1_ref, xh_ref, rs_ref, g_ref, ain_ref, o_ref, ys_ref, lg_ref, wout_ref, wglu_ref, wco_ref, wxo_ref,
             dxp_ref, dr1bt_ref, dgp_ref, dya_ref, dyc_ref, dglu_ref, dyssm_ref,
             dl1g_ref, dl1b_ref, dbg_ref):
        i = pl.program_id(0)

        @pl.when(i == 0)
        def _():
            dl1g_ref[...] = jnp.zeros_like(dl1g_ref)
            dl1b_ref[...] = jnp.zeros_like(dl1b_ref)
            dbg_ref[...] = jnp.zeros_like(dbg_ref)

        dx1 = dx1_ref[...]
        xhat = xh_ref[...]
        dl1g_ref[...] += _colsum(dx1 * xhat)
        dl1b_ref[...] += _colsum(dx1)
        dr1 = _layer_norm_bwd(dx1, xhat, rs_ref[...], lg_ref[...])
        dxp_ref[...] = ALPHA * dr1
        dr1b = dr1.astype(bf16)
        dr1bt_ref[...] = dr1b.T
        dm = _dot(dr1b, wout_ref[...], NT)

        _, ga, sb, ya, yb, yc = _branch_outputs(ys_ref, ain_ref, o_ref, wglu_ref, wco_ref, wxo_ref)
        gt = g_ref[...].astype(f32)
        branch = (ya, yb, yc)
        for j in range(3):
            cols = slice(j * D_MODEL, (j + 1) * D_MODEL)
            gj = gt[:, cols]
            dgp = dm * branch[j] * gj * (1.0 - gj)
            dbg_ref[:, cols] += _colsum(dgp)
            dgp_ref[:, cols] = dgp.astype(bf16)
        dya_ref[...] = (dm * gt[:, :D_MODEL]).astype(bf16)
        dyc_ref[...] = (dm * gt[:, 2 * D_MODEL:]).astype(bf16)
        dyb = dm * gt[:, D_MODEL:2 * D_MODEL]
        dga = (dyb * sb).astype(bf16)
        dgb = (dyb * ga * sb * (1.0 - sb)).astype(bf16)
        dglu_ref[:, :D_MODEL] = dga
        dglu_ref[:, D_MODEL:] = dgb
        dys = _dot(dga, wglu_ref[:D_MODEL, :]) + _dot(dgb, wglu_ref[D_MODEL:, :])
        dyssm_ref[...] = dys * _gelu_grad(ys_ref[...])

    row_cols = [(GATE_COLS, bf16), (D_MODEL, bf16), (D_MODEL, bf16), (2 * D_MODEL, bf16), (SSM_W, f32)]
    acc_shapes = [(1, D_MODEL), (1, D_MODEL), (1, GATE_COLS)]
    return pl.pallas_call(
        body, name="mid_bwd", grid=(n,),
        in_specs=[_row_spec(tm, D_MODEL), _row_spec(tm, D_MODEL), _row_spec(tm, 1), _row_spec(tm, GATE_COLS),
                  _row_spec(tm, CONV_W), _row_spec(tm, XATTN_W), _row_spec(tm, SSM_W),
                  _const_spec((1, D_MODEL)), _const_spec((D_MODEL, D_MODEL)), _const_spec((2 * D_MODEL, SSM_W)),
                  _const_spec((D_MODEL, CONV_W)), _const_spec((D_MODEL, XATTN_W))],
        out_specs=([_row_spec(tm, D_MODEL), _col_spec(D_MODEL, tm)] + [_row_spec(tm, c) for c, _ in row_cols]
                   + [_acc_spec(s) for s in acc_shapes]),
        out_shape=([_sds((s_len, D_MODEL), f32), _sds((D_MODEL, s_len), bf16)]
                   + [_sds((s_len, c), dt) for c, dt in row_cols] + [_sds(s, f32) for s in acc_shapes]),
        compiler_params=_cparams(("arbitrary",)),
    )(dx1, xhat1, rstd1, g, ain, ob, y_ssm, ln1_g, w_out, w_glu_t, w_co_t, w_xo_t)


def _ssm_bwd(u, dy, cm_all, b_half, c_half, pw, d_skip):
    s_len = u.shape[0]
    tb = SSM_BLOCK
    n = s_len // tb

    def body(u_ref, dy_ref, cm_ref, b_ref, c_ref, pw_ref, d_ref,
             du_ref, db_hbm, dc_hbm, da_ref, dd_ref,
             s_ref, g_ref, gcarry_ref, gcm_ref, db_ref, dc_ref, up_ref, dyp_ref, dup_ref, stage_ref):
        i = pl.program_id(0)

        @pl.when(i == 0)
        def _():
            gcarry_ref[...] = jnp.zeros_like(gcarry_ref)
            db_ref[...] = jnp.zeros_like(db_ref)
            dc_ref[...] = jnp.zeros_like(dc_ref)
            da_ref[...] = jnp.zeros_like(da_ref)
            dd_ref[...] = jnp.zeros_like(dd_ref)

        _rows_to_segments(u_ref, stage_ref, up_ref)
        _rows_to_segments(dy_ref, stage_ref, dyp_ref)
        u = up_ref[...]
        ub = u.astype(bf16)
        dy = dyp_ref[...]
        dyb = dy.astype(bf16)
        dd_ref[...] += _colsum(dy * u)

        for half in range(N_HALF):
            s_ref[:, half * HALF_COLS:(half + 1) * HALF_COLS] = _dot(ub[:, half * HALF_W:(half + 1) * HALF_W], b_ref[half])
        _ssm_scan(s_ref, pw_ref, cm_ref, reverse=False, unroll=True)

        for half in range(N_HALF):
            g_ref[:, half * HALF_COLS:(half + 1) * HALF_COLS] = _dot(dyb[:, half * HALF_W:(half + 1) * HALF_W], c_ref[half], NT)
        _ssm_scan(g_ref, pw_ref, None, reverse=True, unroll=True)
        _ssm_carries(0, g_ref, pw_ref, gcarry_ref, gcm_ref, reverse=True)
        _ssm_add_carry(g_ref, pw_ref, gcm_ref, reverse=True)

        for half in range(N_HALF):
            cols = slice(half * HALF_W, (half + 1) * HALF_W)
            scols = slice(half * HALF_COLS, (half + 1) * HALF_COLS)
            gb = g_ref[:, scols].astype(bf16)
            dup_ref[:, cols] = _dot(gb, b_ref[half], NT) + d_ref[:, cols] * dy[:, cols]
            db_ref[half] += _dot(ub[:, cols], gb, TN)
            dc_ref[half] += _dot(s_ref[:, scols].astype(bf16), dyb[:, cols], TN)
        _rows_from_segments(dup_ref, stage_ref, du_ref)

        for chunk in range(N_STATE // LANE_CHUNK):
            re, im = _state_cols(chunk)
            acc_r = da_ref[:, re]
            acc_i = da_ref[:, im]
            for k in range(SSM_SEG):
                rows = slice(k * SUBLANES, (k + 1) * SUBLANES)
                if k == 0:
                    pr, pi = cm_ref[:, re], cm_ref[:, im]
                else:
                    prev = slice((k - 1) * SUBLANES, k * SUBLANES)
                    pr, pi = s_ref[prev, re], s_ref[prev, im]
                gr, gi = g_ref[rows, re], g_ref[rows, im]
                acc_r = acc_r + (gr * pr + gi * pi)
                acc_i = acc_i + (gi * pr - gr * pi)
            da_ref[:, re] = acc_r
            da_ref[:, im] = acc_i

        @pl.when(i == n - 1)
        def _():
            pltpu.sync_copy(db_ref, db_hbm)
            pltpu.sync_copy(dc_ref, dc_hbm)

    rev = functools.partial(_row_spec, rev_n=n)
    any_spec = pl.BlockSpec(memory_space=pl.ANY)
    state_rows = pltpu.VMEM((tb, 2 * N_STATE), f32)
    seg_rows = pltpu.VMEM((SUBLANES, 2 * N_STATE), f32)
    tok_rows = pltpu.VMEM((tb, SSM_W), f32)
    return pl.pallas_call(
        body, name="ssm_bwd", grid=(n,),
        in_specs=[rev(tb, SSM_W), rev(tb, SSM_W), rev(SUBLANES, 2 * N_STATE),
                  _const_spec((N_HALF, HALF_W, HALF_COLS)), _const_spec((N_HALF, HALF_COLS, HALF_W)),
                  _const_spec((SSM_SEG, 2 * N_STATE)), _const_spec((1, SSM_W))],
        out_specs=[rev(tb, SSM_W), any_spec, any_spec, _acc_spec((SUBLANES, 2 * N_STATE)), _acc_spec((1, SSM_W))],
        out_shape=[_sds((s_len, SSM_W), f32), _sds((N_HALF, HALF_W, HALF_COLS), f32),
                   _sds((N_HALF, HALF_COLS, HALF_W), f32), _sds((SUBLANES, 2 * N_STATE), f32), _sds((1, SSM_W), f32)],
        scratch_shapes=[state_rows, state_rows, seg_rows, seg_rows,
                        pltpu.VMEM((N_HALF, HALF_W, HALF_COLS), f32), pltpu.VMEM((N_HALF, HALF_COLS, HALF_W), f32),
                        tok_rows, tok_rows, tok_rows, pltpu.VMEM((SSM_W // LANES, tb, LANES), f32)],
        compiler_params=_cparams(("arbitrary",)),
    )(u, dy, cm_all, b_half, c_half, pw, d_skip)


def _branch_bwd(dya, dyc, cin, q, kv, conv_w, w_co_t, w_xo_t, side_blocks):
    s_len = dya.shape[0]
    tm = TOKEN_TILE
    n = s_len // tm
    halo_blocks = tm // 8
    ns = len(side_blocks)
    conv_tile = _sds((8, CONV_W), f32)
    side_in_specs, side_shapes, side_sems = _side_gather_specs(list(side_blocks) + [conv_tile])

    def body(*refs):
        (dya_ref, dyc_ref, cin_ref, cprev_ref, q_ref, kv_ref, cw_ref, wco_ref, wxo_ref) = refs[:9]
        side_ins = refs[9:9 + ns]
        dconv_ref, dq_ref, dkv_ref = refs[9 + ns:12 + ns]
        side_outs = refs[12 + ns:13 + 2 * ns]
        zs_ref, dczs_ref, dcw_ref = refs[13 + 2 * ns:16 + 2 * ns]
        copies = _side_gather_copies(list(side_ins) + [dcw_ref], side_outs, *refs[16 + 2 * ns:])
        side, conv_side = copies[:ns * N_DEV], copies[ns * N_DEV:]
        i = pl.program_id(0)
        tile = n - 1 - i

        @pl.when(i == 0)
        def _():
            dcw_ref[...] = jnp.zeros_like(dcw_ref)
            dkv_ref[...] = jnp.zeros_like(dkv_ref)
            dczs_ref[tm:tm + 8, :] = jnp.zeros((8, CONV_W), f32)
            for cp in side:
                cp.start()

        cin = cin_ref[...]
        cb, cc, ch = cin[:, :CONV_W], cin[:, CONV_W:2 * CONV_W], cin[:, 2 * CONV_W:]
        z = cc * ch
        cprev = cprev_ref[...]
        zprev = cprev[:, CONV_W:2 * CONV_W] * cprev[:, 2 * CONV_W:]
        zs_ref[0:8, :] = jnp.where(tile == 0, 0.0, zprev)
        zs_ref[8:8 + tm, :] = z
        z1 = zs_ref[pl.ds(7, tm), :]
        z2 = zs_ref[pl.ds(6, tm), :]
        cw = cw_ref[...]
        cz = cw[0:1] * z2 + cw[1:2] * z1 + cw[2:3] * z

        dain = _dot(dya_ref[...], wco_ref[...])
        dcb = dain * cz
        dcz = dain * cb
        dczs_ref[0:tm, :] = dcz
        dcz1 = dczs_ref[pl.ds(1, tm), :]
        dcz2 = dczs_ref[pl.ds(2, tm), :]
        dz = cw[2:3] * dcz + cw[1:2] * dcz1 + cw[0:1] * dcz2
        dczs_ref[tm:tm + 8, :] = dczs_ref[0:8, :]
        dcw_ref[0:1, :] += _colsum(dcz * z2)
        dcw_ref[1:2, :] += _colsum(dcz * z1)
        dcw_ref[2:3, :] += _colsum(dcz * z)
        dconv_ref[:, :CONV_W] = dcb.astype(bf16)
        dconv_ref[:, CONV_W:2 * CONV_W] = (dz * ch).astype(bf16)
        dconv_ref[:, 2 * CONV_W:] = (dz * cc).astype(bf16)

        qb = q_ref[...]
        do = _dot(dyc_ref[...], wxo_ref[...])
        for h in range(HEADS):
            hc = slice(h * HEAD_DIM, (h + 1) * HEAD_DIM)
            vc = slice(XATTN_W + h * HEAD_DIM, XATTN_W + (h + 1) * HEAD_DIM)
            p = _attention_probs(qb, kv_ref, h)
            dob = do[:, hc].astype(bf16)
            dp = _dot(dob, kv_ref[:, vc], NT)
            dkv_ref[:, vc] += _dot(p.astype(bf16), dob, TN)
            ds = p * (dp - jnp.sum(dp * p, axis=-1, keepdims=True)) * (HEAD_DIM ** -0.5)
            dsb = ds.astype(bf16)
            dq_ref[:, hc] = _dot(dsb, kv_ref[:, hc]).astype(bf16)
            dkv_ref[:, hc] += _dot(dsb, qb[:, hc], TN)

        @pl.when(i == n - 1)
        def _():
            for cp in conv_side:
                cp.start()
            for cp in side + conv_side:
                cp.wait()

    rev = functools.partial(_row_spec, rev_n=n)
    prev_spec = pl.BlockSpec((8, 3 * CONV_W), lambda i: (jnp.maximum((n - 1 - i) * halo_blocks - 1, 0), 0))
    outs = pl.pallas_call(
        body, name="branch_bwd", grid=(n,),
        in_specs=[rev(tm, D_MODEL), rev(tm, D_MODEL), rev(tm, 3 * CONV_W), prev_spec, rev(tm, XATTN_W),
                  _const_spec((MEM_LEN, 2 * XATTN_W)), _const_spec((3, CONV_W)), _const_spec((D_MODEL, CONV_W)),
                  _const_spec((D_MODEL, XATTN_W))] + side_in_specs[:ns],
        out_specs=[rev(tm, 3 * CONV_W), rev(tm, XATTN_W), _acc_spec((MEM_LEN, 2 * XATTN_W))] + side_in_specs,
        out_shape=[_sds((s_len, 3 * CONV_W), bf16), _sds((s_len, XATTN_W), bf16),
                   _sds((MEM_LEN, 2 * XATTN_W), f32)] + side_shapes,
        scratch_shapes=[pltpu.VMEM((tm + 8, CONV_W), f32), pltpu.VMEM((tm + 8, CONV_W), f32),
                        pltpu.VMEM((8, CONV_W), f32)] + side_sems,
        compiler_params=_cparams(("arbitrary",)),
    )(dya, dyc, cin, cin, q, kv, conv_w, w_co_t, w_xo_t, *side_blocks)
    return outs[0], outs[1], outs[2], outs[3:]


def _in_proj_bwd(dgp, dconv, du, dq, dxp, w_in_t):
    s_len = dgp.shape[0]
    tm = TOKEN_TILE
    n = s_len // tm

    def body(dgp_ref, dconv_ref, du_ref, dq_ref, dxp_ref, win_ref, dx_ref, dproj_ref):
        dproj = jnp.concatenate([dgp_ref[...], dconv_ref[...], du_ref[...].astype(bf16), dq_ref[...]], axis=1)
        dproj_ref[...] = dproj
        dx_ref[...] = dxp_ref[...] + _dot(dproj, win_ref[...])

    return pl.pallas_call(
        body, name="in_proj_bwd", grid=(n,),
        in_specs=[_row_spec(tm, GATE_COLS), _row_spec(tm, 3 * CONV_W), _row_spec(tm, SSM_W), _row_spec(tm, XATTN_W),
                  _row_spec(tm, D_MODEL), _const_spec((IN_COLS, D_MODEL))],
        out_specs=[_row_spec(tm, D_MODEL), _row_spec(tm, IN_COLS)],
        out_shape=[_sds((s_len, D_MODEL), f32), _sds((s_len, IN_COLS), bf16)],
        compiler_params=_cparams(("parallel",)),
    )(dgp, dconv, du, dq, dxp, w_in_t)


N_CHIP = 4
CHIP_STEPS = [(1, 1), (1, 0), (0, 1), (0, 0)]


def _flip(v, d):
    return 1 - v if d else v


def _chip_order():
    x, y, _ = _mesh_place()
    return jnp.stack([2 * _flip(x, dx) + _flip(y, dy) for dx, dy in CHIP_STEPS]).astype(jnp.int32)


def _weight_grad_scatter(a_t, b, name, tm, tt):
    m, s_len = a_t.shape
    n_cols = b.shape[1]
    w = n_cols // N_DEV
    tn = 2 * w
    tm, tt = min(tm, m), min(tt, s_len)
    nm, nt = m // tm, s_len // tt
    assert m % tm == 0 and s_len % tt == 0

    def body(order_ref, a_ref, b_ref, recv_ref, acc_ref, send_ref, sib_ref, stash_ref,
             d2d_send, d2d_recv, ici_send, ici_recv, local_sem):
        del order_ref
        q, im, t = pl.program_id(0), pl.program_id(1), pl.program_id(2)
        x, y, c = _mesh_place()
        mesh_id = pl.DeviceIdType.MESH

        @pl.when(t == 0)
        def _():
            acc_ref[...] = jnp.zeros_like(acc_ref)

        acc_ref[...] += _dot(a_ref[...], b_ref[...])

        def to_sibling(qq, imm):
            rows = pl.ds(pl.multiple_of(imm * tm, tm), tm)
            return pltpu.make_async_remote_copy(
                src_ref=send_ref.at[qq, 0, rows, :], dst_ref=sib_ref.at[qq, rows, :],
                send_sem=d2d_send.at[qq], recv_sem=d2d_recv.at[qq, imm],
                device_id=(x, y, 1 - c), device_id_type=mesh_id)

        def finish_tile(qq, imm):
            rows = pl.ds(pl.multiple_of(imm * tm, tm), tm)
            to_sibling(qq, imm).wait_recv()
            both = stash_ref[...] + sib_ref[qq, rows, :].astype(f32)
            send_ref[qq, 1, rows, :] = both.astype(bf16)
            for step, (dx, dy) in enumerate(CHIP_STEPS):
                @pl.when(qq == step)
                def _(step=step, dx=dx, dy=dy):
                    src, dst = send_ref.at[step, 1, rows, :], recv_ref.at[step, rows, :]
                    if dx or dy:
                        pltpu.make_async_remote_copy(
                            src_ref=src, dst_ref=dst, send_sem=ici_send.at[step], recv_sem=ici_recv.at[step],
                            device_id=(_flip(x, dx), _flip(y, dy), c), device_id_type=mesh_id).start()
                    else:
                        pltpu.make_async_copy(src, dst, local_sem).start()

        @pl.when(t == nt - 1)
        def _():
            tile = q * nm + im

            @pl.when(tile > 0)
            def _():
                finish_tile((tile - 1) // nm, (tile - 1) % nm)

            rows = pl.ds(pl.multiple_of(im * tm, tm), tm)
            for core in (0, 1):
                @pl.when(c == core)
                def _(core=core):
                    other = 1 - core
                    send_ref[q, 0, rows, :] = acc_ref[:, other * w:(other + 1) * w].astype(bf16)
                    stash_ref[...] = acc_ref[:, core * w:(core + 1) * w]
            to_sibling(q, im).start()

            @pl.when(tile == N_CHIP * nm - 1)
            def _():
                finish_tile(q, im)

        @pl.when((q == N_CHIP - 1) & (im == nm - 1) & (t == nt - 1))
        def _():
            for step, (dx, dy) in enumerate(CHIP_STEPS):
                pltpu.make_async_remote_copy(
                    src_ref=send_ref.at[step, 0], dst_ref=sib_ref.at[step],
                    send_sem=d2d_send.at[step], recv_sem=d2d_recv.at[step, 0],
                    device_id=(x, y, 1 - c), device_id_type=mesh_id).wait_send()
                src, dst = send_ref.at[step, 1], recv_ref.at[step]
                if dx or dy:
                    pltpu.make_async_remote_copy(
                        src_ref=src, dst_ref=dst, send_sem=ici_send.at[step], recv_sem=ici_recv.at[step],
                        device_id=(_flip(x, dx), _flip(y, dy), c), device_id_type=mesh_id).wait()
                else:
                    pltpu.make_async_copy(src, dst, local_sem).wait()

    grid_spec = pltpu.PrefetchScalarGridSpec(
        num_scalar_prefetch=1, grid=(N_CHIP, nm, nt),
        in_specs=[pl.BlockSpec((tm, tt), lambda q, im, t, order: (im, t)),
                  pl.BlockSpec((tt, tn), lambda q, im, t, order: (t, order[q]))],
        out_specs=pl.BlockSpec(memory_space=pl.ANY),
        scratch_shapes=[pltpu.VMEM((tm, tn), f32), pltpu.VMEM((N_CHIP, 2, m, w), bf16), pltpu.VMEM((N_CHIP, m, w), bf16),
                        pltpu.VMEM((tm, w), f32),
                        pltpu.SemaphoreType.DMA((N_CHIP,)), pltpu.SemaphoreType.DMA((N_CHIP, nm)),
                        pltpu.SemaphoreType.DMA((N_CHIP - 1,)), pltpu.SemaphoreType.DMA((N_CHIP - 1,)),
                        pltpu.SemaphoreType.DMA])
    return pl.pallas_call(
        body, name=name, grid_spec=grid_spec,
        out_shape=_sds((N_CHIP, m, w), bf16),
        compiler_params=_cparams(("arbitrary", "arbitrary", "arbitrary")),
    )(_chip_order(), a_t, b)


def _adamw(w, g, m, v):
    m = ADAM_B1 * m + (1.0 - ADAM_B1) * g
    v = ADAM_B2 * v + (1.0 - ADAM_B2) * jnp.square(g)
    m_hat = m / (1.0 - ADAM_B1 ** ADAM_STEP)
    v_hat = v / (1.0 - ADAM_B2 ** ADAM_STEP)
    delta = -ADAM_LR * (m_hat / (jnp.sqrt(v_hat) + ADAM_EPS) + ADAM_WD * w)
    return delta, m, v


def _sum_parts(p_ref):
    g = p_ref[0].astype(f32)
    for j in range(1, p_ref.shape[0]):
        g = g + p_ref[j].astype(f32)
    return g


def _adamw_update(w, m, v, parts, name, transposed):
    rows, cols = w.shape
    n_parts = parts.shape[0]
    if transposed:
        tr = LANES
        p_spec = pl.BlockSpec((n_parts, cols, tr), lambda i: (0, 0, i))
    else:
        tr = next(t for t in (256, 128, 64, 32, 16, 8) if rows % t == 0)
        p_spec = pl.BlockSpec((n_parts, tr, cols), lambda i: (0, i, 0))
    spec = pl.BlockSpec((tr, cols), lambda i: (i, 0))

    def body(w_ref, p_ref, m_ref, v_ref, g_ref, d_ref, nm_ref, nv_ref):
        g = _sum_parts(p_ref)
        if transposed:
            g = g.T
        g_ref[...] = g
        d_ref[...], nm_ref[...], nv_ref[...] = _adamw(w_ref[...], g, m_ref[...], v_ref[...])

    return pl.pallas_call(
        body, name=name, grid=(rows // tr,),
        in_specs=[spec, p_spec, spec, spec], out_specs=[spec] * 4,
        out_shape=[_sds((rows, cols), f32)] * 4,
        compiler_params=_cparams(("parallel",)),
    )(w, parts, m, v)


SMALL_GROUPS = [
    (["b_gate", "ln1_g", "ln1_b", "b_up", "b_down", "ln2_g", "ln2_b", "ssm_d"], 1),
    (["ssm_lam_re", "ssm_lam_im", "ssm_c_re", "ssm_c_im"], 0),
    (["ssm_b_re", "ssm_b_im"], 0),
    (["conv_w"], 0),
    (["ssm_log_dt"], 0),
]


def _sum_small(group_parts):
    def body(*refs):
        n = len(refs) // 2
        for p_ref, o_ref in zip(refs[:n], refs[n:]):
            o_ref[...] = _sum_parts(p_ref)

    return pl.pallas_call(
        body, name="sum_small",
        out_shape=[_sds(p.shape[1:], f32) for p in group_parts],
        compiler_params=_cparams(),
    )(*group_parts)


def _adamw_small(ws, ms, vs, group_sums):
    names = [k for group, _ in SMALL_GROUPS for k in group]
    n = len(names)

    def body(*refs):
        w_refs, m_refs, v_refs = (dict(zip(names, refs[j * n:(j + 1) * n])) for j in range(3))
        p_refs = refs[3 * n:3 * n + len(SMALL_GROUPS)]
        out_refs = [dict(zip(names, refs[3 * n + len(SMALL_GROUPS) + j * n:][:n])) for j in range(4)]
        for (group, axis), p_ref in zip(SMALL_GROUPS, p_refs):
            total = p_ref[...]
            off = 0
            for k in group:
                size = SMALL[k][axis]
                g = total[:, off:off + size] if axis == 1 else total[off:off + size, :]
                off += size
                d, nm, nv = _adamw(w_refs[k][...], g, m_refs[k][...], v_refs[k][...])
                for j, val in enumerate((g, d, nm, nv)):
                    out_refs[j][k][...] = val

    res = pl.pallas_call(
        body, name="adamw_small",
        out_shape=[_sds(SMALL[k], f32) for _ in range(4) for k in names],
        compiler_params=_cparams(),
    )(*[ws[k] for k in names], *[ms[k] for k in names], *[vs[k] for k in names], *group_sums)
    return [dict(zip(names, res[j * n:(j + 1) * n])) for j in range(4)]


def _ssm_discretize(lam_re, lam_im, log_dt, b_re, b_im):
    dt = jnp.exp(log_dt)[:, None]
    mag = jnp.exp(lam_re * dt)
    abar_r = mag * jnp.cos(lam_im * dt)
    abar_i = mag * jnp.sin(lam_im * dt)
    den = lam_re * lam_re + lam_im * lam_im
    nr = abar_r - 1.0
    ni = abar_i
    kr = (nr * lam_re + ni * lam_im) / den
    ki = (ni * lam_re - nr * lam_im) / den
    bbar_r = kr[..., None] * b_re - ki[..., None] * b_im
    bbar_i = kr[..., None] * b_im + ki[..., None] * b_re
    return abar_r, abar_i, bbar_r, bbar_i


def _state_layout(re, im):
    parts = []
    for half in range(N_HALF):
        cols = slice(half * HALF_STATE, (half + 1) * HALF_STATE)
        parts += [re[..., cols], im[..., cols]]
    return jnp.concatenate(parts, axis=-1)


def _state_unlayout(a):
    re = jnp.concatenate([a[..., _half_cols(h)[0]] for h in range(N_HALF)], axis=-1)
    im = jnp.concatenate([a[..., _half_cols(h)[1]] for h in range(N_HALF)], axis=-1)
    return re, im


def _abar_powers(abar_r, abar_i):
    pr, pi = abar_r.reshape(1, N_STATE), abar_i.reshape(1, N_STATE)
    while pr.shape[0] < SSM_SEG:
        tr, ti = pr[-1:], pi[-1:]
        pr, pi = (jnp.concatenate([pr, pr * tr - pi * ti], axis=0), jnp.concatenate([pi, pr * ti + pi * tr], axis=0))
    return _state_layout(pr, pi)


HALF_GROUPS = SSM_GROUPS // N_HALF


def _half_block_diag(blocks):
    _, r, c = blocks.shape
    eye = jnp.eye(HALF_GROUPS, dtype=blocks.dtype)
    b4 = blocks.reshape(N_HALF, HALF_GROUPS, r, c)
    return jnp.einsum("ngrc,gk->ngrkc", b4, eye).reshape(N_HALF, HALF_GROUPS * r, HALF_GROUPS * c)


def _half_diag_blocks(mat, r, c):
    eye = jnp.eye(HALF_GROUPS, dtype=mat.dtype)
    m5 = mat.reshape(N_HALF, HALF_GROUPS, r, HALF_GROUPS, c)
    return jnp.einsum("ngrkc,gk->ngrc", m5, eye).reshape(SSM_GROUPS, r, c)


BIG = ["w_in", "w_conv_out", "w_glu", "w_kv", "w_xattn_out", "w_out", "w_up", "w_down"]
COL_SHARDED = ["w_in", "w_conv_out", "w_glu", "w_xattn_out", "w_up"]
SMALL = {"b_gate": (1, GATE_COLS), "conv_w": (3, CONV_W), "ssm_lam_re": (SSM_GROUPS, SSM_STATE),
         "ssm_lam_im": (SSM_GROUPS, SSM_STATE), "ssm_log_dt": (1, SSM_GROUPS),
         "ssm_b_re": (N_STATE, SSM_GROUP), "ssm_b_im": (N_STATE, SSM_GROUP),
         "ssm_c_re": (SSM_W, SSM_STATE), "ssm_c_im": (SSM_W, SSM_STATE), "ssm_d": (1, SSM_W),
         "ln1_g": (1, D_MODEL), "ln1_b": (1, D_MODEL), "b_up": (1, D_FF), "b_down": (1, D_MODEL),
         "ln2_g": (1, D_MODEL), "ln2_b": (1, D_MODEL)}
WEIGHTS = ["w_in", "b_gate", "conv_w", "w_conv_out", "ssm_lam_re", "ssm_lam_im", "ssm_log_dt", "ssm_b_re", "ssm_b_im",
           "ssm_c_re", "ssm_c_im", "ssm_d", "w_glu", "w_kv", "w_xattn_out", "w_out", "ln1_g", "ln1_b", "w_up", "b_up",
           "w_down", "b_down", "ln2_g", "ln2_b"]


def _local_step(x, mem, tgt, full, late, small):
    lam_re, lam_im, log_dt = small["ssm_lam_re"], small["ssm_lam_im"], small["ssm_log_dt"].reshape(SSM_GROUPS)
    b_shape = (SSM_GROUPS, SSM_STATE, SSM_GROUP)
    c_shape = (SSM_GROUPS, SSM_GROUP, SSM_STATE)
    disc, disc_vjp = jax.vjp(_ssm_discretize, lam_re, lam_im, log_dt,
                             small["ssm_b_re"].reshape(b_shape), small["ssm_b_im"].reshape(b_shape))
    abar_r, abar_i, bbar_r, bbar_i = disc
    pw = _abar_powers(abar_r, abar_i)
    c_re, c_im = small["ssm_c_re"].reshape(c_shape), small["ssm_c_im"].reshape(c_shape)
    b_half = jnp.concatenate([_half_block_diag(bbar_r.transpose(0, 2, 1)), _half_block_diag(bbar_i.transpose(0, 2, 1))],
                             axis=2).astype(bf16)
    c_half = jnp.concatenate([_half_block_diag(c_re.transpose(0, 2, 1)), -_half_block_diag(c_im.transpose(0, 2, 1))],
                             axis=1).astype(bf16)

    s_len = x.shape[0]
    stack = lambda a: a.reshape(-1, a.shape[-1])
    kv, memb = _kv_proj(mem, full["w_kv"])
    (xbt, g, cin, u, q, ain, ob, aint, obt), side = _in_proj(
        x, full["w_in"], small["b_gate"], small["conv_w"], kv,
        [late[k] for k in ("w_glu", "w_conv_out", "w_xattn_out", "w_out", "w_up")])
    w_glu_t, w_co_t, w_xo_t, w_out, w_up_t = (stack(a) for a in side)
    y_ssm, cm_all, side = _ssm_fwd(u, b_half, c_half, pw, small["ssm_d"], [late["w_down"]])
    w_down = stack(side[0])
    ysbt, mb, xhat1, rstd1 = _mid_fwd(y_ssm, g, ain, ob, x, w_glu_t, w_co_t, w_xo_t, w_out,
                                      small["ln1_g"], small["ln1_b"])
    (x1bt, hdn, dr2bt, dpre, dx1, loss, dl2g, dl2b, dbdn, dbup) = _mlp_fwd_bwd(
        xhat1, tgt, small["ln1_g"], small["ln1_b"], w_up_t, small["b_up"], w_down,
        small["b_down"], small["ln2_g"], small["ln2_b"])
    recv = {}
    recv["w_down"] = _weight_grad_scatter(dr2bt, hdn, "dw_down", tm=512, tt=2048)
    recv["w_up"] = _weight_grad_scatter(x1bt, dpre, "dw_up", tm=512, tt=2048)
    (dxp, dr1bt, dgp, dya, dyc, dglu, dyssm, dl1g, dl1b, dbg) = _mid_bwd(
        dx1, xhat1, rstd1, g, ain, ob, y_ssm, small["ln1_g"], w_out, w_glu_t, w_co_t, w_xo_t)
    recv["w_out"] = _weight_grad_scatter(dr1bt, mb, "dw_out", tm=512, tt=s_len)
    recv["w_glu"] = _weight_grad_scatter(ysbt, dglu, "dw_glu", tm=512, tt=s_len)
    du, db_half, dc_half, da8, dd = _ssm_bwd(u, dyssm, cm_all, b_half, c_half, pw, small["ssm_d"])
    dabar_r, dabar_i = _state_unlayout(jnp.sum(da8, axis=0))
    dbbar_r = _half_diag_blocks(db_half[:, :, :HALF_STATE], SSM_GROUP, SSM_STATE).transpose(0, 2, 1)
    dbbar_i = _half_diag_blocks(db_half[:, :, HALF_STATE:], SSM_GROUP, SSM_STATE).transpose(0, 2, 1)
    g_shape = (SSM_GROUPS, SSM_STATE)
    dlam_re, dlam_im, dlog_dt, db_re, db_im = disc_vjp(
        (dabar_r.reshape(g_shape), dabar_i.reshape(g_shape), dbbar_r, dbbar_i))
    dc_re = _half_diag_blocks(dc_half[:, :HALF_STATE, :], SSM_STATE, SSM_GROUP).transpose(0, 2, 1)
    dc_im = -_half_diag_blocks(dc_half[:, HALF_STATE:, :], SSM_STATE, SSM_GROUP).transpose(0, 2, 1)

    small_grads = {
        "b_gate": dbg, "ssm_lam_re": dlam_re, "ssm_lam_im": dlam_im, "ssm_log_dt": dlog_dt,
        "ssm_b_re": db_re, "ssm_b_im": db_im, "ssm_c_re": dc_re, "ssm_c_im": dc_im, "ssm_d": dd,
        "ln1_g": dl1g, "ln1_b": dl1b, "b_up": dbup, "b_down": dbdn, "ln2_g": dl2g, "ln2_b": dl2b,
    }
    small_grads = {k: a.reshape(SMALL[k]) for k, a in small_grads.items()}
    groups = [(group, axis) for group, axis in SMALL_GROUPS if group != ["conv_w"]]
    stacks = [jnp.concatenate([small_grads[k] for k in group], axis=axis) if len(group) > 1 else small_grads[group[0]]
              for group, axis in groups]
    dense = lambda a: a.reshape(-1, LANES) if a.size % LANES == 0 else a
    dconv, dq, dkv, group_parts = _branch_bwd(dya, dyc, cin, q, kv, small["conv_w"], w_co_t, w_xo_t,
                                              [dense(a) for a in stacks])
    recv["w_conv_out"] = _weight_grad_scatter(aint, dya, "dw_conv_out", tm=512, tt=s_len)
    recv["w_xattn_out"] = _weight_grad_scatter(obt, dyc, "dw_xattn_out", tm=512, tt=s_len)
    recv["w_kv"] = _weight_grad_scatter(dkv.T.astype(bf16), memb, "dw_kv", tm=D_MODEL, tt=MEM_LEN)
    dx, dproj = _in_proj_bwd(dgp, dconv, du, dq, dxp, full["w_in"])
    recv["w_in"] = _weight_grad_scatter(xbt, dproj, "dw_in", tm=512, tt=2048)
    sums = _sum_small(group_parts)
    group_sums = dict(zip([tuple(group) for group, _ in groups], [s.reshape(a.shape) for s, a in zip(sums, stacks)]))
    group_sums[("conv_w",)] = sums[-1][0:3]
    return loss[0, 0], dx, recv, [group_sums[tuple(group)] for group, _ in SMALL_GROUPS]


def kernel(x, mem, w_in, b_gate, conv_w, w_conv_out, ssm_lam_re, ssm_lam_im, ssm_log_dt, ssm_b_re, ssm_b_im, ssm_c_re, ssm_c_im, ssm_d, w_glu, w_kv, w_xattn_out, w_out, ln1_g, ln1_b, w_up, b_up, w_down, b_down, ln2_g, ln2_b, loss_target, m_w_in, m_b_gate, m_conv_w, m_w_conv_out, m_ssm_lam_re, m_ssm_lam_im, m_ssm_log_dt, m_ssm_b_re, m_ssm_b_im, m_ssm_c_re, m_ssm_c_im, m_ssm_d, m_w_glu, m_w_kv, m_w_xattn_out, m_w_out, m_ln1_g, m_ln1_b, m_w_up, m_b_up, m_w_down, m_b_down, m_ln2_g, m_ln2_b, v_w_in, v_b_gate, v_conv_w, v_w_conv_out, v_ssm_lam_re, v_ssm_lam_im, v_ssm_log_dt, v_ssm_b_re, v_ssm_b_im, v_ssm_c_re, v_ssm_c_im, v_ssm_d, v_w_glu, v_w_kv, v_w_xattn_out, v_w_out, v_ln1_g, v_ln1_b, v_w_up, v_b_up, v_w_down, v_b_down, v_ln2_g, v_ln2_b):
    w = dict(w_in=w_in, b_gate=b_gate, conv_w=conv_w, w_conv_out=w_conv_out, ssm_lam_re=ssm_lam_re,
             ssm_lam_im=ssm_lam_im, ssm_log_dt=ssm_log_dt, ssm_b_re=ssm_b_re, ssm_b_im=ssm_b_im, ssm_c_re=ssm_c_re,
             ssm_c_im=ssm_c_im, ssm_d=ssm_d, w_glu=w_glu, w_kv=w_kv, w_xattn_out=w_xattn_out, w_out=w_out,
             ln1_g=ln1_g, ln1_b=ln1_b, w_up=w_up, b_up=b_up, w_down=w_down, b_down=b_down, ln2_g=ln2_g, ln2_b=ln2_b)
    m = dict(w_in=m_w_in, b_gate=m_b_gate, conv_w=m_conv_w, w_conv_out=m_w_conv_out, ssm_lam_re=m_ssm_lam_re,
             ssm_lam_im=m_ssm_lam_im, ssm_log_dt=m_ssm_log_dt, ssm_b_re=m_ssm_b_re, ssm_b_im=m_ssm_b_im,
             ssm_c_re=m_ssm_c_re, ssm_c_im=m_ssm_c_im, ssm_d=m_ssm_d, w_glu=m_w_glu, w_kv=m_w_kv,
             w_xattn_out=m_w_xattn_out, w_out=m_w_out, ln1_g=m_ln1_g, ln1_b=m_ln1_b, w_up=m_w_up, b_up=m_b_up,
             w_down=m_w_down, b_down=m_b_down, ln2_g=m_ln2_g, ln2_b=m_ln2_b)
    v = dict(w_in=v_w_in, b_gate=v_b_gate, conv_w=v_conv_w, w_conv_out=v_w_conv_out, ssm_lam_re=v_ssm_lam_re,
             ssm_lam_im=v_ssm_lam_im, ssm_log_dt=v_ssm_log_dt, ssm_b_re=v_ssm_b_re, ssm_b_im=v_ssm_b_im,
             ssm_c_re=v_ssm_c_re, ssm_c_im=v_ssm_c_im, ssm_d=v_ssm_d, w_glu=v_w_glu, w_kv=v_w_kv,
             w_xattn_out=v_w_xattn_out, w_out=v_w_out, ln1_g=v_ln1_g, ln1_b=v_ln1_b, w_up=v_w_up, b_up=v_b_up,
             w_down=v_w_down, b_down=v_b_down, ln2_g=v_ln2_g, ln2_b=v_ln2_b)
    out_shapes = {k: a.shape for k, a in w.items()}
    shard2d = lambda k, a: a.reshape((3, CONV_W // N_DEV) if k == "conv_w" else SMALL[k]) if k in SMALL else a[0]
    w, m, v = ({k: shard2d(k, a) for k, a in d.items()} for d in (w, m, v))

    shards = {k: w[k].T.astype(bf16) if k in COL_SHARDED else w[k].astype(bf16) for k in BIG}
    conv_pad = jnp.pad(w["conv_w"], ((0, 5), (0, LANES - CONV_W // N_DEV)))
    early = ["w_in", "w_kv"]
    gathered = _all_gather([shards[k] for k in early] + [conv_pad], "gather_weights")
    full = {k: a.reshape(-1, a.shape[-1]) for k, a in zip(early, gathered[:-1])}
    late = {k: shards[k] for k in BIG if k not in early}
    conv_full = gathered[-1][:, :3, :CONV_W // N_DEV].transpose(1, 0, 2).reshape(3, CONV_W)
    small = {k: (conv_full if k == "conv_w" else w[k]) for k in SMALL}

    loss, dx, recv, group_sums = _local_step(x[0], mem[0], loss_target[0], full, late, small)

    grads, deltas, new_m, new_v = {}, {}, {}, {}
    for k in BIG:
        res = _adamw_update(w[k], m[k], v[k], recv[k], "adamw_" + k, transposed=k not in COL_SHARDED)
        grads[k], deltas[k], new_m[k], new_v[k] = res

    widen = lambda k, a: jnp.tile(a, (1, N_DEV)) if k == "conv_w" else a
    res = _adamw_small(small, {k: widen(k, m[k]) for k in SMALL}, {k: widen(k, v[k]) for k in SMALL}, group_sums)
    dev = _slot(_mesh_place())
    for d, small_res in zip((grads, deltas, new_m, new_v), res):
        for k, a in small_res.items():
            if k == "conv_w":
                a = lax.dynamic_slice_in_dim(a, dev * (CONV_W // N_DEV), CONV_W // N_DEV, axis=1)
            d[k] = a

    loss = lax.psum(loss, ("x", "y", "c"))
    outs = [loss, dx[None]]
    for d in (grads, deltas, new_m, new_v):
        outs += [d[k].reshape(out_shapes[k]) for k in WEIGHTS]
    return tuple(outs)
```

```python
import functools
import math

import jax
import jax.numpy as jnp
from jax import lax
from jax.experimental import pallas as pl
from jax.experimental.pallas import tpu as pltpu

f32 = jnp.float32
bf16 = jnp.bfloat16

D_MODEL = 1024
MEM_LEN = 256
GATE_COLS = 3 * D_MODEL
CONV_W = 512
SSM_W = 512
XATTN_W = 512
HEADS = 4
HEAD_DIM = 128
D_FF = 4096
IN_COLS = GATE_COLS + 3 * CONV_W + SSM_W + XATTN_W
SSM_GROUPS = 32
SSM_GROUP = 16
SSM_STATE = 64
N_STATE = SSM_GROUPS * SSM_STATE
ALPHA = 2.0 ** 0.25
LN_EPS = 1e-5
N_DEV = 8

ADAM_LR = 0.001
ADAM_B1 = 0.9
ADAM_B2 = 0.999
ADAM_EPS = 1e-08
ADAM_WD = 0.01
ADAM_STEP = 10

VMEM_LIMIT_V7X = 56 * 2 ** 20
SUBLANES = 8
LANES = 128

TOKEN_TILE = 256
SSM_BLOCK = 256
SSM_SEG = SSM_BLOCK // SUBLANES
LANE_CHUNK = 512
N_HALF = 2
HALF_W = SSM_W // N_HALF
HALF_STATE = N_STATE // N_HALF
HALF_COLS = 2 * HALF_STATE

NT = (((1,), (1,)), ((), ()))
TN = (((0,), (0,)), ((), ()))
NN = (((1,), (0,)), ((), ()))


def _dot(a, b, dims=NN):
    return lax.dot_general(a, b, dims, preferred_element_type=f32)


def _cparams(sem=None):
    return pltpu.CompilerParams(dimension_semantics=sem, vmem_limit_bytes=VMEM_LIMIT_V7X)


def _row_spec(tm, cols, rev_n=None):
    if rev_n is None:
        return pl.BlockSpec((tm, cols), lambda i: (i, 0))
    return pl.BlockSpec((tm, cols), lambda i: (rev_n - 1 - i, 0))


def _col_spec(rows, tm):
    return pl.BlockSpec((rows, tm), lambda i: (0, i))


def _const_spec(shape):
    nd = len(shape)
    return pl.BlockSpec(shape, lambda *_: (0,) * nd, pipeline_mode=pl.Buffered(1))


def _acc_spec(shape):
    nd = len(shape)
    return pl.BlockSpec(shape, lambda *_: (0,) * nd)


def _sds(shape, dtype):
    return jax.ShapeDtypeStruct(shape, dtype)


def _gelu(x):
    c = math.sqrt(2.0 / math.pi)
    return 0.5 * x * (1.0 + jnp.tanh(c * (x + 0.044715 * x * x * x)))


def _gelu_grad(x):
    c = math.sqrt(2.0 / math.pi)
    t = jnp.tanh(c * (x + 0.044715 * x * x * x))
    return 0.5 * (1.0 + t) + 0.5 * x * (1.0 - t * t) * c * (1.0 + 3.0 * 0.044715 * x * x)


def _colsum(a):
    return jnp.sum(a, axis=0, keepdims=True)


def _mesh_place():
    return lax.axis_index("x"), lax.axis_index("y"), lax.axis_index("c")


def _slot(p):
    return 4 * p[0] + 2 * p[1] + p[2]


def _other_devices(me):
    x, y, c = me
    flip = lambda v, d: 1 - v if d else v
    return [(flip(x, dx), flip(y, dy), flip(c, dc)) for dx in (0, 1) for dy in (0, 1) for dc in (0, 1)][1:]


def _all_gather(blocks, name):
    n = len(blocks)

    def body(*refs):
        ins, outs = refs[:n], refs[n:2 * n]
        send_sems, recv_sems, local_sems = refs[2 * n:]
        x, y, c = _mesh_place()
        me, sibling = (x, y, c), (x, y, 1 - c)
        chips = [(1 - x, y), (x, 1 - y), (1 - x, 1 - y)]

        def copy(a, k, block, to, src=None):
            rows = outs[a].at[_slot(block)]
            return pltpu.make_async_remote_copy(
                src_ref=rows if src is None else src, dst_ref=rows,
                send_sem=send_sems.at[a, k], recv_sem=recv_sems.at[a, k],
                device_id=to, device_id_type=pl.DeviceIdType.MESH)

        mine = [pltpu.make_async_copy(ins[a], outs[a].at[_slot(me)], local_sems.at[a]) for a in range(n)]
        for cp in mine:
            cp.start()
        first = []
        for a in range(n):
            first.append(copy(a, 0, me, sibling, src=ins[a]))
            first += [copy(a, 1 + j, me, (*chip, c), src=ins[a]) for j, chip in enumerate(chips)]
        for cp in first:
            cp.start()
        passed = []
        for a in range(n):
            for j, chip in enumerate(chips):
                copy(a, 1 + j, (*chip, c), me).wait_recv()
                fwd = copy(a, 4 + j, (*chip, c), sibling)
                fwd.start()
                passed.append(fwd)
        for a in range(n):
            copy(a, 0, sibling, me).wait_recv()
            for j, chip in enumerate(chips):
                copy(a, 4 + j, (*chip, 1 - c), me).wait_recv()
        for cp in first + passed:
            cp.wait_send()
        for cp in mine:
            cp.wait()

    any_spec = pl.BlockSpec(memory_space=pl.ANY)
    return pl.pallas_call(
        body, name=name,
        out_shape=[_sds((N_DEV,) + b.shape, b.dtype) for b in blocks],
        in_specs=[any_spec] * n, out_specs=[any_spec] * n,
        scratch_shapes=[pltpu.SemaphoreType.DMA((n, 7)), pltpu.SemaphoreType.DMA((n, 7)),
                        pltpu.SemaphoreType.DMA((n,))],
    )(*blocks)


def _side_gather_copies(ins, outs, send_sems, recv_sems, local_sems):
    me = _mesh_place()
    copies = []
    for a, (src, dst) in enumerate(zip(ins, outs)):
        copies.append(pltpu.make_async_copy(src, dst.at[_slot(me)], local_sems.at[a]))
        for k, peer in enumerate(_other_devices(me)):
            copies.append(pltpu.make_async_remote_copy(
                src_ref=src, dst_ref=dst.at[_slot(me)], send_sem=send_sems.at[a, k], recv_sem=recv_sems.at[a, k],
                device_id=peer, device_id_type=pl.DeviceIdType.MESH))
    return copies


def _side_gather_specs(blocks):
    n = len(blocks)
    any_spec = pl.BlockSpec(memory_space=pl.ANY)
    return ([any_spec] * n, [_sds((N_DEV,) + b.shape, b.dtype) for b in blocks],
            [pltpu.SemaphoreType.DMA((n, N_DEV - 1)), pltpu.SemaphoreType.DMA((n, N_DEV - 1)),
             pltpu.SemaphoreType.DMA((n,))])


def _kv_proj(mem, w_kv):
    def body(mem_ref, w_ref, kv_ref, kt_ref, memb_ref):
        mb = mem_ref[...].astype(bf16)
        memb_ref[...] = mb
        kv = _dot(mb, w_ref[...]).astype(bf16)
        kv_ref[...] = kv
        kt_ref[...] = kv[:, :XATTN_W].T

    return pl.pallas_call(
        body, name="kv_proj",
        out_shape=[_sds((MEM_LEN, 2 * XATTN_W), bf16), _sds((XATTN_W, MEM_LEN), bf16), _sds((MEM_LEN, D_MODEL), bf16)],
        compiler_params=_cparams(),
    )(mem, w_kv)


def _attention_probs(qb, kv_ref, h):
    kh = kv_ref[:, h * HEAD_DIM:(h + 1) * HEAD_DIM]
    s = _dot(qb[:, h * HEAD_DIM:(h + 1) * HEAD_DIM], kh, NT) * (HEAD_DIM ** -0.5)
    e = jnp.exp(s - jnp.max(s, axis=-1, keepdims=True))
    return e / jnp.sum(e, axis=-1, keepdims=True)


def _in_proj(x, w_in_t, b_gate, conv_w, kv, side_blocks):
    s_len = x.shape[0]
    tm = TOKEN_TILE
    n = s_len // tm
    ns = len(side_blocks)
    side_in_specs, side_shapes, side_sems = _side_gather_specs(side_blocks)

    def body(*refs):
        (x_ref, win_ref, bg_ref, cw_ref, kv_ref) = refs[:5]
        side_ins = refs[5:5 + ns]
        (xbt_ref, g_ref, cin_ref, u_ref, q_ref, ain_ref, o_ref, aint_ref, ot_ref) = refs[5 + ns:14 + ns]
        side_outs = refs[14 + ns:14 + 2 * ns]
        zs_ref = refs[14 + 2 * ns]
        side = _side_gather_copies(side_ins, side_outs, *refs[15 + 2 * ns:])
        i = pl.program_id(0)

        @pl.when(i == 0)
        def _():
            for cp in side:
                cp.start()

        xb = x_ref[...].astype(bf16)
        xbt_ref[...] = xb.T
        proj = _dot(xb, win_ref[...], NT)
        g_ref[...] = jax.nn.sigmoid(proj[:, :GATE_COLS] + bg_ref[...]).astype(bf16)
        cin = proj[:, GATE_COLS:GATE_COLS + 3 * CONV_W]
        cin_ref[...] = cin
        u_ref[...] = proj[:, GATE_COLS + 3 * CONV_W:GATE_COLS + 3 * CONV_W + SSM_W]
        qb = proj[:, IN_COLS - XATTN_W:].astype(bf16)
        q_ref[...] = qb

        cb, cc, ch = cin[:, :CONV_W], cin[:, CONV_W:2 * CONV_W], cin[:, 2 * CONV_W:]
        z = cc * ch

        @pl.when(i == 0)
        def _():
            zs_ref[0:8, :] = jnp.zeros((8, CONV_W), f32)

        zs_ref[8:8 + tm, :] = z
        z1 = zs_ref[pl.ds(7, tm), :]
        z2 = zs_ref[pl.ds(6, tm), :]
        cw = cw_ref[...]
        cz = cw[0:1] * z2 + cw[1:2] * z1 + cw[2:3] * z
        zs_ref[0:8, :] = zs_ref[tm:tm + 8, :]
        ain = (cb * cz).astype(bf16)
        ain_ref[...] = ain
        aint_ref[...] = ain.T

        probs = [_attention_probs(qb, kv_ref, h) for h in range(HEADS)]
        outs = [_dot(probs[h].astype(bf16), kv_ref[:, XATTN_W + h * HEAD_DIM:XATTN_W + (h + 1) * HEAD_DIM])
                for h in range(HEADS)]
        ob = jnp.concatenate(outs, axis=1).astype(bf16)
        o_ref[...] = ob
        ot_ref[...] = ob.T

        @pl.when(i == n - 1)
        def _():
            for cp in side:
                cp.wait()

    row_cols = [(GATE_COLS, bf16), (3 * CONV_W, f32), (SSM_W, f32), (XATTN_W, bf16), (CONV_W, bf16), (XATTN_W, bf16)]
    t_rows = [D_MODEL, CONV_W, XATTN_W]
    outs = pl.pallas_call(
        body, name="in_proj", grid=(n,),
        in_specs=[_row_spec(tm, D_MODEL), _const_spec((IN_COLS, D_MODEL)), _const_spec((1, GATE_COLS)),
                  _const_spec((3, CONV_W)), _const_spec((MEM_LEN, 2 * XATTN_W))] + side_in_specs,
        out_specs=([_col_spec(t_rows[0], tm)] + [_row_spec(tm, c) for c, _ in row_cols]
                   + [_col_spec(t_rows[1], tm), _col_spec(t_rows[2], tm)] + side_in_specs),
        out_shape=([_sds((t_rows[0], s_len), bf16)] + [_sds((s_len, c), dt) for c, dt in row_cols]
                   + [_sds((t_rows[1], s_len), bf16), _sds((t_rows[2], s_len), bf16)] + side_shapes),
        scratch_shapes=[pltpu.VMEM((tm + 8, CONV_W), f32)] + side_sems,
        compiler_params=_cparams(("arbitrary",)),
    )(x, w_in_t, b_gate, conv_w, kv, *side_blocks)
    return outs[:9], outs[9:]


def _state_cols(chunk):
    half, off = divmod(chunk * LANE_CHUNK, HALF_STATE)
    lo = half * HALF_COLS + off
    return slice(lo, lo + LANE_CHUNK), slice(lo + HALF_STATE, lo + HALF_STATE + LANE_CHUNK)


def _half_cols(half):
    lo = half * HALF_COLS
    return slice(lo, lo + HALF_STATE), slice(lo + HALF_STATE, lo + HALF_COLS)


def _rows_to_segments(src_ref, stage_ref, dst_ref):
    nc = SSM_W // LANES
    for c in range(nc):
        stage_ref[c] = src_ref[:, c * LANES:(c + 1) * LANES]
    for c in range(nc):
        for k in range(SSM_SEG):
            dst_ref[k * SUBLANES:(k + 1) * SUBLANES, c * LANES:(c + 1) * LANES] = (
                stage_ref[c, pl.ds(k, SUBLANES, stride=SSM_SEG), :])


def _rows_from_segments(src_ref, stage_ref, dst_ref):
    nc = SSM_W // LANES
    for c in range(nc):
        for k in range(SSM_SEG):
            stage_ref[c, pl.ds(k, SUBLANES, stride=SSM_SEG), :] = (
                src_ref[k * SUBLANES:(k + 1) * SUBLANES, c * LANES:(c + 1) * LANES])
    for c in range(nc):
        dst_ref[:, c * LANES:(c + 1) * LANES] = stage_ref[c]


def _ssm_scan(s_ref, pw_ref, init_ref, reverse, unroll):
    for chunk in range(N_STATE // LANE_CHUNK):
        re, im = _state_cols(chunk)
        ar = jnp.broadcast_to(pw_ref[0:1, re], (SUBLANES, LANE_CHUNK))
        ai = jnp.broadcast_to(pw_ref[0:1, im], (SUBLANES, LANE_CHUNK))
        if reverse:
            ai = -ai

        def step(j, carry, re=re, im=im, ar=ar, ai=ai):
            sr, si = carry
            k = (SSM_SEG - 1 - j) if reverse else j
            r0 = pl.multiple_of(k * SUBLANES, SUBLANES)
            nr = ar * sr - ai * si + s_ref[pl.ds(r0, SUBLANES), re]
            ni = ar * si + ai * sr + s_ref[pl.ds(r0, SUBLANES), im]
            s_ref[pl.ds(r0, SUBLANES), re] = nr
            s_ref[pl.ds(r0, SUBLANES), im] = ni
            return nr, ni

        if init_ref is None:
            init = (jnp.zeros((SUBLANES, LANE_CHUNK), f32),) * 2
        else:
            init = (init_ref[:, re], init_ref[:, im])
        lax.fori_loop(0, SSM_SEG, step, init, unroll=unroll)


def _ssm_add_carry(s_ref, pw_ref, cm_ref, reverse):
    for chunk in range(N_STATE // LANE_CHUNK):
        re, im = _state_cols(chunk)
        cr, ci = cm_ref[:, re], cm_ref[:, im]
        for k in range(SSM_SEG):
            pk = (SSM_SEG - 1 - k) if reverse else k
            pr = pw_ref[pk:pk + 1, re]
            pi = pw_ref[pk:pk + 1, im]
            if reverse:
                pi = -pi
            rows = slice(k * SUBLANES, (k + 1) * SUBLANES)
            s_ref[rows, re] = s_ref[rows, re] + (pr * cr - pi * ci)
            s_ref[rows, im] = s_ref[rows, im] + (pr * ci + pi * cr)


def _ssm_carries(first_row, s_ref, pw_ref, carry_ref, cm_ref, reverse):
    order = range(SUBLANES - 1, -1, -1) if reverse else range(SUBLANES)
    for half in range(N_HALF):
        re, im = _half_cols(half)
        a_r, a_i = pw_ref[SSM_SEG - 1:SSM_SEG, re], pw_ref[SSM_SEG - 1:SSM_SEG, im]
        if reverse:
            a_i = -a_i
        cr, ci = carry_ref[0:1, re], carry_ref[0:1, im]
        for seg in order:
            cm_ref[seg:seg + 1, re] = cr
            cm_ref[seg:seg + 1, im] = ci
            er = s_ref[first_row + seg:first_row + seg + 1, re]
            ei = s_ref[first_row + seg:first_row + seg + 1, im]
            cr, ci = a_r * cr - a_i * ci + er, a_r * ci + a_i * cr + ei
        carry_ref[0:1, re] = cr
        carry_ref[0:1, im] = ci


def _ssm_fwd(u, b_half, c_half, pw, d_skip, side_blocks):
    s_len = u.shape[0]
    tb = SSM_BLOCK
    n = s_len // tb
    ns = len(side_blocks)
    side_in_specs, side_shapes, side_sems = _side_gather_specs(side_blocks)

    def body(*refs):
        u_ref, b_ref, c_ref, pw_ref, d_ref = refs[:5]
        side_ins = refs[5:5 + ns]
        y_ref, cm_ref = refs[5 + ns:7 + ns]
        side_outs = refs[7 + ns:7 + 2 * ns]
        s_ref, carry_ref, up_ref, yp_ref, stage_ref = refs[7 + 2 * ns:12 + 2 * ns]
        side = _side_gather_copies(side_ins, side_outs, *refs[12 + 2 * ns:])
        i = pl.program_id(0)

        @pl.when(i == 0)
        def _():
            carry_ref[...] = jnp.zeros_like(carry_ref)
            for cp in side:
                cp.start()

        _rows_to_segments(u_ref, stage_ref, up_ref)
        u = up_ref[...]
        ub = u.astype(bf16)
        for half in range(N_HALF):
            s_ref[:, half * HALF_COLS:(half + 1) * HALF_COLS] = _dot(ub[:, half * HALF_W:(half + 1) * HALF_W], b_ref[half])
        _ssm_scan(s_ref, pw_ref, None, reverse=False, unroll=4)
        _ssm_carries(tb - SUBLANES, s_ref, pw_ref, carry_ref, cm_ref, reverse=False)
        _ssm_add_carry(s_ref, pw_ref, cm_ref, reverse=False)
        for half in range(N_HALF):
            cols = slice(half * HALF_W, (half + 1) * HALF_W)
            sb = s_ref[:, half * HALF_COLS:(half + 1) * HALF_COLS].astype(bf16)
            yp_ref[:, cols] = _dot(sb, c_ref[half]) + d_ref[:, cols] * u[:, cols]
        _rows_from_segments(yp_ref, stage_ref, y_ref)

        @pl.when(i == n - 1)
        def _():
            for cp in side:
                cp.wait()

    outs = pl.pallas_call(
        body, name="ssm_fwd", grid=(n,),
        in_specs=[_row_spec(tb, SSM_W), _const_spec((N_HALF, HALF_W, HALF_COLS)), _const_spec((N_HALF, HALF_COLS, HALF_W)),
                  _const_spec((SSM_SEG, 2 * N_STATE)), _const_spec((1, SSM_W))] + side_in_specs,
        out_specs=[_row_spec(tb, SSM_W), _row_spec(SUBLANES, 2 * N_STATE)] + side_in_specs,
        out_shape=[_sds((s_len, SSM_W), f32), _sds((n * SUBLANES, 2 * N_STATE), f32)] + side_shapes,
        scratch_shapes=[pltpu.VMEM((tb, 2 * N_STATE), f32), pltpu.VMEM((SUBLANES, 2 * N_STATE), f32),
                        pltpu.VMEM((tb, SSM_W), f32), pltpu.VMEM((tb, SSM_W), f32),
                        pltpu.VMEM((SSM_W // LANES, tb, LANES), f32)] + side_sems,
        compiler_params=_cparams(("arbitrary",)),
    )(u, b_half, c_half, pw, d_skip, *side_blocks)
    return outs[0], outs[1], outs[2:]


def _layer_norm_fwd(r, g, b):
    mu = jnp.mean(r, axis=-1, keepdims=True)
    var = jnp.mean(jnp.square(r - mu), axis=-1, keepdims=True)
    rstd = lax.rsqrt(var + LN_EPS)
    xhat = (r - mu) * rstd
    return xhat, rstd, xhat * g + b


def _layer_norm_bwd(dy, xhat, rstd, g):
    dxh = dy * g
    m1 = jnp.mean(dxh, axis=-1, keepdims=True)
    m2 = jnp.mean(dxh * xhat, axis=-1, keepdims=True)
    return rstd * (dxh - m1 - xhat * m2)


def _branch_outputs(ys_ref, ain_ref, o_ref, wglu_ref, wco_ref, wxo_ref):
    ysb = _gelu(ys_ref[...]).astype(bf16)
    glu = _dot(ysb, wglu_ref[...], NT)
    ga, sb = glu[:, :D_MODEL], jax.nn.sigmoid(glu[:, D_MODEL:])
    ya = _dot(ain_ref[...], wco_ref[...], NT)
    yc = _dot(o_ref[...], wxo_ref[...], NT)
    return ysb, ga, sb, ya, ga * sb, yc


def _mid_fwd(y_ssm, g, ain, ob, x, w_glu_t, w_co_t, w_xo_t, w_out, ln1_g, ln1_b):
    s_len = x.shape[0]
    tm = TOKEN_TILE
    n = s_len // tm

    def body(ys_ref, g_ref, ain_ref, o_ref, x_ref, wglu_ref, wco_ref, wxo_ref, wout_ref, lg_ref, lb_ref,
             ysbt_ref, mb_ref, xhat_ref, rstd_ref):
        ysb, _, _, ya, yb, yc = _branch_outputs(ys_ref, ain_ref, o_ref, wglu_ref, wco_ref, wxo_ref)
        ysbt_ref[...] = ysb.T
        gt = g_ref[...].astype(f32)
        merged = gt[:, :D_MODEL] * ya + gt[:, D_MODEL:2 * D_MODEL] * yb + gt[:, 2 * D_MODEL:] * yc
        mb = merged.astype(bf16)
        mb_ref[...] = mb
        r1 = ALPHA * x_ref[...] + _dot(mb, wout_ref[...])
        xhat, rstd, _ = _layer_norm_fwd(r1, lg_ref[...], lb_ref[...])
        xhat_ref[...] = xhat
        rstd_ref[...] = rstd

    row_cols = [(D_MODEL, bf16), (D_MODEL, f32), (1, f32)]
    return pl.pallas_call(
        body, name="mid_fwd", grid=(n,),
        in_specs=[_row_spec(tm, SSM_W), _row_spec(tm, GATE_COLS), _row_spec(tm, CONV_W), _row_spec(tm, XATTN_W),
                  _row_spec(tm, D_MODEL), _const_spec((2 * D_MODEL, SSM_W)), _const_spec((D_MODEL, CONV_W)),
                  _const_spec((D_MODEL, XATTN_W)), _const_spec((D_MODEL, D_MODEL)),
                  _const_spec((1, D_MODEL)), _const_spec((1, D_MODEL))],
        out_specs=[_col_spec(SSM_W, tm)] + [_row_spec(tm, c) for c, _ in row_cols],
        out_shape=[_sds((SSM_W, s_len), bf16)] + [_sds((s_len, c), dt) for c, dt in row_cols],
        compiler_params=_cparams(("parallel",)),
    )(y_ssm, g, ain, ob, x, w_glu_t, w_co_t, w_xo_t, w_out, ln1_g, ln1_b)


def _mlp_fwd_bwd(xhat1, tgt, ln1_g, ln1_b, w_up_t, b_up, w_down, b_down, ln2_g, ln2_b):
    s_len = xhat1.shape[0]
    tm = TOKEN_TILE
    n = s_len // tm
    fc = 1024
    nfc = D_FF // fc

    def body(xh_ref, t_ref, l1g_ref, l1b_ref, wup_ref, bup_ref, wdn_ref, bdn_ref, l2g_ref, l2b_ref,
             x1bt_ref, hdn_ref, dr2bt_ref, dpre_ref, dx1_ref,
             loss_ref, dl2g_ref, dl2b_ref, dbdn_ref, dbup_ref, rl_ref):
        i = pl.program_id(0)

        @pl.when(i == 0)
        def _():
            loss_ref[...] = jnp.zeros_like(loss_ref)
            dl2g_ref[...] = jnp.zeros_like(dl2g_ref)
            dl2b_ref[...] = jnp.zeros_like(dl2b_ref)
            dbdn_ref[...] = jnp.zeros_like(dbdn_ref)
            dbup_ref[...] = jnp.zeros_like(dbup_ref)

        x1 = xh_ref[...] * l1g_ref[...] + l1b_ref[...]
        x1b = x1.astype(bf16)
        x1bt_ref[...] = x1b.T
        acc = jnp.zeros((tm, D_MODEL), f32)
        for c in range(nfc):
            cols = slice(c * fc, (c + 1) * fc)
            pre = _dot(x1b, wup_ref[cols, :], NT) + bup_ref[:, cols]
            rl = jnp.maximum(pre, 0.0)
            rl_ref[:, cols] = rl
            hb = (rl * rl).astype(bf16)
            hdn_ref[:, cols] = hb
            acc = acc + _dot(hb, wdn_ref[cols, :])
        r2 = ALPHA * x1 + acc + bdn_ref[...]
        xhat2, rstd2, y = _layer_norm_fwd(r2, l2g_ref[...], l2b_ref[...])
        err = y - t_ref[...]
        loss_ref[...] += jnp.sum(jnp.sum(err * err, axis=1, keepdims=True), axis=0, keepdims=True) * (0.5 / D_MODEL)
        dy = err * (1.0 / D_MODEL)
        dl2g_ref[...] += _colsum(dy * xhat2)
        dl2b_ref[...] += _colsum(dy)
        dr2 = _layer_norm_bwd(dy, xhat2, rstd2, l2g_ref[...])
        dbdn_ref[...] += _colsum(dr2)
        dr2b = dr2.astype(bf16)
        dr2bt_ref[...] = dr2b.T
        dacc = jnp.zeros((tm, D_MODEL), f32)
        for c in range(nfc):
            cols = slice(c * fc, (c + 1) * fc)
            dh = _dot(dr2b, wdn_ref[cols, :], NT)
            dpre = dh * (2.0 * rl_ref[:, cols])
            dbup_ref[:, cols] += _colsum(dpre)
            dpb = dpre.astype(bf16)
            dpre_ref[:, cols] = dpb
            dacc = dacc + _dot(dpb, wup_ref[cols, :])
        dx1_ref[...] = ALPHA * dr2 + dacc

    acc_shapes = [(1, LANES), (1, D_MODEL), (1, D_MODEL), (1, D_MODEL), (1, D_FF)]
    return pl.pallas_call(
        body, name="mlp_fwd_bwd", grid=(n,),
        in_specs=[_row_spec(tm, D_MODEL), _row_spec(tm, D_MODEL), _const_spec((1, D_MODEL)), _const_spec((1, D_MODEL)),
                  _const_spec((D_FF, D_MODEL)), _const_spec((1, D_FF)), _const_spec((D_FF, D_MODEL)),
                  _const_spec((1, D_MODEL)), _const_spec((1, D_MODEL)), _const_spec((1, D_MODEL))],
        out_specs=([_col_spec(D_MODEL, tm), _row_spec(tm, D_FF), _col_spec(D_MODEL, tm), _row_spec(tm, D_FF),
                    _row_spec(tm, D_MODEL)] + [_acc_spec(s) for s in acc_shapes]),
        out_shape=([_sds((D_MODEL, s_len), bf16), _sds((s_len, D_FF), bf16), _sds((D_MODEL, s_len), bf16),
                    _sds((s_len, D_FF), bf16), _sds((s_len, D_MODEL), f32)] + [_sds(s, f32) for s in acc_shapes]),
        scratch_shapes=[pltpu.VMEM((tm, D_FF), f32)],
        compiler_params=_cparams(("arbitrary",)),
    )(xhat1, tgt, ln1_g, ln1_b, w_up_t, b_up, w_down, b_down, ln2_g, ln2_b)


def _mid_bwd(dx1, xhat1, rstd1, g, ain, ob, y_ssm, ln1_g, w_out, w_glu_t, w_co_t, w_xo_t):
    s_len = dx1.shape[0]
    tm = TOKEN_TILE
    n = s_len // tm

    def body(dx1_ref, xh_ref, rs_ref, g_ref, ain_ref, o_ref, ys_ref, lg_ref, wout_ref, wglu_ref, wco_ref, wxo_ref,
             dxp_ref, dr1bt_ref, dgp_ref, dya_ref, dyc_ref, dglu_ref, dyssm_ref,
             dl1g_ref, dl1b_ref, dbg_ref):
        i = pl.program_id(0)

        @pl.when(i == 0)
        def _():
            dl1g_ref[...] = jnp.zeros_like(dl1g_ref)
            dl1b_ref[...] = jnp.zeros_like(dl1b_ref)
            dbg_ref[...] = jnp.zeros_like(dbg_ref)

        dx1 = dx1_ref[...]
        xhat = xh_ref[...]
        dl1g_ref[...] += _colsum(dx1 * xhat)
        dl1b_ref[...] += _colsum(dx1)
        dr1 = _layer_norm_bwd(dx1, xhat, rs_ref[...], lg_ref[...])
        dxp_ref[...] = ALPHA * dr1
        dr1b = dr1.astype(bf16)
        dr1bt_ref[...] = dr1b.T
        dm = _dot(dr1b, wout_ref[...], NT)

        _, ga, sb, ya, yb, yc = _branch_outputs(ys_ref, ain_ref, o_ref, wglu_ref, wco_ref, wxo_ref)
        gt = g_ref[...].astype(f32)
        branch = (ya, yb, yc)
        for j in range(3):
            cols = slice(j * D_MODEL, (j + 1) * D_MODEL)
            gj = gt[:, cols]
            dgp = dm * branch[j] * gj * (1.0 - gj)
            dbg_ref[:, cols] += _colsum(dgp)
            dgp_ref[:, cols] = dgp.astype(bf16)
        dya_ref[...] = (dm * gt[:, :D_MODEL]).astype(bf16)
        dyc_ref[...] = (dm * gt[:, 2 * D_MODEL:]).astype(bf16)
        dyb = dm * gt[:, D_MODEL:2 * D_MODEL]
        dga = (dyb * sb).astype(bf16)
        dgb = (dyb * ga * sb * (1.0 - sb)).astype(bf16)
        dglu_ref[:, :D_MODEL] = dga
        dglu_ref[:, D_MODEL:] = dgb
        dys = _dot(dga, wglu_ref[:D_MODEL, :]) + _dot(dgb, wglu_ref[D_MODEL:, :])
        dyssm_ref[...] = dys * _gelu_grad(ys_ref[...])

    row_cols = [(GATE_COLS, bf16), (D_MODEL, bf16), (D_MODEL, bf16), (2 * D_MODEL, bf16), (SSM_W, f32)]
    acc_shapes = [(1, D_MODEL), (1, D_MODEL), (1, GATE_COLS)]
    return pl.pallas_call(
        body, name="mid_bwd", grid=(n,),
        in_specs=[_row_spec(tm, D_MODEL), _row_spec(tm, D_MODEL), _row_spec(tm, 1), _row_spec(tm, GATE_COLS),
                  _row_spec(tm, CONV_W), _row_spec(tm, XATTN_W), _row_spec(tm, SSM_W),
                  _const_spec((1, D_MODEL)), _const_spec((D_MODEL, D_MODEL)), _const_spec((2 * D_MODEL, SSM_W)),
                  _const_spec((D_MODEL, CONV_W)), _const_spec((D_MODEL, XATTN_W))],
        out_specs=([_row_spec(tm, D_MODEL), _col_spec(D_MODEL, tm)] + [_row_spec(tm, c) for c, _ in row_cols]
                   + [_acc_spec(s) for s in acc_shapes]),
        out_shape=([_sds((s_len, D_MODEL), f32), _sds((D_MODEL, s_len), bf16)]
                   + [_sds((s_len, c), dt) for c, dt in row_cols] + [_sds(s, f32) for s in acc_shapes]),
        compiler_params=_cparams(("arbitrary",)),
    )(dx1, xhat1, rstd1, g, ain, ob, y_ssm, ln1_g, w_out, w_glu_t, w_co_t, w_xo_t)


def _ssm_bwd(u, dy, cm_all, b_half, c_half, pw, d_skip):
    s_len = u.shape[0]
    tb = SSM_BLOCK
    n = s_len // tb

    def body(u_ref, dy_ref, cm_ref, b_ref, c_ref, pw_ref, d_ref,
             du_ref, db_hbm, dc_hbm, da_ref, dd_ref,
             s_ref, g_ref, gcarry_ref, gcm_ref, db_ref, dc_ref, up_ref, dyp_ref, dup_ref, stage_ref):
        i = pl.program_id(0)

        @pl.when(i == 0)
        def _():
            gcarry_ref[...] = jnp.zeros_like(gcarry_ref)
            db_ref[...] = jnp.zeros_like(db_ref)
            dc_ref[...] = jnp.zeros_like(dc_ref)
            da_ref[...] = jnp.zeros_like(da_ref)
            dd_ref[...] = jnp.zeros_like(dd_ref)

        _rows_to_segments(u_ref, stage_ref, up_ref)
        _rows_to_segments(dy_ref, stage_ref, dyp_ref)
        u = up_ref[...]
        ub = u.astype(bf16)
        dy = dyp_ref[...]
        dyb = dy.astype(bf16)
        dd_ref[...] += _colsum(dy * u)

        for half in range(N_HALF):
            s_ref[:, half * HALF_COLS:(half + 1) * HALF_COLS] = _dot(ub[:, half * HALF_W:(half + 1) * HALF_W], b_ref[half])
        _ssm_scan(s_ref, pw_ref, cm_ref, reverse=False, unroll=True)

        for half in range(N_HALF):
            g_ref[:, half * HALF_COLS:(half + 1) * HALF_COLS] = _dot(dyb[:, half * HALF_W:(half + 1) * HALF_W], c_ref[half], NT)
        _ssm_scan(g_ref, pw_ref, None, reverse=True, unroll=True)
        _ssm_carries(0, g_ref, pw_ref, gcarry_ref, gcm_ref, reverse=True)
        _ssm_add_carry(g_ref, pw_ref, gcm_ref, reverse=True)

        for half in range(N_HALF):
            cols = slice(half * HALF_W, (half + 1) * HALF_W)
            scols = slice(half * HALF_COLS, (half + 1) * HALF_COLS)
            gb = g_ref[:, scols].astype(bf16)
            dup_ref[:, cols] = _dot(gb, b_ref[half], NT) + d_ref[:, cols] * dy[:, cols]
            db_ref[half] += _dot(ub[:, cols], gb, TN)
            dc_ref[half] += _dot(s_ref[:, scols].astype(bf16), dyb[:, cols], TN)
        _rows_from_segments(dup_ref, stage_ref, du_ref)

        for chunk in range(N_STATE // LANE_CHUNK):
            re, im = _state_cols(chunk)
            acc_r = da_ref[:, re]
            acc_i = da_ref[:, im]
            for k in range(SSM_SEG):
                rows = slice(k * SUBLANES, (k + 1) * SUBLANES)
                if k == 0:
                    pr, pi = cm_ref[:, re], cm_ref[:, im]
                else:
                    prev = slice((k - 1) * SUBLANES, k * SUBLANES)
                    pr, pi = s_ref[prev, re], s_ref[prev, im]
                gr, gi = g_ref[rows, re], g_ref[rows, im]
                acc_r = acc_r + (gr * pr + gi * pi)
                acc_i = acc_i + (gi * pr - gr * pi)
            da_ref[:, re] = acc_r
            da_ref[:, im] = acc_i

        @pl.when(i == n - 1)
        def _():
            pltpu.sync_copy(db_ref, db_hbm)
            pltpu.sync_copy(dc_ref, dc_hbm)

    rev = functools.partial(_row_spec, rev_n=n)
    any_spec = pl.BlockSpec(memory_space=pl.ANY)
    state_rows = pltpu.VMEM((tb, 2 * N_STATE), f32)
    seg_rows = pltpu.VMEM((SUBLANES, 2 * N_STATE), f32)
    tok_rows = pltpu.VMEM((tb, SSM_W), f32)
    return pl.pallas_call(
        body, name="ssm_bwd", grid=(n,),
        in_specs=[rev(tb, SSM_W), rev(tb, SSM_W), rev(SUBLANES, 2 * N_STATE),
                  _const_spec((N_HALF, HALF_W, HALF_COLS)), _const_spec((N_HALF, HALF_COLS, HALF_W)),
                  _const_spec((SSM_SEG, 2 * N_STATE)), _const_spec((1, SSM_W))],
        out_specs=[rev(tb, SSM_W), any_spec, any_spec, _acc_spec((SUBLANES, 2 * N_STATE)), _acc_spec((1, SSM_W))],
        out_shape=[_sds((s_len, SSM_W), f32), _sds((N_HALF, HALF_W, HALF_COLS), f32),
                   _sds((N_HALF, HALF_COLS, HALF_W), f32), _sds((SUBLANES, 2 * N_STATE), f32), _sds((1, SSM_W), f32)],
        scratch_shapes=[state_rows, state_rows, seg_rows, seg_rows,
                        pltpu.VMEM((N_HALF, HALF_W, HALF_COLS), f32), pltpu.VMEM((N_HALF, HALF_COLS, HALF_W), f32),
                        tok_rows, tok_rows, tok_rows, pltpu.VMEM((SSM_W // LANES, tb, LANES), f32)],
        compiler_params=_cparams(("arbitrary",)),
    )(u, dy, cm_all, b_half, c_half, pw, d_skip)


def _branch_bwd(dya, dyc, cin, q, kv, k_t, conv_w, w_co_t, w_xo_t, side_blocks):
    s_len = dya.shape[0]
    tm = TOKEN_TILE
    n = s_len // tm
    halo_blocks = tm // 8
    ns = len(side_blocks)
    conv_tile = _sds((8, CONV_W), f32)
    side_in_specs, side_shapes, side_sems = _side_gather_specs(list(side_blocks) + [conv_tile])

    def body(*refs):
        (dya_ref, dyc_ref, cin_ref, cprev_ref, q_ref, kv_ref, cw_ref, wco_ref, wxo_ref, kt_ref) = refs[:10]
        side_ins = refs[10:10 + ns]
        dconv_ref, dq_ref, dkv_ref = refs[10 + ns:13 + ns]
        side_outs = refs[13 + ns:14 + 2 * ns]
        zs_ref, dczs_ref, dcw_ref = refs[14 + 2 * ns:17 + 2 * ns]
        copies = _side_gather_copies(list(side_ins) + [dcw_ref], side_outs, *refs[17 + 2 * ns:])
        side, conv_side = copies[:ns * N_DEV], copies[ns * N_DEV:]
        i = pl.program_id(0)
        tile = n - 1 - i

        @pl.when(i == 0)
        def _():
            dcw_ref[...] = jnp.zeros_like(dcw_ref)
            dkv_ref[...] = jnp.zeros_like(dkv_ref)
            dczs_ref[tm:tm + 8, :] = jnp.zeros((8, CONV_W), f32)
            for cp in side:
                cp.start()

        cin = cin_ref[...]
        cb, cc, ch = cin[:, :CONV_W], cin[:, CONV_W:2 * CONV_W], cin[:, 2 * CONV_W:]
        z = cc * ch
        cprev = cprev_ref[...]
        zprev = cprev[:, CONV_W:2 * CONV_W] * cprev[:, 2 * CONV_W:]
        zs_ref[0:8, :] = jnp.where(tile == 0, 0.0, zprev)
        zs_ref[8:8 + tm, :] = z
        z1 = zs_ref[pl.ds(7, tm), :]
        z2 = zs_ref[pl.ds(6, tm), :]
        cw = cw_ref[...]
        cz = cw[0:1] * z2 + cw[1:2] * z1 + cw[2:3] * z

        dain = _dot(dya_ref[...], wco_ref[...])
        dcb = dain * cz
        dcz = dain * cb
        dczs_ref[0:tm, :] = dcz
        dcz1 = dczs_ref[pl.ds(1, tm), :]
        dcz2 = dczs_ref[pl.ds(2, tm), :]
        dz = cw[2:3] * dcz + cw[1:2] * dcz1 + cw[0:1] * dcz2
        dczs_ref[tm:tm + 8, :] = dczs_ref[0:8, :]
        dcw_ref[0:1, :] += _colsum(dcz * z2)
        dcw_ref[1:2, :] += _colsum(dcz * z1)
        dcw_ref[2:3, :] += _colsum(dcz * z)
        dconv_ref[:, :CONV_W] = dcb.astype(bf16)
        dconv_ref[:, CONV_W:2 * CONV_W] = (dz * ch).astype(bf16)
        dconv_ref[:, 2 * CONV_W:] = (dz * cc).astype(bf16)

        qb = q_ref[...]
        dob = _dot(dyc_ref[...], wxo_ref[...]).astype(bf16)
        kv = kv_ref[...]
        heads = range(HEADS)
        hcs = [slice(h * HEAD_DIM, (h + 1) * HEAD_DIM) for h in heads]
        vcs = [slice(XATTN_W + h * HEAD_DIM, XATTN_W + (h + 1) * HEAD_DIM) for h in heads]
        s_t = [_dot(kv[:, hcs[h]], qb[:, hcs[h]], NT) * (HEAD_DIM ** -0.5) for h in heads]
        dp_t = [_dot(kv[:, vcs[h]], dob[:, hcs[h]], NT) for h in heads]
        e_t = [jnp.exp(s_t[h] - jnp.max(s_t[h], axis=0, keepdims=True)) for h in heads]
        p_t = [e_t[h] / jnp.sum(e_t[h], axis=0, keepdims=True) for h in heads]
        dv = [_dot(p_t[h].astype(bf16), dob[:, hcs[h]]) for h in heads]
        ds_t = [(p_t[h] * (dp_t[h] - jnp.sum(dp_t[h] * p_t[h], axis=0, keepdims=True)) * (HEAD_DIM ** -0.5)).astype(bf16)
                for h in heads]
        dk = [_dot(ds_t[h], qb[:, hcs[h]]) for h in heads]
        dq_t = [_dot(kt_ref[hcs[h], :], ds_t[h]) for h in heads]
        dq_ref[...] = jnp.concatenate(dq_t, axis=0).T.astype(bf16)
        dkv_ref[...] += jnp.concatenate(dk + dv, axis=1)

        @pl.when(i == n - 1)
        def _():
            for cp in conv_side:
                cp.start()
            for cp in side + conv_side:
                cp.wait()

    rev = functools.partial(_row_spec, rev_n=n)
    prev_spec = pl.BlockSpec((8, 3 * CONV_W), lambda i: (jnp.maximum((n - 1 - i) * halo_blocks - 1, 0), 0))
    outs = pl.pallas_call(
        body, name="branch_bwd", grid=(n,),
        in_specs=[rev(tm, D_MODEL), rev(tm, D_MODEL), rev(tm, 3 * CONV_W), prev_spec, rev(tm, XATTN_W),
                  _const_spec((MEM_LEN, 2 * XATTN_W)), _const_spec((3, CONV_W)), _const_spec((D_MODEL, CONV_W)),
                  _const_spec((D_MODEL, XATTN_W)), _const_spec((XATTN_W, MEM_LEN))] + side_in_specs[:ns],
        out_specs=[rev(tm, 3 * CONV_W), rev(tm, XATTN_W), _acc_spec((MEM_LEN, 2 * XATTN_W))] + side_in_specs,
        out_shape=[_sds((s_len, 3 * CONV_W), bf16), _sds((s_len, XATTN_W), bf16),
                   _sds((MEM_LEN, 2 * XATTN_W), f32)] + side_shapes,
        scratch_shapes=[pltpu.VMEM((tm + 8, CONV_W), f32), pltpu.VMEM((tm + 8, CONV_W), f32),
                        pltpu.VMEM((8, CONV_W), f32)] + side_sems,
        compiler_params=_cparams(("arbitrary",)),
    )(dya, dyc, cin, cin, q, kv, conv_w, w_co_t, w_xo_t, k_t, *side_blocks)
    return outs[0], outs[1], outs[2], outs[3:]


def _in_proj_bwd(dgp, dconv, du, dq, dxp, w_in_t):
    s_len = dgp.shape[0]
    tm = TOKEN_TILE
    n = s_len // tm

    def body(dgp_ref, dconv_ref, du_ref, dq_ref, dxp_ref, win_ref, dx_ref, dproj_ref):
        dproj = jnp.concatenate([dgp_ref[...], dconv_ref[...], du_ref[...].astype(bf16), dq_ref[...]], axis=1)
        dproj_ref[...] = dproj
        dx_ref[...] = dxp_ref[...] + _dot(dproj, win_ref[...])

    return pl.pallas_call(
        body, name="in_proj_bwd", grid=(n,),
        in_specs=[_row_spec(tm, GATE_COLS), _row_spec(tm, 3 * CONV_W), _row_spec(tm, SSM_W), _row_spec(tm, XATTN_W),
                  _row_spec(tm, D_MODEL), _const_spec((IN_COLS, D_MODEL))],
        out_specs=[_row_spec(tm, D_MODEL), _row_spec(tm, IN_COLS)],
        out_shape=[_sds((s_len, D_MODEL), f32), _sds((s_len, IN_COLS), bf16)],
        compiler_params=_cparams(("parallel",)),
    )(dgp, dconv, du, dq, dxp, w_in_t)


N_CHIP = 4
CHIP_STEPS = [(1, 1), (1, 0), (0, 1), (0, 0)]


def _flip(v, d):
    return 1 - v if d else v


def _chip_order():
    x, y, _ = _mesh_place()
    return jnp.stack([2 * _flip(x, dx) + _flip(y, dy) for dx, dy in CHIP_STEPS]).astype(jnp.int32)


def _weight_grad_scatter(a_t, b, name, tm, tt):
    m, s_len = a_t.shape
    n_cols = b.shape[1]
    w = n_cols // N_DEV
    tn = 2 * w
    tm, tt = min(tm, m), min(tt, s_len)
    nm, nt = m // tm, s_len // tt
    assert m % tm == 0 and s_len % tt == 0

    def body(order_ref, a_ref, b_ref, recv_ref, acc_ref, send_ref, sib_ref, stash_ref,
             d2d_send, d2d_recv, ici_send, ici_recv, local_sem):
        del order_ref
        q, im, t = pl.program_id(0), pl.program_id(1), pl.program_id(2)
        x, y, c = _mesh_place()
        mesh_id = pl.DeviceIdType.MESH

        @pl.when(t == 0)
        def _():
            acc_ref[...] = jnp.zeros_like(acc_ref)

        acc_ref[...] += _dot(a_ref[...], b_ref[...])

        def to_sibling(qq, imm):
            rows = pl.ds(pl.multiple_of(imm * tm, tm), tm)
            return pltpu.make_async_remote_copy(
                src_ref=send_ref.at[qq, 0, rows, :], dst_ref=sib_ref.at[qq, rows, :],
                send_sem=d2d_send.at[qq], recv_sem=d2d_recv.at[qq, imm],
                device_id=(x, y, 1 - c), device_id_type=mesh_id)

        def finish_tile(qq, imm):
            rows = pl.ds(pl.multiple_of(imm * tm, tm), tm)
            to_sibling(qq, imm).wait_recv()
            both = stash_ref[...] + sib_ref[qq, rows, :].astype(f32)
            send_ref[qq, 1, rows, :] = both.astype(bf16)
            for step, (dx, dy) in enumerate(CHIP_STEPS):
                @pl.when(qq == step)
                def _(step=step, dx=dx, dy=dy):
                    src, dst = send_ref.at[step, 1, rows, :], recv_ref.at[step, rows, :]
                    if dx or dy:
                        pltpu.make_async_remote_copy(
                            src_ref=src, dst_ref=dst, send_sem=ici_send.at[step], recv_sem=ici_recv.at[step],
                            device_id=(_flip(x, dx), _flip(y, dy), c), device_id_type=mesh_id).start()
                    else:
                        pltpu.make_async_copy(src, dst, local_sem).start()

        @pl.when(t == nt - 1)
        def _():
            tile = q * nm + im

            @pl.when(tile > 0)
            def _():
                finish_tile((tile - 1) // nm, (tile - 1) % nm)

            rows = pl.ds(pl.multiple_of(im * tm, tm), tm)
            for core in (0, 1):
                @pl.when(c == core)
                def _(core=core):
                    other = 1 - core
                    send_ref[q, 0, rows, :] = acc_ref[:, other * w:(other + 1) * w].astype(bf16)
                    stash_ref[...] = acc_ref[:, core * w:(core + 1) * w]
            to_sibling(q, im).start()

            @pl.when(tile == N_CHIP * nm - 1)
            def _():
                finish_tile(q, im)

        @pl.when((q == N_CHIP - 1) & (im == nm - 1) & (t == nt - 1))
        def _():
            for step, (dx, dy) in enumerate(CHIP_STEPS):
                pltpu.make_async_remote_copy(
                    src_ref=send_ref.at[step, 0], dst_ref=sib_ref.at[step],
                    send_sem=d2d_send.at[step], recv_sem=d2d_recv.at[step, 0],
                    device_id=(x, y, 1 - c), device_id_type=mesh_id).wait_send()
                src, dst = send_ref.at[step, 1], recv_ref.at[step]
                if dx or dy:
                    pltpu.make_async_remote_copy(
                        src_ref=src, dst_ref=dst, send_sem=ici_send.at[step], recv_sem=ici_recv.at[step],
                        device_id=(_flip(x, dx), _flip(y, dy), c), device_id_type=mesh_id).wait()
                else:
                    pltpu.make_async_copy(src, dst, local_sem).wait()

    grid_spec = pltpu.PrefetchScalarGridSpec(
        num_scalar_prefetch=1, grid=(N_CHIP, nm, nt),
        in_specs=[pl.BlockSpec((tm, tt), lambda q, im, t, order: (im, t)),
                  pl.BlockSpec((tt, tn), lambda q, im, t, order: (t, order[q]))],
        out_specs=pl.BlockSpec(memory_space=pl.ANY),
        scratch_shapes=[pltpu.VMEM((tm, tn), f32), pltpu.VMEM((N_CHIP, 2, m, w), bf16), pltpu.VMEM((N_CHIP, m, w), bf16),
                        pltpu.VMEM((tm, w), f32),
                        pltpu.SemaphoreType.DMA((N_CHIP,)), pltpu.SemaphoreType.DMA((N_CHIP, nm)),
                        pltpu.SemaphoreType.DMA((N_CHIP - 1,)), pltpu.SemaphoreType.DMA((N_CHIP - 1,)),
                        pltpu.SemaphoreType.DMA])
    return pl.pallas_call(
        body, name=name, grid_spec=grid_spec,
        out_shape=_sds((N_CHIP, m, w), bf16),
        compiler_params=_cparams(("arbitrary", "arbitrary", "arbitrary")),
    )(_chip_order(), a_t, b)


def _adamw(w, g, m, v):
    m = ADAM_B1 * m + (1.0 - ADAM_B1) * g
    v = ADAM_B2 * v + (1.0 - ADAM_B2) * jnp.square(g)
    m_hat = m / (1.0 - ADAM_B1 ** ADAM_STEP)
    v_hat = v / (1.0 - ADAM_B2 ** ADAM_STEP)
    delta = -ADAM_LR * (m_hat / (jnp.sqrt(v_hat) + ADAM_EPS) + ADAM_WD * w)
    return delta, m, v


def _sum_parts(p_ref):
    g = p_ref[0].astype(f32)
    for j in range(1, p_ref.shape[0]):
        g = g + p_ref[j].astype(f32)
    return g


def _adamw_update(w, m, v, parts, name, transposed):
    rows, cols = w.shape
    n_parts = parts.shape[0]
    if transposed:
        tr = LANES
        p_spec = pl.BlockSpec((n_parts, cols, tr), lambda i: (0, 0, i))
    else:
        tr = next(t for t in (256, 128, 64, 32, 16, 8) if rows % t == 0)
        p_spec = pl.BlockSpec((n_parts, tr, cols), lambda i: (0, i, 0))
    spec = pl.BlockSpec((tr, cols), lambda i: (i, 0))

    def body(w_ref, p_ref, m_ref, v_ref, g_ref, d_ref, nm_ref, nv_ref):
        g = _sum_parts(p_ref)
        if transposed:
            g = g.T
        g_ref[...] = g
        d_ref[...], nm_ref[...], nv_ref[...] = _adamw(w_ref[...], g, m_ref[...], v_ref[...])

    return pl.pallas_call(
        body, name=name, grid=(rows // tr,),
        in_specs=[spec, p_spec, spec, spec], out_specs=[spec] * 4,
        out_shape=[_sds((rows, cols), f32)] * 4,
        compiler_params=_cparams(("parallel",)),
    )(w, parts, m, v)


SMALL_GROUPS = [
    (["b_gate", "ln1_g", "ln1_b", "b_up", "b_down", "ln2_g", "ln2_b", "ssm_d"], 1),
    (["ssm_lam_re", "ssm_lam_im", "ssm_c_re", "ssm_c_im"], 0),
    (["ssm_b_re", "ssm_b_im"], 0),
    (["conv_w"], 0),
    (["ssm_log_dt"], 0),
]


def _sum_small(group_parts):
    def body(*refs):
        n = len(refs) // 2
        for p_ref, o_ref in zip(refs[:n], refs[n:]):
            o_ref[...] = _sum_parts(p_ref)

    return pl.pallas_call(
        body, name="sum_small",
        out_shape=[_sds(p.shape[1:], f32) for p in group_parts],
        compiler_params=_cparams(),
    )(*group_parts)


def _adamw_small(ws, ms, vs, group_sums):
    names = [k for group, _ in SMALL_GROUPS for k in group]
    n = len(names)

    def body(*refs):
        w_refs, m_refs, v_refs = (dict(zip(names, refs[j * n:(j + 1) * n])) for j in range(3))
        p_refs = refs[3 * n:3 * n + len(SMALL_GROUPS)]
        out_refs = [dict(zip(names, refs[3 * n + len(SMALL_GROUPS) + j * n:][:n])) for j in range(4)]
        for (group, axis), p_ref in zip(SMALL_GROUPS, p_refs):
            total = p_ref[...]
            off = 0
            for k in group:
                size = SMALL[k][axis]
                g = total[:, off:off + size] if axis == 1 else total[off:off + size, :]
                off += size
                d, nm, nv = _adamw(w_refs[k][...], g, m_refs[k][...], v_refs[k][...])
                for j, val in enumerate((g, d, nm, nv)):
                    out_refs[j][k][...] = val

    res = pl.pallas_call(
        body, name="adamw_small",
        out_shape=[_sds(SMALL[k], f32) for _ in range(4) for k in names],
        compiler_params=_cparams(),
    )(*[ws[k] for k in names], *[ms[k] for k in names], *[vs[k] for k in names], *group_sums)
    return [dict(zip(names, res[j * n:(j + 1) * n])) for j in range(4)]


def _ssm_discretize(lam_re, lam_im, log_dt, b_re, b_im):
    dt = jnp.exp(log_dt)[:, None]
    mag = jnp.exp(lam_re * dt)
    abar_r = mag * jnp.cos(lam_im * dt)
    abar_i = mag * jnp.sin(lam_im * dt)
    den = lam_re * lam_re + lam_im * lam_im
    nr = abar_r - 1.0
    ni = abar_i
    kr = (nr * lam_re + ni * lam_im) / den
    ki = (ni * lam_re - nr * lam_im) / den
    bbar_r = kr[..., None] * b_re - ki[..., None] * b_im
    bbar_i = kr[..., None] * b_im + ki[..., None] * b_re
    return abar_r, abar_i, bbar_r, bbar_i


def _state_layout(re, im):
    parts = []
    for half in range(N_HALF):
        cols = slice(half * HALF_STATE, (half + 1) * HALF_STATE)
        parts += [re[..., cols], im[..., cols]]
    return jnp.concatenate(parts, axis=-1)


def _state_unlayout(a):
    re = jnp.concatenate([a[..., _half_cols(h)[0]] for h in range(N_HALF)], axis=-1)
    im = jnp.concatenate([a[..., _half_cols(h)[1]] for h in range(N_HALF)], axis=-1)
    return re, im


def _abar_powers(abar_r, abar_i):
    pr, pi = abar_r.reshape(1, N_STATE), abar_i.reshape(1, N_STATE)
    while pr.shape[0] < SSM_SEG:
        tr, ti = pr[-1:], pi[-1:]
        pr, pi = (jnp.concatenate([pr, pr * tr - pi * ti], axis=0), jnp.concatenate([pi, pr * ti + pi * tr], axis=0))
    return _state_layout(pr, pi)


HALF_GROUPS = SSM_GROUPS // N_HALF


def _half_block_diag(blocks):
    _, r, c = blocks.shape
    eye = jnp.eye(HALF_GROUPS, dtype=blocks.dtype)
    b4 = blocks.reshape(N_HALF, HALF_GROUPS, r, c)
    return jnp.einsum("ngrc,gk->ngrkc", b4, eye).reshape(N_HALF, HALF_GROUPS * r, HALF_GROUPS * c)


def _half_diag_blocks(mat, r, c):
    eye = jnp.eye(HALF_GROUPS, dtype=mat.dtype)
    m5 = mat.reshape(N_HALF, HALF_GROUPS, r, HALF_GROUPS, c)
    return jnp.einsum("ngrkc,gk->ngrc", m5, eye).reshape(SSM_GROUPS, r, c)


BIG = ["w_in", "w_conv_out", "w_glu", "w_kv", "w_xattn_out", "w_out", "w_up", "w_down"]
COL_SHARDED = ["w_in", "w_conv_out", "w_glu", "w_xattn_out", "w_up"]
SMALL = {"b_gate": (1, GATE_COLS), "conv_w": (3, CONV_W), "ssm_lam_re": (SSM_GROUPS, SSM_STATE),
         "ssm_lam_im": (SSM_GROUPS, SSM_STATE), "ssm_log_dt": (1, SSM_GROUPS),
         "ssm_b_re": (N_STATE, SSM_GROUP), "ssm_b_im": (N_STATE, SSM_GROUP),
         "ssm_c_re": (SSM_W, SSM_STATE), "ssm_c_im": (SSM_W, SSM_STATE), "ssm_d": (1, SSM_W),
         "ln1_g": (1, D_MODEL), "ln1_b": (1, D_MODEL), "b_up": (1, D_FF), "b_down": (1, D_MODEL),
         "ln2_g": (1, D_MODEL), "ln2_b": (1, D_MODEL)}
WEIGHTS = ["w_in", "b_gate", "conv_w", "w_conv_out", "ssm_lam_re", "ssm_lam_im", "ssm_log_dt", "ssm_b_re", "ssm_b_im",
           "ssm_c_re", "ssm_c_im", "ssm_d", "w_glu", "w_kv", "w_xattn_out", "w_out", "ln1_g", "ln1_b", "w_up", "b_up",
           "w_down", "b_down", "ln2_g", "ln2_b"]


def _local_step(x, mem, tgt, full, late, small):
    lam_re, lam_im, log_dt = small["ssm_lam_re"], small["ssm_lam_im"], small["ssm_log_dt"].reshape(SSM_GROUPS)
    b_shape = (SSM_GROUPS, SSM_STATE, SSM_GROUP)
    c_shape = (SSM_GROUPS, SSM_GROUP, SSM_STATE)
    disc, disc_vjp = jax.vjp(_ssm_discretize, lam_re, lam_im, log_dt,
                             small["ssm_b_re"].reshape(b_shape), small["ssm_b_im"].reshape(b_shape))
    abar_r, abar_i, bbar_r, bbar_i = disc
    pw = _abar_powers(abar_r, abar_i)
    c_re, c_im = small["ssm_c_re"].reshape(c_shape), small["ssm_c_im"].reshape(c_shape)
    b_half = jnp.concatenate([_half_block_diag(bbar_r.transpose(0, 2, 1)), _half_block_diag(bbar_i.transpose(0, 2, 1))],
                             axis=2).astype(bf16)
    c_half = jnp.concatenate([_half_block_diag(c_re.transpose(0, 2, 1)), -_half_block_diag(c_im.transpose(0, 2, 1))],
                             axis=1).astype(bf16)

    s_len = x.shape[0]
    stack = lambda a: a.reshape(-1, a.shape[-1])
    kv, k_t, memb = _kv_proj(mem, full["w_kv"])
    (xbt, g, cin, u, q, ain, ob, aint, obt), side = _in_proj(
        x, full["w_in"], small["b_gate"], small["conv_w"], kv,
        [late[k] for k in ("w_glu", "w_conv_out", "w_xattn_out", "w_out", "w_up")])
    w_glu_t, w_co_t, w_xo_t, w_out, w_up_t = (stack(a) for a in side)
    y_ssm, cm_all, side = _ssm_fwd(u, b_half, c_half, pw, small["ssm_d"], [late["w_down"]])
    w_down = stack(side[0])
    ysbt, mb, xhat1, rstd1 = _mid_fwd(y_ssm, g, ain, ob, x, w_glu_t, w_co_t, w_xo_t, w_out,
                                      small["ln1_g"], small["ln1_b"])
    (x1bt, hdn, dr2bt, dpre, dx1, loss, dl2g, dl2b, dbdn, dbup) = _mlp_fwd_bwd(
        xhat1, tgt, small["ln1_g"], small["ln1_b"], w_up_t, small["b_up"], w_down,
        small["b_down"], small["ln2_g"], small["ln2_b"])
    recv = {}
    recv["w_down"] = _weight_grad_scatter(dr2bt, hdn, "dw_down", tm=512, tt=2048)
    recv["w_up"] = _weight_grad_scatter(x1bt, dpre, "dw_up", tm=512, tt=2048)
    (dxp, dr1bt, dgp, dya, dyc, dglu, dyssm, dl1g, dl1b, dbg) = _mid_bwd(
        dx1, xhat1, rstd1, g, ain, ob, y_ssm, small["ln1_g"], w_out, w_glu_t, w_co_t, w_xo_t)
    recv["w_out"] = _weight_grad_scatter(dr1bt, mb, "dw_out", tm=512, tt=s_len)
    recv["w_glu"] = _weight_grad_scatter(ysbt, dglu, "dw_glu", tm=512, tt=s_len)
    du, db_half, dc_half, da8, dd = _ssm_bwd(u, dyssm, cm_all, b_half, c_half, pw, small["ssm_d"])
    dabar_r, dabar_i = _state_unlayout(jnp.sum(da8, axis=0))
    dbbar_r = _half_diag_blocks(db_half[:, :, :HALF_STATE], SSM_GROUP, SSM_STATE).transpose(0, 2, 1)
    dbbar_i = _half_diag_blocks(db_half[:, :, HALF_STATE:], SSM_GROUP, SSM_STATE).transpose(0, 2, 1)
    g_shape = (SSM_GROUPS, SSM_STATE)
    dlam_re, dlam_im, dlog_dt, db_re, db_im = disc_vjp(
        (dabar_r.reshape(g_shape), dabar_i.reshape(g_shape), dbbar_r, dbbar_i))
    dc_re = _half_diag_blocks(dc_half[:, :HALF_STATE, :], SSM_STATE, SSM_GROUP).transpose(0, 2, 1)
    dc_im = -_half_diag_blocks(dc_half[:, HALF_STATE:, :], SSM_STATE, SSM_GROUP).transpose(0, 2, 1)

    small_grads = {
        "b_gate": dbg, "ssm_lam_re": dlam_re, "ssm_lam_im": dlam_im, "ssm_log_dt": dlog_dt,
        "ssm_b_re": db_re, "ssm_b_im": db_im, "ssm_c_re": dc_re, "ssm_c_im": dc_im, "ssm_d": dd,
        "ln1_g": dl1g, "ln1_b": dl1b, "b_up": dbup, "b_down": dbdn, "ln2_g": dl2g, "ln2_b": dl2b,
    }
    small_grads = {k: a.reshape(SMALL[k]) for k, a in small_grads.items()}
    groups = [(group, axis) for group, axis in SMALL_GROUPS if group != ["conv_w"]]
    stacks = [jnp.concatenate([small_grads[k] for k in group], axis=axis) if len(group) > 1 else small_grads[group[0]]
              for group, axis in groups]
    dense = lambda a: a.reshape(-1, LANES) if a.size % LANES == 0 else a
    dconv, dq, dkv, group_parts = _branch_bwd(dya, dyc, cin, q, kv, k_t, small["conv_w"], w_co_t, w_xo_t,
                                              [dense(a) for a in stacks])
    recv["w_conv_out"] = _weight_grad_scatter(aint, dya, "dw_conv_out", tm=512, tt=s_len)
    recv["w_xattn_out"] = _weight_grad_scatter(obt, dyc, "dw_xattn_out", tm=512, tt=s_len)
    recv["w_kv"] = _weight_grad_scatter(dkv.T.astype(bf16), memb, "dw_kv", tm=D_MODEL, tt=MEM_LEN)
    dx, dproj = _in_proj_bwd(dgp, dconv, du, dq, dxp, full["w_in"])
    recv["w_in"] = _weight_grad_scatter(xbt, dproj, "dw_in", tm=512, tt=2048)
    sums = _sum_small(group_parts)
    group_sums = dict(zip([tuple(group) for group, _ in groups], [s.reshape(a.shape) for s, a in zip(sums, stacks)]))
    group_sums[("conv_w",)] = sums[-1][0:3]
    return loss[0, 0], dx, recv, [group_sums[tuple(group)] for group, _ in SMALL_GROUPS]


def kernel(x, mem, w_in, b_gate, conv_w, w_conv_out, ssm_lam_re, ssm_lam_im, ssm_log_dt, ssm_b_re, ssm_b_im, ssm_c_re, ssm_c_im, ssm_d, w_glu, w_kv, w_xattn_out, w_out, ln1_g, ln1_b, w_up, b_up, w_down, b_down, ln2_g, ln2_b, loss_target, m_w_in, m_b_gate, m_conv_w, m_w_conv_out, m_ssm_lam_re, m_ssm_lam_im, m_ssm_log_dt, m_ssm_b_re, m_ssm_b_im, m_ssm_c_re, m_ssm_c_im, m_ssm_d, m_w_glu, m_w_kv, m_w_xattn_out, m_w_out, m_ln1_g, m_ln1_b, m_w_up, m_b_up, m_w_down, m_b_down, m_ln2_g, m_ln2_b, v_w_in, v_b_gate, v_conv_w, v_w_conv_out, v_ssm_lam_re, v_ssm_lam_im, v_ssm_log_dt, v_ssm_b_re, v_ssm_b_im, v_ssm_c_re, v_ssm_c_im, v_ssm_d, v_w_glu, v_w_kv, v_w_xattn_out, v_w_out, v_ln1_g, v_ln1_b, v_w_up, v_b_up, v_w_down, v_b_down, v_ln2_g, v_ln2_b):
    w = dict(w_in=w_in, b_gate=b_gate, conv_w=conv_w, w_conv_out=w_conv_out, ssm_lam_re=ssm_lam_re,
             ssm_lam_im=ssm_lam_im, ssm_log_dt=ssm_log_dt, ssm_b_re=ssm_b_re, ssm_b_im=ssm_b_im, ssm_c_re=ssm_c_re,
             ssm_c_im=ssm_c_im, ssm_d=ssm_d, w_glu=w_glu, w_kv=w_kv, w_xattn_out=w_xattn_out, w_out=w_out,
             ln1_g=ln1_g, ln1_b=ln1_b, w_up=w_up, b_up=b_up, w_down=w_down, b_down=b_down, ln2_g=ln2_g, ln2_b=ln2_b)
    m = dict(w_in=m_w_in, b_gate=m_b_gate, conv_w=m_conv_w, w_conv_out=m_w_conv_out, ssm_lam_re=m_ssm_lam_re,
             ssm_lam_im=m_ssm_lam_im, ssm_log_dt=m_ssm_log_dt, ssm_b_re=m_ssm_b_re, ssm_b_im=m_ssm_b_im,
             ssm_c_re=m_ssm_c_re, ssm_c_im=m_ssm_c_im, ssm_d=m_ssm_d, w_glu=m_w_glu, w_kv=m_w_kv,
             w_xattn_out=m_w_xattn_out, w_out=m_w_out, ln1_g=m_ln1_g, ln1_b=m_ln1_b, w_up=m_w_up, b_up=m_b_up,
             w_down=m_w_down, b_down=m_b_down, ln2_g=m_ln2_g, ln2_b=m_ln2_b)
    v = dict(w_in=v_w_in, b_gate=v_b_gate, conv_w=v_conv_w, w_conv_out=v_w_conv_out, ssm_lam_re=v_ssm_lam_re,
             ssm_lam_im=v_ssm_lam_im, ssm_log_dt=v_ssm_log_dt, ssm_b_re=v_ssm_b_re, ssm_b_im=v_ssm_b_im,
             ssm_c_re=v_ssm_c_re, ssm_c_im=v_ssm_c_im, ssm_d=v_ssm_d, w_glu=v_w_glu, w_kv=v_w_kv,
             w_xattn_out=v_w_xattn_out, w_out=v_w_out, ln1_g=v_ln1_g, ln1_b=v_ln1_b, w_up=v_w_up, b_up=v_b_up,
             w_down=v_w_down, b_down=v_b_down, ln2_g=v_ln2_g, ln2_b=v_ln2_b)
    out_shapes = {k: a.shape for k, a in w.items()}
    shard2d = lambda k, a: a.reshape((3, CONV_W // N_DEV) if k == "conv_w" else SMALL[k]) if k in SMALL else a[0]
    w, m, v = ({k: shard2d(k, a) for k, a in d.items()} for d in (w, m, v))

    shards = {k: w[k].T.astype(bf16) if k in COL_SHARDED else w[k].astype(bf16) for k in BIG}
    conv_pad = jnp.pad(w["conv_w"], ((0, 5), (0, LANES - CONV_W // N_DEV)))
    early = ["w_in", "w_kv"]
    gathered = _all_gather([shards[k] for k in early] + [conv_pad], "gather_weights")
    full = {k: a.reshape(-1, a.shape[-1]) for k, a in zip(early, gathered[:-1])}
    late = {k: shards[k] for k in BIG if k not in early}
    conv_full = gathered[-1][:, :3, :CONV_W // N_DEV].transpose(1, 0, 2).reshape(3, CONV_W)
    small = {k: (conv_full if k == "conv_w" else w[k]) for k in SMALL}

    loss, dx, recv, group_sums = _local_step(x[0], mem[0], loss_target[0], full, late, small)

    grads, deltas, new_m, new_v = {}, {}, {}, {}
    for k in BIG:
        res = _adamw_update(w[k], m[k], v[k], recv[k], "adamw_" + k, transposed=k not in COL_SHARDED)
        grads[k], deltas[k], new_m[k], new_v[k] = res

    widen = lambda k, a: jnp.tile(a, (1, N_DEV)) if k == "conv_w" else a
    res = _adamw_small(small, {k: widen(k, m[k]) for k in SMALL}, {k: widen(k, v[k]) for k in SMALL}, group_sums)
    dev = _slot(_mesh_place())
    for d, small_res in zip((grads, deltas, new_m, new_v), res):
        for k, a in small_res.items():
            if k == "conv_w":
                a = lax.dynamic_slice_in_dim(a, dev * (CONV_W // N_DEV), CONV_W // N_DEV, axis=1)
            d[k] = a

    loss = lax.psum(loss, ("x", "y", "c"))
    outs = [loss, dx[None]]
    for d in (grads, deltas, new_m, new_v):
        outs += [d[k].reshape(out_shapes[k]) for k in WEIGHTS]
    return tuple(outs)
```

```python
import functools
import math

import jax
import jax.numpy as jnp
from jax import lax
from jax.experimental import pallas as pl
from jax.experimental.pallas import tpu as pltpu

f32 = jnp.float32
bf16 = jnp.bfloat16

D_MODEL = 1024
MEM_LEN = 256
GATE_COLS = 3 * D_MODEL
CONV_W = 512
SSM_W = 512
XATTN_W = 512
HEADS = 4
HEAD_DIM = 128
D_FF = 4096
IN_COLS = GATE_COLS + 3 * CONV_W + SSM_W + XATTN_W
SSM_GROUPS = 32
SSM_GROUP = 16
SSM_STATE = 64
N_STATE = SSM_GROUPS * SSM_STATE
ALPHA = 2.0 ** 0.25
LN_EPS = 1e-5
N_DEV = 8

ADAM_LR = 0.001
ADAM_B1 = 0.9
ADAM_B2 = 0.999
ADAM_EPS = 1e-08
ADAM_WD = 0.01
ADAM_STEP = 10

VMEM_LIMIT_V7X = 56 * 2 ** 20
SUBLANES = 8
LANES = 128

TOKEN_TILE = 256
SSM_BLOCK = 256
SSM_SEG = SSM_BLOCK // SUBLANES
LANE_CHUNK = 512
N_HALF = 2
HALF_W = SSM_W // N_HALF
HALF_STATE = N_STATE // N_HALF
HALF_COLS = 2 * HALF_STATE

NT = (((1,), (1,)), ((), ()))
TN = (((0,), (0,)), ((), ()))
NN = (((1,), (0,)), ((), ()))


def _dot(a, b, dims=NN):
    return lax.dot_general(a, b, dims, preferred_element_type=f32)


def _cparams(sem=None):
    return pltpu.CompilerParams(dimension_semantics=sem, vmem_limit_bytes=VMEM_LIMIT_V7X)


def _row_spec(tm, cols, rev_n=None):
    if rev_n is None:
        return pl.BlockSpec((tm, cols), lambda i: (i, 0))
    return pl.BlockSpec((tm, cols), lambda i: (rev_n - 1 - i, 0))


def _col_spec(rows, tm):
    return pl.BlockSpec((rows, tm), lambda i: (0, i))


def _const_spec(shape):
    nd = len(shape)
    return pl.BlockSpec(shape, lambda *_: (0,) * nd, pipeline_mode=pl.Buffered(1))


def _acc_spec(shape):
    nd = len(shape)
    return pl.BlockSpec(shape, lambda *_: (0,) * nd)


def _sds(shape, dtype):
    return jax.ShapeDtypeStruct(shape, dtype)


def _gelu(x):
    c = math.sqrt(2.0 / math.pi)
    return 0.5 * x * (1.0 + jnp.tanh(c * (x + 0.044715 * x * x * x)))


def _gelu_grad(x):
    c = math.sqrt(2.0 / math.pi)
    t = jnp.tanh(c * (x + 0.044715 * x * x * x))
    return 0.5 * (1.0 + t) + 0.5 * x * (1.0 - t * t) * c * (1.0 + 3.0 * 0.044715 * x * x)


def _colsum(a):
    return jnp.sum(a, axis=0, keepdims=True)


def _mesh_place():
    return lax.axis_index("x"), lax.axis_index("y"), lax.axis_index("c")


def _slot(p):
    return 4 * p[0] + 2 * p[1] + p[2]


def _other_devices(me):
    x, y, c = me
    flip = lambda v, d: 1 - v if d else v
    return [(flip(x, dx), flip(y, dy), flip(c, dc)) for dx in (0, 1) for dy in (0, 1) for dc in (0, 1)][1:]


def _all_gather(blocks, name):
    n = len(blocks)

    def body(*refs):
        ins, outs = refs[:n], refs[n:2 * n]
        send_sems, recv_sems, local_sems = refs[2 * n:]
        x, y, c = _mesh_place()
        me, sibling = (x, y, c), (x, y, 1 - c)
        chips = [(1 - x, y), (x, 1 - y), (1 - x, 1 - y)]

        def copy(a, k, block, to, src=None):
            rows = outs[a].at[_slot(block)]
            return pltpu.make_async_remote_copy(
                src_ref=rows if src is None else src, dst_ref=rows,
                send_sem=send_sems.at[a, k], recv_sem=recv_sems.at[a, k],
                device_id=to, device_id_type=pl.DeviceIdType.MESH)

        mine = [pltpu.make_async_copy(ins[a], outs[a].at[_slot(me)], local_sems.at[a]) for a in range(n)]
        for cp in mine:
            cp.start()
        first = []
        for a in range(n):
            first.append(copy(a, 0, me, sibling, src=ins[a]))
            first += [copy(a, 1 + j, me, (*chip, c), src=ins[a]) for j, chip in enumerate(chips)]
        for cp in first:
            cp.start()
        passed = []
        for a in range(n):
            for j, chip in enumerate(chips):
                copy(a, 1 + j, (*chip, c), me).wait_recv()
                fwd = copy(a, 4 + j, (*chip, c), sibling)
                fwd.start()
                passed.append(fwd)
        for a in range(n):
            copy(a, 0, sibling, me).wait_recv()
            for j, chip in enumerate(chips):
                copy(a, 4 + j, (*chip, 1 - c), me).wait_recv()
        for cp in first + passed:
            cp.wait_send()
        for cp in mine:
            cp.wait()

    any_spec = pl.BlockSpec(memory_space=pl.ANY)
    return pl.pallas_call(
        body, name=name,
        out_shape=[_sds((N_DEV,) + b.shape, b.dtype) for b in blocks],
        in_specs=[any_spec] * n, out_specs=[any_spec] * n,
        scratch_shapes=[pltpu.SemaphoreType.DMA((n, 7)), pltpu.SemaphoreType.DMA((n, 7)),
                        pltpu.SemaphoreType.DMA((n,))],
    )(*blocks)


def _side_gather_copies(ins, outs, send_sems, recv_sems, local_sems):
    me = _mesh_place()
    copies = []
    for a, (src, dst) in enumerate(zip(ins, outs)):
        copies.append(pltpu.make_async_copy(src, dst.at[_slot(me)], local_sems.at[a]))
        for k, peer in enumerate(_other_devices(me)):
            copies.append(pltpu.make_async_remote_copy(
                src_ref=src, dst_ref=dst.at[_slot(me)], send_sem=send_sems.at[a, k], recv_sem=recv_sems.at[a, k],
                device_id=peer, device_id_type=pl.DeviceIdType.MESH))
    return copies


def _side_gather_specs(blocks):
    n = len(blocks)
    any_spec = pl.BlockSpec(memory_space=pl.ANY)
    return ([any_spec] * n, [_sds((N_DEV,) + b.shape, b.dtype) for b in blocks],
            [pltpu.SemaphoreType.DMA((n, N_DEV - 1)), pltpu.SemaphoreType.DMA((n, N_DEV - 1)),
             pltpu.SemaphoreType.DMA((n,))])


def _kv_proj(mem, w_kv):
    def body(mem_ref, w_ref, kv_ref, kt_ref, memb_ref):
        mb = mem_ref[...].astype(bf16)
        memb_ref[...] = mb
        kv = _dot(mb, w_ref[...]).astype(bf16)
        kv_ref[...] = kv
        kt_ref[...] = kv[:, :XATTN_W].T

    return pl.pallas_call(
        body, name="kv_proj",
        out_shape=[_sds((MEM_LEN, 2 * XATTN_W), bf16), _sds((XATTN_W, MEM_LEN), bf16), _sds((MEM_LEN, D_MODEL), bf16)],
        compiler_params=_cparams(),
    )(mem, w_kv)


def _attention_probs(qb, kv_ref, h):
    kh = kv_ref[:, h * HEAD_DIM:(h + 1) * HEAD_DIM]
    s = _dot(qb[:, h * HEAD_DIM:(h + 1) * HEAD_DIM], kh, NT) * (HEAD_DIM ** -0.5)
    e = jnp.exp(s - jnp.max(s, axis=-1, keepdims=True))
    return e / jnp.sum(e, axis=-1, keepdims=True)


def _in_proj(x, w_in_t, b_gate, conv_w, kv, side_blocks):
    s_len = x.shape[0]
    tm = TOKEN_TILE
    n = s_len // tm
    ns = len(side_blocks)
    side_in_specs, side_shapes, side_sems = _side_gather_specs(side_blocks)

    def body(*refs):
        (x_ref, win_ref, bg_ref, cw_ref, kv_ref) = refs[:5]
        side_ins = refs[5:5 + ns]
        (xbt_ref, g_ref, cin_ref, u_ref, q_ref, ain_ref, o_ref, aint_ref, ot_ref) = refs[5 + ns:14 + ns]
        side_outs = refs[14 + ns:14 + 2 * ns]
        zs_ref = refs[14 + 2 * ns]
        side = _side_gather_copies(side_ins, side_outs, *refs[15 + 2 * ns:])
        i = pl.program_id(0)

        @pl.when(i == 0)
        def _():
            for cp in side:
                cp.start()

        xb = x_ref[...].astype(bf16)
        xbt_ref[...] = xb.T
        proj = _dot(xb, win_ref[...], NT)
        g_ref[...] = jax.nn.sigmoid(proj[:, :GATE_COLS] + bg_ref[...]).astype(bf16)
        cin = proj[:, GATE_COLS:GATE_COLS + 3 * CONV_W]
        cin_ref[...] = cin
        u_ref[...] = proj[:, GATE_COLS + 3 * CONV_W:GATE_COLS + 3 * CONV_W + SSM_W]
        qb = proj[:, IN_COLS - XATTN_W:].astype(bf16)
        q_ref[...] = qb

        cb, cc, ch = cin[:, :CONV_W], cin[:, CONV_W:2 * CONV_W], cin[:, 2 * CONV_W:]
        z = cc * ch

        @pl.when(i == 0)
        def _():
            zs_ref[0:8, :] = jnp.zeros((8, CONV_W), f32)

        zs_ref[8:8 + tm, :] = z
        z1 = zs_ref[pl.ds(7, tm), :]
        z2 = zs_ref[pl.ds(6, tm), :]
        cw = cw_ref[...]
        cz = cw[0:1] * z2 + cw[1:2] * z1 + cw[2:3] * z
        zs_ref[0:8, :] = zs_ref[tm:tm + 8, :]
        ain = (cb * cz).astype(bf16)
        ain_ref[...] = ain
        aint_ref[...] = ain.T

        probs = [_attention_probs(qb, kv_ref, h) for h in range(HEADS)]
        outs = [_dot(probs[h].astype(bf16), kv_ref[:, XATTN_W + h * HEAD_DIM:XATTN_W + (h + 1) * HEAD_DIM])
                for h in range(HEADS)]
        ob = jnp.concatenate(outs, axis=1).astype(bf16)
        o_ref[...] = ob
        ot_ref[...] = ob.T

        @pl.when(i == n - 1)
        def _():
            for cp in side:
                cp.wait()

    row_cols = [(GATE_COLS, bf16), (3 * CONV_W, f32), (SSM_W, f32), (XATTN_W, bf16), (CONV_W, bf16), (XATTN_W, bf16)]
    t_rows = [D_MODEL, CONV_W, XATTN_W]
    outs = pl.pallas_call(
        body, name="in_proj", grid=(n,),
        in_specs=[_row_spec(tm, D_MODEL), _const_spec((IN_COLS, D_MODEL)), _const_spec((1, GATE_COLS)),
                  _const_spec((3, CONV_W)), _const_spec((MEM_LEN, 2 * XATTN_W))] + side_in_specs,
        out_specs=([_col_spec(t_rows[0], tm)] + [_row_spec(tm, c) for c, _ in row_cols]
                   + [_col_spec(t_rows[1], tm), _col_spec(t_rows[2], tm)] + side_in_specs),
        out_shape=([_sds((t_rows[0], s_len), bf16)] + [_sds((s_len, c), dt) for c, dt in row_cols]
                   + [_sds((t_rows[1], s_len), bf16), _sds((t_rows[2], s_len), bf16)] + side_shapes),
        scratch_shapes=[pltpu.VMEM((tm + 8, CONV_W), f32)] + side_sems,
        compiler_params=_cparams(("arbitrary",)),
    )(x, w_in_t, b_gate, conv_w, kv, *side_blocks)
    return outs[:9], outs[9:]


def _state_cols(chunk):
    half, off = divmod(chunk * LANE_CHUNK, HALF_STATE)
    lo = half * HALF_COLS + off
    return slice(lo, lo + LANE_CHUNK), slice(lo + HALF_STATE, lo + HALF_STATE + LANE_CHUNK)


def _half_cols(half):
    lo = half * HALF_COLS
    return slice(lo, lo + HALF_STATE), slice(lo + HALF_STATE, lo + HALF_COLS)


def _rows_to_segments(src_ref, stage_ref, dst_ref):
    nc = SSM_W // LANES
    for c in range(nc):
        stage_ref[c] = src_ref[:, c * LANES:(c + 1) * LANES]
    for c in range(nc):
        for k in range(SSM_SEG):
            dst_ref[k * SUBLANES:(k + 1) * SUBLANES, c * LANES:(c + 1) * LANES] = (
                stage_ref[c, pl.ds(k, SUBLANES, stride=SSM_SEG), :])


def _rows_from_segments(src_ref, stage_ref, dst_ref):
    nc = SSM_W // LANES
    for c in range(nc):
        for k in range(SSM_SEG):
            stage_ref[c, pl.ds(k, SUBLANES, stride=SSM_SEG), :] = (
                src_ref[k * SUBLANES:(k + 1) * SUBLANES, c * LANES:(c + 1) * LANES])
    for c in range(nc):
        dst_ref[:, c * LANES:(c + 1) * LANES] = stage_ref[c]


def _ssm_scan(s_ref, pw_ref, init_ref, reverse, unroll):
    for chunk in range(N_STATE // LANE_CHUNK):
        re, im = _state_cols(chunk)
        ar = jnp.broadcast_to(pw_ref[0:1, re], (SUBLANES, LANE_CHUNK))
        ai = jnp.broadcast_to(pw_ref[0:1, im], (SUBLANES, LANE_CHUNK))
        if reverse:
            ai = -ai

        def step(j, carry, re=re, im=im, ar=ar, ai=ai):
            sr, si = carry
            k = (SSM_SEG - 1 - j) if reverse else j
            r0 = pl.multiple_of(k * SUBLANES, SUBLANES)
            nr = ar * sr - ai * si + s_ref[pl.ds(r0, SUBLANES), re]
            ni = ar * si + ai * sr + s_ref[pl.ds(r0, SUBLANES), im]
            s_ref[pl.ds(r0, SUBLANES), re] = nr
            s_ref[pl.ds(r0, SUBLANES), im] = ni
            return nr, ni

        if init_ref is None:
            init = (jnp.zeros((SUBLANES, LANE_CHUNK), f32),) * 2
        else:
            init = (init_ref[:, re], init_ref[:, im])
        lax.fori_loop(0, SSM_SEG, step, init, unroll=unroll)


def _ssm_add_carry(s_ref, pw_ref, cm_ref, reverse):
    for chunk in range(N_STATE // LANE_CHUNK):
        re, im = _state_cols(chunk)
        cr, ci = cm_ref[:, re], cm_ref[:, im]
        for k in range(SSM_SEG):
            pk = (SSM_SEG - 1 - k) if reverse else k
            pr = pw_ref[pk:pk + 1, re]
            pi = pw_ref[pk:pk + 1, im]
            if reverse:
                pi = -pi
            rows = slice(k * SUBLANES, (k + 1) * SUBLANES)
            s_ref[rows, re] = s_ref[rows, re] + (pr * cr - pi * ci)
            s_ref[rows, im] = s_ref[rows, im] + (pr * ci + pi * cr)


def _ssm_carries(first_row, s_ref, pw_ref, carry_ref, cm_ref, reverse):
    order = range(SUBLANES - 1, -1, -1) if reverse else range(SUBLANES)
    for half in range(N_HALF):
        re, im = _half_cols(half)
        a_r, a_i = pw_ref[SSM_SEG - 1:SSM_SEG, re], pw_ref[SSM_SEG - 1:SSM_SEG, im]
        if reverse:
            a_i = -a_i
        cr, ci = carry_ref[0:1, re], carry_ref[0:1, im]
        for seg in order:
            cm_ref[seg:seg + 1, re] = cr
            cm_ref[seg:seg + 1, im] = ci
            er = s_ref[first_row + seg:first_row + seg + 1, re]
            ei = s_ref[first_row + seg:first_row + seg + 1, im]
            cr, ci = a_r * cr - a_i * ci + er, a_r * ci + a_i * cr + ei
        carry_ref[0:1, re] = cr
        carry_ref[0:1, im] = ci


def _ssm_fwd(u, b_half, c_half, pw, d_skip, side_blocks):
    s_len = u.shape[0]
    tb = SSM_BLOCK
    n = s_len // tb
    ns = len(side_blocks)
    side_in_specs, side_shapes, side_sems = _side_gather_specs(side_blocks)

    def body(*refs):
        u_ref, b_ref, c_ref, pw_ref, d_ref = refs[:5]
        side_ins = refs[5:5 + ns]
        y_ref, cm_ref = refs[5 + ns:7 + ns]
        side_outs = refs[7 + ns:7 + 2 * ns]
        s_ref, carry_ref, up_ref, yp_ref, stage_ref = refs[7 + 2 * ns:12 + 2 * ns]
        side = _side_gather_copies(side_ins, side_outs, *refs[12 + 2 * ns:])
        i = pl.program_id(0)

        @pl.when(i == 0)
        def _():
            carry_ref[...] = jnp.zeros_like(carry_ref)
            for cp in side:
                cp.start()

        _rows_to_segments(u_ref, stage_ref, up_ref)
        u = up_ref[...]
        ub = u.astype(bf16)
        for half in range(N_HALF):
            s_ref[:, half * HALF_COLS:(half + 1) * HALF_COLS] = _dot(ub[:, half * HALF_W:(half + 1) * HALF_W], b_ref[half])
        _ssm_scan(s_ref, pw_ref, None, reverse=False, unroll=4)
        _ssm_carries(tb - SUBLANES, s_ref, pw_ref, carry_ref, cm_ref, reverse=False)
        _ssm_add_carry(s_ref, pw_ref, cm_ref, reverse=False)
        for half in range(N_HALF):
            cols = slice(half * HALF_W, (half + 1) * HALF_W)
            sb = s_ref[:, half * HALF_COLS:(half + 1) * HALF_COLS].astype(bf16)
            yp_ref[:, cols] = _dot(sb, c_ref[half]) + d_ref[:, cols] * u[:, cols]
        _rows_from_segments(yp_ref, stage_ref, y_ref)

        @pl.when(i == n - 1)
        def _():
            for cp in side:
                cp.wait()

    outs = pl.pallas_call(
        body, name="ssm_fwd", grid=(n,),
        in_specs=[_row_spec(tb, SSM_W), _const_spec((N_HALF, HALF_W, HALF_COLS)), _const_spec((N_HALF, HALF_COLS, HALF_W)),
                  _const_spec((SSM_SEG, 2 * N_STATE)), _const_spec((1, SSM_W))] + side_in_specs,
        out_specs=[_row_spec(tb, SSM_W), _row_spec(SUBLANES, 2 * N_STATE)] + side_in_specs,
        out_shape=[_sds((s_len, SSM_W), f32), _sds((n * SUBLANES, 2 * N_STATE), f32)] + side_shapes,
        scratch_shapes=[pltpu.VMEM((tb, 2 * N_STATE), f32), pltpu.VMEM((SUBLANES, 2 * N_STATE), f32),
                        pltpu.VMEM((tb, SSM_W), f32), pltpu.VMEM((tb, SSM_W), f32),
                        pltpu.VMEM((SSM_W // LANES, tb, LANES), f32)] + side_sems,
        compiler_params=_cparams(("arbitrary",)),
    )(u, b_half, c_half, pw, d_skip, *side_blocks)
    return outs[0], outs[1], outs[2:]


def _layer_norm_fwd(r, g, b):
    mu = jnp.mean(r, axis=-1, keepdims=True)
    var = jnp.mean(jnp.square(r - mu), axis=-1, keepdims=True)
    rstd = lax.rsqrt(var + LN_EPS)
    xhat = (r - mu) * rstd
    return xhat, rstd, xhat * g + b


def _layer_norm_bwd(dy, xhat, rstd, g):
    dxh = dy * g
    m1 = jnp.mean(dxh, axis=-1, keepdims=True)
    m2 = jnp.mean(dxh * xhat, axis=-1, keepdims=True)
    return rstd * (dxh - m1 - xhat * m2)


def _branch_outputs(ys_ref, ain_ref, o_ref, wglu_ref, wco_ref, wxo_ref):
    ysb = _gelu(ys_ref[...]).astype(bf16)
    glu = _dot(ysb, wglu_ref[...], NT)
    ga, sb = glu[:, :D_MODEL], jax.nn.sigmoid(glu[:, D_MODEL:])
    ya = _dot(ain_ref[...], wco_ref[...], NT)
    yc = _dot(o_ref[...], wxo_ref[...], NT)
    return ysb, ga, sb, ya, ga * sb, yc


def _mid_fwd(y_ssm, g, ain, ob, x, w_glu_t, w_co_t, w_xo_t, w_out, ln1_g, ln1_b):
    s_len = x.shape[0]
    tm = TOKEN_TILE
    n = s_len // tm

    def body(ys_ref, g_ref, ain_ref, o_ref, x_ref, wglu_ref, wco_ref, wxo_ref, wout_ref, lg_ref, lb_ref,
             ysbt_ref, mb_ref, xhat_ref, rstd_ref):
        ysb, _, _, ya, yb, yc = _branch_outputs(ys_ref, ain_ref, o_ref, wglu_ref, wco_ref, wxo_ref)
        ysbt_ref[...] = ysb.T
        gt = g_ref[...].astype(f32)
        merged = gt[:, :D_MODEL] * ya + gt[:, D_MODEL:2 * D_MODEL] * yb + gt[:, 2 * D_MODEL:] * yc
        mb = merged.astype(bf16)
        mb_ref[...] = mb
        r1 = ALPHA * x_ref[...] + _dot(mb, wout_ref[...])
        xhat, rstd, _ = _layer_norm_fwd(r1, lg_ref[...], lb_ref[...])
        xhat_ref[...] = xhat
        rstd_ref[...] = rstd

    row_cols = [(D_MODEL, bf16), (D_MODEL, f32), (1, f32)]
    return pl.pallas_call(
        body, name="mid_fwd", grid=(n,),
        in_specs=[_row_spec(tm, SSM_W), _row_spec(tm, GATE_COLS), _row_spec(tm, CONV_W), _row_spec(tm, XATTN_W),
                  _row_spec(tm, D_MODEL), _const_spec((2 * D_MODEL, SSM_W)), _const_spec((D_MODEL, CONV_W)),
                  _const_spec((D_MODEL, XATTN_W)), _const_spec((D_MODEL, D_MODEL)),
                  _const_spec((1, D_MODEL)), _const_spec((1, D_MODEL))],
        out_specs=[_col_spec(SSM_W, tm)] + [_row_spec(tm, c) for c, _ in row_cols],
        out_shape=[_sds((SSM_W, s_len), bf16)] + [_sds((s_len, c), dt) for c, dt in row_cols],
        compiler_params=_cparams(("parallel",)),
    )(y_ssm, g, ain, ob, x, w_glu_t, w_co_t, w_xo_t, w_out, ln1_g, ln1_b)


def _mlp_fwd_bwd(xhat1, tgt, ln1_g, ln1_b, w_up_t, b_up, w_down, b_down, ln2_g, ln2_b):
    s_len = xhat1.shape[0]
    tm = TOKEN_TILE
    n = s_len // tm
    fc = 1024
    nfc = D_FF // fc

    def body(xh_ref, t_ref, l1g_ref, l1b_ref, wup_ref, bup_ref, wdn_ref, bdn_ref, l2g_ref, l2b_ref,
             x1bt_ref, hdn_ref, dr2bt_ref, dpre_ref, dx1_ref,
             loss_ref, dl2g_ref, dl2b_ref, dbdn_ref, dbup_ref, rl_ref):
        i = pl.program_id(0)

        @pl.when(i == 0)
        def _():
            loss_ref[...] = jnp.zeros_like(loss_ref)
            dl2g_ref[...] = jnp.zeros_like(dl2g_ref)
            dl2b_ref[...] = jnp.zeros_like(dl2b_ref)
            dbdn_ref[...] = jnp.zeros_like(dbdn_ref)
            dbup_ref[...] = jnp.zeros_like(dbup_ref)

        x1 = xh_ref[...] * l1g_ref[...] + l1b_ref[...]
        x1b = x1.astype(bf16)
        x1bt_ref[...] = x1b.T
        chunks = [slice(c * fc, (c + 1) * fc) for c in range(nfc)]
        pres = [_dot(x1b, wup_ref[cols, :], NT) for cols in chunks]
        hbs = []
        for cols, pre in zip(chunks, pres):
            rl = jnp.maximum(pre + bup_ref[:, cols], 0.0)
            rl_ref[:, cols] = rl
            hb = (rl * rl).astype(bf16)
            hdn_ref[:, cols] = hb
            hbs.append(hb)
        acc = _dot(hbs[0], wdn_ref[chunks[0], :])
        for cols, hb in zip(chunks[1:], hbs[1:]):
            acc = acc + _dot(hb, wdn_ref[cols, :])
        r2 = ALPHA * x1 + acc + bdn_ref[...]
        xhat2, rstd2, y = _layer_norm_fwd(r2, l2g_ref[...], l2b_ref[...])
        err = y - t_ref[...]
        loss_ref[...] += jnp.sum(jnp.sum(err * err, axis=1, keepdims=True), axis=0, keepdims=True) * (0.5 / D_MODEL)
        dy = err * (1.0 / D_MODEL)
        dl2g_ref[...] += _colsum(dy * xhat2)
        dl2b_ref[...] += _colsum(dy)
        dr2 = _layer_norm_bwd(dy, xhat2, rstd2, l2g_ref[...])
        dbdn_ref[...] += _colsum(dr2)
        dr2b = dr2.astype(bf16)
        dr2bt_ref[...] = dr2b.T
        dhs = [_dot(dr2b, wdn_ref[cols, :], NT) for cols in chunks]
        dpbs = []
        for cols, dh in zip(chunks, dhs):
            dpre = dh * (2.0 * rl_ref[:, cols])
            dbup_ref[:, cols] += _colsum(dpre)
            dpb = dpre.astype(bf16)
            dpre_ref[:, cols] = dpb
            dpbs.append(dpb)
        dacc = _dot(dpbs[0], wup_ref[chunks[0], :])
        for cols, dpb in zip(chunks[1:], dpbs[1:]):
            dacc = dacc + _dot(dpb, wup_ref[cols, :])
        dx1_ref[...] = ALPHA * dr2 + dacc

    acc_shapes = [(1, LANES), (1, D_MODEL), (1, D_MODEL), (1, D_MODEL), (1, D_FF)]
    return pl.pallas_call(
        body, name="mlp_fwd_bwd", grid=(n,),
        in_specs=[_row_spec(tm, D_MODEL), _row_spec(tm, D_MODEL), _const_spec((1, D_MODEL)), _const_spec((1, D_MODEL)),
                  _const_spec((D_FF, D_MODEL)), _const_spec((1, D_FF)), _const_spec((D_FF, D_MODEL)),
                  _const_spec((1, D_MODEL)), _const_spec((1, D_MODEL)), _const_spec((1, D_MODEL))],
        out_specs=([_col_spec(D_MODEL, tm), _row_spec(tm, D_FF), _col_spec(D_MODEL, tm), _row_spec(tm, D_FF),
                    _row_spec(tm, D_MODEL)] + [_acc_spec(s) for s in acc_shapes]),
        out_shape=([_sds((D_MODEL, s_len), bf16), _sds((s_len, D_FF), bf16), _sds((D_MODEL, s_len), bf16),
                    _sds((s_len, D_FF), bf16), _sds((s_len, D_MODEL), f32)] + [_sds(s, f32) for s in acc_shapes]),
        scratch_shapes=[pltpu.VMEM((tm, D_FF), f32)],
        compiler_params=_cparams(("arbitrary",)),
    )(xhat1, tgt, ln1_g, ln1_b, w_up_t, b_up, w_down, b_down, ln2_g, ln2_b)


def _mid_bwd(dx1, xhat1, rstd1, g, ain, ob, y_ssm, ln1_g, w_out, w_glu_t, w_co_t, w_xo_t):
    s_len = dx1.shape[0]
    tm = TOKEN_TILE
    n = s_len // tm

    def body(dx1_ref, xh_ref, rs_ref, g_ref, ain_ref, o_ref, ys_ref, lg_ref, wout_ref, wglu_ref, wco_ref, wxo_ref,
             dxp_ref, dr1bt_ref, dgp_ref, dya_ref, dyc_ref, dglu_ref, dyssm_ref,
             dl1g_ref, dl1b_ref, dbg_ref):
        i = pl.program_id(0)

        @pl.when(i == 0)
        def _():
            dl1g_ref[...] = jnp.zeros_like(dl1g_ref)
            dl1b_ref[...] = jnp.zeros_like(dl1b_ref)
            dbg_ref[...] = jnp.zeros_like(dbg_ref)

        dx1 = dx1_ref[...]
        xhat = xh_ref[...]
        dl1g_ref[...] += _colsum(dx1 * xhat)
        dl1b_ref[...] += _colsum(dx1)
        dr1 = _layer_norm_bwd(dx1, xhat, rs_ref[...], lg_ref[...])
        dxp_ref[...] = ALPHA * dr1
        dr1b = dr1.astype(bf16)
        dr1bt_ref[...] = dr1b.T
        dm = _dot(dr1b, wout_ref[...], NT)

        _, ga, sb, ya, yb, yc = _branch_outputs(ys_ref, ain_ref, o_ref, wglu_ref, wco_ref, wxo_ref)
        gt = g_ref[...].astype(f32)
        branch = (ya, yb, yc)
        for j in range(3):
            cols = slice(j * D_MODEL, (j + 1) * D_MODEL)
            gj = gt[:, cols]
            dgp = dm * branch[j] * gj * (1.0 - gj)
            dbg_ref[:, cols] += _colsum(dgp)
            dgp_ref[:, cols] = dgp.astype(bf16)
        dya_ref[...] = (dm * gt[:, :D_MODEL]).astype(bf16)
        dyc_ref[...] = (dm * gt[:, 2 * D_MODEL:]).astype(bf16)
        dyb = dm * gt[:, D_MODEL:2 * D_MODEL]
        dga = (dyb * sb).astype(bf16)
        dgb = (dyb * ga * sb * (1.0 - sb)).astype(bf16)
        dglu_ref[:, :D_MODEL] = dga
        dglu_ref[:, D_MODEL:] = dgb
        dys = _dot(dga, wglu_ref[:D_MODEL, :]) + _dot(dgb, wglu_ref[D_MODEL:, :])
        dyssm_ref[...] = dys * _gelu_grad(ys_ref[...])

    row_cols = [(GATE_COLS, bf16), (D_MODEL, bf16), (D_MODEL, bf16), (2 * D_MODEL, bf16), (SSM_W, f32)]
    acc_shapes = [(1, D_MODEL), (1, D_MODEL), (1, GATE_COLS)]
    return pl.pallas_call(
        body, name="mid_bwd", grid=(n,),
        in_specs=[_row_spec(tm, D_MODEL), _row_spec(tm, D_MODEL), _row_spec(tm, 1), _row_spec(tm, GATE_COLS),
                  _row_spec(tm, CONV_W), _row_spec(tm, XATTN_W), _row_spec(tm, SSM_W),
                  _const_spec((1, D_MODEL)), _const_spec((D_MODEL, D_MODEL)), _const_spec((2 * D_MODEL, SSM_W)),
                  _const_spec((D_MODEL, CONV_W)), _const_spec((D_MODEL, XATTN_W))],
        out_specs=([_row_spec(tm, D_MODEL), _col_spec(D_MODEL, tm)] + [_row_spec(tm, c) for c, _ in row_cols]
                   + [_acc_spec(s) for s in acc_shapes]),
        out_shape=([_sds((s_len, D_MODEL), f32), _sds((D_MODEL, s_len), bf16)]
                   + [_sds((s_len, c), dt) for c, dt in row_cols] + [_sds(s, f32) for s in acc_shapes]),
        compiler_params=_cparams(("arbitrary",)),
    )(dx1, xhat1, rstd1, g, ain, ob, y_ssm, ln1_g, w_out, w_glu_t, w_co_t, w_xo_t)


def _ssm_bwd(u, dy, cm_all, b_half, c_half, pw, d_skip):
    s_len = u.shape[0]
    tb = SSM_BLOCK
    n = s_len // tb

    def body(u_ref, dy_ref, cm_ref, b_ref, c_ref, pw_ref, d_ref,
             du_ref, db_hbm, dc_hbm, da_ref, dd_ref,
             s_ref, g_ref, gcarry_ref, gcm_ref, db_ref, dc_ref, up_ref, dyp_ref, dup_ref, stage_ref):
        i = pl.program_id(0)

        @pl.when(i == 0)
        def _():
            gcarry_ref[...] = jnp.zeros_like(gcarry_ref)
            db_ref[...] = jnp.zeros_like(db_ref)
            dc_ref[...] = jnp.zeros_like(dc_ref)
            da_ref[...] = jnp.zeros_like(da_ref)
            dd_ref[...] = jnp.zeros_like(dd_ref)

        _rows_to_segments(u_ref, stage_ref, up_ref)
        _rows_to_segments(dy_ref, stage_ref, dyp_ref)
        u = up_ref[...]
        ub = u.astype(bf16)
        dy = dyp_ref[...]
        dyb = dy.astype(bf16)
        dd_ref[...] += _colsum(dy * u)

        for half in range(N_HALF):
            s_ref[:, half * HALF_COLS:(half + 1) * HALF_COLS] = _dot(ub[:, half * HALF_W:(half + 1) * HALF_W], b_ref[half])
        _ssm_scan(s_ref, pw_ref, cm_ref, reverse=False, unroll=True)

        for half in range(N_HALF):
            g_ref[:, half * HALF_COLS:(half + 1) * HALF_COLS] = _dot(dyb[:, half * HALF_W:(half + 1) * HALF_W], c_ref[half], NT)
        _ssm_scan(g_ref, pw_ref, None, reverse=True, unroll=True)
        _ssm_carries(0, g_ref, pw_ref, gcarry_ref, gcm_ref, reverse=True)
        _ssm_add_carry(g_ref, pw_ref, gcm_ref, reverse=True)

        for half in range(N_HALF):
            cols = slice(half * HALF_W, (half + 1) * HALF_W)
            scols = slice(half * HALF_COLS, (half + 1) * HALF_COLS)
            gb = g_ref[:, scols].astype(bf16)
            dup_ref[:, cols] = _dot(gb, b_ref[half], NT) + d_ref[:, cols] * dy[:, cols]
            db_ref[half] += _dot(ub[:, cols], gb, TN)
            dc_ref[half] += _dot(s_ref[:, scols].astype(bf16), dyb[:, cols], TN)
        _rows_from_segments(dup_ref, stage_ref, du_ref)

        for chunk in range(N_STATE // LANE_CHUNK):
            re, im = _state_cols(chunk)
            acc_r = da_ref[:, re]
            acc_i = da_ref[:, im]
            for k in range(SSM_SEG):
                rows = slice(k * SUBLANES, (k + 1) * SUBLANES)
                if k == 0:
                    pr, pi = cm_ref[:, re], cm_ref[:, im]
                else:
                    prev = slice((k - 1) * SUBLANES, k * SUBLANES)
                    pr, pi = s_ref[prev, re], s_ref[prev, im]
                gr, gi = g_ref[rows, re], g_ref[rows, im]
                acc_r = acc_r + (gr * pr + gi * pi)
                acc_i = acc_i + (gi * pr - gr * pi)
            da_ref[:, re] = acc_r
            da_ref[:, im] = acc_i

        @pl.when(i == n - 1)
        def _():
            pltpu.sync_copy(db_ref, db_hbm)
            pltpu.sync_copy(dc_ref, dc_hbm)

    rev = functools.partial(_row_spec, rev_n=n)
    any_spec = pl.BlockSpec(memory_space=pl.ANY)
    state_rows = pltpu.VMEM((tb, 2 * N_STATE), f32)
    seg_rows = pltpu.VMEM((SUBLANES, 2 * N_STATE), f32)
    tok_rows = pltpu.VMEM((tb, SSM_W), f32)
    return pl.pallas_call(
        body, name="ssm_bwd", grid=(n,),
        in_specs=[rev(tb, SSM_W), rev(tb, SSM_W), rev(SUBLANES, 2 * N_STATE),
                  _const_spec((N_HALF, HALF_W, HALF_COLS)), _const_spec((N_HALF, HALF_COLS, HALF_W)),
                  _const_spec((SSM_SEG, 2 * N_STATE)), _const_spec((1, SSM_W))],
        out_specs=[rev(tb, SSM_W), any_spec, any_spec, _acc_spec((SUBLANES, 2 * N_STATE)), _acc_spec((1, SSM_W))],
        out_shape=[_sds((s_len, SSM_W), f32), _sds((N_HALF, HALF_W, HALF_COLS), f32),
                   _sds((N_HALF, HALF_COLS, HALF_W), f32), _sds((SUBLANES, 2 * N_STATE), f32), _sds((1, SSM_W), f32)],
        scratch_shapes=[state_rows, state_rows, seg_rows, seg_rows,
                        pltpu.VMEM((N_HALF, HALF_W, HALF_COLS), f32), pltpu.VMEM((N_HALF, HALF_COLS, HALF_W), f32),
                        tok_rows, tok_rows, tok_rows, pltpu.VMEM((SSM_W // LANES, tb, LANES), f32)],
        compiler_params=_cparams(("arbitrary",)),
    )(u, dy, cm_all, b_half, c_half, pw, d_skip)


def _branch_bwd(dya, dyc, cin, q, kv, k_t, conv_w, w_co_t, w_xo_t, side_blocks):
    s_len = dya.shape[0]
    tm = TOKEN_TILE
    n = s_len // tm
    halo_blocks = tm // 8
    ns = len(side_blocks)
    conv_tile = _sds((8, CONV_W), f32)
    side_in_specs, side_shapes, side_sems = _side_gather_specs(list(side_blocks) + [conv_tile])

    def body(*refs):
        (dya_ref, dyc_ref, cin_ref, cprev_ref, q_ref, kv_ref, cw_ref, wco_ref, wxo_ref, kt_ref) = refs[:10]
        side_ins = refs[10:10 + ns]
        dconv_ref, dq_ref, dkv_ref = refs[10 + ns:13 + ns]
        side_outs = refs[13 + ns:14 + 2 * ns]
        zs_ref, dczs_ref, dcw_ref = refs[14 + 2 * ns:17 + 2 * ns]
        copies = _side_gather_copies(list(side_ins) + [dcw_ref], side_outs, *refs[17 + 2 * ns:])
        side, conv_side = copies[:ns * N_DEV], copies[ns * N_DEV:]
        i = pl.program_id(0)
        tile = n - 1 - i

        @pl.when(i == 0)
        def _():
            dcw_ref[...] = jnp.zeros_like(dcw_ref)
            dkv_ref[...] = jnp.zeros_like(dkv_ref)
            dczs_ref[tm:tm + 8, :] = jnp.zeros((8, CONV_W), f32)
            for cp in side:
                cp.start()

        cin = cin_ref[...]
        cb, cc, ch = cin[:, :CONV_W], cin[:, CONV_W:2 * CONV_W], cin[:, 2 * CONV_W:]
        z = cc * ch
        cprev = cprev_ref[...]
        zprev = cprev[:, CONV_W:2 * CONV_W] * cprev[:, 2 * CONV_W:]
        zs_ref[0:8, :] = jnp.where(tile == 0, 0.0, zprev)
        zs_ref[8:8 + tm, :] = z
        z1 = zs_ref[pl.ds(7, tm), :]
        z2 = zs_ref[pl.ds(6, tm), :]
        cw = cw_ref[...]
        cz = cw[0:1] * z2 + cw[1:2] * z1 + cw[2:3] * z

        dain = _dot(dya_ref[...], wco_ref[...])
        dcb = dain * cz
        dcz = dain * cb
        dczs_ref[0:tm, :] = dcz
        dcz1 = dczs_ref[pl.ds(1, tm), :]
        dcz2 = dczs_ref[pl.ds(2, tm), :]
        dz = cw[2:3] * dcz + cw[1:2] * dcz1 + cw[0:1] * dcz2
        dczs_ref[tm:tm + 8, :] = dczs_ref[0:8, :]
        dcw_ref[0:1, :] += _colsum(dcz * z2)
        dcw_ref[1:2, :] += _colsum(dcz * z1)
        dcw_ref[2:3, :] += _colsum(dcz * z)
        dconv_ref[:, :CONV_W] = dcb.astype(bf16)
        dconv_ref[:, CONV_W:2 * CONV_W] = (dz * ch).astype(bf16)
        dconv_ref[:, 2 * CONV_W:] = (dz * cc).astype(bf16)

        qb = q_ref[...]
        dob = _dot(dyc_ref[...], wxo_ref[...]).astype(bf16)
        kv = kv_ref[...]
        heads = range(HEADS)
        hcs = [slice(h * HEAD_DIM, (h + 1) * HEAD_DIM) for h in heads]
        vcs = [slice(XATTN_W + h * HEAD_DIM, XATTN_W + (h + 1) * HEAD_DIM) for h in heads]
        s_t = [_dot(kv[:, hcs[h]], qb[:, hcs[h]], NT) * (HEAD_DIM ** -0.5) for h in heads]
        dp_t = [_dot(kv[:, vcs[h]], dob[:, hcs[h]], NT) for h in heads]
        e_t = [jnp.exp(s_t[h] - jnp.max(s_t[h], axis=0, keepdims=True)) for h in heads]
        p_t = [e_t[h] / jnp.sum(e_t[h], axis=0, keepdims=True) for h in heads]
        dv = [_dot(p_t[h].astype(bf16), dob[:, hcs[h]]) for h in heads]
        ds_t = [(p_t[h] * (dp_t[h] - jnp.sum(dp_t[h] * p_t[h], axis=0, keepdims=True)) * (HEAD_DIM ** -0.5)).astype(bf16)
                for h in heads]
        dk = [_dot(ds_t[h], qb[:, hcs[h]]) for h in heads]
        dq_t = [_dot(kt_ref[hcs[h], :], ds_t[h]) for h in heads]
        dq_ref[...] = jnp.concatenate(dq_t, axis=0).T.astype(bf16)
        dkv_ref[...] += jnp.concatenate(dk + dv, axis=1)

        @pl.when(i == n - 1)
        def _():
            for cp in conv_side:
                cp.start()
            for cp in side + conv_side:
                cp.wait()

    rev = functools.partial(_row_spec, rev_n=n)
    prev_spec = pl.BlockSpec((8, 3 * CONV_W), lambda i: (jnp.maximum((n - 1 - i) * halo_blocks - 1, 0), 0))
    outs = pl.pallas_call(
        body, name="branch_bwd", grid=(n,),
        in_specs=[rev(tm, D_MODEL), rev(tm, D_MODEL), rev(tm, 3 * CONV_W), prev_spec, rev(tm, XATTN_W),
                  _const_spec((MEM_LEN, 2 * XATTN_W)), _const_spec((3, CONV_W)), _const_spec((D_MODEL, CONV_W)),
                  _const_spec((D_MODEL, XATTN_W)), _const_spec((XATTN_W, MEM_LEN))] + side_in_specs[:ns],
        out_specs=[rev(tm, 3 * CONV_W), rev(tm, XATTN_W), _acc_spec((MEM_LEN, 2 * XATTN_W))] + side_in_specs,
        out_shape=[_sds((s_len, 3 * CONV_W), bf16), _sds((s_len, XATTN_W), bf16),
                   _sds((MEM_LEN, 2 * XATTN_W), f32)] + side_shapes,
        scratch_shapes=[pltpu.VMEM((tm + 8, CONV_W), f32), pltpu.VMEM((tm + 8, CONV_W), f32),
                        pltpu.VMEM((8, CONV_W), f32)] + side_sems,
        compiler_params=_cparams(("arbitrary",)),
    )(dya, dyc, cin, cin, q, kv, conv_w, w_co_t, w_xo_t, k_t, *side_blocks)
    return outs[0], outs[1], outs[2], outs[3:]


def _in_proj_bwd(dgp, dconv, du, dq, dxp, w_in_t):
    s_len = dgp.shape[0]
    tm = TOKEN_TILE
    n = s_len // tm

    def body(dgp_ref, dconv_ref, du_ref, dq_ref, dxp_ref, win_ref, dx_ref, dproj_ref):
        dproj = jnp.concatenate([dgp_ref[...], dconv_ref[...], du_ref[...].astype(bf16), dq_ref[...]], axis=1)
        dproj_ref[...] = dproj
        dx_ref[...] = dxp_ref[...] + _dot(dproj, win_ref[...])

    return pl.pallas_call(
        body, name="in_proj_bwd", grid=(n,),
        in_specs=[_row_spec(tm, GATE_COLS), _row_spec(tm, 3 * CONV_W), _row_spec(tm, SSM_W), _row_spec(tm, XATTN_W),
                  _row_spec(tm, D_MODEL), _const_spec((IN_COLS, D_MODEL))],
        out_specs=[_row_spec(tm, D_MODEL), _row_spec(tm, IN_COLS)],
        out_shape=[_sds((s_len, D_MODEL), f32), _sds((s_len, IN_COLS), bf16)],
        compiler_params=_cparams(("parallel",)),
    )(dgp, dconv, du, dq, dxp, w_in_t)


N_CHIP = 4
CHIP_STEPS = [(1, 1), (1, 0), (0, 1), (0, 0)]


def _flip(v, d):
    return 1 - v if d else v


def _chip_order():
    x, y, _ = _mesh_place()
    return jnp.stack([2 * _flip(x, dx) + _flip(y, dy) for dx, dy in CHIP_STEPS]).astype(jnp.int32)


def _weight_grad_scatter(a_t, b, name, tm, tt):
    m, s_len = a_t.shape
    n_cols = b.shape[1]
    w = n_cols // N_DEV
    tn = 2 * w
    tm, tt = min(tm, m), min(tt, s_len)
    nm, nt = m // tm, s_len // tt
    assert m % tm == 0 and s_len % tt == 0

    def body(order_ref, a_ref, b_ref, recv_ref, acc_ref, send_ref, sib_ref, stash_ref,
             d2d_send, d2d_recv, ici_send, ici_recv, local_sem):
        del order_ref
        q, im, t = pl.program_id(0), pl.program_id(1), pl.program_id(2)
        x, y, c = _mesh_place()
        mesh_id = pl.DeviceIdType.MESH

        @pl.when(t == 0)
        def _():
            acc_ref[...] = jnp.zeros_like(acc_ref)

        acc_ref[...] += _dot(a_ref[...], b_ref[...])

        def to_sibling(qq, imm):
            rows = pl.ds(pl.multiple_of(imm * tm, tm), tm)
            return pltpu.make_async_remote_copy(
                src_ref=send_ref.at[qq, 0, rows, :], dst_ref=sib_ref.at[qq, rows, :],
                send_sem=d2d_send.at[qq], recv_sem=d2d_recv.at[qq, imm],
                device_id=(x, y, 1 - c), device_id_type=mesh_id)

        def finish_tile(qq, imm):
            rows = pl.ds(pl.multiple_of(imm * tm, tm), tm)
            to_sibling(qq, imm).wait_recv()
            both = stash_ref[...] + sib_ref[qq, rows, :].astype(f32)
            send_ref[qq, 1, rows, :] = both.astype(bf16)
            for step, (dx, dy) in enumerate(CHIP_STEPS):
                @pl.when(qq == step)
                def _(step=step, dx=dx, dy=dy):
                    src, dst = send_ref.at[step, 1, rows, :], recv_ref.at[step, rows, :]
                    if dx or dy:
                        pltpu.make_async_remote_copy(
                            src_ref=src, dst_ref=dst, send_sem=ici_send.at[step], recv_sem=ici_recv.at[step],
                            device_id=(_flip(x, dx), _flip(y, dy), c), device_id_type=mesh_id).start()
                    else:
                        pltpu.make_async_copy(src, dst, local_sem).start()

        @pl.when(t == nt - 1)
        def _():
            tile = q * nm + im

            @pl.when(tile > 0)
            def _():
                finish_tile((tile - 1) // nm, (tile - 1) % nm)

            rows = pl.ds(pl.multiple_of(im * tm, tm), tm)
            for core in (0, 1):
                @pl.when(c == core)
                def _(core=core):
                    other = 1 - core
                    send_ref[q, 0, rows, :] = acc_ref[:, other * w:(other + 1) * w].astype(bf16)
                    stash_ref[...] = acc_ref[:, core * w:(core + 1) * w]
            to_sibling(q, im).start()

            @pl.when(tile == N_CHIP * nm - 1)
            def _():
                finish_tile(q, im)

        @pl.when((q == N_CHIP - 1) & (im == nm - 1) & (t == nt - 1))
        def _():
            for step, (dx, dy) in enumerate(CHIP_STEPS):
                pltpu.make_async_remote_copy(
                    src_ref=send_ref.at[step, 0], dst_ref=sib_ref.at[step],
                    send_sem=d2d_send.at[step], recv_sem=d2d_recv.at[step, 0],
                    device_id=(x, y, 1 - c), device_id_type=mesh_id).wait_send()
                src, dst = send_ref.at[step, 1], recv_ref.at[step]
                if dx or dy:
                    pltpu.make_async_remote_copy(
                        src_ref=src, dst_ref=dst, send_sem=ici_send.at[step], recv_sem=ici_recv.at[step],
                        device_id=(_flip(x, dx), _flip(y, dy), c), device_id_type=mesh_id).wait()
                else:
                    pltpu.make_async_copy(src, dst, local_sem).wait()

    grid_spec = pltpu.PrefetchScalarGridSpec(
        num_scalar_prefetch=1, grid=(N_CHIP, nm, nt),
        in_specs=[pl.BlockSpec((tm, tt), lambda q, im, t, order: (im, t)),
                  pl.BlockSpec((tt, tn), lambda q, im, t, order: (t, order[q]))],
        out_specs=pl.BlockSpec(memory_space=pl.ANY),
        scratch_shapes=[pltpu.VMEM((tm, tn), f32), pltpu.VMEM((N_CHIP, 2, m, w), bf16), pltpu.VMEM((N_CHIP, m, w), bf16),
                        pltpu.VMEM((tm, w), f32),
                        pltpu.SemaphoreType.DMA((N_CHIP,)), pltpu.SemaphoreType.DMA((N_CHIP, nm)),
                        pltpu.SemaphoreType.DMA((N_CHIP - 1,)), pltpu.SemaphoreType.DMA((N_CHIP - 1,)),
                        pltpu.SemaphoreType.DMA])
    return pl.pallas_call(
        body, name=name, grid_spec=grid_spec,
        out_shape=_sds((N_CHIP, m, w), bf16),
        compiler_params=_cparams(("arbitrary", "arbitrary", "arbitrary")),
    )(_chip_order(), a_t, b)


def _adamw(w, g, m, v):
    m = ADAM_B1 * m + (1.0 - ADAM_B1) * g
    v = ADAM_B2 * v + (1.0 - ADAM_B2) * jnp.square(g)
    m_hat = m / (1.0 - ADAM_B1 ** ADAM_STEP)
    v_hat = v / (1.0 - ADAM_B2 ** ADAM_STEP)
    delta = -ADAM_LR * (m_hat / (jnp.sqrt(v_hat) + ADAM_EPS) + ADAM_WD * w)
    return delta, m, v


def _sum_parts(p_ref):
    g = p_ref[0].astype(f32)
    for j in range(1, p_ref.shape[0]):
        g = g + p_ref[j].astype(f32)
    return g


def _adamw_update(w, m, v, parts, name, transposed):
    rows, cols = w.shape
    n_parts = parts.shape[0]
    if transposed:
        tc = 256
        steps = cols // tc
        p_spec = pl.BlockSpec((n_parts, tc, rows), lambda i: (0, i, 0))
        spec = pl.BlockSpec((rows, tc), lambda i: (0, i))
    else:
        tr = next(t for t in (256, 128, 64, 32, 16, 8) if rows % t == 0)
        steps = rows // tr
        p_spec = pl.BlockSpec((n_parts, tr, cols), lambda i: (0, i, 0))
        spec = pl.BlockSpec((tr, cols), lambda i: (i, 0))

    def body(w_ref, p_ref, m_ref, v_ref, g_ref, d_ref, nm_ref, nv_ref):
        g = _sum_parts(p_ref)
        if transposed:
            g = g.T
        g_ref[...] = g
        d_ref[...], nm_ref[...], nv_ref[...] = _adamw(w_ref[...], g, m_ref[...], v_ref[...])

    return pl.pallas_call(
        body, name=name, grid=(steps,),
        in_specs=[spec, p_spec, spec, spec], out_specs=[spec] * 4,
        out_shape=[_sds((rows, cols), f32)] * 4,
        compiler_params=_cparams(("parallel",)),
    )(w, parts, m, v)


SMALL_GROUPS = [
    (["b_gate", "ln1_g", "ln1_b", "b_up", "b_down", "ln2_g", "ln2_b", "ssm_d"], 1),
    (["ssm_lam_re", "ssm_lam_im", "ssm_c_re", "ssm_c_im", "ssm_b_re", "ssm_b_im"], 0),
    (["conv_w"], 0),
    (["ssm_log_dt"], 0),
]


def _sum_small(group_parts):
    def body(*refs):
        n = len(refs) // 2
        for p_ref, o_ref in zip(refs[:n], refs[n:]):
            o_ref[...] = _sum_parts(p_ref)

    return pl.pallas_call(
        body, name="sum_small",
        out_shape=[_sds(p.shape[1:], f32) for p in group_parts],
        compiler_params=_cparams(),
    )(*group_parts)


def _adamw_small(ws, ms, vs, group_sums):
    names = [k for group, _ in SMALL_GROUPS for k in group]
    n = len(names)

    def body(*refs):
        w_refs, m_refs, v_refs = (dict(zip(names, refs[j * n:(j + 1) * n])) for j in range(3))
        p_refs = refs[3 * n:3 * n + len(SMALL_GROUPS)]
        out_refs = [dict(zip(names, refs[3 * n + len(SMALL_GROUPS) + j * n:][:n])) for j in range(4)]
        for (group, axis), p_ref in zip(SMALL_GROUPS, p_refs):
            total = p_ref[...]
            off = 0
            for k in group:
                size = SMALL[k][axis]
                g = total[:, off:off + size] if axis == 1 else total[off:off + size, :]
                off += size
                d, nm, nv = _adamw(w_refs[k][...], g, m_refs[k][...], v_refs[k][...])
                for j, val in enumerate((g, d, nm, nv)):
                    out_refs[j][k][...] = val

    res = pl.pallas_call(
        body, name="adamw_small",
        out_shape=[_sds(SMALL[k], f32) for _ in range(4) for k in names],
        compiler_params=_cparams(),
    )(*[ws[k] for k in names], *[ms[k] for k in names], *[vs[k] for k in names], *group_sums)
    return [dict(zip(names, res[j * n:(j + 1) * n])) for j in range(4)]


def _ssm_discretize(lam_re, lam_im, log_dt, b_re, b_im):
    dt = jnp.exp(log_dt)[:, None]
    mag = jnp.exp(lam_re * dt)
    abar_r = mag * jnp.cos(lam_im * dt)
    abar_i = mag * jnp.sin(lam_im * dt)
    den = lam_re * lam_re + lam_im * lam_im
    nr = abar_r - 1.0
    ni = abar_i
    kr = (nr * lam_re + ni * lam_im) / den
    ki = (ni * lam_re - nr * lam_im) / den
    bbar_r = kr[:, None, :] * b_re - ki[:, None, :] * b_im
    bbar_i = kr[:, None, :] * b_im + ki[:, None, :] * b_re
    return abar_r, abar_i, bbar_r, bbar_i


def _state_layout(re, im):
    parts = []
    for half in range(N_HALF):
        cols = slice(half * HALF_STATE, (half + 1) * HALF_STATE)
        parts += [re[..., cols], im[..., cols]]
    return jnp.concatenate(parts, axis=-1)


def _state_unlayout(a):
    re = jnp.concatenate([a[..., _half_cols(h)[0]] for h in range(N_HALF)], axis=-1)
    im = jnp.concatenate([a[..., _half_cols(h)[1]] for h in range(N_HALF)], axis=-1)
    return re, im


def _abar_powers(abar_r, abar_i):
    pr, pi = abar_r.reshape(1, N_STATE), abar_i.reshape(1, N_STATE)
    while pr.shape[0] < SSM_SEG:
        tr, ti = pr[-1:], pi[-1:]
        pr, pi = (jnp.concatenate([pr, pr * tr - pi * ti], axis=0), jnp.concatenate([pi, pr * ti + pi * tr], axis=0))
    return _state_layout(pr, pi)


HALF_GROUPS = SSM_GROUPS // N_HALF


def _half_block_diag(blocks):
    _, r, c = blocks.shape
    eye = jnp.eye(HALF_GROUPS, dtype=blocks.dtype)
    b4 = blocks.reshape(N_HALF, HALF_GROUPS, r, c)
    return jnp.einsum("ngrc,gk->ngrkc", b4, eye).reshape(N_HALF, HALF_GROUPS * r, HALF_GROUPS * c)


def _half_diag_blocks(mat, r, c):
    eye = jnp.eye(HALF_GROUPS, dtype=mat.dtype)
    m5 = mat.reshape(N_HALF, HALF_GROUPS, r, HALF_GROUPS, c)
    return jnp.einsum("ngrkc,gk->ngrc", m5, eye).reshape(SSM_GROUPS, r, c)


BIG = ["w_in", "w_conv_out", "w_glu", "w_kv", "w_xattn_out", "w_out", "w_up", "w_down"]
GATHER_TRANSPOSED = ["w_conv_out", "w_glu", "w_xattn_out", "w_up"]
PARTS_TRANSPOSED = ["w_in", "w_kv", "w_out", "w_down"]
SMALL = {"b_gate": (1, GATE_COLS), "conv_w": (3, CONV_W), "ssm_lam_re": (SSM_GROUPS, SSM_STATE),
         "ssm_lam_im": (SSM_GROUPS, SSM_STATE), "ssm_log_dt": (1, SSM_GROUPS),
         "ssm_b_re": (SSM_W, SSM_STATE), "ssm_b_im": (SSM_W, SSM_STATE),
         "ssm_c_re": (SSM_W, SSM_STATE), "ssm_c_im": (SSM_W, SSM_STATE), "ssm_d": (1, SSM_W),
         "ln1_g": (1, D_MODEL), "ln1_b": (1, D_MODEL), "b_up": (1, D_FF), "b_down": (1, D_MODEL),
         "ln2_g": (1, D_MODEL), "ln2_b": (1, D_MODEL)}
WEIGHTS = ["w_in", "b_gate", "conv_w", "w_conv_out", "ssm_lam_re", "ssm_lam_im", "ssm_log_dt", "ssm_b_re", "ssm_b_im",
           "ssm_c_re", "ssm_c_im", "ssm_d", "w_glu", "w_kv", "w_xattn_out", "w_out", "ln1_g", "ln1_b", "w_up", "b_up",
           "w_down", "b_down", "ln2_g", "ln2_b"]


def _local_step(x, mem, tgt, full, late, small):
    lam_re, lam_im, log_dt = small["ssm_lam_re"], small["ssm_lam_im"], small["ssm_log_dt"].reshape(SSM_GROUPS)
    c_shape = (SSM_GROUPS, SSM_GROUP, SSM_STATE)
    disc, disc_vjp = jax.vjp(_ssm_discretize, lam_re, lam_im, log_dt,
                             small["ssm_b_re"].reshape(c_shape), small["ssm_b_im"].reshape(c_shape))
    abar_r, abar_i, bbar_r, bbar_i = disc
    pw = _abar_powers(abar_r, abar_i)
    c_re, c_im = small["ssm_c_re"].reshape(c_shape), small["ssm_c_im"].reshape(c_shape)
    b_half = jnp.concatenate([_half_block_diag(bbar_r), _half_block_diag(bbar_i)], axis=2).astype(bf16)
    c_half = jnp.concatenate([_half_block_diag(c_re.transpose(0, 2, 1)), -_half_block_diag(c_im.transpose(0, 2, 1))],
                             axis=1).astype(bf16)

    s_len = x.shape[0]
    stack = lambda a: a.reshape(-1, a.shape[-1])
    kv, k_t, memb = _kv_proj(mem, full["w_kv"])
    (xbt, g, cin, u, q, ain, ob, aint, obt), side = _in_proj(
        x, full["w_in"], small["b_gate"], small["conv_w"], kv,
        [late[k] for k in ("w_glu", "w_conv_out", "w_xattn_out", "w_out", "w_up")])
    w_glu_t, w_co_t, w_xo_t, w_out, w_up_t = (stack(a) for a in side)
    y_ssm, cm_all, side = _ssm_fwd(u, b_half, c_half, pw, small["ssm_d"], [late["w_down"]])
    w_down = stack(side[0])
    ysbt, mb, xhat1, rstd1 = _mid_fwd(y_ssm, g, ain, ob, x, w_glu_t, w_co_t, w_xo_t, w_out,
                                      small["ln1_g"], small["ln1_b"])
    (x1bt, hdn, dr2bt, dpre, dx1, loss, dl2g, dl2b, dbdn, dbup) = _mlp_fwd_bwd(
        xhat1, tgt, small["ln1_g"], small["ln1_b"], w_up_t, small["b_up"], w_down,
        small["b_down"], small["ln2_g"], small["ln2_b"])
    recv = {}
    recv["w_down"] = _weight_grad_scatter(dr2bt, hdn, "dw_down", tm=512, tt=2048)
    recv["w_up"] = _weight_grad_scatter(x1bt, dpre, "dw_up", tm=512, tt=2048)
    (dxp, dr1bt, dgp, dya, dyc, dglu, dyssm, dl1g, dl1b, dbg) = _mid_bwd(
        dx1, xhat1, rstd1, g, ain, ob, y_ssm, small["ln1_g"], w_out, w_glu_t, w_co_t, w_xo_t)
    recv["w_out"] = _weight_grad_scatter(dr1bt, mb, "dw_out", tm=512, tt=s_len)
    recv["w_glu"] = _weight_grad_scatter(ysbt, dglu, "dw_glu", tm=512, tt=s_len)
    du, db_half, dc_half, da8, dd = _ssm_bwd(u, dyssm, cm_all, b_half, c_half, pw, small["ssm_d"])
    dabar_r, dabar_i = _state_unlayout(jnp.sum(da8, axis=0))
    dbbar_r = _half_diag_blocks(db_half[:, :, :HALF_STATE], SSM_GROUP, SSM_STATE)
    dbbar_i = _half_diag_blocks(db_half[:, :, HALF_STATE:], SSM_GROUP, SSM_STATE)
    g_shape = (SSM_GROUPS, SSM_STATE)
    dlam_re, dlam_im, dlog_dt, db_re, db_im = disc_vjp(
        (dabar_r.reshape(g_shape), dabar_i.reshape(g_shape), dbbar_r, dbbar_i))
    dc_re = _half_diag_blocks(dc_half[:, :HALF_STATE, :], SSM_STATE, SSM_GROUP).transpose(0, 2, 1)
    dc_im = -_half_diag_blocks(dc_half[:, HALF_STATE:, :], SSM_STATE, SSM_GROUP).transpose(0, 2, 1)

    small_grads = {
        "b_gate": dbg, "ssm_lam_re": dlam_re, "ssm_lam_im": dlam_im, "ssm_log_dt": dlog_dt,
        "ssm_b_re": db_re, "ssm_b_im": db_im, "ssm_c_re": dc_re, "ssm_c_im": dc_im, "ssm_d": dd,
        "ln1_g": dl1g, "ln1_b": dl1b, "b_up": dbup, "b_down": dbdn, "ln2_g": dl2g, "ln2_b": dl2b,
    }
    small_grads = {k: a.reshape(SMALL[k]) for k, a in small_grads.items()}
    groups = [(group, axis) for group, axis in SMALL_GROUPS if group != ["conv_w"]]
    stacks = [jnp.concatenate([small_grads[k] for k in group], axis=axis) if len(group) > 1 else small_grads[group[0]]
              for group, axis in groups]
    dense = lambda a: a.reshape(-1, LANES) if a.size % LANES == 0 else a
    dconv, dq, dkv, group_parts = _branch_bwd(dya, dyc, cin, q, kv, k_t, small["conv_w"], w_co_t, w_xo_t,
                                              [dense(a) for a in stacks])
    recv["w_conv_out"] = _weight_grad_scatter(aint, dya, "dw_conv_out", tm=512, tt=s_len)
    recv["w_xattn_out"] = _weight_grad_scatter(obt, dyc, "dw_xattn_out", tm=512, tt=s_len)
    recv["w_kv"] = _weight_grad_scatter(dkv.T.astype(bf16), memb, "dw_kv", tm=D_MODEL, tt=MEM_LEN)
    dx, dproj = _in_proj_bwd(dgp, dconv, du, dq, dxp, full["w_in"])
    recv["w_in"] = _weight_grad_scatter(xbt, dproj, "dw_in", tm=512, tt=2048)
    sums = _sum_small(group_parts)
    group_sums = dict(zip([tuple(group) for group, _ in groups], [s.reshape(a.shape) for s, a in zip(sums, stacks)]))
    group_sums[("conv_w",)] = sums[-1][0:3]
    return loss[0, 0], dx, recv, [group_sums[tuple(group)] for group, _ in SMALL_GROUPS]


def kernel(x, mem, w_in, b_gate, conv_w, w_conv_out, ssm_lam_re, ssm_lam_im, ssm_log_dt, ssm_b_re, ssm_b_im, ssm_c_re, ssm_c_im, ssm_d, w_glu, w_kv, w_xattn_out, w_out, ln1_g, ln1_b, w_up, b_up, w_down, b_down, ln2_g, ln2_b, loss_target, m_w_in, m_b_gate, m_conv_w, m_w_conv_out, m_ssm_lam_re, m_ssm_lam_im, m_ssm_log_dt, m_ssm_b_re, m_ssm_b_im, m_ssm_c_re, m_ssm_c_im, m_ssm_d, m_w_glu, m_w_kv, m_w_xattn_out, m_w_out, m_ln1_g, m_ln1_b, m_w_up, m_b_up, m_w_down, m_b_down, m_ln2_g, m_ln2_b, v_w_in, v_b_gate, v_conv_w, v_w_conv_out, v_ssm_lam_re, v_ssm_lam_im, v_ssm_log_dt, v_ssm_b_re, v_ssm_b_im, v_ssm_c_re, v_ssm_c_im, v_ssm_d, v_w_glu, v_w_kv, v_w_xattn_out, v_w_out, v_ln1_g, v_ln1_b, v_w_up, v_b_up, v_w_down, v_b_down, v_ln2_g, v_ln2_b):
    w = dict(w_in=w_in, b_gate=b_gate, conv_w=conv_w, w_conv_out=w_conv_out, ssm_lam_re=ssm_lam_re,
             ssm_lam_im=ssm_lam_im, ssm_log_dt=ssm_log_dt, ssm_b_re=ssm_b_re, ssm_b_im=ssm_b_im, ssm_c_re=ssm_c_re,
             ssm_c_im=ssm_c_im, ssm_d=ssm_d, w_glu=w_glu, w_kv=w_kv, w_xattn_out=w_xattn_out, w_out=w_out,
             ln1_g=ln1_g, ln1_b=ln1_b, w_up=w_up, b_up=b_up, w_down=w_down, b_down=b_down, ln2_g=ln2_g, ln2_b=ln2_b)
    m = dict(w_in=m_w_in, b_gate=m_b_gate, conv_w=m_conv_w, w_conv_out=m_w_conv_out, ssm_lam_re=m_ssm_lam_re,
             ssm_lam_im=m_ssm_lam_im, ssm_log_dt=m_ssm_log_dt, ssm_b_re=m_ssm_b_re, ssm_b_im=m_ssm_b_im,
             ssm_c_re=m_ssm_c_re, ssm_c_im=m_ssm_c_im, ssm_d=m_ssm_d, w_glu=m_w_glu, w_kv=m_w_kv,
             w_xattn_out=m_w_xattn_out, w_out=m_w_out, ln1_g=m_ln1_g, ln1_b=m_ln1_b, w_up=m_w_up, b_up=m_b_up,
             w_down=m_w_down, b_down=m_b_down, ln2_g=m_ln2_g, ln2_b=m_ln2_b)
    v = dict(w_in=v_w_in, b_gate=v_b_gate, conv_w=v_conv_w, w_conv_out=v_w_conv_out, ssm_lam_re=v_ssm_lam_re,
             ssm_lam_im=v_ssm_lam_im, ssm_log_dt=v_ssm_log_dt, ssm_b_re=v_ssm_b_re, ssm_b_im=v_ssm_b_im,
             ssm_c_re=v_ssm_c_re, ssm_c_im=v_ssm_c_im, ssm_d=v_ssm_d, w_glu=v_w_glu, w_kv=v_w_kv,
             w_xattn_out=v_w_xattn_out, w_out=v_w_out, ln1_g=v_ln1_g, ln1_b=v_ln1_b, w_up=v_w_up, b_up=v_b_up,
             w_down=v_w_down, b_down=v_b_down, ln2_g=v_ln2_g, ln2_b=v_ln2_b)
    out_shapes = {k: a.shape for k, a in w.items()}
    swapped = ("w_in", "ssm_b_re", "ssm_b_im")

    def shard2d(k, a):
        if k in swapped:
            a = jnp.swapaxes(a, -1, -2)
        if k in SMALL:
            return a.reshape((3, CONV_W // N_DEV) if k == "conv_w" else SMALL[k])
        return a[0]

    def result(k, a):
        if k in swapped:
            shape = out_shapes[k]
            return jnp.swapaxes(a.reshape(shape[:-2] + (shape[-1], shape[-2])), -1, -2)
        return a.reshape(out_shapes[k])

    w, m, v = ({k: shard2d(k, a) for k, a in d.items()} for d in (w, m, v))

    shards = {k: w[k].T.astype(bf16) if k in GATHER_TRANSPOSED else w[k].astype(bf16) for k in BIG}
    conv_pad = jnp.pad(w["conv_w"], ((0, 5), (0, LANES - CONV_W // N_DEV)))
    early = ["w_in", "w_kv"]
    gathered = _all_gather([shards[k] for k in early] + [conv_pad], "gather_weights")
    full = {k: a.reshape(-1, a.shape[-1]) for k, a in zip(early, gathered[:-1])}
    late = {k: shards[k] for k in BIG if k not in early}
    conv_full = gathered[-1][:, :3, :CONV_W // N_DEV].transpose(1, 0, 2).reshape(3, CONV_W)
    small = {k: (conv_full if k == "conv_w" else w[k]) for k in SMALL}

    loss, dx, recv, group_sums = _local_step(x[0], mem[0], loss_target[0], full, late, small)

    grads, deltas, new_m, new_v = {}, {}, {}, {}
    for k in BIG:
        res = _adamw_update(w[k], m[k], v[k], recv[k], "adamw_" + k, transposed=k in PARTS_TRANSPOSED)
        grads[k], deltas[k], new_m[k], new_v[k] = res

    widen = lambda k, a: jnp.tile(a, (1, N_DEV)) if k == "conv_w" else a
    res = _adamw_small(small, {k: widen(k, m[k]) for k in SMALL}, {k: widen(k, v[k]) for k in SMALL}, group_sums)
    dev = _slot(_mesh_place())
    for d, small_res in zip((grads, deltas, new_m, new_v), res):
        for k, a in small_res.items():
            if k == "conv_w":
                a = lax.dynamic_slice_in_dim(a, dev * (CONV_W // N_DEV), CONV_W // N_DEV, axis=1)
            d[k] = a

    loss = lax.psum(loss, ("x", "y", "c"))
    outs = [loss, dx[None]]
    for d in (grads, deltas, new_m, new_v):
        outs += [result(k, d[k]) for k in WEIGHTS]
    return tuple(outs)
```

```python
import functools
import math

import jax
import jax.numpy as jnp
from jax import lax
from jax.experimental import pallas as pl
from jax.experimental.pallas import tpu as pltpu

f32 = jnp.float32
bf16 = jnp.bfloat16

D_MODEL = 1024
MEM_LEN = 256
GATE_COLS = 3 * D_MODEL
CONV_W = 512
SSM_W = 512
XATTN_W = 512
HEADS = 4
HEAD_DIM = 128
D_FF = 4096
IN_COLS = GATE_COLS + 3 * CONV_W + SSM_W + XATTN_W
SSM_GROUPS = 32
SSM_GROUP = 16
SSM_STATE = 64
N_STATE = SSM_GROUPS * SSM_STATE
ALPHA = 2.0 ** 0.25
LN_EPS = 1e-5
N_DEV = 8

ADAM_LR = 0.001
ADAM_B1 = 0.9
ADAM_B2 = 0.999
ADAM_EPS = 1e-08
ADAM_WD = 0.01
ADAM_STEP = 10

VMEM_LIMIT_V7X = 56 * 2 ** 20
SUBLANES = 8
LANES = 128

TOKEN_TILE = 256
SSM_BLOCK = 256
SSM_SEG = SSM_BLOCK // SUBLANES
LANE_CHUNK = 512
N_HALF = 2
HALF_W = SSM_W // N_HALF
HALF_STATE = N_STATE // N_HALF
HALF_COLS = 2 * HALF_STATE

NT = (((1,), (1,)), ((), ()))
TN = (((0,), (0,)), ((), ()))
NN = (((1,), (0,)), ((), ()))


def _dot(a, b, dims=NN):
    return lax.dot_general(a, b, dims, preferred_element_type=f32)


def _cparams(sem=None):
    return pltpu.CompilerParams(dimension_semantics=sem, vmem_limit_bytes=VMEM_LIMIT_V7X)


def _row_spec(tm, cols, rev_n=None):
    if rev_n is None:
        return pl.BlockSpec((tm, cols), lambda i: (i, 0))
    return pl.BlockSpec((tm, cols), lambda i: (rev_n - 1 - i, 0))


def _col_spec(rows, tm):
    return pl.BlockSpec((rows, tm), lambda i: (0, i))


def _const_spec(shape):
    nd = len(shape)
    return pl.BlockSpec(shape, lambda *_: (0,) * nd, pipeline_mode=pl.Buffered(1))


def _acc_spec(shape):
    nd = len(shape)
    return pl.BlockSpec(shape, lambda *_: (0,) * nd)


def _sds(shape, dtype):
    return jax.ShapeDtypeStruct(shape, dtype)


def _gelu(x):
    c = math.sqrt(2.0 / math.pi)
    return 0.5 * x * (1.0 + jnp.tanh(c * (x + 0.044715 * x * x * x)))


def _gelu_grad(x):
    c = math.sqrt(2.0 / math.pi)
    t = jnp.tanh(c * (x + 0.044715 * x * x * x))
    return 0.5 * (1.0 + t) + 0.5 * x * (1.0 - t * t) * c * (1.0 + 3.0 * 0.044715 * x * x)


def _colsum(a):
    return jnp.sum(a, axis=0, keepdims=True)


def _mesh_place():
    return lax.axis_index("x"), lax.axis_index("y"), lax.axis_index("c")


def _slot(p):
    return 4 * p[0] + 2 * p[1] + p[2]


def _other_devices(me):
    x, y, c = me
    flip = lambda v, d: 1 - v if d else v
    return [(flip(x, dx), flip(y, dy), flip(c, dc)) for dx in (0, 1) for dy in (0, 1) for dc in (0, 1)][1:]


def _all_gather(blocks, name):
    n = len(blocks)

    def body(*refs):
        ins, outs = refs[:n], refs[n:2 * n]
        send_sems, recv_sems, local_sems = refs[2 * n:]
        x, y, c = _mesh_place()
        me, sibling = (x, y, c), (x, y, 1 - c)
        chips = [(1 - x, y), (x, 1 - y), (1 - x, 1 - y)]

        def copy(a, k, block, to, src=None):
            rows = outs[a].at[_slot(block)]
            return pltpu.make_async_remote_copy(
                src_ref=rows if src is None else src, dst_ref=rows,
                send_sem=send_sems.at[a, k], recv_sem=recv_sems.at[a, k],
                device_id=to, device_id_type=pl.DeviceIdType.MESH)

        mine = [pltpu.make_async_copy(ins[a], outs[a].at[_slot(me)], local_sems.at[a]) for a in range(n)]
        for cp in mine:
            cp.start()
        first = []
        for a in range(n):
            first.append(copy(a, 0, me, sibling, src=ins[a]))
            first += [copy(a, 1 + j, me, (*chip, c), src=ins[a]) for j, chip in enumerate(chips)]
        for cp in first:
            cp.start()
        passed = []
        for a in range(n):
            for j, chip in enumerate(chips):
                copy(a, 1 + j, (*chip, c), me).wait_recv()
                fwd = copy(a, 4 + j, (*chip, c), sibling)
                fwd.start()
                passed.append(fwd)
        for a in range(n):
            copy(a, 0, sibling, me).wait_recv()
            for j, chip in enumerate(chips):
                copy(a, 4 + j, (*chip, 1 - c), me).wait_recv()
        for cp in first + passed:
            cp.wait_send()
        for cp in mine:
            cp.wait()

    any_spec = pl.BlockSpec(memory_space=pl.ANY)
    return pl.pallas_call(
        body, name=name,
        out_shape=[_sds((N_DEV,) + b.shape, b.dtype) for b in blocks],
        in_specs=[any_spec] * n, out_specs=[any_spec] * n,
        scratch_shapes=[pltpu.SemaphoreType.DMA((n, 7)), pltpu.SemaphoreType.DMA((n, 7)),
                        pltpu.SemaphoreType.DMA((n,))],
    )(*blocks)


def _side_gather_copies(ins, outs, send_sems, recv_sems, local_sems):
    me = _mesh_place()
    copies = []
    for a, (src, dst) in enumerate(zip(ins, outs)):
        copies.append(pltpu.make_async_copy(src, dst.at[_slot(me)], local_sems.at[a]))
        for k, peer in enumerate(_other_devices(me)):
            copies.append(pltpu.make_async_remote_copy(
                src_ref=src, dst_ref=dst.at[_slot(me)], send_sem=send_sems.at[a, k], recv_sem=recv_sems.at[a, k],
                device_id=peer, device_id_type=pl.DeviceIdType.MESH))
    return copies


def _side_gather_specs(blocks):
    n = len(blocks)
    any_spec = pl.BlockSpec(memory_space=pl.ANY)
    return ([any_spec] * n, [_sds((N_DEV,) + b.shape, b.dtype) for b in blocks],
            [pltpu.SemaphoreType.DMA((n, N_DEV - 1)), pltpu.SemaphoreType.DMA((n, N_DEV - 1)),
             pltpu.SemaphoreType.DMA((n,))])


def _kv_proj(mem, w_kv):
    def body(mem_ref, w_ref, kv_ref, kt_ref, memb_ref):
        mb = mem_ref[...].astype(bf16)
        memb_ref[...] = mb
        kv = _dot(mb, w_ref[...]).astype(bf16)
        kv_ref[...] = kv
        kt_ref[...] = kv[:, :XATTN_W].T

    return pl.pallas_call(
        body, name="kv_proj",
        out_shape=[_sds((MEM_LEN, 2 * XATTN_W), bf16), _sds((XATTN_W, MEM_LEN), bf16), _sds((MEM_LEN, D_MODEL), bf16)],
        compiler_params=_cparams(),
    )(mem, w_kv)


def _attention_probs(qb, kv_ref, h):
    kh = kv_ref[:, h * HEAD_DIM:(h + 1) * HEAD_DIM]
    s = _dot(qb[:, h * HEAD_DIM:(h + 1) * HEAD_DIM], kh, NT) * (HEAD_DIM ** -0.5)
    e = jnp.exp(s - jnp.max(s, axis=-1, keepdims=True))
    return e / jnp.sum(e, axis=-1, keepdims=True)


def _in_proj(x, w_in_t, b_gate, conv_w, kv, side_blocks):
    s_len = x.shape[0]
    tm = 2 * TOKEN_TILE
    n = s_len // tm
    ns = len(side_blocks)
    side_in_specs, side_shapes, side_sems = _side_gather_specs(side_blocks)

    def body(*refs):
        (x_ref, win_ref, bg_ref, cw_ref, kv_ref) = refs[:5]
        side_ins = refs[5:5 + ns]
        (xbt_ref, g_ref, cin_ref, u_ref, q_ref, ain_ref, o_ref, aint_ref, ot_ref) = refs[5 + ns:14 + ns]
        side_outs = refs[14 + ns:14 + 2 * ns]
        zs_ref = refs[14 + 2 * ns]
        side = _side_gather_copies(side_ins, side_outs, *refs[15 + 2 * ns:])
        i = pl.program_id(0)

        @pl.when(i == 0)
        def _():
            for cp in side:
                cp.start()

        xb = x_ref[...].astype(bf16)
        xbt_ref[...] = xb.T
        proj = _dot(xb, win_ref[...], NT)
        g_ref[...] = jax.nn.sigmoid(proj[:, :GATE_COLS] + bg_ref[...]).astype(bf16)
        cin = proj[:, GATE_COLS:GATE_COLS + 3 * CONV_W]
        cin_ref[...] = cin
        u_ref[...] = proj[:, GATE_COLS + 3 * CONV_W:GATE_COLS + 3 * CONV_W + SSM_W]
        qb = proj[:, IN_COLS - XATTN_W:].astype(bf16)
        q_ref[...] = qb

        cb, cc, ch = cin[:, :CONV_W], cin[:, CONV_W:2 * CONV_W], cin[:, 2 * CONV_W:]
        z = cc * ch

        @pl.when(i == 0)
        def _():
            zs_ref[0:8, :] = jnp.zeros((8, CONV_W), f32)

        zs_ref[8:8 + tm, :] = z
        z1 = zs_ref[pl.ds(7, tm), :]
        z2 = zs_ref[pl.ds(6, tm), :]
        cw = cw_ref[...]
        cz = cw[0:1] * z2 + cw[1:2] * z1 + cw[2:3] * z
        zs_ref[0:8, :] = zs_ref[tm:tm + 8, :]
        ain = (cb * cz).astype(bf16)
        ain_ref[...] = ain
        aint_ref[...] = ain.T

        probs = [_attention_probs(qb, kv_ref, h) for h in range(HEADS)]
        outs = [_dot(probs[h].astype(bf16), kv_ref[:, XATTN_W + h * HEAD_DIM:XATTN_W + (h + 1) * HEAD_DIM])
                for h in range(HEADS)]
        ob = jnp.concatenate(outs, axis=1).astype(bf16)
        o_ref[...] = ob
        ot_ref[...] = ob.T

        @pl.when(i == n - 1)
        def _():
            for cp in side:
                cp.wait()

    row_cols = [(GATE_COLS, bf16), (3 * CONV_W, f32), (SSM_W, f32), (XATTN_W, bf16), (CONV_W, bf16), (XATTN_W, bf16)]
    t_rows = [D_MODEL, CONV_W, XATTN_W]
    outs = pl.pallas_call(
        body, name="in_proj", grid=(n,),
        in_specs=[_row_spec(tm, D_MODEL), _const_spec((IN_COLS, D_MODEL)), _const_spec((1, GATE_COLS)),
                  _const_spec((3, CONV_W)), _const_spec((MEM_LEN, 2 * XATTN_W))] + side_in_specs,
        out_specs=([_col_spec(t_rows[0], tm)] + [_row_spec(tm, c) for c, _ in row_cols]
                   + [_col_spec(t_rows[1], tm), _col_spec(t_rows[2], tm)] + side_in_specs),
        out_shape=([_sds((t_rows[0], s_len), bf16)] + [_sds((s_len, c), dt) for c, dt in row_cols]
                   + [_sds((t_rows[1], s_len), bf16), _sds((t_rows[2], s_len), bf16)] + side_shapes),
        scratch_shapes=[pltpu.VMEM((tm + 8, CONV_W), f32)] + side_sems,
        compiler_params=_cparams(("arbitrary",)),
    )(x, w_in_t, b_gate, conv_w, kv, *side_blocks)
    return outs[:9], outs[9:]


def _state_cols(chunk):
    half, off = divmod(chunk * LANE_CHUNK, HALF_STATE)
    lo = half * HALF_COLS + off
    return slice(lo, lo + LANE_CHUNK), slice(lo + HALF_STATE, lo + HALF_STATE + LANE_CHUNK)


def _half_cols(half):
    lo = half * HALF_COLS
    return slice(lo, lo + HALF_STATE), slice(lo + HALF_STATE, lo + HALF_COLS)


def _rows_to_segments(src_ref, stage_ref, dst_ref):
    nc = SSM_W // LANES
    for c in range(nc):
        stage_ref[c] = src_ref[:, c * LANES:(c + 1) * LANES]
    for c in range(nc):
        for k in range(SSM_SEG):
            dst_ref[k * SUBLANES:(k + 1) * SUBLANES, c * LANES:(c + 1) * LANES] = (
                stage_ref[c, pl.ds(k, SUBLANES, stride=SSM_SEG), :])


def _rows_from_segments(src_ref, stage_ref, dst_ref):
    nc = SSM_W // LANES
    for c in range(nc):
        for k in range(SSM_SEG):
            stage_ref[c, pl.ds(k, SUBLANES, stride=SSM_SEG), :] = (
                src_ref[k * SUBLANES:(k + 1) * SUBLANES, c * LANES:(c + 1) * LANES])
    for c in range(nc):
        dst_ref[:, c * LANES:(c + 1) * LANES] = stage_ref[c]


def _ssm_scan(s_ref, pw_ref, init_ref, reverse, unroll):
    for chunk in range(N_STATE // LANE_CHUNK):
        re, im = _state_cols(chunk)
        ar = jnp.broadcast_to(pw_ref[0:1, re], (SUBLANES, LANE_CHUNK))
        ai = jnp.broadcast_to(pw_ref[0:1, im], (SUBLANES, LANE_CHUNK))
        if reverse:
            ai = -ai

        def step(j, carry, re=re, im=im, ar=ar, ai=ai):
            sr, si = carry
            k = (SSM_SEG - 1 - j) if reverse else j
            r0 = pl.multiple_of(k * SUBLANES, SUBLANES)
            nr = ar * sr - ai * si + s_ref[pl.ds(r0, SUBLANES), re]
            ni = ar * si + ai * sr + s_ref[pl.ds(r0, SUBLANES), im]
            s_ref[pl.ds(r0, SUBLANES), re] = nr
            s_ref[pl.ds(r0, SUBLANES), im] = ni
            return nr, ni

        if init_ref is None:
            init = (jnp.zeros((SUBLANES, LANE_CHUNK), f32),) * 2
        else:
            init = (init_ref[:, re], init_ref[:, im])
        lax.fori_loop(0, SSM_SEG, step, init, unroll=unroll)


def _ssm_add_carry(s_ref, pw_ref, cm_ref, reverse):
    for chunk in range(N_STATE // LANE_CHUNK):
        re, im = _state_cols(chunk)
        cr, ci = cm_ref[:, re], cm_ref[:, im]
        for k in range(SSM_SEG):
            pk = (SSM_SEG - 1 - k) if reverse else k
            pr = pw_ref[pk:pk + 1, re]
            pi = pw_ref[pk:pk + 1, im]
            if reverse:
                pi = -pi
            rows = slice(k * SUBLANES, (k + 1) * SUBLANES)
            s_ref[rows, re] = s_ref[rows, re] + (pr * cr - pi * ci)
            s_ref[rows, im] = s_ref[rows, im] + (pr * ci + pi * cr)


def _ssm_carries(first_row, s_ref, pw_ref, carry_ref, cm_ref, reverse):
    order = range(SUBLANES - 1, -1, -1) if reverse else range(SUBLANES)
    for half in range(N_HALF):
        re, im = _half_cols(half)
        a_r, a_i = pw_ref[SSM_SEG - 1:SSM_SEG, re], pw_ref[SSM_SEG - 1:SSM_SEG, im]
        if reverse:
            a_i = -a_i
        cr, ci = carry_ref[0:1, re], carry_ref[0:1, im]
        for seg in order:
            cm_ref[seg:seg + 1, re] = cr
            cm_ref[seg:seg + 1, im] = ci
            er = s_ref[first_row + seg:first_row + seg + 1, re]
            ei = s_ref[first_row + seg:first_row + seg + 1, im]
            cr, ci = a_r * cr - a_i * ci + er, a_r * ci + a_i * cr + ei
        carry_ref[0:1, re] = cr
        carry_ref[0:1, im] = ci


def _ssm_fwd(u, b_half, c_half, pw, d_skip, side_blocks):
    s_len = u.shape[0]
    tb = SSM_BLOCK
    n = s_len // tb
    ns = len(side_blocks)
    side_in_specs, side_shapes, side_sems = _side_gather_specs(side_blocks)

    def body(*refs):
        u_ref, b_ref, c_ref, pw_ref, d_ref = refs[:5]
        side_ins = refs[5:5 + ns]
        y_ref, cm_ref = refs[5 + ns:7 + ns]
        side_outs = refs[7 + ns:7 + 2 * ns]
        s_ref, carry_ref, up_ref, yp_ref, stage_ref = refs[7 + 2 * ns:12 + 2 * ns]
        side = _side_gather_copies(side_ins, side_outs, *refs[12 + 2 * ns:])
        i = pl.program_id(0)

        @pl.when(i == 0)
        def _():
            carry_ref[...] = jnp.zeros_like(carry_ref)
            for cp in side:
                cp.start()

        _rows_to_segments(u_ref, stage_ref, up_ref)
        u = up_ref[...]
        ub = u.astype(bf16)
        for half in range(N_HALF):
            s_ref[:, half * HALF_COLS:(half + 1) * HALF_COLS] = _dot(ub[:, half * HALF_W:(half + 1) * HALF_W], b_ref[half])
        _ssm_scan(s_ref, pw_ref, None, reverse=False, unroll=4)
        _ssm_carries(tb - SUBLANES, s_ref, pw_ref, carry_ref, cm_ref, reverse=False)
        _ssm_add_carry(s_ref, pw_ref, cm_ref, reverse=False)
        for half in range(N_HALF):
            cols = slice(half * HALF_W, (half + 1) * HALF_W)
            sb = s_ref[:, half * HALF_COLS:(half + 1) * HALF_COLS].astype(bf16)
            yp_ref[:, cols] = _dot(sb, c_ref[half]) + d_ref[:, cols] * u[:, cols]
        _rows_from_segments(yp_ref, stage_ref, y_ref)

        @pl.when(i == n - 1)
        def _():
            for cp in side:
                cp.wait()

    outs = pl.pallas_call(
        body, name="ssm_fwd", grid=(n,),
        in_specs=[_row_spec(tb, SSM_W), _const_spec((N_HALF, HALF_W, HALF_COLS)), _const_spec((N_HALF, HALF_COLS, HALF_W)),
                  _const_spec((SSM_SEG, 2 * N_STATE)), _const_spec((1, SSM_W))] + side_in_specs,
        out_specs=[_row_spec(tb, SSM_W), _row_spec(SUBLANES, 2 * N_STATE)] + side_in_specs,
        out_shape=[_sds((s_len, SSM_W), f32), _sds((n * SUBLANES, 2 * N_STATE), f32)] + side_shapes,
        scratch_shapes=[pltpu.VMEM((tb, 2 * N_STATE), f32), pltpu.VMEM((SUBLANES, 2 * N_STATE), f32),
                        pltpu.VMEM((tb, SSM_W), f32), pltpu.VMEM((tb, SSM_W), f32),
                        pltpu.VMEM((SSM_W // LANES, tb, LANES), f32)] + side_sems,
        compiler_params=_cparams(("arbitrary",)),
    )(u, b_half, c_half, pw, d_skip, *side_blocks)
    return outs[0], outs[1], outs[2:]


def _layer_norm_fwd(r, g, b):
    mu = jnp.mean(r, axis=-1, keepdims=True)
    var = jnp.mean(jnp.square(r - mu), axis=-1, keepdims=True)
    rstd = lax.rsqrt(var + LN_EPS)
    xhat = (r - mu) * rstd
    return xhat, rstd, xhat * g + b


def _layer_norm_bwd(dy, xhat, rstd, g):
    dxh = dy * g
    m1 = jnp.mean(dxh, axis=-1, keepdims=True)
    m2 = jnp.mean(dxh * xhat, axis=-1, keepdims=True)
    return rstd * (dxh - m1 - xhat * m2)


def _branch_outputs(ys_ref, ain_ref, o_ref, wglu_ref, wco_ref, wxo_ref):
    ysb = _gelu(ys_ref[...]).astype(bf16)
    glu = _dot(ysb, wglu_ref[...], NT)
    ga, sb = glu[:, :D_MODEL], jax.nn.sigmoid(glu[:, D_MODEL:])
    ya = _dot(ain_ref[...], wco_ref[...], NT)
    yc = _dot(o_ref[...], wxo_ref[...], NT)
    return ysb, ga, sb, ya, ga * sb, yc


def _mid_fwd(y_ssm, g, ain, ob, x, w_glu_t, w_co_t, w_xo_t, w_out, ln1_g, ln1_b):
    s_len = x.shape[0]
    tm = TOKEN_TILE
    n = s_len // tm

    def body(ys_ref, g_ref, ain_ref, o_ref, x_ref, wglu_ref, wco_ref, wxo_ref, wout_ref, lg_ref, lb_ref,
             ysbt_ref, mb_ref, xhat_ref, rstd_ref):
        ysb, _, _, ya, yb, yc = _branch_outputs(ys_ref, ain_ref, o_ref, wglu_ref, wco_ref, wxo_ref)
        ysbt_ref[...] = ysb.T
        gt = g_ref[...].astype(f32)
        merged = gt[:, :D_MODEL] * ya + gt[:, D_MODEL:2 * D_MODEL] * yb + gt[:, 2 * D_MODEL:] * yc
        mb = merged.astype(bf16)
        mb_ref[...] = mb
        r1 = ALPHA * x_ref[...] + _dot(mb, wout_ref[...])
        xhat, rstd, _ = _layer_norm_fwd(r1, lg_ref[...], lb_ref[...])
        xhat_ref[...] = xhat
        rstd_ref[...] = rstd

    row_cols = [(D_MODEL, bf16), (D_MODEL, f32), (1, f32)]
    return pl.pallas_call(
        body, name="mid_fwd", grid=(n,),
        in_specs=[_row_spec(tm, SSM_W), _row_spec(tm, GATE_COLS), _row_spec(tm, CONV_W), _row_spec(tm, XATTN_W),
                  _row_spec(tm, D_MODEL), _const_spec((2 * D_MODEL, SSM_W)), _const_spec((D_MODEL, CONV_W)),
                  _const_spec((D_MODEL, XATTN_W)), _const_spec((D_MODEL, D_MODEL)),
                  _const_spec((1, D_MODEL)), _const_spec((1, D_MODEL))],
        out_specs=[_col_spec(SSM_W, tm)] + [_row_spec(tm, c) for c, _ in row_cols],
        out_shape=[_sds((SSM_W, s_len), bf16)] + [_sds((s_len, c), dt) for c, dt in row_cols],
        compiler_params=_cparams(("parallel",)),
    )(y_ssm, g, ain, ob, x, w_glu_t, w_co_t, w_xo_t, w_out, ln1_g, ln1_b)


def _mlp_fwd_bwd(xhat1, tgt, ln1_g, ln1_b, w_up_t, b_up, w_down, b_down, ln2_g, ln2_b):
    s_len = xhat1.shape[0]
    tm = TOKEN_TILE
    n = s_len // tm
    fc = 1024
    nfc = D_FF // fc

    def body(xh_ref, t_ref, l1g_ref, l1b_ref, wup_ref, bup_ref, wdn_ref, bdn_ref, l2g_ref, l2b_ref,
             x1bt_ref, hdn_ref, dr2bt_ref, dpre_ref, dx1_ref,
             loss_ref, dl2g_ref, dl2b_ref, dbdn_ref, dbup_ref, rl_ref):
        i = pl.program_id(0)

        @pl.when(i == 0)
        def _():
            loss_ref[...] = jnp.zeros_like(loss_ref)
            dl2g_ref[...] = jnp.zeros_like(dl2g_ref)
            dl2b_ref[...] = jnp.zeros_like(dl2b_ref)
            dbdn_ref[...] = jnp.zeros_like(dbdn_ref)
            dbup_ref[...] = jnp.zeros_like(dbup_ref)

        x1 = xh_ref[...] * l1g_ref[...] + l1b_ref[...]
        x1b = x1.astype(bf16)
        x1bt_ref[...] = x1b.T
        chunks = [slice(c * fc, (c + 1) * fc) for c in range(nfc)]
        pres = [_dot(x1b, wup_ref[cols, :], NT) for cols in chunks]
        hbs = []
        for cols, pre in zip(chunks, pres):
            rl = jnp.maximum(pre + bup_ref[:, cols], 0.0)
            rl_ref[:, cols] = rl
            hb = (rl * rl).astype(bf16)
            hdn_ref[:, cols] = hb
            hbs.append(hb)
        acc = _dot(hbs[0], wdn_ref[chunks[0], :])
        for cols, hb in zip(chunks[1:], hbs[1:]):
            acc = acc + _dot(hb, wdn_ref[cols, :])
        r2 = ALPHA * x1 + acc + bdn_ref[...]
        xhat2, rstd2, y = _layer_norm_fwd(r2, l2g_ref[...], l2b_ref[...])
        err = y - t_ref[...]
        loss_ref[...] += jnp.sum(jnp.sum(err * err, axis=1, keepdims=True), axis=0, keepdims=True) * (0.5 / D_MODEL)
        dy = err * (1.0 / D_MODEL)
        dl2g_ref[...] += _colsum(dy * xhat2)
        dl2b_ref[...] += _colsum(dy)
        dr2 = _layer_norm_bwd(dy, xhat2, rstd2, l2g_ref[...])
        dbdn_ref[...] += _colsum(dr2)
        dr2b = dr2.astype(bf16)
        dr2bt_ref[...] = dr2b.T
        dhs = [_dot(dr2b, wdn_ref[cols, :], NT) for cols in chunks]
        dpbs = []
        for cols, dh in zip(chunks, dhs):
            dpre = dh * (2.0 * rl_ref[:, cols])
            dbup_ref[:, cols] += _colsum(dpre)
            dpb = dpre.astype(bf16)
            dpre_ref[:, cols] = dpb
            dpbs.append(dpb)
        dacc = _dot(dpbs[0], wup_ref[chunks[0], :])
        for cols, dpb in zip(chunks[1:], dpbs[1:]):
            dacc = dacc + _dot(dpb, wup_ref[cols, :])
        dx1_ref[...] = ALPHA * dr2 + dacc

    acc_shapes = [(1, LANES), (1, D_MODEL), (1, D_MODEL), (1, D_MODEL), (1, D_FF)]
    return pl.pallas_call(
        body, name="mlp_fwd_bwd", grid=(n,),
        in_specs=[_row_spec(tm, D_MODEL), _row_spec(tm, D_MODEL), _const_spec((1, D_MODEL)), _const_spec((1, D_MODEL)),
                  _const_spec((D_FF, D_MODEL)), _const_spec((1, D_FF)), _const_spec((D_FF, D_MODEL)),
                  _const_spec((1, D_MODEL)), _const_spec((1, D_MODEL)), _const_spec((1, D_MODEL))],
        out_specs=([_col_spec(D_MODEL, tm), _row_spec(tm, D_FF), _col_spec(D_MODEL, tm), _row_spec(tm, D_FF),
                    _row_spec(tm, D_MODEL)] + [_acc_spec(s) for s in acc_shapes]),
        out_shape=([_sds((D_MODEL, s_len), bf16), _sds((s_len, D_FF), bf16), _sds((D_MODEL, s_len), bf16),
                    _sds((s_len, D_FF), bf16), _sds((s_len, D_MODEL), f32)] + [_sds(s, f32) for s in acc_shapes]),
        scratch_shapes=[pltpu.VMEM((tm, D_FF), f32)],
        compiler_params=_cparams(("arbitrary",)),
    )(xhat1, tgt, ln1_g, ln1_b, w_up_t, b_up, w_down, b_down, ln2_g, ln2_b)


def _mid_bwd(dx1, xhat1, rstd1, g, ain, ob, y_ssm, ln1_g, w_out, w_glu_t, w_co_t, w_xo_t):
    s_len = dx1.shape[0]
    tm = TOKEN_TILE
    n = s_len // tm

    def body(dx1_ref, xh_ref, rs_ref, g_ref, ain_ref, o_ref, ys_ref, lg_ref, wout_ref, wglu_ref, wco_ref, wxo_ref,
             dxp_ref, dr1bt_ref, dgp_ref, dya_ref, dyc_ref, dglu_ref, dyssm_ref,
             dl1g_ref, dl1b_ref, dbg_ref):
        i = pl.program_id(0)

        @pl.when(i == 0)
        def _():
            dl1g_ref[...] = jnp.zeros_like(dl1g_ref)
            dl1b_ref[...] = jnp.zeros_like(dl1b_ref)
            dbg_ref[...] = jnp.zeros_like(dbg_ref)

        dx1 = dx1_ref[...]
        xhat = xh_ref[...]
        dl1g_ref[...] += _colsum(dx1 * xhat)
        dl1b_ref[...] += _colsum(dx1)
        dr1 = _layer_norm_bwd(dx1, xhat, rs_ref[...], lg_ref[...])
        dxp_ref[...] = ALPHA * dr1
        dr1b = dr1.astype(bf16)
        dr1bt_ref[...] = dr1b.T
        dm = _dot(dr1b, wout_ref[...], NT)

        _, ga, sb, ya, yb, yc = _branch_outputs(ys_ref, ain_ref, o_ref, wglu_ref, wco_ref, wxo_ref)
        gt = g_ref[...].astype(f32)
        branch = (ya, yb, yc)
        for j in range(3):
            cols = slice(j * D_MODEL, (j + 1) * D_MODEL)
            gj = gt[:, cols]
            dgp = dm * branch[j] * gj * (1.0 - gj)
            dbg_ref[:, cols] += _colsum(dgp)
            dgp_ref[:, cols] = dgp.astype(bf16)
        dya_ref[...] = (dm * gt[:, :D_MODEL]).astype(bf16)
        dyc_ref[...] = (dm * gt[:, 2 * D_MODEL:]).astype(bf16)
        dyb = dm * gt[:, D_MODEL:2 * D_MODEL]
        dga = (dyb * sb).astype(bf16)
        dgb = (dyb * ga * sb * (1.0 - sb)).astype(bf16)
        dglu_ref[:, :D_MODEL] = dga
        dglu_ref[:, D_MODEL:] = dgb
        dys = _dot(dga, wglu_ref[:D_MODEL, :]) + _dot(dgb, wglu_ref[D_MODEL:, :])
        dyssm_ref[...] = dys * _gelu_grad(ys_ref[...])

    row_cols = [(GATE_COLS, bf16), (D_MODEL, bf16), (D_MODEL, bf16), (2 * D_MODEL, bf16), (SSM_W, f32)]
    acc_shapes = [(1, D_MODEL), (1, D_MODEL), (1, GATE_COLS)]
    return pl.pallas_call(
        body, name="mid_bwd", grid=(n,),
        in_specs=[_row_spec(tm, D_MODEL), _row_spec(tm, D_MODEL), _row_spec(tm, 1), _row_spec(tm, GATE_COLS),
                  _row_spec(tm, CONV_W), _row_spec(tm, XATTN_W), _row_spec(tm, SSM_W),
                  _const_spec((1, D_MODEL)), _const_spec((D_MODEL, D_MODEL)), _const_spec((2 * D_MODEL, SSM_W)),
                  _const_spec((D_MODEL, CONV_W)), _const_spec((D_MODEL, XATTN_W))],
        out_specs=([_row_spec(tm, D_MODEL), _col_spec(D_MODEL, tm)] + [_row_spec(tm, c) for c, _ in row_cols]
                   + [_acc_spec(s) for s in acc_shapes]),
        out_shape=([_sds((s_len, D_MODEL), f32), _sds((D_MODEL, s_len), bf16)]
                   + [_sds((s_len, c), dt) for c, dt in row_cols] + [_sds(s, f32) for s in acc_shapes]),
        compiler_params=_cparams(("arbitrary",)),
    )(dx1, xhat1, rstd1, g, ain, ob, y_ssm, ln1_g, w_out, w_glu_t, w_co_t, w_xo_t)


def _ssm_bwd(u, dy, cm_all, b_half, c_half, pw, d_skip):
    s_len = u.shape[0]
    tb = SSM_BLOCK
    n = s_len // tb

    def body(u_ref, dy_ref, cm_ref, b_ref, c_ref, pw_ref, d_ref,
             du_ref, db_hbm, dc_hbm, da_ref, dd_ref,
             s_ref, g_ref, gcarry_ref, gcm_ref, db_ref, dc_ref, up_ref, dyp_ref, dup_ref, stage_ref):
        i = pl.program_id(0)

        @pl.when(i == 0)
        def _():
            gcarry_ref[...] = jnp.zeros_like(gcarry_ref)
            db_ref[...] = jnp.zeros_like(db_ref)
            dc_ref[...] = jnp.zeros_like(dc_ref)
            da_ref[...] = jnp.zeros_like(da_ref)
            dd_ref[...] = jnp.zeros_like(dd_ref)

        _rows_to_segments(u_ref, stage_ref, up_ref)
        _rows_to_segments(dy_ref, stage_ref, dyp_ref)
        u = up_ref[...]
        ub = u.astype(bf16)
        dy = dyp_ref[...]
        dyb = dy.astype(bf16)
        dd_ref[...] += _colsum(dy * u)

        for half in range(N_HALF):
            s_ref[:, half * HALF_COLS:(half + 1) * HALF_COLS] = _dot(ub[:, half * HALF_W:(half + 1) * HALF_W], b_ref[half])
        _ssm_scan(s_ref, pw_ref, cm_ref, reverse=False, unroll=True)

        for half in range(N_HALF):
            g_ref[:, half * HALF_COLS:(half + 1) * HALF_COLS] = _dot(dyb[:, half * HALF_W:(half + 1) * HALF_W], c_ref[half], NT)
        _ssm_scan(g_ref, pw_ref, None, reverse=True, unroll=True)
        _ssm_carries(0, g_ref, pw_ref, gcarry_ref, gcm_ref, reverse=True)
        _ssm_add_carry(g_ref, pw_ref, gcm_ref, reverse=True)

        for half in range(N_HALF):
            cols = slice(half * HALF_W, (half + 1) * HALF_W)
            scols = slice(half * HALF_COLS, (half + 1) * HALF_COLS)
            gb = g_ref[:, scols].astype(bf16)
            dup_ref[:, cols] = _dot(gb, b_ref[half], NT) + d_ref[:, cols] * dy[:, cols]
            db_ref[half] += _dot(ub[:, cols], gb, TN)
            dc_ref[half] += _dot(s_ref[:, scols].astype(bf16), dyb[:, cols], TN)
        _rows_from_segments(dup_ref, stage_ref, du_ref)

        for chunk in range(N_STATE // LANE_CHUNK):
            re, im = _state_cols(chunk)
            acc_r = da_ref[:, re]
            acc_i = da_ref[:, im]
            for k in range(SSM_SEG):
                rows = slice(k * SUBLANES, (k + 1) * SUBLANES)
                if k == 0:
                    pr, pi = cm_ref[:, re], cm_ref[:, im]
                else:
                    prev = slice((k - 1) * SUBLANES, k * SUBLANES)
                    pr, pi = s_ref[prev, re], s_ref[prev, im]
                gr, gi = g_ref[rows, re], g_ref[rows, im]
                acc_r = acc_r + (gr * pr + gi * pi)
                acc_i = acc_i + (gi * pr - gr * pi)
            da_ref[:, re] = acc_r
            da_ref[:, im] = acc_i

        @pl.when(i == n - 1)
        def _():
            pltpu.sync_copy(db_ref, db_hbm)
            pltpu.sync_copy(dc_ref, dc_hbm)

    rev = functools.partial(_row_spec, rev_n=n)
    any_spec = pl.BlockSpec(memory_space=pl.ANY)
    state_rows = pltpu.VMEM((tb, 2 * N_STATE), f32)
    seg_rows = pltpu.VMEM((SUBLANES, 2 * N_STATE), f32)
    tok_rows = pltpu.VMEM((tb, SSM_W), f32)
    return pl.pallas_call(
        body, name="ssm_bwd", grid=(n,),
        in_specs=[rev(tb, SSM_W), rev(tb, SSM_W), rev(SUBLANES, 2 * N_STATE),
                  _const_spec((N_HALF, HALF_W, HALF_COLS)), _const_spec((N_HALF, HALF_COLS, HALF_W)),
                  _const_spec((SSM_SEG, 2 * N_STATE)), _const_spec((1, SSM_W))],
        out_specs=[rev(tb, SSM_W), any_spec, any_spec, _acc_spec((SUBLANES, 2 * N_STATE)), _acc_spec((1, SSM_W))],
        out_shape=[_sds((s_len, SSM_W), f32), _sds((N_HALF, HALF_W, HALF_COLS), f32),
                   _sds((N_HALF, HALF_COLS, HALF_W), f32), _sds((SUBLANES, 2 * N_STATE), f32), _sds((1, SSM_W), f32)],
        scratch_shapes=[state_rows, state_rows, seg_rows, seg_rows,
                        pltpu.VMEM((N_HALF, HALF_W, HALF_COLS), f32), pltpu.VMEM((N_HALF, HALF_COLS, HALF_W), f32),
                        tok_rows, tok_rows, tok_rows, pltpu.VMEM((SSM_W // LANES, tb, LANES), f32)],
        compiler_params=_cparams(("arbitrary",)),
    )(u, dy, cm_all, b_half, c_half, pw, d_skip)


def _branch_bwd(dya, dyc, cin, q, kv, k_t, conv_w, w_co_t, w_xo_t, side_blocks):
    s_len = dya.shape[0]
    tm = TOKEN_TILE
    n = s_len // tm
    halo_blocks = tm // 8
    ns = len(side_blocks)
    conv_tile = _sds((8, CONV_W), f32)
    side_in_specs, side_shapes, side_sems = _side_gather_specs(list(side_blocks) + [conv_tile])

    def body(*refs):
        (dya_ref, dyc_ref, cin_ref, cprev_ref, q_ref, kv_ref, cw_ref, wco_ref, wxo_ref, kt_ref) = refs[:10]
        side_ins = refs[10:10 + ns]
        dconv_ref, dq_ref, dkv_ref = refs[10 + ns:13 + ns]
        side_outs = refs[13 + ns:14 + 2 * ns]
        zs_ref, dczs_ref, dcw_ref = refs[14 + 2 * ns:17 + 2 * ns]
        copies = _side_gather_copies(list(side_ins) + [dcw_ref], side_outs, *refs[17 + 2 * ns:])
        side, conv_side = copies[:ns * N_DEV], copies[ns * N_DEV:]
        i = pl.program_id(0)
        tile = n - 1 - i

        @pl.when(i == 0)
        def _():
            dcw_ref[...] = jnp.zeros_like(dcw_ref)
            dkv_ref[...] = jnp.zeros_like(dkv_ref)
            dczs_ref[tm:tm + 8, :] = jnp.zeros((8, CONV_W), f32)
            for cp in side:
                cp.start()

        cin = cin_ref[...]
        cb, cc, ch = cin[:, :CONV_W], cin[:, CONV_W:2 * CONV_W], cin[:, 2 * CONV_W:]
        z = cc * ch
        cprev = cprev_ref[...]
        zprev = cprev[:, CONV_W:2 * CONV_W] * cprev[:, 2 * CONV_W:]
        zs_ref[0:8, :] = jnp.where(tile == 0, 0.0, zprev)
        zs_ref[8:8 + tm, :] = z
        z1 = zs_ref[pl.ds(7, tm), :]
        z2 = zs_ref[pl.ds(6, tm), :]
        cw = cw_ref[...]
        cz = cw[0:1] * z2 + cw[1:2] * z1 + cw[2:3] * z

        dain = _dot(dya_ref[...], wco_ref[...])
        dcb = dain * cz
        dcz = dain * cb
        dczs_ref[0:tm, :] = dcz
        dcz1 = dczs_ref[pl.ds(1, tm), :]
        dcz2 = dczs_ref[pl.ds(2, tm), :]
        dz = cw[2:3] * dcz + cw[1:2] * dcz1 + cw[0:1] * dcz2
        dczs_ref[tm:tm + 8, :] = dczs_ref[0:8, :]
        dcw_ref[0:1, :] += _colsum(dcz * z2)
        dcw_ref[1:2, :] += _colsum(dcz * z1)
        dcw_ref[2:3, :] += _colsum(dcz * z)
        dconv_ref[:, :CONV_W] = dcb.astype(bf16)
        dconv_ref[:, CONV_W:2 * CONV_W] = (dz * ch).astype(bf16)
        dconv_ref[:, 2 * CONV_W:] = (dz * cc).astype(bf16)

        qb = q_ref[...]
        dob = _dot(dyc_ref[...], wxo_ref[...]).astype(bf16)
        kv = kv_ref[...]
        heads = range(HEADS)
        hcs = [slice(h * HEAD_DIM, (h + 1) * HEAD_DIM) for h in heads]
        vcs = [slice(XATTN_W + h * HEAD_DIM, XATTN_W + (h + 1) * HEAD_DIM) for h in heads]
        s_t = [_dot(kv[:, hcs[h]], qb[:, hcs[h]], NT) * (HEAD_DIM ** -0.5) for h in heads]
        dp_t = [_dot(kv[:, vcs[h]], dob[:, hcs[h]], NT) for h in heads]
        e_t = [jnp.exp(s_t[h] - jnp.max(s_t[h], axis=0, keepdims=True)) for h in heads]
        p_t = [e_t[h] / jnp.sum(e_t[h], axis=0, keepdims=True) for h in heads]
        dv = [_dot(p_t[h].astype(bf16), dob[:, hcs[h]]) for h in heads]
        ds_t = [(p_t[h] * (dp_t[h] - jnp.sum(dp_t[h] * p_t[h], axis=0, keepdims=True)) * (HEAD_DIM ** -0.5)).astype(bf16)
                for h in heads]
        dk = [_dot(ds_t[h], qb[:, hcs[h]]) for h in heads]
        dq_t = [_dot(kt_ref[hcs[h], :], ds_t[h]) for h in heads]
        dq_ref[...] = jnp.concatenate(dq_t, axis=0).T.astype(bf16)
        dkv_ref[...] += jnp.concatenate(dk + dv, axis=1)

        @pl.when(i == n - 1)
        def _():
            for cp in conv_side:
                cp.start()
            for cp in side + conv_side:
                cp.wait()

    rev = functools.partial(_row_spec, rev_n=n)
    prev_spec = pl.BlockSpec((8, 3 * CONV_W), lambda i: (jnp.maximum((n - 1 - i) * halo_blocks - 1, 0), 0))
    outs = pl.pallas_call(
        body, name="branch_bwd", grid=(n,),
        in_specs=[rev(tm, D_MODEL), rev(tm, D_MODEL), rev(tm, 3 * CONV_W), prev_spec, rev(tm, XATTN_W),
                  _const_spec((MEM_LEN, 2 * XATTN_W)), _const_spec((3, CONV_W)), _const_spec((D_MODEL, CONV_W)),
                  _const_spec((D_MODEL, XATTN_W)), _const_spec((XATTN_W, MEM_LEN))] + side_in_specs[:ns],
        out_specs=[rev(tm, 3 * CONV_W), rev(tm, XATTN_W), _acc_spec((MEM_LEN, 2 * XATTN_W))] + side_in_specs,
        out_shape=[_sds((s_len, 3 * CONV_W), bf16), _sds((s_len, XATTN_W), bf16),
                   _sds((MEM_LEN, 2 * XATTN_W), f32)] + side_shapes,
        scratch_shapes=[pltpu.VMEM((tm + 8, CONV_W), f32), pltpu.VMEM((tm + 8, CONV_W), f32),
                        pltpu.VMEM((8, CONV_W), f32)] + side_sems,
        compiler_params=_cparams(("arbitrary",)),
    )(dya, dyc, cin, cin, q, kv, conv_w, w_co_t, w_xo_t, k_t, *side_blocks)
    return outs[0], outs[1], outs[2], outs[3:]


def _in_proj_bwd(dgp, dconv, du, dq, dxp, w_in_t):
    s_len = dgp.shape[0]
    tm = 2 * TOKEN_TILE
    n = s_len // tm

    def body(dgp_ref, dconv_ref, du_ref, dq_ref, dxp_ref, win_ref, dx_ref, dproj_ref):
        dproj = jnp.concatenate([dgp_ref[...], dconv_ref[...], du_ref[...].astype(bf16), dq_ref[...]], axis=1)
        dproj_ref[...] = dproj
        dx_ref[...] = dxp_ref[...] + _dot(dproj, win_ref[...])

    return pl.pallas_call(
        body, name="in_proj_bwd", grid=(n,),
        in_specs=[_row_spec(tm, GATE_COLS), _row_spec(tm, 3 * CONV_W), _row_spec(tm, SSM_W), _row_spec(tm, XATTN_W),
                  _row_spec(tm, D_MODEL), _const_spec((IN_COLS, D_MODEL))],
        out_specs=[_row_spec(tm, D_MODEL), _row_spec(tm, IN_COLS)],
        out_shape=[_sds((s_len, D_MODEL), f32), _sds((s_len, IN_COLS), bf16)],
        compiler_params=_cparams(("parallel",)),
    )(dgp, dconv, du, dq, dxp, w_in_t)


N_CHIP = 4
CHIP_STEPS = [(1, 1), (1, 0), (0, 1), (0, 0)]


def _flip(v, d):
    return 1 - v if d else v


def _chip_order():
    x, y, _ = _mesh_place()
    return jnp.stack([2 * _flip(x, dx) + _flip(y, dy) for dx, dy in CHIP_STEPS]).astype(jnp.int32)


def _weight_grad_scatter(a_t, b, name, tm, tt):
    m, s_len = a_t.shape
    n_cols = b.shape[1]
    w = n_cols // N_DEV
    tn = 2 * w
    tm, tt = min(tm, m), min(tt, s_len)
    nm, nt = m // tm, s_len // tt
    assert m % tm == 0 and s_len % tt == 0

    def body(order_ref, a_ref, b_ref, recv_ref, acc_ref, send_ref, sib_ref, stash_ref,
             d2d_send, d2d_recv, ici_send, ici_recv, local_sem):
        del order_ref
        q, im, t = pl.program_id(0), pl.program_id(1), pl.program_id(2)
        x, y, c = _mesh_place()
        mesh_id = pl.DeviceIdType.MESH

        @pl.when(t == 0)
        def _():
            acc_ref[...] = jnp.zeros_like(acc_ref)

        acc_ref[...] += _dot(a_ref[...], b_ref[...])

        def to_sibling(qq, imm):
            rows = pl.ds(pl.multiple_of(imm * tm, tm), tm)
            return pltpu.make_async_remote_copy(
                src_ref=send_ref.at[qq, 0, rows, :], dst_ref=sib_ref.at[qq, rows, :],
                send_sem=d2d_send.at[qq], recv_sem=d2d_recv.at[qq, imm],
                device_id=(x, y, 1 - c), device_id_type=mesh_id)

        def finish_tile(qq, imm):
            rows = pl.ds(pl.multiple_of(imm * tm, tm), tm)
            to_sibling(qq, imm).wait_recv()
            both = stash_ref[...] + sib_ref[qq, rows, :].astype(f32)
            send_ref[qq, 1, rows, :] = both.astype(bf16)
            for step, (dx, dy) in enumerate(CHIP_STEPS):
                @pl.when(qq == step)
                def _(step=step, dx=dx, dy=dy):
                    src, dst = send_ref.at[step, 1, rows, :], recv_ref.at[step, rows, :]
                    if dx or dy:
                        pltpu.make_async_remote_copy(
                            src_ref=src, dst_ref=dst, send_sem=ici_send.at[step], recv_sem=ici_recv.at[step],
                            device_id=(_flip(x, dx), _flip(y, dy), c), device_id_type=mesh_id).start()
                    else:
                        pltpu.make_async_copy(src, dst, local_sem).start()

        @pl.when(t == nt - 1)
        def _():
            tile = q * nm + im

            @pl.when(tile > 0)
            def _():
                finish_tile((tile - 1) // nm, (tile - 1) % nm)

            rows = pl.ds(pl.multiple_of(im * tm, tm), tm)
            for core in (0, 1):
                @pl.when(c == core)
                def _(core=core):
                    other = 1 - core
                    send_ref[q, 0, rows, :] = acc_ref[:, other * w:(other + 1) * w].astype(bf16)
                    stash_ref[...] = acc_ref[:, core * w:(core + 1) * w]
            to_sibling(q, im).start()

            @pl.when(tile == N_CHIP * nm - 1)
            def _():
                finish_tile(q, im)

        @pl.when((q == N_CHIP - 1) & (im == nm - 1) & (t == nt - 1))
        def _():
            for step, (dx, dy) in enumerate(CHIP_STEPS):
                pltpu.make_async_remote_copy(
                    src_ref=send_ref.at[step, 0], dst_ref=sib_ref.at[step],
                    send_sem=d2d_send.at[step], recv_sem=d2d_recv.at[step, 0],
                    device_id=(x, y, 1 - c), device_id_type=mesh_id).wait_send()
                src, dst = send_ref.at[step, 1], recv_ref.at[step]
                if dx or dy:
                    pltpu.make_async_remote_copy(
                        src_ref=src, dst_ref=dst, send_sem=ici_send.at[step], recv_sem=ici_recv.at[step],
                        device_id=(_flip(x, dx), _flip(y, dy), c), device_id_type=mesh_id).wait()
                else:
                    pltpu.make_async_copy(src, dst, local_sem).wait()

    grid_spec = pltpu.PrefetchScalarGridSpec(
        num_scalar_prefetch=1, grid=(N_CHIP, nm, nt),
        in_specs=[pl.BlockSpec((tm, tt), lambda q, im, t, order: (im, t)),
                  pl.BlockSpec((tt, tn), lambda q, im, t, order: (t, order[q]))],
        out_specs=pl.BlockSpec(memory_space=pl.ANY),
        scratch_shapes=[pltpu.VMEM((tm, tn), f32), pltpu.VMEM((N_CHIP, 2, m, w), bf16), pltpu.VMEM((N_CHIP, m, w), bf16),
                        pltpu.VMEM((tm, w), f32),
                        pltpu.SemaphoreType.DMA((N_CHIP,)), pltpu.SemaphoreType.DMA((N_CHIP, nm)),
                        pltpu.SemaphoreType.DMA((N_CHIP - 1,)), pltpu.SemaphoreType.DMA((N_CHIP - 1,)),
                        pltpu.SemaphoreType.DMA])
    return pl.pallas_call(
        body, name=name, grid_spec=grid_spec,
        out_shape=_sds((N_CHIP, m, w), bf16),
        compiler_params=_cparams(("arbitrary", "arbitrary", "arbitrary")),
    )(_chip_order(), a_t, b)


def _adamw(w, g, m, v):
    m = ADAM_B1 * m + (1.0 - ADAM_B1) * g
    v = ADAM_B2 * v + (1.0 - ADAM_B2) * jnp.square(g)
    m_hat = m / (1.0 - ADAM_B1 ** ADAM_STEP)
    v_hat = v / (1.0 - ADAM_B2 ** ADAM_STEP)
    delta = -ADAM_LR * (m_hat / (jnp.sqrt(v_hat) + ADAM_EPS) + ADAM_WD * w)
    return delta, m, v


def _sum_parts(p_ref):
    g = p_ref[0].astype(f32)
    for j in range(1, p_ref.shape[0]):
        g = g + p_ref[j].astype(f32)
    return g


def _adamw_update(w, m, v, parts, name, transposed):
    rows, cols = w.shape
    n_parts = parts.shape[0]
    if transposed:
        tc = 256
        steps = cols // tc
        p_spec = pl.BlockSpec((n_parts, tc, rows), lambda i: (0, i, 0))
        spec = pl.BlockSpec((rows, tc), lambda i: (0, i))
    else:
        tr = next(t for t in (256, 128, 64, 32, 16, 8) if rows % t == 0)
        steps = rows // tr
        p_spec = pl.BlockSpec((n_parts, tr, cols), lambda i: (0, i, 0))
        spec = pl.BlockSpec((tr, cols), lambda i: (i, 0))

    def body(w_ref, p_ref, m_ref, v_ref, g_ref, d_ref, nm_ref, nv_ref):
        g = _sum_parts(p_ref)
        if transposed:
            g = g.T
        g_ref[...] = g
        d_ref[...], nm_ref[...], nv_ref[...] = _adamw(w_ref[...], g, m_ref[...], v_ref[...])

    return pl.pallas_call(
        body, name=name, grid=(steps,),
        in_specs=[spec, p_spec, spec, spec], out_specs=[spec] * 4,
        out_shape=[_sds((rows, cols), f32)] * 4,
        compiler_params=_cparams(("parallel",)),
    )(w, parts, m, v)


SMALL_GROUPS = [
    (["b_gate", "ln1_g", "ln1_b", "b_up", "b_down", "ln2_g", "ln2_b", "ssm_d"], 1),
    (["ssm_lam_re", "ssm_lam_im", "ssm_c_re", "ssm_c_im", "ssm_b_re", "ssm_b_im"], 0),
    (["conv_w"], 0),
    (["ssm_log_dt"], 0),
]


def _sum_small(group_parts):
    def body(*refs):
        n = len(refs) // 2
        for p_ref, o_ref in zip(refs[:n], refs[n:]):
            o_ref[...] = _sum_parts(p_ref)

    return pl.pallas_call(
        body, name="sum_small",
        out_shape=[_sds(p.shape[1:], f32) for p in group_parts],
        compiler_params=_cparams(),
    )(*group_parts)


def _adamw_small(ws, ms, vs, group_sums):
    names = [k for group, _ in SMALL_GROUPS for k in group]
    n = len(names)

    def body(*refs):
        w_refs, m_refs, v_refs = (dict(zip(names, refs[j * n:(j + 1) * n])) for j in range(3))
        p_refs = refs[3 * n:3 * n + len(SMALL_GROUPS)]
        out_refs = [dict(zip(names, refs[3 * n + len(SMALL_GROUPS) + j * n:][:n])) for j in range(4)]
        for (group, axis), p_ref in zip(SMALL_GROUPS, p_refs):
            total = p_ref[...]
            off = 0
            for k in group:
                size = SMALL[k][axis]
                g = total[:, off:off + size] if axis == 1 else total[off:off + size, :]
                off += size
                d, nm, nv = _adamw(w_refs[k][...], g, m_refs[k][...], v_refs[k][...])
                for j, val in enumerate((g, d, nm, nv)):
                    out_refs[j][k][...] = val

    res = pl.pallas_call(
        body, name="adamw_small",
        out_shape=[_sds(SMALL[k], f32) for _ in range(4) for k in names],
        compiler_params=_cparams(),
    )(*[ws[k] for k in names], *[ms[k] for k in names], *[vs[k] for k in names], *group_sums)
    return [dict(zip(names, res[j * n:(j + 1) * n])) for j in range(4)]


def _ssm_discretize(lam_re, lam_im, log_dt, b_re, b_im):
    dt = jnp.exp(log_dt)[:, None]
    mag = jnp.exp(lam_re * dt)
    abar_r = mag * jnp.cos(lam_im * dt)
    abar_i = mag * jnp.sin(lam_im * dt)
    den = lam_re * lam_re + lam_im * lam_im
    nr = abar_r - 1.0
    ni = abar_i
    kr = (nr * lam_re + ni * lam_im) / den
    ki = (ni * lam_re - nr * lam_im) / den
    bbar_r = kr[:, None, :] * b_re - ki[:, None, :] * b_im
    bbar_i = kr[:, None, :] * b_im + ki[:, None, :] * b_re
    return abar_r, abar_i, bbar_r, bbar_i


def _state_layout(re, im):
    parts = []
    for half in range(N_HALF):
        cols = slice(half * HALF_STATE, (half + 1) * HALF_STATE)
        parts += [re[..., cols], im[..., cols]]
    return jnp.concatenate(parts, axis=-1)


def _state_unlayout(a):
    re = jnp.concatenate([a[..., _half_cols(h)[0]] for h in range(N_HALF)], axis=-1)
    im = jnp.concatenate([a[..., _half_cols(h)[1]] for h in range(N_HALF)], axis=-1)
    return re, im


def _abar_powers(abar_r, abar_i):
    pr, pi = abar_r.reshape(1, N_STATE), abar_i.reshape(1, N_STATE)
    while pr.shape[0] < SSM_SEG:
        tr, ti = pr[-1:], pi[-1:]
        pr, pi = (jnp.concatenate([pr, pr * tr - pi * ti], axis=0), jnp.concatenate([pi, pr * ti + pi * tr], axis=0))
    return _state_layout(pr, pi)


HALF_GROUPS = SSM_GROUPS // N_HALF


def _half_block_diag(blocks):
    _, r, c = blocks.shape
    eye = jnp.eye(HALF_GROUPS, dtype=blocks.dtype)
    b4 = blocks.reshape(N_HALF, HALF_GROUPS, r, c)
    return jnp.einsum("ngrc,gk->ngrkc", b4, eye).reshape(N_HALF, HALF_GROUPS * r, HALF_GROUPS * c)


def _half_diag_blocks(mat, r, c):
    eye = jnp.eye(HALF_GROUPS, dtype=mat.dtype)
    m5 = mat.reshape(N_HALF, HALF_GROUPS, r, HALF_GROUPS, c)
    return jnp.einsum("ngrkc,gk->ngrc", m5, eye).reshape(SSM_GROUPS, r, c)


BIG = ["w_in", "w_conv_out", "w_glu", "w_kv", "w_xattn_out", "w_out", "w_up", "w_down"]
GATHER_TRANSPOSED = ["w_conv_out", "w_glu", "w_xattn_out", "w_up"]
PARTS_TRANSPOSED = ["w_in", "w_kv", "w_out", "w_down"]
SMALL = {"b_gate": (1, GATE_COLS), "conv_w": (3, CONV_W), "ssm_lam_re": (SSM_GROUPS, SSM_STATE),
         "ssm_lam_im": (SSM_GROUPS, SSM_STATE), "ssm_log_dt": (1, SSM_GROUPS),
         "ssm_b_re": (SSM_W, SSM_STATE), "ssm_b_im": (SSM_W, SSM_STATE),
         "ssm_c_re": (SSM_W, SSM_STATE), "ssm_c_im": (SSM_W, SSM_STATE), "ssm_d": (1, SSM_W),
         "ln1_g": (1, D_MODEL), "ln1_b": (1, D_MODEL), "b_up": (1, D_FF), "b_down": (1, D_MODEL),
         "ln2_g": (1, D_MODEL), "ln2_b": (1, D_MODEL)}
WEIGHTS = ["w_in", "b_gate", "conv_w", "w_conv_out", "ssm_lam_re", "ssm_lam_im", "ssm_log_dt", "ssm_b_re", "ssm_b_im",
           "ssm_c_re", "ssm_c_im", "ssm_d", "w_glu", "w_kv", "w_xattn_out", "w_out", "ln1_g", "ln1_b", "w_up", "b_up",
           "w_down", "b_down", "ln2_g", "ln2_b"]


def _local_step(x, mem, tgt, full, late, small):
    lam_re, lam_im, log_dt = small["ssm_lam_re"], small["ssm_lam_im"], small["ssm_log_dt"].reshape(SSM_GROUPS)
    c_shape = (SSM_GROUPS, SSM_GROUP, SSM_STATE)
    disc, disc_vjp = jax.vjp(_ssm_discretize, lam_re, lam_im, log_dt,
                             small["ssm_b_re"].reshape(c_shape), small["ssm_b_im"].reshape(c_shape))
    abar_r, abar_i, bbar_r, bbar_i = disc
    pw = _abar_powers(abar_r, abar_i)
    c_re, c_im = small["ssm_c_re"].reshape(c_shape), small["ssm_c_im"].reshape(c_shape)
    b_half = jnp.concatenate([_half_block_diag(bbar_r), _half_block_diag(bbar_i)], axis=2).astype(bf16)
    c_half = jnp.concatenate([_half_block_diag(c_re.transpose(0, 2, 1)), -_half_block_diag(c_im.transpose(0, 2, 1))],
                             axis=1).astype(bf16)

    s_len = x.shape[0]
    stack = lambda a: a.reshape(-1, a.shape[-1])
    kv, k_t, memb = _kv_proj(mem, full["w_kv"])
    (xbt, g, cin, u, q, ain, ob, aint, obt), side = _in_proj(
        x, full["w_in"], small["b_gate"], small["conv_w"], kv,
        [late[k] for k in ("w_glu", "w_conv_out", "w_xattn_out", "w_out", "w_up")])
    w_glu_t, w_co_t, w_xo_t, w_out, w_up_t = (stack(a) for a in side)
    y_ssm, cm_all, side = _ssm_fwd(u, b_half, c_half, pw, small["ssm_d"], [late["w_down"]])
    w_down = stack(side[0])
    ysbt, mb, xhat1, rstd1 = _mid_fwd(y_ssm, g, ain, ob, x, w_glu_t, w_co_t, w_xo_t, w_out,
                                      small["ln1_g"], small["ln1_b"])
    (x1bt, hdn, dr2bt, dpre, dx1, loss, dl2g, dl2b, dbdn, dbup) = _mlp_fwd_bwd(
        xhat1, tgt, small["ln1_g"], small["ln1_b"], w_up_t, small["b_up"], w_down,
        small["b_down"], small["ln2_g"], small["ln2_b"])
    recv = {}
    recv["w_down"] = _weight_grad_scatter(dr2bt, hdn, "dw_down", tm=512, tt=2048)
    recv["w_up"] = _weight_grad_scatter(x1bt, dpre, "dw_up", tm=512, tt=2048)
    (dxp, dr1bt, dgp, dya, dyc, dglu, dyssm, dl1g, dl1b, dbg) = _mid_bwd(
        dx1, xhat1, rstd1, g, ain, ob, y_ssm, small["ln1_g"], w_out, w_glu_t, w_co_t, w_xo_t)
    recv["w_out"] = _weight_grad_scatter(dr1bt, mb, "dw_out", tm=512, tt=s_len)
    recv["w_glu"] = _weight_grad_scatter(ysbt, dglu, "dw_glu", tm=512, tt=s_len)
    du, db_half, dc_half, da8, dd = _ssm_bwd(u, dyssm, cm_all, b_half, c_half, pw, small["ssm_d"])
    dabar_r, dabar_i = _state_unlayout(jnp.sum(da8, axis=0))
    dbbar_r = _half_diag_blocks(db_half[:, :, :HALF_STATE], SSM_GROUP, SSM_STATE)
    dbbar_i = _half_diag_blocks(db_half[:, :, HALF_STATE:], SSM_GROUP, SSM_STATE)
    g_shape = (SSM_GROUPS, SSM_STATE)
    dlam_re, dlam_im, dlog_dt, db_re, db_im = disc_vjp(
        (dabar_r.reshape(g_shape), dabar_i.reshape(g_shape), dbbar_r, dbbar_i))
    dc_re = _half_diag_blocks(dc_half[:, :HALF_STATE, :], SSM_STATE, SSM_GROUP).transpose(0, 2, 1)
    dc_im = -_half_diag_blocks(dc_half[:, HALF_STATE:, :], SSM_STATE, SSM_GROUP).transpose(0, 2, 1)

    small_grads = {
        "b_gate": dbg, "ssm_lam_re": dlam_re, "ssm_lam_im": dlam_im, "ssm_log_dt": dlog_dt,
        "ssm_b_re": db_re, "ssm_b_im": db_im, "ssm_c_re": dc_re, "ssm_c_im": dc_im, "ssm_d": dd,
        "ln1_g": dl1g, "ln1_b": dl1b, "b_up": dbup, "b_down": dbdn, "ln2_g": dl2g, "ln2_b": dl2b,
    }
    small_grads = {k: a.reshape(SMALL[k]) for k, a in small_grads.items()}
    groups = [(group, axis) for group, axis in SMALL_GROUPS if group != ["conv_w"]]
    stacks = [jnp.concatenate([small_grads[k] for k in group], axis=axis) if len(group) > 1 else small_grads[group[0]]
              for group, axis in groups]
    n_rowvec = stacks[0].shape[1]
    stacks[0] = jnp.concatenate([stacks[0], loss], axis=1)
    dense = lambda a: a.reshape(-1, LANES) if a.size % LANES == 0 else a
    dconv, dq, dkv, group_parts = _branch_bwd(dya, dyc, cin, q, kv, k_t, small["conv_w"], w_co_t, w_xo_t,
                                              [dense(a) for a in stacks])
    recv["w_conv_out"] = _weight_grad_scatter(aint, dya, "dw_conv_out", tm=512, tt=s_len)
    recv["w_xattn_out"] = _weight_grad_scatter(obt, dyc, "dw_xattn_out", tm=512, tt=s_len)
    recv["w_kv"] = _weight_grad_scatter(dkv.T.astype(bf16), memb, "dw_kv", tm=D_MODEL, tt=MEM_LEN)
    dx, dproj = _in_proj_bwd(dgp, dconv, du, dq, dxp, full["w_in"])
    recv["w_in"] = _weight_grad_scatter(xbt, dproj, "dw_in", tm=512, tt=2048)
    sums = _sum_small(group_parts)
    group_sums = dict(zip([tuple(group) for group, _ in groups], [s.reshape(a.shape) for s, a in zip(sums, stacks)]))
    group_sums[("conv_w",)] = sums[-1][0:3]
    first = tuple(groups[0][0])
    loss_all = group_sums[first][0, n_rowvec]
    group_sums[first] = group_sums[first][:, :n_rowvec]
    return loss_all, dx, recv, [group_sums[tuple(group)] for group, _ in SMALL_GROUPS]


def kernel(x, mem, w_in, b_gate, conv_w, w_conv_out, ssm_lam_re, ssm_lam_im, ssm_log_dt, ssm_b_re, ssm_b_im, ssm_c_re, ssm_c_im, ssm_d, w_glu, w_kv, w_xattn_out, w_out, ln1_g, ln1_b, w_up, b_up, w_down, b_down, ln2_g, ln2_b, loss_target, m_w_in, m_b_gate, m_conv_w, m_w_conv_out, m_ssm_lam_re, m_ssm_lam_im, m_ssm_log_dt, m_ssm_b_re, m_ssm_b_im, m_ssm_c_re, m_ssm_c_im, m_ssm_d, m_w_glu, m_w_kv, m_w_xattn_out, m_w_out, m_ln1_g, m_ln1_b, m_w_up, m_b_up, m_w_down, m_b_down, m_ln2_g, m_ln2_b, v_w_in, v_b_gate, v_conv_w, v_w_conv_out, v_ssm_lam_re, v_ssm_lam_im, v_ssm_log_dt, v_ssm_b_re, v_ssm_b_im, v_ssm_c_re, v_ssm_c_im, v_ssm_d, v_w_glu, v_w_kv, v_w_xattn_out, v_w_out, v_ln1_g, v_ln1_b, v_w_up, v_b_up, v_w_down, v_b_down, v_ln2_g, v_ln2_b):
    w = dict(w_in=w_in, b_gate=b_gate, conv_w=conv_w, w_conv_out=w_conv_out, ssm_lam_re=ssm_lam_re,
             ssm_lam_im=ssm_lam_im, ssm_log_dt=ssm_log_dt, ssm_b_re=ssm_b_re, ssm_b_im=ssm_b_im, ssm_c_re=ssm_c_re,
             ssm_c_im=ssm_c_im, ssm_d=ssm_d, w_glu=w_glu, w_kv=w_kv, w_xattn_out=w_xattn_out, w_out=w_out,
             ln1_g=ln1_g, ln1_b=ln1_b, w_up=w_up, b_up=b_up, w_down=w_down, b_down=b_down, ln2_g=ln2_g, ln2_b=ln2_b)
    m = dict(w_in=m_w_in, b_gate=m_b_gate, conv_w=m_conv_w, w_conv_out=m_w_conv_out, ssm_lam_re=m_ssm_lam_re,
             ssm_lam_im=m_ssm_lam_im, ssm_log_dt=m_ssm_log_dt, ssm_b_re=m_ssm_b_re, ssm_b_im=m_ssm_b_im,
             ssm_c_re=m_ssm_c_re, ssm_c_im=m_ssm_c_im, ssm_d=m_ssm_d, w_glu=m_w_glu, w_kv=m_w_kv,
             w_xattn_out=m_w_xattn_out, w_out=m_w_out, ln1_g=m_ln1_g, ln1_b=m_ln1_b, w_up=m_w_up, b_up=m_b_up,
             w_down=m_w_down, b_down=m_b_down, ln2_g=m_ln2_g, ln2_b=m_ln2_b)
    v = dict(w_in=v_w_in, b_gate=v_b_gate, conv_w=v_conv_w, w_conv_out=v_w_conv_out, ssm_lam_re=v_ssm_lam_re,
             ssm_lam_im=v_ssm_lam_im, ssm_log_dt=v_ssm_log_dt, ssm_b_re=v_ssm_b_re, ssm_b_im=v_ssm_b_im,
             ssm_c_re=v_ssm_c_re, ssm_c_im=v_ssm_c_im, ssm_d=v_ssm_d, w_glu=v_w_glu, w_kv=v_w_kv,
             w_xattn_out=v_w_xattn_out, w_out=v_w_out, ln1_g=v_ln1_g, ln1_b=v_ln1_b, w_up=v_w_up, b_up=v_b_up,
             w_down=v_w_down, b_down=v_b_down, ln2_g=v_ln2_g, ln2_b=v_ln2_b)
    out_shapes = {k: a.shape for k, a in w.items()}
    swapped = ("w_in", "ssm_b_re", "ssm_b_im")

    def shard2d(k, a):
        if k in swapped:
            a = jnp.swapaxes(a, -1, -2)
        if k in SMALL:
            return a.reshape((3, CONV_W // N_DEV) if k == "conv_w" else SMALL[k])
        return a[0]

    def result(k, a):
        if k in swapped:
            shape = out_shapes[k]
            return jnp.swapaxes(a.reshape(shape[:-2] + (shape[-1], shape[-2])), -1, -2)
        return a.reshape(out_shapes[k])

    w, m, v = ({k: shard2d(k, a) for k, a in d.items()} for d in (w, m, v))

    shards = {k: w[k].T.astype(bf16) if k in GATHER_TRANSPOSED else w[k].astype(bf16) for k in BIG}
    conv_pad = jnp.pad(w["conv_w"], ((0, 5), (0, LANES - CONV_W // N_DEV)))
    early = ["w_in", "w_kv"]
    gathered = _all_gather([shards[k] for k in early] + [conv_pad], "gather_weights")
    full = {k: a.reshape(-1, a.shape[-1]) for k, a in zip(early, gathered[:-1])}
    late = {k: shards[k] for k in BIG if k not in early}
    conv_full = gathered[-1][:, :3, :CONV_W // N_DEV].transpose(1, 0, 2).reshape(3, CONV_W)
    small = {k: (conv_full if k == "conv_w" else w[k]) for k in SMALL}

    loss, dx, recv, group_sums = _local_step(x[0], mem[0], loss_target[0], full, late, small)

    grads, deltas, new_m, new_v = {}, {}, {}, {}
    for k in BIG:
        res = _adamw_update(w[k], m[k], v[k], recv[k], "adamw_" + k, transposed=k in PARTS_TRANSPOSED)
        grads[k], deltas[k], new_m[k], new_v[k] = res

    widen = lambda k, a: jnp.tile(a, (1, N_DEV)) if k == "conv_w" else a
    res = _adamw_small(small, {k: widen(k, m[k]) for k in SMALL}, {k: widen(k, v[k]) for k in SMALL}, group_sums)
    dev = _slot(_mesh_place())
    for d, small_res in zip((grads, deltas, new_m, new_v), res):
        for k, a in small_res.items():
            if k == "conv_w":
                a = lax.dynamic_slice_in_dim(a, dev * (CONV_W // N_DEV), CONV_W // N_DEV, axis=1)
            d[k] = a

    outs = [loss, dx[None]]
    for d in (grads, deltas, new_m, new_v):
        outs += [result(k, d[k]) for k in WEIGHTS]
    return tuple(outs)
```

```python
import functools
import math

import jax
import jax.numpy as jnp
from jax import lax
from jax.experimental import pallas as pl
from jax.experimental.pallas import tpu as pltpu

f32 = jnp.float32
bf16 = jnp.bfloat16

D_MODEL = 1024
MEM_LEN = 256
GATE_COLS = 3 * D_MODEL
CONV_W = 512
SSM_W = 512
XATTN_W = 512
HEADS = 4
HEAD_DIM = 128
D_FF = 4096
IN_COLS = GATE_COLS + 3 * CONV_W + SSM_W + XATTN_W
SSM_GROUPS = 32
SSM_GROUP = 16
SSM_STATE = 64
N_STATE = SSM_GROUPS * SSM_STATE
ALPHA = 2.0 ** 0.25
LN_EPS = 1e-5
N_DEV = 8

ADAM_LR = 0.001
ADAM_B1 = 0.9
ADAM_B2 = 0.999
ADAM_EPS = 1e-08
ADAM_WD = 0.01
ADAM_STEP = 10

VMEM_LIMIT_V7X = 56 * 2 ** 20
SUBLANES = 8
LANES = 128

TOKEN_TILE = 256
SSM_BLOCK = 256
SSM_SEG = SSM_BLOCK // SUBLANES
LANE_CHUNK = 512
N_HALF = 2
HALF_W = SSM_W // N_HALF
HALF_STATE = N_STATE // N_HALF
HALF_COLS = 2 * HALF_STATE

NT = (((1,), (1,)), ((), ()))
TN = (((0,), (0,)), ((), ()))
NN = (((1,), (0,)), ((), ()))


def _dot(a, b, dims=NN):
    return lax.dot_general(a, b, dims, preferred_element_type=f32)


def _cparams(sem=None):
    return pltpu.CompilerParams(dimension_semantics=sem, vmem_limit_bytes=VMEM_LIMIT_V7X)


def _row_spec(tm, cols, rev_n=None):
    if rev_n is None:
        return pl.BlockSpec((tm, cols), lambda i: (i, 0))
    return pl.BlockSpec((tm, cols), lambda i: (rev_n - 1 - i, 0))


def _col_spec(rows, tm):
    return pl.BlockSpec((rows, tm), lambda i: (0, i))


def _const_spec(shape):
    nd = len(shape)
    return pl.BlockSpec(shape, lambda *_: (0,) * nd, pipeline_mode=pl.Buffered(1))


def _acc_spec(shape):
    nd = len(shape)
    return pl.BlockSpec(shape, lambda *_: (0,) * nd)


def _sds(shape, dtype):
    return jax.ShapeDtypeStruct(shape, dtype)


def _gelu(x):
    c = math.sqrt(2.0 / math.pi)
    return 0.5 * x * (1.0 + jnp.tanh(c * (x + 0.044715 * x * x * x)))


def _gelu_grad(x):
    c = math.sqrt(2.0 / math.pi)
    t = jnp.tanh(c * (x + 0.044715 * x * x * x))
    return 0.5 * (1.0 + t) + 0.5 * x * (1.0 - t * t) * c * (1.0 + 3.0 * 0.044715 * x * x)


def _colsum(a):
    return jnp.sum(a, axis=0, keepdims=True)


def _mesh_place():
    return lax.axis_index("x"), lax.axis_index("y"), lax.axis_index("c")


def _slot(p):
    return 4 * p[0] + 2 * p[1] + p[2]


def _other_devices(me):
    x, y, c = me
    flip = lambda v, d: 1 - v if d else v
    return [(flip(x, dx), flip(y, dy), flip(c, dc)) for dx in (0, 1) for dy in (0, 1) for dc in (0, 1)][1:]


def _all_gather(blocks, name):
    n = len(blocks)

    def body(*refs):
        ins, outs = refs[:n], refs[n:2 * n]
        send_sems, recv_sems, local_sems = refs[2 * n:]
        x, y, c = _mesh_place()
        me, sibling = (x, y, c), (x, y, 1 - c)
        chips = [(1 - x, y), (x, 1 - y), (1 - x, 1 - y)]

        def copy(a, k, block, to, src=None):
            rows = outs[a].at[_slot(block)]
            return pltpu.make_async_remote_copy(
                src_ref=rows if src is None else src, dst_ref=rows,
                send_sem=send_sems.at[a, k], recv_sem=recv_sems.at[a, k],
                device_id=to, device_id_type=pl.DeviceIdType.MESH)

        mine = [pltpu.make_async_copy(ins[a], outs[a].at[_slot(me)], local_sems.at[a]) for a in range(n)]
        for cp in mine:
            cp.start()
        first = []
        for a in range(n):
            first.append(copy(a, 0, me, sibling, src=ins[a]))
            first += [copy(a, 1 + j, me, (*chip, c), src=ins[a]) for j, chip in enumerate(chips)]
        for cp in first:
            cp.start()
        passed = []
        for a in range(n):
            for j, chip in enumerate(chips):
                copy(a, 1 + j, (*chip, c), me).wait_recv()
                fwd = copy(a, 4 + j, (*chip, c), sibling)
                fwd.start()
                passed.append(fwd)
        for a in range(n):
            copy(a, 0, sibling, me).wait_recv()
            for j, chip in enumerate(chips):
                copy(a, 4 + j, (*chip, 1 - c), me).wait_recv()
        for cp in first + passed:
            cp.wait_send()
        for cp in mine:
            cp.wait()

    any_spec = pl.BlockSpec(memory_space=pl.ANY)
    return pl.pallas_call(
        body, name=name,
        out_shape=[_sds((N_DEV,) + b.shape, b.dtype) for b in blocks],
        in_specs=[any_spec] * n, out_specs=[any_spec] * n,
        scratch_shapes=[pltpu.SemaphoreType.DMA((n, 7)), pltpu.SemaphoreType.DMA((n, 7)),
                        pltpu.SemaphoreType.DMA((n,))],
    )(*blocks)


def _side_gather_copies(ins, outs, send_sems, recv_sems, local_sems):
    me = _mesh_place()
    copies = []
    for a, (src, dst) in enumerate(zip(ins, outs)):
        copies.append(pltpu.make_async_copy(src, dst.at[_slot(me)], local_sems.at[a]))
        for k, peer in enumerate(_other_devices(me)):
            copies.append(pltpu.make_async_remote_copy(
                src_ref=src, dst_ref=dst.at[_slot(me)], send_sem=send_sems.at[a, k], recv_sem=recv_sems.at[a, k],
                device_id=peer, device_id_type=pl.DeviceIdType.MESH))
    return copies


def _side_gather_two_level(ins, outs, send_sems, recv_sems, local_sems):
    x, y, c = _mesh_place()
    me, sibling = (x, y, c), (x, y, 1 - c)
    chips = [(1 - x, y), (x, 1 - y), (1 - x, 1 - y)]

    def copy(a, k, block, to, src=None):
        rows = outs[a].at[_slot(block)]
        return pltpu.make_async_remote_copy(
            src_ref=rows if src is None else src, dst_ref=rows, send_sem=send_sems.at[a, k], recv_sem=recv_sems.at[a, k],
            device_id=to, device_id_type=pl.DeviceIdType.MESH)

    n = len(ins)
    mine = [pltpu.make_async_copy(ins[a], outs[a].at[_slot(me)], local_sems.at[a]) for a in range(n)]
    first = [copy(a, 0, me, sibling, src=ins[a]) for a in range(n)]
    first += [copy(a, 1 + j, me, (*chip, c), src=ins[a]) for a in range(n) for j, chip in enumerate(chips)]
    passed = [copy(a, 4 + j, (*chip, c), sibling) for a in range(n) for j, chip in enumerate(chips)]

    def start():
        for cp in mine + first:
            cp.start()

    def forward():
        for a in range(n):
            for j, chip in enumerate(chips):
                copy(a, 1 + j, (*chip, c), me).wait_recv()
        for cp in passed:
            cp.start()

    def finish():
        for a in range(n):
            copy(a, 0, sibling, me).wait_recv()
            for j, chip in enumerate(chips):
                copy(a, 4 + j, (*chip, 1 - c), me).wait_recv()
        for cp in first + passed:
            cp.wait_send()
        for cp in mine:
            cp.wait()

    return start, forward, finish


def _side_gather_specs(blocks):
    n = len(blocks)
    any_spec = pl.BlockSpec(memory_space=pl.ANY)
    return ([any_spec] * n, [_sds((N_DEV,) + b.shape, b.dtype) for b in blocks],
            [pltpu.SemaphoreType.DMA((n, N_DEV - 1)), pltpu.SemaphoreType.DMA((n, N_DEV - 1)),
             pltpu.SemaphoreType.DMA((n,))])


def _kv_proj(mem, w_kv):
    def body(mem_ref, w_ref, kv_ref, kt_ref, memb_ref):
        mb = mem_ref[...].astype(bf16)
        memb_ref[...] = mb
        kv = _dot(mb, w_ref[...]).astype(bf16)
        kv_ref[...] = kv
        kt_ref[...] = kv[:, :XATTN_W].T

    return pl.pallas_call(
        body, name="kv_proj",
        out_shape=[_sds((MEM_LEN, 2 * XATTN_W), bf16), _sds((XATTN_W, MEM_LEN), bf16), _sds((MEM_LEN, D_MODEL), bf16)],
        compiler_params=_cparams(),
    )(mem, w_kv)


def _attention_probs(qb, kv_ref, h):
    kh = kv_ref[:, h * HEAD_DIM:(h + 1) * HEAD_DIM]
    s = _dot(qb[:, h * HEAD_DIM:(h + 1) * HEAD_DIM], kh, NT) * (HEAD_DIM ** -0.5)
    e = jnp.exp(s - jnp.max(s, axis=-1, keepdims=True))
    return e / jnp.sum(e, axis=-1, keepdims=True)


def _in_proj(x, w_in_t, b_gate, conv_w, kv, side_blocks):
    s_len = x.shape[0]
    tm = 2 * TOKEN_TILE
    n = s_len // tm
    ns = len(side_blocks)
    side_in_specs, side_shapes, side_sems = _side_gather_specs(side_blocks)

    def body(*refs):
        (x_ref, win_ref, bg_ref, cw_ref, kv_ref) = refs[:5]
        side_ins = refs[5:5 + ns]
        (xbt_ref, g_ref, cin_ref, u_ref, q_ref, ain_ref, o_ref, aint_ref, ot_ref) = refs[5 + ns:14 + ns]
        side_outs = refs[14 + ns:14 + 2 * ns]
        zs_ref = refs[14 + 2 * ns]
        side_start, side_forward, side_finish = _side_gather_two_level(side_ins, side_outs, *refs[15 + 2 * ns:])
        i = pl.program_id(0)
        pl.when(i == 0)(side_start)
        pl.when(i == (3 * n) // 4)(side_forward)

        xb = x_ref[...].astype(bf16)
        xbt_ref[...] = xb.T
        proj = _dot(xb, win_ref[...], NT)
        g_ref[...] = jax.nn.sigmoid(proj[:, :GATE_COLS] + bg_ref[...]).astype(bf16)
        cin = proj[:, GATE_COLS:GATE_COLS + 3 * CONV_W]
        cin_ref[...] = cin
        u_ref[...] = proj[:, GATE_COLS + 3 * CONV_W:GATE_COLS + 3 * CONV_W + SSM_W]
        qb = proj[:, IN_COLS - XATTN_W:].astype(bf16)
        q_ref[...] = qb

        cb, cc, ch = cin[:, :CONV_W], cin[:, CONV_W:2 * CONV_W], cin[:, 2 * CONV_W:]
        z = cc * ch

        @pl.when(i == 0)
        def _():
            zs_ref[0:8, :] = jnp.zeros((8, CONV_W), f32)

        zs_ref[8:8 + tm, :] = z
        z1 = zs_ref[pl.ds(7, tm), :]
        z2 = zs_ref[pl.ds(6, tm), :]
        cw = cw_ref[...]
        cz = cw[0:1] * z2 + cw[1:2] * z1 + cw[2:3] * z
        zs_ref[0:8, :] = zs_ref[tm:tm + 8, :]
        ain = (cb * cz).astype(bf16)
        ain_ref[...] = ain
        aint_ref[...] = ain.T

        probs = [_attention_probs(qb, kv_ref, h) for h in range(HEADS)]
        outs = [_dot(probs[h].astype(bf16), kv_ref[:, XATTN_W + h * HEAD_DIM:XATTN_W + (h + 1) * HEAD_DIM])
                for h in range(HEADS)]
        ob = jnp.concatenate(outs, axis=1).astype(bf16)
        o_ref[...] = ob
        ot_ref[...] = ob.T

        pl.when(i == n - 1)(side_finish)

    row_cols = [(GATE_COLS, bf16), (3 * CONV_W, f32), (SSM_W, f32), (XATTN_W, bf16), (CONV_W, bf16), (XATTN_W, bf16)]
    t_rows = [D_MODEL, CONV_W, XATTN_W]
    outs = pl.pallas_call(
        body, name="in_proj", grid=(n,),
        in_specs=[_row_spec(tm, D_MODEL), _const_spec((IN_COLS, D_MODEL)), _const_spec((1, GATE_COLS)),
                  _const_spec((3, CONV_W)), _const_spec((MEM_LEN, 2 * XATTN_W))] + side_in_specs,
        out_specs=([_col_spec(t_rows[0], tm)] + [_row_spec(tm, c) for c, _ in row_cols]
                   + [_col_spec(t_rows[1], tm), _col_spec(t_rows[2], tm)] + side_in_specs),
        out_shape=([_sds((t_rows[0], s_len), bf16)] + [_sds((s_len, c), dt) for c, dt in row_cols]
                   + [_sds((t_rows[1], s_len), bf16), _sds((t_rows[2], s_len), bf16)] + side_shapes),
        scratch_shapes=[pltpu.VMEM((tm + 8, CONV_W), f32)] + side_sems,
        compiler_params=_cparams(("arbitrary",)),
    )(x, w_in_t, b_gate, conv_w, kv, *side_blocks)
    return outs[:9], outs[9:]


def _state_cols(chunk):
    half, off = divmod(chunk * LANE_CHUNK, HALF_STATE)
    lo = half * HALF_COLS + off
    return slice(lo, lo + LANE_CHUNK), slice(lo + HALF_STATE, lo + HALF_STATE + LANE_CHUNK)


def _half_cols(half):
    lo = half * HALF_COLS
    return slice(lo, lo + HALF_STATE), slice(lo + HALF_STATE, lo + HALF_COLS)


def _rows_to_segments(src_ref, stage_ref, dst_ref):
    nc = SSM_W // LANES
    for c in range(nc):
        stage_ref[c] = src_ref[:, c * LANES:(c + 1) * LANES]
    for c in range(nc):
        for k in range(SSM_SEG):
            dst_ref[k * SUBLANES:(k + 1) * SUBLANES, c * LANES:(c + 1) * LANES] = (
                stage_ref[c, pl.ds(k, SUBLANES, stride=SSM_SEG), :])


def _rows_from_segments(src_ref, stage_ref, dst_ref):
    nc = SSM_W // LANES
    for c in range(nc):
        for k in range(SSM_SEG):
            stage_ref[c, pl.ds(k, SUBLANES, stride=SSM_SEG), :] = (
                src_ref[k * SUBLANES:(k + 1) * SUBLANES, c * LANES:(c + 1) * LANES])
    for c in range(nc):
        dst_ref[:, c * LANES:(c + 1) * LANES] = stage_ref[c]


def _ssm_scan(s_ref, pw_ref, init_ref, reverse, unroll):
    for chunk in range(N_STATE // LANE_CHUNK):
        re, im = _state_cols(chunk)
        ar = jnp.broadcast_to(pw_ref[0:1, re], (SUBLANES, LANE_CHUNK))
        ai = jnp.broadcast_to(pw_ref[0:1, im], (SUBLANES, LANE_CHUNK))
        if reverse:
            ai = -ai

        def step(j, carry, re=re, im=im, ar=ar, ai=ai):
            sr, si = carry
            k = (SSM_SEG - 1 - j) if reverse else j
            r0 = pl.multiple_of(k * SUBLANES, SUBLANES)
            nr = ar * sr - ai * si + s_ref[pl.ds(r0, SUBLANES), re]
            ni = ar * si + ai * sr + s_ref[pl.ds(r0, SUBLANES), im]
            s_ref[pl.ds(r0, SUBLANES), re] = nr
            s_ref[pl.ds(r0, SUBLANES), im] = ni
            return nr, ni

        if init_ref is None:
            init = (jnp.zeros((SUBLANES, LANE_CHUNK), f32),) * 2
        else:
            init = (init_ref[:, re], init_ref[:, im])
        lax.fori_loop(0, SSM_SEG, step, init, unroll=unroll)


def _ssm_add_carry(s_ref, pw_ref, cm_ref, reverse):
    for chunk in range(N_STATE // LANE_CHUNK):
        re, im = _state_cols(chunk)
        cr, ci = cm_ref[:, re], cm_ref[:, im]
        for k in range(SSM_SEG):
            pk = (SSM_SEG - 1 - k) if reverse else k
            pr = pw_ref[pk:pk + 1, re]
            pi = pw_ref[pk:pk + 1, im]
            if reverse:
                pi = -pi
            rows = slice(k * SUBLANES, (k + 1) * SUBLANES)
            s_ref[rows, re] = s_ref[rows, re] + (pr * cr - pi * ci)
            s_ref[rows, im] = s_ref[rows, im] + (pr * ci + pi * cr)


def _ssm_carries(first_row, s_ref, pw_ref, carry_ref, cm_ref, reverse):
    order = range(SUBLANES - 1, -1, -1) if reverse else range(SUBLANES)
    for half in range(N_HALF):
        re, im = _half_cols(half)
        a_r, a_i = pw_ref[SSM_SEG - 1:SSM_SEG, re], pw_ref[SSM_SEG - 1:SSM_SEG, im]
        if reverse:
            a_i = -a_i
        cr, ci = carry_ref[0:1, re], carry_ref[0:1, im]
        for seg in order:
            cm_ref[seg:seg + 1, re] = cr
            cm_ref[seg:seg + 1, im] = ci
            er = s_ref[first_row + seg:first_row + seg + 1, re]
            ei = s_ref[first_row + seg:first_row + seg + 1, im]
            cr, ci = a_r * cr - a_i * ci + er, a_r * ci + a_i * cr + ei
        carry_ref[0:1, re] = cr
        carry_ref[0:1, im] = ci


def _ssm_fwd(u, b_half, c_half, pw, d_skip, side_blocks):
    s_len = u.shape[0]
    tb = SSM_BLOCK
    n = s_len // tb
    ns = len(side_blocks)
    side_in_specs, side_shapes, side_sems = _side_gather_specs(side_blocks)

    def body(*refs):
        u_ref, b_ref, c_ref, pw_ref, d_ref = refs[:5]
        side_ins = refs[5:5 + ns]
        y_ref, cm_ref = refs[5 + ns:7 + ns]
        side_outs = refs[7 + ns:7 + 2 * ns]
        s_ref, carry_ref, up_ref, yp_ref, stage_ref = refs[7 + 2 * ns:12 + 2 * ns]
        side_start, side_forward, side_finish = _side_gather_two_level(side_ins, side_outs, *refs[12 + 2 * ns:])
        i = pl.program_id(0)

        @pl.when(i == 0)
        def _():
            carry_ref[...] = jnp.zeros_like(carry_ref)
            side_start()

        pl.when(i == (3 * n) // 4)(side_forward)
        _rows_to_segments(u_ref, stage_ref, up_ref)
        u = up_ref[...]
        ub = u.astype(bf16)
        for half in range(N_HALF):
            s_ref[:, half * HALF_COLS:(half + 1) * HALF_COLS] = _dot(ub[:, half * HALF_W:(half + 1) * HALF_W], b_ref[half])
        _ssm_scan(s_ref, pw_ref, None, reverse=False, unroll=4)
        _ssm_carries(tb - SUBLANES, s_ref, pw_ref, carry_ref, cm_ref, reverse=False)
        _ssm_add_carry(s_ref, pw_ref, cm_ref, reverse=False)
        for half in range(N_HALF):
            cols = slice(half * HALF_W, (half + 1) * HALF_W)
            sb = s_ref[:, half * HALF_COLS:(half + 1) * HALF_COLS].astype(bf16)
            yp_ref[:, cols] = _dot(sb, c_ref[half]) + d_ref[:, cols] * u[:, cols]
        _rows_from_segments(yp_ref, stage_ref, y_ref)

        pl.when(i == n - 1)(side_finish)

    outs = pl.pallas_call(
        body, name="ssm_fwd", grid=(n,),
        in_specs=[_row_spec(tb, SSM_W), _const_spec((N_HALF, HALF_W, HALF_COLS)), _const_spec((N_HALF, HALF_COLS, HALF_W)),
                  _const_spec((SSM_SEG, 2 * N_STATE)), _const_spec((1, SSM_W))] + side_in_specs,
        out_specs=[_row_spec(tb, SSM_W), _row_spec(SUBLANES, 2 * N_STATE)] + side_in_specs,
        out_shape=[_sds((s_len, SSM_W), f32), _sds((n * SUBLANES, 2 * N_STATE), f32)] + side_shapes,
        scratch_shapes=[pltpu.VMEM((tb, 2 * N_STATE), f32), pltpu.VMEM((SUBLANES, 2 * N_STATE), f32),
                        pltpu.VMEM((tb, SSM_W), f32), pltpu.VMEM((tb, SSM_W), f32),
                        pltpu.VMEM((SSM_W // LANES, tb, LANES), f32)] + side_sems,
        compiler_params=_cparams(("arbitrary",)),
    )(u, b_half, c_half, pw, d_skip, *side_blocks)
    return outs[0], outs[1], outs[2:]


def _layer_norm_fwd(r, g, b):
    mu = jnp.mean(r, axis=-1, keepdims=True)
    var = jnp.mean(jnp.square(r - mu), axis=-1, keepdims=True)
    rstd = lax.rsqrt(var + LN_EPS)
    xhat = (r - mu) * rstd
    return xhat, rstd, xhat * g + b


def _layer_norm_bwd(dy, xhat, rstd, g):
    dxh = dy * g
    m1 = jnp.mean(dxh, axis=-1, keepdims=True)
    m2 = jnp.mean(dxh * xhat, axis=-1, keepdims=True)
    return rstd * (dxh - m1 - xhat * m2)


def _branch_outputs(ys_ref, ain_ref, o_ref, wglu_ref, wco_ref, wxo_ref):
    ysb = _gelu(ys_ref[...]).astype(bf16)
    glu = _dot(ysb, wglu_ref[...], NT)
    ga, sb = glu[:, :D_MODEL], jax.nn.sigmoid(glu[:, D_MODEL:])
    ya = _dot(ain_ref[...], wco_ref[...], NT)
    yc = _dot(o_ref[...], wxo_ref[...], NT)
    return ysb, ga, sb, ya, ga * sb, yc


def _mid_fwd(y_ssm, g, ain, ob, x, w_glu_t, w_co_t, w_xo_t, w_out, ln1_g, ln1_b):
    s_len = x.shape[0]
    tm = TOKEN_TILE
    n = s_len // tm

    def body(ys_ref, g_ref, ain_ref, o_ref, x_ref, wglu_ref, wco_ref, wxo_ref, wout_ref, lg_ref, lb_ref,
             ysbt_ref, mb_ref, xhat_ref, rstd_ref):
        ysb, _, _, ya, yb, yc = _branch_outputs(ys_ref, ain_ref, o_ref, wglu_ref, wco_ref, wxo_ref)
        ysbt_ref[...] = ysb.T
        gt = g_ref[...].astype(f32)
        merged = gt[:, :D_MODEL] * ya + gt[:, D_MODEL:2 * D_MODEL] * yb + gt[:, 2 * D_MODEL:] * yc
        mb = merged.astype(bf16)
        mb_ref[...] = mb
        r1 = ALPHA * x_ref[...] + _dot(mb, wout_ref[...])
        xhat, rstd, _ = _layer_norm_fwd(r1, lg_ref[...], lb_ref[...])
        xhat_ref[...] = xhat
        rstd_ref[...] = rstd

    row_cols = [(D_MODEL, bf16), (D_MODEL, f32), (1, f32)]
    return pl.pallas_call(
        body, name="mid_fwd", grid=(n,),
        in_specs=[_row_spec(tm, SSM_W), _row_spec(tm, GATE_COLS), _row_spec(tm, CONV_W), _row_spec(tm, XATTN_W),
                  _row_spec(tm, D_MODEL), _const_spec((2 * D_MODEL, SSM_W)), _const_spec((D_MODEL, CONV_W)),
                  _const_spec((D_MODEL, XATTN_W)), _const_spec((D_MODEL, D_MODEL)),
                  _const_spec((1, D_MODEL)), _const_spec((1, D_MODEL))],
        out_specs=[_col_spec(SSM_W, tm)] + [_row_spec(tm, c) for c, _ in row_cols],
        out_shape=[_sds((SSM_W, s_len), bf16)] + [_sds((s_len, c), dt) for c, dt in row_cols],
        compiler_params=_cparams(("parallel",)),
    )(y_ssm, g, ain, ob, x, w_glu_t, w_co_t, w_xo_t, w_out, ln1_g, ln1_b)


def _mlp_fwd_bwd(xhat1, tgt, ln1_g, ln1_b, w_up_t, b_up, w_down, b_down, ln2_g, ln2_b):
    s_len = xhat1.shape[0]
    tm = TOKEN_TILE
    n = s_len // tm
    fc = 1024
    nfc = D_FF // fc

    def body(xh_ref, t_ref, l1g_ref, l1b_ref, wup_ref, bup_ref, wdn_ref, bdn_ref, l2g_ref, l2b_ref,
             x1bt_ref, hdn_ref, dr2bt_ref, dpre_ref, dx1_ref,
             loss_ref, dl2g_ref, dl2b_ref, dbdn_ref, dbup_ref, rl_ref):
        i = pl.program_id(0)

        @pl.when(i == 0)
        def _():
            loss_ref[...] = jnp.zeros_like(loss_ref)
            dl2g_ref[...] = jnp.zeros_like(dl2g_ref)
            dl2b_ref[...] = jnp.zeros_like(dl2b_ref)
            dbdn_ref[...] = jnp.zeros_like(dbdn_ref)
            dbup_ref[...] = jnp.zeros_like(dbup_ref)

        x1 = xh_ref[...] * l1g_ref[...] + l1b_ref[...]
        x1b = x1.astype(bf16)
        x1bt_ref[...] = x1b.T
        chunks = [slice(c * fc, (c + 1) * fc) for c in range(nfc)]
        pres = [_dot(x1b, wup_ref[cols, :], NT) for cols in chunks]
        hbs = []
        for cols, pre in zip(chunks, pres):
            rl = jnp.maximum(pre + bup_ref[:, cols], 0.0)
            rl_ref[:, cols] = rl
            hb = (rl * rl).astype(bf16)
            hdn_ref[:, cols] = hb
            hbs.append(hb)
        acc = _dot(hbs[0], wdn_ref[chunks[0], :])
        for cols, hb in zip(chunks[1:], hbs[1:]):
            acc = acc + _dot(hb, wdn_ref[cols, :])
        r2 = ALPHA * x1 + acc + bdn_ref[...]
        xhat2, rstd2, y = _layer_norm_fwd(r2, l2g_ref[...], l2b_ref[...])
        err = y - t_ref[...]
        loss_ref[...] += jnp.sum(jnp.sum(err * err, axis=1, keepdims=True), axis=0, keepdims=True) * (0.5 / D_MODEL)
        dy = err * (1.0 / D_MODEL)
        dl2g_ref[...] += _colsum(dy * xhat2)
        dl2b_ref[...] += _colsum(dy)
        dr2 = _layer_norm_bwd(dy, xhat2, rstd2, l2g_ref[...])
        dbdn_ref[...] += _colsum(dr2)
        dr2b = dr2.astype(bf16)
        dr2bt_ref[...] = dr2b.T
        dhs = [_dot(dr2b, wdn_ref[cols, :], NT) for cols in chunks]
        dpbs = []
        for cols, dh in zip(chunks, dhs):
            dpre = dh * (2.0 * rl_ref[:, cols])
            dbup_ref[:, cols] += _colsum(dpre)
            dpb = dpre.astype(bf16)
            dpre_ref[:, cols] = dpb
            dpbs.append(dpb)
        dacc = _dot(dpbs[0], wup_ref[chunks[0], :])
        for cols, dpb in zip(chunks[1:], dpbs[1:]):
            dacc = dacc + _dot(dpb, wup_ref[cols, :])
        dx1_ref[...] = ALPHA * dr2 + dacc

    acc_shapes = [(1, LANES), (1, D_MODEL), (1, D_MODEL), (1, D_MODEL), (1, D_FF)]
    return pl.pallas_call(
        body, name="mlp_fwd_bwd", grid=(n,),
        in_specs=[_row_spec(tm, D_MODEL), _row_spec(tm, D_MODEL), _const_spec((1, D_MODEL)), _const_spec((1, D_MODEL)),
                  _const_spec((D_FF, D_MODEL)), _const_spec((1, D_FF)), _const_spec((D_FF, D_MODEL)),
                  _const_spec((1, D_MODEL)), _const_spec((1, D_MODEL)), _const_spec((1, D_MODEL))],
        out_specs=([_col_spec(D_MODEL, tm), _row_spec(tm, D_FF), _col_spec(D_MODEL, tm), _row_spec(tm, D_FF),
                    _row_spec(tm, D_MODEL)] + [_acc_spec(s) for s in acc_shapes]),
        out_shape=([_sds((D_MODEL, s_len), bf16), _sds((s_len, D_FF), bf16), _sds((D_MODEL, s_len), bf16),
                    _sds((s_len, D_FF), bf16), _sds((s_len, D_MODEL), f32)] + [_sds(s, f32) for s in acc_shapes]),
        scratch_shapes=[pltpu.VMEM((tm, D_FF), f32)],
        compiler_params=_cparams(("arbitrary",)),
    )(xhat1, tgt, ln1_g, ln1_b, w_up_t, b_up, w_down, b_down, ln2_g, ln2_b)


def _mid_bwd(dx1, xhat1, rstd1, g, ain, ob, y_ssm, ln1_g, w_out, w_glu_t, w_co_t, w_xo_t):
    s_len = dx1.shape[0]
    tm = TOKEN_TILE
    n = s_len // tm

    def body(dx1_ref, xh_ref, rs_ref, g_ref, ain_ref, o_ref, ys_ref, lg_ref, wout_ref, wglu_ref, wco_ref, wxo_ref,
             dxp_ref, dr1bt_ref, dgp_ref, dya_ref, dyc_ref, dglu_ref, dyssm_ref,
             dl1g_ref, dl1b_ref, dbg_ref):
        i = pl.program_id(0)

        @pl.when(i == 0)
        def _():
            dl1g_ref[...] = jnp.zeros_like(dl1g_ref)
            dl1b_ref[...] = jnp.zeros_like(dl1b_ref)
            dbg_ref[...] = jnp.zeros_like(dbg_ref)

        dx1 = dx1_ref[...]
        xhat = xh_ref[...]
        dl1g_ref[...] += _colsum(dx1 * xhat)
        dl1b_ref[...] += _colsum(dx1)
        dr1 = _layer_norm_bwd(dx1, xhat, rs_ref[...], lg_ref[...])
        dxp_ref[...] = ALPHA * dr1
        dr1b = dr1.astype(bf16)
        dr1bt_ref[...] = dr1b.T
        dm = _dot(dr1b, wout_ref[...], NT)

        _, ga, sb, ya, yb, yc = _branch_outputs(ys_ref, ain_ref, o_ref, wglu_ref, wco_ref, wxo_ref)
        gt = g_ref[...].astype(f32)
        branch = (ya, yb, yc)
        for j in range(3):
            cols = slice(j * D_MODEL, (j + 1) * D_MODEL)
            gj = gt[:, cols]
            dgp = dm * branch[j] * gj * (1.0 - gj)
            dbg_ref[:, cols] += _colsum(dgp)
            dgp_ref[:, cols] = dgp.astype(bf16)
        dya_ref[...] = (dm * gt[:, :D_MODEL]).astype(bf16)
        dyc_ref[...] = (dm * gt[:, 2 * D_MODEL:]).astype(bf16)
        dyb = dm * gt[:, D_MODEL:2 * D_MODEL]
        dga = (dyb * sb).astype(bf16)
        dgb = (dyb * ga * sb * (1.0 - sb)).astype(bf16)
        dglu_ref[:, :D_MODEL] = dga
        dglu_ref[:, D_MODEL:] = dgb
        dys = _dot(dga, wglu_ref[:D_MODEL, :]) + _dot(dgb, wglu_ref[D_MODEL:, :])
        dyssm_ref[...] = dys * _gelu_grad(ys_ref[...])

    row_cols = [(GATE_COLS, bf16), (D_MODEL, bf16), (D_MODEL, bf16), (2 * D_MODEL, bf16), (SSM_W, f32)]
    acc_shapes = [(1, D_MODEL), (1, D_MODEL), (1, GATE_COLS)]
    return pl.pallas_call(
        body, name="mid_bwd", grid=(n,),
        in_specs=[_row_spec(tm, D_MODEL), _row_spec(tm, D_MODEL), _row_spec(tm, 1), _row_spec(tm, GATE_COLS),
                  _row_spec(tm, CONV_W), _row_spec(tm, XATTN_W), _row_spec(tm, SSM_W),
                  _const_spec((1, D_MODEL)), _const_spec((D_MODEL, D_MODEL)), _const_spec((2 * D_MODEL, SSM_W)),
                  _const_spec((D_MODEL, CONV_W)), _const_spec((D_MODEL, XATTN_W))],
        out_specs=([_row_spec(tm, D_MODEL), _col_spec(D_MODEL, tm)] + [_row_spec(tm, c) for c, _ in row_cols]
                   + [_acc_spec(s) for s in acc_shapes]),
        out_shape=([_sds((s_len, D_MODEL), f32), _sds((D_MODEL, s_len), bf16)]
                   + [_sds((s_len, c), dt) for c, dt in row_cols] + [_sds(s, f32) for s in acc_shapes]),
        compiler_params=_cparams(("arbitrary",)),
    )(dx1, xhat1, rstd1, g, ain, ob, y_ssm, ln1_g, w_out, w_glu_t, w_co_t, w_xo_t)


def _ssm_bwd(u, dy, cm_all, b_half, c_half, pw, d_skip):
    s_len = u.shape[0]
    tb = SSM_BLOCK
    n = s_len // tb

    def body(u_ref, dy_ref, cm_ref, b_ref, c_ref, pw_ref, d_ref,
             du_ref, db_hbm, dc_hbm, da_ref, dd_ref,
             s_ref, g_ref, gcarry_ref, gcm_ref, db_ref, dc_ref, up_ref, dyp_ref, dup_ref, stage_ref):
        i = pl.program_id(0)

        @pl.when(i == 0)
        def _():
            gcarry_ref[...] = jnp.zeros_like(gcarry_ref)
            db_ref[...] = jnp.zeros_like(db_ref)
            dc_ref[...] = jnp.zeros_like(dc_ref)
            da_ref[...] = jnp.zeros_like(da_ref)
            dd_ref[...] = jnp.zeros_like(dd_ref)

        _rows_to_segments(u_ref, stage_ref, up_ref)
        _rows_to_segments(dy_ref, stage_ref, dyp_ref)
        u = up_ref[...]
        ub = u.astype(bf16)
        dy = dyp_ref[...]
        dyb = dy.astype(bf16)
        dd_ref[...] += _colsum(dy * u)

        for half in range(N_HALF):
            s_ref[:, half * HALF_COLS:(half + 1) * HALF_COLS] = _dot(ub[:, half * HALF_W:(half + 1) * HALF_W], b_ref[half])
        _ssm_scan(s_ref, pw_ref, cm_ref, reverse=False, unroll=True)

        for half in range(N_HALF):
            g_ref[:, half * HALF_COLS:(half + 1) * HALF_COLS] = _dot(dyb[:, half * HALF_W:(half + 1) * HALF_W], c_ref[half], NT)
        _ssm_scan(g_ref, pw_ref, None, reverse=True, unroll=True)
        _ssm_carries(0, g_ref, pw_ref, gcarry_ref, gcm_ref, reverse=True)
        _ssm_add_carry(g_ref, pw_ref, gcm_ref, reverse=True)

        for half in range(N_HALF):
            cols = slice(half * HALF_W, (half + 1) * HALF_W)
            scols = slice(half * HALF_COLS, (half + 1) * HALF_COLS)
            gb = g_ref[:, scols].astype(bf16)
            dup_ref[:, cols] = _dot(gb, b_ref[half], NT) + d_ref[:, cols] * dy[:, cols]
            db_ref[half] += _dot(ub[:, cols], gb, TN)
            dc_ref[half] += _dot(s_ref[:, scols].astype(bf16), dyb[:, cols], TN)
        _rows_from_segments(dup_ref, stage_ref, du_ref)

        for chunk in range(N_STATE // LANE_CHUNK):
            re, im = _state_cols(chunk)
            acc_r = da_ref[:, re]
            acc_i = da_ref[:, im]
            for k in range(SSM_SEG):
                rows = slice(k * SUBLANES, (k + 1) * SUBLANES)
                if k == 0:
                    pr, pi = cm_ref[:, re], cm_ref[:, im]
                else:
                    prev = slice((k - 1) * SUBLANES, k * SUBLANES)
                    pr, pi = s_ref[prev, re], s_ref[prev, im]
                gr, gi = g_ref[rows, re], g_ref[rows, im]
                acc_r = acc_r + (gr * pr + gi * pi)
                acc_i = acc_i + (gi * pr - gr * pi)
            da_ref[:, re] = acc_r
            da_ref[:, im] = acc_i

        @pl.when(i == n - 1)
        def _():
            pltpu.sync_copy(db_ref, db_hbm)
            pltpu.sync_copy(dc_ref, dc_hbm)

    rev = functools.partial(_row_spec, rev_n=n)
    any_spec = pl.BlockSpec(memory_space=pl.ANY)
    state_rows = pltpu.VMEM((tb, 2 * N_STATE), f32)
    seg_rows = pltpu.VMEM((SUBLANES, 2 * N_STATE), f32)
    tok_rows = pltpu.VMEM((tb, SSM_W), f32)
    return pl.pallas_call(
        body, name="ssm_bwd", grid=(n,),
        in_specs=[rev(tb, SSM_W), rev(tb, SSM_W), rev(SUBLANES, 2 * N_STATE),
                  _const_spec((N_HALF, HALF_W, HALF_COLS)), _const_spec((N_HALF, HALF_COLS, HALF_W)),
                  _const_spec((SSM_SEG, 2 * N_STATE)), _const_spec((1, SSM_W))],
        out_specs=[rev(tb, SSM_W), any_spec, any_spec, _acc_spec((SUBLANES, 2 * N_STATE)), _acc_spec((1, SSM_W))],
        out_shape=[_sds((s_len, SSM_W), f32), _sds((N_HALF, HALF_W, HALF_COLS), f32),
                   _sds((N_HALF, HALF_COLS, HALF_W), f32), _sds((SUBLANES, 2 * N_STATE), f32), _sds((1, SSM_W), f32)],
        scratch_shapes=[state_rows, state_rows, seg_rows, seg_rows,
                        pltpu.VMEM((N_HALF, HALF_W, HALF_COLS), f32), pltpu.VMEM((N_HALF, HALF_COLS, HALF_W), f32),
                        tok_rows, tok_rows, tok_rows, pltpu.VMEM((SSM_W // LANES, tb, LANES), f32)],
        compiler_params=_cparams(("arbitrary",)),
    )(u, dy, cm_all, b_half, c_half, pw, d_skip)


def _branch_bwd(dya, dyc, cin, q, kv, k_t, conv_w, w_co_t, w_xo_t, side_blocks):
    s_len = dya.shape[0]
    tm = TOKEN_TILE
    n = s_len // tm
    halo_blocks = tm // 8
    ns = len(side_blocks)
    conv_tile = _sds((8, CONV_W), f32)
    side_in_specs, side_shapes, side_sems = _side_gather_specs(list(side_blocks) + [conv_tile])

    def body(*refs):
        (dya_ref, dyc_ref, cin_ref, cprev_ref, q_ref, kv_ref, cw_ref, wco_ref, wxo_ref, kt_ref) = refs[:10]
        side_ins = refs[10:10 + ns]
        dconv_ref, dq_ref, dkv_ref = refs[10 + ns:13 + ns]
        side_outs = refs[13 + ns:14 + 2 * ns]
        zs_ref, dczs_ref, dcw_ref = refs[14 + 2 * ns:17 + 2 * ns]
        copies = _side_gather_copies(list(side_ins) + [dcw_ref], side_outs, *refs[17 + 2 * ns:])
        side, conv_side = copies[:ns * N_DEV], copies[ns * N_DEV:]
        i = pl.program_id(0)
        tile = n - 1 - i

        @pl.when(i == 0)
        def _():
            dcw_ref[...] = jnp.zeros_like(dcw_ref)
            dkv_ref[...] = jnp.zeros_like(dkv_ref)
            dczs_ref[tm:tm + 8, :] = jnp.zeros((8, CONV_W), f32)
            for cp in side:
                cp.start()

        cin = cin_ref[...]
        cb, cc, ch = cin[:, :CONV_W], cin[:, CONV_W:2 * CONV_W], cin[:, 2 * CONV_W:]
        z = cc * ch
        cprev = cprev_ref[...]
        zprev = cprev[:, CONV_W:2 * CONV_W] * cprev[:, 2 * CONV_W:]
        zs_ref[0:8, :] = jnp.where(tile == 0, 0.0, zprev)
        zs_ref[8:8 + tm, :] = z
        z1 = zs_ref[pl.ds(7, tm), :]
        z2 = zs_ref[pl.ds(6, tm), :]
        cw = cw_ref[...]
        cz = cw[0:1] * z2 + cw[1:2] * z1 + cw[2:3] * z

        dain = _dot(dya_ref[...], wco_ref[...])
        dcb = dain * cz
        dcz = dain * cb
        dczs_ref[0:tm, :] = dcz
        dcz1 = dczs_ref[pl.ds(1, tm), :]
        dcz2 = dczs_ref[pl.ds(2, tm), :]
        dz = cw[2:3] * dcz + cw[1:2] * dcz1 + cw[0:1] * dcz2
        dczs_ref[tm:tm + 8, :] = dczs_ref[0:8, :]
        dcw_ref[0:1, :] += _colsum(dcz * z2)
        dcw_ref[1:2, :] += _colsum(dcz * z1)
        dcw_ref[2:3, :] += _colsum(dcz * z)
        dconv_ref[:, :CONV_W] = dcb.astype(bf16)
        dconv_ref[:, CONV_W:2 * CONV_W] = (dz * ch).astype(bf16)
        dconv_ref[:, 2 * CONV_W:] = (dz * cc).astype(bf16)

        qb = q_ref[...]
        dob = _dot(dyc_ref[...], wxo_ref[...]).astype(bf16)
        kv = kv_ref[...]
        heads = range(HEADS)
        hcs = [slice(h * HEAD_DIM, (h + 1) * HEAD_DIM) for h in heads]
        vcs = [slice(XATTN_W + h * HEAD_DIM, XATTN_W + (h + 1) * HEAD_DIM) for h in heads]
        s_t = [_dot(kv[:, hcs[h]], qb[:, hcs[h]], NT) * (HEAD_DIM ** -0.5) for h in heads]
        dp_t = [_dot(kv[:, vcs[h]], dob[:, hcs[h]], NT) for h in heads]
        e_t = [jnp.exp(s_t[h] - jnp.max(s_t[h], axis=0, keepdims=True)) for h in heads]
        p_t = [e_t[h] / jnp.sum(e_t[h], axis=0, keepdims=True) for h in heads]
        dv = [_dot(p_t[h].astype(bf16), dob[:, hcs[h]]) for h in heads]
        ds_t = [(p_t[h] * (dp_t[h] - jnp.sum(dp_t[h] * p_t[h], axis=0, keepdims=True)) * (HEAD_DIM ** -0.5)).astype(bf16)
                for h in heads]
        dk = [_dot(ds_t[h], qb[:, hcs[h]]) for h in heads]
        dq_t = [_dot(kt_ref[hcs[h], :], ds_t[h]) for h in heads]
        dq_ref[...] = jnp.concatenate(dq_t, axis=0).T.astype(bf16)
        dkv_ref[...] += jnp.concatenate(dk + dv, axis=1)

        @pl.when(i == n - 1)
        def _():
            for cp in conv_side:
                cp.start()
            for cp in side + conv_side:
                cp.wait()

    rev = functools.partial(_row_spec, rev_n=n)
    prev_spec = pl.BlockSpec((8, 3 * CONV_W), lambda i: (jnp.maximum((n - 1 - i) * halo_blocks - 1, 0), 0))
    outs = pl.pallas_call(
        body, name="branch_bwd", grid=(n,),
        in_specs=[rev(tm, D_MODEL), rev(tm, D_MODEL), rev(tm, 3 * CONV_W), prev_spec, rev(tm, XATTN_W),
                  _const_spec((MEM_LEN, 2 * XATTN_W)), _const_spec((3, CONV_W)), _const_spec((D_MODEL, CONV_W)),
                  _const_spec((D_MODEL, XATTN_W)), _const_spec((XATTN_W, MEM_LEN))] + side_in_specs[:ns],
        out_specs=[rev(tm, 3 * CONV_W), rev(tm, XATTN_W), _acc_spec((MEM_LEN, 2 * XATTN_W))] + side_in_specs,
        out_shape=[_sds((s_len, 3 * CONV_W), bf16), _sds((s_len, XATTN_W), bf16),
                   _sds((MEM_LEN, 2 * XATTN_W), f32)] + side_shapes,
        scratch_shapes=[pltpu.VMEM((tm + 8, CONV_W), f32), pltpu.VMEM((tm + 8, CONV_W), f32),
                        pltpu.VMEM((8, CONV_W), f32)] + side_sems,
        compiler_params=_cparams(("arbitrary",)),
    )(dya, dyc, cin, cin, q, kv, conv_w, w_co_t, w_xo_t, k_t, *side_blocks)
    return outs[0], outs[1], outs[2], outs[3:]


def _in_proj_bwd(dgp, dconv, du, dq, dxp, w_in_t):
    s_len = dgp.shape[0]
    tm = 2 * TOKEN_TILE
    n = s_len // tm

    def body(dgp_ref, dconv_ref, du_ref, dq_ref, dxp_ref, win_ref, dx_ref, dproj_ref):
        dproj = jnp.concatenate([dgp_ref[...], dconv_ref[...], du_ref[...].astype(bf16), dq_ref[...]], axis=1)
        dproj_ref[...] = dproj
        dx_ref[...] = dxp_ref[...] + _dot(dproj, win_ref[...])

    return pl.pallas_call(
        body, name="in_proj_bwd", grid=(n,),
        in_specs=[_row_spec(tm, GATE_COLS), _row_spec(tm, 3 * CONV_W), _row_spec(tm, SSM_W), _row_spec(tm, XATTN_W),
                  _row_spec(tm, D_MODEL), _const_spec((IN_COLS, D_MODEL))],
        out_specs=[_row_spec(tm, D_MODEL), _row_spec(tm, IN_COLS)],
        out_shape=[_sds((s_len, D_MODEL), f32), _sds((s_len, IN_COLS), bf16)],
        compiler_params=_cparams(("parallel",)),
    )(dgp, dconv, du, dq, dxp, w_in_t)


N_CHIP = 4
CHIP_STEPS = [(1, 1), (1, 0), (0, 1), (0, 0)]


def _flip(v, d):
    return 1 - v if d else v


def _chip_order():
    x, y, _ = _mesh_place()
    return jnp.stack([2 * _flip(x, dx) + _flip(y, dy) for dx, dy in CHIP_STEPS]).astype(jnp.int32)


def _weight_grad_scatter(a_t, b, name, tm, tt):
    m, s_len = a_t.shape
    n_cols = b.shape[1]
    w = n_cols // N_DEV
    tn = 2 * w
    tm, tt = min(tm, m), min(tt, s_len)
    nm, nt = m // tm, s_len // tt
    assert m % tm == 0 and s_len % tt == 0

    def body(order_ref, a_ref, b_ref, recv_ref, acc_ref, send_ref, sib_ref, stash_ref,
             d2d_send, d2d_recv, ici_send, ici_recv, local_sem):
        del order_ref
        q, im, t = pl.program_id(0), pl.program_id(1), pl.program_id(2)
        x, y, c = _mesh_place()
        mesh_id = pl.DeviceIdType.MESH

        @pl.when(t == 0)
        def _():
            acc_ref[...] = jnp.zeros_like(acc_ref)

        acc_ref[...] += _dot(a_ref[...], b_ref[...])

        def to_sibling(qq, imm):
            rows = pl.ds(pl.multiple_of(imm * tm, tm), tm)
            return pltpu.make_async_remote_copy(
                src_ref=send_ref.at[qq, 0, rows, :], dst_ref=sib_ref.at[qq, rows, :],
                send_sem=d2d_send.at[qq], recv_sem=d2d_recv.at[qq, imm],
                device_id=(x, y, 1 - c), device_id_type=mesh_id)

        def finish_tile(qq, imm):
            rows = pl.ds(pl.multiple_of(imm * tm, tm), tm)
            to_sibling(qq, imm).wait_recv()
            both = stash_ref[...] + sib_ref[qq, rows, :].astype(f32)
            send_ref[qq, 1, rows, :] = both.astype(bf16)
            for step, (dx, dy) in enumerate(CHIP_STEPS):
                @pl.when(qq == step)
                def _(step=step, dx=dx, dy=dy):
                    src, dst = send_ref.at[step, 1, rows, :], recv_ref.at[step, rows, :]
                    if dx or dy:
                        pltpu.make_async_remote_copy(
                            src_ref=src, dst_ref=dst, send_sem=ici_send.at[step], recv_sem=ici_recv.at[step],
                            device_id=(_flip(x, dx), _flip(y, dy), c), device_id_type=mesh_id).start()
                    else:
                        pltpu.make_async_copy(src, dst, local_sem).start()

        @pl.when(t == nt - 1)
        def _():
            tile = q * nm + im

            @pl.when(tile > 0)
            def _():
                finish_tile((tile - 1) // nm, (tile - 1) % nm)

            rows = pl.ds(pl.multiple_of(im * tm, tm), tm)
            for core in (0, 1):
                @pl.when(c == core)
                def _(core=core):
                    other = 1 - core
                    send_ref[q, 0, rows, :] = acc_ref[:, other * w:(other + 1) * w].astype(bf16)
                    stash_ref[...] = acc_ref[:, core * w:(core + 1) * w]
            to_sibling(q, im).start()

            @pl.when(tile == N_CHIP * nm - 1)
            def _():
                finish_tile(q, im)

        @pl.when((q == N_CHIP - 1) & (im == nm - 1) & (t == nt - 1))
        def _():
            for step, (dx, dy) in enumerate(CHIP_STEPS):
                pltpu.make_async_remote_copy(
                    src_ref=send_ref.at[step, 0], dst_ref=sib_ref.at[step],
                    send_sem=d2d_send.at[step], recv_sem=d2d_recv.at[step, 0],
                    device_id=(x, y, 1 - c), device_id_type=mesh_id).wait_send()
                src, dst = send_ref.at[step, 1], recv_ref.at[step]
                if dx or dy:
                    pltpu.make_async_remote_copy(
                        src_ref=src, dst_ref=dst, send_sem=ici_send.at[step], recv_sem=ici_recv.at[step],
                        device_id=(_flip(x, dx), _flip(y, dy), c), device_id_type=mesh_id).wait()
                else:
                    pltpu.make_async_copy(src, dst, local_sem).wait()

    grid_spec = pltpu.PrefetchScalarGridSpec(
        num_scalar_prefetch=1, grid=(N_CHIP, nm, nt),
        in_specs=[pl.BlockSpec((tm, tt), lambda q, im, t, order: (im, t)),
                  pl.BlockSpec((tt, tn), lambda q, im, t, order: (t, order[q]))],
        out_specs=pl.BlockSpec(memory_space=pl.ANY),
        scratch_shapes=[pltpu.VMEM((tm, tn), f32), pltpu.VMEM((N_CHIP, 2, m, w), bf16), pltpu.VMEM((N_CHIP, m, w), bf16),
                        pltpu.VMEM((tm, w), f32),
                        pltpu.SemaphoreType.DMA((N_CHIP,)), pltpu.SemaphoreType.DMA((N_CHIP, nm)),
                        pltpu.SemaphoreType.DMA((N_CHIP - 1,)), pltpu.SemaphoreType.DMA((N_CHIP - 1,)),
                        pltpu.SemaphoreType.DMA])
    return pl.pallas_call(
        body, name=name, grid_spec=grid_spec,
        out_shape=_sds((N_CHIP, m, w), bf16),
        compiler_params=_cparams(("arbitrary", "arbitrary", "arbitrary")),
    )(_chip_order(), a_t, b)


def _adamw(w, g, m, v):
    m = ADAM_B1 * m + (1.0 - ADAM_B1) * g
    v = ADAM_B2 * v + (1.0 - ADAM_B2) * jnp.square(g)
    m_hat = m / (1.0 - ADAM_B1 ** ADAM_STEP)
    v_hat = v / (1.0 - ADAM_B2 ** ADAM_STEP)
    delta = -ADAM_LR * (m_hat / (jnp.sqrt(v_hat) + ADAM_EPS) + ADAM_WD * w)
    return delta, m, v


def _sum_parts(p_ref):
    g = p_ref[0].astype(f32)
    for j in range(1, p_ref.shape[0]):
        g = g + p_ref[j].astype(f32)
    return g


def _adamw_update(w, m, v, parts, name, transposed):
    rows, cols = w.shape
    n_parts = parts.shape[0]
    if transposed:
        tc = 256
        steps = cols // tc
        p_spec = pl.BlockSpec((n_parts, tc, rows), lambda i: (0, i, 0))
        spec = pl.BlockSpec((rows, tc), lambda i: (0, i))
    else:
        tr = next(t for t in (256, 128, 64, 32, 16, 8) if rows % t == 0)
        steps = rows // tr
        p_spec = pl.BlockSpec((n_parts, tr, cols), lambda i: (0, i, 0))
        spec = pl.BlockSpec((tr, cols), lambda i: (i, 0))

    def body(w_ref, p_ref, m_ref, v_ref, g_ref, d_ref, nm_ref, nv_ref):
        g = _sum_parts(p_ref)
        if transposed:
            g = g.T
        g_ref[...] = g
        d_ref[...], nm_ref[...], nv_ref[...] = _adamw(w_ref[...], g, m_ref[...], v_ref[...])

    return pl.pallas_call(
        body, name=name, grid=(steps,),
        in_specs=[spec, p_spec, spec, spec], out_specs=[spec] * 4,
        out_shape=[_sds((rows, cols), f32)] * 4,
        compiler_params=_cparams(("parallel",)),
    )(w, parts, m, v)


SMALL_GROUPS = [
    (["b_gate", "ln1_g", "ln1_b", "b_up", "b_down", "ln2_g", "ln2_b", "ssm_d"], 1),
    (["ssm_lam_re", "ssm_lam_im", "ssm_c_re", "ssm_c_im", "ssm_b_re", "ssm_b_im"], 0),
    (["conv_w"], 0),
    (["ssm_log_dt"], 0),
]


def _sum_small(group_parts):
    def body(*refs):
        n = len(refs) // 2
        for p_ref, o_ref in zip(refs[:n], refs[n:]):
            o_ref[...] = _sum_parts(p_ref)

    return pl.pallas_call(
        body, name="sum_small",
        out_shape=[_sds(p.shape[1:], f32) for p in group_parts],
        compiler_params=_cparams(),
    )(*group_parts)


def _adamw_small(ws, ms, vs, group_sums):
    names = [k for group, _ in SMALL_GROUPS for k in group]
    n = len(names)

    def body(*refs):
        w_refs, m_refs, v_refs = (dict(zip(names, refs[j * n:(j + 1) * n])) for j in range(3))
        p_refs = refs[3 * n:3 * n + len(SMALL_GROUPS)]
        out_refs = [dict(zip(names, refs[3 * n + len(SMALL_GROUPS) + j * n:][:n])) for j in range(4)]
        for (group, axis), p_ref in zip(SMALL_GROUPS, p_refs):
            total = p_ref[...]
            off = 0
            for k in group:
                size = SMALL[k][axis]
                g = total[:, off:off + size] if axis == 1 else total[off:off + size, :]
                off += size
                d, nm, nv = _adamw(w_refs[k][...], g, m_refs[k][...], v_refs[k][...])
                for j, val in enumerate((g, d, nm, nv)):
                    out_refs[j][k][...] = val

    res = pl.pallas_call(
        body, name="adamw_small",
        out_shape=[_sds(SMALL[k], f32) for _ in range(4) for k in names],
        compiler_params=_cparams(),
    )(*[ws[k] for k in names], *[ms[k] for k in names], *[vs[k] for k in names], *group_sums)
    return [dict(zip(names, res[j * n:(j + 1) * n])) for j in range(4)]


def _ssm_discretize(lam_re, lam_im, log_dt, b_re, b_im):
    dt = jnp.exp(log_dt)[:, None]
    mag = jnp.exp(lam_re * dt)
    abar_r = mag * jnp.cos(lam_im * dt)
    abar_i = mag * jnp.sin(lam_im * dt)
    den = lam_re * lam_re + lam_im * lam_im
    nr = abar_r - 1.0
    ni = abar_i
    kr = (nr * lam_re + ni * lam_im) / den
    ki = (ni * lam_re - nr * lam_im) / den
    bbar_r = kr[:, None, :] * b_re - ki[:, None, :] * b_im
    bbar_i = kr[:, None, :] * b_im + ki[:, None, :] * b_re
    return abar_r, abar_i, bbar_r, bbar_i


def _state_layout(re, im):
    parts = []
    for half in range(N_HALF):
        cols = slice(half * HALF_STATE, (half + 1) * HALF_STATE)
        parts += [re[..., cols], im[..., cols]]
    return jnp.concatenate(parts, axis=-1)


def _state_unlayout(a):
    re = jnp.concatenate([a[..., _half_cols(h)[0]] for h in range(N_HALF)], axis=-1)
    im = jnp.concatenate([a[..., _half_cols(h)[1]] for h in range(N_HALF)], axis=-1)
    return re, im


def _abar_powers(abar_r, abar_i):
    pr, pi = abar_r.reshape(1, N_STATE), abar_i.reshape(1, N_STATE)
    while pr.shape[0] < SSM_SEG:
        tr, ti = pr[-1:], pi[-1:]
        pr, pi = (jnp.concatenate([pr, pr * tr - pi * ti], axis=0), jnp.concatenate([pi, pr * ti + pi * tr], axis=0))
    return _state_layout(pr, pi)


HALF_GROUPS = SSM_GROUPS // N_HALF


def _half_block_diag(blocks):
    _, r, c = blocks.shape
    eye = jnp.eye(HALF_GROUPS, dtype=blocks.dtype)
    b4 = blocks.reshape(N_HALF, HALF_GROUPS, r, c)
    return jnp.einsum("ngrc,gk->ngrkc", b4, eye).reshape(N_HALF, HALF_GROUPS * r, HALF_GROUPS * c)


def _half_diag_blocks(mat, r, c):
    eye = jnp.eye(HALF_GROUPS, dtype=mat.dtype)
    m5 = mat.reshape(N_HALF, HALF_GROUPS, r, HALF_GROUPS, c)
    return jnp.einsum("ngrkc,gk->ngrc", m5, eye).reshape(SSM_GROUPS, r, c)


BIG = ["w_in", "w_conv_out", "w_glu", "w_kv", "w_xattn_out", "w_out", "w_up", "w_down"]
GATHER_TRANSPOSED = ["w_conv_out", "w_glu", "w_xattn_out", "w_up"]
PARTS_TRANSPOSED = ["w_in", "w_kv", "w_out", "w_down"]
SMALL = {"b_gate": (1, GATE_COLS), "conv_w": (3, CONV_W), "ssm_lam_re": (SSM_GROUPS, SSM_STATE),
         "ssm_lam_im": (SSM_GROUPS, SSM_STATE), "ssm_log_dt": (1, SSM_GROUPS),
         "ssm_b_re": (SSM_W, SSM_STATE), "ssm_b_im": (SSM_W, SSM_STATE),
         "ssm_c_re": (SSM_W, SSM_STATE), "ssm_c_im": (SSM_W, SSM_STATE), "ssm_d": (1, SSM_W),
         "ln1_g": (1, D_MODEL), "ln1_b": (1, D_MODEL), "b_up": (1, D_FF), "b_down": (1, D_MODEL),
         "ln2_g": (1, D_MODEL), "ln2_b": (1, D_MODEL)}
WEIGHTS = ["w_in", "b_gate", "conv_w", "w_conv_out", "ssm_lam_re", "ssm_lam_im", "ssm_log_dt", "ssm_b_re", "ssm_b_im",
           "ssm_c_re", "ssm_c_im", "ssm_d", "w_glu", "w_kv", "w_xattn_out", "w_out", "ln1_g", "ln1_b", "w_up", "b_up",
           "w_down", "b_down", "ln2_g", "ln2_b"]


def _local_step(x, mem, tgt, full, late, small):
    lam_re, lam_im, log_dt = small["ssm_lam_re"], small["ssm_lam_im"], small["ssm_log_dt"].reshape(SSM_GROUPS)
    c_shape = (SSM_GROUPS, SSM_GROUP, SSM_STATE)
    disc, disc_vjp = jax.vjp(_ssm_discretize, lam_re, lam_im, log_dt,
                             small["ssm_b_re"].reshape(c_shape), small["ssm_b_im"].reshape(c_shape))
    abar_r, abar_i, bbar_r, bbar_i = disc
    pw = _abar_powers(abar_r, abar_i)
    c_re, c_im = small["ssm_c_re"].reshape(c_shape), small["ssm_c_im"].reshape(c_shape)
    b_half = jnp.concatenate([_half_block_diag(bbar_r), _half_block_diag(bbar_i)], axis=2).astype(bf16)
    c_half = jnp.concatenate([_half_block_diag(c_re.transpose(0, 2, 1)), -_half_block_diag(c_im.transpose(0, 2, 1))],
                             axis=1).astype(bf16)

    s_len = x.shape[0]
    stack = lambda a: a.reshape(-1, a.shape[-1])
    kv, k_t, memb = _kv_proj(mem, full["w_kv"])
    (xbt, g, cin, u, q, ain, ob, aint, obt), side = _in_proj(
        x, full["w_in"], small["b_gate"], small["conv_w"], kv,
        [late[k] for k in ("w_glu", "w_conv_out", "w_xattn_out", "w_out", "w_up")])
    w_glu_t, w_co_t, w_xo_t, w_out, w_up_t = (stack(a) for a in side)
    y_ssm, cm_all, side = _ssm_fwd(u, b_half, c_half, pw, small["ssm_d"], [late["w_down"]])
    w_down = stack(side[0])
    ysbt, mb, xhat1, rstd1 = _mid_fwd(y_ssm, g, ain, ob, x, w_glu_t, w_co_t, w_xo_t, w_out,
                                      small["ln1_g"], small["ln1_b"])
    (x1bt, hdn, dr2bt, dpre, dx1, loss, dl2g, dl2b, dbdn, dbup) = _mlp_fwd_bwd(
        xhat1, tgt, small["ln1_g"], small["ln1_b"], w_up_t, small["b_up"], w_down,
        small["b_down"], small["ln2_g"], small["ln2_b"])
    recv = {}
    recv["w_down"] = _weight_grad_scatter(dr2bt, hdn, "dw_down", tm=512, tt=2048)
    recv["w_up"] = _weight_grad_scatter(x1bt, dpre, "dw_up", tm=512, tt=2048)
    (dxp, dr1bt, dgp, dya, dyc, dglu, dyssm, dl1g, dl1b, dbg) = _mid_bwd(
        dx1, xhat1, rstd1, g, ain, ob, y_ssm, small["ln1_g"], w_out, w_glu_t, w_co_t, w_xo_t)
    recv["w_out"] = _weight_grad_scatter(dr1bt, mb, "dw_out", tm=512, tt=s_len)
    recv["w_glu"] = _weight_grad_scatter(ysbt, dglu, "dw_glu", tm=512, tt=s_len)
    du, db_half, dc_half, da8, dd = _ssm_bwd(u, dyssm, cm_all, b_half, c_half, pw, small["ssm_d"])
    dabar_r, dabar_i = _state_unlayout(jnp.sum(da8, axis=0))
    dbbar_r = _half_diag_blocks(db_half[:, :, :HALF_STATE], SSM_GROUP, SSM_STATE)
    dbbar_i = _half_diag_blocks(db_half[:, :, HALF_STATE:], SSM_GROUP, SSM_STATE)
    g_shape = (SSM_GROUPS, SSM_STATE)
    dlam_re, dlam_im, dlog_dt, db_re, db_im = disc_vjp(
        (dabar_r.reshape(g_shape), dabar_i.reshape(g_shape), dbbar_r, dbbar_i))
    dc_re = _half_diag_blocks(dc_half[:, :HALF_STATE, :], SSM_STATE, SSM_GROUP).transpose(0, 2, 1)
    dc_im = -_half_diag_blocks(dc_half[:, HALF_STATE:, :], SSM_STATE, SSM_GROUP).transpose(0, 2, 1)

    small_grads = {
        "b_gate": dbg, "ssm_lam_re": dlam_re, "ssm_lam_im": dlam_im, "ssm_log_dt": dlog_dt,
        "ssm_b_re": db_re, "ssm_b_im": db_im, "ssm_c_re": dc_re, "ssm_c_im": dc_im, "ssm_d": dd,
        "ln1_g": dl1g, "ln1_b": dl1b, "b_up": dbup, "b_down": dbdn, "ln2_g": dl2g, "ln2_b": dl2b,
    }
    small_grads = {k: a.reshape(SMALL[k]) for k, a in small_grads.items()}
    groups = [(group, axis) for group, axis in SMALL_GROUPS if group != ["conv_w"]]
    stacks = [jnp.concatenate([small_grads[k] for k in group], axis=axis) if len(group) > 1 else small_grads[group[0]]
              for group, axis in groups]
    n_rowvec = stacks[0].shape[1]
    stacks[0] = jnp.concatenate([stacks[0], loss], axis=1)
    dense = lambda a: a.reshape(-1, LANES) if a.size % LANES == 0 else a
    dconv, dq, dkv, group_parts = _branch_bwd(dya, dyc, cin, q, kv, k_t, small["conv_w"], w_co_t, w_xo_t,
                                              [dense(a) for a in stacks])
    recv["w_conv_out"] = _weight_grad_scatter(aint, dya, "dw_conv_out", tm=512, tt=s_len)
    recv["w_xattn_out"] = _weight_grad_scatter(obt, dyc, "dw_xattn_out", tm=512, tt=s_len)
    recv["w_kv"] = _weight_grad_scatter(dkv.T.astype(bf16), memb, "dw_kv", tm=D_MODEL, tt=MEM_LEN)
    dx, dproj = _in_proj_bwd(dgp, dconv, du, dq, dxp, full["w_in"])
    recv["w_in"] = _weight_grad_scatter(xbt, dproj, "dw_in", tm=512, tt=2048)
    sums = _sum_small(group_parts)
    group_sums = dict(zip([tuple(group) for group, _ in groups], [s.reshape(a.shape) for s, a in zip(sums, stacks)]))
    group_sums[("conv_w",)] = sums[-1][0:3]
    first = tuple(groups[0][0])
    loss_all = group_sums[first][0, n_rowvec]
    group_sums[first] = group_sums[first][:, :n_rowvec]
    return loss_all, dx, recv, [group_sums[tuple(group)] for group, _ in SMALL_GROUPS]


def kernel(x, mem, w_in, b_gate, conv_w, w_conv_out, ssm_lam_re, ssm_lam_im, ssm_log_dt, ssm_b_re, ssm_b_im, ssm_c_re, ssm_c_im, ssm_d, w_glu, w_kv, w_xattn_out, w_out, ln1_g, ln1_b, w_up, b_up, w_down, b_down, ln2_g, ln2_b, loss_target, m_w_in, m_b_gate, m_conv_w, m_w_conv_out, m_ssm_lam_re, m_ssm_lam_im, m_ssm_log_dt, m_ssm_b_re, m_ssm_b_im, m_ssm_c_re, m_ssm_c_im, m_ssm_d, m_w_glu, m_w_kv, m_w_xattn_out, m_w_out, m_ln1_g, m_ln1_b, m_w_up, m_b_up, m_w_down, m_b_down, m_ln2_g, m_ln2_b, v_w_in, v_b_gate, v_conv_w, v_w_conv_out, v_ssm_lam_re, v_ssm_lam_im, v_ssm_log_dt, v_ssm_b_re, v_ssm_b_im, v_ssm_c_re, v_ssm_c_im, v_ssm_d, v_w_glu, v_w_kv, v_w_xattn_out, v_w_out, v_ln1_g, v_ln1_b, v_w_up, v_b_up, v_w_down, v_b_down, v_ln2_g, v_ln2_b):
    w = dict(w_in=w_in, b_gate=b_gate, conv_w=conv_w, w_conv_out=w_conv_out, ssm_lam_re=ssm_lam_re,
             ssm_lam_im=ssm_lam_im, ssm_log_dt=ssm_log_dt, ssm_b_re=ssm_b_re, ssm_b_im=ssm_b_im, ssm_c_re=ssm_c_re,
             ssm_c_im=ssm_c_im, ssm_d=ssm_d, w_glu=w_glu, w_kv=w_kv, w_xattn_out=w_xattn_out, w_out=w_out,
             ln1_g=ln1_g, ln1_b=ln1_b, w_up=w_up, b_up=b_up, w_down=w_down, b_down=b_down, ln2_g=ln2_g, ln2_b=ln2_b)
    m = dict(w_in=m_w_in, b_gate=m_b_gate, conv_w=m_conv_w, w_conv_out=m_w_conv_out, ssm_lam_re=m_ssm_lam_re,
             ssm_lam_im=m_ssm_lam_im, ssm_log_dt=m_ssm_log_dt, ssm_b_re=m_ssm_b_re, ssm_b_im=m_ssm_b_im,
             ssm_c_re=m_ssm_c_re, ssm_c_im=m_ssm_c_im, ssm_d=m_ssm_d, w_glu=m_w_glu, w_kv=m_w_kv,
             w_xattn_out=m_w_xattn_out, w_out=m_w_out, ln1_g=m_ln1_g, ln1_b=m_ln1_b, w_up=m_w_up, b_up=m_b_up,
             w_down=m_w_down, b_down=m_b_down, ln2_g=m_ln2_g, ln2_b=m_ln2_b)
    v = dict(w_in=v_w_in, b_gate=v_b_gate, conv_w=v_conv_w, w_conv_out=v_w_conv_out, ssm_lam_re=v_ssm_lam_re,
             ssm_lam_im=v_ssm_lam_im, ssm_log_dt=v_ssm_log_dt, ssm_b_re=v_ssm_b_re, ssm_b_im=v_ssm_b_im,
             ssm_c_re=v_ssm_c_re, ssm_c_im=v_ssm_c_im, ssm_d=v_ssm_d, w_glu=v_w_glu, w_kv=v_w_kv,
             w_xattn_out=v_w_xattn_out, w_out=v_w_out, ln1_g=v_ln1_g, ln1_b=v_ln1_b, w_up=v_w_up, b_up=v_b_up,
             w_down=v_w_down, b_down=v_b_down, ln2_g=v_ln2_g, ln2_b=v_ln2_b)
    out_shapes = {k: a.shape for k, a in w.items()}
    swapped = ("w_in", "ssm_b_re", "ssm_b_im")

    def shard2d(k, a):
        if k in swapped:
            a = jnp.swapaxes(a, -1, -2)
        if k in SMALL:
            return a.reshape((3, CONV_W // N_DEV) if k == "conv_w" else SMALL[k])
        return a[0]

    def result(k, a):
        if k in swapped:
            shape = out_shapes[k]
            return jnp.swapaxes(a.reshape(shape[:-2] + (shape[-1], shape[-2])), -1, -2)
        return a.reshape(out_shapes[k])

    w, m, v = ({k: shard2d(k, a) for k, a in d.items()} for d in (w, m, v))

    shards = {k: w[k].T.astype(bf16) if k in GATHER_TRANSPOSED else w[k].astype(bf16) for k in BIG}
    conv_pad = jnp.pad(w["conv_w"], ((0, 5), (0, LANES - CONV_W // N_DEV)))
    early = ["w_in", "w_kv"]
    gathered = _all_gather([shards[k] for k in early] + [conv_pad], "gather_weights")
    full = {k: a.reshape(-1, a.shape[-1]) for k, a in zip(early, gathered[:-1])}
    late = {k: shards[k] for k in BIG if k not in early}
    conv_full = gathered[-1][:, :3, :CONV_W // N_DEV].transpose(1, 0, 2).reshape(3, CONV_W)
    small = {k: (conv_full if k == "conv_w" else w[k]) for k in SMALL}

    loss, dx, recv, group_sums = _local_step(x[0], mem[0], loss_target[0], full, late, small)

    grads, deltas, new_m, new_v = {}, {}, {}, {}
    for k in BIG:
        res = _adamw_update(w[k], m[k], v[k], recv[k], "adamw_" + k, transposed=k in PARTS_TRANSPOSED)
        grads[k], deltas[k], new_m[k], new_v[k] = res

    widen = lambda k, a: jnp.tile(a, (1, N_DEV)) if k == "conv_w" else a
    res = _adamw_small(small, {k: widen(k, m[k]) for k in SMALL}, {k: widen(k, v[k]) for k in SMALL}, group_sums)
    dev = _slot(_mesh_place())
    for d, small_res in zip((grads, deltas, new_m, new_v), res):
        for k, a in small_res.items():
            if k == "conv_w":
                a = lax.dynamic_slice_in_dim(a, dev * (CONV_W // N_DEV), CONV_W // N_DEV, axis=1)
            d[k] = a

    outs = [loss, dx[None]]
    for d in (grads, deltas, new_m, new_v):
        outs += [result(k, d[k]) for k in WEIGHTS]
    return tuple(outs)
```

```python
import functools
import math

import jax
import jax.numpy as jnp
from jax import lax
from jax.experimental import pallas as pl
from jax.experimental.pallas import tpu as pltpu

f32 = jnp.float32
bf16 = jnp.bfloat16

D_MODEL = 1024
MEM_LEN = 256
GATE_COLS = 3 * D_MODEL
CONV_W = 512
SSM_W = 512
XATTN_W = 512
HEADS = 4
HEAD_DIM = 128
D_FF = 4096
IN_COLS = GATE_COLS + 3 * CONV_W + SSM_W + XATTN_W
SSM_GROUPS = 32
SSM_GROUP = 16
SSM_STATE = 64
N_STATE = SSM_GROUPS * SSM_STATE
ALPHA = 2.0 ** 0.25
LN_EPS = 1e-5
N_DEV = 8

ADAM_LR = 0.001
ADAM_B1 = 0.9
ADAM_B2 = 0.999
ADAM_EPS = 1e-08
ADAM_WD = 0.01
ADAM_STEP = 10

VMEM_LIMIT_V7X = 56 * 2 ** 20
SUBLANES = 8
LANES = 128

TOKEN_TILE = 256
SSM_BLOCK = 512
SSM_SEG = SSM_BLOCK // SUBLANES
LANE_CHUNK = 256
N_HALF = 2
HALF_W = SSM_W // N_HALF
HALF_STATE = N_STATE // N_HALF
HALF_COLS = 2 * HALF_STATE

NT = (((1,), (1,)), ((), ()))
TN = (((0,), (0,)), ((), ()))
NN = (((1,), (0,)), ((), ()))


def _dot(a, b, dims=NN):
    return lax.dot_general(a, b, dims, preferred_element_type=f32)


def _cparams(sem=None):
    return pltpu.CompilerParams(dimension_semantics=sem, vmem_limit_bytes=VMEM_LIMIT_V7X)


def _row_spec(tm, cols, rev_n=None):
    if rev_n is None:
        return pl.BlockSpec((tm, cols), lambda i: (i, 0))
    return pl.BlockSpec((tm, cols), lambda i: (rev_n - 1 - i, 0))


def _col_spec(rows, tm):
    return pl.BlockSpec((rows, tm), lambda i: (0, i))


def _const_spec(shape):
    nd = len(shape)
    return pl.BlockSpec(shape, lambda *_: (0,) * nd, pipeline_mode=pl.Buffered(1))


def _acc_spec(shape):
    nd = len(shape)
    return pl.BlockSpec(shape, lambda *_: (0,) * nd)


def _sds(shape, dtype):
    return jax.ShapeDtypeStruct(shape, dtype)


def _gelu(x):
    c = math.sqrt(2.0 / math.pi)
    return 0.5 * x * (1.0 + jnp.tanh(c * (x + 0.044715 * x * x * x)))


def _gelu_grad(x):
    c = math.sqrt(2.0 / math.pi)
    t = jnp.tanh(c * (x + 0.044715 * x * x * x))
    return 0.5 * (1.0 + t) + 0.5 * x * (1.0 - t * t) * c * (1.0 + 3.0 * 0.044715 * x * x)


def _colsum(a):
    return jnp.sum(a, axis=0, keepdims=True)


def _mesh_place():
    return lax.axis_index("x"), lax.axis_index("y"), lax.axis_index("c")


def _slot(p):
    return 4 * p[0] + 2 * p[1] + p[2]


def _other_devices(me):
    x, y, c = me
    flip = lambda v, d: 1 - v if d else v
    return [(flip(x, dx), flip(y, dy), flip(c, dc)) for dx in (0, 1) for dy in (0, 1) for dc in (0, 1)][1:]


def _all_gather(blocks, name):
    n = len(blocks)

    def body(*refs):
        ins, outs = refs[:n], refs[n:2 * n]
        send_sems, recv_sems, local_sems = refs[2 * n:]
        x, y, c = _mesh_place()
        me, sibling = (x, y, c), (x, y, 1 - c)
        chips = [(1 - x, y), (x, 1 - y), (1 - x, 1 - y)]

        def copy(a, k, block, to, src=None):
            rows = outs[a].at[_slot(block)]
            return pltpu.make_async_remote_copy(
                src_ref=rows if src is None else src, dst_ref=rows,
                send_sem=send_sems.at[a, k], recv_sem=recv_sems.at[a, k],
                device_id=to, device_id_type=pl.DeviceIdType.MESH)

        mine = [pltpu.make_async_copy(ins[a], outs[a].at[_slot(me)], local_sems.at[a]) for a in range(n)]
        for cp in mine:
            cp.start()
        first = []
        for a in range(n):
            first.append(copy(a, 0, me, sibling, src=ins[a]))
            first += [copy(a, 1 + j, me, (*chip, c), src=ins[a]) for j, chip in enumerate(chips)]
        for cp in first:
            cp.start()
        passed = []
        for a in range(n):
            for j, chip in enumerate(chips):
                copy(a, 1 + j, (*chip, c), me).wait_recv()
                fwd = copy(a, 4 + j, (*chip, c), sibling)
                fwd.start()
                passed.append(fwd)
        for a in range(n):
            copy(a, 0, sibling, me).wait_recv()
            for j, chip in enumerate(chips):
                copy(a, 4 + j, (*chip, 1 - c), me).wait_recv()
        for cp in first + passed:
            cp.wait_send()
        for cp in mine:
            cp.wait()

    any_spec = pl.BlockSpec(memory_space=pl.ANY)
    return pl.pallas_call(
        body, name=name,
        out_shape=[_sds((N_DEV,) + b.shape, b.dtype) for b in blocks],
        in_specs=[any_spec] * n, out_specs=[any_spec] * n,
        scratch_shapes=[pltpu.SemaphoreType.DMA((n, 7)), pltpu.SemaphoreType.DMA((n, 7)),
                        pltpu.SemaphoreType.DMA((n,))],
    )(*blocks)


def _side_gather_copies(ins, outs, send_sems, recv_sems, local_sems):
    me = _mesh_place()
    copies = []
    for a, (src, dst) in enumerate(zip(ins, outs)):
        copies.append(pltpu.make_async_copy(src, dst.at[_slot(me)], local_sems.at[a]))
        for k, peer in enumerate(_other_devices(me)):
            copies.append(pltpu.make_async_remote_copy(
                src_ref=src, dst_ref=dst.at[_slot(me)], send_sem=send_sems.at[a, k], recv_sem=recv_sems.at[a, k],
                device_id=peer, device_id_type=pl.DeviceIdType.MESH))
    return copies


def _side_gather_two_level(ins, outs, send_sems, recv_sems, local_sems):
    x, y, c = _mesh_place()
    me, sibling = (x, y, c), (x, y, 1 - c)
    chips = [(1 - x, y), (x, 1 - y), (1 - x, 1 - y)]

    def copy(a, k, block, to, src=None):
        rows = outs[a].at[_slot(block)]
        return pltpu.make_async_remote_copy(
            src_ref=rows if src is None else src, dst_ref=rows, send_sem=send_sems.at[a, k], recv_sem=recv_sems.at[a, k],
            device_id=to, device_id_type=pl.DeviceIdType.MESH)

    n = len(ins)
    mine = [pltpu.make_async_copy(ins[a], outs[a].at[_slot(me)], local_sems.at[a]) for a in range(n)]
    first = [copy(a, 0, me, sibling, src=ins[a]) for a in range(n)]
    first += [copy(a, 1 + j, me, (*chip, c), src=ins[a]) for a in range(n) for j, chip in enumerate(chips)]
    passed = [copy(a, 4 + j, (*chip, c), sibling) for a in range(n) for j, chip in enumerate(chips)]

    def start():
        for cp in mine + first:
            cp.start()

    def forward():
        for a in range(n):
            for j, chip in enumerate(chips):
                copy(a, 1 + j, (*chip, c), me).wait_recv()
        for cp in passed:
            cp.start()

    def finish():
        for a in range(n):
            copy(a, 0, sibling, me).wait_recv()
            for j, chip in enumerate(chips):
                copy(a, 4 + j, (*chip, 1 - c), me).wait_recv()
        for cp in first + passed:
            cp.wait_send()
        for cp in mine:
            cp.wait()

    return start, forward, finish


def _side_gather_specs(blocks):
    n = len(blocks)
    any_spec = pl.BlockSpec(memory_space=pl.ANY)
    return ([any_spec] * n, [_sds((N_DEV,) + b.shape, b.dtype) for b in blocks],
            [pltpu.SemaphoreType.DMA((n, N_DEV - 1)), pltpu.SemaphoreType.DMA((n, N_DEV - 1)),
             pltpu.SemaphoreType.DMA((n,))])


def _kv_proj(mem, w_kv):
    def body(mem_ref, w_ref, kv_ref, kt_ref, memb_ref):
        mb = mem_ref[...].astype(bf16)
        memb_ref[...] = mb
        kv = _dot(mb, w_ref[...]).astype(bf16)
        kv_ref[...] = kv
        kt_ref[...] = kv[:, :XATTN_W].T

    return pl.pallas_call(
        body, name="kv_proj",
        out_shape=[_sds((MEM_LEN, 2 * XATTN_W), bf16), _sds((XATTN_W, MEM_LEN), bf16), _sds((MEM_LEN, D_MODEL), bf16)],
        compiler_params=_cparams(),
    )(mem, w_kv)


def _attention_probs(qb, kv_ref, h):
    kh = kv_ref[:, h * HEAD_DIM:(h + 1) * HEAD_DIM]
    s = _dot(qb[:, h * HEAD_DIM:(h + 1) * HEAD_DIM], kh, NT) * (HEAD_DIM ** -0.5)
    e = jnp.exp(s - jnp.max(s, axis=-1, keepdims=True))
    return e / jnp.sum(e, axis=-1, keepdims=True)


def _in_proj(x, w_in_t, b_gate, conv_w, kv, side_blocks):
    s_len = x.shape[0]
    tm = 2 * TOKEN_TILE
    n = s_len // tm
    ns = len(side_blocks)
    side_in_specs, side_shapes, side_sems = _side_gather_specs(side_blocks)

    def body(*refs):
        (x_ref, win_ref, bg_ref, cw_ref, kv_ref) = refs[:5]
        side_ins = refs[5:5 + ns]
        (xbt_ref, g_ref, cin_ref, u_ref, q_ref, ain_ref, o_ref, aint_ref, ot_ref) = refs[5 + ns:14 + ns]
        side_outs = refs[14 + ns:14 + 2 * ns]
        zs_ref = refs[14 + 2 * ns]
        side_start, side_forward, side_finish = _side_gather_two_level(side_ins, side_outs, *refs[15 + 2 * ns:])
        i = pl.program_id(0)
        pl.when(i == 0)(side_start)
        pl.when(i == (3 * n) // 4)(side_forward)

        xb = x_ref[...].astype(bf16)
        xbt_ref[...] = xb.T
        proj = _dot(xb, win_ref[...], NT)
        g_ref[...] = jax.nn.sigmoid(proj[:, :GATE_COLS] + bg_ref[...]).astype(bf16)
        cin = proj[:, GATE_COLS:GATE_COLS + 3 * CONV_W]
        cin_ref[...] = cin
        u_ref[...] = proj[:, GATE_COLS + 3 * CONV_W:GATE_COLS + 3 * CONV_W + SSM_W]
        qb = proj[:, IN_COLS - XATTN_W:].astype(bf16)
        q_ref[...] = qb

        cb, cc, ch = cin[:, :CONV_W], cin[:, CONV_W:2 * CONV_W], cin[:, 2 * CONV_W:]
        z = cc * ch

        @pl.when(i == 0)
        def _():
            zs_ref[0:8, :] = jnp.zeros((8, CONV_W), f32)

        zs_ref[8:8 + tm, :] = z
        z1 = zs_ref[pl.ds(7, tm), :]
        z2 = zs_ref[pl.ds(6, tm), :]
        cw = cw_ref[...]
        cz = cw[0:1] * z2 + cw[1:2] * z1 + cw[2:3] * z
        zs_ref[0:8, :] = zs_ref[tm:tm + 8, :]
        ain = (cb * cz).astype(bf16)
        ain_ref[...] = ain
        aint_ref[...] = ain.T

        probs = [_attention_probs(qb, kv_ref, h) for h in range(HEADS)]
        outs = [_dot(probs[h].astype(bf16), kv_ref[:, XATTN_W + h * HEAD_DIM:XATTN_W + (h + 1) * HEAD_DIM])
                for h in range(HEADS)]
        ob = jnp.concatenate(outs, axis=1).astype(bf16)
        o_ref[...] = ob
        ot_ref[...] = ob.T

        pl.when(i == n - 1)(side_finish)

    row_cols = [(GATE_COLS, bf16), (3 * CONV_W, f32), (SSM_W, f32), (XATTN_W, bf16), (CONV_W, bf16), (XATTN_W, bf16)]
    t_rows = [D_MODEL, CONV_W, XATTN_W]
    outs = pl.pallas_call(
        body, name="in_proj", grid=(n,),
        in_specs=[_row_spec(tm, D_MODEL), _const_spec((IN_COLS, D_MODEL)), _const_spec((1, GATE_COLS)),
                  _const_spec((3, CONV_W)), _const_spec((MEM_LEN, 2 * XATTN_W))] + side_in_specs,
        out_specs=([_col_spec(t_rows[0], tm)] + [_row_spec(tm, c) for c, _ in row_cols]
                   + [_col_spec(t_rows[1], tm), _col_spec(t_rows[2], tm)] + side_in_specs),
        out_shape=([_sds((t_rows[0], s_len), bf16)] + [_sds((s_len, c), dt) for c, dt in row_cols]
                   + [_sds((t_rows[1], s_len), bf16), _sds((t_rows[2], s_len), bf16)] + side_shapes),
        scratch_shapes=[pltpu.VMEM((tm + 8, CONV_W), f32)] + side_sems,
        compiler_params=_cparams(("arbitrary",)),
    )(x, w_in_t, b_gate, conv_w, kv, *side_blocks)
    return outs[:9], outs[9:]


def _state_cols(chunk):
    half, off = divmod(chunk * LANE_CHUNK, HALF_STATE)
    lo = half * HALF_COLS + off
    return slice(lo, lo + LANE_CHUNK), slice(lo + HALF_STATE, lo + HALF_STATE + LANE_CHUNK)


def _half_cols(half):
    lo = half * HALF_COLS
    return slice(lo, lo + HALF_STATE), slice(lo + HALF_STATE, lo + HALF_COLS)


def _rows_to_segments(src_ref, stage_ref, dst_ref):
    nc = SSM_W // LANES
    for c in range(nc):
        stage_ref[c] = src_ref[:, c * LANES:(c + 1) * LANES]
    for c in range(nc):
        for k in range(SSM_SEG):
            dst_ref[k * SUBLANES:(k + 1) * SUBLANES, c * LANES:(c + 1) * LANES] = (
                stage_ref[c, pl.ds(k, SUBLANES, stride=SSM_SEG), :])


def _rows_from_segments(src_ref, stage_ref, dst_ref):
    nc = SSM_W // LANES
    for c in range(nc):
        for k in range(SSM_SEG):
            stage_ref[c, pl.ds(k, SUBLANES, stride=SSM_SEG), :] = (
                src_ref[k * SUBLANES:(k + 1) * SUBLANES, c * LANES:(c + 1) * LANES])
    for c in range(nc):
        dst_ref[:, c * LANES:(c + 1) * LANES] = stage_ref[c]


def _ssm_scan(s_ref, pw_ref, init_ref, reverse, unroll):
    for chunk in range(N_STATE // LANE_CHUNK):
        re, im = _state_cols(chunk)
        ar = jnp.broadcast_to(pw_ref[0:1, re], (SUBLANES, LANE_CHUNK))
        ai = jnp.broadcast_to(pw_ref[0:1, im], (SUBLANES, LANE_CHUNK))
        if reverse:
            ai = -ai

        def step(j, carry, re=re, im=im, ar=ar, ai=ai):
            sr, si = carry
            k = (SSM_SEG - 1 - j) if reverse else j
            r0 = pl.multiple_of(k * SUBLANES, SUBLANES)
            nr = ar * sr - ai * si + s_ref[pl.ds(r0, SUBLANES), re]
            ni = ar * si + ai * sr + s_ref[pl.ds(r0, SUBLANES), im]
            s_ref[pl.ds(r0, SUBLANES), re] = nr
            s_ref[pl.ds(r0, SUBLANES), im] = ni
            return nr, ni

        if init_ref is None:
            init = (jnp.zeros((SUBLANES, LANE_CHUNK), f32),) * 2
        else:
            init = (init_ref[:, re], init_ref[:, im])
        lax.fori_loop(0, SSM_SEG, step, init, unroll=unroll)


def _ssm_add_carry(s_ref, pw_ref, cm_ref, reverse):
    for chunk in range(N_STATE // LANE_CHUNK):
        re, im = _state_cols(chunk)
        cr, ci = cm_ref[:, re], cm_ref[:, im]
        for k in range(SSM_SEG):
            pk = (SSM_SEG - 1 - k) if reverse else k
            pr = pw_ref[pk:pk + 1, re]
            pi = pw_ref[pk:pk + 1, im]
            if reverse:
                pi = -pi
            rows = slice(k * SUBLANES, (k + 1) * SUBLANES)
            s_ref[rows, re] = s_ref[rows, re] + (pr * cr - pi * ci)
            s_ref[rows, im] = s_ref[rows, im] + (pr * ci + pi * cr)


def _ssm_carries(first_row, s_ref, pw_ref, carry_ref, cm_ref, reverse):
    order = range(SUBLANES - 1, -1, -1) if reverse else range(SUBLANES)
    for half in range(N_HALF):
        re, im = _half_cols(half)
        a_r, a_i = pw_ref[SSM_SEG - 1:SSM_SEG, re], pw_ref[SSM_SEG - 1:SSM_SEG, im]
        if reverse:
            a_i = -a_i
        cr, ci = carry_ref[0:1, re], carry_ref[0:1, im]
        for seg in order:
            cm_ref[seg:seg + 1, re] = cr
            cm_ref[seg:seg + 1, im] = ci
            er = s_ref[first_row + seg:first_row + seg + 1, re]
            ei = s_ref[first_row + seg:first_row + seg + 1, im]
            cr, ci = a_r * cr - a_i * ci + er, a_r * ci + a_i * cr + ei
        carry_ref[0:1, re] = cr
        carry_ref[0:1, im] = ci


def _ssm_fwd(u, b_half, c_half, pw, d_skip, side_blocks):
    s_len = u.shape[0]
    tb = SSM_BLOCK
    n = s_len // tb
    ns = len(side_blocks)
    side_in_specs, side_shapes, side_sems = _side_gather_specs(side_blocks)

    def body(*refs):
        u_ref, b_ref, c_ref, pw_ref, d_ref = refs[:5]
        side_ins = refs[5:5 + ns]
        y_ref, cm_ref = refs[5 + ns:7 + ns]
        side_outs = refs[7 + ns:7 + 2 * ns]
        s_ref, carry_ref, up_ref, yp_ref, stage_ref = refs[7 + 2 * ns:12 + 2 * ns]
        side_start, side_forward, side_finish = _side_gather_two_level(side_ins, side_outs, *refs[12 + 2 * ns:])
        i = pl.program_id(0)

        @pl.when(i == 0)
        def _():
            carry_ref[...] = jnp.zeros_like(carry_ref)
            side_start()

        pl.when(i == (3 * n) // 4)(side_forward)
        _rows_to_segments(u_ref, stage_ref, up_ref)
        u = up_ref[...]
        ub = u.astype(bf16)
        for half in range(N_HALF):
            s_ref[:, half * HALF_COLS:(half + 1) * HALF_COLS] = _dot(ub[:, half * HALF_W:(half + 1) * HALF_W], b_ref[half])
        _ssm_scan(s_ref, pw_ref, None, reverse=False, unroll=4)
        _ssm_carries(tb - SUBLANES, s_ref, pw_ref, carry_ref, cm_ref, reverse=False)
        _ssm_add_carry(s_ref, pw_ref, cm_ref, reverse=False)
        for half in range(N_HALF):
            cols = slice(half * HALF_W, (half + 1) * HALF_W)
            sb = s_ref[:, half * HALF_COLS:(half + 1) * HALF_COLS].astype(bf16)
            yp_ref[:, cols] = _dot(sb, c_ref[half]) + d_ref[:, cols] * u[:, cols]
        _rows_from_segments(yp_ref, stage_ref, y_ref)

        pl.when(i == n - 1)(side_finish)

    outs = pl.pallas_call(
        body, name="ssm_fwd", grid=(n,),
        in_specs=[_row_spec(tb, SSM_W), _const_spec((N_HALF, HALF_W, HALF_COLS)), _const_spec((N_HALF, HALF_COLS, HALF_W)),
                  _const_spec((SSM_SEG, 2 * N_STATE)), _const_spec((1, SSM_W))] + side_in_specs,
        out_specs=[_row_spec(tb, SSM_W), _row_spec(SUBLANES, 2 * N_STATE)] + side_in_specs,
        out_shape=[_sds((s_len, SSM_W), f32), _sds((n * SUBLANES, 2 * N_STATE), f32)] + side_shapes,
        scratch_shapes=[pltpu.VMEM((tb, 2 * N_STATE), f32), pltpu.VMEM((SUBLANES, 2 * N_STATE), f32),
                        pltpu.VMEM((tb, SSM_W), f32), pltpu.VMEM((tb, SSM_W), f32),
                        pltpu.VMEM((SSM_W // LANES, tb, LANES), f32)] + side_sems,
        compiler_params=_cparams(("arbitrary",)),
    )(u, b_half, c_half, pw, d_skip, *side_blocks)
    return outs[0], outs[1], outs[2:]


def _layer_norm_fwd(r, g, b):
    mu = jnp.mean(r, axis=-1, keepdims=True)
    var = jnp.mean(jnp.square(r - mu), axis=-1, keepdims=True)
    rstd = lax.rsqrt(var + LN_EPS)
    xhat = (r - mu) * rstd
    return xhat, rstd, xhat * g + b


def _layer_norm_bwd(dy, xhat, rstd, g):
    dxh = dy * g
    m1 = jnp.mean(dxh, axis=-1, keepdims=True)
    m2 = jnp.mean(dxh * xhat, axis=-1, keepdims=True)
    return rstd * (dxh - m1 - xhat * m2)


def _branch_outputs(ys_ref, ain_ref, o_ref, wglu_ref, wco_ref, wxo_ref):
    ysb = _gelu(ys_ref[...]).astype(bf16)
    glu = _dot(ysb, wglu_ref[...], NT)
    ga, sb = glu[:, :D_MODEL], jax.nn.sigmoid(glu[:, D_MODEL:])
    ya = _dot(ain_ref[...], wco_ref[...], NT)
    yc = _dot(o_ref[...], wxo_ref[...], NT)
    return ysb, ga, sb, ya, ga * sb, yc


def _mid_fwd(y_ssm, g, ain, ob, x, w_glu_t, w_co_t, w_xo_t, w_out, ln1_g, ln1_b):
    s_len = x.shape[0]
    tm = TOKEN_TILE
    n = s_len // tm

    def body(ys_ref, g_ref, ain_ref, o_ref, x_ref, wglu_ref, wco_ref, wxo_ref, wout_ref, lg_ref, lb_ref,
             ysbt_ref, mb_ref, xhat_ref, rstd_ref):
        ysb, _, _, ya, yb, yc = _branch_outputs(ys_ref, ain_ref, o_ref, wglu_ref, wco_ref, wxo_ref)
        ysbt_ref[...] = ysb.T
        gt = g_ref[...].astype(f32)
        merged = gt[:, :D_MODEL] * ya + gt[:, D_MODEL:2 * D_MODEL] * yb + gt[:, 2 * D_MODEL:] * yc
        mb = merged.astype(bf16)
        mb_ref[...] = mb
        r1 = ALPHA * x_ref[...] + _dot(mb, wout_ref[...])
        xhat, rstd, _ = _layer_norm_fwd(r1, lg_ref[...], lb_ref[...])
        xhat_ref[...] = xhat
        rstd_ref[...] = rstd

    row_cols = [(D_MODEL, bf16), (D_MODEL, f32), (1, f32)]
    return pl.pallas_call(
        body, name="mid_fwd", grid=(n,),
        in_specs=[_row_spec(tm, SSM_W), _row_spec(tm, GATE_COLS), _row_spec(tm, CONV_W), _row_spec(tm, XATTN_W),
                  _row_spec(tm, D_MODEL), _const_spec((2 * D_MODEL, SSM_W)), _const_spec((D_MODEL, CONV_W)),
                  _const_spec((D_MODEL, XATTN_W)), _const_spec((D_MODEL, D_MODEL)),
                  _const_spec((1, D_MODEL)), _const_spec((1, D_MODEL))],
        out_specs=[_col_spec(SSM_W, tm)] + [_row_spec(tm, c) for c, _ in row_cols],
        out_shape=[_sds((SSM_W, s_len), bf16)] + [_sds((s_len, c), dt) for c, dt in row_cols],
        compiler_params=_cparams(("parallel",)),
    )(y_ssm, g, ain, ob, x, w_glu_t, w_co_t, w_xo_t, w_out, ln1_g, ln1_b)


def _mlp_fwd_bwd(xhat1, tgt, ln1_g, ln1_b, w_up_t, b_up, w_down, b_down, ln2_g, ln2_b):
    s_len = xhat1.shape[0]
    tm = TOKEN_TILE
    n = s_len // tm
    fc = 1024
    nfc = D_FF // fc

    def body(xh_ref, t_ref, l1g_ref, l1b_ref, wup_ref, bup_ref, wdn_ref, bdn_ref, l2g_ref, l2b_ref,
             x1bt_ref, hdn_ref, dr2bt_ref, dpre_ref, dx1_ref,
             loss_ref, dl2g_ref, dl2b_ref, dbdn_ref, dbup_ref, rl_ref):
        i = pl.program_id(0)

        @pl.when(i == 0)
        def _():
            loss_ref[...] = jnp.zeros_like(loss_ref)
            dl2g_ref[...] = jnp.zeros_like(dl2g_ref)
            dl2b_ref[...] = jnp.zeros_like(dl2b_ref)
            dbdn_ref[...] = jnp.zeros_like(dbdn_ref)
            dbup_ref[...] = jnp.zeros_like(dbup_ref)

        x1 = xh_ref[...] * l1g_ref[...] + l1b_ref[...]
        x1b = x1.astype(bf16)
        x1bt_ref[...] = x1b.T
        chunks = [slice(c * fc, (c + 1) * fc) for c in range(nfc)]
        pres = [_dot(x1b, wup_ref[cols, :], NT) for cols in chunks]
        hbs = []
        for cols, pre in zip(chunks, pres):
            rl = jnp.maximum(pre + bup_ref[:, cols], 0.0)
            rl_ref[:, cols] = rl
            hb = (rl * rl).astype(bf16)
            hdn_ref[:, cols] = hb
            hbs.append(hb)
        acc = _dot(hbs[0], wdn_ref[chunks[0], :])
        for cols, hb in zip(chunks[1:], hbs[1:]):
            acc = acc + _dot(hb, wdn_ref[cols, :])
        r2 = ALPHA * x1 + acc + bdn_ref[...]
        xhat2, rstd2, y = _layer_norm_fwd(r2, l2g_ref[...], l2b_ref[...])
        err = y - t_ref[...]
        loss_ref[...] += jnp.sum(jnp.sum(err * err, axis=1, keepdims=True), axis=0, keepdims=True) * (0.5 / D_MODEL)
        dy = err * (1.0 / D_MODEL)
        dl2g_ref[...] += _colsum(dy * xhat2)
        dl2b_ref[...] += _colsum(dy)
        dr2 = _layer_norm_bwd(dy, xhat2, rstd2, l2g_ref[...])
        dbdn_ref[...] += _colsum(dr2)
        dr2b = dr2.astype(bf16)
        dr2bt_ref[...] = dr2b.T
        dhs = [_dot(dr2b, wdn_ref[cols, :], NT) for cols in chunks]
        dpbs = []
        for cols, dh in zip(chunks, dhs):
            dpre = dh * (2.0 * rl_ref[:, cols])
            dbup_ref[:, cols] += _colsum(dpre)
            dpb = dpre.astype(bf16)
            dpre_ref[:, cols] = dpb
            dpbs.append(dpb)
        dacc = _dot(dpbs[0], wup_ref[chunks[0], :])
        for cols, dpb in zip(chunks[1:], dpbs[1:]):
            dacc = dacc + _dot(dpb, wup_ref[cols, :])
        dx1_ref[...] = ALPHA * dr2 + dacc

    acc_shapes = [(1, LANES), (1, D_MODEL), (1, D_MODEL), (1, D_MODEL), (1, D_FF)]
    return pl.pallas_call(
        body, name="mlp_fwd_bwd", grid=(n,),
        in_specs=[_row_spec(tm, D_MODEL), _row_spec(tm, D_MODEL), _const_spec((1, D_MODEL)), _const_spec((1, D_MODEL)),
                  _const_spec((D_FF, D_MODEL)), _const_spec((1, D_FF)), _const_spec((D_FF, D_MODEL)),
                  _const_spec((1, D_MODEL)), _const_spec((1, D_MODEL)), _const_spec((1, D_MODEL))],
        out_specs=([_col_spec(D_MODEL, tm), _row_spec(tm, D_FF), _col_spec(D_MODEL, tm), _row_spec(tm, D_FF),
                    _row_spec(tm, D_MODEL)] + [_acc_spec(s) for s in acc_shapes]),
        out_shape=([_sds((D_MODEL, s_len), bf16), _sds((s_len, D_FF), bf16), _sds((D_MODEL, s_len), bf16),
                    _sds((s_len, D_FF), bf16), _sds((s_len, D_MODEL), f32)] + [_sds(s, f32) for s in acc_shapes]),
        scratch_shapes=[pltpu.VMEM((tm, D_FF), f32)],
        compiler_params=_cparams(("arbitrary",)),
    )(xhat1, tgt, ln1_g, ln1_b, w_up_t, b_up, w_down, b_down, ln2_g, ln2_b)


def _mid_bwd(dx1, xhat1, rstd1, g, ain, ob, y_ssm, ln1_g, w_out, w_glu_t, w_co_t, w_xo_t):
    s_len = dx1.shape[0]
    tm = TOKEN_TILE
    n = s_len // tm

    def body(dx1_ref, xh_ref, rs_ref, g_ref, ain_ref, o_ref, ys_ref, lg_ref, wout_ref, wglu_ref, wco_ref, wxo_ref,
             dxp_ref, dr1bt_ref, dgp_ref, dya_ref, dyc_ref, dglu_ref, dyssm_ref,
             dl1g_ref, dl1b_ref, dbg_ref):
        i = pl.program_id(0)

        @pl.when(i == 0)
        def _():
            dl1g_ref[...] = jnp.zeros_like(dl1g_ref)
            dl1b_ref[...] = jnp.zeros_like(dl1b_ref)
            dbg_ref[...] = jnp.zeros_like(dbg_ref)

        dx1 = dx1_ref[...]
        xhat = xh_ref[...]
        dl1g_ref[...] += _colsum(dx1 * xhat)
        dl1b_ref[...] += _colsum(dx1)
        dr1 = _layer_norm_bwd(dx1, xhat, rs_ref[...], lg_ref[...])
        dxp_ref[...] = ALPHA * dr1
        dr1b = dr1.astype(bf16)
        dr1bt_ref[...] = dr1b.T
        dm = _dot(dr1b, wout_ref[...], NT)

        _, ga, sb, ya, yb, yc = _branch_outputs(ys_ref, ain_ref, o_ref, wglu_ref, wco_ref, wxo_ref)
        gt = g_ref[...].astype(f32)
        branch = (ya, yb, yc)
        for j in range(3):
            cols = slice(j * D_MODEL, (j + 1) * D_MODEL)
            gj = gt[:, cols]
            dgp = dm * branch[j] * gj * (1.0 - gj)
            dbg_ref[:, cols] += _colsum(dgp)
            dgp_ref[:, cols] = dgp.astype(bf16)
        dya_ref[...] = (dm * gt[:, :D_MODEL]).astype(bf16)
        dyc_ref[...] = (dm * gt[:, 2 * D_MODEL:]).astype(bf16)
        dyb = dm * gt[:, D_MODEL:2 * D_MODEL]
        dga = (dyb * sb).astype(bf16)
        dgb = (dyb * ga * sb * (1.0 - sb)).astype(bf16)
        dglu_ref[:, :D_MODEL] = dga
        dglu_ref[:, D_MODEL:] = dgb
        dys = _dot(dga, wglu_ref[:D_MODEL, :]) + _dot(dgb, wglu_ref[D_MODEL:, :])
        dyssm_ref[...] = dys * _gelu_grad(ys_ref[...])

    row_cols = [(GATE_COLS, bf16), (D_MODEL, bf16), (D_MODEL, bf16), (2 * D_MODEL, bf16), (SSM_W, f32)]
    acc_shapes = [(1, D_MODEL), (1, D_MODEL), (1, GATE_COLS)]
    return pl.pallas_call(
        body, name="mid_bwd", grid=(n,),
        in_specs=[_row_spec(tm, D_MODEL), _row_spec(tm, D_MODEL), _row_spec(tm, 1), _row_spec(tm, GATE_COLS),
                  _row_spec(tm, CONV_W), _row_spec(tm, XATTN_W), _row_spec(tm, SSM_W),
                  _const_spec((1, D_MODEL)), _const_spec((D_MODEL, D_MODEL)), _const_spec((2 * D_MODEL, SSM_W)),
                  _const_spec((D_MODEL, CONV_W)), _const_spec((D_MODEL, XATTN_W))],
        out_specs=([_row_spec(tm, D_MODEL), _col_spec(D_MODEL, tm)] + [_row_spec(tm, c) for c, _ in row_cols]
                   + [_acc_spec(s) for s in acc_shapes]),
        out_shape=([_sds((s_len, D_MODEL), f32), _sds((D_MODEL, s_len), bf16)]
                   + [_sds((s_len, c), dt) for c, dt in row_cols] + [_sds(s, f32) for s in acc_shapes]),
        compiler_params=_cparams(("arbitrary",)),
    )(dx1, xhat1, rstd1, g, ain, ob, y_ssm, ln1_g, w_out, w_glu_t, w_co_t, w_xo_t)


def _ssm_bwd(u, dy, cm_all, b_half, c_half, pw, d_skip):
    s_len = u.shape[0]
    tb = SSM_BLOCK
    n = s_len // tb

    def body(u_ref, dy_ref, cm_ref, b_ref, c_ref, pw_ref, d_ref,
             du_ref, db_hbm, dc_hbm, da_ref, dd_ref,
             s_ref, g_ref, gcarry_ref, gcm_ref, db_ref, dc_ref, up_ref, dyp_ref, dup_ref, stage_ref):
        i = pl.program_id(0)

        @pl.when(i == 0)
        def _():
            gcarry_ref[...] = jnp.zeros_like(gcarry_ref)
            db_ref[...] = jnp.zeros_like(db_ref)
            dc_ref[...] = jnp.zeros_like(dc_ref)
            da_ref[...] = jnp.zeros_like(da_ref)
            dd_ref[...] = jnp.zeros_like(dd_ref)

        _rows_to_segments(u_ref, stage_ref, up_ref)
        _rows_to_segments(dy_ref, stage_ref, dyp_ref)
        u = up_ref[...]
        ub = u.astype(bf16)
        dy = dyp_ref[...]
        dyb = dy.astype(bf16)
        dd_ref[...] += _colsum(dy * u)

        for half in range(N_HALF):
            s_ref[:, half * HALF_COLS:(half + 1) * HALF_COLS] = _dot(ub[:, half * HALF_W:(half + 1) * HALF_W], b_ref[half])
        _ssm_scan(s_ref, pw_ref, cm_ref, reverse=False, unroll=True)

        for half in range(N_HALF):
            g_ref[:, half * HALF_COLS:(half + 1) * HALF_COLS] = _dot(dyb[:, half * HALF_W:(half + 1) * HALF_W], c_ref[half], NT)
        _ssm_scan(g_ref, pw_ref, None, reverse=True, unroll=True)
        _ssm_carries(0, g_ref, pw_ref, gcarry_ref, gcm_ref, reverse=True)
        _ssm_add_carry(g_ref, pw_ref, gcm_ref, reverse=True)

        for half in range(N_HALF):
            cols = slice(half * HALF_W, (half + 1) * HALF_W)
            scols = slice(half * HALF_COLS, (half + 1) * HALF_COLS)
            gb = g_ref[:, scols].astype(bf16)
            dup_ref[:, cols] = _dot(gb, b_ref[half], NT) + d_ref[:, cols] * dy[:, cols]
            db_ref[half] += _dot(ub[:, cols], gb, TN)
            dc_ref[half] += _dot(s_ref[:, scols].astype(bf16), dyb[:, cols], TN)
        _rows_from_segments(dup_ref, stage_ref, du_ref)

        for chunk in range(N_STATE // LANE_CHUNK):
            re, im = _state_cols(chunk)
            acc_r = da_ref[:, re]
            acc_i = da_ref[:, im]
            for k in range(SSM_SEG):
                rows = slice(k * SUBLANES, (k + 1) * SUBLANES)
                if k == 0:
                    pr, pi = cm_ref[:, re], cm_ref[:, im]
                else:
                    prev = slice((k - 1) * SUBLANES, k * SUBLANES)
                    pr, pi = s_ref[prev, re], s_ref[prev, im]
                gr, gi = g_ref[rows, re], g_ref[rows, im]
                acc_r = acc_r + (gr * pr + gi * pi)
                acc_i = acc_i + (gi * pr - gr * pi)
            da_ref[:, re] = acc_r
            da_ref[:, im] = acc_i

        @pl.when(i == n - 1)
        def _():
            pltpu.sync_copy(db_ref, db_hbm)
            pltpu.sync_copy(dc_ref, dc_hbm)

    rev = functools.partial(_row_spec, rev_n=n)
    any_spec = pl.BlockSpec(memory_space=pl.ANY)
    state_rows = pltpu.VMEM((tb, 2 * N_STATE), f32)
    seg_rows = pltpu.VMEM((SUBLANES, 2 * N_STATE), f32)
    tok_rows = pltpu.VMEM((tb, SSM_W), f32)
    return pl.pallas_call(
        body, name="ssm_bwd", grid=(n,),
        in_specs=[rev(tb, SSM_W), rev(tb, SSM_W), rev(SUBLANES, 2 * N_STATE),
                  _const_spec((N_HALF, HALF_W, HALF_COLS)), _const_spec((N_HALF, HALF_COLS, HALF_W)),
                  _const_spec((SSM_SEG, 2 * N_STATE)), _const_spec((1, SSM_W))],
        out_specs=[rev(tb, SSM_W), any_spec, any_spec, _acc_spec((SUBLANES, 2 * N_STATE)), _acc_spec((1, SSM_W))],
        out_shape=[_sds((s_len, SSM_W), f32), _sds((N_HALF, HALF_W, HALF_COLS), f32),
                   _sds((N_HALF, HALF_COLS, HALF_W), f32), _sds((SUBLANES, 2 * N_STATE), f32), _sds((1, SSM_W), f32)],
        scratch_shapes=[state_rows, state_rows, seg_rows, seg_rows,
                        pltpu.VMEM((N_HALF, HALF_W, HALF_COLS), f32), pltpu.VMEM((N_HALF, HALF_COLS, HALF_W), f32),
                        tok_rows, tok_rows, tok_rows, pltpu.VMEM((SSM_W // LANES, tb, LANES), f32)],
        compiler_params=_cparams(("arbitrary",)),
    )(u, dy, cm_all, b_half, c_half, pw, d_skip)


def _branch_bwd(dya, dyc, cin, q, kv, k_t, conv_w, w_co_t, w_xo_t, side_blocks):
    s_len = dya.shape[0]
    tm = TOKEN_TILE
    n = s_len // tm
    halo_blocks = tm // 8
    ns = len(side_blocks)
    conv_tile = _sds((8, CONV_W), f32)
    side_in_specs, side_shapes, side_sems = _side_gather_specs(list(side_blocks) + [conv_tile])

    def body(*refs):
        (dya_ref, dyc_ref, cin_ref, cprev_ref, q_ref, kv_ref, cw_ref, wco_ref, wxo_ref, kt_ref) = refs[:10]
        side_ins = refs[10:10 + ns]
        dconv_ref, dq_ref, dkv_ref = refs[10 + ns:13 + ns]
        side_outs = refs[13 + ns:14 + 2 * ns]
        zs_ref, dczs_ref, dcw_ref = refs[14 + 2 * ns:17 + 2 * ns]
        copies = _side_gather_copies(list(side_ins) + [dcw_ref], side_outs, *refs[17 + 2 * ns:])
        side, conv_side = copies[:ns * N_DEV], copies[ns * N_DEV:]
        i = pl.program_id(0)
        tile = n - 1 - i

        @pl.when(i == 0)
        def _():
            dcw_ref[...] = jnp.zeros_like(dcw_ref)
            dkv_ref[...] = jnp.zeros_like(dkv_ref)
            dczs_ref[tm:tm + 8, :] = jnp.zeros((8, CONV_W), f32)
            for cp in side:
                cp.start()

        cin = cin_ref[...]
        cb, cc, ch = cin[:, :CONV_W], cin[:, CONV_W:2 * CONV_W], cin[:, 2 * CONV_W:]
        z = cc * ch
        cprev = cprev_ref[...]
        zprev = cprev[:, CONV_W:2 * CONV_W] * cprev[:, 2 * CONV_W:]
        zs_ref[0:8, :] = jnp.where(tile == 0, 0.0, zprev)
        zs_ref[8:8 + tm, :] = z
        z1 = zs_ref[pl.ds(7, tm), :]
        z2 = zs_ref[pl.ds(6, tm), :]
        cw = cw_ref[...]
        cz = cw[0:1] * z2 + cw[1:2] * z1 + cw[2:3] * z

        dain = _dot(dya_ref[...], wco_ref[...])
        dcb = dain * cz
        dcz = dain * cb
        dczs_ref[0:tm, :] = dcz
        dcz1 = dczs_ref[pl.ds(1, tm), :]
        dcz2 = dczs_ref[pl.ds(2, tm), :]
        dz = cw[2:3] * dcz + cw[1:2] * dcz1 + cw[0:1] * dcz2
        dczs_ref[tm:tm + 8, :] = dczs_ref[0:8, :]
        dcw_ref[0:1, :] += _colsum(dcz * z2)
        dcw_ref[1:2, :] += _colsum(dcz * z1)
        dcw_ref[2:3, :] += _colsum(dcz * z)
        dconv_ref[:, :CONV_W] = dcb.astype(bf16)
        dconv_ref[:, CONV_W:2 * CONV_W] = (dz * ch).astype(bf16)
        dconv_ref[:, 2 * CONV_W:] = (dz * cc).astype(bf16)

        qb = q_ref[...]
        dob = _dot(dyc_ref[...], wxo_ref[...]).astype(bf16)
        kv = kv_ref[...]
        heads = range(HEADS)
        hcs = [slice(h * HEAD_DIM, (h + 1) * HEAD_DIM) for h in heads]
        vcs = [slice(XATTN_W + h * HEAD_DIM, XATTN_W + (h + 1) * HEAD_DIM) for h in heads]
        s_t = [_dot(kv[:, hcs[h]], qb[:, hcs[h]], NT) * (HEAD_DIM ** -0.5) for h in heads]
        dp_t = [_dot(kv[:, vcs[h]], dob[:, hcs[h]], NT) for h in heads]
        e_t = [jnp.exp(s_t[h] - jnp.max(s_t[h], axis=0, keepdims=True)) for h in heads]
        p_t = [e_t[h] / jnp.sum(e_t[h], axis=0, keepdims=True) for h in heads]
        dv = [_dot(p_t[h].astype(bf16), dob[:, hcs[h]]) for h in heads]
        ds_t = [(p_t[h] * (dp_t[h] - jnp.sum(dp_t[h] * p_t[h], axis=0, keepdims=True)) * (HEAD_DIM ** -0.5)).astype(bf16)
                for h in heads]
        dk = [_dot(ds_t[h], qb[:, hcs[h]]) for h in heads]
        dq_t = [_dot(kt_ref[hcs[h], :], ds_t[h]) for h in heads]
        dq_ref[...] = jnp.concatenate(dq_t, axis=0).T.astype(bf16)
        dkv_ref[...] += jnp.concatenate(dk + dv, axis=1)

        @pl.when(i == n - 1)
        def _():
            for cp in conv_side:
                cp.start()
            for cp in side + conv_side:
                cp.wait()

    rev = functools.partial(_row_spec, rev_n=n)
    prev_spec = pl.BlockSpec((8, 3 * CONV_W), lambda i: (jnp.maximum((n - 1 - i) * halo_blocks - 1, 0), 0))
    outs = pl.pallas_call(
        body, name="branch_bwd", grid=(n,),
        in_specs=[rev(tm, D_MODEL), rev(tm, D_MODEL), rev(tm, 3 * CONV_W), prev_spec, rev(tm, XATTN_W),
                  _const_spec((MEM_LEN, 2 * XATTN_W)), _const_spec((3, CONV_W)), _const_spec((D_MODEL, CONV_W)),
                  _const_spec((D_MODEL, XATTN_W)), _const_spec((XATTN_W, MEM_LEN))] + side_in_specs[:ns],
        out_specs=[rev(tm, 3 * CONV_W), rev(tm, XATTN_W), _acc_spec((MEM_LEN, 2 * XATTN_W))] + side_in_specs,
        out_shape=[_sds((s_len, 3 * CONV_W), bf16), _sds((s_len, XATTN_W), bf16),
                   _sds((MEM_LEN, 2 * XATTN_W), f32)] + side_shapes,
        scratch_shapes=[pltpu.VMEM((tm + 8, CONV_W), f32), pltpu.VMEM((tm + 8, CONV_W), f32),
                        pltpu.VMEM((8, CONV_W), f32)] + side_sems,
        compiler_params=_cparams(("arbitrary",)),
    )(dya, dyc, cin, cin, q, kv, conv_w, w_co_t, w_xo_t, k_t, *side_blocks)
    return outs[0], outs[1], outs[2], outs[3:]


def _in_proj_bwd(dgp, dconv, du, dq, dxp, w_in_t):
    s_len = dgp.shape[0]
    tm = 2 * TOKEN_TILE
    n = s_len // tm

    def body(dgp_ref, dconv_ref, du_ref, dq_ref, dxp_ref, win_ref, dx_ref, dproj_ref):
        dproj = jnp.concatenate([dgp_ref[...], dconv_ref[...], du_ref[...].astype(bf16), dq_ref[...]], axis=1)
        dproj_ref[...] = dproj
        dx_ref[...] = dxp_ref[...] + _dot(dproj, win_ref[...])

    return pl.pallas_call(
        body, name="in_proj_bwd", grid=(n,),
        in_specs=[_row_spec(tm, GATE_COLS), _row_spec(tm, 3 * CONV_W), _row_spec(tm, SSM_W), _row_spec(tm, XATTN_W),
                  _row_spec(tm, D_MODEL), _const_spec((IN_COLS, D_MODEL))],
        out_specs=[_row_spec(tm, D_MODEL), _row_spec(tm, IN_COLS)],
        out_shape=[_sds((s_len, D_MODEL), f32), _sds((s_len, IN_COLS), bf16)],
        compiler_params=_cparams(("parallel",)),
    )(dgp, dconv, du, dq, dxp, w_in_t)


N_CHIP = 4
CHIP_STEPS = [(1, 1), (1, 0), (0, 1), (0, 0)]


def _flip(v, d):
    return 1 - v if d else v


def _chip_order():
    x, y, _ = _mesh_place()
    return jnp.stack([2 * _flip(x, dx) + _flip(y, dy) for dx, dy in CHIP_STEPS]).astype(jnp.int32)


def _weight_grad_scatter(a_t, b, name, tm, tt):
    m, s_len = a_t.shape
    n_cols = b.shape[1]
    w = n_cols // N_DEV
    tn = 2 * w
    tm, tt = min(tm, m), min(tt, s_len)
    nm, nt = m // tm, s_len // tt
    assert m % tm == 0 and s_len % tt == 0

    def body(order_ref, a_ref, b_ref, recv_ref, acc_ref, send_ref, sib_ref, stash_ref,
             d2d_send, d2d_recv, ici_send, ici_recv, local_sem):
        del order_ref
        q, im, t = pl.program_id(0), pl.program_id(1), pl.program_id(2)
        x, y, c = _mesh_place()
        mesh_id = pl.DeviceIdType.MESH

        @pl.when(t == 0)
        def _():
            acc_ref[...] = jnp.zeros_like(acc_ref)

        acc_ref[...] += _dot(a_ref[...], b_ref[...])

        def to_sibling(qq, imm):
            rows = pl.ds(pl.multiple_of(imm * tm, tm), tm)
            return pltpu.make_async_remote_copy(
                src_ref=send_ref.at[qq, 0, rows, :], dst_ref=sib_ref.at[qq, rows, :],
                send_sem=d2d_send.at[qq], recv_sem=d2d_recv.at[qq, imm],
                device_id=(x, y, 1 - c), device_id_type=mesh_id)

        def finish_tile(qq, imm):
            rows = pl.ds(pl.multiple_of(imm * tm, tm), tm)
            to_sibling(qq, imm).wait_recv()
            both = stash_ref[...] + sib_ref[qq, rows, :].astype(f32)
            send_ref[qq, 1, rows, :] = both.astype(bf16)
            for step, (dx, dy) in enumerate(CHIP_STEPS):
                @pl.when(qq == step)
                def _(step=step, dx=dx, dy=dy):
                    src, dst = send_ref.at[step, 1, rows, :], recv_ref.at[step, rows, :]
                    if dx or dy:
                        pltpu.make_async_remote_copy(
                            src_ref=src, dst_ref=dst, send_sem=ici_send.at[step], recv_sem=ici_recv.at[step],
                            device_id=(_flip(x, dx), _flip(y, dy), c), device_id_type=mesh_id).start()
                    else:
                        pltpu.make_async_copy(src, dst, local_sem).start()

        @pl.when(t == nt - 1)
        def _():
            tile = q * nm + im

            @pl.when(tile > 0)
            def _():
                finish_tile((tile - 1) // nm, (tile - 1) % nm)

            rows = pl.ds(pl.multiple_of(im * tm, tm), tm)
            for core in (0, 1):
                @pl.when(c == core)
                def _(core=core):
                    other = 1 - core
                    send_ref[q, 0, rows, :] = acc_ref[:, other * w:(other + 1) * w].astype(bf16)
                    stash_ref[...] = acc_ref[:, core * w:(core + 1) * w]
            to_sibling(q, im).start()

            @pl.when(tile == N_CHIP * nm - 1)
            def _():
                finish_tile(q, im)

        @pl.when((q == N_CHIP - 1) & (im == nm - 1) & (t == nt - 1))
        def _():
            for step, (dx, dy) in enumerate(CHIP_STEPS):
                pltpu.make_async_remote_copy(
                    src_ref=send_ref.at[step, 0], dst_ref=sib_ref.at[step],
                    send_sem=d2d_send.at[step], recv_sem=d2d_recv.at[step, 0],
                    device_id=(x, y, 1 - c), device_id_type=mesh_id).wait_send()
                src, dst = send_ref.at[step, 1], recv_ref.at[step]
                if dx or dy:
                    pltpu.make_async_remote_copy(
                        src_ref=src, dst_ref=dst, send_sem=ici_send.at[step], recv_sem=ici_recv.at[step],
                        device_id=(_flip(x, dx), _flip(y, dy), c), device_id_type=mesh_id).wait()
                else:
                    pltpu.make_async_copy(src, dst, local_sem).wait()

    grid_spec = pltpu.PrefetchScalarGridSpec(
        num_scalar_prefetch=1, grid=(N_CHIP, nm, nt),
        in_specs=[pl.BlockSpec((tm, tt), lambda q, im, t, order: (im, t)),
                  pl.BlockSpec((tt, tn), lambda q, im, t, order: (t, order[q]))],
        out_specs=pl.BlockSpec(memory_space=pl.ANY),
        scratch_shapes=[pltpu.VMEM((tm, tn), f32), pltpu.VMEM((N_CHIP, 2, m, w), bf16), pltpu.VMEM((N_CHIP, m, w), bf16),
                        pltpu.VMEM((tm, w), f32),
                        pltpu.SemaphoreType.DMA((N_CHIP,)), pltpu.SemaphoreType.DMA((N_CHIP, nm)),
                        pltpu.SemaphoreType.DMA((N_CHIP - 1,)), pltpu.SemaphoreType.DMA((N_CHIP - 1,)),
                        pltpu.SemaphoreType.DMA])
    return pl.pallas_call(
        body, name=name, grid_spec=grid_spec,
        out_shape=_sds((N_CHIP, m, w), bf16),
        compiler_params=_cparams(("arbitrary", "arbitrary", "arbitrary")),
    )(_chip_order(), a_t, b)


def _adamw(w, g, m, v):
    m = ADAM_B1 * m + (1.0 - ADAM_B1) * g
    v = ADAM_B2 * v + (1.0 - ADAM_B2) * jnp.square(g)
    m_hat = m / (1.0 - ADAM_B1 ** ADAM_STEP)
    v_hat = v / (1.0 - ADAM_B2 ** ADAM_STEP)
    delta = -ADAM_LR * (m_hat / (jnp.sqrt(v_hat) + ADAM_EPS) + ADAM_WD * w)
    return delta, m, v


def _sum_parts(p_ref):
    g = p_ref[0].astype(f32)
    for j in range(1, p_ref.shape[0]):
        g = g + p_ref[j].astype(f32)
    return g


def _adamw_update(w, m, v, parts, name, transposed):
    rows, cols = w.shape
    n_parts = parts.shape[0]
    if transposed:
        tc = 256
        steps = cols // tc
        p_spec = pl.BlockSpec((n_parts, tc, rows), lambda i: (0, i, 0))
        spec = pl.BlockSpec((rows, tc), lambda i: (0, i))
    else:
        tr = next(t for t in (256, 128, 64, 32, 16, 8) if rows % t == 0)
        steps = rows // tr
        p_spec = pl.BlockSpec((n_parts, tr, cols), lambda i: (0, i, 0))
        spec = pl.BlockSpec((tr, cols), lambda i: (i, 0))

    def body(w_ref, p_ref, m_ref, v_ref, g_ref, d_ref, nm_ref, nv_ref):
        g = _sum_parts(p_ref)
        if transposed:
            g = g.T
        g_ref[...] = g
        d_ref[...], nm_ref[...], nv_ref[...] = _adamw(w_ref[...], g, m_ref[...], v_ref[...])

    return pl.pallas_call(
        body, name=name, grid=(steps,),
        in_specs=[spec, p_spec, spec, spec], out_specs=[spec] * 4,
        out_shape=[_sds((rows, cols), f32)] * 4,
        compiler_params=_cparams(("parallel",)),
    )(w, parts, m, v)


SMALL_GROUPS = [
    (["b_gate", "ln1_g", "ln1_b", "b_up", "b_down", "ln2_g", "ln2_b", "ssm_d"], 1),
    (["ssm_lam_re", "ssm_lam_im", "ssm_c_re", "ssm_c_im", "ssm_b_re", "ssm_b_im"], 0),
    (["conv_w"], 0),
    (["ssm_log_dt"], 0),
]


def _sum_small(group_parts):
    def body(*refs):
        n = len(refs) // 2
        for p_ref, o_ref in zip(refs[:n], refs[n:]):
            o_ref[...] = _sum_parts(p_ref)

    return pl.pallas_call(
        body, name="sum_small",
        out_shape=[_sds(p.shape[1:], f32) for p in group_parts],
        compiler_params=_cparams(),
    )(*group_parts)


def _adamw_small(ws, ms, vs, group_sums):
    names = [k for group, _ in SMALL_GROUPS for k in group]
    n = len(names)

    def body(*refs):
        w_refs, m_refs, v_refs = (dict(zip(names, refs[j * n:(j + 1) * n])) for j in range(3))
        p_refs = refs[3 * n:3 * n + len(SMALL_GROUPS)]
        out_refs = [dict(zip(names, refs[3 * n + len(SMALL_GROUPS) + j * n:][:n])) for j in range(4)]
        for (group, axis), p_ref in zip(SMALL_GROUPS, p_refs):
            total = p_ref[...]
            off = 0
            for k in group:
                size = SMALL[k][axis]
                g = total[:, off:off + size] if axis == 1 else total[off:off + size, :]
                off += size
                d, nm, nv = _adamw(w_refs[k][...], g, m_refs[k][...], v_refs[k][...])
                for j, val in enumerate((g, d, nm, nv)):
                    out_refs[j][k][...] = val

    res = pl.pallas_call(
        body, name="adamw_small",
        out_shape=[_sds(SMALL[k], f32) for _ in range(4) for k in names],
        compiler_params=_cparams(),
    )(*[ws[k] for k in names], *[ms[k] for k in names], *[vs[k] for k in names], *group_sums)
    return [dict(zip(names, res[j * n:(j + 1) * n])) for j in range(4)]


def _ssm_discretize(lam_re, lam_im, log_dt, b_re, b_im):
    dt = jnp.exp(log_dt)[:, None]
    mag = jnp.exp(lam_re * dt)
    abar_r = mag * jnp.cos(lam_im * dt)
    abar_i = mag * jnp.sin(lam_im * dt)
    den = lam_re * lam_re + lam_im * lam_im
    nr = abar_r - 1.0
    ni = abar_i
    kr = (nr * lam_re + ni * lam_im) / den
    ki = (ni * lam_re - nr * lam_im) / den
    bbar_r = kr[:, None, :] * b_re - ki[:, None, :] * b_im
    bbar_i = kr[:, None, :] * b_im + ki[:, None, :] * b_re
    return abar_r, abar_i, bbar_r, bbar_i


def _state_layout(re, im):
    parts = []
    for half in range(N_HALF):
        cols = slice(half * HALF_STATE, (half + 1) * HALF_STATE)
        parts += [re[..., cols], im[..., cols]]
    return jnp.concatenate(parts, axis=-1)


def _state_unlayout(a):
    re = jnp.concatenate([a[..., _half_cols(h)[0]] for h in range(N_HALF)], axis=-1)
    im = jnp.concatenate([a[..., _half_cols(h)[1]] for h in range(N_HALF)], axis=-1)
    return re, im


def _abar_powers(abar_r, abar_i):
    pr, pi = abar_r.reshape(1, N_STATE), abar_i.reshape(1, N_STATE)
    while pr.shape[0] < SSM_SEG:
        tr, ti = pr[-1:], pi[-1:]
        pr, pi = (jnp.concatenate([pr, pr * tr - pi * ti], axis=0), jnp.concatenate([pi, pr * ti + pi * tr], axis=0))
    return _state_layout(pr, pi)


HALF_GROUPS = SSM_GROUPS // N_HALF


def _half_block_diag(blocks):
    _, r, c = blocks.shape
    eye = jnp.eye(HALF_GROUPS, dtype=blocks.dtype)
    b4 = blocks.reshape(N_HALF, HALF_GROUPS, r, c)
    return jnp.einsum("ngrc,gk->ngrkc", b4, eye).reshape(N_HALF, HALF_GROUPS * r, HALF_GROUPS * c)


def _half_diag_blocks(mat, r, c):
    eye = jnp.eye(HALF_GROUPS, dtype=mat.dtype)
    m5 = mat.reshape(N_HALF, HALF_GROUPS, r, HALF_GROUPS, c)
    return jnp.einsum("ngrkc,gk->ngrc", m5, eye).reshape(SSM_GROUPS, r, c)


BIG = ["w_in", "w_conv_out", "w_glu", "w_kv", "w_xattn_out", "w_out", "w_up", "w_down"]
GATHER_TRANSPOSED = ["w_conv_out", "w_glu", "w_xattn_out", "w_up"]
PARTS_TRANSPOSED = ["w_in", "w_kv", "w_out", "w_down"]
SMALL = {"b_gate": (1, GATE_COLS), "conv_w": (3, CONV_W), "ssm_lam_re": (SSM_GROUPS, SSM_STATE),
         "ssm_lam_im": (SSM_GROUPS, SSM_STATE), "ssm_log_dt": (1, SSM_GROUPS),
         "ssm_b_re": (SSM_W, SSM_STATE), "ssm_b_im": (SSM_W, SSM_STATE),
         "ssm_c_re": (SSM_W, SSM_STATE), "ssm_c_im": (SSM_W, SSM_STATE), "ssm_d": (1, SSM_W),
         "ln1_g": (1, D_MODEL), "ln1_b": (1, D_MODEL), "b_up": (1, D_FF), "b_down": (1, D_MODEL),
         "ln2_g": (1, D_MODEL), "ln2_b": (1, D_MODEL)}
WEIGHTS = ["w_in", "b_gate", "conv_w", "w_conv_out", "ssm_lam_re", "ssm_lam_im", "ssm_log_dt", "ssm_b_re", "ssm_b_im",
           "ssm_c_re", "ssm_c_im", "ssm_d", "w_glu", "w_kv", "w_xattn_out", "w_out", "ln1_g", "ln1_b", "w_up", "b_up",
           "w_down", "b_down", "ln2_g", "ln2_b"]


def _local_step(x, mem, tgt, full, late, small):
    lam_re, lam_im, log_dt = small["ssm_lam_re"], small["ssm_lam_im"], small["ssm_log_dt"].reshape(SSM_GROUPS)
    c_shape = (SSM_GROUPS, SSM_GROUP, SSM_STATE)
    disc, disc_vjp = jax.vjp(_ssm_discretize, lam_re, lam_im, log_dt,
                             small["ssm_b_re"].reshape(c_shape), small["ssm_b_im"].reshape(c_shape))
    abar_r, abar_i, bbar_r, bbar_i = disc
    pw = _abar_powers(abar_r, abar_i)
    c_re, c_im = small["ssm_c_re"].reshape(c_shape), small["ssm_c_im"].reshape(c_shape)
    b_half = jnp.concatenate([_half_block_diag(bbar_r), _half_block_diag(bbar_i)], axis=2).astype(bf16)
    c_half = jnp.concatenate([_half_block_diag(c_re.transpose(0, 2, 1)), -_half_block_diag(c_im.transpose(0, 2, 1))],
                             axis=1).astype(bf16)

    s_len = x.shape[0]
    stack = lambda a: a.reshape(-1, a.shape[-1])
    kv, k_t, memb = _kv_proj(mem, full["w_kv"])
    (xbt, g, cin, u, q, ain, ob, aint, obt), side = _in_proj(
        x, full["w_in"], small["b_gate"], small["conv_w"], kv,
        [late[k] for k in ("w_glu", "w_conv_out", "w_xattn_out", "w_out", "w_up")])
    w_glu_t, w_co_t, w_xo_t, w_out, w_up_t = (stack(a) for a in side)
    y_ssm, cm_all, side = _ssm_fwd(u, b_half, c_half, pw, small["ssm_d"], [late["w_down"]])
    w_down = stack(side[0])
    ysbt, mb, xhat1, rstd1 = _mid_fwd(y_ssm, g, ain, ob, x, w_glu_t, w_co_t, w_xo_t, w_out,
                                      small["ln1_g"], small["ln1_b"])
    (x1bt, hdn, dr2bt, dpre, dx1, loss, dl2g, dl2b, dbdn, dbup) = _mlp_fwd_bwd(
        xhat1, tgt, small["ln1_g"], small["ln1_b"], w_up_t, small["b_up"], w_down,
        small["b_down"], small["ln2_g"], small["ln2_b"])
    recv = {}
    recv["w_down"] = _weight_grad_scatter(dr2bt, hdn, "dw_down", tm=512, tt=2048)
    recv["w_up"] = _weight_grad_scatter(x1bt, dpre, "dw_up", tm=512, tt=2048)
    (dxp, dr1bt, dgp, dya, dyc, dglu, dyssm, dl1g, dl1b, dbg) = _mid_bwd(
        dx1, xhat1, rstd1, g, ain, ob, y_ssm, small["ln1_g"], w_out, w_glu_t, w_co_t, w_xo_t)
    recv["w_out"] = _weight_grad_scatter(dr1bt, mb, "dw_out", tm=512, tt=s_len)
    recv["w_glu"] = _weight_grad_scatter(ysbt, dglu, "dw_glu", tm=512, tt=s_len)
    du, db_half, dc_half, da8, dd = _ssm_bwd(u, dyssm, cm_all, b_half, c_half, pw, small["ssm_d"])
    dabar_r, dabar_i = _state_unlayout(jnp.sum(da8, axis=0))
    dbbar_r = _half_diag_blocks(db_half[:, :, :HALF_STATE], SSM_GROUP, SSM_STATE)
    dbbar_i = _half_diag_blocks(db_half[:, :, HALF_STATE:], SSM_GROUP, SSM_STATE)
    g_shape = (SSM_GROUPS, SSM_STATE)
    dlam_re, dlam_im, dlog_dt, db_re, db_im = disc_vjp(
        (dabar_r.reshape(g_shape), dabar_i.reshape(g_shape), dbbar_r, dbbar_i))
    dc_re = _half_diag_blocks(dc_half[:, :HALF_STATE, :], SSM_STATE, SSM_GROUP).transpose(0, 2, 1)
    dc_im = -_half_diag_blocks(dc_half[:, HALF_STATE:, :], SSM_STATE, SSM_GROUP).transpose(0, 2, 1)

    small_grads = {
        "b_gate": dbg, "ssm_lam_re": dlam_re, "ssm_lam_im": dlam_im, "ssm_log_dt": dlog_dt,
        "ssm_b_re": db_re, "ssm_b_im": db_im, "ssm_c_re": dc_re, "ssm_c_im": dc_im, "ssm_d": dd,
        "ln1_g": dl1g, "ln1_b": dl1b, "b_up": dbup, "b_down": dbdn, "ln2_g": dl2g, "ln2_b": dl2b,
    }
    small_grads = {k: a.reshape(SMALL[k]) for k, a in small_grads.items()}
    groups = [(group, axis) for group, axis in SMALL_GROUPS if group != ["conv_w"]]
    stacks = [jnp.concatenate([small_grads[k] for k in group], axis=axis) if len(group) > 1 else small_grads[group[0]]
              for group, axis in groups]
    n_rowvec = stacks[0].shape[1]
    stacks[0] = jnp.concatenate([stacks[0], loss], axis=1)
    dense = lambda a: a.reshape(-1, LANES) if a.size % LANES == 0 else a
    dconv, dq, dkv, group_parts = _branch_bwd(dya, dyc, cin, q, kv, k_t, small["conv_w"], w_co_t, w_xo_t,
                                              [dense(a) for a in stacks])
    recv["w_conv_out"] = _weight_grad_scatter(aint, dya, "dw_conv_out", tm=512, tt=s_len)
    recv["w_xattn_out"] = _weight_grad_scatter(obt, dyc, "dw_xattn_out", tm=512, tt=s_len)
    recv["w_kv"] = _weight_grad_scatter(dkv.T.astype(bf16), memb, "dw_kv", tm=D_MODEL, tt=MEM_LEN)
    dx, dproj = _in_proj_bwd(dgp, dconv, du, dq, dxp, full["w_in"])
    recv["w_in"] = _weight_grad_scatter(xbt, dproj, "dw_in", tm=512, tt=2048)
    sums = _sum_small(group_parts)
    group_sums = dict(zip([tuple(group) for group, _ in groups], [s.reshape(a.shape) for s, a in zip(sums, stacks)]))
    group_sums[("conv_w",)] = sums[-1][0:3]
    first = tuple(groups[0][0])
    loss_all = group_sums[first][0, n_rowvec]
    group_sums[first] = group_sums[first][:, :n_rowvec]
    return loss_all, dx, recv, [group_sums[tuple(group)] for group, _ in SMALL_GROUPS]


def kernel(x, mem, w_in, b_gate, conv_w, w_conv_out, ssm_lam_re, ssm_lam_im, ssm_log_dt, ssm_b_re, ssm_b_im, ssm_c_re, ssm_c_im, ssm_d, w_glu, w_kv, w_xattn_out, w_out, ln1_g, ln1_b, w_up, b_up, w_down, b_down, ln2_g, ln2_b, loss_target, m_w_in, m_b_gate, m_conv_w, m_w_conv_out, m_ssm_lam_re, m_ssm_lam_im, m_ssm_log_dt, m_ssm_b_re, m_ssm_b_im, m_ssm_c_re, m_ssm_c_im, m_ssm_d, m_w_glu, m_w_kv, m_w_xattn_out, m_w_out, m_ln1_g, m_ln1_b, m_w_up, m_b_up, m_w_down, m_b_down, m_ln2_g, m_ln2_b, v_w_in, v_b_gate, v_conv_w, v_w_conv_out, v_ssm_lam_re, v_ssm_lam_im, v_ssm_log_dt, v_ssm_b_re, v_ssm_b_im, v_ssm_c_re, v_ssm_c_im, v_ssm_d, v_w_glu, v_w_kv, v_w_xattn_out, v_w_out, v_ln1_g, v_ln1_b, v_w_up, v_b_up, v_w_down, v_b_down, v_ln2_g, v_ln2_b):
    w = dict(w_in=w_in, b_gate=b_gate, conv_w=conv_w, w_conv_out=w_conv_out, ssm_lam_re=ssm_lam_re,
             ssm_lam_im=ssm_lam_im, ssm_log_dt=ssm_log_dt, ssm_b_re=ssm_b_re, ssm_b_im=ssm_b_im, ssm_c_re=ssm_c_re,
             ssm_c_im=ssm_c_im, ssm_d=ssm_d, w_glu=w_glu, w_kv=w_kv, w_xattn_out=w_xattn_out, w_out=w_out,
             ln1_g=ln1_g, ln1_b=ln1_b, w_up=w_up, b_up=b_up, w_down=w_down, b_down=b_down, ln2_g=ln2_g, ln2_b=ln2_b)
    m = dict(w_in=m_w_in, b_gate=m_b_gate, conv_w=m_conv_w, w_conv_out=m_w_conv_out, ssm_lam_re=m_ssm_lam_re,
             ssm_lam_im=m_ssm_lam_im, ssm_log_dt=m_ssm_log_dt, ssm_b_re=m_ssm_b_re, ssm_b_im=m_ssm_b_im,
             ssm_c_re=m_ssm_c_re, ssm_c_im=m_ssm_c_im, ssm_d=m_ssm_d, w_glu=m_w_glu, w_kv=m_w_kv,
             w_xattn_out=m_w_xattn_out, w_out=m_w_out, ln1_g=m_ln1_g, ln1_b=m_ln1_b, w_up=m_w_up, b_up=m_b_up,
             w_down=m_w_down, b_down=m_b_down, ln2_g=m_ln2_g, ln2_b=m_ln2_b)
    v = dict(w_in=v_w_in, b_gate=v_b_gate, conv_w=v_conv_w, w_conv_out=v_w_conv_out, ssm_lam_re=v_ssm_lam_re,
             ssm_lam_im=v_ssm_lam_im, ssm_log_dt=v_ssm_log_dt, ssm_b_re=v_ssm_b_re, ssm_b_im=v_ssm_b_im,
             ssm_c_re=v_ssm_c_re, ssm_c_im=v_ssm_c_im, ssm_d=v_ssm_d, w_glu=v_w_glu, w_kv=v_w_kv,
             w_xattn_out=v_w_xattn_out, w_out=v_w_out, ln1_g=v_ln1_g, ln1_b=v_ln1_b, w_up=v_w_up, b_up=v_b_up,
             w_down=v_w_down, b_down=v_b_down, ln2_g=v_ln2_g, ln2_b=v_ln2_b)
    out_shapes = {k: a.shape for k, a in w.items()}
    swapped = ("w_in", "ssm_b_re", "ssm_b_im")

    def shard2d(k, a):
        if k in swapped:
            a = jnp.swapaxes(a, -1, -2)
        if k in SMALL:
            return a.reshape((3, CONV_W // N_DEV) if k == "conv_w" else SMALL[k])
        return a[0]

    def result(k, a):
        if k in swapped:
            shape = out_shapes[k]
            return jnp.swapaxes(a.reshape(shape[:-2] + (shape[-1], shape[-2])), -1, -2)
        return a.reshape(out_shapes[k])

    w, m, v = ({k: shard2d(k, a) for k, a in d.items()} for d in (w, m, v))

    shards = {k: w[k].T.astype(bf16) if k in GATHER_TRANSPOSED else w[k].astype(bf16) for k in BIG}
    conv_pad = jnp.pad(w["conv_w"], ((0, 5), (0, LANES - CONV_W // N_DEV)))
    early = ["w_in", "w_kv"]
    gathered = _all_gather([shards[k] for k in early] + [conv_pad], "gather_weights")
    full = {k: a.reshape(-1, a.shape[-1]) for k, a in zip(early, gathered[:-1])}
    late = {k: shards[k] for k in BIG if k not in early}
    conv_full = gathered[-1][:, :3, :CONV_W // N_DEV].transpose(1, 0, 2).reshape(3, CONV_W)
    small = {k: (conv_full if k == "conv_w" else w[k]) for k in SMALL}

    loss, dx, recv, group_sums = _local_step(x[0], mem[0], loss_target[0], full, late, small)

    grads, deltas, new_m, new_v = {}, {}, {}, {}
    for k in BIG:
        res = _adamw_update(w[k], m[k], v[k], recv[k], "adamw_" + k, transposed=k in PARTS_TRANSPOSED)
        grads[k], deltas[k], new_m[k], new_v[k] = res

    widen = lambda k, a: jnp.tile(a, (1, N_DEV)) if k == "conv_w" else a
    res = _adamw_small(small, {k: widen(k, m[k]) for k in SMALL}, {k: widen(k, v[k]) for k in SMALL}, group_sums)
    dev = _slot(_mesh_place())
    for d, small_res in zip((grads, deltas, new_m, new_v), res):
        for k, a in small_res.items():
            if k == "conv_w":
                a = lax.dynamic_slice_in_dim(a, dev * (CONV_W // N_DEV), CONV_W // N_DEV, axis=1)
            d[k] = a

    outs = [loss, dx[None]]
    for d in (grads, deltas, new_m, new_v):
        outs += [result(k, d[k]) for k in WEIGHTS]
    return tuple(outs)
```

```python
import functools
import math

import jax
import jax.numpy as jnp
from jax import lax
from jax.experimental import pallas as pl
from jax.experimental.pallas import tpu as pltpu

f32 = jnp.float32
bf16 = jnp.bfloat16

D_MODEL = 1024
MEM_LEN = 256
GATE_COLS = 3 * D_MODEL
CONV_W = 512
SSM_W = 512
XATTN_W = 512
HEADS = 4
HEAD_DIM = 128
D_FF = 4096
IN_COLS = GATE_COLS + 3 * CONV_W + SSM_W + XATTN_W
SSM_GROUPS = 32
SSM_GROUP = 16
SSM_STATE = 64
N_STATE = SSM_GROUPS * SSM_STATE
ALPHA = 2.0 ** 0.25
LN_EPS = 1e-5
N_DEV = 8

ADAM_LR = 0.001
ADAM_B1 = 0.9
ADAM_B2 = 0.999
ADAM_EPS = 1e-08
ADAM_WD = 0.01
ADAM_STEP = 10

VMEM_LIMIT_V7X = 56 * 2 ** 20
SUBLANES = 8
LANES = 128

TOKEN_TILE = 256
SSM_BLOCK = 512
SSM_SEG = SSM_BLOCK // SUBLANES
LANE_CHUNK = 256
N_HALF = 2
HALF_W = SSM_W // N_HALF
HALF_STATE = N_STATE // N_HALF
HALF_COLS = 2 * HALF_STATE

NT = (((1,), (1,)), ((), ()))
TN = (((0,), (0,)), ((), ()))
NN = (((1,), (0,)), ((), ()))


def _dot(a, b, dims=NN):
    return lax.dot_general(a, b, dims, preferred_element_type=f32)


def _cparams(sem=None):
    return pltpu.CompilerParams(dimension_semantics=sem, vmem_limit_bytes=VMEM_LIMIT_V7X)


def _row_spec(tm, cols, rev_n=None):
    if rev_n is None:
        return pl.BlockSpec((tm, cols), lambda i: (i, 0))
    return pl.BlockSpec((tm, cols), lambda i: (rev_n - 1 - i, 0))


def _col_spec(rows, tm):
    return pl.BlockSpec((rows, tm), lambda i: (0, i))


def _const_spec(shape):
    nd = len(shape)
    return pl.BlockSpec(shape, lambda *_: (0,) * nd, pipeline_mode=pl.Buffered(1))


def _acc_spec(shape):
    nd = len(shape)
    return pl.BlockSpec(shape, lambda *_: (0,) * nd)


def _sds(shape, dtype):
    return jax.ShapeDtypeStruct(shape, dtype)


def _gelu(x):
    c = math.sqrt(2.0 / math.pi)
    return 0.5 * x * (1.0 + jnp.tanh(c * (x + 0.044715 * x * x * x)))


def _gelu_grad(x):
    c = math.sqrt(2.0 / math.pi)
    t = jnp.tanh(c * (x + 0.044715 * x * x * x))
    return 0.5 * (1.0 + t) + 0.5 * x * (1.0 - t * t) * c * (1.0 + 3.0 * 0.044715 * x * x)


def _colsum(a):
    return jnp.sum(a, axis=0, keepdims=True)


def _mesh_place():
    return lax.axis_index("x"), lax.axis_index("y"), lax.axis_index("c")


def _slot(p):
    return 4 * p[0] + 2 * p[1] + p[2]


def _other_devices(me):
    x, y, c = me
    flip = lambda v, d: 1 - v if d else v
    return [(flip(x, dx), flip(y, dy), flip(c, dc)) for dx in (0, 1) for dy in (0, 1) for dc in (0, 1)][1:]


def _all_gather(blocks, name):
    n = len(blocks)

    def body(*refs):
        ins, outs = refs[:n], refs[n:2 * n]
        send_sems, recv_sems, local_sems = refs[2 * n:]
        x, y, c = _mesh_place()
        me, sibling = (x, y, c), (x, y, 1 - c)
        chips = [(1 - x, y), (x, 1 - y), (1 - x, 1 - y)]

        def copy(a, k, block, to, src=None):
            rows = outs[a].at[_slot(block)]
            return pltpu.make_async_remote_copy(
                src_ref=rows if src is None else src, dst_ref=rows,
                send_sem=send_sems.at[a, k], recv_sem=recv_sems.at[a, k],
                device_id=to, device_id_type=pl.DeviceIdType.MESH)

        mine = [pltpu.make_async_copy(ins[a], outs[a].at[_slot(me)], local_sems.at[a]) for a in range(n)]
        for cp in mine:
            cp.start()
        first = []
        for a in range(n):
            first.append(copy(a, 0, me, sibling, src=ins[a]))
            first += [copy(a, 1 + j, me, (*chip, c), src=ins[a]) for j, chip in enumerate(chips)]
        for cp in first:
            cp.start()
        passed = []
        for a in range(n):
            for j, chip in enumerate(chips):
                copy(a, 1 + j, (*chip, c), me).wait_recv()
                fwd = copy(a, 4 + j, (*chip, c), sibling)
                fwd.start()
                passed.append(fwd)
        for a in range(n):
            copy(a, 0, sibling, me).wait_recv()
            for j, chip in enumerate(chips):
                copy(a, 4 + j, (*chip, 1 - c), me).wait_recv()
        for cp in first + passed:
            cp.wait_send()
        for cp in mine:
            cp.wait()

    any_spec = pl.BlockSpec(memory_space=pl.ANY)
    return pl.pallas_call(
        body, name=name,
        out_shape=[_sds((N_DEV,) + b.shape, b.dtype) for b in blocks],
        in_specs=[any_spec] * n, out_specs=[any_spec] * n,
        scratch_shapes=[pltpu.SemaphoreType.DMA((n, 7)), pltpu.SemaphoreType.DMA((n, 7)),
                        pltpu.SemaphoreType.DMA((n,))],
    )(*blocks)


def _side_gather_copies(ins, outs, send_sems, recv_sems, local_sems):
    me = _mesh_place()
    copies = []
    for a, (src, dst) in enumerate(zip(ins, outs)):
        copies.append(pltpu.make_async_copy(src, dst.at[_slot(me)], local_sems.at[a]))
        for k, peer in enumerate(_other_devices(me)):
            copies.append(pltpu.make_async_remote_copy(
                src_ref=src, dst_ref=dst.at[_slot(me)], send_sem=send_sems.at[a, k], recv_sem=recv_sems.at[a, k],
                device_id=peer, device_id_type=pl.DeviceIdType.MESH))
    return copies


def _side_gather_two_level(ins, outs, send_sems, recv_sems, local_sems):
    x, y, c = _mesh_place()
    me, sibling = (x, y, c), (x, y, 1 - c)
    chips = [(1 - x, y), (x, 1 - y), (1 - x, 1 - y)]

    def copy(a, k, block, to, src=None):
        rows = outs[a].at[_slot(block)]
        return pltpu.make_async_remote_copy(
            src_ref=rows if src is None else src, dst_ref=rows, send_sem=send_sems.at[a, k], recv_sem=recv_sems.at[a, k],
            device_id=to, device_id_type=pl.DeviceIdType.MESH)

    n = len(ins)
    mine = [pltpu.make_async_copy(ins[a], outs[a].at[_slot(me)], local_sems.at[a]) for a in range(n)]
    first = [copy(a, 0, me, sibling, src=ins[a]) for a in range(n)]
    first += [copy(a, 1 + j, me, (*chip, c), src=ins[a]) for a in range(n) for j, chip in enumerate(chips)]
    passed = [copy(a, 4 + j, (*chip, c), sibling) for a in range(n) for j, chip in enumerate(chips)]

    def start():
        for cp in mine + first:
            cp.start()

    def forward():
        for a in range(n):
            for j, chip in enumerate(chips):
                copy(a, 1 + j, (*chip, c), me).wait_recv()
        for cp in passed:
            cp.start()

    def finish():
        for a in range(n):
            copy(a, 0, sibling, me).wait_recv()
            for j, chip in enumerate(chips):
                copy(a, 4 + j, (*chip, 1 - c), me).wait_recv()
        for cp in first + passed:
            cp.wait_send()
        for cp in mine:
            cp.wait()

    return start, forward, finish


def _side_gather_specs(blocks):
    n = len(blocks)
    any_spec = pl.BlockSpec(memory_space=pl.ANY)
    return ([any_spec] * n, [_sds((N_DEV,) + b.shape, b.dtype) for b in blocks],
            [pltpu.SemaphoreType.DMA((n, N_DEV - 1)), pltpu.SemaphoreType.DMA((n, N_DEV - 1)),
             pltpu.SemaphoreType.DMA((n,))])


def _kv_proj(mem, w_kv):
    def body(mem_ref, w_ref, kv_ref, kt_ref, memb_ref):
        mb = mem_ref[...].astype(bf16)
        memb_ref[...] = mb
        kv = _dot(mb, w_ref[...]).astype(bf16)
        kv_ref[...] = kv
        kt_ref[...] = kv[:, :XATTN_W].T

    return pl.pallas_call(
        body, name="kv_proj",
        out_shape=[_sds((MEM_LEN, 2 * XATTN_W), bf16), _sds((XATTN_W, MEM_LEN), bf16), _sds((MEM_LEN, D_MODEL), bf16)],
        compiler_params=_cparams(),
    )(mem, w_kv)


def _attention_probs(qb, kv_ref, h):
    kh = kv_ref[:, h * HEAD_DIM:(h + 1) * HEAD_DIM]
    s = _dot(qb[:, h * HEAD_DIM:(h + 1) * HEAD_DIM], kh, NT) * (HEAD_DIM ** -0.5)
    e = jnp.exp(s - jnp.max(s, axis=-1, keepdims=True))
    return e / jnp.sum(e, axis=-1, keepdims=True)


def _in_proj(x, w_in_t, b_gate, conv_w, kv, side_blocks):
    s_len = x.shape[0]
    tm = 2 * TOKEN_TILE
    n = s_len // tm
    ns = len(side_blocks)
    side_in_specs, side_shapes, side_sems = _side_gather_specs(side_blocks)

    def body(*refs):
        (x_ref, win_ref, bg_ref, cw_ref, kv_ref) = refs[:5]
        side_ins = refs[5:5 + ns]
        (xbt_ref, g_ref, cin_ref, u_ref, q_ref, ain_ref, o_ref, aint_ref, ot_ref) = refs[5 + ns:14 + ns]
        side_outs = refs[14 + ns:14 + 2 * ns]
        zs_ref = refs[14 + 2 * ns]
        side_start, side_forward, side_finish = _side_gather_two_level(side_ins, side_outs, *refs[15 + 2 * ns:])
        i = pl.program_id(0)
        pl.when(i == 0)(side_start)
        pl.when(i == (3 * n) // 4)(side_forward)

        xb = x_ref[...].astype(bf16)
        xbt_ref[...] = xb.T
        proj = _dot(xb, win_ref[...], NT)
        g_ref[...] = jax.nn.sigmoid(proj[:, :GATE_COLS] + bg_ref[...]).astype(bf16)
        cin = proj[:, GATE_COLS:GATE_COLS + 3 * CONV_W]
        cin_ref[...] = cin
        u_ref[...] = proj[:, GATE_COLS + 3 * CONV_W:GATE_COLS + 3 * CONV_W + SSM_W]
        qb = proj[:, IN_COLS - XATTN_W:].astype(bf16)
        q_ref[...] = qb

        cb, cc, ch = cin[:, :CONV_W], cin[:, CONV_W:2 * CONV_W], cin[:, 2 * CONV_W:]
        z = cc * ch

        @pl.when(i == 0)
        def _():
            zs_ref[0:8, :] = jnp.zeros((8, CONV_W), f32)

        zs_ref[8:8 + tm, :] = z
        z1 = zs_ref[pl.ds(7, tm), :]
        z2 = zs_ref[pl.ds(6, tm), :]
        cw = cw_ref[...]
        cz = cw[0:1] * z2 + cw[1:2] * z1 + cw[2:3] * z
        zs_ref[0:8, :] = zs_ref[tm:tm + 8, :]
        ain = (cb * cz).astype(bf16)
        ain_ref[...] = ain
        aint_ref[...] = ain.T

        probs = [_attention_probs(qb, kv_ref, h) for h in range(HEADS)]
        outs = [_dot(probs[h].astype(bf16), kv_ref[:, XATTN_W + h * HEAD_DIM:XATTN_W + (h + 1) * HEAD_DIM])
                for h in range(HEADS)]
        ob = jnp.concatenate(outs, axis=1).astype(bf16)
        o_ref[...] = ob
        ot_ref[...] = ob.T

        pl.when(i == n - 1)(side_finish)

    row_cols = [(GATE_COLS, bf16), (3 * CONV_W, f32), (SSM_W, f32), (XATTN_W, bf16), (CONV_W, bf16), (XATTN_W, bf16)]
    t_rows = [D_MODEL, CONV_W, XATTN_W]
    outs = pl.pallas_call(
        body, name="in_proj", grid=(n,),
        in_specs=[_row_spec(tm, D_MODEL), _const_spec((IN_COLS, D_MODEL)), _const_spec((1, GATE_COLS)),
                  _const_spec((3, CONV_W)), _const_spec((MEM_LEN, 2 * XATTN_W))] + side_in_specs,
        out_specs=([_col_spec(t_rows[0], tm)] + [_row_spec(tm, c) for c, _ in row_cols]
                   + [_col_spec(t_rows[1], tm), _col_spec(t_rows[2], tm)] + side_in_specs),
        out_shape=([_sds((t_rows[0], s_len), bf16)] + [_sds((s_len, c), dt) for c, dt in row_cols]
                   + [_sds((t_rows[1], s_len), bf16), _sds((t_rows[2], s_len), bf16)] + side_shapes),
        scratch_shapes=[pltpu.VMEM((tm + 8, CONV_W), f32)] + side_sems,
        compiler_params=_cparams(("arbitrary",)),
    )(x, w_in_t, b_gate, conv_w, kv, *side_blocks)
    return outs[:9], outs[9:]


def _state_cols(chunk, width=LANE_CHUNK):
    half, off = divmod(chunk * width, HALF_STATE)
    lo = half * HALF_COLS + off
    return slice(lo, lo + width), slice(lo + HALF_STATE, lo + HALF_STATE + width)


def _half_cols(half):
    lo = half * HALF_COLS
    return slice(lo, lo + HALF_STATE), slice(lo + HALF_STATE, lo + HALF_COLS)


def _rows_to_segments(src_ref, stage_ref, dst_ref):
    nc = SSM_W // LANES
    for c in range(nc):
        stage_ref[c] = src_ref[:, c * LANES:(c + 1) * LANES]
    for c in range(nc):
        for k in range(SSM_SEG):
            dst_ref[k * SUBLANES:(k + 1) * SUBLANES, c * LANES:(c + 1) * LANES] = (
                stage_ref[c, pl.ds(k, SUBLANES, stride=SSM_SEG), :])


def _rows_from_segments(src_ref, stage_ref, dst_ref):
    nc = SSM_W // LANES
    for c in range(nc):
        for k in range(SSM_SEG):
            stage_ref[c, pl.ds(k, SUBLANES, stride=SSM_SEG), :] = (
                src_ref[k * SUBLANES:(k + 1) * SUBLANES, c * LANES:(c + 1) * LANES])
    for c in range(nc):
        dst_ref[:, c * LANES:(c + 1) * LANES] = stage_ref[c]


def _ssm_scan(s_ref, pw_ref, init_ref, reverse, unroll, width=LANE_CHUNK):
    for chunk in range(N_STATE // width):
        re, im = _state_cols(chunk, width)
        ar = jnp.broadcast_to(pw_ref[0:1, re], (SUBLANES, width))
        ai = jnp.broadcast_to(pw_ref[0:1, im], (SUBLANES, width))
        if reverse:
            ai = -ai

        def step(j, carry, re=re, im=im, ar=ar, ai=ai):
            sr, si = carry
            k = (SSM_SEG - 1 - j) if reverse else j
            r0 = pl.multiple_of(k * SUBLANES, SUBLANES)
            nr = ar * sr - ai * si + s_ref[pl.ds(r0, SUBLANES), re]
            ni = ar * si + ai * sr + s_ref[pl.ds(r0, SUBLANES), im]
            s_ref[pl.ds(r0, SUBLANES), re] = nr
            s_ref[pl.ds(r0, SUBLANES), im] = ni
            return nr, ni

        if init_ref is None:
            init = (jnp.zeros((SUBLANES, width), f32),) * 2
        else:
            init = (init_ref[:, re], init_ref[:, im])
        lax.fori_loop(0, SSM_SEG, step, init, unroll=unroll)


def _ssm_add_carry(s_ref, pw_ref, cm_ref, reverse):
    for chunk in range(N_STATE // LANE_CHUNK):
        re, im = _state_cols(chunk)
        cr, ci = cm_ref[:, re], cm_ref[:, im]
        for k in range(SSM_SEG):
            pk = (SSM_SEG - 1 - k) if reverse else k
            pr = pw_ref[pk:pk + 1, re]
            pi = pw_ref[pk:pk + 1, im]
            if reverse:
                pi = -pi
            rows = slice(k * SUBLANES, (k + 1) * SUBLANES)
            s_ref[rows, re] = s_ref[rows, re] + (pr * cr - pi * ci)
            s_ref[rows, im] = s_ref[rows, im] + (pr * ci + pi * cr)


def _ssm_carries(first_row, s_ref, pw_ref, carry_ref, cm_ref, reverse):
    order = range(SUBLANES - 1, -1, -1) if reverse else range(SUBLANES)
    for half in range(N_HALF):
        re, im = _half_cols(half)
        a_r, a_i = pw_ref[SSM_SEG - 1:SSM_SEG, re], pw_ref[SSM_SEG - 1:SSM_SEG, im]
        if reverse:
            a_i = -a_i
        cr, ci = carry_ref[0:1, re], carry_ref[0:1, im]
        for seg in order:
            cm_ref[seg:seg + 1, re] = cr
            cm_ref[seg:seg + 1, im] = ci
            er = s_ref[first_row + seg:first_row + seg + 1, re]
            ei = s_ref[first_row + seg:first_row + seg + 1, im]
            cr, ci = a_r * cr - a_i * ci + er, a_r * ci + a_i * cr + ei
        carry_ref[0:1, re] = cr
        carry_ref[0:1, im] = ci


def _ssm_fwd(u, b_half, c_half, pw, d_skip, side_blocks):
    s_len = u.shape[0]
    tb = SSM_BLOCK
    n = s_len // tb
    ns = len(side_blocks)
    side_in_specs, side_shapes, side_sems = _side_gather_specs(side_blocks)

    def body(*refs):
        u_ref, b_ref, c_ref, pw_ref, d_ref = refs[:5]
        side_ins = refs[5:5 + ns]
        y_ref, cm_ref = refs[5 + ns:7 + ns]
        side_outs = refs[7 + ns:7 + 2 * ns]
        s_ref, carry_ref, up_ref, yp_ref, stage_ref = refs[7 + 2 * ns:12 + 2 * ns]
        side_start, side_forward, side_finish = _side_gather_two_level(side_ins, side_outs, *refs[12 + 2 * ns:])
        i = pl.program_id(0)

        @pl.when(i == 0)
        def _():
            carry_ref[...] = jnp.zeros_like(carry_ref)
            side_start()

        pl.when(i == (3 * n) // 4)(side_forward)
        _rows_to_segments(u_ref, stage_ref, up_ref)
        u = up_ref[...]
        ub = u.astype(bf16)
        for half in range(N_HALF):
            s_ref[:, half * HALF_COLS:(half + 1) * HALF_COLS] = _dot(ub[:, half * HALF_W:(half + 1) * HALF_W], b_ref[half])
        _ssm_scan(s_ref, pw_ref, None, reverse=False, unroll=4, width=2 * LANE_CHUNK)
        _ssm_carries(tb - SUBLANES, s_ref, pw_ref, carry_ref, cm_ref, reverse=False)
        _ssm_add_carry(s_ref, pw_ref, cm_ref, reverse=False)
        for half in range(N_HALF):
            cols = slice(half * HALF_W, (half + 1) * HALF_W)
            sb = s_ref[:, half * HALF_COLS:(half + 1) * HALF_COLS].astype(bf16)
            yp_ref[:, cols] = _dot(sb, c_ref[half]) + d_ref[:, cols] * u[:, cols]
        _rows_from_segments(yp_ref, stage_ref, y_ref)

        pl.when(i == n - 1)(side_finish)

    outs = pl.pallas_call(
        body, name="ssm_fwd", grid=(n,),
        in_specs=[_row_spec(tb, SSM_W), _const_spec((N_HALF, HALF_W, HALF_COLS)), _const_spec((N_HALF, HALF_COLS, HALF_W)),
                  _const_spec((SSM_SEG, 2 * N_STATE)), _const_spec((1, SSM_W))] + side_in_specs,
        out_specs=[_row_spec(tb, SSM_W), _row_spec(SUBLANES, 2 * N_STATE)] + side_in_specs,
        out_shape=[_sds((s_len, SSM_W), f32), _sds((n * SUBLANES, 2 * N_STATE), f32)] + side_shapes,
        scratch_shapes=[pltpu.VMEM((tb, 2 * N_STATE), f32), pltpu.VMEM((SUBLANES, 2 * N_STATE), f32),
                        pltpu.VMEM((tb, SSM_W), f32), pltpu.VMEM((tb, SSM_W), f32),
                        pltpu.VMEM((SSM_W // LANES, tb, LANES), f32)] + side_sems,
        compiler_params=_cparams(("arbitrary",)),
    )(u, b_half, c_half, pw, d_skip, *side_blocks)
    return outs[0], outs[1], outs[2:]


def _layer_norm_fwd(r, g, b):
    mu = jnp.mean(r, axis=-1, keepdims=True)
    var = jnp.mean(jnp.square(r - mu), axis=-1, keepdims=True)
    rstd = lax.rsqrt(var + LN_EPS)
    xhat = (r - mu) * rstd
    return xhat, rstd, xhat * g + b


def _layer_norm_bwd(dy, xhat, rstd, g):
    dxh = dy * g
    m1 = jnp.mean(dxh, axis=-1, keepdims=True)
    m2 = jnp.mean(dxh * xhat, axis=-1, keepdims=True)
    return rstd * (dxh - m1 - xhat * m2)


def _branch_outputs(ys_ref, ain_ref, o_ref, wglu_ref, wco_ref, wxo_ref):
    ysb = _gelu(ys_ref[...]).astype(bf16)
    glu = _dot(ysb, wglu_ref[...], NT)
    ga, sb = glu[:, :D_MODEL], jax.nn.sigmoid(glu[:, D_MODEL:])
    ya = _dot(ain_ref[...], wco_ref[...], NT)
    yc = _dot(o_ref[...], wxo_ref[...], NT)
    return ysb, ga, sb, ya, ga * sb, yc


def _mid_fwd(y_ssm, g, ain, ob, x, w_glu_t, w_co_t, w_xo_t, w_out, ln1_g, ln1_b):
    s_len = x.shape[0]
    tm = TOKEN_TILE
    n = s_len // tm

    def body(ys_ref, g_ref, ain_ref, o_ref, x_ref, wglu_ref, wco_ref, wxo_ref, wout_ref, lg_ref, lb_ref,
             ysbt_ref, mb_ref, xhat_ref, rstd_ref):
        ysb, _, _, ya, yb, yc = _branch_outputs(ys_ref, ain_ref, o_ref, wglu_ref, wco_ref, wxo_ref)
        ysbt_ref[...] = ysb.T
        gt = g_ref[...].astype(f32)
        merged = gt[:, :D_MODEL] * ya + gt[:, D_MODEL:2 * D_MODEL] * yb + gt[:, 2 * D_MODEL:] * yc
        mb = merged.astype(bf16)
        mb_ref[...] = mb
        r1 = ALPHA * x_ref[...] + _dot(mb, wout_ref[...])
        xhat, rstd, _ = _layer_norm_fwd(r1, lg_ref[...], lb_ref[...])
        xhat_ref[...] = xhat
        rstd_ref[...] = rstd

    row_cols = [(D_MODEL, bf16), (D_MODEL, f32), (1, f32)]
    return pl.pallas_call(
        body, name="mid_fwd", grid=(n,),
        in_specs=[_row_spec(tm, SSM_W), _row_spec(tm, GATE_COLS), _row_spec(tm, CONV_W), _row_spec(tm, XATTN_W),
                  _row_spec(tm, D_MODEL), _const_spec((2 * D_MODEL, SSM_W)), _const_spec((D_MODEL, CONV_W)),
                  _const_spec((D_MODEL, XATTN_W)), _const_spec((D_MODEL, D_MODEL)),
                  _const_spec((1, D_MODEL)), _const_spec((1, D_MODEL))],
        out_specs=[_col_spec(SSM_W, tm)] + [_row_spec(tm, c) for c, _ in row_cols],
        out_shape=[_sds((SSM_W, s_len), bf16)] + [_sds((s_len, c), dt) for c, dt in row_cols],
        compiler_params=_cparams(("parallel",)),
    )(y_ssm, g, ain, ob, x, w_glu_t, w_co_t, w_xo_t, w_out, ln1_g, ln1_b)


def _mlp_fwd_bwd(xhat1, tgt, ln1_g, ln1_b, w_up_t, b_up, w_down, b_down, ln2_g, ln2_b):
    s_len = xhat1.shape[0]
    tm = TOKEN_TILE
    n = s_len // tm
    fc = 1024
    nfc = D_FF // fc

    def body(xh_ref, t_ref, l1g_ref, l1b_ref, wup_ref, bup_ref, wdn_ref, bdn_ref, l2g_ref, l2b_ref,
             x1bt_ref, hdn_ref, dr2bt_ref, dpre_ref, dx1_ref,
             loss_ref, dl2g_ref, dl2b_ref, dbdn_ref, dbup_ref, rl_ref):
        i = pl.program_id(0)

        @pl.when(i == 0)
        def _():
            loss_ref[...] = jnp.zeros_like(loss_ref)
            dl2g_ref[...] = jnp.zeros_like(dl2g_ref)
            dl2b_ref[...] = jnp.zeros_like(dl2b_ref)
            dbdn_ref[...] = jnp.zeros_like(dbdn_ref)
            dbup_ref[...] = jnp.zeros_like(dbup_ref)

        x1 = xh_ref[...] * l1g_ref[...] + l1b_ref[...]
        x1b = x1.astype(bf16)
        x1bt_ref[...] = x1b.T
        chunks = [slice(c * fc, (c + 1) * fc) for c in range(nfc)]
        pres = [_dot(x1b, wup_ref[cols, :], NT) for cols in chunks]
        hbs = []
        for cols, pre in zip(chunks, pres):
            rl = jnp.maximum(pre + bup_ref[:, cols], 0.0)
            rl_ref[:, cols] = rl
            hb = (rl * rl).astype(bf16)
            hdn_ref[:, cols] = hb
            hbs.append(hb)
        acc = _dot(hbs[0], wdn_ref[chunks[0], :])
        for cols, hb in zip(chunks[1:], hbs[1:]):
            acc = acc + _dot(hb, wdn_ref[cols, :])
        r2 = ALPHA * x1 + acc + bdn_ref[...]
        xhat2, rstd2, y = _layer_norm_fwd(r2, l2g_ref[...], l2b_ref[...])
        err = y - t_ref[...]
        loss_ref[...] += jnp.sum(jnp.sum(err * err, axis=1, keepdims=True), axis=0, keepdims=True) * (0.5 / D_MODEL)
        dy = err * (1.0 / D_MODEL)
        dl2g_ref[...] += _colsum(dy * xhat2)
        dl2b_ref[...] += _colsum(dy)
        dr2 = _layer_norm_bwd(dy, xhat2, rstd2, l2g_ref[...])
        dbdn_ref[...] += _colsum(dr2)
        dr2b = dr2.astype(bf16)
        dr2bt_ref[...] = dr2b.T
        dhs = [_dot(dr2b, wdn_ref[cols, :], NT) for cols in chunks]
        dpbs = []
        for cols, dh in zip(chunks, dhs):
            dpre = dh * (2.0 * rl_ref[:, cols])
            dbup_ref[:, cols] += _colsum(dpre)
            dpb = dpre.astype(bf16)
            dpre_ref[:, cols] = dpb
            dpbs.append(dpb)
        dacc = _dot(dpbs[0], wup_ref[chunks[0], :])
        for cols, dpb in zip(chunks[1:], dpbs[1:]):
            dacc = dacc + _dot(dpb, wup_ref[cols, :])
        dx1_ref[...] = ALPHA * dr2 + dacc

    acc_shapes = [(1, LANES), (1, D_MODEL), (1, D_MODEL), (1, D_MODEL), (1, D_FF)]
    return pl.pallas_call(
        body, name="mlp_fwd_bwd", grid=(n,),
        in_specs=[_row_spec(tm, D_MODEL), _row_spec(tm, D_MODEL), _const_spec((1, D_MODEL)), _const_spec((1, D_MODEL)),
                  _const_spec((D_FF, D_MODEL)), _const_spec((1, D_FF)), _const_spec((D_FF, D_MODEL)),
                  _const_spec((1, D_MODEL)), _const_spec((1, D_MODEL)), _const_spec((1, D_MODEL))],
        out_specs=([_col_spec(D_MODEL, tm), _row_spec(tm, D_FF), _col_spec(D_MODEL, tm), _row_spec(tm, D_FF),
                    _row_spec(tm, D_MODEL)] + [_acc_spec(s) for s in acc_shapes]),
        out_shape=([_sds((D_MODEL, s_len), bf16), _sds((s_len, D_FF), bf16), _sds((D_MODEL, s_len), bf16),
                    _sds((s_len, D_FF), bf16), _sds((s_len, D_MODEL), f32)] + [_sds(s, f32) for s in acc_shapes]),
        scratch_shapes=[pltpu.VMEM((tm, D_FF), f32)],
        compiler_params=_cparams(("arbitrary",)),
    )(xhat1, tgt, ln1_g, ln1_b, w_up_t, b_up, w_down, b_down, ln2_g, ln2_b)


def _mid_bwd(dx1, xhat1, rstd1, g, ain, ob, y_ssm, ln1_g, w_out, w_glu_t, w_co_t, w_xo_t):
    s_len = dx1.shape[0]
    tm = TOKEN_TILE
    n = s_len // tm

    def body(dx1_ref, xh_ref, rs_ref, g_ref, ain_ref, o_ref, ys_ref, lg_ref, wout_ref, wglu_ref, wco_ref, wxo_ref,
             dxp_ref, dr1bt_ref, dgp_ref, dya_ref, dyc_ref, dglu_ref, dyssm_ref,
             dl1g_ref, dl1b_ref, dbg_ref):
        i = pl.program_id(0)

        @pl.when(i == 0)
        def _():
            dl1g_ref[...] = jnp.zeros_like(dl1g_ref)
            dl1b_ref[...] = jnp.zeros_like(dl1b_ref)
            dbg_ref[...] = jnp.zeros_like(dbg_ref)

        dx1 = dx1_ref[...]
        xhat = xh_ref[...]
        dl1g_ref[...] += _colsum(dx1 * xhat)
        dl1b_ref[...] += _colsum(dx1)
        dr1 = _layer_norm_bwd(dx1, xhat, rs_ref[...], lg_ref[...])
        dxp_ref[...] = ALPHA * dr1
        dr1b = dr1.astype(bf16)
        dr1bt_ref[...] = dr1b.T
        dm = _dot(dr1b, wout_ref[...], NT)

        _, ga, sb, ya, yb, yc = _branch_outputs(ys_ref, ain_ref, o_ref, wglu_ref, wco_ref, wxo_ref)
        gt = g_ref[...].astype(f32)
        branch = (ya, yb, yc)
        for j in range(3):
            cols = slice(j * D_MODEL, (j + 1) * D_MODEL)
            gj = gt[:, cols]
            dgp = dm * branch[j] * gj * (1.0 - gj)
            dbg_ref[:, cols] += _colsum(dgp)
            dgp_ref[:, cols] = dgp.astype(bf16)
        dya_ref[...] = (dm * gt[:, :D_MODEL]).astype(bf16)
        dyc_ref[...] = (dm * gt[:, 2 * D_MODEL:]).astype(bf16)
        dyb = dm * gt[:, D_MODEL:2 * D_MODEL]
        dga = (dyb * sb).astype(bf16)
        dgb = (dyb * ga * sb * (1.0 - sb)).astype(bf16)
        dglu_ref[:, :D_MODEL] = dga
        dglu_ref[:, D_MODEL:] = dgb
        dys = _dot(dga, wglu_ref[:D_MODEL, :]) + _dot(dgb, wglu_ref[D_MODEL:, :])
        dyssm_ref[...] = dys * _gelu_grad(ys_ref[...])

    row_cols = [(GATE_COLS, bf16), (D_MODEL, bf16), (D_MODEL, bf16), (2 * D_MODEL, bf16), (SSM_W, f32)]
    acc_shapes = [(1, D_MODEL), (1, D_MODEL), (1, GATE_COLS)]
    return pl.pallas_call(
        body, name="mid_bwd", grid=(n,),
        in_specs=[_row_spec(tm, D_MODEL), _row_spec(tm, D_MODEL), _row_spec(tm, 1), _row_spec(tm, GATE_COLS),
                  _row_spec(tm, CONV_W), _row_spec(tm, XATTN_W), _row_spec(tm, SSM_W),
                  _const_spec((1, D_MODEL)), _const_spec((D_MODEL, D_MODEL)), _const_spec((2 * D_MODEL, SSM_W)),
                  _const_spec((D_MODEL, CONV_W)), _const_spec((D_MODEL, XATTN_W))],
        out_specs=([_row_spec(tm, D_MODEL), _col_spec(D_MODEL, tm)] + [_row_spec(tm, c) for c, _ in row_cols]
                   + [_acc_spec(s) for s in acc_shapes]),
        out_shape=([_sds((s_len, D_MODEL), f32), _sds((D_MODEL, s_len), bf16)]
                   + [_sds((s_len, c), dt) for c, dt in row_cols] + [_sds(s, f32) for s in acc_shapes]),
        compiler_params=_cparams(("arbitrary",)),
    )(dx1, xhat1, rstd1, g, ain, ob, y_ssm, ln1_g, w_out, w_glu_t, w_co_t, w_xo_t)


def _ssm_bwd(u, dy, cm_all, b_half, c_half, pw, d_skip):
    s_len = u.shape[0]
    tb = SSM_BLOCK
    n = s_len // tb

    def body(u_ref, dy_ref, cm_ref, b_ref, c_ref, pw_ref, d_ref,
             du_ref, db_hbm, dc_hbm, da_ref, dd_ref,
             s_ref, g_ref, gcarry_ref, gcm_ref, db_ref, dc_ref, up_ref, dyp_ref, dup_ref, stage_ref):
        i = pl.program_id(0)

        @pl.when(i == 0)
        def _():
            gcarry_ref[...] = jnp.zeros_like(gcarry_ref)
            db_ref[...] = jnp.zeros_like(db_ref)
            dc_ref[...] = jnp.zeros_like(dc_ref)
            da_ref[...] = jnp.zeros_like(da_ref)
            dd_ref[...] = jnp.zeros_like(dd_ref)

        _rows_to_segments(u_ref, stage_ref, up_ref)
        _rows_to_segments(dy_ref, stage_ref, dyp_ref)
        u = up_ref[...]
        ub = u.astype(bf16)
        dy = dyp_ref[...]
        dyb = dy.astype(bf16)
        dd_ref[...] += _colsum(dy * u)

        for half in range(N_HALF):
            s_ref[:, half * HALF_COLS:(half + 1) * HALF_COLS] = _dot(ub[:, half * HALF_W:(half + 1) * HALF_W], b_ref[half])
        _ssm_scan(s_ref, pw_ref, cm_ref, reverse=False, unroll=True)

        for half in range(N_HALF):
            g_ref[:, half * HALF_COLS:(half + 1) * HALF_COLS] = _dot(dyb[:, half * HALF_W:(half + 1) * HALF_W], c_ref[half], NT)
        _ssm_scan(g_ref, pw_ref, None, reverse=True, unroll=True)
        _ssm_carries(0, g_ref, pw_ref, gcarry_ref, gcm_ref, reverse=True)
        _ssm_add_carry(g_ref, pw_ref, gcm_ref, reverse=True)

        for half in range(N_HALF):
            cols = slice(half * HALF_W, (half + 1) * HALF_W)
            scols = slice(half * HALF_COLS, (half + 1) * HALF_COLS)
            gb = g_ref[:, scols].astype(bf16)
            dup_ref[:, cols] = _dot(gb, b_ref[half], NT) + d_ref[:, cols] * dy[:, cols]
            db_ref[half] += _dot(ub[:, cols], gb, TN)
            dc_ref[half] += _dot(s_ref[:, scols].astype(bf16), dyb[:, cols], TN)
        _rows_from_segments(dup_ref, stage_ref, du_ref)

        for chunk in range(N_STATE // LANE_CHUNK):
            re, im = _state_cols(chunk)
            acc_r = da_ref[:, re]
            acc_i = da_ref[:, im]
            for k in range(SSM_SEG):
                rows = slice(k * SUBLANES, (k + 1) * SUBLANES)
                if k == 0:
                    pr, pi = cm_ref[:, re], cm_ref[:, im]
                else:
                    prev = slice((k - 1) * SUBLANES, k * SUBLANES)
                    pr, pi = s_ref[prev, re], s_ref[prev, im]
                gr, gi = g_ref[rows, re], g_ref[rows, im]
                acc_r = acc_r + (gr * pr + gi * pi)
                acc_i = acc_i + (gi * pr - gr * pi)
            da_ref[:, re] = acc_r
            da_ref[:, im] = acc_i

        @pl.when(i == n - 1)
        def _():
            pltpu.sync_copy(db_ref, db_hbm)
            pltpu.sync_copy(dc_ref, dc_hbm)

    rev = functools.partial(_row_spec, rev_n=n)
    any_spec = pl.BlockSpec(memory_space=pl.ANY)
    state_rows = pltpu.VMEM((tb, 2 * N_STATE), f32)
    seg_rows = pltpu.VMEM((SUBLANES, 2 * N_STATE), f32)
    tok_rows = pltpu.VMEM((tb, SSM_W), f32)
    return pl.pallas_call(
        body, name="ssm_bwd", grid=(n,),
        in_specs=[rev(tb, SSM_W), rev(tb, SSM_W), rev(SUBLANES, 2 * N_STATE),
                  _const_spec((N_HALF, HALF_W, HALF_COLS)), _const_spec((N_HALF, HALF_COLS, HALF_W)),
                  _const_spec((SSM_SEG, 2 * N_STATE)), _const_spec((1, SSM_W))],
        out_specs=[rev(tb, SSM_W), any_spec, any_spec, _acc_spec((SUBLANES, 2 * N_STATE)), _acc_spec((1, SSM_W))],
        out_shape=[_sds((s_len, SSM_W), f32), _sds((N_HALF, HALF_W, HALF_COLS), f32),
                   _sds((N_HALF, HALF_COLS, HALF_W), f32), _sds((SUBLANES, 2 * N_STATE), f32), _sds((1, SSM_W), f32)],
        scratch_shapes=[state_rows, state_rows, seg_rows, seg_rows,
                        pltpu.VMEM((N_HALF, HALF_W, HALF_COLS), f32), pltpu.VMEM((N_HALF, HALF_COLS, HALF_W), f32),
                        tok_rows, tok_rows, tok_rows, pltpu.VMEM((SSM_W // LANES, tb, LANES), f32)],
        compiler_params=_cparams(("arbitrary",)),
    )(u, dy, cm_all, b_half, c_half, pw, d_skip)


def _branch_bwd(dya, dyc, cin, q, kv, k_t, conv_w, w_co_t, w_xo_t, side_blocks):
    s_len = dya.shape[0]
    tm = TOKEN_TILE
    n = s_len // tm
    halo_blocks = tm // 8
    ns = len(side_blocks)
    conv_tile = _sds((8, CONV_W), f32)
    side_in_specs, side_shapes, side_sems = _side_gather_specs(list(side_blocks) + [conv_tile])

    def body(*refs):
        (dya_ref, dyc_ref, cin_ref, cprev_ref, q_ref, kv_ref, cw_ref, wco_ref, wxo_ref, kt_ref) = refs[:10]
        side_ins = refs[10:10 + ns]
        dconv_ref, dq_ref, dkv_ref = refs[10 + ns:13 + ns]
        side_outs = refs[13 + ns:14 + 2 * ns]
        zs_ref, dczs_ref, dcw_ref = refs[14 + 2 * ns:17 + 2 * ns]
        copies = _side_gather_copies(list(side_ins) + [dcw_ref], side_outs, *refs[17 + 2 * ns:])
        side, conv_side = copies[:ns * N_DEV], copies[ns * N_DEV:]
        i = pl.program_id(0)
        tile = n - 1 - i

        @pl.when(i == 0)
        def _():
            dcw_ref[...] = jnp.zeros_like(dcw_ref)
            dkv_ref[...] = jnp.zeros_like(dkv_ref)
            dczs_ref[tm:tm + 8, :] = jnp.zeros((8, CONV_W), f32)
            for cp in side:
                cp.start()

        cin = cin_ref[...]
        cb, cc, ch = cin[:, :CONV_W], cin[:, CONV_W:2 * CONV_W], cin[:, 2 * CONV_W:]
        z = cc * ch
        cprev = cprev_ref[...]
        zprev = cprev[:, CONV_W:2 * CONV_W] * cprev[:, 2 * CONV_W:]
        zs_ref[0:8, :] = jnp.where(tile == 0, 0.0, zprev)
        zs_ref[8:8 + tm, :] = z
        z1 = zs_ref[pl.ds(7, tm), :]
        z2 = zs_ref[pl.ds(6, tm), :]
        cw = cw_ref[...]
        cz = cw[0:1] * z2 + cw[1:2] * z1 + cw[2:3] * z

        dain = _dot(dya_ref[...], wco_ref[...])
        dcb = dain * cz
        dcz = dain * cb
        dczs_ref[0:tm, :] = dcz
        dcz1 = dczs_ref[pl.ds(1, tm), :]
        dcz2 = dczs_ref[pl.ds(2, tm), :]
        dz = cw[2:3] * dcz + cw[1:2] * dcz1 + cw[0:1] * dcz2
        dczs_ref[tm:tm + 8, :] = dczs_ref[0:8, :]
        dcw_ref[0:1, :] += _colsum(dcz * z2)
        dcw_ref[1:2, :] += _colsum(dcz * z1)
        dcw_ref[2:3, :] += _colsum(dcz * z)
        dconv_ref[:, :CONV_W] = dcb.astype(bf16)
        dconv_ref[:, CONV_W:2 * CONV_W] = (dz * ch).astype(bf16)
        dconv_ref[:, 2 * CONV_W:] = (dz * cc).astype(bf16)

        qb = q_ref[...]
        dob = _dot(dyc_ref[...], wxo_ref[...]).astype(bf16)
        kv = kv_ref[...]
        heads = range(HEADS)
        hcs = [slice(h * HEAD_DIM, (h + 1) * HEAD_DIM) for h in heads]
        vcs = [slice(XATTN_W + h * HEAD_DIM, XATTN_W + (h + 1) * HEAD_DIM) for h in heads]
        s_t = [_dot(kv[:, hcs[h]], qb[:, hcs[h]], NT) * (HEAD_DIM ** -0.5) for h in heads]
        dp_t = [_dot(kv[:, vcs[h]], dob[:, hcs[h]], NT) for h in heads]
        e_t = [jnp.exp(s_t[h] - jnp.max(s_t[h], axis=0, keepdims=True)) for h in heads]
        p_t = [e_t[h] / jnp.sum(e_t[h], axis=0, keepdims=True) for h in heads]
        dv = [_dot(p_t[h].astype(bf16), dob[:, hcs[h]]) for h in heads]
        ds_t = [(p_t[h] * (dp_t[h] - jnp.sum(dp_t[h] * p_t[h], axis=0, keepdims=True)) * (HEAD_DIM ** -0.5)).astype(bf16)
                for h in heads]
        dk = [_dot(ds_t[h], qb[:, hcs[h]]) for h in heads]
        dq_t = [_dot(kt_ref[hcs[h], :], ds_t[h]) for h in heads]
        dq_ref[...] = jnp.concatenate(dq_t, axis=0).T.astype(bf16)
        dkv_ref[...] += jnp.concatenate(dk + dv, axis=1)

        @pl.when(i == n - 1)
        def _():
            for cp in conv_side:
                cp.start()
            for cp in side + conv_side:
                cp.wait()

    rev = functools.partial(_row_spec, rev_n=n)
    prev_spec = pl.BlockSpec((8, 3 * CONV_W), lambda i: (jnp.maximum((n - 1 - i) * halo_blocks - 1, 0), 0))
    outs = pl.pallas_call(
        body, name="branch_bwd", grid=(n,),
        in_specs=[rev(tm, D_MODEL), rev(tm, D_MODEL), rev(tm, 3 * CONV_W), prev_spec, rev(tm, XATTN_W),
                  _const_spec((MEM_LEN, 2 * XATTN_W)), _const_spec((3, CONV_W)), _const_spec((D_MODEL, CONV_W)),
                  _const_spec((D_MODEL, XATTN_W)), _const_spec((XATTN_W, MEM_LEN))] + side_in_specs[:ns],
        out_specs=[rev(tm, 3 * CONV_W), rev(tm, XATTN_W), _acc_spec((MEM_LEN, 2 * XATTN_W))] + side_in_specs,
        out_shape=[_sds((s_len, 3 * CONV_W), bf16), _sds((s_len, XATTN_W), bf16),
                   _sds((MEM_LEN, 2 * XATTN_W), f32)] + side_shapes,
        scratch_shapes=[pltpu.VMEM((tm + 8, CONV_W), f32), pltpu.VMEM((tm + 8, CONV_W), f32),
                        pltpu.VMEM((8, CONV_W), f32)] + side_sems,
        compiler_params=_cparams(("arbitrary",)),
    )(dya, dyc, cin, cin, q, kv, conv_w, w_co_t, w_xo_t, k_t, *side_blocks)
    return outs[0], outs[1], outs[2], outs[3:]


def _in_proj_bwd(dgp, dconv, du, dq, dxp, w_in_t):
    s_len = dgp.shape[0]
    tm = 2 * TOKEN_TILE
    n = s_len // tm

    def body(dgp_ref, dconv_ref, du_ref, dq_ref, dxp_ref, win_ref, dx_ref, dproj_ref):
        dproj = jnp.concatenate([dgp_ref[...], dconv_ref[...], du_ref[...].astype(bf16), dq_ref[...]], axis=1)
        dproj_ref[...] = dproj
        dx_ref[...] = dxp_ref[...] + _dot(dproj, win_ref[...])

    return pl.pallas_call(
        body, name="in_proj_bwd", grid=(n,),
        in_specs=[_row_spec(tm, GATE_COLS), _row_spec(tm, 3 * CONV_W), _row_spec(tm, SSM_W), _row_spec(tm, XATTN_W),
                  _row_spec(tm, D_MODEL), _const_spec((IN_COLS, D_MODEL))],
        out_specs=[_row_spec(tm, D_MODEL), _row_spec(tm, IN_COLS)],
        out_shape=[_sds((s_len, D_MODEL), f32), _sds((s_len, IN_COLS), bf16)],
        compiler_params=_cparams(("parallel",)),
    )(dgp, dconv, du, dq, dxp, w_in_t)


N_CHIP = 4
CHIP_STEPS = [(1, 1), (1, 0), (0, 1), (0, 0)]


def _flip(v, d):
    return 1 - v if d else v


def _chip_order():
    x, y, _ = _mesh_place()
    return jnp.stack([2 * _flip(x, dx) + _flip(y, dy) for dx, dy in CHIP_STEPS]).astype(jnp.int32)


def _weight_grad_scatter(a_t, b, name, tm, tt):
    m, s_len = a_t.shape
    n_cols = b.shape[1]
    w = n_cols // N_DEV
    tn = 2 * w
    tm, tt = min(tm, m), min(tt, s_len)
    nm, nt = m // tm, s_len // tt
    assert m % tm == 0 and s_len % tt == 0

    def body(order_ref, a_ref, b_ref, recv_ref, acc_ref, send_ref, sib_ref, stash_ref,
             d2d_send, d2d_recv, ici_send, ici_recv, local_sem):
        del order_ref
        q, im, t = pl.program_id(0), pl.program_id(1), pl.program_id(2)
        x, y, c = _mesh_place()
        mesh_id = pl.DeviceIdType.MESH

        @pl.when(t == 0)
        def _():
            acc_ref[...] = jnp.zeros_like(acc_ref)

        acc_ref[...] += _dot(a_ref[...], b_ref[...])

        def to_sibling(qq, imm):
            rows = pl.ds(pl.multiple_of(imm * tm, tm), tm)
            return pltpu.make_async_remote_copy(
                src_ref=send_ref.at[qq, 0, rows, :], dst_ref=sib_ref.at[qq, rows, :],
                send_sem=d2d_send.at[qq], recv_sem=d2d_recv.at[qq, imm],
                device_id=(x, y, 1 - c), device_id_type=mesh_id)

        def finish_tile(qq, imm):
            rows = pl.ds(pl.multiple_of(imm * tm, tm), tm)
            to_sibling(qq, imm).wait_recv()
            both = stash_ref[...] + sib_ref[qq, rows, :].astype(f32)
            send_ref[qq, 1, rows, :] = both.astype(bf16)
            for step, (dx, dy) in enumerate(CHIP_STEPS):
                @pl.when(qq == step)
                def _(step=step, dx=dx, dy=dy):
                    src, dst = send_ref.at[step, 1, rows, :], recv_ref.at[step, rows, :]
                    if dx or dy:
                        pltpu.make_async_remote_copy(
                            src_ref=src, dst_ref=dst, send_sem=ici_send.at[step], recv_sem=ici_recv.at[step],
                            device_id=(_flip(x, dx), _flip(y, dy), c), device_id_type=mesh_id).start()
                    else:
                        pltpu.make_async_copy(src, dst, local_sem).start()

        @pl.when(t == nt - 1)
        def _():
            tile = q * nm + im

            @pl.when(tile > 0)
            def _():
                finish_tile((tile - 1) // nm, (tile - 1) % nm)

            rows = pl.ds(pl.multiple_of(im * tm, tm), tm)
            for core in (0, 1):
                @pl.when(c == core)
                def _(core=core):
                    other = 1 - core
                    send_ref[q, 0, rows, :] = acc_ref[:, other * w:(other + 1) * w].astype(bf16)
                    stash_ref[...] = acc_ref[:, core * w:(core + 1) * w]
            to_sibling(q, im).start()

            @pl.when(tile == N_CHIP * nm - 1)
            def _():
                finish_tile(q, im)

        @pl.when((q == N_CHIP - 1) & (im == nm - 1) & (t == nt - 1))
        def _():
            for step, (dx, dy) in enumerate(CHIP_STEPS):
                pltpu.make_async_remote_copy(
                    src_ref=send_ref.at[step, 0], dst_ref=sib_ref.at[step],
                    send_sem=d2d_send.at[step], recv_sem=d2d_recv.at[step, 0],
                    device_id=(x, y, 1 - c), device_id_type=mesh_id).wait_send()
                src, dst = send_ref.at[step, 1], recv_ref.at[step]
                if dx or dy:
                    pltpu.make_async_remote_copy(
                        src_ref=src, dst_ref=dst, send_sem=ici_send.at[step], recv_sem=ici_recv.at[step],
                        device_id=(_flip(x, dx), _flip(y, dy), c), device_id_type=mesh_id).wait()
                else:
                    pltpu.make_async_copy(src, dst, local_sem).wait()

    grid_spec = pltpu.PrefetchScalarGridSpec(
        num_scalar_prefetch=1, grid=(N_CHIP, nm, nt),
        in_specs=[pl.BlockSpec((tm, tt), lambda q, im, t, order: (im, t)),
                  pl.BlockSpec((tt, tn), lambda q, im, t, order: (t, order[q]))],
        out_specs=pl.BlockSpec(memory_space=pl.ANY),
        scratch_shapes=[pltpu.VMEM((tm, tn), f32), pltpu.VMEM((N_CHIP, 2, m, w), bf16), pltpu.VMEM((N_CHIP, m, w), bf16),
                        pltpu.VMEM((tm, w), f32),
                        pltpu.SemaphoreType.DMA((N_CHIP,)), pltpu.SemaphoreType.DMA((N_CHIP, nm)),
                        pltpu.SemaphoreType.DMA((N_CHIP - 1,)), pltpu.SemaphoreType.DMA((N_CHIP - 1,)),
                        pltpu.SemaphoreType.DMA])
    return pl.pallas_call(
        body, name=name, grid_spec=grid_spec,
        out_shape=_sds((N_CHIP, m, w), bf16),
        compiler_params=_cparams(("arbitrary", "arbitrary", "arbitrary")),
    )(_chip_order(), a_t, b)


def _adamw(w, g, m, v):
    m = ADAM_B1 * m + (1.0 - ADAM_B1) * g
    v = ADAM_B2 * v + (1.0 - ADAM_B2) * jnp.square(g)
    m_hat = m / (1.0 - ADAM_B1 ** ADAM_STEP)
    v_hat = v / (1.0 - ADAM_B2 ** ADAM_STEP)
    delta = -ADAM_LR * (m_hat / (jnp.sqrt(v_hat) + ADAM_EPS) + ADAM_WD * w)
    return delta, m, v


def _sum_parts(p_ref):
    g = p_ref[0].astype(f32)
    for j in range(1, p_ref.shape[0]):
        g = g + p_ref[j].astype(f32)
    return g


def _adamw_update(w, m, v, parts, name, transposed):
    rows, cols = w.shape
    n_parts = parts.shape[0]
    if transposed:
        tc = 256
        steps = cols // tc
        p_spec = pl.BlockSpec((n_parts, tc, rows), lambda i: (0, i, 0))
        spec = pl.BlockSpec((rows, tc), lambda i: (0, i))
    else:
        tr = next(t for t in (256, 128, 64, 32, 16, 8) if rows % t == 0)
        steps = rows // tr
        p_spec = pl.BlockSpec((n_parts, tr, cols), lambda i: (0, i, 0))
        spec = pl.BlockSpec((tr, cols), lambda i: (i, 0))

    def body(w_ref, p_ref, m_ref, v_ref, g_ref, d_ref, nm_ref, nv_ref):
        g = _sum_parts(p_ref)
        if transposed:
            g = g.T
        g_ref[...] = g
        d_ref[...], nm_ref[...], nv_ref[...] = _adamw(w_ref[...], g, m_ref[...], v_ref[...])

    return pl.pallas_call(
        body, name=name, grid=(steps,),
        in_specs=[spec, p_spec, spec, spec], out_specs=[spec] * 4,
        out_shape=[_sds((rows, cols), f32)] * 4,
        compiler_params=_cparams(("parallel",)),
    )(w, parts, m, v)


SMALL_GROUPS = [
    (["b_gate", "ln1_g", "ln1_b", "b_up", "b_down", "ln2_g", "ln2_b", "ssm_d"], 1),
    (["ssm_lam_re", "ssm_lam_im", "ssm_c_re", "ssm_c_im", "ssm_b_re", "ssm_b_im"], 0),
    (["conv_w"], 0),
    (["ssm_log_dt"], 0),
]


def _sum_small(group_parts):
    def body(*refs):
        n = len(refs) // 2
        for p_ref, o_ref in zip(refs[:n], refs[n:]):
            o_ref[...] = _sum_parts(p_ref)

    return pl.pallas_call(
        body, name="sum_small",
        out_shape=[_sds(p.shape[1:], f32) for p in group_parts],
        compiler_params=_cparams(),
    )(*group_parts)


def _adamw_small(ws, ms, vs, group_sums):
    names = [k for group, _ in SMALL_GROUPS for k in group]
    n = len(names)

    def body(*refs):
        w_refs, m_refs, v_refs = (dict(zip(names, refs[j * n:(j + 1) * n])) for j in range(3))
        p_refs = refs[3 * n:3 * n + len(SMALL_GROUPS)]
        out_refs = [dict(zip(names, refs[3 * n + len(SMALL_GROUPS) + j * n:][:n])) for j in range(4)]
        for (group, axis), p_ref in zip(SMALL_GROUPS, p_refs):
            total = p_ref[...]
            off = 0
            for k in group:
                size = SMALL[k][axis]
                g = total[:, off:off + size] if axis == 1 else total[off:off + size, :]
                off += size
                d, nm, nv = _adamw(w_refs[k][...], g, m_refs[k][...], v_refs[k][...])
                for j, val in enumerate((g, d, nm, nv)):
                    out_refs[j][k][...] = val

    res = pl.pallas_call(
        body, name="adamw_small",
        out_shape=[_sds(SMALL[k], f32) for _ in range(4) for k in names],
        compiler_params=_cparams(),
    )(*[ws[k] for k in names], *[ms[k] for k in names], *[vs[k] for k in names], *group_sums)
    return [dict(zip(names, res[j * n:(j + 1) * n])) for j in range(4)]


def _ssm_discretize(lam_re, lam_im, log_dt, b_re, b_im):
    dt = jnp.exp(log_dt)[:, None]
    mag = jnp.exp(lam_re * dt)
    abar_r = mag * jnp.cos(lam_im * dt)
    abar_i = mag * jnp.sin(lam_im * dt)
    den = lam_re * lam_re + lam_im * lam_im
    nr = abar_r - 1.0
    ni = abar_i
    kr = (nr * lam_re + ni * lam_im) / den
    ki = (ni * lam_re - nr * lam_im) / den
    bbar_r = kr[:, None, :] * b_re - ki[:, None, :] * b_im
    bbar_i = kr[:, None, :] * b_im + ki[:, None, :] * b_re
    return abar_r, abar_i, bbar_r, bbar_i


def _state_layout(re, im):
    parts = []
    for half in range(N_HALF):
        cols = slice(half * HALF_STATE, (half + 1) * HALF_STATE)
        parts += [re[..., cols], im[..., cols]]
    return jnp.concatenate(parts, axis=-1)


def _state_unlayout(a):
    re = jnp.concatenate([a[..., _half_cols(h)[0]] for h in range(N_HALF)], axis=-1)
    im = jnp.concatenate([a[..., _half_cols(h)[1]] for h in range(N_HALF)], axis=-1)
    return re, im


def _abar_powers(abar_r, abar_i):
    pr, pi = abar_r.reshape(1, N_STATE), abar_i.reshape(1, N_STATE)
    while pr.shape[0] < SSM_SEG:
        tr, ti = pr[-1:], pi[-1:]
        pr, pi = (jnp.concatenate([pr, pr * tr - pi * ti], axis=0), jnp.concatenate([pi, pr * ti + pi * tr], axis=0))
    return _state_layout(pr, pi)


HALF_GROUPS = SSM_GROUPS // N_HALF


def _half_block_diag(blocks):
    _, r, c = blocks.shape
    eye = jnp.eye(HALF_GROUPS, dtype=blocks.dtype)
    b4 = blocks.reshape(N_HALF, HALF_GROUPS, r, c)
    return jnp.einsum("ngrc,gk->ngrkc", b4, eye).reshape(N_HALF, HALF_GROUPS * r, HALF_GROUPS * c)


def _half_diag_blocks(mat, r, c):
    eye = jnp.eye(HALF_GROUPS, dtype=mat.dtype)
    m5 = mat.reshape(N_HALF, HALF_GROUPS, r, HALF_GROUPS, c)
    return jnp.einsum("ngrkc,gk->ngrc", m5, eye).reshape(SSM_GROUPS, r, c)


BIG = ["w_in", "w_conv_out", "w_glu", "w_kv", "w_xattn_out", "w_out", "w_up", "w_down"]
GATHER_TRANSPOSED = ["w_conv_out", "w_glu", "w_xattn_out", "w_up"]
PARTS_TRANSPOSED = ["w_in", "w_kv", "w_out", "w_down"]
SMALL = {"b_gate": (1, GATE_COLS), "conv_w": (3, CONV_W), "ssm_lam_re": (SSM_GROUPS, SSM_STATE),
         "ssm_lam_im": (SSM_GROUPS, SSM_STATE), "ssm_log_dt": (1, SSM_GROUPS),
         "ssm_b_re": (SSM_W, SSM_STATE), "ssm_b_im": (SSM_W, SSM_STATE),
         "ssm_c_re": (SSM_W, SSM_STATE), "ssm_c_im": (SSM_W, SSM_STATE), "ssm_d": (1, SSM_W),
         "ln1_g": (1, D_MODEL), "ln1_b": (1, D_MODEL), "b_up": (1, D_FF), "b_down": (1, D_MODEL),
         "ln2_g": (1, D_MODEL), "ln2_b": (1, D_MODEL)}
WEIGHTS = ["w_in", "b_gate", "conv_w", "w_conv_out", "ssm_lam_re", "ssm_lam_im", "ssm_log_dt", "ssm_b_re", "ssm_b_im",
           "ssm_c_re", "ssm_c_im", "ssm_d", "w_glu", "w_kv", "w_xattn_out", "w_out", "ln1_g", "ln1_b", "w_up", "b_up",
           "w_down", "b_down", "ln2_g", "ln2_b"]


def _local_step(x, mem, tgt, full, late, small):
    lam_re, lam_im, log_dt = small["ssm_lam_re"], small["ssm_lam_im"], small["ssm_log_dt"].reshape(SSM_GROUPS)
    c_shape = (SSM_GROUPS, SSM_GROUP, SSM_STATE)
    disc, disc_vjp = jax.vjp(_ssm_discretize, lam_re, lam_im, log_dt,
                             small["ssm_b_re"].reshape(c_shape), small["ssm_b_im"].reshape(c_shape))
    abar_r, abar_i, bbar_r, bbar_i = disc
    pw = _abar_powers(abar_r, abar_i)
    c_re, c_im = small["ssm_c_re"].reshape(c_shape), small["ssm_c_im"].reshape(c_shape)
    b_half = jnp.concatenate([_half_block_diag(bbar_r), _half_block_diag(bbar_i)], axis=2).astype(bf16)
    c_half = jnp.concatenate([_half_block_diag(c_re.transpose(0, 2, 1)), -_half_block_diag(c_im.transpose(0, 2, 1))],
                             axis=1).astype(bf16)

    s_len = x.shape[0]
    stack = lambda a: a.reshape(-1, a.shape[-1])
    kv, k_t, memb = _kv_proj(mem, full["w_kv"])
    (xbt, g, cin, u, q, ain, ob, aint, obt), side = _in_proj(
        x, full["w_in"], small["b_gate"], small["conv_w"], kv,
        [late[k] for k in ("w_glu", "w_conv_out", "w_xattn_out", "w_out", "w_up")])
    w_glu_t, w_co_t, w_xo_t, w_out, w_up_t = (stack(a) for a in side)
    y_ssm, cm_all, side = _ssm_fwd(u, b_half, c_half, pw, small["ssm_d"], [late["w_down"]])
    w_down = stack(side[0])
    ysbt, mb, xhat1, rstd1 = _mid_fwd(y_ssm, g, ain, ob, x, w_glu_t, w_co_t, w_xo_t, w_out,
                                      small["ln1_g"], small["ln1_b"])
    (x1bt, hdn, dr2bt, dpre, dx1, loss, dl2g, dl2b, dbdn, dbup) = _mlp_fwd_bwd(
        xhat1, tgt, small["ln1_g"], small["ln1_b"], w_up_t, small["b_up"], w_down,
        small["b_down"], small["ln2_g"], small["ln2_b"])
    recv = {}
    recv["w_down"] = _weight_grad_scatter(dr2bt, hdn, "dw_down", tm=512, tt=2048)
    recv["w_up"] = _weight_grad_scatter(x1bt, dpre, "dw_up", tm=512, tt=2048)
    (dxp, dr1bt, dgp, dya, dyc, dglu, dyssm, dl1g, dl1b, dbg) = _mid_bwd(
        dx1, xhat1, rstd1, g, ain, ob, y_ssm, small["ln1_g"], w_out, w_glu_t, w_co_t, w_xo_t)
    recv["w_out"] = _weight_grad_scatter(dr1bt, mb, "dw_out", tm=512, tt=s_len)
    recv["w_glu"] = _weight_grad_scatter(ysbt, dglu, "dw_glu", tm=512, tt=s_len)
    du, db_half, dc_half, da8, dd = _ssm_bwd(u, dyssm, cm_all, b_half, c_half, pw, small["ssm_d"])
    dabar_r, dabar_i = _state_unlayout(jnp.sum(da8, axis=0))
    dbbar_r = _half_diag_blocks(db_half[:, :, :HALF_STATE], SSM_GROUP, SSM_STATE)
    dbbar_i = _half_diag_blocks(db_half[:, :, HALF_STATE:], SSM_GROUP, SSM_STATE)
    g_shape = (SSM_GROUPS, SSM_STATE)
    dlam_re, dlam_im, dlog_dt, db_re, db_im = disc_vjp(
        (dabar_r.reshape(g_shape), dabar_i.reshape(g_shape), dbbar_r, dbbar_i))
    dc_re = _half_diag_blocks(dc_half[:, :HALF_STATE, :], SSM_STATE, SSM_GROUP).transpose(0, 2, 1)
    dc_im = -_half_diag_blocks(dc_half[:, HALF_STATE:, :], SSM_STATE, SSM_GROUP).transpose(0, 2, 1)

    small_grads = {
        "b_gate": dbg, "ssm_lam_re": dlam_re, "ssm_lam_im": dlam_im, "ssm_log_dt": dlog_dt,
        "ssm_b_re": db_re, "ssm_b_im": db_im, "ssm_c_re": dc_re, "ssm_c_im": dc_im, "ssm_d": dd,
        "ln1_g": dl1g, "ln1_b": dl1b, "b_up": dbup, "b_down": dbdn, "ln2_g": dl2g, "ln2_b": dl2b,
    }
    small_grads = {k: a.reshape(SMALL[k]) for k, a in small_grads.items()}
    groups = [(group, axis) for group, axis in SMALL_GROUPS if group != ["conv_w"]]
    stacks = [jnp.concatenate([small_grads[k] for k in group], axis=axis) if len(group) > 1 else small_grads[group[0]]
              for group, axis in groups]
    n_rowvec = stacks[0].shape[1]
    stacks[0] = jnp.concatenate([stacks[0], loss], axis=1)
    dense = lambda a: a.reshape(-1, LANES) if a.size % LANES == 0 else a
    dconv, dq, dkv, group_parts = _branch_bwd(dya, dyc, cin, q, kv, k_t, small["conv_w"], w_co_t, w_xo_t,
                                              [dense(a) for a in stacks])
    recv["w_conv_out"] = _weight_grad_scatter(aint, dya, "dw_conv_out", tm=512, tt=s_len)
    recv["w_xattn_out"] = _weight_grad_scatter(obt, dyc, "dw_xattn_out", tm=512, tt=s_len)
    recv["w_kv"] = _weight_grad_scatter(dkv.T.astype(bf16), memb, "dw_kv", tm=D_MODEL, tt=MEM_LEN)
    dx, dproj = _in_proj_bwd(dgp, dconv, du, dq, dxp, full["w_in"])
    recv["w_in"] = _weight_grad_scatter(xbt, dproj, "dw_in", tm=512, tt=2048)
    sums = _sum_small(group_parts)
    group_sums = dict(zip([tuple(group) for group, _ in groups], [s.reshape(a.shape) for s, a in zip(sums, stacks)]))
    group_sums[("conv_w",)] = sums[-1][0:3]
    first = tuple(groups[0][0])
    loss_all = group_sums[first][0, n_rowvec]
    group_sums[first] = group_sums[first][:, :n_rowvec]
    return loss_all, dx, recv, [group_sums[tuple(group)] for group, _ in SMALL_GROUPS]


def kernel(x, mem, w_in, b_gate, conv_w, w_conv_out, ssm_lam_re, ssm_lam_im, ssm_log_dt, ssm_b_re, ssm_b_im, ssm_c_re, ssm_c_im, ssm_d, w_glu, w_kv, w_xattn_out, w_out, ln1_g, ln1_b, w_up, b_up, w_down, b_down, ln2_g, ln2_b, loss_target, m_w_in, m_b_gate, m_conv_w, m_w_conv_out, m_ssm_lam_re, m_ssm_lam_im, m_ssm_log_dt, m_ssm_b_re, m_ssm_b_im, m_ssm_c_re, m_ssm_c_im, m_ssm_d, m_w_glu, m_w_kv, m_w_xattn_out, m_w_out, m_ln1_g, m_ln1_b, m_w_up, m_b_up, m_w_down, m_b_down, m_ln2_g, m_ln2_b, v_w_in, v_b_gate, v_conv_w, v_w_conv_out, v_ssm_lam_re, v_ssm_lam_im, v_ssm_log_dt, v_ssm_b_re, v_ssm_b_im, v_ssm_c_re, v_ssm_c_im, v_ssm_d, v_w_glu, v_w_kv, v_w_xattn_out, v_w_out, v_ln1_g, v_ln1_b, v_w_up, v_b_up, v_w_down, v_b_down, v_ln2_g, v_ln2_b):
    w = dict(w_in=w_in, b_gate=b_gate, conv_w=conv_w, w_conv_out=w_conv_out, ssm_lam_re=ssm_lam_re,
             ssm_lam_im=ssm_lam_im, ssm_log_dt=ssm_log_dt, ssm_b_re=ssm_b_re, ssm_b_im=ssm_b_im, ssm_c_re=ssm_c_re,
             ssm_c_im=ssm_c_im, ssm_d=ssm_d, w_glu=w_glu, w_kv=w_kv, w_xattn_out=w_xattn_out, w_out=w_out,
             ln1_g=ln1_g, ln1_b=ln1_b, w_up=w_up, b_up=b_up, w_down=w_down, b_down=b_down, ln2_g=ln2_g, ln2_b=ln2_b)
    m = dict(w_in=m_w_in, b_gate=m_b_gate, conv_w=m_conv_w, w_conv_out=m_w_conv_out, ssm_lam_re=m_ssm_lam_re,
             ssm_lam_im=m_ssm_lam_im, ssm_log_dt=m_ssm_log_dt, ssm_b_re=m_ssm_b_re, ssm_b_im=m_ssm_b_im,
             ssm_c_re=m_ssm_c_re, ssm_c_im=m_ssm_c_im, ssm_d=m_ssm_d, w_glu=m_w_glu, w_kv=m_w_kv,
             w_xattn_out=m_w_xattn_out, w_out=m_w_out, ln1_g=m_ln1_g, ln1_b=m_ln1_b, w_up=m_w_up, b_up=m_b_up,
             w_down=m_w_down, b_down=m_b_down, ln2_g=m_ln2_g, ln2_b=m_ln2_b)
    v = dict(w_in=v_w_in, b_gate=v_b_gate, conv_w=v_conv_w, w_conv_out=v_w_conv_out, ssm_lam_re=v_ssm_lam_re,
             ssm_lam_im=v_ssm_lam_im, ssm_log_dt=v_ssm_log_dt, ssm_b_re=v_ssm_b_re, ssm_b_im=v_ssm_b_im,
             ssm_c_re=v_ssm_c_re, ssm_c_im=v_ssm_c_im, ssm_d=v_ssm_d, w_glu=v_w_glu, w_kv=v_w_kv,
             w_xattn_out=v_w_xattn_out, w_out=v_w_out, ln1_g=v_ln1_g, ln1_b=v_ln1_b, w_up=v_w_up, b_up=v_b_up,
             w_down=v_w_down, b_down=v_b_down, ln2_g=v_ln2_g, ln2_b=v_ln2_b)
    out_shapes = {k: a.shape for k, a in w.items()}
    swapped = ("w_in", "ssm_b_re", "ssm_b_im")

    def shard2d(k, a):
        if k in swapped:
            a = jnp.swapaxes(a, -1, -2)
        if k in SMALL:
            return a.reshape((3, CONV_W // N_DEV) if k == "conv_w" else SMALL[k])
        return a[0]

    def result(k, a):
        if k in swapped:
            shape = out_shapes[k]
            return jnp.swapaxes(a.reshape(shape[:-2] + (shape[-1], shape[-2])), -1, -2)
        return a.reshape(out_shapes[k])

    w, m, v = ({k: shard2d(k, a) for k, a in d.items()} for d in (w, m, v))

    shards = {k: w[k].T.astype(bf16) if k in GATHER_TRANSPOSED else w[k].astype(bf16) for k in BIG}
    conv_pad = jnp.pad(w["conv_w"], ((0, 5), (0, LANES - CONV_W // N_DEV)))
    early = ["w_in", "w_kv"]
    gathered = _all_gather([shards[k] for k in early] + [conv_pad], "gather_weights")
    full = {k: a.reshape(-1, a.shape[-1]) for k, a in zip(early, gathered[:-1])}
    late = {k: shards[k] for k in BIG if k not in early}
    conv_full = gathered[-1][:, :3, :CONV_W // N_DEV].transpose(1, 0, 2).reshape(3, CONV_W)
    small = {k: (conv_full if k == "conv_w" else w[k]) for k in SMALL}

    loss, dx, recv, group_sums = _local_step(x[0], mem[0], loss_target[0], full, late, small)

    grads, deltas, new_m, new_v = {}, {}, {}, {}
    for k in BIG:
        res = _adamw_update(w[k], m[k], v[k], recv[k], "adamw_" + k, transposed=k in PARTS_TRANSPOSED)
        grads[k], deltas[k], new_m[k], new_v[k] = res

    widen = lambda k, a: jnp.tile(a, (1, N_DEV)) if k == "conv_w" else a
    res = _adamw_small(small, {k: widen(k, m[k]) for k in SMALL}, {k: widen(k, v[k]) for k in SMALL}, group_sums)
    dev = _slot(_mesh_place())
    for d, small_res in zip((grads, deltas, new_m, new_v), res):
        for k, a in small_res.items():
            if k == "conv_w":
                a = lax.dynamic_slice_in_dim(a, dev * (CONV_W // N_DEV), CONV_W // N_DEV, axis=1)
            d[k] = a

    outs = [loss, dx[None]]
    for d in (grads, deltas, new_m, new_v):
        outs += [result(k, d[k]) for k in WEIGHTS]
    return tuple(outs)
```

```python
import functools
import math

import jax
import jax.numpy as jnp
from jax import lax
from jax.experimental import pallas as pl
from jax.experimental.pallas import tpu as pltpu

f32 = jnp.float32
bf16 = jnp.bfloat16

D_MODEL = 1024
MEM_LEN = 256
GATE_COLS = 3 * D_MODEL
CONV_W = 512
SSM_W = 512
XATTN_W = 512
HEADS = 4
HEAD_DIM = 128
D_FF = 4096
IN_COLS = GATE_COLS + 3 * CONV_W + SSM_W + XATTN_W
SSM_GROUPS = 32
SSM_GROUP = 16
SSM_STATE = 64
N_STATE = SSM_GROUPS * SSM_STATE
ALPHA = 2.0 ** 0.25
LN_EPS = 1e-5
N_DEV = 8

ADAM_LR = 0.001
ADAM_B1 = 0.9
ADAM_B2 = 0.999
ADAM_EPS = 1e-08
ADAM_WD = 0.01
ADAM_STEP = 10

VMEM_LIMIT_V7X = 56 * 2 ** 20
SUBLANES = 8
LANES = 128

TOKEN_TILE = 256
SSM_BLOCK = 512
SSM_SEG = SSM_BLOCK // SUBLANES
LANE_CHUNK = 256
N_HALF = 2
HALF_W = SSM_W // N_HALF
HALF_STATE = N_STATE // N_HALF
HALF_COLS = 2 * HALF_STATE

NT = (((1,), (1,)), ((), ()))
TN = (((0,), (0,)), ((), ()))
NN = (((1,), (0,)), ((), ()))


def _dot(a, b, dims=NN):
    return lax.dot_general(a, b, dims, preferred_element_type=f32)


def _cparams(sem=None):
    return pltpu.CompilerParams(dimension_semantics=sem, vmem_limit_bytes=VMEM_LIMIT_V7X)


def _row_spec(tm, cols, rev_n=None):
    if rev_n is None:
        return pl.BlockSpec((tm, cols), lambda i: (i, 0))
    return pl.BlockSpec((tm, cols), lambda i: (rev_n - 1 - i, 0))


def _col_spec(rows, tm):
    return pl.BlockSpec((rows, tm), lambda i: (0, i))


def _const_spec(shape):
    nd = len(shape)
    return pl.BlockSpec(shape, lambda *_: (0,) * nd, pipeline_mode=pl.Buffered(1))


def _acc_spec(shape):
    nd = len(shape)
    return pl.BlockSpec(shape, lambda *_: (0,) * nd)


def _sds(shape, dtype):
    return jax.ShapeDtypeStruct(shape, dtype)


def _gelu(x):
    c = math.sqrt(2.0 / math.pi)
    return 0.5 * x * (1.0 + jnp.tanh(c * (x + 0.044715 * x * x * x)))


def _gelu_grad(x):
    c = math.sqrt(2.0 / math.pi)
    t = jnp.tanh(c * (x + 0.044715 * x * x * x))
    return 0.5 * (1.0 + t) + 0.5 * x * (1.0 - t * t) * c * (1.0 + 3.0 * 0.044715 * x * x)


def _colsum(a):
    return jnp.sum(a, axis=0, keepdims=True)


def _mesh_place():
    return lax.axis_index("x"), lax.axis_index("y"), lax.axis_index("c")


def _slot(p):
    return 4 * p[0] + 2 * p[1] + p[2]


def _other_devices(me):
    x, y, c = me
    flip = lambda v, d: 1 - v if d else v
    return [(flip(x, dx), flip(y, dy), flip(c, dc)) for dx in (0, 1) for dy in (0, 1) for dc in (0, 1)][1:]


def _all_gather(blocks, name):
    n = len(blocks)

    def body(*refs):
        ins, outs = refs[:n], refs[n:2 * n]
        send_sems, recv_sems, local_sems = refs[2 * n:]
        x, y, c = _mesh_place()
        me, sibling = (x, y, c), (x, y, 1 - c)
        chips = [(1 - x, y), (x, 1 - y), (1 - x, 1 - y)]

        def copy(a, k, block, to, src=None):
            rows = outs[a].at[_slot(block)]
            return pltpu.make_async_remote_copy(
                src_ref=rows if src is None else src, dst_ref=rows,
                send_sem=send_sems.at[a, k], recv_sem=recv_sems.at[a, k],
                device_id=to, device_id_type=pl.DeviceIdType.MESH)

        mine = [pltpu.make_async_copy(ins[a], outs[a].at[_slot(me)], local_sems.at[a]) for a in range(n)]
        for cp in mine:
            cp.start()
        first = []
        for a in range(n):
            first.append(copy(a, 0, me, sibling, src=ins[a]))
            first += [copy(a, 1 + j, me, (*chip, c), src=ins[a]) for j, chip in enumerate(chips)]
        for cp in first:
            cp.start()
        passed = []
        for a in range(n):
            for j, chip in enumerate(chips):
                copy(a, 1 + j, (*chip, c), me).wait_recv()
                fwd = copy(a, 4 + j, (*chip, c), sibling)
                fwd.start()
                passed.append(fwd)
        for a in range(n):
            copy(a, 0, sibling, me).wait_recv()
            for j, chip in enumerate(chips):
                copy(a, 4 + j, (*chip, 1 - c), me).wait_recv()
        for cp in first + passed:
            cp.wait_send()
        for cp in mine:
            cp.wait()

    any_spec = pl.BlockSpec(memory_space=pl.ANY)
    return pl.pallas_call(
        body, name=name,
        out_shape=[_sds((N_DEV,) + b.shape, b.dtype) for b in blocks],
        in_specs=[any_spec] * n, out_specs=[any_spec] * n,
        scratch_shapes=[pltpu.SemaphoreType.DMA((n, 7)), pltpu.SemaphoreType.DMA((n, 7)),
                        pltpu.SemaphoreType.DMA((n,))],
    )(*blocks)


def _side_gather_copies(ins, outs, send_sems, recv_sems, local_sems):
    me = _mesh_place()
    copies = []
    for a, (src, dst) in enumerate(zip(ins, outs)):
        copies.append(pltpu.make_async_copy(src, dst.at[_slot(me)], local_sems.at[a]))
        for k, peer in enumerate(_other_devices(me)):
            copies.append(pltpu.make_async_remote_copy(
                src_ref=src, dst_ref=dst.at[_slot(me)], send_sem=send_sems.at[a, k], recv_sem=recv_sems.at[a, k],
                device_id=peer, device_id_type=pl.DeviceIdType.MESH))
    return copies


def _side_gather_two_level(ins, outs, send_sems, recv_sems, local_sems):
    x, y, c = _mesh_place()
    me, sibling = (x, y, c), (x, y, 1 - c)
    chips = [(1 - x, y), (x, 1 - y), (1 - x, 1 - y)]

    def copy(a, k, block, to, src=None):
        rows = outs[a].at[_slot(block)]
        return pltpu.make_async_remote_copy(
            src_ref=rows if src is None else src, dst_ref=rows, send_sem=send_sems.at[a, k], recv_sem=recv_sems.at[a, k],
            device_id=to, device_id_type=pl.DeviceIdType.MESH)

    n = len(ins)
    mine = [pltpu.make_async_copy(ins[a], outs[a].at[_slot(me)], local_sems.at[a]) for a in range(n)]
    first = [copy(a, 0, me, sibling, src=ins[a]) for a in range(n)]
    first += [copy(a, 1 + j, me, (*chip, c), src=ins[a]) for a in range(n) for j, chip in enumerate(chips)]
    passed = [copy(a, 4 + j, (*chip, c), sibling) for a in range(n) for j, chip in enumerate(chips)]

    def start():
        for cp in mine + first:
            cp.start()

    def forward():
        for a in range(n):
            for j, chip in enumerate(chips):
                copy(a, 1 + j, (*chip, c), me).wait_recv()
        for cp in passed:
            cp.start()

    def finish():
        for a in range(n):
            copy(a, 0, sibling, me).wait_recv()
            for j, chip in enumerate(chips):
                copy(a, 4 + j, (*chip, 1 - c), me).wait_recv()
        for cp in first + passed:
            cp.wait_send()
        for cp in mine:
            cp.wait()

    return start, forward, finish


def _side_gather_specs(blocks):
    n = len(blocks)
    any_spec = pl.BlockSpec(memory_space=pl.ANY)
    return ([any_spec] * n, [_sds((N_DEV,) + b.shape, b.dtype) for b in blocks],
            [pltpu.SemaphoreType.DMA((n, N_DEV - 1)), pltpu.SemaphoreType.DMA((n, N_DEV - 1)),
             pltpu.SemaphoreType.DMA((n,))])


def _kv_proj(mem, w_kv):
    def body(mem_ref, w_ref, kv_ref, kt_ref, memb_ref):
        mb = mem_ref[...].astype(bf16)
        memb_ref[...] = mb
        kv = _dot(mb, w_ref[...]).astype(bf16)
        kv_ref[...] = kv
        kt_ref[...] = kv[:, :XATTN_W].T

    return pl.pallas_call(
        body, name="kv_proj",
        out_shape=[_sds((MEM_LEN, 2 * XATTN_W), bf16), _sds((XATTN_W, MEM_LEN), bf16), _sds((MEM_LEN, D_MODEL), bf16)],
        compiler_params=_cparams(),
    )(mem, w_kv)


def _attention_probs(qb, kv_ref, h):
    kh = kv_ref[:, h * HEAD_DIM:(h + 1) * HEAD_DIM]
    s = _dot(qb[:, h * HEAD_DIM:(h + 1) * HEAD_DIM], kh, NT) * (HEAD_DIM ** -0.5)
    e = jnp.exp(s - jnp.max(s, axis=-1, keepdims=True))
    return e / jnp.sum(e, axis=-1, keepdims=True)


def _in_proj(x, w_in_t, b_gate, conv_w, kv, side_blocks):
    s_len = x.shape[0]
    tm = 2 * TOKEN_TILE
    n = s_len // tm
    ns = len(side_blocks)
    side_in_specs, side_shapes, side_sems = _side_gather_specs(side_blocks)

    def body(*refs):
        (x_ref, win_ref, bg_ref, cw_ref, kv_ref) = refs[:5]
        side_ins = refs[5:5 + ns]
        (xbt_ref, g_ref, cin_ref, u_ref, q_ref, ain_ref, o_ref, aint_ref, ot_ref) = refs[5 + ns:14 + ns]
        side_outs = refs[14 + ns:14 + 2 * ns]
        zs_ref = refs[14 + 2 * ns]
        side_start, side_forward, side_finish = _side_gather_two_level(side_ins, side_outs, *refs[15 + 2 * ns:])
        i = pl.program_id(0)
        pl.when(i == 0)(side_start)
        pl.when(i == (3 * n) // 4)(side_forward)

        xb = x_ref[...].astype(bf16)
        xbt_ref[...] = xb.T
        proj = _dot(xb, win_ref[...], NT)
        g_ref[...] = jax.nn.sigmoid(proj[:, :GATE_COLS] + bg_ref[...]).astype(bf16)
        cin = proj[:, GATE_COLS:GATE_COLS + 3 * CONV_W]
        cin_ref[...] = cin
        u_ref[...] = proj[:, GATE_COLS + 3 * CONV_W:GATE_COLS + 3 * CONV_W + SSM_W]
        qb = proj[:, IN_COLS - XATTN_W:].astype(bf16)
        q_ref[...] = qb

        cb, cc, ch = cin[:, :CONV_W], cin[:, CONV_W:2 * CONV_W], cin[:, 2 * CONV_W:]
        z = cc * ch

        @pl.when(i == 0)
        def _():
            zs_ref[0:8, :] = jnp.zeros((8, CONV_W), f32)

        zs_ref[8:8 + tm, :] = z
        z1 = zs_ref[pl.ds(7, tm), :]
        z2 = zs_ref[pl.ds(6, tm), :]
        cw = cw_ref[...]
        cz = cw[0:1] * z2 + cw[1:2] * z1 + cw[2:3] * z
        zs_ref[0:8, :] = zs_ref[tm:tm + 8, :]
        ain = (cb * cz).astype(bf16)
        ain_ref[...] = ain
        aint_ref[...] = ain.T

        probs = [_attention_probs(qb, kv_ref, h) for h in range(HEADS)]
        outs = [_dot(probs[h].astype(bf16), kv_ref[:, XATTN_W + h * HEAD_DIM:XATTN_W + (h + 1) * HEAD_DIM])
                for h in range(HEADS)]
        ob = jnp.concatenate(outs, axis=1).astype(bf16)
        o_ref[...] = ob
        ot_ref[...] = ob.T

        pl.when(i == n - 1)(side_finish)

    row_cols = [(GATE_COLS, bf16), (3 * CONV_W, f32), (SSM_W, f32), (XATTN_W, bf16), (CONV_W, bf16), (XATTN_W, bf16)]
    t_rows = [D_MODEL, CONV_W, XATTN_W]
    outs = pl.pallas_call(
        body, name="in_proj", grid=(n,),
        in_specs=[_row_spec(tm, D_MODEL), _const_spec((IN_COLS, D_MODEL)), _const_spec((1, GATE_COLS)),
                  _const_spec((3, CONV_W)), _const_spec((MEM_LEN, 2 * XATTN_W))] + side_in_specs,
        out_specs=([_col_spec(t_rows[0], tm)] + [_row_spec(tm, c) for c, _ in row_cols]
                   + [_col_spec(t_rows[1], tm), _col_spec(t_rows[2], tm)] + side_in_specs),
        out_shape=([_sds((t_rows[0], s_len), bf16)] + [_sds((s_len, c), dt) for c, dt in row_cols]
                   + [_sds((t_rows[1], s_len), bf16), _sds((t_rows[2], s_len), bf16)] + side_shapes),
        scratch_shapes=[pltpu.VMEM((tm + 8, CONV_W), f32)] + side_sems,
        compiler_params=_cparams(("arbitrary",)),
    )(x, w_in_t, b_gate, conv_w, kv, *side_blocks)
    return outs[:9], outs[9:]


def _state_cols(chunk, width=LANE_CHUNK):
    half, off = divmod(chunk * width, HALF_STATE)
    lo = half * HALF_COLS + off
    return slice(lo, lo + width), slice(lo + HALF_STATE, lo + HALF_STATE + width)


def _half_cols(half):
    lo = half * HALF_COLS
    return slice(lo, lo + HALF_STATE), slice(lo + HALF_STATE, lo + HALF_COLS)


def _rows_to_segments(src_ref, stage_ref, dst_ref):
    nc = SSM_W // LANES
    for c in range(nc):
        stage_ref[c] = src_ref[:, c * LANES:(c + 1) * LANES]
    for c in range(nc):
        for k in range(SSM_SEG):
            dst_ref[k * SUBLANES:(k + 1) * SUBLANES, c * LANES:(c + 1) * LANES] = (
                stage_ref[c, pl.ds(k, SUBLANES, stride=SSM_SEG), :])


def _rows_from_segments(src_ref, stage_ref, dst_ref):
    nc = SSM_W // LANES
    for c in range(nc):
        for k in range(SSM_SEG):
            stage_ref[c, pl.ds(k, SUBLANES, stride=SSM_SEG), :] = (
                src_ref[k * SUBLANES:(k + 1) * SUBLANES, c * LANES:(c + 1) * LANES])
    for c in range(nc):
        dst_ref[:, c * LANES:(c + 1) * LANES] = stage_ref[c]


def _ssm_scan(s_ref, pw_ref, init_ref, reverse, unroll, width=LANE_CHUNK):
    for chunk in range(N_STATE // width):
        re, im = _state_cols(chunk, width)
        ar = jnp.broadcast_to(pw_ref[0:1, re], (SUBLANES, width))
        ai = jnp.broadcast_to(pw_ref[0:1, im], (SUBLANES, width))
        if reverse:
            ai = -ai

        def step(j, carry, re=re, im=im, ar=ar, ai=ai):
            sr, si = carry
            k = (SSM_SEG - 1 - j) if reverse else j
            r0 = pl.multiple_of(k * SUBLANES, SUBLANES)
            nr = ar * sr - ai * si + s_ref[pl.ds(r0, SUBLANES), re]
            ni = ar * si + ai * sr + s_ref[pl.ds(r0, SUBLANES), im]
            s_ref[pl.ds(r0, SUBLANES), re] = nr
            s_ref[pl.ds(r0, SUBLANES), im] = ni
            return nr, ni

        if init_ref is None:
            init = (jnp.zeros((SUBLANES, width), f32),) * 2
        else:
            init = (init_ref[:, re], init_ref[:, im])
        lax.fori_loop(0, SSM_SEG, step, init, unroll=unroll)


def _ssm_add_carry(s_ref, pw_ref, cm_ref, reverse):
    for chunk in range(N_STATE // LANE_CHUNK):
        re, im = _state_cols(chunk)
        cr, ci = cm_ref[:, re], cm_ref[:, im]
        for k in range(SSM_SEG):
            pk = (SSM_SEG - 1 - k) if reverse else k
            pr = pw_ref[pk:pk + 1, re]
            pi = pw_ref[pk:pk + 1, im]
            if reverse:
                pi = -pi
            rows = slice(k * SUBLANES, (k + 1) * SUBLANES)
            s_ref[rows, re] = s_ref[rows, re] + (pr * cr - pi * ci)
            s_ref[rows, im] = s_ref[rows, im] + (pr * ci + pi * cr)


def _ssm_carries(first_row, s_ref, pw_ref, carry_ref, cm_ref, reverse):
    order = range(SUBLANES - 1, -1, -1) if reverse else range(SUBLANES)
    for half in range(N_HALF):
        re, im = _half_cols(half)
        a_r, a_i = pw_ref[SSM_SEG - 1:SSM_SEG, re], pw_ref[SSM_SEG - 1:SSM_SEG, im]
        if reverse:
            a_i = -a_i
        cr, ci = carry_ref[0:1, re], carry_ref[0:1, im]
        for seg in order:
            cm_ref[seg:seg + 1, re] = cr
            cm_ref[seg:seg + 1, im] = ci
            er = s_ref[first_row + seg:first_row + seg + 1, re]
            ei = s_ref[first_row + seg:first_row + seg + 1, im]
            cr, ci = a_r * cr - a_i * ci + er, a_r * ci + a_i * cr + ei
        carry_ref[0:1, re] = cr
        carry_ref[0:1, im] = ci


def _ssm_fwd(u, b_half, c_half, pw, d_skip, side_blocks):
    s_len = u.shape[0]
    tb = SSM_BLOCK
    n = s_len // tb
    ns = len(side_blocks)
    side_in_specs, side_shapes, side_sems = _side_gather_specs(side_blocks)

    def body(*refs):
        u_ref, b_ref, c_ref, pw_ref, d_ref = refs[:5]
        side_ins = refs[5:5 + ns]
        y_ref, cm_ref = refs[5 + ns:7 + ns]
        side_outs = refs[7 + ns:7 + 2 * ns]
        s_ref, carry_ref, up_ref, yp_ref, stage_ref = refs[7 + 2 * ns:12 + 2 * ns]
        side_start, side_forward, side_finish = _side_gather_two_level(side_ins, side_outs, *refs[12 + 2 * ns:])
        i = pl.program_id(0)

        @pl.when(i == 0)
        def _():
            carry_ref[...] = jnp.zeros_like(carry_ref)
            side_start()

        pl.when(i == (3 * n) // 4)(side_forward)
        _rows_to_segments(u_ref, stage_ref, up_ref)
        u = up_ref[...]
        ub = u.astype(bf16)
        for half in range(N_HALF):
            s_ref[:, half * HALF_COLS:(half + 1) * HALF_COLS] = _dot(ub[:, half * HALF_W:(half + 1) * HALF_W], b_ref[half])
        _ssm_scan(s_ref, pw_ref, None, reverse=False, unroll=4, width=2 * LANE_CHUNK)
        _ssm_carries(tb - SUBLANES, s_ref, pw_ref, carry_ref, cm_ref, reverse=False)
        _ssm_add_carry(s_ref, pw_ref, cm_ref, reverse=False)
        for half in range(N_HALF):
            cols = slice(half * HALF_W, (half + 1) * HALF_W)
            sb = s_ref[:, half * HALF_COLS:(half + 1) * HALF_COLS].astype(bf16)
            yp_ref[:, cols] = _dot(sb, c_ref[half]) + d_ref[:, cols] * u[:, cols]
        _rows_from_segments(yp_ref, stage_ref, y_ref)

        pl.when(i == n - 1)(side_finish)

    outs = pl.pallas_call(
        body, name="ssm_fwd", grid=(n,),
        in_specs=[_row_spec(tb, SSM_W), _const_spec((N_HALF, HALF_W, HALF_COLS)), _const_spec((N_HALF, HALF_COLS, HALF_W)),
                  _const_spec((SSM_SEG, 2 * N_STATE)), _const_spec((1, SSM_W))] + side_in_specs,
        out_specs=[_row_spec(tb, SSM_W), _row_spec(SUBLANES, 2 * N_STATE)] + side_in_specs,
        out_shape=[_sds((s_len, SSM_W), f32), _sds((n * SUBLANES, 2 * N_STATE), f32)] + side_shapes,
        scratch_shapes=[pltpu.VMEM((tb, 2 * N_STATE), f32), pltpu.VMEM((SUBLANES, 2 * N_STATE), f32),
                        pltpu.VMEM((tb, SSM_W), f32), pltpu.VMEM((tb, SSM_W), f32),
                        pltpu.VMEM((SSM_W // LANES, tb, LANES), f32)] + side_sems,
        compiler_params=_cparams(("arbitrary",)),
    )(u, b_half, c_half, pw, d_skip, *side_blocks)
    return outs[0], outs[1], outs[2:]


def _layer_norm_fwd(r, g, b):
    mu = jnp.mean(r, axis=-1, keepdims=True)
    var = jnp.mean(jnp.square(r - mu), axis=-1, keepdims=True)
    rstd = lax.rsqrt(var + LN_EPS)
    xhat = (r - mu) * rstd
    return xhat, rstd, xhat * g + b


def _layer_norm_bwd(dy, xhat, rstd, g):
    dxh = dy * g
    m1 = jnp.mean(dxh, axis=-1, keepdims=True)
    m2 = jnp.mean(dxh * xhat, axis=-1, keepdims=True)
    return rstd * (dxh - m1 - xhat * m2)


def _branch_outputs(ys_ref, ain_ref, o_ref, wglu_ref, wco_ref, wxo_ref):
    ysb = _gelu(ys_ref[...]).astype(bf16)
    glu = _dot(ysb, wglu_ref[...], NT)
    ga, sb = glu[:, :D_MODEL], jax.nn.sigmoid(glu[:, D_MODEL:])
    ya = _dot(ain_ref[...], wco_ref[...], NT)
    yc = _dot(o_ref[...], wxo_ref[...], NT)
    return ysb, ga, sb, ya, ga * sb, yc


def _mid_fwd(y_ssm, g, ain, ob, x, w_glu_t, w_co_t, w_xo_t, w_out, ln1_g, ln1_b):
    s_len = x.shape[0]
    tm = TOKEN_TILE
    n = s_len // tm

    def body(ys_ref, g_ref, ain_ref, o_ref, x_ref, wglu_ref, wco_ref, wxo_ref, wout_ref, lg_ref, lb_ref,
             ysbt_ref, mb_ref, xhat_ref, rstd_ref):
        ysb, _, _, ya, yb, yc = _branch_outputs(ys_ref, ain_ref, o_ref, wglu_ref, wco_ref, wxo_ref)
        ysbt_ref[...] = ysb.T
        gt = g_ref[...].astype(f32)
        merged = gt[:, :D_MODEL] * ya + gt[:, D_MODEL:2 * D_MODEL] * yb + gt[:, 2 * D_MODEL:] * yc
        mb = merged.astype(bf16)
        mb_ref[...] = mb
        r1 = ALPHA * x_ref[...] + _dot(mb, wout_ref[...])
        xhat, rstd, _ = _layer_norm_fwd(r1, lg_ref[...], lb_ref[...])
        xhat_ref[...] = xhat
        rstd_ref[...] = rstd

    row_cols = [(D_MODEL, bf16), (D_MODEL, f32), (1, f32)]
    return pl.pallas_call(
        body, name="mid_fwd", grid=(n,),
        in_specs=[_row_spec(tm, SSM_W), _row_spec(tm, GATE_COLS), _row_spec(tm, CONV_W), _row_spec(tm, XATTN_W),
                  _row_spec(tm, D_MODEL), _const_spec((2 * D_MODEL, SSM_W)), _const_spec((D_MODEL, CONV_W)),
                  _const_spec((D_MODEL, XATTN_W)), _const_spec((D_MODEL, D_MODEL)),
                  _const_spec((1, D_MODEL)), _const_spec((1, D_MODEL))],
        out_specs=[_col_spec(SSM_W, tm)] + [_row_spec(tm, c) for c, _ in row_cols],
        out_shape=[_sds((SSM_W, s_len), bf16)] + [_sds((s_len, c), dt) for c, dt in row_cols],
        compiler_params=_cparams(("parallel",)),
    )(y_ssm, g, ain, ob, x, w_glu_t, w_co_t, w_xo_t, w_out, ln1_g, ln1_b)


def _mlp_fwd_bwd(xhat1, tgt, ln1_g, ln1_b, w_up_t, b_up, w_down, b_down, ln2_g, ln2_b):
    s_len = xhat1.shape[0]
    tm = TOKEN_TILE
    n = s_len // tm
    fc = 1024
    nfc = D_FF // fc

    def body(xh_ref, t_ref, l1g_ref, l1b_ref, wup_ref, bup_ref, wdn_ref, bdn_ref, l2g_ref, l2b_ref,
             x1bt_ref, hdn_ref, dr2bt_ref, dpre_ref, dx1_ref,
             loss_ref, dl2g_ref, dl2b_ref, dbdn_ref, dbup_ref, rl_ref):
        i = pl.program_id(0)

        @pl.when(i == 0)
        def _():
            loss_ref[...] = jnp.zeros_like(loss_ref)
            dl2g_ref[...] = jnp.zeros_like(dl2g_ref)
            dl2b_ref[...] = jnp.zeros_like(dl2b_ref)
            dbdn_ref[...] = jnp.zeros_like(dbdn_ref)
            dbup_ref[...] = jnp.zeros_like(dbup_ref)

        x1 = xh_ref[...] * l1g_ref[...] + l1b_ref[...]
        x1b = x1.astype(bf16)
        x1bt_ref[...] = x1b.T
        chunks = [slice(c * fc, (c + 1) * fc) for c in range(nfc)]
        pres = [_dot(x1b, wup_ref[cols, :], NT) for cols in chunks]
        hbs = []
        for cols, pre in zip(chunks, pres):
            rl = jnp.maximum(pre + bup_ref[:, cols], 0.0)
            rl_ref[:, cols] = rl
            hb = (rl * rl).astype(bf16)
            hdn_ref[:, cols] = hb
            hbs.append(hb)
        acc = _dot(hbs[0], wdn_ref[chunks[0], :])
        for cols, hb in zip(chunks[1:], hbs[1:]):
            acc = acc + _dot(hb, wdn_ref[cols, :])
        r2 = ALPHA * x1 + acc + bdn_ref[...]
        xhat2, rstd2, y = _layer_norm_fwd(r2, l2g_ref[...], l2b_ref[...])
        err = y - t_ref[...]
        loss_ref[...] += jnp.sum(jnp.sum(err * err, axis=1, keepdims=True), axis=0, keepdims=True) * (0.5 / D_MODEL)
        dy = err * (1.0 / D_MODEL)
        dl2g_ref[...] += _colsum(dy * xhat2)
        dl2b_ref[...] += _colsum(dy)
        dr2 = _layer_norm_bwd(dy, xhat2, rstd2, l2g_ref[...])
        dbdn_ref[...] += _colsum(dr2)
        dr2b = dr2.astype(bf16)
        dr2bt_ref[...] = dr2b.T
        dhs = [_dot(dr2b, wdn_ref[cols, :], NT) for cols in chunks]
        dpbs = []
        for cols, dh in zip(chunks, dhs):
            dpre = dh * (2.0 * rl_ref[:, cols])
            dbup_ref[:, cols] += _colsum(dpre)
            dpb = dpre.astype(bf16)
            dpre_ref[:, cols] = dpb
            dpbs.append(dpb)
        dacc = _dot(dpbs[0], wup_ref[chunks[0], :])
        for cols, dpb in zip(chunks[1:], dpbs[1:]):
            dacc = dacc + _dot(dpb, wup_ref[cols, :])
        dx1_ref[...] = ALPHA * dr2 + dacc

    acc_shapes = [(1, LANES), (1, D_MODEL), (1, D_MODEL), (1, D_MODEL), (1, D_FF)]
    return pl.pallas_call(
        body, name="mlp_fwd_bwd", grid=(n,),
        in_specs=[_row_spec(tm, D_MODEL), _row_spec(tm, D_MODEL), _const_spec((1, D_MODEL)), _const_spec((1, D_MODEL)),
                  _const_spec((D_FF, D_MODEL)), _const_spec((1, D_FF)), _const_spec((D_FF, D_MODEL)),
                  _const_spec((1, D_MODEL)), _const_spec((1, D_MODEL)), _const_spec((1, D_MODEL))],
        out_specs=([_col_spec(D_MODEL, tm), _row_spec(tm, D_FF), _col_spec(D_MODEL, tm), _row_spec(tm, D_FF),
                    _row_spec(tm, D_MODEL)] + [_acc_spec(s) for s in acc_shapes]),
        out_shape=([_sds((D_MODEL, s_len), bf16), _sds((s_len, D_FF), bf16), _sds((D_MODEL, s_len), bf16),
                    _sds((s_len, D_FF), bf16), _sds((s_len, D_MODEL), f32)] + [_sds(s, f32) for s in acc_shapes]),
        scratch_shapes=[pltpu.VMEM((tm, D_FF), f32)],
        compiler_params=_cparams(("arbitrary",)),
    )(xhat1, tgt, ln1_g, ln1_b, w_up_t, b_up, w_down, b_down, ln2_g, ln2_b)


def _mid_bwd(dx1, xhat1, rstd1, g, ain, ob, y_ssm, ln1_g, w_out, w_glu_t, w_co_t, w_xo_t):
    s_len = dx1.shape[0]
    tm = TOKEN_TILE
    n = s_len // tm

    def body(dx1_ref, xh_ref, rs_ref, g_ref, ain_ref, o_ref, ys_ref, lg_ref, wout_ref, wglu_ref, wco_ref, wxo_ref,
             dxp_ref, dr1bt_ref, dgp_ref, dya_ref, dyc_ref, dglu_ref, dyssm_ref,
             dl1g_ref, dl1b_ref, dbg_ref):
        i = pl.program_id(0)

        @pl.when(i == 0)
        def _():
            dl1g_ref[...] = jnp.zeros_like(dl1g_ref)
            dl1b_ref[...] = jnp.zeros_like(dl1b_ref)
            dbg_ref[...] = jnp.zeros_like(dbg_ref)

        dx1 = dx1_ref[...]
        xhat = xh_ref[...]
        dl1g_ref[...] += _colsum(dx1 * xhat)
        dl1b_ref[...] += _colsum(dx1)
        dr1 = _layer_norm_bwd(dx1, xhat, rs_ref[...], lg_ref[...])
        dxp_ref[...] = ALPHA * dr1
        dr1b = dr1.astype(bf16)
        dr1bt_ref[...] = dr1b.T
        dm = _dot(dr1b, wout_ref[...], NT)

        _, ga, sb, ya, yb, yc = _branch_outputs(ys_ref, ain_ref, o_ref, wglu_ref, wco_ref, wxo_ref)
        gt = g_ref[...].astype(f32)
        branch = (ya, yb, yc)
        for j in range(3):
            cols = slice(j * D_MODEL, (j + 1) * D_MODEL)
            gj = gt[:, cols]
            dgp = dm * branch[j] * gj * (1.0 - gj)
            dbg_ref[:, cols] += _colsum(dgp)
            dgp_ref[:, cols] = dgp.astype(bf16)
        dya_ref[...] = (dm * gt[:, :D_MODEL]).astype(bf16)
        dyc_ref[...] = (dm * gt[:, 2 * D_MODEL:]).astype(bf16)
        dyb = dm * gt[:, D_MODEL:2 * D_MODEL]
        dga = (dyb * sb).astype(bf16)
        dgb = (dyb * ga * sb * (1.0 - sb)).astype(bf16)
        dglu_ref[:, :D_MODEL] = dga
        dglu_ref[:, D_MODEL:] = dgb
        dys = _dot(dga, wglu_ref[:D_MODEL, :]) + _dot(dgb, wglu_ref[D_MODEL:, :])
        dyssm_ref[...] = dys * _gelu_grad(ys_ref[...])

    row_cols = [(GATE_COLS, bf16), (D_MODEL, bf16), (D_MODEL, bf16), (2 * D_MODEL, bf16), (SSM_W, f32)]
    acc_shapes = [(1, D_MODEL), (1, D_MODEL), (1, GATE_COLS)]
    return pl.pallas_call(
        body, name="mid_bwd", grid=(n,),
        in_specs=[_row_spec(tm, D_MODEL), _row_spec(tm, D_MODEL), _row_spec(tm, 1), _row_spec(tm, GATE_COLS),
                  _row_spec(tm, CONV_W), _row_spec(tm, XATTN_W), _row_spec(tm, SSM_W),
                  _const_spec((1, D_MODEL)), _const_spec((D_MODEL, D_MODEL)), _const_spec((2 * D_MODEL, SSM_W)),
                  _const_spec((D_MODEL, CONV_W)), _const_spec((D_MODEL, XATTN_W))],
        out_specs=([_row_spec(tm, D_MODEL), _col_spec(D_MODEL, tm)] + [_row_spec(tm, c) for c, _ in row_cols]
                   + [_acc_spec(s) for s in acc_shapes]),
        out_shape=([_sds((s_len, D_MODEL), f32), _sds((D_MODEL, s_len), bf16)]
                   + [_sds((s_len, c), dt) for c, dt in row_cols] + [_sds(s, f32) for s in acc_shapes]),
        compiler_params=_cparams(("arbitrary",)),
    )(dx1, xhat1, rstd1, g, ain, ob, y_ssm, ln1_g, w_out, w_glu_t, w_co_t, w_xo_t)


def _ssm_bwd(u, dy, cm_all, b_half, c_half, pw, d_skip):
    s_len = u.shape[0]
    tb = SSM_BLOCK
    n = s_len // tb

    def body(u_ref, dy_ref, cm_ref, b_ref, c_ref, pw_ref, d_ref,
             du_ref, db_hbm, dc_hbm, da_ref, dd_ref,
             s_ref, g_ref, gcarry_ref, gcm_ref, db_ref, dc_ref, up_ref, dyp_ref, dup_ref, stage_ref):
        i = pl.program_id(0)

        @pl.when(i == 0)
        def _():
            gcarry_ref[...] = jnp.zeros_like(gcarry_ref)
            db_ref[...] = jnp.zeros_like(db_ref)
            dc_ref[...] = jnp.zeros_like(dc_ref)
            da_ref[...] = jnp.zeros_like(da_ref)
            dd_ref[...] = jnp.zeros_like(dd_ref)

        _rows_to_segments(u_ref, stage_ref, up_ref)
        _rows_to_segments(dy_ref, stage_ref, dyp_ref)
        u = up_ref[...]
        ub = u.astype(bf16)
        dy = dyp_ref[...]
        dyb = dy.astype(bf16)
        dd_ref[...] += _colsum(dy * u)

        for half in range(N_HALF):
            s_ref[:, half * HALF_COLS:(half + 1) * HALF_COLS] = _dot(ub[:, half * HALF_W:(half + 1) * HALF_W], b_ref[half])
        _ssm_scan(s_ref, pw_ref, cm_ref, reverse=False, unroll=True)

        for half in range(N_HALF):
            g_ref[:, half * HALF_COLS:(half + 1) * HALF_COLS] = _dot(dyb[:, half * HALF_W:(half + 1) * HALF_W], c_ref[half], NT)
        _ssm_scan(g_ref, pw_ref, None, reverse=True, unroll=True)
        _ssm_carries(0, g_ref, pw_ref, gcarry_ref, gcm_ref, reverse=True)
        _ssm_add_carry(g_ref, pw_ref, gcm_ref, reverse=True)

        for half in range(N_HALF):
            cols = slice(half * HALF_W, (half + 1) * HALF_W)
            scols = slice(half * HALF_COLS, (half + 1) * HALF_COLS)
            gb = g_ref[:, scols].astype(bf16)
            dup_ref[:, cols] = _dot(gb, b_ref[half], NT) + d_ref[:, cols] * dy[:, cols]
            db_ref[half] += _dot(ub[:, cols], gb, TN)
            dc_ref[half] += _dot(s_ref[:, scols].astype(bf16), dyb[:, cols], TN)
        _rows_from_segments(dup_ref, stage_ref, du_ref)

        for chunk in range(N_STATE // LANE_CHUNK):
            re, im = _state_cols(chunk)
            acc_r = da_ref[:, re]
            acc_i = da_ref[:, im]
            for k in range(SSM_SEG):
                rows = slice(k * SUBLANES, (k + 1) * SUBLANES)
                if k == 0:
                    pr, pi = cm_ref[:, re], cm_ref[:, im]
                else:
                    prev = slice((k - 1) * SUBLANES, k * SUBLANES)
                    pr, pi = s_ref[prev, re], s_ref[prev, im]
                gr, gi = g_ref[rows, re], g_ref[rows, im]
                acc_r = acc_r + (gr * pr + gi * pi)
                acc_i = acc_i + (gi * pr - gr * pi)
            da_ref[:, re] = acc_r
            da_ref[:, im] = acc_i

        @pl.when(i == n - 1)
        def _():
            pltpu.sync_copy(db_ref, db_hbm)
            pltpu.sync_copy(dc_ref, dc_hbm)

    rev = functools.partial(_row_spec, rev_n=n)
    any_spec = pl.BlockSpec(memory_space=pl.ANY)
    state_rows = pltpu.VMEM((tb, 2 * N_STATE), f32)
    seg_rows = pltpu.VMEM((SUBLANES, 2 * N_STATE), f32)
    tok_rows = pltpu.VMEM((tb, SSM_W), f32)
    return pl.pallas_call(
        body, name="ssm_bwd", grid=(n,),
        in_specs=[rev(tb, SSM_W), rev(tb, SSM_W), rev(SUBLANES, 2 * N_STATE),
                  _const_spec((N_HALF, HALF_W, HALF_COLS)), _const_spec((N_HALF, HALF_COLS, HALF_W)),
                  _const_spec((SSM_SEG, 2 * N_STATE)), _const_spec((1, SSM_W))],
        out_specs=[rev(tb, SSM_W), any_spec, any_spec, _acc_spec((SUBLANES, 2 * N_STATE)), _acc_spec((1, SSM_W))],
        out_shape=[_sds((s_len, SSM_W), f32), _sds((N_HALF, HALF_W, HALF_COLS), f32),
                   _sds((N_HALF, HALF_COLS, HALF_W), f32), _sds((SUBLANES, 2 * N_STATE), f32), _sds((1, SSM_W), f32)],
        scratch_shapes=[state_rows, state_rows, seg_rows, seg_rows,
                        pltpu.VMEM((N_HALF, HALF_W, HALF_COLS), f32), pltpu.VMEM((N_HALF, HALF_COLS, HALF_W), f32),
                        tok_rows, tok_rows, tok_rows, pltpu.VMEM((SSM_W // LANES, tb, LANES), f32)],
        compiler_params=_cparams(("arbitrary",)),
    )(u, dy, cm_all, b_half, c_half, pw, d_skip)


def _branch_bwd(dya, dyc, cin, q, kv, k_t, conv_w, w_co_t, w_xo_t, side_blocks):
    s_len = dya.shape[0]
    tm = TOKEN_TILE
    n = s_len // tm
    halo_blocks = tm // 8
    ns = len(side_blocks)
    conv_tile = _sds((8, CONV_W), f32)
    side_in_specs, side_shapes, side_sems = _side_gather_specs(list(side_blocks) + [conv_tile])

    def body(*refs):
        (dya_ref, dyc_ref, cin_ref, cprev_ref, q_ref, kv_ref, cw_ref, wco_ref, wxo_ref, kt_ref) = refs[:10]
        side_ins = refs[10:10 + ns]
        dconv_ref, dq_ref, dkv_ref = refs[10 + ns:13 + ns]
        side_outs = refs[13 + ns:14 + 2 * ns]
        zs_ref, dczs_ref, dcw_ref = refs[14 + 2 * ns:17 + 2 * ns]
        copies = _side_gather_copies(list(side_ins) + [dcw_ref], side_outs, *refs[17 + 2 * ns:])
        side, conv_side = copies[:ns * N_DEV], copies[ns * N_DEV:]
        i = pl.program_id(0)
        tile = n - 1 - i

        @pl.when(i == 0)
        def _():
            dcw_ref[...] = jnp.zeros_like(dcw_ref)
            dkv_ref[...] = jnp.zeros_like(dkv_ref)
            dczs_ref[tm:tm + 8, :] = jnp.zeros((8, CONV_W), f32)
            for cp in side:
                cp.start()

        cin = cin_ref[...]
        cb, cc, ch = cin[:, :CONV_W], cin[:, CONV_W:2 * CONV_W], cin[:, 2 * CONV_W:]
        z = cc * ch
        cprev = cprev_ref[...]
        zprev = cprev[:, CONV_W:2 * CONV_W] * cprev[:, 2 * CONV_W:]
        zs_ref[0:8, :] = jnp.where(tile == 0, 0.0, zprev)
        zs_ref[8:8 + tm, :] = z
        z1 = zs_ref[pl.ds(7, tm), :]
        z2 = zs_ref[pl.ds(6, tm), :]
        cw = cw_ref[...]
        cz = cw[0:1] * z2 + cw[1:2] * z1 + cw[2:3] * z

        dain = _dot(dya_ref[...], wco_ref[...])
        dcb = dain * cz
        dcz = dain * cb
        dczs_ref[0:tm, :] = dcz
        dcz1 = dczs_ref[pl.ds(1, tm), :]
        dcz2 = dczs_ref[pl.ds(2, tm), :]
        dz = cw[2:3] * dcz + cw[1:2] * dcz1 + cw[0:1] * dcz2
        dczs_ref[tm:tm + 8, :] = dczs_ref[0:8, :]
        dcw_ref[0:1, :] += _colsum(dcz * z2)
        dcw_ref[1:2, :] += _colsum(dcz * z1)
        dcw_ref[2:3, :] += _colsum(dcz * z)
        dconv_ref[:, :CONV_W] = dcb.astype(bf16)
        dconv_ref[:, CONV_W:2 * CONV_W] = (dz * ch).astype(bf16)
        dconv_ref[:, 2 * CONV_W:] = (dz * cc).astype(bf16)

        qb = q_ref[...]
        dob = _dot(dyc_ref[...], wxo_ref[...]).astype(bf16)
        kv = kv_ref[...]
        heads = range(HEADS)
        hcs = [slice(h * HEAD_DIM, (h + 1) * HEAD_DIM) for h in heads]
        vcs = [slice(XATTN_W + h * HEAD_DIM, XATTN_W + (h + 1) * HEAD_DIM) for h in heads]
        s_t = [_dot(kv[:, hcs[h]], qb[:, hcs[h]], NT) * (HEAD_DIM ** -0.5) for h in heads]
        dp_t = [_dot(kv[:, vcs[h]], dob[:, hcs[h]], NT) for h in heads]
        e_t = [jnp.exp(s_t[h] - jnp.max(s_t[h], axis=0, keepdims=True)) for h in heads]
        p_t = [e_t[h] / jnp.sum(e_t[h], axis=0, keepdims=True) for h in heads]
        dv = [_dot(p_t[h].astype(bf16), dob[:, hcs[h]]) for h in heads]
        ds_t = [(p_t[h] * (dp_t[h] - jnp.sum(dp_t[h] * p_t[h], axis=0, keepdims=True)) * (HEAD_DIM ** -0.5)).astype(bf16)
                for h in heads]
        dk = [_dot(ds_t[h], qb[:, hcs[h]]) for h in heads]
        dq_t = [_dot(kt_ref[hcs[h], :], ds_t[h]) for h in heads]
        dq_ref[...] = jnp.concatenate(dq_t, axis=0).T.astype(bf16)
        dkv_ref[...] += jnp.concatenate(dk + dv, axis=1)

        @pl.when(i == n - 1)
        def _():
            for cp in conv_side:
                cp.start()
            for cp in side + conv_side:
                cp.wait()

    rev = functools.partial(_row_spec, rev_n=n)
    prev_spec = pl.BlockSpec((8, 3 * CONV_W), lambda i: (jnp.maximum((n - 1 - i) * halo_blocks - 1, 0), 0))
    outs = pl.pallas_call(
        body, name="branch_bwd", grid=(n,),
        in_specs=[rev(tm, D_MODEL), rev(tm, D_MODEL), rev(tm, 3 * CONV_W), prev_spec, rev(tm, XATTN_W),
                  _const_spec((MEM_LEN, 2 * XATTN_W)), _const_spec((3, CONV_W)), _const_spec((D_MODEL, CONV_W)),
                  _const_spec((D_MODEL, XATTN_W)), _const_spec((XATTN_W, MEM_LEN))] + side_in_specs[:ns],
        out_specs=[rev(tm, 3 * CONV_W), rev(tm, XATTN_W), _acc_spec((MEM_LEN, 2 * XATTN_W))] + side_in_specs,
        out_shape=[_sds((s_len, 3 * CONV_W), bf16), _sds((s_len, XATTN_W), bf16),
                   _sds((MEM_LEN, 2 * XATTN_W), f32)] + side_shapes,
        scratch_shapes=[pltpu.VMEM((tm + 8, CONV_W), f32), pltpu.VMEM((tm + 8, CONV_W), f32),
                        pltpu.VMEM((8, CONV_W), f32)] + side_sems,
        compiler_params=_cparams(("arbitrary",)),
    )(dya, dyc, cin, cin, q, kv, conv_w, w_co_t, w_xo_t, k_t, *side_blocks)
    return outs[0], outs[1], outs[2], outs[3:]


def _in_proj_bwd(dgp, dconv, du, dq, dxp, w_in_t):
    s_len = dgp.shape[0]
    tm = 2 * TOKEN_TILE
    n = s_len // tm

    def body(dgp_ref, dconv_ref, du_ref, dq_ref, dxp_ref, win_ref, dx_ref, dproj_ref):
        dproj = jnp.concatenate([dgp_ref[...], dconv_ref[...], du_ref[...].astype(bf16), dq_ref[...]], axis=1)
        dproj_ref[...] = dproj
        dx_ref[...] = dxp_ref[...] + _dot(dproj, win_ref[...])

    return pl.pallas_call(
        body, name="in_proj_bwd", grid=(n,),
        in_specs=[_row_spec(tm, GATE_COLS), _row_spec(tm, 3 * CONV_W), _row_spec(tm, SSM_W), _row_spec(tm, XATTN_W),
                  _row_spec(tm, D_MODEL), _const_spec((IN_COLS, D_MODEL))],
        out_specs=[_row_spec(tm, D_MODEL), _row_spec(tm, IN_COLS)],
        out_shape=[_sds((s_len, D_MODEL), f32), _sds((s_len, IN_COLS), bf16)],
        compiler_params=_cparams(("parallel",)),
    )(dgp, dconv, du, dq, dxp, w_in_t)


N_CHIP = 4
CHIP_STEPS = [(1, 1), (1, 0), (0, 1), (0, 0)]


def _flip(v, d):
    return 1 - v if d else v


def _chip_order():
    x, y, _ = _mesh_place()
    return jnp.stack([2 * _flip(x, dx) + _flip(y, dy) for dx, dy in CHIP_STEPS]).astype(jnp.int32)


def _weight_grads_scatter(problems, name):
    dims = []
    first = 0
    for a_t, b, tm, tt in problems:
        m, s_len = a_t.shape
        w = b.shape[1] // N_DEV
        tm, tt = min(tm, m), min(tt, s_len)
        assert m % tm == 0 and s_len % tt == 0
        nm, nt = m // tm, s_len // tt
        dims.append(dict(m=m, w=w, tm=tm, tt=tt, nm=nm, nt=nt, first=first, steps=N_CHIP * nm * nt))
        first += N_CHIP * nm * nt
    n_prob, total = len(problems), first
    n_scratch = 9

    def place(d, s):
        local = jnp.clip(s - d["first"], 0, d["steps"] - 1)
        return local // (d["nm"] * d["nt"]), (local // d["nt"]) % d["nm"], local % d["nt"]

    def run(d, q, im, t, a_ref, b_ref, recv_ref, acc_ref, send_ref, sib_ref, stash_ref,
            d2d_send, d2d_recv, ici_send, ici_recv, local_sem):
        tm, w, nm, nt = d["tm"], d["w"], d["nm"], d["nt"]
        x, y, c = _mesh_place()
        mesh_id = pl.DeviceIdType.MESH

        @pl.when(t == 0)
        def _():
            acc_ref[...] = jnp.zeros_like(acc_ref)

        acc_ref[...] += _dot(a_ref[...], b_ref[...])

        def to_sibling(qq, imm):
            rows = pl.ds(pl.multiple_of(imm * tm, tm), tm)
            return pltpu.make_async_remote_copy(
                src_ref=send_ref.at[qq, 0, rows, :], dst_ref=sib_ref.at[qq, rows, :],
                send_sem=d2d_send.at[qq], recv_sem=d2d_recv.at[qq, imm],
                device_id=(x, y, 1 - c), device_id_type=mesh_id)

        def finish_tile(qq, imm):
            rows = pl.ds(pl.multiple_of(imm * tm, tm), tm)
            to_sibling(qq, imm).wait_recv()
            both = stash_ref[...] + sib_ref[qq, rows, :].astype(f32)
            send_ref[qq, 1, rows, :] = both.astype(bf16)
            for step, (dx, dy) in enumerate(CHIP_STEPS):
                @pl.when(qq == step)
                def _(step=step, dx=dx, dy=dy):
                    src, dst = send_ref.at[step, 1, rows, :], recv_ref.at[step, rows, :]
                    if dx or dy:
                        pltpu.make_async_remote_copy(
                            src_ref=src, dst_ref=dst, send_sem=ici_send.at[step], recv_sem=ici_recv.at[step],
                            device_id=(_flip(x, dx), _flip(y, dy), c), device_id_type=mesh_id).start()
                    else:
                        pltpu.make_async_copy(src, dst, local_sem).start()

        @pl.when(t == nt - 1)
        def _():
            tile = q * nm + im

            @pl.when(tile > 0)
            def _():
                finish_tile((tile - 1) // nm, (tile - 1) % nm)

            rows = pl.ds(pl.multiple_of(im * tm, tm), tm)
            for core in (0, 1):
                @pl.when(c == core)
                def _(core=core):
                    other = 1 - core
                    send_ref[q, 0, rows, :] = acc_ref[:, other * w:(other + 1) * w].astype(bf16)
                    stash_ref[...] = acc_ref[:, core * w:(core + 1) * w]
            to_sibling(q, im).start()

            @pl.when(tile == N_CHIP * nm - 1)
            def _():
                finish_tile(q, im)
                for step, (dx, dy) in enumerate(CHIP_STEPS):
                    pltpu.make_async_remote_copy(
                        src_ref=send_ref.at[step, 0], dst_ref=sib_ref.at[step],
                        send_sem=d2d_send.at[step], recv_sem=d2d_recv.at[step, 0],
                        device_id=(x, y, 1 - c), device_id_type=mesh_id).wait_send()
                    src, dst = send_ref.at[step, 1], recv_ref.at[step]
                    if dx or dy:
                        pltpu.make_async_remote_copy(
                            src_ref=src, dst_ref=dst, send_sem=ici_send.at[step], recv_sem=ici_recv.at[step],
                            device_id=(_flip(x, dx), _flip(y, dy), c), device_id_type=mesh_id).wait()
                    else:
                        pltpu.make_async_copy(src, dst, local_sem).wait()

    def body(order_ref, *refs):
        del order_ref
        s = pl.program_id(0)
        operands, rest = refs[:2 * n_prob], refs[2 * n_prob:]
        results, scratch = rest[:n_prob], rest[n_prob:]
        for k, d in enumerate(dims):
            @pl.when((s >= d["first"]) & (s < d["first"] + d["steps"]))
            def _(k=k, d=d):
                q, im, t = place(d, s)
                run(d, q, im, t, operands[2 * k], operands[2 * k + 1], results[k],
                    *scratch[n_scratch * k:n_scratch * (k + 1)])

    in_specs, scratch_shapes = [], []
    for d in dims:
        def a_map(s, order, d=d):
            _, im, t = place(d, s)
            return im, t

        def b_map(s, order, d=d):
            q, _, t = place(d, s)
            return t, order[q]

        in_specs += [pl.BlockSpec((d["tm"], d["tt"]), a_map), pl.BlockSpec((d["tt"], 2 * d["w"]), b_map)]
        scratch_shapes += [pltpu.VMEM((d["tm"], 2 * d["w"]), f32), pltpu.VMEM((N_CHIP, 2, d["m"], d["w"]), bf16),
                           pltpu.VMEM((N_CHIP, d["m"], d["w"]), bf16), pltpu.VMEM((d["tm"], d["w"]), f32),
                           pltpu.SemaphoreType.DMA((N_CHIP,)), pltpu.SemaphoreType.DMA((N_CHIP, d["nm"])),
                           pltpu.SemaphoreType.DMA((N_CHIP - 1,)), pltpu.SemaphoreType.DMA((N_CHIP - 1,)),
                           pltpu.SemaphoreType.DMA]
    grid_spec = pltpu.PrefetchScalarGridSpec(
        num_scalar_prefetch=1, grid=(total,), in_specs=in_specs,
        out_specs=[pl.BlockSpec(memory_space=pl.ANY)] * n_prob, scratch_shapes=scratch_shapes)
    return pl.pallas_call(
        body, name=name, grid_spec=grid_spec,
        out_shape=[_sds((N_CHIP, d["m"], d["w"]), bf16) for d in dims],
        compiler_params=_cparams(("arbitrary",)),
    )(_chip_order(), *[op for a_t, b, _, _ in problems for op in (a_t, b)])


def _adamw(w, g, m, v):
    m = ADAM_B1 * m + (1.0 - ADAM_B1) * g
    v = ADAM_B2 * v + (1.0 - ADAM_B2) * jnp.square(g)
    m_hat = m / (1.0 - ADAM_B1 ** ADAM_STEP)
    v_hat = v / (1.0 - ADAM_B2 ** ADAM_STEP)
    delta = -ADAM_LR * (m_hat / (jnp.sqrt(v_hat) + ADAM_EPS) + ADAM_WD * w)
    return delta, m, v


def _sum_parts(p_ref):
    g = p_ref[0].astype(f32)
    for j in range(1, p_ref.shape[0]):
        g = g + p_ref[j].astype(f32)
    return g


def _adamw_update(w, m, v, parts, name, transposed):
    rows, cols = w.shape
    n_parts = parts.shape[0]
    if transposed:
        tc = 256
        steps = cols // tc
        p_spec = pl.BlockSpec((n_parts, tc, rows), lambda i: (0, i, 0))
        spec = pl.BlockSpec((rows, tc), lambda i: (0, i))
    else:
        tr = next(t for t in (256, 128, 64, 32, 16, 8) if rows % t == 0)
        steps = rows // tr
        p_spec = pl.BlockSpec((n_parts, tr, cols), lambda i: (0, i, 0))
        spec = pl.BlockSpec((tr, cols), lambda i: (i, 0))

    def body(w_ref, p_ref, m_ref, v_ref, g_ref, d_ref, nm_ref, nv_ref):
        g = _sum_parts(p_ref)
        if transposed:
            g = g.T
        g_ref[...] = g
        d_ref[...], nm_ref[...], nv_ref[...] = _adamw(w_ref[...], g, m_ref[...], v_ref[...])

    return pl.pallas_call(
        body, name=name, grid=(steps,),
        in_specs=[spec, p_spec, spec, spec], out_specs=[spec] * 4,
        out_shape=[_sds((rows, cols), f32)] * 4,
        compiler_params=_cparams(("parallel",)),
    )(w, parts, m, v)


SMALL_GROUPS = [
    (["b_gate", "ln1_g", "ln1_b", "b_up", "b_down", "ln2_g", "ln2_b", "ssm_d"], 1),
    (["ssm_lam_re", "ssm_lam_im", "ssm_c_re", "ssm_c_im", "ssm_b_re", "ssm_b_im"], 0),
    (["conv_w"], 0),
    (["ssm_log_dt"], 0),
]


def _sum_small(group_parts):
    def body(*refs):
        n = len(refs) // 2
        for p_ref, o_ref in zip(refs[:n], refs[n:]):
            o_ref[...] = _sum_parts(p_ref)

    return pl.pallas_call(
        body, name="sum_small",
        out_shape=[_sds(p.shape[1:], f32) for p in group_parts],
        compiler_params=_cparams(),
    )(*group_parts)


def _adamw_small(ws, ms, vs, group_sums):
    names = [k for group, _ in SMALL_GROUPS for k in group]
    n = len(names)

    def body(*refs):
        w_refs, m_refs, v_refs = (dict(zip(names, refs[j * n:(j + 1) * n])) for j in range(3))
        p_refs = refs[3 * n:3 * n + len(SMALL_GROUPS)]
        out_refs = [dict(zip(names, refs[3 * n + len(SMALL_GROUPS) + j * n:][:n])) for j in range(4)]
        for (group, axis), p_ref in zip(SMALL_GROUPS, p_refs):
            total = p_ref[...]
            off = 0
            for k in group:
                size = SMALL[k][axis]
                g = total[:, off:off + size] if axis == 1 else total[off:off + size, :]
                off += size
                d, nm, nv = _adamw(w_refs[k][...], g, m_refs[k][...], v_refs[k][...])
                for j, val in enumerate((g, d, nm, nv)):
                    out_refs[j][k][...] = val

    res = pl.pallas_call(
        body, name="adamw_small",
        out_shape=[_sds(SMALL[k], f32) for _ in range(4) for k in names],
        compiler_params=_cparams(),
    )(*[ws[k] for k in names], *[ms[k] for k in names], *[vs[k] for k in names], *group_sums)
    return [dict(zip(names, res[j * n:(j + 1) * n])) for j in range(4)]


def _ssm_discretize(lam_re, lam_im, log_dt, b_re, b_im):
    dt = jnp.exp(log_dt)[:, None]
    mag = jnp.exp(lam_re * dt)
    abar_r = mag * jnp.cos(lam_im * dt)
    abar_i = mag * jnp.sin(lam_im * dt)
    den = lam_re * lam_re + lam_im * lam_im
    nr = abar_r - 1.0
    ni = abar_i
    kr = (nr * lam_re + ni * lam_im) / den
    ki = (ni * lam_re - nr * lam_im) / den
    bbar_r = kr[:, None, :] * b_re - ki[:, None, :] * b_im
    bbar_i = kr[:, None, :] * b_im + ki[:, None, :] * b_re
    return abar_r, abar_i, bbar_r, bbar_i


def _state_layout(re, im):
    parts = []
    for half in range(N_HALF):
        cols = slice(half * HALF_STATE, (half + 1) * HALF_STATE)
        parts += [re[..., cols], im[..., cols]]
    return jnp.concatenate(parts, axis=-1)


def _state_unlayout(a):
    re = jnp.concatenate([a[..., _half_cols(h)[0]] for h in range(N_HALF)], axis=-1)
    im = jnp.concatenate([a[..., _half_cols(h)[1]] for h in range(N_HALF)], axis=-1)
    return re, im


def _abar_powers(abar_r, abar_i):
    pr, pi = abar_r.reshape(1, N_STATE), abar_i.reshape(1, N_STATE)
    while pr.shape[0] < SSM_SEG:
        tr, ti = pr[-1:], pi[-1:]
        pr, pi = (jnp.concatenate([pr, pr * tr - pi * ti], axis=0), jnp.concatenate([pi, pr * ti + pi * tr], axis=0))
    return _state_layout(pr, pi)


HALF_GROUPS = SSM_GROUPS // N_HALF


def _half_block_diag(blocks):
    _, r, c = blocks.shape
    eye = jnp.eye(HALF_GROUPS, dtype=blocks.dtype)
    b4 = blocks.reshape(N_HALF, HALF_GROUPS, r, c)
    return jnp.einsum("ngrc,gk->ngrkc", b4, eye).reshape(N_HALF, HALF_GROUPS * r, HALF_GROUPS * c)


def _half_diag_blocks(mat, r, c):
    eye = jnp.eye(HALF_GROUPS, dtype=mat.dtype)
    m5 = mat.reshape(N_HALF, HALF_GROUPS, r, HALF_GROUPS, c)
    return jnp.einsum("ngrkc,gk->ngrc", m5, eye).reshape(SSM_GROUPS, r, c)


BIG = ["w_in", "w_conv_out", "w_glu", "w_kv", "w_xattn_out", "w_out", "w_up", "w_down"]
GATHER_TRANSPOSED = ["w_conv_out", "w_glu", "w_xattn_out", "w_up"]
PARTS_TRANSPOSED = ["w_in", "w_kv", "w_out", "w_down"]
SMALL = {"b_gate": (1, GATE_COLS), "conv_w": (3, CONV_W), "ssm_lam_re": (SSM_GROUPS, SSM_STATE),
         "ssm_lam_im": (SSM_GROUPS, SSM_STATE), "ssm_log_dt": (1, SSM_GROUPS),
         "ssm_b_re": (SSM_W, SSM_STATE), "ssm_b_im": (SSM_W, SSM_STATE),
         "ssm_c_re": (SSM_W, SSM_STATE), "ssm_c_im": (SSM_W, SSM_STATE), "ssm_d": (1, SSM_W),
         "ln1_g": (1, D_MODEL), "ln1_b": (1, D_MODEL), "b_up": (1, D_FF), "b_down": (1, D_MODEL),
         "ln2_g": (1, D_MODEL), "ln2_b": (1, D_MODEL)}
WEIGHTS = ["w_in", "b_gate", "conv_w", "w_conv_out", "ssm_lam_re", "ssm_lam_im", "ssm_log_dt", "ssm_b_re", "ssm_b_im",
           "ssm_c_re", "ssm_c_im", "ssm_d", "w_glu", "w_kv", "w_xattn_out", "w_out", "ln1_g", "ln1_b", "w_up", "b_up",
           "w_down", "b_down", "ln2_g", "ln2_b"]


def _local_step(x, mem, tgt, full, late, small):
    lam_re, lam_im, log_dt = small["ssm_lam_re"], small["ssm_lam_im"], small["ssm_log_dt"].reshape(SSM_GROUPS)
    c_shape = (SSM_GROUPS, SSM_GROUP, SSM_STATE)
    disc, disc_vjp = jax.vjp(_ssm_discretize, lam_re, lam_im, log_dt,
                             small["ssm_b_re"].reshape(c_shape), small["ssm_b_im"].reshape(c_shape))
    abar_r, abar_i, bbar_r, bbar_i = disc
    pw = _abar_powers(abar_r, abar_i)
    c_re, c_im = small["ssm_c_re"].reshape(c_shape), small["ssm_c_im"].reshape(c_shape)
    b_half = jnp.concatenate([_half_block_diag(bbar_r), _half_block_diag(bbar_i)], axis=2).astype(bf16)
    c_half = jnp.concatenate([_half_block_diag(c_re.transpose(0, 2, 1)), -_half_block_diag(c_im.transpose(0, 2, 1))],
                             axis=1).astype(bf16)

    s_len = x.shape[0]
    stack = lambda a: a.reshape(-1, a.shape[-1])
    kv, k_t, memb = _kv_proj(mem, full["w_kv"])
    (xbt, g, cin, u, q, ain, ob, aint, obt), side = _in_proj(
        x, full["w_in"], small["b_gate"], small["conv_w"], kv,
        [late[k] for k in ("w_glu", "w_conv_out", "w_xattn_out", "w_out", "w_up")])
    w_glu_t, w_co_t, w_xo_t, w_out, w_up_t = (stack(a) for a in side)
    y_ssm, cm_all, side = _ssm_fwd(u, b_half, c_half, pw, small["ssm_d"], [late["w_down"]])
    w_down = stack(side[0])
    ysbt, mb, xhat1, rstd1 = _mid_fwd(y_ssm, g, ain, ob, x, w_glu_t, w_co_t, w_xo_t, w_out,
                                      small["ln1_g"], small["ln1_b"])
    (x1bt, hdn, dr2bt, dpre, dx1, loss, dl2g, dl2b, dbdn, dbup) = _mlp_fwd_bwd(
        xhat1, tgt, small["ln1_g"], small["ln1_b"], w_up_t, small["b_up"], w_down,
        small["b_down"], small["ln2_g"], small["ln2_b"])
    (dxp, dr1bt, dgp, dya, dyc, dglu, dyssm, dl1g, dl1b, dbg) = _mid_bwd(
        dx1, xhat1, rstd1, g, ain, ob, y_ssm, small["ln1_g"], w_out, w_glu_t, w_co_t, w_xo_t)
    du, db_half, dc_half, da8, dd = _ssm_bwd(u, dyssm, cm_all, b_half, c_half, pw, small["ssm_d"])
    dabar_r, dabar_i = _state_unlayout(jnp.sum(da8, axis=0))
    dbbar_r = _half_diag_blocks(db_half[:, :, :HALF_STATE], SSM_GROUP, SSM_STATE)
    dbbar_i = _half_diag_blocks(db_half[:, :, HALF_STATE:], SSM_GROUP, SSM_STATE)
    g_shape = (SSM_GROUPS, SSM_STATE)
    dlam_re, dlam_im, dlog_dt, db_re, db_im = disc_vjp(
        (dabar_r.reshape(g_shape), dabar_i.reshape(g_shape), dbbar_r, dbbar_i))
    dc_re = _half_diag_blocks(dc_half[:, :HALF_STATE, :], SSM_STATE, SSM_GROUP).transpose(0, 2, 1)
    dc_im = -_half_diag_blocks(dc_half[:, HALF_STATE:, :], SSM_STATE, SSM_GROUP).transpose(0, 2, 1)

    small_grads = {
        "b_gate": dbg, "ssm_lam_re": dlam_re, "ssm_lam_im": dlam_im, "ssm_log_dt": dlog_dt,
        "ssm_b_re": db_re, "ssm_b_im": db_im, "ssm_c_re": dc_re, "ssm_c_im": dc_im, "ssm_d": dd,
        "ln1_g": dl1g, "ln1_b": dl1b, "b_up": dbup, "b_down": dbdn, "ln2_g": dl2g, "ln2_b": dl2b,
    }
    small_grads = {k: a.reshape(SMALL[k]) for k, a in small_grads.items()}
    groups = [(group, axis) for group, axis in SMALL_GROUPS if group != ["conv_w"]]
    stacks = [jnp.concatenate([small_grads[k] for k in group], axis=axis) if len(group) > 1 else small_grads[group[0]]
              for group, axis in groups]
    n_rowvec = stacks[0].shape[1]
    stacks[0] = jnp.concatenate([stacks[0], loss], axis=1)
    dense = lambda a: a.reshape(-1, LANES) if a.size % LANES == 0 else a
    dconv, dq, dkv, group_parts = _branch_bwd(dya, dyc, cin, q, kv, k_t, small["conv_w"], w_co_t, w_xo_t,
                                              [dense(a) for a in stacks])
    dx, dproj = _in_proj_bwd(dgp, dconv, du, dq, dxp, full["w_in"])
    tm, tt = 512, 2048
    calls = {
        "dw_in_kv": {"w_in": (xbt, dproj, tm, tt), "w_kv": (dkv.T.astype(bf16), memb, D_MODEL, MEM_LEN)},
        "dw_up_conv_xattn": {"w_up": (x1bt, dpre, tm, tt), "w_conv_out": (aint, dya, tm, tt),
                             "w_xattn_out": (obt, dyc, tm, tt)},
        "dw_down_out_glu": {"w_down": (dr2bt, hdn, tm, tt), "w_out": (dr1bt, mb, tm, tt),
                            "w_glu": (ysbt, dglu, tm, tt)},
    }
    recv = {}
    for call_name, problems in calls.items():
        recv.update(zip(problems, _weight_grads_scatter(list(problems.values()), call_name)))
    sums = _sum_small(group_parts)
    group_sums = dict(zip([tuple(group) for group, _ in groups], [s.reshape(a.shape) for s, a in zip(sums, stacks)]))
    group_sums[("conv_w",)] = sums[-1][0:3]
    first = tuple(groups[0][0])
    loss_all = group_sums[first][0, n_rowvec]
    group_sums[first] = group_sums[first][:, :n_rowvec]
    return loss_all, dx, recv, [group_sums[tuple(group)] for group, _ in SMALL_GROUPS]


def kernel(x, mem, w_in, b_gate, conv_w, w_conv_out, ssm_lam_re, ssm_lam_im, ssm_log_dt, ssm_b_re, ssm_b_im, ssm_c_re, ssm_c_im, ssm_d, w_glu, w_kv, w_xattn_out, w_out, ln1_g, ln1_b, w_up, b_up, w_down, b_down, ln2_g, ln2_b, loss_target, m_w_in, m_b_gate, m_conv_w, m_w_conv_out, m_ssm_lam_re, m_ssm_lam_im, m_ssm_log_dt, m_ssm_b_re, m_ssm_b_im, m_ssm_c_re, m_ssm_c_im, m_ssm_d, m_w_glu, m_w_kv, m_w_xattn_out, m_w_out, m_ln1_g, m_ln1_b, m_w_up, m_b_up, m_w_down, m_b_down, m_ln2_g, m_ln2_b, v_w_in, v_b_gate, v_conv_w, v_w_conv_out, v_ssm_lam_re, v_ssm_lam_im, v_ssm_log_dt, v_ssm_b_re, v_ssm_b_im, v_ssm_c_re, v_ssm_c_im, v_ssm_d, v_w_glu, v_w_kv, v_w_xattn_out, v_w_out, v_ln1_g, v_ln1_b, v_w_up, v_b_up, v_w_down, v_b_down, v_ln2_g, v_ln2_b):
    w = dict(w_in=w_in, b_gate=b_gate, conv_w=conv_w, w_conv_out=w_conv_out, ssm_lam_re=ssm_lam_re,
             ssm_lam_im=ssm_lam_im, ssm_log_dt=ssm_log_dt, ssm_b_re=ssm_b_re, ssm_b_im=ssm_b_im, ssm_c_re=ssm_c_re,
             ssm_c_im=ssm_c_im, ssm_d=ssm_d, w_glu=w_glu, w_kv=w_kv, w_xattn_out=w_xattn_out, w_out=w_out,
             ln1_g=ln1_g, ln1_b=ln1_b, w_up=w_up, b_up=b_up, w_down=w_down, b_down=b_down, ln2_g=ln2_g, ln2_b=ln2_b)
    m = dict(w_in=m_w_in, b_gate=m_b_gate, conv_w=m_conv_w, w_conv_out=m_w_conv_out, ssm_lam_re=m_ssm_lam_re,
             ssm_lam_im=m_ssm_lam_im, ssm_log_dt=m_ssm_log_dt, ssm_b_re=m_ssm_b_re, ssm_b_im=m_ssm_b_im,
             ssm_c_re=m_ssm_c_re, ssm_c_im=m_ssm_c_im, ssm_d=m_ssm_d, w_glu=m_w_glu, w_kv=m_w_kv,
             w_xattn_out=m_w_xattn_out, w_out=m_w_out, ln1_g=m_ln1_g, ln1_b=m_ln1_b, w_up=m_w_up, b_up=m_b_up,
             w_down=m_w_down, b_down=m_b_down, ln2_g=m_ln2_g, ln2_b=m_ln2_b)
    v = dict(w_in=v_w_in, b_gate=v_b_gate, conv_w=v_conv_w, w_conv_out=v_w_conv_out, ssm_lam_re=v_ssm_lam_re,
             ssm_lam_im=v_ssm_lam_im, ssm_log_dt=v_ssm_log_dt, ssm_b_re=v_ssm_b_re, ssm_b_im=v_ssm_b_im,
             ssm_c_re=v_ssm_c_re, ssm_c_im=v_ssm_c_im, ssm_d=v_ssm_d, w_glu=v_w_glu, w_kv=v_w_kv,
             w_xattn_out=v_w_xattn_out, w_out=v_w_out, ln1_g=v_ln1_g, ln1_b=v_ln1_b, w_up=v_w_up, b_up=v_b_up,
             w_down=v_w_down, b_down=v_b_down, ln2_g=v_ln2_g, ln2_b=v_ln2_b)
    out_shapes = {k: a.shape for k, a in w.items()}
    swapped = ("w_in", "ssm_b_re", "ssm_b_im")

    def shard2d(k, a):
        if k in swapped:
            a = jnp.swapaxes(a, -1, -2)
        if k in SMALL:
            return a.reshape((3, CONV_W // N_DEV) if k == "conv_w" else SMALL[k])
        return a[0]

    def result(k, a):
        if k in swapped:
            shape = out_shapes[k]
            return jnp.swapaxes(a.reshape(shape[:-2] + (shape[-1], shape[-2])), -1, -2)
        return a.reshape(out_shapes[k])

    w, m, v = ({k: shard2d(k, a) for k, a in d.items()} for d in (w, m, v))

    shards = {k: w[k].T.astype(bf16) if k in GATHER_TRANSPOSED else w[k].astype(bf16) for k in BIG}
    conv_pad = jnp.pad(w["conv_w"], ((0, 5), (0, LANES - CONV_W // N_DEV)))
    early = ["w_in", "w_kv"]
    gathered = _all_gather([shards[k] for k in early] + [conv_pad], "gather_weights")
    full = {k: a.reshape(-1, a.shape[-1]) for k, a in zip(early, gathered[:-1])}
    late = {k: shards[k] for k in BIG if k not in early}
    conv_full = gathered[-1][:, :3, :CONV_W // N_DEV].transpose(1, 0, 2).reshape(3, CONV_W)
    small = {k: (conv_full if k == "conv_w" else w[k]) for k in SMALL}

    loss, dx, recv, group_sums = _local_step(x[0], mem[0], loss_target[0], full, late, small)

    grads, deltas, new_m, new_v = {}, {}, {}, {}
    for k in BIG:
        res = _adamw_update(w[k], m[k], v[k], recv[k], "adamw_" + k, transposed=k in PARTS_TRANSPOSED)
        grads[k], deltas[k], new_m[k], new_v[k] = res

    widen = lambda k, a: jnp.tile(a, (1, N_DEV)) if k == "conv_w" else a
    res = _adamw_small(small, {k: widen(k, m[k]) for k in SMALL}, {k: widen(k, v[k]) for k in SMALL}, group_sums)
    dev = _slot(_mesh_place())
    for d, small_res in zip((grads, deltas, new_m, new_v), res):
        for k, a in small_res.items():
            if k == "conv_w":
                a = lax.dynamic_slice_in_dim(a, dev * (CONV_W // N_DEV), CONV_W // N_DEV, axis=1)
            d[k] = a

    outs = [loss, dx[None]]
    for d in (grads, deltas, new_m, new_v):
        outs += [result(k, d[k]) for k in WEIGHTS]
    return tuple(outs)
```

```python
import functools
import math

import jax
import jax.numpy as jnp
from jax import lax
from jax.experimental import pallas as pl
from jax.experimental.pallas import tpu as pltpu

f32 = jnp.float32
bf16 = jnp.bfloat16

D_MODEL = 1024
MEM_LEN = 256
GATE_COLS = 3 * D_MODEL
CONV_W = 512
SSM_W = 512
XATTN_W = 512
HEADS = 4
HEAD_DIM = 128
D_FF = 4096
IN_COLS = GATE_COLS + 3 * CONV_W + SSM_W + XATTN_W
SSM_GROUPS = 32
SSM_GROUP = 16
SSM_STATE = 64
N_STATE = SSM_GROUPS * SSM_STATE
ALPHA = 2.0 ** 0.25
LN_EPS = 1e-5
N_DEV = 8

ADAM_LR = 0.001
ADAM_B1 = 0.9
ADAM_B2 = 0.999
ADAM_EPS = 1e-08
ADAM_WD = 0.01
ADAM_STEP = 10

VMEM_LIMIT_V7X = 56 * 2 ** 20
SUBLANES = 8
LANES = 128

TOKEN_TILE = 256
SSM_BLOCK = 512
SSM_SEG = SSM_BLOCK // SUBLANES
LANE_CHUNK = 256
N_HALF = 2
HALF_W = SSM_W // N_HALF
HALF_STATE = N_STATE // N_HALF
HALF_COLS = 2 * HALF_STATE

NT = (((1,), (1,)), ((), ()))
TN = (((0,), (0,)), ((), ()))
NN = (((1,), (0,)), ((), ()))


def _dot(a, b, dims=NN):
    return lax.dot_general(a, b, dims, preferred_element_type=f32)


def _cparams(sem=None):
    return pltpu.CompilerParams(dimension_semantics=sem, vmem_limit_bytes=VMEM_LIMIT_V7X)


def _row_spec(tm, cols, rev_n=None):
    if rev_n is None:
        return pl.BlockSpec((tm, cols), lambda i: (i, 0))
    return pl.BlockSpec((tm, cols), lambda i: (rev_n - 1 - i, 0))


def _col_spec(rows, tm):
    return pl.BlockSpec((rows, tm), lambda i: (0, i))


def _const_spec(shape):
    nd = len(shape)
    return pl.BlockSpec(shape, lambda *_: (0,) * nd, pipeline_mode=pl.Buffered(1))


def _acc_spec(shape):
    nd = len(shape)
    return pl.BlockSpec(shape, lambda *_: (0,) * nd)


def _sds(shape, dtype):
    return jax.ShapeDtypeStruct(shape, dtype)


def _gelu(x):
    c = math.sqrt(2.0 / math.pi)
    return 0.5 * x * (1.0 + jnp.tanh(c * (x + 0.044715 * x * x * x)))


def _gelu_grad(x):
    c = math.sqrt(2.0 / math.pi)
    t = jnp.tanh(c * (x + 0.044715 * x * x * x))
    return 0.5 * (1.0 + t) + 0.5 * x * (1.0 - t * t) * c * (1.0 + 3.0 * 0.044715 * x * x)


def _colsum(a):
    return jnp.sum(a, axis=0, keepdims=True)


def _mesh_place():
    return lax.axis_index("x"), lax.axis_index("y"), lax.axis_index("c")


def _slot(p):
    return 4 * p[0] + 2 * p[1] + p[2]


def _other_devices(me):
    x, y, c = me
    flip = lambda v, d: 1 - v if d else v
    return [(flip(x, dx), flip(y, dy), flip(c, dc)) for dx in (0, 1) for dy in (0, 1) for dc in (0, 1)][1:]


def _all_gather(blocks, name):
    n = len(blocks)

    def body(*refs):
        ins, outs = refs[:n], refs[n:2 * n]
        send_sems, recv_sems, local_sems = refs[2 * n:]
        x, y, c = _mesh_place()
        me, sibling = (x, y, c), (x, y, 1 - c)
        chips = [(1 - x, y), (x, 1 - y), (1 - x, 1 - y)]

        def copy(a, k, block, to, src=None):
            rows = outs[a].at[_slot(block)]
            return pltpu.make_async_remote_copy(
                src_ref=rows if src is None else src, dst_ref=rows,
                send_sem=send_sems.at[a, k], recv_sem=recv_sems.at[a, k],
                device_id=to, device_id_type=pl.DeviceIdType.MESH)

        mine = [pltpu.make_async_copy(ins[a], outs[a].at[_slot(me)], local_sems.at[a]) for a in range(n)]
        for cp in mine:
            cp.start()
        first = []
        for a in range(n):
            first.append(copy(a, 0, me, sibling, src=ins[a]))
            first += [copy(a, 1 + j, me, (*chip, c), src=ins[a]) for j, chip in enumerate(chips)]
        for cp in first:
            cp.start()
        passed = []
        for a in range(n):
            for j, chip in enumerate(chips):
                copy(a, 1 + j, (*chip, c), me).wait_recv()
                fwd = copy(a, 4 + j, (*chip, c), sibling)
                fwd.start()
                passed.append(fwd)
        for a in range(n):
            copy(a, 0, sibling, me).wait_recv()
            for j, chip in enumerate(chips):
                copy(a, 4 + j, (*chip, 1 - c), me).wait_recv()
        for cp in first + passed:
            cp.wait_send()
        for cp in mine:
            cp.wait()

    any_spec = pl.BlockSpec(memory_space=pl.ANY)
    return pl.pallas_call(
        body, name=name,
        out_shape=[_sds((N_DEV,) + b.shape, b.dtype) for b in blocks],
        in_specs=[any_spec] * n, out_specs=[any_spec] * n,
        scratch_shapes=[pltpu.SemaphoreType.DMA((n, 7)), pltpu.SemaphoreType.DMA((n, 7)),
                        pltpu.SemaphoreType.DMA((n,))],
    )(*blocks)


def _side_gather_copies(ins, outs, send_sems, recv_sems, local_sems):
    me = _mesh_place()
    copies = []
    for a, (src, dst) in enumerate(zip(ins, outs)):
        copies.append(pltpu.make_async_copy(src, dst.at[_slot(me)], local_sems.at[a]))
        for k, peer in enumerate(_other_devices(me)):
            copies.append(pltpu.make_async_remote_copy(
                src_ref=src, dst_ref=dst.at[_slot(me)], send_sem=send_sems.at[a, k], recv_sem=recv_sems.at[a, k],
                device_id=peer, device_id_type=pl.DeviceIdType.MESH))
    return copies


def _side_gather_two_level(ins, outs, send_sems, recv_sems, local_sems):
    x, y, c = _mesh_place()
    me, sibling = (x, y, c), (x, y, 1 - c)
    chips = [(1 - x, y), (x, 1 - y), (1 - x, 1 - y)]

    def copy(a, k, block, to, src=None):
        rows = outs[a].at[_slot(block)]
        return pltpu.make_async_remote_copy(
            src_ref=rows if src is None else src, dst_ref=rows, send_sem=send_sems.at[a, k], recv_sem=recv_sems.at[a, k],
            device_id=to, device_id_type=pl.DeviceIdType.MESH)

    n = len(ins)
    mine = [pltpu.make_async_copy(ins[a], outs[a].at[_slot(me)], local_sems.at[a]) for a in range(n)]
    first = [copy(a, 0, me, sibling, src=ins[a]) for a in range(n)]
    first += [copy(a, 1 + j, me, (*chip, c), src=ins[a]) for a in range(n) for j, chip in enumerate(chips)]
    passed = [copy(a, 4 + j, (*chip, c), sibling) for a in range(n) for j, chip in enumerate(chips)]

    def start():
        for cp in mine + first:
            cp.start()

    def forward():
        for a in range(n):
            for j, chip in enumerate(chips):
                copy(a, 1 + j, (*chip, c), me).wait_recv()
        for cp in passed:
            cp.start()

    def finish():
        for a in range(n):
            copy(a, 0, sibling, me).wait_recv()
            for j, chip in enumerate(chips):
                copy(a, 4 + j, (*chip, 1 - c), me).wait_recv()
        for cp in first + passed:
            cp.wait_send()
        for cp in mine:
            cp.wait()

    return start, forward, finish


def _side_gather_specs(blocks):
    n = len(blocks)
    any_spec = pl.BlockSpec(memory_space=pl.ANY)
    return ([any_spec] * n, [_sds((N_DEV,) + b.shape, b.dtype) for b in blocks],
            [pltpu.SemaphoreType.DMA((n, N_DEV - 1)), pltpu.SemaphoreType.DMA((n, N_DEV - 1)),
             pltpu.SemaphoreType.DMA((n,))])


def _kv_proj(mem, w_kv):
    def body(mem_ref, w_ref, kv_ref, kt_ref, memb_ref):
        mb = mem_ref[...].astype(bf16)
        memb_ref[...] = mb
        kv = _dot(mb, w_ref[...]).astype(bf16)
        kv_ref[...] = kv
        kt_ref[...] = kv[:, :XATTN_W].T

    return pl.pallas_call(
        body, name="kv_proj",
        out_shape=[_sds((MEM_LEN, 2 * XATTN_W), bf16), _sds((XATTN_W, MEM_LEN), bf16), _sds((MEM_LEN, D_MODEL), bf16)],
        compiler_params=_cparams(),
    )(mem, w_kv)


def _attention_probs(qb, kv_ref, h):
    kh = kv_ref[:, h * HEAD_DIM:(h + 1) * HEAD_DIM]
    s = _dot(qb[:, h * HEAD_DIM:(h + 1) * HEAD_DIM], kh, NT) * (HEAD_DIM ** -0.5)
    e = jnp.exp(s - jnp.max(s, axis=-1, keepdims=True))
    return e / jnp.sum(e, axis=-1, keepdims=True)


def _in_proj(x, w_in_t, b_gate, conv_w, kv, side_blocks):
    s_len = x.shape[0]
    tm = 2 * TOKEN_TILE
    n = s_len // tm
    ns = len(side_blocks)
    side_in_specs, side_shapes, side_sems = _side_gather_specs(side_blocks)

    def body(*refs):
        (x_ref, win_ref, bg_ref, cw_ref, kv_ref) = refs[:5]
        side_ins = refs[5:5 + ns]
        (xbt_ref, g_ref, cin_ref, u_ref, q_ref, ain_ref, o_ref, aint_ref, ot_ref) = refs[5 + ns:14 + ns]
        side_outs = refs[14 + ns:14 + 2 * ns]
        zs_ref = refs[14 + 2 * ns]
        side_start, side_forward, side_finish = _side_gather_two_level(side_ins, side_outs, *refs[15 + 2 * ns:])
        i = pl.program_id(0)
        pl.when(i == 0)(side_start)
        pl.when(i == (3 * n) // 4)(side_forward)

        xb = x_ref[...].astype(bf16)
        xbt_ref[...] = xb.T
        proj = _dot(xb, win_ref[...], NT)
        g_ref[...] = jax.nn.sigmoid(proj[:, :GATE_COLS] + bg_ref[...]).astype(bf16)
        cin = proj[:, GATE_COLS:GATE_COLS + 3 * CONV_W]
        cin_ref[...] = cin
        u_ref[...] = proj[:, GATE_COLS + 3 * CONV_W:GATE_COLS + 3 * CONV_W + SSM_W]
        qb = proj[:, IN_COLS - XATTN_W:].astype(bf16)
        q_ref[...] = qb

        cb, cc, ch = cin[:, :CONV_W], cin[:, CONV_W:2 * CONV_W], cin[:, 2 * CONV_W:]
        z = cc * ch

        @pl.when(i == 0)
        def _():
            zs_ref[0:8, :] = jnp.zeros((8, CONV_W), f32)

        zs_ref[8:8 + tm, :] = z
        z1 = zs_ref[pl.ds(7, tm), :]
        z2 = zs_ref[pl.ds(6, tm), :]
        cw = cw_ref[...]
        cz = cw[0:1] * z2 + cw[1:2] * z1 + cw[2:3] * z
        zs_ref[0:8, :] = zs_ref[tm:tm + 8, :]
        ain = (cb * cz).astype(bf16)
        ain_ref[...] = ain
        aint_ref[...] = ain.T

        probs = [_attention_probs(qb, kv_ref, h) for h in range(HEADS)]
        outs = [_dot(probs[h].astype(bf16), kv_ref[:, XATTN_W + h * HEAD_DIM:XATTN_W + (h + 1) * HEAD_DIM])
                for h in range(HEADS)]
        ob = jnp.concatenate(outs, axis=1).astype(bf16)
        o_ref[...] = ob
        ot_ref[...] = ob.T

        pl.when(i == n - 1)(side_finish)

    row_cols = [(GATE_COLS, bf16), (3 * CONV_W, f32), (SSM_W, f32), (XATTN_W, bf16), (CONV_W, bf16), (XATTN_W, bf16)]
    t_rows = [D_MODEL, CONV_W, XATTN_W]
    outs = pl.pallas_call(
        body, name="in_proj", grid=(n,),
        in_specs=[_row_spec(tm, D_MODEL), _const_spec((IN_COLS, D_MODEL)), _const_spec((1, GATE_COLS)),
                  _const_spec((3, CONV_W)), _const_spec((MEM_LEN, 2 * XATTN_W))] + side_in_specs,
        out_specs=([_col_spec(t_rows[0], tm)] + [_row_spec(tm, c) for c, _ in row_cols]
                   + [_col_spec(t_rows[1], tm), _col_spec(t_rows[2], tm)] + side_in_specs),
        out_shape=([_sds((t_rows[0], s_len), bf16)] + [_sds((s_len, c), dt) for c, dt in row_cols]
                   + [_sds((t_rows[1], s_len), bf16), _sds((t_rows[2], s_len), bf16)] + side_shapes),
        scratch_shapes=[pltpu.VMEM((tm + 8, CONV_W), f32)] + side_sems,
        compiler_params=_cparams(("arbitrary",)),
    )(x, w_in_t, b_gate, conv_w, kv, *side_blocks)
    return outs[:9], outs[9:]


def _state_cols(chunk, width=LANE_CHUNK):
    half, off = divmod(chunk * width, HALF_STATE)
    lo = half * HALF_COLS + off
    return slice(lo, lo + width), slice(lo + HALF_STATE, lo + HALF_STATE + width)


def _half_cols(half):
    lo = half * HALF_COLS
    return slice(lo, lo + HALF_STATE), slice(lo + HALF_STATE, lo + HALF_COLS)


def _rows_to_segments(src_ref, stage_ref, dst_ref):
    nc = SSM_W // LANES
    for c in range(nc):
        stage_ref[c] = src_ref[:, c * LANES:(c + 1) * LANES]
    for c in range(nc):
        for k in range(SSM_SEG):
            dst_ref[k * SUBLANES:(k + 1) * SUBLANES, c * LANES:(c + 1) * LANES] = (
                stage_ref[c, pl.ds(k, SUBLANES, stride=SSM_SEG), :])


def _rows_from_segments(src_ref, stage_ref, dst_ref):
    nc = SSM_W // LANES
    for c in range(nc):
        for k in range(SSM_SEG):
            stage_ref[c, pl.ds(k, SUBLANES, stride=SSM_SEG), :] = (
                src_ref[k * SUBLANES:(k + 1) * SUBLANES, c * LANES:(c + 1) * LANES])
    for c in range(nc):
        dst_ref[:, c * LANES:(c + 1) * LANES] = stage_ref[c]


def _ssm_scan(s_ref, pw_ref, init_ref, reverse, unroll, width=LANE_CHUNK):
    for chunk in range(N_STATE // width):
        re, im = _state_cols(chunk, width)
        ar = jnp.broadcast_to(pw_ref[0:1, re], (SUBLANES, width))
        ai = jnp.broadcast_to(pw_ref[0:1, im], (SUBLANES, width))
        if reverse:
            ai = -ai

        def step(j, carry, re=re, im=im, ar=ar, ai=ai):
            sr, si = carry
            k = (SSM_SEG - 1 - j) if reverse else j
            r0 = pl.multiple_of(k * SUBLANES, SUBLANES)
            nr = ar * sr - ai * si + s_ref[pl.ds(r0, SUBLANES), re]
            ni = ar * si + ai * sr + s_ref[pl.ds(r0, SUBLANES), im]
            s_ref[pl.ds(r0, SUBLANES), re] = nr
            s_ref[pl.ds(r0, SUBLANES), im] = ni
            return nr, ni

        if init_ref is None:
            init = (jnp.zeros((SUBLANES, width), f32),) * 2
        else:
            init = (init_ref[:, re], init_ref[:, im])
        lax.fori_loop(0, SSM_SEG, step, init, unroll=unroll)


def _ssm_add_carry(s_ref, pw_ref, cm_ref, reverse):
    for chunk in range(N_STATE // LANE_CHUNK):
        re, im = _state_cols(chunk)
        cr, ci = cm_ref[:, re], cm_ref[:, im]
        for k in range(SSM_SEG):
            pk = (SSM_SEG - 1 - k) if reverse else k
            pr = pw_ref[pk:pk + 1, re]
            pi = pw_ref[pk:pk + 1, im]
            if reverse:
                pi = -pi
            rows = slice(k * SUBLANES, (k + 1) * SUBLANES)
            s_ref[rows, re] = s_ref[rows, re] + (pr * cr - pi * ci)
            s_ref[rows, im] = s_ref[rows, im] + (pr * ci + pi * cr)


def _ssm_carries(first_row, s_ref, pw_ref, carry_ref, cm_ref, reverse):
    order = range(SUBLANES - 1, -1, -1) if reverse else range(SUBLANES)
    for half in range(N_HALF):
        re, im = _half_cols(half)
        a_r, a_i = pw_ref[SSM_SEG - 1:SSM_SEG, re], pw_ref[SSM_SEG - 1:SSM_SEG, im]
        if reverse:
            a_i = -a_i
        cr, ci = carry_ref[0:1, re], carry_ref[0:1, im]
        for seg in order:
            cm_ref[seg:seg + 1, re] = cr
            cm_ref[seg:seg + 1, im] = ci
            er = s_ref[first_row + seg:first_row + seg + 1, re]
            ei = s_ref[first_row + seg:first_row + seg + 1, im]
            cr, ci = a_r * cr - a_i * ci + er, a_r * ci + a_i * cr + ei
        carry_ref[0:1, re] = cr
        carry_ref[0:1, im] = ci


def _ssm_fwd(u, b_half, c_half, pw, d_skip, side_blocks):
    s_len = u.shape[0]
    tb = SSM_BLOCK
    n = s_len // tb
    ns = len(side_blocks)
    side_in_specs, side_shapes, side_sems = _side_gather_specs(side_blocks)

    def body(*refs):
        u_ref, b_ref, c_ref, pw_ref, d_ref = refs[:5]
        side_ins = refs[5:5 + ns]
        y_ref, cm_ref = refs[5 + ns:7 + ns]
        side_outs = refs[7 + ns:7 + 2 * ns]
        s_ref, carry_ref, up_ref, yp_ref, stage_ref = refs[7 + 2 * ns:12 + 2 * ns]
        side_start, side_forward, side_finish = _side_gather_two_level(side_ins, side_outs, *refs[12 + 2 * ns:])
        i = pl.program_id(0)

        @pl.when(i == 0)
        def _():
            carry_ref[...] = jnp.zeros_like(carry_ref)
            side_start()

        pl.when(i == (3 * n) // 4)(side_forward)
        _rows_to_segments(u_ref, stage_ref, up_ref)
        u = up_ref[...]
        ub = u.astype(bf16)
        for half in range(N_HALF):
            s_ref[:, half * HALF_COLS:(half + 1) * HALF_COLS] = _dot(ub[:, half * HALF_W:(half + 1) * HALF_W], b_ref[half])
        _ssm_scan(s_ref, pw_ref, None, reverse=False, unroll=4, width=2 * LANE_CHUNK)
        _ssm_carries(tb - SUBLANES, s_ref, pw_ref, carry_ref, cm_ref, reverse=False)
        _ssm_add_carry(s_ref, pw_ref, cm_ref, reverse=False)
        for half in range(N_HALF):
            cols = slice(half * HALF_W, (half + 1) * HALF_W)
            sb = s_ref[:, half * HALF_COLS:(half + 1) * HALF_COLS].astype(bf16)
            yp_ref[:, cols] = _dot(sb, c_ref[half]) + d_ref[:, cols] * u[:, cols]
        _rows_from_segments(yp_ref, stage_ref, y_ref)

        pl.when(i == n - 1)(side_finish)

    outs = pl.pallas_call(
        body, name="ssm_fwd", grid=(n,),
        in_specs=[_row_spec(tb, SSM_W), _const_spec((N_HALF, HALF_W, HALF_COLS)), _const_spec((N_HALF, HALF_COLS, HALF_W)),
                  _const_spec((SSM_SEG, 2 * N_STATE)), _const_spec((1, SSM_W))] + side_in_specs,
        out_specs=[_row_spec(tb, SSM_W), _row_spec(SUBLANES, 2 * N_STATE)] + side_in_specs,
        out_shape=[_sds((s_len, SSM_W), f32), _sds((n * SUBLANES, 2 * N_STATE), f32)] + side_shapes,
        scratch_shapes=[pltpu.VMEM((tb, 2 * N_STATE), f32), pltpu.VMEM((SUBLANES, 2 * N_STATE), f32),
                        pltpu.VMEM((tb, SSM_W), f32), pltpu.VMEM((tb, SSM_W), f32),
                        pltpu.VMEM((SSM_W // LANES, tb, LANES), f32)] + side_sems,
        compiler_params=_cparams(("arbitrary",)),
    )(u, b_half, c_half, pw, d_skip, *side_blocks)
    return outs[0], outs[1], outs[2:]


def _layer_norm_fwd(r, g, b):
    mu = jnp.mean(r, axis=-1, keepdims=True)
    var = jnp.mean(jnp.square(r - mu), axis=-1, keepdims=True)
    rstd = lax.rsqrt(var + LN_EPS)
    xhat = (r - mu) * rstd
    return xhat, rstd, xhat * g + b


def _layer_norm_bwd(dy, xhat, rstd, g):
    dxh = dy * g
    m1 = jnp.mean(dxh, axis=-1, keepdims=True)
    m2 = jnp.mean(dxh * xhat, axis=-1, keepdims=True)
    return rstd * (dxh - m1 - xhat * m2)


def _branch_outputs(ys_ref, ain_ref, o_ref, wglu_ref, wco_ref, wxo_ref):
    ysb = _gelu(ys_ref[...]).astype(bf16)
    glu = _dot(ysb, wglu_ref[...], NT)
    ga, sb = glu[:, :D_MODEL], jax.nn.sigmoid(glu[:, D_MODEL:])
    ya = _dot(ain_ref[...], wco_ref[...], NT)
    yc = _dot(o_ref[...], wxo_ref[...], NT)
    return ysb, ga, sb, ya, ga * sb, yc


def _mid_fwd(y_ssm, g, ain, ob, x, w_glu_t, w_co_t, w_xo_t, w_out, ln1_g, ln1_b):
    s_len = x.shape[0]
    tm = 2 * TOKEN_TILE
    n = s_len // tm

    def body(ys_ref, g_ref, ain_ref, o_ref, x_ref, wglu_ref, wco_ref, wxo_ref, wout_ref, lg_ref, lb_ref,
             ysbt_ref, mb_ref, xhat_ref, rstd_ref):
        ysb, _, _, ya, yb, yc = _branch_outputs(ys_ref, ain_ref, o_ref, wglu_ref, wco_ref, wxo_ref)
        ysbt_ref[...] = ysb.T
        gt = g_ref[...].astype(f32)
        merged = gt[:, :D_MODEL] * ya + gt[:, D_MODEL:2 * D_MODEL] * yb + gt[:, 2 * D_MODEL:] * yc
        mb = merged.astype(bf16)
        mb_ref[...] = mb
        r1 = ALPHA * x_ref[...] + _dot(mb, wout_ref[...])
        xhat, rstd, _ = _layer_norm_fwd(r1, lg_ref[...], lb_ref[...])
        xhat_ref[...] = xhat
        rstd_ref[...] = rstd

    row_cols = [(D_MODEL, bf16), (D_MODEL, f32), (1, f32)]
    return pl.pallas_call(
        body, name="mid_fwd", grid=(n,),
        in_specs=[_row_spec(tm, SSM_W), _row_spec(tm, GATE_COLS), _row_spec(tm, CONV_W), _row_spec(tm, XATTN_W),
                  _row_spec(tm, D_MODEL), _const_spec((2 * D_MODEL, SSM_W)), _const_spec((D_MODEL, CONV_W)),
                  _const_spec((D_MODEL, XATTN_W)), _const_spec((D_MODEL, D_MODEL)),
                  _const_spec((1, D_MODEL)), _const_spec((1, D_MODEL))],
        out_specs=[_col_spec(SSM_W, tm)] + [_row_spec(tm, c) for c, _ in row_cols],
        out_shape=[_sds((SSM_W, s_len), bf16)] + [_sds((s_len, c), dt) for c, dt in row_cols],
        compiler_params=_cparams(("parallel",)),
    )(y_ssm, g, ain, ob, x, w_glu_t, w_co_t, w_xo_t, w_out, ln1_g, ln1_b)


def _mlp_fwd_bwd(xhat1, tgt, ln1_g, ln1_b, w_up_t, b_up, w_down, b_down, ln2_g, ln2_b):
    s_len = xhat1.shape[0]
    tm = TOKEN_TILE
    n = s_len // tm
    fc = 1024
    nfc = D_FF // fc

    def body(xh_ref, t_ref, l1g_ref, l1b_ref, wup_ref, bup_ref, wdn_ref, bdn_ref, l2g_ref, l2b_ref,
             x1bt_ref, hdn_ref, dr2bt_ref, dpre_ref, dx1_ref,
             loss_ref, dl2g_ref, dl2b_ref, dbdn_ref, dbup_ref, rl_ref):
        i = pl.program_id(0)

        @pl.when(i == 0)
        def _():
            loss_ref[...] = jnp.zeros_like(loss_ref)
            dl2g_ref[...] = jnp.zeros_like(dl2g_ref)
            dl2b_ref[...] = jnp.zeros_like(dl2b_ref)
            dbdn_ref[...] = jnp.zeros_like(dbdn_ref)
            dbup_ref[...] = jnp.zeros_like(dbup_ref)

        x1 = xh_ref[...] * l1g_ref[...] + l1b_ref[...]
        x1b = x1.astype(bf16)
        x1bt_ref[...] = x1b.T
        chunks = [slice(c * fc, (c + 1) * fc) for c in range(nfc)]
        pres = [_dot(x1b, wup_ref[cols, :], NT) for cols in chunks]
        hbs = []
        for cols, pre in zip(chunks, pres):
            rl = jnp.maximum(pre + bup_ref[:, cols], 0.0)
            rl_ref[:, cols] = rl
            hb = (rl * rl).astype(bf16)
            hdn_ref[:, cols] = hb
            hbs.append(hb)
        acc = _dot(hbs[0], wdn_ref[chunks[0], :])
        for cols, hb in zip(chunks[1:], hbs[1:]):
            acc = acc + _dot(hb, wdn_ref[cols, :])
        r2 = ALPHA * x1 + acc + bdn_ref[...]
        xhat2, rstd2, y = _layer_norm_fwd(r2, l2g_ref[...], l2b_ref[...])
        err = y - t_ref[...]
        loss_ref[...] += jnp.sum(jnp.sum(err * err, axis=1, keepdims=True), axis=0, keepdims=True) * (0.5 / D_MODEL)
        dy = err * (1.0 / D_MODEL)
        dl2g_ref[...] += _colsum(dy * xhat2)
        dl2b_ref[...] += _colsum(dy)
        dr2 = _layer_norm_bwd(dy, xhat2, rstd2, l2g_ref[...])
        dbdn_ref[...] += _colsum(dr2)
        dr2b = dr2.astype(bf16)
        dr2bt_ref[...] = dr2b.T
        dhs = [_dot(dr2b, wdn_ref[cols, :], NT) for cols in chunks]
        dpbs = []
        for cols, dh in zip(chunks, dhs):
            dpre = dh * (2.0 * rl_ref[:, cols])
            dbup_ref[:, cols] += _colsum(dpre)
            dpb = dpre.astype(bf16)
            dpre_ref[:, cols] = dpb
            dpbs.append(dpb)
        dacc = _dot(dpbs[0], wup_ref[chunks[0], :])
        for cols, dpb in zip(chunks[1:], dpbs[1:]):
            dacc = dacc + _dot(dpb, wup_ref[cols, :])
        dx1_ref[...] = ALPHA * dr2 + dacc

    acc_shapes = [(1, LANES), (1, D_MODEL), (1, D_MODEL), (1, D_MODEL), (1, D_FF)]
    return pl.pallas_call(
        body, name="mlp_fwd_bwd", grid=(n,),
        in_specs=[_row_spec(tm, D_MODEL), _row_spec(tm, D_MODEL), _const_spec((1, D_MODEL)), _const_spec((1, D_MODEL)),
                  _const_spec((D_FF, D_MODEL)), _const_spec((1, D_FF)), _const_spec((D_FF, D_MODEL)),
                  _const_spec((1, D_MODEL)), _const_spec((1, D_MODEL)), _const_spec((1, D_MODEL))],
        out_specs=([_col_spec(D_MODEL, tm), _row_spec(tm, D_FF), _col_spec(D_MODEL, tm), _row_spec(tm, D_FF),
                    _row_spec(tm, D_MODEL)] + [_acc_spec(s) for s in acc_shapes]),
        out_shape=([_sds((D_MODEL, s_len), bf16), _sds((s_len, D_FF), bf16), _sds((D_MODEL, s_len), bf16),
                    _sds((s_len, D_FF), bf16), _sds((s_len, D_MODEL), f32)] + [_sds(s, f32) for s in acc_shapes]),
        scratch_shapes=[pltpu.VMEM((tm, D_FF), f32)],
        compiler_params=_cparams(("arbitrary",)),
    )(xhat1, tgt, ln1_g, ln1_b, w_up_t, b_up, w_down, b_down, ln2_g, ln2_b)


def _mid_bwd(dx1, xhat1, rstd1, g, ain, ob, y_ssm, ln1_g, w_out, w_glu_t, w_co_t, w_xo_t):
    s_len = dx1.shape[0]
    tm = TOKEN_TILE
    n = s_len // tm

    def body(dx1_ref, xh_ref, rs_ref, g_ref, ain_ref, o_ref, ys_ref, lg_ref, wout_ref, wglu_ref, wco_ref, wxo_ref,
             dxp_ref, dr1bt_ref, dgp_ref, dya_ref, dyc_ref, dglu_ref, dyssm_ref,
             dl1g_ref, dl1b_ref, dbg_ref):
        i = pl.program_id(0)

        @pl.when(i == 0)
        def _():
            dl1g_ref[...] = jnp.zeros_like(dl1g_ref)
            dl1b_ref[...] = jnp.zeros_like(dl1b_ref)
            dbg_ref[...] = jnp.zeros_like(dbg_ref)

        dx1 = dx1_ref[...]
        xhat = xh_ref[...]
        dl1g_ref[...] += _colsum(dx1 * xhat)
        dl1b_ref[...] += _colsum(dx1)
        dr1 = _layer_norm_bwd(dx1, xhat, rs_ref[...], lg_ref[...])
        dxp_ref[...] = ALPHA * dr1
        dr1b = dr1.astype(bf16)
        dr1bt_ref[...] = dr1b.T
        dm = _dot(dr1b, wout_ref[...], NT)

        _, ga, sb, ya, yb, yc = _branch_outputs(ys_ref, ain_ref, o_ref, wglu_ref, wco_ref, wxo_ref)
        gt = g_ref[...].astype(f32)
        branch = (ya, yb, yc)
        for j in range(3):
            cols = slice(j * D_MODEL, (j + 1) * D_MODEL)
            gj = gt[:, cols]
            dgp = dm * branch[j] * gj * (1.0 - gj)
            dbg_ref[:, cols] += _colsum(dgp)
            dgp_ref[:, cols] = dgp.astype(bf16)
        dya_ref[...] = (dm * gt[:, :D_MODEL]).astype(bf16)
        dyc_ref[...] = (dm * gt[:, 2 * D_MODEL:]).astype(bf16)
        dyb = dm * gt[:, D_MODEL:2 * D_MODEL]
        dga = (dyb * sb).astype(bf16)
        dgb = (dyb * ga * sb * (1.0 - sb)).astype(bf16)
        dglu_ref[:, :D_MODEL] = dga
        dglu_ref[:, D_MODEL:] = dgb
        dys = _dot(dga, wglu_ref[:D_MODEL, :]) + _dot(dgb, wglu_ref[D_MODEL:, :])
        dyssm_ref[...] = dys * _gelu_grad(ys_ref[...])

    row_cols = [(GATE_COLS, bf16), (D_MODEL, bf16), (D_MODEL, bf16), (2 * D_MODEL, bf16), (SSM_W, f32)]
    acc_shapes = [(1, D_MODEL), (1, D_MODEL), (1, GATE_COLS)]
    return pl.pallas_call(
        body, name="mid_bwd", grid=(n,),
        in_specs=[_row_spec(tm, D_MODEL), _row_spec(tm, D_MODEL), _row_spec(tm, 1), _row_spec(tm, GATE_COLS),
                  _row_spec(tm, CONV_W), _row_spec(tm, XATTN_W), _row_spec(tm, SSM_W),
                  _const_spec((1, D_MODEL)), _const_spec((D_MODEL, D_MODEL)), _const_spec((2 * D_MODEL, SSM_W)),
                  _const_spec((D_MODEL, CONV_W)), _const_spec((D_MODEL, XATTN_W))],
        out_specs=([_row_spec(tm, D_MODEL), _col_spec(D_MODEL, tm)] + [_row_spec(tm, c) for c, _ in row_cols]
                   + [_acc_spec(s) for s in acc_shapes]),
        out_shape=([_sds((s_len, D_MODEL), f32), _sds((D_MODEL, s_len), bf16)]
                   + [_sds((s_len, c), dt) for c, dt in row_cols] + [_sds(s, f32) for s in acc_shapes]),
        compiler_params=_cparams(("arbitrary",)),
    )(dx1, xhat1, rstd1, g, ain, ob, y_ssm, ln1_g, w_out, w_glu_t, w_co_t, w_xo_t)


def _ssm_bwd(u, dy, cm_all, b_half, c_half, pw, d_skip):
    s_len = u.shape[0]
    tb = SSM_BLOCK
    n = s_len // tb

    def body(u_ref, dy_ref, cm_ref, b_ref, c_ref, pw_ref, d_ref,
             du_ref, db_hbm, dc_hbm, da_ref, dd_ref,
             s_ref, g_ref, gcarry_ref, gcm_ref, db_ref, dc_ref, up_ref, dyp_ref, dup_ref, stage_ref):
        i = pl.program_id(0)

        @pl.when(i == 0)
        def _():
            gcarry_ref[...] = jnp.zeros_like(gcarry_ref)
            db_ref[...] = jnp.zeros_like(db_ref)
            dc_ref[...] = jnp.zeros_like(dc_ref)
            da_ref[...] = jnp.zeros_like(da_ref)
            dd_ref[...] = jnp.zeros_like(dd_ref)

        _rows_to_segments(u_ref, stage_ref, up_ref)
        _rows_to_segments(dy_ref, stage_ref, dyp_ref)
        u = up_ref[...]
        ub = u.astype(bf16)
        dy = dyp_ref[...]
        dyb = dy.astype(bf16)
        dd_ref[...] += _colsum(dy * u)

        for half in range(N_HALF):
            s_ref[:, half * HALF_COLS:(half + 1) * HALF_COLS] = _dot(ub[:, half * HALF_W:(half + 1) * HALF_W], b_ref[half])
        _ssm_scan(s_ref, pw_ref, cm_ref, reverse=False, unroll=True)

        for half in range(N_HALF):
            g_ref[:, half * HALF_COLS:(half + 1) * HALF_COLS] = _dot(dyb[:, half * HALF_W:(half + 1) * HALF_W], c_ref[half], NT)
        _ssm_scan(g_ref, pw_ref, None, reverse=True, unroll=True)
        _ssm_carries(0, g_ref, pw_ref, gcarry_ref, gcm_ref, reverse=True)
        _ssm_add_carry(g_ref, pw_ref, gcm_ref, reverse=True)

        for half in range(N_HALF):
            cols = slice(half * HALF_W, (half + 1) * HALF_W)
            scols = slice(half * HALF_COLS, (half + 1) * HALF_COLS)
            gb = g_ref[:, scols].astype(bf16)
            dup_ref[:, cols] = _dot(gb, b_ref[half], NT) + d_ref[:, cols] * dy[:, cols]
            db_ref[half] += _dot(ub[:, cols], gb, TN)
            dc_ref[half] += _dot(s_ref[:, scols].astype(bf16), dyb[:, cols], TN)
        _rows_from_segments(dup_ref, stage_ref, du_ref)

        for chunk in range(N_STATE // LANE_CHUNK):
            re, im = _state_cols(chunk)
            acc_r = da_ref[:, re]
            acc_i = da_ref[:, im]
            for k in range(SSM_SEG):
                rows = slice(k * SUBLANES, (k + 1) * SUBLANES)
                if k == 0:
                    pr, pi = cm_ref[:, re], cm_ref[:, im]
                else:
                    prev = slice((k - 1) * SUBLANES, k * SUBLANES)
                    pr, pi = s_ref[prev, re], s_ref[prev, im]
                gr, gi = g_ref[rows, re], g_ref[rows, im]
                acc_r = acc_r + (gr * pr + gi * pi)
                acc_i = acc_i + (gi * pr - gr * pi)
            da_ref[:, re] = acc_r
            da_ref[:, im] = acc_i

        @pl.when(i == n - 1)
        def _():
            pltpu.sync_copy(db_ref, db_hbm)
            pltpu.sync_copy(dc_ref, dc_hbm)

    rev = functools.partial(_row_spec, rev_n=n)
    any_spec = pl.BlockSpec(memory_space=pl.ANY)
    state_rows = pltpu.VMEM((tb, 2 * N_STATE), f32)
    seg_rows = pltpu.VMEM((SUBLANES, 2 * N_STATE), f32)
    tok_rows = pltpu.VMEM((tb, SSM_W), f32)
    return pl.pallas_call(
        body, name="ssm_bwd", grid=(n,),
        in_specs=[rev(tb, SSM_W), rev(tb, SSM_W), rev(SUBLANES, 2 * N_STATE),
                  _const_spec((N_HALF, HALF_W, HALF_COLS)), _const_spec((N_HALF, HALF_COLS, HALF_W)),
                  _const_spec((SSM_SEG, 2 * N_STATE)), _const_spec((1, SSM_W))],
        out_specs=[rev(tb, SSM_W), any_spec, any_spec, _acc_spec((SUBLANES, 2 * N_STATE)), _acc_spec((1, SSM_W))],
        out_shape=[_sds((s_len, SSM_W), f32), _sds((N_HALF, HALF_W, HALF_COLS), f32),
                   _sds((N_HALF, HALF_COLS, HALF_W), f32), _sds((SUBLANES, 2 * N_STATE), f32), _sds((1, SSM_W), f32)],
        scratch_shapes=[state_rows, state_rows, seg_rows, seg_rows,
                        pltpu.VMEM((N_HALF, HALF_W, HALF_COLS), f32), pltpu.VMEM((N_HALF, HALF_COLS, HALF_W), f32),
                        tok_rows, tok_rows, tok_rows, pltpu.VMEM((SSM_W // LANES, tb, LANES), f32)],
        compiler_params=_cparams(("arbitrary",)),
    )(u, dy, cm_all, b_half, c_half, pw, d_skip)


def _branch_bwd(dya, dyc, cin, q, kv, k_t, conv_w, w_co_t, w_xo_t, side_blocks):
    s_len = dya.shape[0]
    tm = 2 * TOKEN_TILE
    n = s_len // tm
    halo_blocks = tm // 8
    ns = len(side_blocks)
    conv_tile = _sds((8, CONV_W), f32)
    side_in_specs, side_shapes, side_sems = _side_gather_specs(list(side_blocks) + [conv_tile])

    def body(*refs):
        (dya_ref, dyc_ref, cin_ref, cprev_ref, q_ref, kv_ref, cw_ref, wco_ref, wxo_ref, kt_ref) = refs[:10]
        side_ins = refs[10:10 + ns]
        dconv_ref, dq_ref, dkv_ref = refs[10 + ns:13 + ns]
        side_outs = refs[13 + ns:14 + 2 * ns]
        zs_ref, dczs_ref, dcw_ref = refs[14 + 2 * ns:17 + 2 * ns]
        copies = _side_gather_copies(list(side_ins) + [dcw_ref], side_outs, *refs[17 + 2 * ns:])
        side, conv_side = copies[:ns * N_DEV], copies[ns * N_DEV:]
        i = pl.program_id(0)
        tile = n - 1 - i

        @pl.when(i == 0)
        def _():
            dcw_ref[...] = jnp.zeros_like(dcw_ref)
            dkv_ref[...] = jnp.zeros_like(dkv_ref)
            dczs_ref[tm:tm + 8, :] = jnp.zeros((8, CONV_W), f32)
            for cp in side:
                cp.start()

        cin = cin_ref[...]
        cb, cc, ch = cin[:, :CONV_W], cin[:, CONV_W:2 * CONV_W], cin[:, 2 * CONV_W:]
        z = cc * ch
        cprev = cprev_ref[...]
        zprev = cprev[:, CONV_W:2 * CONV_W] * cprev[:, 2 * CONV_W:]
        zs_ref[0:8, :] = jnp.where(tile == 0, 0.0, zprev)
        zs_ref[8:8 + tm, :] = z
        z1 = zs_ref[pl.ds(7, tm), :]
        z2 = zs_ref[pl.ds(6, tm), :]
        cw = cw_ref[...]
        cz = cw[0:1] * z2 + cw[1:2] * z1 + cw[2:3] * z

        dain = _dot(dya_ref[...], wco_ref[...])
        dcb = dain * cz
        dcz = dain * cb
        dczs_ref[0:tm, :] = dcz
        dcz1 = dczs_ref[pl.ds(1, tm), :]
        dcz2 = dczs_ref[pl.ds(2, tm), :]
        dz = cw[2:3] * dcz + cw[1:2] * dcz1 + cw[0:1] * dcz2
        dczs_ref[tm:tm + 8, :] = dczs_ref[0:8, :]
        dcw_ref[0:1, :] += _colsum(dcz * z2)
        dcw_ref[1:2, :] += _colsum(dcz * z1)
        dcw_ref[2:3, :] += _colsum(dcz * z)
        dconv_ref[:, :CONV_W] = dcb.astype(bf16)
        dconv_ref[:, CONV_W:2 * CONV_W] = (dz * ch).astype(bf16)
        dconv_ref[:, 2 * CONV_W:] = (dz * cc).astype(bf16)

        qb = q_ref[...]
        dob = _dot(dyc_ref[...], wxo_ref[...]).astype(bf16)
        kv = kv_ref[...]
        heads = range(HEADS)
        hcs = [slice(h * HEAD_DIM, (h + 1) * HEAD_DIM) for h in heads]
        vcs = [slice(XATTN_W + h * HEAD_DIM, XATTN_W + (h + 1) * HEAD_DIM) for h in heads]
        s_t = [_dot(kv[:, hcs[h]], qb[:, hcs[h]], NT) * (HEAD_DIM ** -0.5) for h in heads]
        dp_t = [_dot(kv[:, vcs[h]], dob[:, hcs[h]], NT) for h in heads]
        e_t = [jnp.exp(s_t[h] - jnp.max(s_t[h], axis=0, keepdims=True)) for h in heads]
        p_t = [e_t[h] / jnp.sum(e_t[h], axis=0, keepdims=True) for h in heads]
        dv = [_dot(p_t[h].astype(bf16), dob[:, hcs[h]]) for h in heads]
        ds_t = [(p_t[h] * (dp_t[h] - jnp.sum(dp_t[h] * p_t[h], axis=0, keepdims=True)) * (HEAD_DIM ** -0.5)).astype(bf16)
                for h in heads]
        dk = [_dot(ds_t[h], qb[:, hcs[h]]) for h in heads]
        dq_t = [_dot(kt_ref[hcs[h], :], ds_t[h]) for h in heads]
        dq_ref[...] = jnp.concatenate(dq_t, axis=0).T.astype(bf16)
        dkv_ref[...] += jnp.concatenate(dk + dv, axis=1)

        @pl.when(i == n - 1)
        def _():
            for cp in conv_side:
                cp.start()
            for cp in side + conv_side:
                cp.wait()

    rev = functools.partial(_row_spec, rev_n=n)
    prev_spec = pl.BlockSpec((8, 3 * CONV_W), lambda i: (jnp.maximum((n - 1 - i) * halo_blocks - 1, 0), 0))
    outs = pl.pallas_call(
        body, name="branch_bwd", grid=(n,),
        in_specs=[rev(tm, D_MODEL), rev(tm, D_MODEL), rev(tm, 3 * CONV_W), prev_spec, rev(tm, XATTN_W),
                  _const_spec((MEM_LEN, 2 * XATTN_W)), _const_spec((3, CONV_W)), _const_spec((D_MODEL, CONV_W)),
                  _const_spec((D_MODEL, XATTN_W)), _const_spec((XATTN_W, MEM_LEN))] + side_in_specs[:ns],
        out_specs=[rev(tm, 3 * CONV_W), rev(tm, XATTN_W), _acc_spec((MEM_LEN, 2 * XATTN_W))] + side_in_specs,
        out_shape=[_sds((s_len, 3 * CONV_W), bf16), _sds((s_len, XATTN_W), bf16),
                   _sds((MEM_LEN, 2 * XATTN_W), f32)] + side_shapes,
        scratch_shapes=[pltpu.VMEM((tm + 8, CONV_W), f32), pltpu.VMEM((tm + 8, CONV_W), f32),
                        pltpu.VMEM((8, CONV_W), f32)] + side_sems,
        compiler_params=_cparams(("arbitrary",)),
    )(dya, dyc, cin, cin, q, kv, conv_w, w_co_t, w_xo_t, k_t, *side_blocks)
    return outs[0], outs[1], outs[2], outs[3:]


def _in_proj_bwd(dgp, dconv, du, dq, dxp, w_in_t):
    s_len = dgp.shape[0]
    tm = 2 * TOKEN_TILE
    n = s_len // tm

    def body(dgp_ref, dconv_ref, du_ref, dq_ref, dxp_ref, win_ref, dx_ref, dproj_ref):
        dproj = jnp.concatenate([dgp_ref[...], dconv_ref[...], du_ref[...].astype(bf16), dq_ref[...]], axis=1)
        dproj_ref[...] = dproj
        dx_ref[...] = dxp_ref[...] + _dot(dproj, win_ref[...])

    return pl.pallas_call(
        body, name="in_proj_bwd", grid=(n,),
        in_specs=[_row_spec(tm, GATE_COLS), _row_spec(tm, 3 * CONV_W), _row_spec(tm, SSM_W), _row_spec(tm, XATTN_W),
                  _row_spec(tm, D_MODEL), _const_spec((IN_COLS, D_MODEL))],
        out_specs=[_row_spec(tm, D_MODEL), _row_spec(tm, IN_COLS)],
        out_shape=[_sds((s_len, D_MODEL), f32), _sds((s_len, IN_COLS), bf16)],
        compiler_params=_cparams(("parallel",)),
    )(dgp, dconv, du, dq, dxp, w_in_t)


N_CHIP = 4
CHIP_STEPS = [(1, 1), (1, 0), (0, 1), (0, 0)]


def _flip(v, d):
    return 1 - v if d else v


def _chip_order():
    x, y, _ = _mesh_place()
    return jnp.stack([2 * _flip(x, dx) + _flip(y, dy) for dx, dy in CHIP_STEPS]).astype(jnp.int32)


def _weight_grads_scatter(problems, name):
    dims = []
    first = 0
    for a_t, b, tm, tt in problems:
        m, s_len = a_t.shape
        w = b.shape[1] // N_DEV
        tm, tt = min(tm, m), min(tt, s_len)
        assert m % tm == 0 and s_len % tt == 0
        nm, nt = m // tm, s_len // tt
        dims.append(dict(m=m, w=w, tm=tm, tt=tt, nm=nm, nt=nt, first=first, steps=N_CHIP * nm * nt))
        first += N_CHIP * nm * nt
    n_prob, total = len(problems), first
    n_scratch = 9

    def place(d, s):
        local = jnp.clip(s - d["first"], 0, d["steps"] - 1)
        return local // (d["nm"] * d["nt"]), (local // d["nt"]) % d["nm"], local % d["nt"]

    def run(d, q, im, t, a_ref, b_ref, recv_ref, acc_ref, send_ref, sib_ref, stash_ref,
            d2d_send, d2d_recv, ici_send, ici_recv, local_sem):
        tm, w, nm, nt = d["tm"], d["w"], d["nm"], d["nt"]
        x, y, c = _mesh_place()
        mesh_id = pl.DeviceIdType.MESH

        @pl.when(t == 0)
        def _():
            acc_ref[...] = jnp.zeros_like(acc_ref)

        acc_ref[...] += _dot(a_ref[...], b_ref[...])

        def to_sibling(qq, imm):
            rows = pl.ds(pl.multiple_of(imm * tm, tm), tm)
            return pltpu.make_async_remote_copy(
                src_ref=send_ref.at[qq, 0, rows, :], dst_ref=sib_ref.at[qq, rows, :],
                send_sem=d2d_send.at[qq], recv_sem=d2d_recv.at[qq, imm],
                device_id=(x, y, 1 - c), device_id_type=mesh_id)

        def finish_tile(qq, imm):
            rows = pl.ds(pl.multiple_of(imm * tm, tm), tm)
            to_sibling(qq, imm).wait_recv()
            both = stash_ref[...] + sib_ref[qq, rows, :].astype(f32)
            send_ref[qq, 1, rows, :] = both.astype(bf16)
            for step, (dx, dy) in enumerate(CHIP_STEPS):
                @pl.when(qq == step)
                def _(step=step, dx=dx, dy=dy):
                    src, dst = send_ref.at[step, 1, rows, :], recv_ref.at[step, rows, :]
                    if dx or dy:
                        pltpu.make_async_remote_copy(
                            src_ref=src, dst_ref=dst, send_sem=ici_send.at[step], recv_sem=ici_recv.at[step],
                            device_id=(_flip(x, dx), _flip(y, dy), c), device_id_type=mesh_id).start()
                    else:
                        pltpu.make_async_copy(src, dst, local_sem).start()

        @pl.when(t == nt - 1)
        def _():
            tile = q * nm + im

            @pl.when(tile > 0)
            def _():
                finish_tile((tile - 1) // nm, (tile - 1) % nm)

            rows = pl.ds(pl.multiple_of(im * tm, tm), tm)
            for core in (0, 1):
                @pl.when(c == core)
                def _(core=core):
                    other = 1 - core
                    send_ref[q, 0, rows, :] = acc_ref[:, other * w:(other + 1) * w].astype(bf16)
                    stash_ref[...] = acc_ref[:, core * w:(core + 1) * w]
            to_sibling(q, im).start()

            @pl.when(tile == N_CHIP * nm - 1)
            def _():
                finish_tile(q, im)
                for step, (dx, dy) in enumerate(CHIP_STEPS):
                    pltpu.make_async_remote_copy(
                        src_ref=send_ref.at[step, 0], dst_ref=sib_ref.at[step],
                        send_sem=d2d_send.at[step], recv_sem=d2d_recv.at[step, 0],
                        device_id=(x, y, 1 - c), device_id_type=mesh_id).wait_send()
                    src, dst = send_ref.at[step, 1], recv_ref.at[step]
                    if dx or dy:
                        pltpu.make_async_remote_copy(
                            src_ref=src, dst_ref=dst, send_sem=ici_send.at[step], recv_sem=ici_recv.at[step],
                            device_id=(_flip(x, dx), _flip(y, dy), c), device_id_type=mesh_id).wait()
                    else:
                        pltpu.make_async_copy(src, dst, local_sem).wait()

    def body(order_ref, *refs):
        del order_ref
        s = pl.program_id(0)
        operands, rest = refs[:2 * n_prob], refs[2 * n_prob:]
        results, scratch = rest[:n_prob], rest[n_prob:]
        for k, d in enumerate(dims):
            @pl.when((s >= d["first"]) & (s < d["first"] + d["steps"]))
            def _(k=k, d=d):
                q, im, t = place(d, s)
                run(d, q, im, t, operands[2 * k], operands[2 * k + 1], results[k],
                    *scratch[n_scratch * k:n_scratch * (k + 1)])

    in_specs, scratch_shapes = [], []
    for d in dims:
        def a_map(s, order, d=d):
            _, im, t = place(d, s)
            return im, t

        def b_map(s, order, d=d):
            q, _, t = place(d, s)
            return t, order[q]

        in_specs += [pl.BlockSpec((d["tm"], d["tt"]), a_map), pl.BlockSpec((d["tt"], 2 * d["w"]), b_map)]
        scratch_shapes += [pltpu.VMEM((d["tm"], 2 * d["w"]), f32), pltpu.VMEM((N_CHIP, 2, d["m"], d["w"]), bf16),
                           pltpu.VMEM((N_CHIP, d["m"], d["w"]), bf16), pltpu.VMEM((d["tm"], d["w"]), f32),
                           pltpu.SemaphoreType.DMA((N_CHIP,)), pltpu.SemaphoreType.DMA((N_CHIP, d["nm"])),
                           pltpu.SemaphoreType.DMA((N_CHIP - 1,)), pltpu.SemaphoreType.DMA((N_CHIP - 1,)),
                           pltpu.SemaphoreType.DMA]
    grid_spec = pltpu.PrefetchScalarGridSpec(
        num_scalar_prefetch=1, grid=(total,), in_specs=in_specs,
        out_specs=[pl.BlockSpec(memory_space=pl.ANY)] * n_prob, scratch_shapes=scratch_shapes)
    return pl.pallas_call(
        body, name=name, grid_spec=grid_spec,
        out_shape=[_sds((N_CHIP, d["m"], d["w"]), bf16) for d in dims],
        compiler_params=_cparams(("arbitrary",)),
    )(_chip_order(), *[op for a_t, b, _, _ in problems for op in (a_t, b)])


def _adamw(w, g, m, v):
    m = ADAM_B1 * m + (1.0 - ADAM_B1) * g
    v = ADAM_B2 * v + (1.0 - ADAM_B2) * jnp.square(g)
    m_hat = m / (1.0 - ADAM_B1 ** ADAM_STEP)
    v_hat = v / (1.0 - ADAM_B2 ** ADAM_STEP)
    delta = -ADAM_LR * (m_hat / (jnp.sqrt(v_hat) + ADAM_EPS) + ADAM_WD * w)
    return delta, m, v


def _sum_parts(p_ref):
    g = p_ref[0].astype(f32)
    for j in range(1, p_ref.shape[0]):
        g = g + p_ref[j].astype(f32)
    return g


def _adamw_update(w, m, v, parts, name, transposed):
    rows, cols = w.shape
    n_parts = parts.shape[0]
    if transposed:
        tc = 256
        steps = cols // tc
        p_spec = pl.BlockSpec((n_parts, tc, rows), lambda i: (0, i, 0))
        spec = pl.BlockSpec((rows, tc), lambda i: (0, i))
    else:
        tr = next(t for t in (256, 128, 64, 32, 16, 8) if rows % t == 0)
        steps = rows // tr
        p_spec = pl.BlockSpec((n_parts, tr, cols), lambda i: (0, i, 0))
        spec = pl.BlockSpec((tr, cols), lambda i: (i, 0))

    def body(w_ref, p_ref, m_ref, v_ref, g_ref, d_ref, nm_ref, nv_ref):
        g = _sum_parts(p_ref)
        if transposed:
            g = g.T
        g_ref[...] = g
        d_ref[...], nm_ref[...], nv_ref[...] = _adamw(w_ref[...], g, m_ref[...], v_ref[...])

    return pl.pallas_call(
        body, name=name, grid=(steps,),
        in_specs=[spec, p_spec, spec, spec], out_specs=[spec] * 4,
        out_shape=[_sds((rows, cols), f32)] * 4,
        compiler_params=_cparams(("parallel",)),
    )(w, parts, m, v)


SMALL_GROUPS = [
    (["b_gate", "ln1_g", "ln1_b", "b_up", "b_down", "ln2_g", "ln2_b", "ssm_d"], 1),
    (["ssm_lam_re", "ssm_lam_im", "ssm_c_re", "ssm_c_im", "ssm_b_re", "ssm_b_im"], 0),
    (["conv_w"], 0),
    (["ssm_log_dt"], 0),
]


def _sum_small(group_parts):
    def body(*refs):
        n = len(refs) // 2
        for p_ref, o_ref in zip(refs[:n], refs[n:]):
            o_ref[...] = _sum_parts(p_ref)

    return pl.pallas_call(
        body, name="sum_small",
        out_shape=[_sds(p.shape[1:], f32) for p in group_parts],
        compiler_params=_cparams(),
    )(*group_parts)


def _adamw_small(ws, ms, vs, group_sums):
    names = [k for group, _ in SMALL_GROUPS for k in group]
    n = len(names)

    def body(*refs):
        w_refs, m_refs, v_refs = (dict(zip(names, refs[j * n:(j + 1) * n])) for j in range(3))
        p_refs = refs[3 * n:3 * n + len(SMALL_GROUPS)]
        out_refs = [dict(zip(names, refs[3 * n + len(SMALL_GROUPS) + j * n:][:n])) for j in range(4)]
        for (group, axis), p_ref in zip(SMALL_GROUPS, p_refs):
            total = p_ref[...]
            off = 0
            for k in group:
                size = SMALL[k][axis]
                g = total[:, off:off + size] if axis == 1 else total[off:off + size, :]
                off += size
                d, nm, nv = _adamw(w_refs[k][...], g, m_refs[k][...], v_refs[k][...])
                for j, val in enumerate((g, d, nm, nv)):
                    out_refs[j][k][...] = val

    res = pl.pallas_call(
        body, name="adamw_small",
        out_shape=[_sds(SMALL[k], f32) for _ in range(4) for k in names],
        compiler_params=_cparams(),
    )(*[ws[k] for k in names], *[ms[k] for k in names], *[vs[k] for k in names], *group_sums)
    return [dict(zip(names, res[j * n:(j + 1) * n])) for j in range(4)]


def _ssm_discretize(lam_re, lam_im, log_dt, b_re, b_im):
    dt = jnp.exp(log_dt)[:, None]
    mag = jnp.exp(lam_re * dt)
    abar_r = mag * jnp.cos(lam_im * dt)
    abar_i = mag * jnp.sin(lam_im * dt)
    den = lam_re * lam_re + lam_im * lam_im
    nr = abar_r - 1.0
    ni = abar_i
    kr = (nr * lam_re + ni * lam_im) / den
    ki = (ni * lam_re - nr * lam_im) / den
    bbar_r = kr[:, None, :] * b_re - ki[:, None, :] * b_im
    bbar_i = kr[:, None, :] * b_im + ki[:, None, :] * b_re
    return abar_r, abar_i, bbar_r, bbar_i


def _state_layout(re, im):
    parts = []
    for half in range(N_HALF):
        cols = slice(half * HALF_STATE, (half + 1) * HALF_STATE)
        parts += [re[..., cols], im[..., cols]]
    return jnp.concatenate(parts, axis=-1)


def _state_unlayout(a):
    re = jnp.concatenate([a[..., _half_cols(h)[0]] for h in range(N_HALF)], axis=-1)
    im = jnp.concatenate([a[..., _half_cols(h)[1]] for h in range(N_HALF)], axis=-1)
    return re, im


def _abar_powers(abar_r, abar_i):
    pr, pi = abar_r.reshape(1, N_STATE), abar_i.reshape(1, N_STATE)
    while pr.shape[0] < SSM_SEG:
        tr, ti = pr[-1:], pi[-1:]
        pr, pi = (jnp.concatenate([pr, pr * tr - pi * ti], axis=0), jnp.concatenate([pi, pr * ti + pi * tr], axis=0))
    return _state_layout(pr, pi)


HALF_GROUPS = SSM_GROUPS // N_HALF


def _half_block_diag(blocks):
    _, r, c = blocks.shape
    eye = jnp.eye(HALF_GROUPS, dtype=blocks.dtype)
    b4 = blocks.reshape(N_HALF, HALF_GROUPS, r, c)
    return jnp.einsum("ngrc,gk->ngrkc", b4, eye).reshape(N_HALF, HALF_GROUPS * r, HALF_GROUPS * c)


def _half_diag_blocks(mat, r, c):
    eye = jnp.eye(HALF_GROUPS, dtype=mat.dtype)
    m5 = mat.reshape(N_HALF, HALF_GROUPS, r, HALF_GROUPS, c)
    return jnp.einsum("ngrkc,gk->ngrc", m5, eye).reshape(SSM_GROUPS, r, c)


BIG = ["w_in", "w_conv_out", "w_glu", "w_kv", "w_xattn_out", "w_out", "w_up", "w_down"]
GATHER_TRANSPOSED = ["w_conv_out", "w_glu", "w_xattn_out", "w_up"]
PARTS_TRANSPOSED = ["w_in", "w_kv", "w_out", "w_down"]
SMALL = {"b_gate": (1, GATE_COLS), "conv_w": (3, CONV_W), "ssm_lam_re": (SSM_GROUPS, SSM_STATE),
         "ssm_lam_im": (SSM_GROUPS, SSM_STATE), "ssm_log_dt": (1, SSM_GROUPS),
         "ssm_b_re": (SSM_W, SSM_STATE), "ssm_b_im": (SSM_W, SSM_STATE),
         "ssm_c_re": (SSM_W, SSM_STATE), "ssm_c_im": (SSM_W, SSM_STATE), "ssm_d": (1, SSM_W),
         "ln1_g": (1, D_MODEL), "ln1_b": (1, D_MODEL), "b_up": (1, D_FF), "b_down": (1, D_MODEL),
         "ln2_g": (1, D_MODEL), "ln2_b": (1, D_MODEL)}
WEIGHTS = ["w_in", "b_gate", "conv_w", "w_conv_out", "ssm_lam_re", "ssm_lam_im", "ssm_log_dt", "ssm_b_re", "ssm_b_im",
           "ssm_c_re", "ssm_c_im", "ssm_d", "w_glu", "w_kv", "w_xattn_out", "w_out", "ln1_g", "ln1_b", "w_up", "b_up",
           "w_down", "b_down", "ln2_g", "ln2_b"]


def _local_step(x, mem, tgt, full, late, small):
    lam_re, lam_im, log_dt = small["ssm_lam_re"], small["ssm_lam_im"], small["ssm_log_dt"].reshape(SSM_GROUPS)
    c_shape = (SSM_GROUPS, SSM_GROUP, SSM_STATE)
    disc, disc_vjp = jax.vjp(_ssm_discretize, lam_re, lam_im, log_dt,
                             small["ssm_b_re"].reshape(c_shape), small["ssm_b_im"].reshape(c_shape))
    abar_r, abar_i, bbar_r, bbar_i = disc
    pw = _abar_powers(abar_r, abar_i)
    c_re, c_im = small["ssm_c_re"].reshape(c_shape), small["ssm_c_im"].reshape(c_shape)
    b_half = jnp.concatenate([_half_block_diag(bbar_r), _half_block_diag(bbar_i)], axis=2).astype(bf16)
    c_half = jnp.concatenate([_half_block_diag(c_re.transpose(0, 2, 1)), -_half_block_diag(c_im.transpose(0, 2, 1))],
                             axis=1).astype(bf16)

    s_len = x.shape[0]
    stack = lambda a: a.reshape(-1, a.shape[-1])
    kv, k_t, memb = _kv_proj(mem, full["w_kv"])
    (xbt, g, cin, u, q, ain, ob, aint, obt), side = _in_proj(
        x, full["w_in"], small["b_gate"], small["conv_w"], kv,
        [late[k] for k in ("w_glu", "w_conv_out", "w_xattn_out", "w_out", "w_up")])
    w_glu_t, w_co_t, w_xo_t, w_out, w_up_t = (stack(a) for a in side)
    y_ssm, cm_all, side = _ssm_fwd(u, b_half, c_half, pw, small["ssm_d"], [late["w_down"]])
    w_down = stack(side[0])
    ysbt, mb, xhat1, rstd1 = _mid_fwd(y_ssm, g, ain, ob, x, w_glu_t, w_co_t, w_xo_t, w_out,
                                      small["ln1_g"], small["ln1_b"])
    (x1bt, hdn, dr2bt, dpre, dx1, loss, dl2g, dl2b, dbdn, dbup) = _mlp_fwd_bwd(
        xhat1, tgt, small["ln1_g"], small["ln1_b"], w_up_t, small["b_up"], w_down,
        small["b_down"], small["ln2_g"], small["ln2_b"])
    (dxp, dr1bt, dgp, dya, dyc, dglu, dyssm, dl1g, dl1b, dbg) = _mid_bwd(
        dx1, xhat1, rstd1, g, ain, ob, y_ssm, small["ln1_g"], w_out, w_glu_t, w_co_t, w_xo_t)
    du, db_half, dc_half, da8, dd = _ssm_bwd(u, dyssm, cm_all, b_half, c_half, pw, small["ssm_d"])
    dabar_r, dabar_i = _state_unlayout(jnp.sum(da8, axis=0))
    dbbar_r = _half_diag_blocks(db_half[:, :, :HALF_STATE], SSM_GROUP, SSM_STATE)
    dbbar_i = _half_diag_blocks(db_half[:, :, HALF_STATE:], SSM_GROUP, SSM_STATE)
    g_shape = (SSM_GROUPS, SSM_STATE)
    dlam_re, dlam_im, dlog_dt, db_re, db_im = disc_vjp(
        (dabar_r.reshape(g_shape), dabar_i.reshape(g_shape), dbbar_r, dbbar_i))
    dc_re = _half_diag_blocks(dc_half[:, :HALF_STATE, :], SSM_STATE, SSM_GROUP).transpose(0, 2, 1)
    dc_im = -_half_diag_blocks(dc_half[:, HALF_STATE:, :], SSM_STATE, SSM_GROUP).transpose(0, 2, 1)

    small_grads = {
        "b_gate": dbg, "ssm_lam_re": dlam_re, "ssm_lam_im": dlam_im, "ssm_log_dt": dlog_dt,
        "ssm_b_re": db_re, "ssm_b_im": db_im, "ssm_c_re": dc_re, "ssm_c_im": dc_im, "ssm_d": dd,
        "ln1_g": dl1g, "ln1_b": dl1b, "b_up": dbup, "b_down": dbdn, "ln2_g": dl2g, "ln2_b": dl2b,
    }
    small_grads = {k: a.reshape(SMALL[k]) for k, a in small_grads.items()}
    groups = [(group, axis) for group, axis in SMALL_GROUPS if group != ["conv_w"]]
    stacks = [jnp.concatenate([small_grads[k] for k in group], axis=axis) if len(group) > 1 else small_grads[group[0]]
              for group, axis in groups]
    n_rowvec = stacks[0].shape[1]
    stacks[0] = jnp.concatenate([stacks[0], loss], axis=1)
    dense = lambda a: a.reshape(-1, LANES) if a.size % LANES == 0 else a
    dconv, dq, dkv, group_parts = _branch_bwd(dya, dyc, cin, q, kv, k_t, small["conv_w"], w_co_t, w_xo_t,
                                              [dense(a) for a in stacks])
    dx, dproj = _in_proj_bwd(dgp, dconv, du, dq, dxp, full["w_in"])
    tm, tt = 512, 2048
    products = {
        "w_down": (dr2bt, hdn, tm, tt), "w_up": (x1bt, dpre, tm, tt), "w_out": (dr1bt, mb, tm, s_len),
        "w_glu": (ysbt, dglu, tm, s_len), "w_conv_out": (aint, dya, tm, s_len), "w_xattn_out": (obt, dyc, tm, s_len),
        "w_kv": (dkv.T.astype(bf16), memb, D_MODEL, MEM_LEN), "w_in": (xbt, dproj, tm, tt),
    }
    recv = {k: _weight_grads_scatter([problem], "d" + k)[0] for k, problem in products.items()}
    sums = _sum_small(group_parts)
    group_sums = dict(zip([tuple(group) for group, _ in groups], [s.reshape(a.shape) for s, a in zip(sums, stacks)]))
    group_sums[("conv_w",)] = sums[-1][0:3]
    first = tuple(groups[0][0])
    loss_all = group_sums[first][0, n_rowvec]
    group_sums[first] = group_sums[first][:, :n_rowvec]
    return loss_all, dx, recv, [group_sums[tuple(group)] for group, _ in SMALL_GROUPS]


def kernel(x, mem, w_in, b_gate, conv_w, w_conv_out, ssm_lam_re, ssm_lam_im, ssm_log_dt, ssm_b_re, ssm_b_im, ssm_c_re, ssm_c_im, ssm_d, w_glu, w_kv, w_xattn_out, w_out, ln1_g, ln1_b, w_up, b_up, w_down, b_down, ln2_g, ln2_b, loss_target, m_w_in, m_b_gate, m_conv_w, m_w_conv_out, m_ssm_lam_re, m_ssm_lam_im, m_ssm_log_dt, m_ssm_b_re, m_ssm_b_im, m_ssm_c_re, m_ssm_c_im, m_ssm_d, m_w_glu, m_w_kv, m_w_xattn_out, m_w_out, m_ln1_g, m_ln1_b, m_w_up, m_b_up, m_w_down, m_b_down, m_ln2_g, m_ln2_b, v_w_in, v_b_gate, v_conv_w, v_w_conv_out, v_ssm_lam_re, v_ssm_lam_im, v_ssm_log_dt, v_ssm_b_re, v_ssm_b_im, v_ssm_c_re, v_ssm_c_im, v_ssm_d, v_w_glu, v_w_kv, v_w_xattn_out, v_w_out, v_ln1_g, v_ln1_b, v_w_up, v_b_up, v_w_down, v_b_down, v_ln2_g, v_ln2_b):
    w = dict(w_in=w_in, b_gate=b_gate, conv_w=conv_w, w_conv_out=w_conv_out, ssm_lam_re=ssm_lam_re,
             ssm_lam_im=ssm_lam_im, ssm_log_dt=ssm_log_dt, ssm_b_re=ssm_b_re, ssm_b_im=ssm_b_im, ssm_c_re=ssm_c_re,
             ssm_c_im=ssm_c_im, ssm_d=ssm_d, w_glu=w_glu, w_kv=w_kv, w_xattn_out=w_xattn_out, w_out=w_out,
             ln1_g=ln1_g, ln1_b=ln1_b, w_up=w_up, b_up=b_up, w_down=w_down, b_down=b_down, ln2_g=ln2_g, ln2_b=ln2_b)
    m = dict(w_in=m_w_in, b_gate=m_b_gate, conv_w=m_conv_w, w_conv_out=m_w_conv_out, ssm_lam_re=m_ssm_lam_re,
             ssm_lam_im=m_ssm_lam_im, ssm_log_dt=m_ssm_log_dt, ssm_b_re=m_ssm_b_re, ssm_b_im=m_ssm_b_im,
             ssm_c_re=m_ssm_c_re, ssm_c_im=m_ssm_c_im, ssm_d=m_ssm_d, w_glu=m_w_glu, w_kv=m_w_kv,
             w_xattn_out=m_w_xattn_out, w_out=m_w_out, ln1_g=m_ln1_g, ln1_b=m_ln1_b, w_up=m_w_up, b_up=m_b_up,
             w_down=m_w_down, b_down=m_b_down, ln2_g=m_ln2_g, ln2_b=m_ln2_b)
    v = dict(w_in=v_w_in, b_gate=v_b_gate, conv_w=v_conv_w, w_conv_out=v_w_conv_out, ssm_lam_re=v_ssm_lam_re,
             ssm_lam_im=v_ssm_lam_im, ssm_log_dt=v_ssm_log_dt, ssm_b_re=v_ssm_b_re, ssm_b_im=v_ssm_b_im,
             ssm_c_re=v_ssm_c_re, ssm_c_im=v_ssm_c_im, ssm_d=v_ssm_d, w_glu=v_w_glu, w_kv=v_w_kv,
             w_xattn_out=v_w_xattn_out, w_out=v_w_out, ln1_g=v_ln1_g, ln1_b=v_ln1_b, w_up=v_w_up, b_up=v_b_up,
             w_down=v_w_down, b_down=v_b_down, ln2_g=v_ln2_g, ln2_b=v_ln2_b)
    out_shapes = {k: a.shape for k, a in w.items()}
    swapped = ("w_in", "ssm_b_re", "ssm_b_im")

    def shard2d(k, a):
        if k in swapped:
            a = jnp.swapaxes(a, -1, -2)
        if k in SMALL:
            return a.reshape((3, CONV_W // N_DEV) if k == "conv_w" else SMALL[k])
        return a[0]

    def result(k, a):
        if k in swapped:
            shape = out_shapes[k]
            return jnp.swapaxes(a.reshape(shape[:-2] + (shape[-1], shape[-2])), -1, -2)
        return a.reshape(out_shapes[k])

    w, m, v = ({k: shard2d(k, a) for k, a in d.items()} for d in (w, m, v))

    shards = {k: w[k].T.astype(bf16) if k in GATHER_TRANSPOSED else w[k].astype(bf16) for k in BIG}
    conv_pad = jnp.pad(w["conv_w"], ((0, 5), (0, LANES - CONV_W // N_DEV)))
    early = ["w_in", "w_kv"]
    gathered = _all_gather([shards[k] for k in early] + [conv_pad], "gather_weights")
    full = {k: a.reshape(-1, a.shape[-1]) for k, a in zip(early, gathered[:-1])}
    late = {k: shards[k] for k in BIG if k not in early}
    conv_full = gathered[-1][:, :3, :CONV_W // N_DEV].transpose(1, 0, 2).reshape(3, CONV_W)
    small = {k: (conv_full if k == "conv_w" else w[k]) for k in SMALL}

    loss, dx, recv, group_sums = _local_step(x[0], mem[0], loss_target[0], full, late, small)

    grads, deltas, new_m, new_v = {}, {}, {}, {}
    for k in BIG:
        res = _adamw_update(w[k], m[k], v[k], recv[k], "adamw_" + k, transposed=k in PARTS_TRANSPOSED)
        grads[k], deltas[k], new_m[k], new_v[k] = res

    widen = lambda k, a: jnp.tile(a, (1, N_DEV)) if k == "conv_w" else a
    res = _adamw_small(small, {k: widen(k, m[k]) for k in SMALL}, {k: widen(k, v[k]) for k in SMALL}, group_sums)
    dev = _slot(_mesh_place())
    for d, small_res in zip((grads, deltas, new_m, new_v), res):
        for k, a in small_res.items():
            if k == "conv_w":
                a = lax.dynamic_slice_in_dim(a, dev * (CONV_W // N_DEV), CONV_W // N_DEV, axis=1)
            d[k] = a

    outs = [loss, dx[None]]
    for d in (grads, deltas, new_m, new_v):
        outs += [result(k, d[k]) for k in WEIGHTS]
    return tuple(outs)
```

```python
import functools
import math

import jax
import jax.numpy as jnp
from jax import lax
from jax.experimental import pallas as pl
from jax.experimental.pallas import tpu as pltpu

f32 = jnp.float32
bf16 = jnp.bfloat16

D_MODEL = 1024
MEM_LEN = 256
GATE_COLS = 3 * D_MODEL
CONV_W = 512
SSM_W = 512
XATTN_W = 512
HEADS = 4
HEAD_DIM = 128
D_FF = 4096
IN_COLS = GATE_COLS + 3 * CONV_W + SSM_W + XATTN_W
SSM_GROUPS = 32
SSM_GROUP = 16
SSM_STATE = 64
N_STATE = SSM_GROUPS * SSM_STATE
ALPHA = 2.0 ** 0.25
LN_EPS = 1e-5
N_DEV = 8

ADAM_LR = 0.001
ADAM_B1 = 0.9
ADAM_B2 = 0.999
ADAM_EPS = 1e-08
ADAM_WD = 0.01
ADAM_STEP = 10

VMEM_LIMIT_V7X = 56 * 2 ** 20
SUBLANES = 8
LANES = 128

TOKEN_TILE = 256
SSM_BLOCK = 512
SSM_SEG = SSM_BLOCK // SUBLANES
LANE_CHUNK = 256
N_HALF = 2
HALF_W = SSM_W // N_HALF
HALF_STATE = N_STATE // N_HALF
HALF_COLS = 2 * HALF_STATE

NT = (((1,), (1,)), ((), ()))
TN = (((0,), (0,)), ((), ()))
NN = (((1,), (0,)), ((), ()))


def _dot(a, b, dims=NN):
    return lax.dot_general(a, b, dims, preferred_element_type=f32)


def _cparams(sem=None):
    return pltpu.CompilerParams(dimension_semantics=sem, vmem_limit_bytes=VMEM_LIMIT_V7X)


def _row_spec(tm, cols, rev_n=None):
    if rev_n is None:
        return pl.BlockSpec((tm, cols), lambda i: (i, 0))
    return pl.BlockSpec((tm, cols), lambda i: (rev_n - 1 - i, 0))


def _col_spec(rows, tm):
    return pl.BlockSpec((rows, tm), lambda i: (0, i))


def _const_spec(shape):
    nd = len(shape)
    return pl.BlockSpec(shape, lambda *_: (0,) * nd, pipeline_mode=pl.Buffered(1))


def _acc_spec(shape):
    nd = len(shape)
    return pl.BlockSpec(shape, lambda *_: (0,) * nd)


def _sds(shape, dtype):
    return jax.ShapeDtypeStruct(shape, dtype)


def _gelu(x):
    c = math.sqrt(2.0 / math.pi)
    return 0.5 * x * (1.0 + jnp.tanh(c * (x + 0.044715 * x * x * x)))


def _gelu_grad(x):
    c = math.sqrt(2.0 / math.pi)
    t = jnp.tanh(c * (x + 0.044715 * x * x * x))
    return 0.5 * (1.0 + t) + 0.5 * x * (1.0 - t * t) * c * (1.0 + 3.0 * 0.044715 * x * x)


def _colsum(a):
    return jnp.sum(a, axis=0, keepdims=True)


def _mesh_place():
    return lax.axis_index("x"), lax.axis_index("y"), lax.axis_index("c")


def _slot(p):
    return 4 * p[0] + 2 * p[1] + p[2]


def _other_devices(me):
    x, y, c = me
    flip = lambda v, d: 1 - v if d else v
    return [(flip(x, dx), flip(y, dy), flip(c, dc)) for dx in (0, 1) for dy in (0, 1) for dc in (0, 1)][1:]


def _all_gather(blocks, name):
    n = len(blocks)

    def body(*refs):
        ins, outs = refs[:n], refs[n:2 * n]
        send_sems, recv_sems, local_sems = refs[2 * n:]
        x, y, c = _mesh_place()
        me, sibling = (x, y, c), (x, y, 1 - c)
        chips = [(1 - x, y), (x, 1 - y), (1 - x, 1 - y)]

        def copy(a, k, block, to, src=None):
            rows = outs[a].at[_slot(block)]
            return pltpu.make_async_remote_copy(
                src_ref=rows if src is None else src, dst_ref=rows,
                send_sem=send_sems.at[a, k], recv_sem=recv_sems.at[a, k],
                device_id=to, device_id_type=pl.DeviceIdType.MESH)

        mine = [pltpu.make_async_copy(ins[a], outs[a].at[_slot(me)], local_sems.at[a]) for a in range(n)]
        for cp in mine:
            cp.start()
        first = []
        for a in range(n):
            first.append(copy(a, 0, me, sibling, src=ins[a]))
            first += [copy(a, 1 + j, me, (*chip, c), src=ins[a]) for j, chip in enumerate(chips)]
        for cp in first:
            cp.start()
        passed = []
        for a in range(n):
            for j, chip in enumerate(chips):
                copy(a, 1 + j, (*chip, c), me).wait_recv()
                fwd = copy(a, 4 + j, (*chip, c), sibling)
                fwd.start()
                passed.append(fwd)
        for a in range(n):
            copy(a, 0, sibling, me).wait_recv()
            for j, chip in enumerate(chips):
                copy(a, 4 + j, (*chip, 1 - c), me).wait_recv()
        for cp in first + passed:
            cp.wait_send()
        for cp in mine:
            cp.wait()

    any_spec = pl.BlockSpec(memory_space=pl.ANY)
    return pl.pallas_call(
        body, name=name,
        out_shape=[_sds((N_DEV,) + b.shape, b.dtype) for b in blocks],
        in_specs=[any_spec] * n, out_specs=[any_spec] * n,
        scratch_shapes=[pltpu.SemaphoreType.DMA((n, 7)), pltpu.SemaphoreType.DMA((n, 7)),
                        pltpu.SemaphoreType.DMA((n,))],
    )(*blocks)


def _side_gather_copies(ins, outs, send_sems, recv_sems, local_sems):
    me = _mesh_place()
    copies = []
    for a, (src, dst) in enumerate(zip(ins, outs)):
        copies.append(pltpu.make_async_copy(src, dst.at[_slot(me)], local_sems.at[a]))
        for k, peer in enumerate(_other_devices(me)):
            copies.append(pltpu.make_async_remote_copy(
                src_ref=src, dst_ref=dst.at[_slot(me)], send_sem=send_sems.at[a, k], recv_sem=recv_sems.at[a, k],
                device_id=peer, device_id_type=pl.DeviceIdType.MESH))
    return copies


def _side_gather_two_level(ins, outs, send_sems, recv_sems, local_sems):
    x, y, c = _mesh_place()
    me, sibling = (x, y, c), (x, y, 1 - c)
    chips = [(1 - x, y), (x, 1 - y), (1 - x, 1 - y)]

    def copy(a, k, block, to, src=None):
        rows = outs[a].at[_slot(block)]
        return pltpu.make_async_remote_copy(
            src_ref=rows if src is None else src, dst_ref=rows, send_sem=send_sems.at[a, k], recv_sem=recv_sems.at[a, k],
            device_id=to, device_id_type=pl.DeviceIdType.MESH)

    n = len(ins)
    mine = [pltpu.make_async_copy(ins[a], outs[a].at[_slot(me)], local_sems.at[a]) for a in range(n)]
    first = [copy(a, 0, me, sibling, src=ins[a]) for a in range(n)]
    first += [copy(a, 1 + j, me, (*chip, c), src=ins[a]) for a in range(n) for j, chip in enumerate(chips)]
    passed = [copy(a, 4 + j, (*chip, c), sibling) for a in range(n) for j, chip in enumerate(chips)]

    def start():
        for cp in mine + first:
            cp.start()

    def forward():
        for a in range(n):
            for j, chip in enumerate(chips):
                copy(a, 1 + j, (*chip, c), me).wait_recv()
        for cp in passed:
            cp.start()

    def finish():
        for a in range(n):
            copy(a, 0, sibling, me).wait_recv()
            for j, chip in enumerate(chips):
                copy(a, 4 + j, (*chip, 1 - c), me).wait_recv()
        for cp in first + passed:
            cp.wait_send()
        for cp in mine:
            cp.wait()

    return start, forward, finish


def _side_gather_specs(blocks):
    n = len(blocks)
    any_spec = pl.BlockSpec(memory_space=pl.ANY)
    return ([any_spec] * n, [_sds((N_DEV,) + b.shape, b.dtype) for b in blocks],
            [pltpu.SemaphoreType.DMA((n, N_DEV - 1)), pltpu.SemaphoreType.DMA((n, N_DEV - 1)),
             pltpu.SemaphoreType.DMA((n,))])


def _kv_proj(mem, w_kv):
    def body(mem_ref, w_ref, kv_ref, kt_ref, memb_ref):
        mb = mem_ref[...].astype(bf16)
        memb_ref[...] = mb
        kv = _dot(mb, w_ref[...]).astype(bf16)
        kv_ref[...] = kv
        kt_ref[...] = kv[:, :XATTN_W].T

    return pl.pallas_call(
        body, name="kv_proj",
        out_shape=[_sds((MEM_LEN, 2 * XATTN_W), bf16), _sds((XATTN_W, MEM_LEN), bf16), _sds((MEM_LEN, D_MODEL), bf16)],
        compiler_params=_cparams(),
    )(mem, w_kv)


def _attention_probs(qb, kv_ref, h):
    kh = kv_ref[:, h * HEAD_DIM:(h + 1) * HEAD_DIM]
    s = _dot(qb[:, h * HEAD_DIM:(h + 1) * HEAD_DIM], kh, NT) * (HEAD_DIM ** -0.5)
    e = jnp.exp(s - jnp.max(s, axis=-1, keepdims=True))
    return e / jnp.sum(e, axis=-1, keepdims=True)


def _in_proj(x, w_in_t, b_gate, conv_w, kv, side_blocks):
    s_len = x.shape[0]
    tm = 2 * TOKEN_TILE
    n = s_len // tm
    ns = len(side_blocks)
    side_in_specs, side_shapes, side_sems = _side_gather_specs(side_blocks)

    def body(*refs):
        (x_ref, win_ref, bg_ref, cw_ref, kv_ref) = refs[:5]
        side_ins = refs[5:5 + ns]
        (xbt_ref, g_ref, cin_ref, u_ref, q_ref, ain_ref, o_ref, aint_ref, ot_ref) = refs[5 + ns:14 + ns]
        side_outs = refs[14 + ns:14 + 2 * ns]
        zs_ref = refs[14 + 2 * ns]
        side_start, side_forward, side_finish = _side_gather_two_level(side_ins, side_outs, *refs[15 + 2 * ns:])
        i = pl.program_id(0)
        pl.when(i == 0)(side_start)
        pl.when(i == (3 * n) // 4)(side_forward)

        xb = x_ref[...].astype(bf16)
        xbt_ref[...] = xb.T
        proj = _dot(xb, win_ref[...], NT)
        g_ref[...] = jax.nn.sigmoid(proj[:, :GATE_COLS] + bg_ref[...]).astype(bf16)
        cin = proj[:, GATE_COLS:GATE_COLS + 3 * CONV_W]
        cin_ref[...] = cin
        u_ref[...] = proj[:, GATE_COLS + 3 * CONV_W:GATE_COLS + 3 * CONV_W + SSM_W]
        qb = proj[:, IN_COLS - XATTN_W:].astype(bf16)
        q_ref[...] = qb

        cb, cc, ch = cin[:, :CONV_W], cin[:, CONV_W:2 * CONV_W], cin[:, 2 * CONV_W:]
        z = cc * ch

        @pl.when(i == 0)
        def _():
            zs_ref[0:8, :] = jnp.zeros((8, CONV_W), f32)

        zs_ref[8:8 + tm, :] = z
        z1 = zs_ref[pl.ds(7, tm), :]
        z2 = zs_ref[pl.ds(6, tm), :]
        cw = cw_ref[...]
        cz = cw[0:1] * z2 + cw[1:2] * z1 + cw[2:3] * z
        zs_ref[0:8, :] = zs_ref[tm:tm + 8, :]
        ain = (cb * cz).astype(bf16)
        ain_ref[...] = ain
        aint_ref[...] = ain.T

        probs = [_attention_probs(qb, kv_ref, h) for h in range(HEADS)]
        outs = [_dot(probs[h].astype(bf16), kv_ref[:, XATTN_W + h * HEAD_DIM:XATTN_W + (h + 1) * HEAD_DIM])
                for h in range(HEADS)]
        ob = jnp.concatenate(outs, axis=1).astype(bf16)
        o_ref[...] = ob
        ot_ref[...] = ob.T

        pl.when(i == n - 1)(side_finish)

    row_cols = [(GATE_COLS, bf16), (3 * CONV_W, f32), (SSM_W, f32), (XATTN_W, bf16), (CONV_W, bf16), (XATTN_W, bf16)]
    t_rows = [D_MODEL, CONV_W, XATTN_W]
    outs = pl.pallas_call(
        body, name="in_proj", grid=(n,),
        in_specs=[_row_spec(tm, D_MODEL), _const_spec((IN_COLS, D_MODEL)), _const_spec((1, GATE_COLS)),
                  _const_spec((3, CONV_W)), _const_spec((MEM_LEN, 2 * XATTN_W))] + side_in_specs,
        out_specs=([_col_spec(t_rows[0], tm)] + [_row_spec(tm, c) for c, _ in row_cols]
                   + [_col_spec(t_rows[1], tm), _col_spec(t_rows[2], tm)] + side_in_specs),
        out_shape=([_sds((t_rows[0], s_len), bf16)] + [_sds((s_len, c), dt) for c, dt in row_cols]
                   + [_sds((t_rows[1], s_len), bf16), _sds((t_rows[2], s_len), bf16)] + side_shapes),
        scratch_shapes=[pltpu.VMEM((tm + 8, CONV_W), f32)] + side_sems,
        compiler_params=_cparams(("arbitrary",)),
    )(x, w_in_t, b_gate, conv_w, kv, *side_blocks)
    return outs[:9], outs[9:]


def _state_cols(chunk, width=LANE_CHUNK):
    half, off = divmod(chunk * width, HALF_STATE)
    lo = half * HALF_COLS + off
    return slice(lo, lo + width), slice(lo + HALF_STATE, lo + HALF_STATE + width)


def _half_cols(half):
    lo = half * HALF_COLS
    return slice(lo, lo + HALF_STATE), slice(lo + HALF_STATE, lo + HALF_COLS)


def _rows_to_segments(src_ref, stage_ref, dst_ref):
    nc = SSM_W // LANES
    for c in range(nc):
        stage_ref[c] = src_ref[:, c * LANES:(c + 1) * LANES]
    for c in range(nc):
        for k in range(SSM_SEG):
            dst_ref[k * SUBLANES:(k + 1) * SUBLANES, c * LANES:(c + 1) * LANES] = (
                stage_ref[c, pl.ds(k, SUBLANES, stride=SSM_SEG), :])


def _rows_from_segments(src_ref, stage_ref, dst_ref):
    nc = SSM_W // LANES
    for c in range(nc):
        for k in range(SSM_SEG):
            stage_ref[c, pl.ds(k, SUBLANES, stride=SSM_SEG), :] = (
                src_ref[k * SUBLANES:(k + 1) * SUBLANES, c * LANES:(c + 1) * LANES])
    for c in range(nc):
        dst_ref[:, c * LANES:(c + 1) * LANES] = stage_ref[c]


def _ssm_scan(s_ref, pw_ref, init_ref, reverse, unroll, width=LANE_CHUNK):
    for chunk in range(N_STATE // width):
        re, im = _state_cols(chunk, width)
        ar = jnp.broadcast_to(pw_ref[0:1, re], (SUBLANES, width))
        ai = jnp.broadcast_to(pw_ref[0:1, im], (SUBLANES, width))
        if reverse:
            ai = -ai

        def step(j, carry, re=re, im=im, ar=ar, ai=ai):
            sr, si = carry
            k = (SSM_SEG - 1 - j) if reverse else j
            r0 = pl.multiple_of(k * SUBLANES, SUBLANES)
            nr = ar * sr - ai * si + s_ref[pl.ds(r0, SUBLANES), re]
            ni = ar * si + ai * sr + s_ref[pl.ds(r0, SUBLANES), im]
            s_ref[pl.ds(r0, SUBLANES), re] = nr
            s_ref[pl.ds(r0, SUBLANES), im] = ni
            return nr, ni

        if init_ref is None:
            init = (jnp.zeros((SUBLANES, width), f32),) * 2
        else:
            init = (init_ref[:, re], init_ref[:, im])
        lax.fori_loop(0, SSM_SEG, step, init, unroll=unroll)


def _ssm_add_carry(s_ref, pw_ref, cm_ref, reverse):
    for chunk in range(N_STATE // LANE_CHUNK):
        re, im = _state_cols(chunk)
        cr, ci = cm_ref[:, re], cm_ref[:, im]
        for k in range(SSM_SEG):
            pk = (SSM_SEG - 1 - k) if reverse else k
            pr = pw_ref[pk:pk + 1, re]
            pi = pw_ref[pk:pk + 1, im]
            if reverse:
                pi = -pi
            rows = slice(k * SUBLANES, (k + 1) * SUBLANES)
            s_ref[rows, re] = s_ref[rows, re] + (pr * cr - pi * ci)
            s_ref[rows, im] = s_ref[rows, im] + (pr * ci + pi * cr)


def _ssm_carries(first_row, s_ref, pw_ref, carry_ref, cm_ref, reverse):
    order = range(SUBLANES - 1, -1, -1) if reverse else range(SUBLANES)
    for half in range(N_HALF):
        re, im = _half_cols(half)
        a_r, a_i = pw_ref[SSM_SEG - 1:SSM_SEG, re], pw_ref[SSM_SEG - 1:SSM_SEG, im]
        if reverse:
            a_i = -a_i
        cr, ci = carry_ref[0:1, re], carry_ref[0:1, im]
        for seg in order:
            cm_ref[seg:seg + 1, re] = cr
            cm_ref[seg:seg + 1, im] = ci
            er = s_ref[first_row + seg:first_row + seg + 1, re]
            ei = s_ref[first_row + seg:first_row + seg + 1, im]
            cr, ci = a_r * cr - a_i * ci + er, a_r * ci + a_i * cr + ei
        carry_ref[0:1, re] = cr
        carry_ref[0:1, im] = ci


def _ssm_fwd(u, b_half, c_half, pw, d_skip, side_blocks):
    s_len = u.shape[0]
    tb = SSM_BLOCK
    n = s_len // tb
    ns = len(side_blocks)
    side_in_specs, side_shapes, side_sems = _side_gather_specs(side_blocks)

    def body(*refs):
        u_ref, b_ref, c_ref, pw_ref, d_ref = refs[:5]
        side_ins = refs[5:5 + ns]
        y_ref, cm_ref = refs[5 + ns:7 + ns]
        side_outs = refs[7 + ns:7 + 2 * ns]
        s_ref, carry_ref, up_ref, yp_ref, stage_ref = refs[7 + 2 * ns:12 + 2 * ns]
        side_start, side_forward, side_finish = _side_gather_two_level(side_ins, side_outs, *refs[12 + 2 * ns:])
        i = pl.program_id(0)

        @pl.when(i == 0)
        def _():
            carry_ref[...] = jnp.zeros_like(carry_ref)
            side_start()

        pl.when(i == (3 * n) // 4)(side_forward)
        _rows_to_segments(u_ref, stage_ref, up_ref)
        u = up_ref[...]
        ub = u.astype(bf16)
        for half in range(N_HALF):
            s_ref[:, half * HALF_COLS:(half + 1) * HALF_COLS] = _dot(ub[:, half * HALF_W:(half + 1) * HALF_W], b_ref[half])
        _ssm_scan(s_ref, pw_ref, None, reverse=False, unroll=4, width=2 * LANE_CHUNK)
        _ssm_carries(tb - SUBLANES, s_ref, pw_ref, carry_ref, cm_ref, reverse=False)
        _ssm_add_carry(s_ref, pw_ref, cm_ref, reverse=False)
        for half in range(N_HALF):
            cols = slice(half * HALF_W, (half + 1) * HALF_W)
            sb = s_ref[:, half * HALF_COLS:(half + 1) * HALF_COLS].astype(bf16)
            yp_ref[:, cols] = _dot(sb, c_ref[half]) + d_ref[:, cols] * u[:, cols]
        _rows_from_segments(yp_ref, stage_ref, y_ref)

        pl.when(i == n - 1)(side_finish)

    outs = pl.pallas_call(
        body, name="ssm_fwd", grid=(n,),
        in_specs=[_row_spec(tb, SSM_W), _const_spec((N_HALF, HALF_W, HALF_COLS)), _const_spec((N_HALF, HALF_COLS, HALF_W)),
                  _const_spec((SSM_SEG, 2 * N_STATE)), _const_spec((1, SSM_W))] + side_in_specs,
        out_specs=[_row_spec(tb, SSM_W), _row_spec(SUBLANES, 2 * N_STATE)] + side_in_specs,
        out_shape=[_sds((s_len, SSM_W), f32), _sds((n * SUBLANES, 2 * N_STATE), f32)] + side_shapes,
        scratch_shapes=[pltpu.VMEM((tb, 2 * N_STATE), f32), pltpu.VMEM((SUBLANES, 2 * N_STATE), f32),
                        pltpu.VMEM((tb, SSM_W), f32), pltpu.VMEM((tb, SSM_W), f32),
                        pltpu.VMEM((SSM_W // LANES, tb, LANES), f32)] + side_sems,
        compiler_params=_cparams(("arbitrary",)),
    )(u, b_half, c_half, pw, d_skip, *side_blocks)
    return outs[0], outs[1], outs[2:]


def _layer_norm_fwd(r, g, b):
    mu = jnp.mean(r, axis=-1, keepdims=True)
    var = jnp.mean(jnp.square(r - mu), axis=-1, keepdims=True)
    rstd = lax.rsqrt(var + LN_EPS)
    xhat = (r - mu) * rstd
    return xhat, rstd, xhat * g + b


def _layer_norm_bwd(dy, xhat, rstd, g):
    dxh = dy * g
    m1 = jnp.mean(dxh, axis=-1, keepdims=True)
    m2 = jnp.mean(dxh * xhat, axis=-1, keepdims=True)
    return rstd * (dxh - m1 - xhat * m2)


def _branch_outputs(ys_ref, ain_ref, o_ref, wglu_ref, wco_ref, wxo_ref):
    ysb = _gelu(ys_ref[...]).astype(bf16)
    glu = _dot(ysb, wglu_ref[...], NT)
    ga, sb = glu[:, :D_MODEL], jax.nn.sigmoid(glu[:, D_MODEL:])
    ya = _dot(ain_ref[...], wco_ref[...], NT)
    yc = _dot(o_ref[...], wxo_ref[...], NT)
    return ysb, ga, sb, ya, ga * sb, yc


def _mid_fwd(y_ssm, g, ain, ob, x, w_glu_t, w_co_t, w_xo_t, w_out, ln1_g, ln1_b):
    s_len = x.shape[0]
    tm = 2 * TOKEN_TILE
    n = s_len // tm

    def body(ys_ref, g_ref, ain_ref, o_ref, x_ref, wglu_ref, wco_ref, wxo_ref, wout_ref, lg_ref, lb_ref,
             ysbt_ref, mb_ref, xhat_ref, rstd_ref):
        ysb, _, _, ya, yb, yc = _branch_outputs(ys_ref, ain_ref, o_ref, wglu_ref, wco_ref, wxo_ref)
        ysbt_ref[...] = ysb.T
        gt = g_ref[...].astype(f32)
        merged = gt[:, :D_MODEL] * ya + gt[:, D_MODEL:2 * D_MODEL] * yb + gt[:, 2 * D_MODEL:] * yc
        mb = merged.astype(bf16)
        mb_ref[...] = mb
        r1 = ALPHA * x_ref[...] + _dot(mb, wout_ref[...])
        xhat, rstd, _ = _layer_norm_fwd(r1, lg_ref[...], lb_ref[...])
        xhat_ref[...] = xhat
        rstd_ref[...] = rstd

    row_cols = [(D_MODEL, bf16), (D_MODEL, f32), (1, f32)]
    return pl.pallas_call(
        body, name="mid_fwd", grid=(n,),
        in_specs=[_row_spec(tm, SSM_W), _row_spec(tm, GATE_COLS), _row_spec(tm, CONV_W), _row_spec(tm, XATTN_W),
                  _row_spec(tm, D_MODEL), _const_spec((2 * D_MODEL, SSM_W)), _const_spec((D_MODEL, CONV_W)),
                  _const_spec((D_MODEL, XATTN_W)), _const_spec((D_MODEL, D_MODEL)),
                  _const_spec((1, D_MODEL)), _const_spec((1, D_MODEL))],
        out_specs=[_col_spec(SSM_W, tm)] + [_row_spec(tm, c) for c, _ in row_cols],
        out_shape=[_sds((SSM_W, s_len), bf16)] + [_sds((s_len, c), dt) for c, dt in row_cols],
        compiler_params=_cparams(("parallel",)),
    )(y_ssm, g, ain, ob, x, w_glu_t, w_co_t, w_xo_t, w_out, ln1_g, ln1_b)


def _mlp_fwd_bwd(xhat1, tgt, ln1_g, ln1_b, w_up_t, b_up, w_down, b_down, ln2_g, ln2_b):
    s_len = xhat1.shape[0]
    tm = TOKEN_TILE
    n = s_len // tm
    fc = 1024
    nfc = D_FF // fc

    def body(xh_ref, t_ref, l1g_ref, l1b_ref, wup_ref, bup_ref, wdn_ref, bdn_ref, l2g_ref, l2b_ref,
             x1bt_ref, hdn_ref, dr2bt_ref, dpre_ref, dx1_ref,
             loss_ref, dl2g_ref, dl2b_ref, dbdn_ref, dbup_ref, rl_ref):
        i = pl.program_id(0)

        @pl.when(i == 0)
        def _():
            loss_ref[...] = jnp.zeros_like(loss_ref)
            dl2g_ref[...] = jnp.zeros_like(dl2g_ref)
            dl2b_ref[...] = jnp.zeros_like(dl2b_ref)
            dbdn_ref[...] = jnp.zeros_like(dbdn_ref)
            dbup_ref[...] = jnp.zeros_like(dbup_ref)

        x1 = xh_ref[...] * l1g_ref[...] + l1b_ref[...]
        x1b = x1.astype(bf16)
        x1bt_ref[...] = x1b.T
        chunks = [slice(c * fc, (c + 1) * fc) for c in range(nfc)]
        pres = [_dot(x1b, wup_ref[cols, :], NT) for cols in chunks]
        hbs = []
        for cols, pre in zip(chunks, pres):
            rl = jnp.maximum(pre + bup_ref[:, cols], 0.0)
            rl_ref[:, cols] = rl
            hb = (rl * rl).astype(bf16)
            hdn_ref[:, cols] = hb
            hbs.append(hb)
        acc = _dot(hbs[0], wdn_ref[chunks[0], :])
        for cols, hb in zip(chunks[1:], hbs[1:]):
            acc = acc + _dot(hb, wdn_ref[cols, :])
        r2 = ALPHA * x1 + acc + bdn_ref[...]
        xhat2, rstd2, y = _layer_norm_fwd(r2, l2g_ref[...], l2b_ref[...])
        err = y - t_ref[...]
        loss_ref[...] += jnp.sum(jnp.sum(err * err, axis=1, keepdims=True), axis=0, keepdims=True) * (0.5 / D_MODEL)
        dy = err * (1.0 / D_MODEL)
        dl2g_ref[...] += _colsum(dy * xhat2)
        dl2b_ref[...] += _colsum(dy)
        dr2 = _layer_norm_bwd(dy, xhat2, rstd2, l2g_ref[...])
        dbdn_ref[...] += _colsum(dr2)
        dr2b = dr2.astype(bf16)
        dr2bt_ref[...] = dr2b.T
        dhs = [_dot(dr2b, wdn_ref[cols, :], NT) for cols in chunks]
        dpbs = []
        for cols, dh in zip(chunks, dhs):
            dpre = dh * (2.0 * rl_ref[:, cols])
            dbup_ref[:, cols] += _colsum(dpre)
            dpb = dpre.astype(bf16)
            dpre_ref[:, cols] = dpb
            dpbs.append(dpb)
        dacc = _dot(dpbs[0], wup_ref[chunks[0], :])
        for cols, dpb in zip(chunks[1:], dpbs[1:]):
            dacc = dacc + _dot(dpb, wup_ref[cols, :])
        dx1_ref[...] = ALPHA * dr2 + dacc

    acc_shapes = [(1, LANES), (1, D_MODEL), (1, D_MODEL), (1, D_MODEL), (1, D_FF)]
    return pl.pallas_call(
        body, name="mlp_fwd_bwd", grid=(n,),
        in_specs=[_row_spec(tm, D_MODEL), _row_spec(tm, D_MODEL), _const_spec((1, D_MODEL)), _const_spec((1, D_MODEL)),
                  _const_spec((D_FF, D_MODEL)), _const_spec((1, D_FF)), _const_spec((D_FF, D_MODEL)),
                  _const_spec((1, D_MODEL)), _const_spec((1, D_MODEL)), _const_spec((1, D_MODEL))],
        out_specs=([_col_spec(D_MODEL, tm), _row_spec(tm, D_FF), _col_spec(D_MODEL, tm), _row_spec(tm, D_FF),
                    _row_spec(tm, D_MODEL)] + [_acc_spec(s) for s in acc_shapes]),
        out_shape=([_sds((D_MODEL, s_len), bf16), _sds((s_len, D_FF), bf16), _sds((D_MODEL, s_len), bf16),
                    _sds((s_len, D_FF), bf16), _sds((s_len, D_MODEL), f32)] + [_sds(s, f32) for s in acc_shapes]),
        scratch_shapes=[pltpu.VMEM((tm, D_FF), f32)],
        compiler_params=_cparams(("arbitrary",)),
    )(xhat1, tgt, ln1_g, ln1_b, w_up_t, b_up, w_down, b_down, ln2_g, ln2_b)


def _mid_bwd(dx1, xhat1, rstd1, g, ain, ob, y_ssm, ln1_g, w_out, w_glu_t, w_co_t, w_xo_t):
    s_len = dx1.shape[0]
    tm = TOKEN_TILE
    n = s_len // tm

    def body(dx1_ref, xh_ref, rs_ref, g_ref, ain_ref, o_ref, ys_ref, lg_ref, wout_ref, wglu_ref, wco_ref, wxo_ref,
             dxp_ref, dr1bt_ref, dgp_ref, dya_ref, dyc_ref, dglu_ref, dyssm_ref,
             dl1g_ref, dl1b_ref, dbg_ref):
        i = pl.program_id(0)

        @pl.when(i == 0)
        def _():
            dl1g_ref[...] = jnp.zeros_like(dl1g_ref)
            dl1b_ref[...] = jnp.zeros_like(dl1b_ref)
            dbg_ref[...] = jnp.zeros_like(dbg_ref)

        dx1 = dx1_ref[...]
        xhat = xh_ref[...]
        dl1g_ref[...] += _colsum(dx1 * xhat)
        dl1b_ref[...] += _colsum(dx1)
        dr1 = _layer_norm_bwd(dx1, xhat, rs_ref[...], lg_ref[...])
        dxp_ref[...] = ALPHA * dr1
        dr1b = dr1.astype(bf16)
        dr1bt_ref[...] = dr1b.T
        dm = _dot(dr1b, wout_ref[...], NT)

        _, ga, sb, ya, yb, yc = _branch_outputs(ys_ref, ain_ref, o_ref, wglu_ref, wco_ref, wxo_ref)
        gt = g_ref[...].astype(f32)
        branch = (ya, yb, yc)
        for j in range(3):
            cols = slice(j * D_MODEL, (j + 1) * D_MODEL)
            gj = gt[:, cols]
            dgp = dm * branch[j] * gj * (1.0 - gj)
            dbg_ref[:, cols] += _colsum(dgp)
            dgp_ref[:, cols] = dgp.astype(bf16)
        dya_ref[...] = (dm * gt[:, :D_MODEL]).astype(bf16)
        dyc_ref[...] = (dm * gt[:, 2 * D_MODEL:]).astype(bf16)
        dyb = dm * gt[:, D_MODEL:2 * D_MODEL]
        dga = (dyb * sb).astype(bf16)
        dgb = (dyb * ga * sb * (1.0 - sb)).astype(bf16)
        dglu_ref[:, :D_MODEL] = dga
        dglu_ref[:, D_MODEL:] = dgb
        dys = _dot(dga, wglu_ref[:D_MODEL, :]) + _dot(dgb, wglu_ref[D_MODEL:, :])
        dyssm_ref[...] = dys * _gelu_grad(ys_ref[...])

    row_cols = [(GATE_COLS, bf16), (D_MODEL, bf16), (D_MODEL, bf16), (2 * D_MODEL, bf16), (SSM_W, f32)]
    acc_shapes = [(1, D_MODEL), (1, D_MODEL), (1, GATE_COLS)]
    return pl.pallas_call(
        body, name="mid_bwd", grid=(n,),
        in_specs=[_row_spec(tm, D_MODEL), _row_spec(tm, D_MODEL), _row_spec(tm, 1), _row_spec(tm, GATE_COLS),
                  _row_spec(tm, CONV_W), _row_spec(tm, XATTN_W), _row_spec(tm, SSM_W),
                  _const_spec((1, D_MODEL)), _const_spec((D_MODEL, D_MODEL)), _const_spec((2 * D_MODEL, SSM_W)),
                  _const_spec((D_MODEL, CONV_W)), _const_spec((D_MODEL, XATTN_W))],
        out_specs=([_row_spec(tm, D_MODEL), _col_spec(D_MODEL, tm)] + [_row_spec(tm, c) for c, _ in row_cols]
                   + [_acc_spec(s) for s in acc_shapes]),
        out_shape=([_sds((s_len, D_MODEL), f32), _sds((D_MODEL, s_len), bf16)]
                   + [_sds((s_len, c), dt) for c, dt in row_cols] + [_sds(s, f32) for s in acc_shapes]),
        compiler_params=_cparams(("arbitrary",)),
    )(dx1, xhat1, rstd1, g, ain, ob, y_ssm, ln1_g, w_out, w_glu_t, w_co_t, w_xo_t)


def _ssm_bwd(u, dy, cm_all, b_half, c_half, pw, d_skip):
    s_len = u.shape[0]
    tb = SSM_BLOCK
    n = s_len // tb

    def body(u_ref, dy_ref, cm_ref, b_ref, c_ref, pw_ref, d_ref,
             du_ref, db_hbm, dc_hbm, da_ref, dd_ref,
             s_ref, g_ref, gcarry_ref, gcm_ref, db_ref, dc_ref, up_ref, dyp_ref, dup_ref, stage_ref):
        i = pl.program_id(0)

        @pl.when(i == 0)
        def _():
            gcarry_ref[...] = jnp.zeros_like(gcarry_ref)
            db_ref[...] = jnp.zeros_like(db_ref)
            dc_ref[...] = jnp.zeros_like(dc_ref)
            da_ref[...] = jnp.zeros_like(da_ref)
            dd_ref[...] = jnp.zeros_like(dd_ref)

        _rows_to_segments(u_ref, stage_ref, up_ref)
        _rows_to_segments(dy_ref, stage_ref, dyp_ref)
        u = up_ref[...]
        ub = u.astype(bf16)
        dy = dyp_ref[...]
        dyb = dy.astype(bf16)
        dd_ref[...] += _colsum(dy * u)

        for half in range(N_HALF):
            s_ref[:, half * HALF_COLS:(half + 1) * HALF_COLS] = _dot(ub[:, half * HALF_W:(half + 1) * HALF_W], b_ref[half])
        _ssm_scan(s_ref, pw_ref, cm_ref, reverse=False, unroll=True)

        for half in range(N_HALF):
            g_ref[:, half * HALF_COLS:(half + 1) * HALF_COLS] = _dot(dyb[:, half * HALF_W:(half + 1) * HALF_W], c_ref[half], NT)
        _ssm_scan(g_ref, pw_ref, None, reverse=True, unroll=True)
        _ssm_carries(0, g_ref, pw_ref, gcarry_ref, gcm_ref, reverse=True)
        _ssm_add_carry(g_ref, pw_ref, gcm_ref, reverse=True)

        for half in range(N_HALF):
            cols = slice(half * HALF_W, (half + 1) * HALF_W)
            scols = slice(half * HALF_COLS, (half + 1) * HALF_COLS)
            gb = g_ref[:, scols].astype(bf16)
            dup_ref[:, cols] = _dot(gb, b_ref[half], NT) + d_ref[:, cols] * dy[:, cols]
            db_ref[half] += _dot(ub[:, cols], gb, TN)
            dc_ref[half] += _dot(s_ref[:, scols].astype(bf16), dyb[:, cols], TN)
        _rows_from_segments(dup_ref, stage_ref, du_ref)

        for chunk in range(N_STATE // LANE_CHUNK):
            re, im = _state_cols(chunk)
            acc_r = da_ref[:, re]
            acc_i = da_ref[:, im]
            for k in range(SSM_SEG):
                rows = slice(k * SUBLANES, (k + 1) * SUBLANES)
                if k == 0:
                    pr, pi = cm_ref[:, re], cm_ref[:, im]
                else:
                    prev = slice((k - 1) * SUBLANES, k * SUBLANES)
                    pr, pi = s_ref[prev, re], s_ref[prev, im]
                gr, gi = g_ref[rows, re], g_ref[rows, im]
                acc_r = acc_r + (gr * pr + gi * pi)
                acc_i = acc_i + (gi * pr - gr * pi)
            da_ref[:, re] = acc_r
            da_ref[:, im] = acc_i

        @pl.when(i == n - 1)
        def _():
            pltpu.sync_copy(db_ref, db_hbm)
            pltpu.sync_copy(dc_ref, dc_hbm)

    rev = functools.partial(_row_spec, rev_n=n)
    any_spec = pl.BlockSpec(memory_space=pl.ANY)
    state_rows = pltpu.VMEM((tb, 2 * N_STATE), f32)
    seg_rows = pltpu.VMEM((SUBLANES, 2 * N_STATE), f32)
    tok_rows = pltpu.VMEM((tb, SSM_W), f32)
    return pl.pallas_call(
        body, name="ssm_bwd", grid=(n,),
        in_specs=[rev(tb, SSM_W), rev(tb, SSM_W), rev(SUBLANES, 2 * N_STATE),
                  _const_spec((N_HALF, HALF_W, HALF_COLS)), _const_spec((N_HALF, HALF_COLS, HALF_W)),
                  _const_spec((SSM_SEG, 2 * N_STATE)), _const_spec((1, SSM_W))],
        out_specs=[rev(tb, SSM_W), any_spec, any_spec, _acc_spec((SUBLANES, 2 * N_STATE)), _acc_spec((1, SSM_W))],
        out_shape=[_sds((s_len, SSM_W), f32), _sds((N_HALF, HALF_W, HALF_COLS), f32),
                   _sds((N_HALF, HALF_COLS, HALF_W), f32), _sds((SUBLANES, 2 * N_STATE), f32), _sds((1, SSM_W), f32)],
        scratch_shapes=[state_rows, state_rows, seg_rows, seg_rows,
                        pltpu.VMEM((N_HALF, HALF_W, HALF_COLS), f32), pltpu.VMEM((N_HALF, HALF_COLS, HALF_W), f32),
                        tok_rows, tok_rows, tok_rows, pltpu.VMEM((SSM_W // LANES, tb, LANES), f32)],
        compiler_params=_cparams(("arbitrary",)),
    )(u, dy, cm_all, b_half, c_half, pw, d_skip)


def _branch_bwd(dya, dyc, cin, q, kv, k_t, conv_w, w_co_t, w_xo_t, side_blocks):
    s_len = dya.shape[0]
    tm = 2 * TOKEN_TILE
    n = s_len // tm
    halo_blocks = tm // 8
    ns = len(side_blocks)
    conv_tile = _sds((8, CONV_W), f32)
    side_in_specs, side_shapes, side_sems = _side_gather_specs(list(side_blocks) + [conv_tile])

    def body(*refs):
        (dya_ref, dyc_ref, cin_ref, cprev_ref, q_ref, kv_ref, cw_ref, wco_ref, wxo_ref, kt_ref) = refs[:10]
        side_ins = refs[10:10 + ns]
        dconv_ref, dq_ref, dkv_ref = refs[10 + ns:13 + ns]
        side_outs = refs[13 + ns:14 + 2 * ns]
        zs_ref, dczs_ref, dcw_ref = refs[14 + 2 * ns:17 + 2 * ns]
        copies = _side_gather_copies(list(side_ins) + [dcw_ref], side_outs, *refs[17 + 2 * ns:])
        side, conv_side = copies[:ns * N_DEV], copies[ns * N_DEV:]
        i = pl.program_id(0)
        tile = n - 1 - i

        @pl.when(i == 0)
        def _():
            dcw_ref[...] = jnp.zeros_like(dcw_ref)
            dkv_ref[...] = jnp.zeros_like(dkv_ref)
            dczs_ref[tm:tm + 8, :] = jnp.zeros((8, CONV_W), f32)
            for cp in side:
                cp.start()

        cin = cin_ref[...]
        cb, cc, ch = cin[:, :CONV_W], cin[:, CONV_W:2 * CONV_W], cin[:, 2 * CONV_W:]
        z = cc * ch
        cprev = cprev_ref[...]
        zprev = cprev[:, CONV_W:2 * CONV_W] * cprev[:, 2 * CONV_W:]
        zs_ref[0:8, :] = jnp.where(tile == 0, 0.0, zprev)
        zs_ref[8:8 + tm, :] = z
        z1 = zs_ref[pl.ds(7, tm), :]
        z2 = zs_ref[pl.ds(6, tm), :]
        cw = cw_ref[...]
        cz = cw[0:1] * z2 + cw[1:2] * z1 + cw[2:3] * z

        dain = _dot(dya_ref[...], wco_ref[...])
        dcb = dain * cz
        dcz = dain * cb
        dczs_ref[0:tm, :] = dcz
        dcz1 = dczs_ref[pl.ds(1, tm), :]
        dcz2 = dczs_ref[pl.ds(2, tm), :]
        dz = cw[2:3] * dcz + cw[1:2] * dcz1 + cw[0:1] * dcz2
        dczs_ref[tm:tm + 8, :] = dczs_ref[0:8, :]
        dcw_ref[0:1, :] += _colsum(dcz * z2)
        dcw_ref[1:2, :] += _colsum(dcz * z1)
        dcw_ref[2:3, :] += _colsum(dcz * z)
        dconv_ref[:, :CONV_W] = dcb.astype(bf16)
        dconv_ref[:, CONV_W:2 * CONV_W] = (dz * ch).astype(bf16)
        dconv_ref[:, 2 * CONV_W:] = (dz * cc).astype(bf16)

        qb = q_ref[...]
        dob = _dot(dyc_ref[...], wxo_ref[...]).astype(bf16)
        kv = kv_ref[...]
        heads = range(HEADS)
        hcs = [slice(h * HEAD_DIM, (h + 1) * HEAD_DIM) for h in heads]
        vcs = [slice(XATTN_W + h * HEAD_DIM, XATTN_W + (h + 1) * HEAD_DIM) for h in heads]
        s_t = [_dot(kv[:, hcs[h]], qb[:, hcs[h]], NT) * (HEAD_DIM ** -0.5) for h in heads]
        dp_t = [_dot(kv[:, vcs[h]], dob[:, hcs[h]], NT) for h in heads]
        e_t = [jnp.exp(s_t[h] - jnp.max(s_t[h], axis=0, keepdims=True)) for h in heads]
        p_t = [e_t[h] / jnp.sum(e_t[h], axis=0, keepdims=True) for h in heads]
        dv = [_dot(p_t[h].astype(bf16), dob[:, hcs[h]]) for h in heads]
        ds_t = [(p_t[h] * (dp_t[h] - jnp.sum(dp_t[h] * p_t[h], axis=0, keepdims=True)) * (HEAD_DIM ** -0.5)).astype(bf16)
                for h in heads]
        dk = [_dot(ds_t[h], qb[:, hcs[h]]) for h in heads]
        dq_t = [_dot(kt_ref[hcs[h], :], ds_t[h]) for h in heads]
        dq_ref[...] = jnp.concatenate(dq_t, axis=0).T.astype(bf16)
        dkv_ref[...] += jnp.concatenate(dk + dv, axis=1)

        @pl.when(i == n - 1)
        def _():
            for cp in conv_side:
                cp.start()
            for cp in side + conv_side:
                cp.wait()

    rev = functools.partial(_row_spec, rev_n=n)
    prev_spec = pl.BlockSpec((8, 3 * CONV_W), lambda i: (jnp.maximum((n - 1 - i) * halo_blocks - 1, 0), 0))
    outs = pl.pallas_call(
        body, name="branch_bwd", grid=(n,),
        in_specs=[rev(tm, D_MODEL), rev(tm, D_MODEL), rev(tm, 3 * CONV_W), prev_spec, rev(tm, XATTN_W),
                  _const_spec((MEM_LEN, 2 * XATTN_W)), _const_spec((3, CONV_W)), _const_spec((D_MODEL, CONV_W)),
                  _const_spec((D_MODEL, XATTN_W)), _const_spec((XATTN_W, MEM_LEN))] + side_in_specs[:ns],
        out_specs=[rev(tm, 3 * CONV_W), rev(tm, XATTN_W), _acc_spec((MEM_LEN, 2 * XATTN_W))] + side_in_specs,
        out_shape=[_sds((s_len, 3 * CONV_W), bf16), _sds((s_len, XATTN_W), bf16),
                   _sds((MEM_LEN, 2 * XATTN_W), f32)] + side_shapes,
        scratch_shapes=[pltpu.VMEM((tm + 8, CONV_W), f32), pltpu.VMEM((tm + 8, CONV_W), f32),
                        pltpu.VMEM((8, CONV_W), f32)] + side_sems,
        compiler_params=_cparams(("arbitrary",)),
    )(dya, dyc, cin, cin, q, kv, conv_w, w_co_t, w_xo_t, k_t, *side_blocks)
    return outs[0], outs[1], outs[2], outs[3:]


def _in_proj_bwd(dgp, dconv, du, dq, dxp, w_in_t):
    s_len = dgp.shape[0]
    tm = 2 * TOKEN_TILE
    n = s_len // tm

    def body(dgp_ref, dconv_ref, du_ref, dq_ref, dxp_ref, win_ref, dx_ref, dproj_ref):
        dproj = jnp.concatenate([dgp_ref[...], dconv_ref[...], du_ref[...].astype(bf16), dq_ref[...]], axis=1)
        dproj_ref[...] = dproj
        dx_ref[...] = dxp_ref[...] + _dot(dproj, win_ref[...])

    return pl.pallas_call(
        body, name="in_proj_bwd", grid=(n,),
        in_specs=[_row_spec(tm, GATE_COLS), _row_spec(tm, 3 * CONV_W), _row_spec(tm, SSM_W), _row_spec(tm, XATTN_W),
                  _row_spec(tm, D_MODEL), _const_spec((IN_COLS, D_MODEL))],
        out_specs=[_row_spec(tm, D_MODEL), _row_spec(tm, IN_COLS)],
        out_shape=[_sds((s_len, D_MODEL), f32), _sds((s_len, IN_COLS), bf16)],
        compiler_params=_cparams(("parallel",)),
    )(dgp, dconv, du, dq, dxp, w_in_t)


N_CHIP = 4
CHIP_STEPS = [(1, 1), (1, 0), (0, 1), (0, 0)]


def _flip(v, d):
    return 1 - v if d else v


def _chip_order():
    x, y, _ = _mesh_place()
    return jnp.stack([2 * _flip(x, dx) + _flip(y, dy) for dx, dy in CHIP_STEPS]).astype(jnp.int32)


def _weight_grads_scatter(problems, name):
    dims = []
    first = 0
    for a_t, b, tm, tt in problems:
        m, s_len = a_t.shape
        w = b.shape[1] // N_DEV
        tm, tt = min(tm, m), min(tt, s_len)
        assert m % tm == 0 and s_len % tt == 0
        nm, nt = m // tm, s_len // tt
        dims.append(dict(m=m, w=w, tm=tm, tt=tt, nm=nm, nt=nt, first=first, steps=N_CHIP * nm * nt))
        first += N_CHIP * nm * nt
    n_prob, total = len(problems), first
    n_scratch = 9

    def place(d, s):
        local = jnp.clip(s - d["first"], 0, d["steps"] - 1)
        return local // (d["nm"] * d["nt"]), (local // d["nt"]) % d["nm"], local % d["nt"]

    def run(d, q, im, t, a_ref, b_ref, recv_ref, acc_ref, send_ref, sib_ref, stash_ref,
            d2d_send, d2d_recv, ici_send, ici_recv, local_sem):
        tm, w, nm, nt = d["tm"], d["w"], d["nm"], d["nt"]
        x, y, c = _mesh_place()
        mesh_id = pl.DeviceIdType.MESH

        @pl.when(t == 0)
        def _():
            acc_ref[...] = jnp.zeros_like(acc_ref)

        acc_ref[...] += _dot(a_ref[...], b_ref[...])

        def to_sibling(qq, imm):
            rows = pl.ds(pl.multiple_of(imm * tm, tm), tm)
            return pltpu.make_async_remote_copy(
                src_ref=send_ref.at[qq, 0, rows, :], dst_ref=sib_ref.at[qq, rows, :],
                send_sem=d2d_send.at[qq], recv_sem=d2d_recv.at[qq, imm],
                device_id=(x, y, 1 - c), device_id_type=mesh_id)

        def finish_tile(qq, imm):
            rows = pl.ds(pl.multiple_of(imm * tm, tm), tm)
            to_sibling(qq, imm).wait_recv()
            both = stash_ref[...] + sib_ref[qq, rows, :].astype(f32)
            send_ref[qq, 1, rows, :] = both.astype(bf16)
            for step, (dx, dy) in enumerate(CHIP_STEPS):
                @pl.when(qq == step)
                def _(step=step, dx=dx, dy=dy):
                    src, dst = send_ref.at[step, 1, rows, :], recv_ref.at[step, rows, :]
                    if dx or dy:
                        pltpu.make_async_remote_copy(
                            src_ref=src, dst_ref=dst, send_sem=ici_send.at[step], recv_sem=ici_recv.at[step],
                            device_id=(_flip(x, dx), _flip(y, dy), c), device_id_type=mesh_id).start()
                    else:
                        pltpu.make_async_copy(src, dst, local_sem).start()

        @pl.when(t == nt - 1)
        def _():
            tile = q * nm + im

            @pl.when(tile > 0)
            def _():
                finish_tile((tile - 1) // nm, (tile - 1) % nm)

            rows = pl.ds(pl.multiple_of(im * tm, tm), tm)
            for core in (0, 1):
                @pl.when(c == core)
                def _(core=core):
                    other = 1 - core
                    send_ref[q, 0, rows, :] = acc_ref[:, other * w:(other + 1) * w].astype(bf16)
                    stash_ref[...] = acc_ref[:, core * w:(core + 1) * w]
            to_sibling(q, im).start()

            @pl.when(tile == N_CHIP * nm - 1)
            def _():
                finish_tile(q, im)
                for step, (dx, dy) in enumerate(CHIP_STEPS):
                    pltpu.make_async_remote_copy(
                        src_ref=send_ref.at[step, 0], dst_ref=sib_ref.at[step],
                        send_sem=d2d_send.at[step], recv_sem=d2d_recv.at[step, 0],
                        device_id=(x, y, 1 - c), device_id_type=mesh_id).wait_send()
                    src, dst = send_ref.at[step, 1], recv_ref.at[step]
                    if dx or dy:
                        pltpu.make_async_remote_copy(
                            src_ref=src, dst_ref=dst, send_sem=ici_send.at[step], recv_sem=ici_recv.at[step],
                            device_id=(_flip(x, dx), _flip(y, dy), c), device_id_type=mesh_id).wait()
                    else:
                        pltpu.make_async_copy(src, dst, local_sem).wait()

    def body(order_ref, *refs):
        del order_ref
        s = pl.program_id(0)
        operands, rest = refs[:2 * n_prob], refs[2 * n_prob:]
        results, scratch = rest[:n_prob], rest[n_prob:]
        for k, d in enumerate(dims):
            @pl.when((s >= d["first"]) & (s < d["first"] + d["steps"]))
            def _(k=k, d=d):
                q, im, t = place(d, s)
                run(d, q, im, t, operands[2 * k], operands[2 * k + 1], results[k],
                    *scratch[n_scratch * k:n_scratch * (k + 1)])

    in_specs, scratch_shapes = [], []
    for d in dims:
        def a_map(s, order, d=d):
            _, im, t = place(d, s)
            return im, t

        def b_map(s, order, d=d):
            q, _, t = place(d, s)
            return t, order[q]

        in_specs += [pl.BlockSpec((d["tm"], d["tt"]), a_map), pl.BlockSpec((d["tt"], 2 * d["w"]), b_map)]
        scratch_shapes += [pltpu.VMEM((d["tm"], 2 * d["w"]), f32), pltpu.VMEM((N_CHIP, 2, d["m"], d["w"]), bf16),
                           pltpu.VMEM((N_CHIP, d["m"], d["w"]), bf16), pltpu.VMEM((d["tm"], d["w"]), f32),
                           pltpu.SemaphoreType.DMA((N_CHIP,)), pltpu.SemaphoreType.DMA((N_CHIP, d["nm"])),
                           pltpu.SemaphoreType.DMA((N_CHIP - 1,)), pltpu.SemaphoreType.DMA((N_CHIP - 1,)),
                           pltpu.SemaphoreType.DMA]
    grid_spec = pltpu.PrefetchScalarGridSpec(
        num_scalar_prefetch=1, grid=(total,), in_specs=in_specs,
        out_specs=[pl.BlockSpec(memory_space=pl.ANY)] * n_prob, scratch_shapes=scratch_shapes)
    return pl.pallas_call(
        body, name=name, grid_spec=grid_spec,
        out_shape=[_sds((N_CHIP, d["m"], d["w"]), bf16) for d in dims],
        compiler_params=_cparams(("arbitrary",)),
    )(_chip_order(), *[op for a_t, b, _, _ in problems for op in (a_t, b)])


def _adamw(w, g, m, v):
    m = ADAM_B1 * m + (1.0 - ADAM_B1) * g
    v = ADAM_B2 * v + (1.0 - ADAM_B2) * jnp.square(g)
    m_hat = m / (1.0 - ADAM_B1 ** ADAM_STEP)
    v_hat = v / (1.0 - ADAM_B2 ** ADAM_STEP)
    delta = -ADAM_LR * (m_hat / (jnp.sqrt(v_hat) + ADAM_EPS) + ADAM_WD * w)
    return delta, m, v


def _sum_parts(p_ref):
    g = p_ref[0].astype(f32)
    for j in range(1, p_ref.shape[0]):
        g = g + p_ref[j].astype(f32)
    return g


def _adamw_update(w, m, v, parts, name, transposed):
    rows, cols = w.shape
    n_parts = parts.shape[0]
    if transposed:
        tc = 256
        steps = cols // tc
        p_spec = pl.BlockSpec((n_parts, tc, rows), lambda i: (0, i, 0))
        spec = pl.BlockSpec((rows, tc), lambda i: (0, i))
    else:
        tr = next(t for t in (256, 128, 64, 32, 16, 8) if rows % t == 0)
        steps = rows // tr
        p_spec = pl.BlockSpec((n_parts, tr, cols), lambda i: (0, i, 0))
        spec = pl.BlockSpec((tr, cols), lambda i: (i, 0))

    def body(w_ref, p_ref, m_ref, v_ref, g_ref, d_ref, nm_ref, nv_ref):
        g = _sum_parts(p_ref)
        if transposed:
            g = g.T
        g_ref[...] = g
        d_ref[...], nm_ref[...], nv_ref[...] = _adamw(w_ref[...], g, m_ref[...], v_ref[...])

    return pl.pallas_call(
        body, name=name, grid=(steps,),
        in_specs=[spec, p_spec, spec, spec], out_specs=[spec] * 4,
        out_shape=[_sds((rows, cols), f32)] * 4,
        compiler_params=_cparams(("parallel",)),
    )(w, parts, m, v)


def _adamw_whole(ws, ms, vs, parts, transposed):
    n = len(ws)

    def body(*refs):
        w_refs, m_refs, v_refs, p_refs = (refs[j * n:(j + 1) * n] for j in range(4))
        out_refs = refs[4 * n:]
        for a in range(n):
            g = _sum_parts(p_refs[a])
            if transposed[a]:
                g = g.T
            d, nm, nv = _adamw(w_refs[a][...], g, m_refs[a][...], v_refs[a][...])
            for j, val in enumerate((g, d, nm, nv)):
                out_refs[j * n + a][...] = val

    res = pl.pallas_call(
        body, name="adamw_small_weights",
        out_shape=[_sds(w.shape, f32) for _ in range(4) for w in ws],
        compiler_params=_cparams(),
    )(*ws, *ms, *vs, *parts)
    return [res[j * n:(j + 1) * n] for j in range(4)]


SMALL_GROUPS = [
    (["b_gate", "ln1_g", "ln1_b", "b_up", "b_down", "ln2_g", "ln2_b", "ssm_d"], 1),
    (["ssm_lam_re", "ssm_lam_im", "ssm_c_re", "ssm_c_im", "ssm_b_re", "ssm_b_im"], 0),
    (["conv_w"], 0),
    (["ssm_log_dt"], 0),
]


def _sum_small(group_parts):
    def body(*refs):
        n = len(refs) // 2
        for p_ref, o_ref in zip(refs[:n], refs[n:]):
            o_ref[...] = _sum_parts(p_ref)

    return pl.pallas_call(
        body, name="sum_small",
        out_shape=[_sds(p.shape[1:], f32) for p in group_parts],
        compiler_params=_cparams(),
    )(*group_parts)


def _adamw_small(ws, ms, vs, group_sums):
    names = [k for group, _ in SMALL_GROUPS for k in group]
    n = len(names)

    def body(*refs):
        w_refs, m_refs, v_refs = (dict(zip(names, refs[j * n:(j + 1) * n])) for j in range(3))
        p_refs = refs[3 * n:3 * n + len(SMALL_GROUPS)]
        out_refs = [dict(zip(names, refs[3 * n + len(SMALL_GROUPS) + j * n:][:n])) for j in range(4)]
        for (group, axis), p_ref in zip(SMALL_GROUPS, p_refs):
            total = p_ref[...]
            off = 0
            for k in group:
                size = SMALL[k][axis]
                g = total[:, off:off + size] if axis == 1 else total[off:off + size, :]
                off += size
                d, nm, nv = _adamw(w_refs[k][...], g, m_refs[k][...], v_refs[k][...])
                for j, val in enumerate((g, d, nm, nv)):
                    out_refs[j][k][...] = val

    res = pl.pallas_call(
        body, name="adamw_small",
        out_shape=[_sds(SMALL[k], f32) for _ in range(4) for k in names],
        compiler_params=_cparams(),
    )(*[ws[k] for k in names], *[ms[k] for k in names], *[vs[k] for k in names], *group_sums)
    return [dict(zip(names, res[j * n:(j + 1) * n])) for j in range(4)]


def _ssm_discretize(lam_re, lam_im, log_dt, b_re, b_im):
    dt = jnp.exp(log_dt)[:, None]
    mag = jnp.exp(lam_re * dt)
    abar_r = mag * jnp.cos(lam_im * dt)
    abar_i = mag * jnp.sin(lam_im * dt)
    den = lam_re * lam_re + lam_im * lam_im
    nr = abar_r - 1.0
    ni = abar_i
    kr = (nr * lam_re + ni * lam_im) / den
    ki = (ni * lam_re - nr * lam_im) / den
    bbar_r = kr[:, None, :] * b_re - ki[:, None, :] * b_im
    bbar_i = kr[:, None, :] * b_im + ki[:, None, :] * b_re
    return abar_r, abar_i, bbar_r, bbar_i


def _state_layout(re, im):
    parts = []
    for half in range(N_HALF):
        cols = slice(half * HALF_STATE, (half + 1) * HALF_STATE)
        parts += [re[..., cols], im[..., cols]]
    return jnp.concatenate(parts, axis=-1)


def _state_unlayout(a):
    re = jnp.concatenate([a[..., _half_cols(h)[0]] for h in range(N_HALF)], axis=-1)
    im = jnp.concatenate([a[..., _half_cols(h)[1]] for h in range(N_HALF)], axis=-1)
    return re, im


def _abar_powers(abar_r, abar_i):
    pr, pi = abar_r.reshape(1, N_STATE), abar_i.reshape(1, N_STATE)
    while pr.shape[0] < SSM_SEG:
        tr, ti = pr[-1:], pi[-1:]
        pr, pi = (jnp.concatenate([pr, pr * tr - pi * ti], axis=0), jnp.concatenate([pi, pr * ti + pi * tr], axis=0))
    return _state_layout(pr, pi)


HALF_GROUPS = SSM_GROUPS // N_HALF


def _half_block_diag(blocks):
    _, r, c = blocks.shape
    eye = jnp.eye(HALF_GROUPS, dtype=blocks.dtype)
    b4 = blocks.reshape(N_HALF, HALF_GROUPS, r, c)
    return jnp.einsum("ngrc,gk->ngrkc", b4, eye).reshape(N_HALF, HALF_GROUPS * r, HALF_GROUPS * c)


def _half_diag_blocks(mat, r, c):
    eye = jnp.eye(HALF_GROUPS, dtype=mat.dtype)
    m5 = mat.reshape(N_HALF, HALF_GROUPS, r, HALF_GROUPS, c)
    return jnp.einsum("ngrkc,gk->ngrc", m5, eye).reshape(SSM_GROUPS, r, c)


BIG = ["w_in", "w_conv_out", "w_glu", "w_kv", "w_xattn_out", "w_out", "w_up", "w_down"]
GATHER_TRANSPOSED = ["w_conv_out", "w_glu", "w_xattn_out", "w_up"]
PARTS_TRANSPOSED = ["w_in", "w_kv", "w_out", "w_down"]
SMALL = {"b_gate": (1, GATE_COLS), "conv_w": (3, CONV_W), "ssm_lam_re": (SSM_GROUPS, SSM_STATE),
         "ssm_lam_im": (SSM_GROUPS, SSM_STATE), "ssm_log_dt": (1, SSM_GROUPS),
         "ssm_b_re": (SSM_W, SSM_STATE), "ssm_b_im": (SSM_W, SSM_STATE),
         "ssm_c_re": (SSM_W, SSM_STATE), "ssm_c_im": (SSM_W, SSM_STATE), "ssm_d": (1, SSM_W),
         "ln1_g": (1, D_MODEL), "ln1_b": (1, D_MODEL), "b_up": (1, D_FF), "b_down": (1, D_MODEL),
         "ln2_g": (1, D_MODEL), "ln2_b": (1, D_MODEL)}
WEIGHTS = ["w_in", "b_gate", "conv_w", "w_conv_out", "ssm_lam_re", "ssm_lam_im", "ssm_log_dt", "ssm_b_re", "ssm_b_im",
           "ssm_c_re", "ssm_c_im", "ssm_d", "w_glu", "w_kv", "w_xattn_out", "w_out", "ln1_g", "ln1_b", "w_up", "b_up",
           "w_down", "b_down", "ln2_g", "ln2_b"]


def _local_step(x, mem, tgt, full, late, small):
    lam_re, lam_im, log_dt = small["ssm_lam_re"], small["ssm_lam_im"], small["ssm_log_dt"].reshape(SSM_GROUPS)
    c_shape = (SSM_GROUPS, SSM_GROUP, SSM_STATE)
    disc, disc_vjp = jax.vjp(_ssm_discretize, lam_re, lam_im, log_dt,
                             small["ssm_b_re"].reshape(c_shape), small["ssm_b_im"].reshape(c_shape))
    abar_r, abar_i, bbar_r, bbar_i = disc
    pw = _abar_powers(abar_r, abar_i)
    c_re, c_im = small["ssm_c_re"].reshape(c_shape), small["ssm_c_im"].reshape(c_shape)
    b_half = jnp.concatenate([_half_block_diag(bbar_r), _half_block_diag(bbar_i)], axis=2).astype(bf16)
    c_half = jnp.concatenate([_half_block_diag(c_re.transpose(0, 2, 1)), -_half_block_diag(c_im.transpose(0, 2, 1))],
                             axis=1).astype(bf16)

    s_len = x.shape[0]
    stack = lambda a: a.reshape(-1, a.shape[-1])
    kv, k_t, memb = _kv_proj(mem, full["w_kv"])
    (xbt, g, cin, u, q, ain, ob, aint, obt), side = _in_proj(
        x, full["w_in"], small["b_gate"], small["conv_w"], kv,
        [late[k] for k in ("w_glu", "w_conv_out", "w_xattn_out", "w_out", "w_up")])
    w_glu_t, w_co_t, w_xo_t, w_out, w_up_t = (stack(a) for a in side)
    y_ssm, cm_all, side = _ssm_fwd(u, b_half, c_half, pw, small["ssm_d"], [late["w_down"]])
    w_down = stack(side[0])
    ysbt, mb, xhat1, rstd1 = _mid_fwd(y_ssm, g, ain, ob, x, w_glu_t, w_co_t, w_xo_t, w_out,
                                      small["ln1_g"], small["ln1_b"])
    (x1bt, hdn, dr2bt, dpre, dx1, loss, dl2g, dl2b, dbdn, dbup) = _mlp_fwd_bwd(
        xhat1, tgt, small["ln1_g"], small["ln1_b"], w_up_t, small["b_up"], w_down,
        small["b_down"], small["ln2_g"], small["ln2_b"])
    (dxp, dr1bt, dgp, dya, dyc, dglu, dyssm, dl1g, dl1b, dbg) = _mid_bwd(
        dx1, xhat1, rstd1, g, ain, ob, y_ssm, small["ln1_g"], w_out, w_glu_t, w_co_t, w_xo_t)
    du, db_half, dc_half, da8, dd = _ssm_bwd(u, dyssm, cm_all, b_half, c_half, pw, small["ssm_d"])
    dabar_r, dabar_i = _state_unlayout(jnp.sum(da8, axis=0))
    dbbar_r = _half_diag_blocks(db_half[:, :, :HALF_STATE], SSM_GROUP, SSM_STATE)
    dbbar_i = _half_diag_blocks(db_half[:, :, HALF_STATE:], SSM_GROUP, SSM_STATE)
    g_shape = (SSM_GROUPS, SSM_STATE)
    dlam_re, dlam_im, dlog_dt, db_re, db_im = disc_vjp(
        (dabar_r.reshape(g_shape), dabar_i.reshape(g_shape), dbbar_r, dbbar_i))
    dc_re = _half_diag_blocks(dc_half[:, :HALF_STATE, :], SSM_STATE, SSM_GROUP).transpose(0, 2, 1)
    dc_im = -_half_diag_blocks(dc_half[:, HALF_STATE:, :], SSM_STATE, SSM_GROUP).transpose(0, 2, 1)

    small_grads = {
        "b_gate": dbg, "ssm_lam_re": dlam_re, "ssm_lam_im": dlam_im, "ssm_log_dt": dlog_dt,
        "ssm_b_re": db_re, "ssm_b_im": db_im, "ssm_c_re": dc_re, "ssm_c_im": dc_im, "ssm_d": dd,
        "ln1_g": dl1g, "ln1_b": dl1b, "b_up": dbup, "b_down": dbdn, "ln2_g": dl2g, "ln2_b": dl2b,
    }
    small_grads = {k: a.reshape(SMALL[k]) for k, a in small_grads.items()}
    groups = [(group, axis) for group, axis in SMALL_GROUPS if group != ["conv_w"]]
    stacks = [jnp.concatenate([small_grads[k] for k in group], axis=axis) if len(group) > 1 else small_grads[group[0]]
              for group, axis in groups]
    n_rowvec = stacks[0].shape[1]
    stacks[0] = jnp.concatenate([stacks[0], loss], axis=1)
    dense = lambda a: a.reshape(-1, LANES) if a.size % LANES == 0 else a
    dconv, dq, dkv, group_parts = _branch_bwd(dya, dyc, cin, q, kv, k_t, small["conv_w"], w_co_t, w_xo_t,
                                              [dense(a) for a in stacks])
    dx, dproj = _in_proj_bwd(dgp, dconv, du, dq, dxp, full["w_in"])
    tm, tt = 512, 2048
    products = {
        "w_down": (dr2bt, hdn, tm, tt), "w_up": (x1bt, dpre, tm, tt), "w_out": (dr1bt, mb, tm, s_len),
        "w_glu": (ysbt, dglu, tm, s_len), "w_conv_out": (aint, dya, tm, s_len), "w_xattn_out": (obt, dyc, tm, s_len),
        "w_kv": (dkv.T.astype(bf16), memb, D_MODEL, MEM_LEN), "w_in": (xbt, dproj, tm, tt),
    }
    recv = {k: _weight_grads_scatter([problem], "d" + k)[0] for k, problem in products.items()}
    sums = _sum_small(group_parts)
    group_sums = dict(zip([tuple(group) for group, _ in groups], [s.reshape(a.shape) for s, a in zip(sums, stacks)]))
    group_sums[("conv_w",)] = sums[-1][0:3]
    first = tuple(groups[0][0])
    loss_all = group_sums[first][0, n_rowvec]
    group_sums[first] = group_sums[first][:, :n_rowvec]
    return loss_all, dx, recv, [group_sums[tuple(group)] for group, _ in SMALL_GROUPS]


def kernel(x, mem, w_in, b_gate, conv_w, w_conv_out, ssm_lam_re, ssm_lam_im, ssm_log_dt, ssm_b_re, ssm_b_im, ssm_c_re, ssm_c_im, ssm_d, w_glu, w_kv, w_xattn_out, w_out, ln1_g, ln1_b, w_up, b_up, w_down, b_down, ln2_g, ln2_b, loss_target, m_w_in, m_b_gate, m_conv_w, m_w_conv_out, m_ssm_lam_re, m_ssm_lam_im, m_ssm_log_dt, m_ssm_b_re, m_ssm_b_im, m_ssm_c_re, m_ssm_c_im, m_ssm_d, m_w_glu, m_w_kv, m_w_xattn_out, m_w_out, m_ln1_g, m_ln1_b, m_w_up, m_b_up, m_w_down, m_b_down, m_ln2_g, m_ln2_b, v_w_in, v_b_gate, v_conv_w, v_w_conv_out, v_ssm_lam_re, v_ssm_lam_im, v_ssm_log_dt, v_ssm_b_re, v_ssm_b_im, v_ssm_c_re, v_ssm_c_im, v_ssm_d, v_w_glu, v_w_kv, v_w_xattn_out, v_w_out, v_ln1_g, v_ln1_b, v_w_up, v_b_up, v_w_down, v_b_down, v_ln2_g, v_ln2_b):
    w = dict(w_in=w_in, b_gate=b_gate, conv_w=conv_w, w_conv_out=w_conv_out, ssm_lam_re=ssm_lam_re,
             ssm_lam_im=ssm_lam_im, ssm_log_dt=ssm_log_dt, ssm_b_re=ssm_b_re, ssm_b_im=ssm_b_im, ssm_c_re=ssm_c_re,
             ssm_c_im=ssm_c_im, ssm_d=ssm_d, w_glu=w_glu, w_kv=w_kv, w_xattn_out=w_xattn_out, w_out=w_out,
             ln1_g=ln1_g, ln1_b=ln1_b, w_up=w_up, b_up=b_up, w_down=w_down, b_down=b_down, ln2_g=ln2_g, ln2_b=ln2_b)
    m = dict(w_in=m_w_in, b_gate=m_b_gate, conv_w=m_conv_w, w_conv_out=m_w_conv_out, ssm_lam_re=m_ssm_lam_re,
             ssm_lam_im=m_ssm_lam_im, ssm_log_dt=m_ssm_log_dt, ssm_b_re=m_ssm_b_re, ssm_b_im=m_ssm_b_im,
             ssm_c_re=m_ssm_c_re, ssm_c_im=m_ssm_c_im, ssm_d=m_ssm_d, w_glu=m_w_glu, w_kv=m_w_kv,
             w_xattn_out=m_w_xattn_out, w_out=m_w_out, ln1_g=m_ln1_g, ln1_b=m_ln1_b, w_up=m_w_up, b_up=m_b_up,
             w_down=m_w_down, b_down=m_b_down, ln2_g=m_ln2_g, ln2_b=m_ln2_b)
    v = dict(w_in=v_w_in, b_gate=v_b_gate, conv_w=v_conv_w, w_conv_out=v_w_conv_out, ssm_lam_re=v_ssm_lam_re,
             ssm_lam_im=v_ssm_lam_im, ssm_log_dt=v_ssm_log_dt, ssm_b_re=v_ssm_b_re, ssm_b_im=v_ssm_b_im,
             ssm_c_re=v_ssm_c_re, ssm_c_im=v_ssm_c_im, ssm_d=v_ssm_d, w_glu=v_w_glu, w_kv=v_w_kv,
             w_xattn_out=v_w_xattn_out, w_out=v_w_out, ln1_g=v_ln1_g, ln1_b=v_ln1_b, w_up=v_w_up, b_up=v_b_up,
             w_down=v_w_down, b_down=v_b_down, ln2_g=v_ln2_g, ln2_b=v_ln2_b)
    out_shapes = {k: a.shape for k, a in w.items()}
    swapped = ("w_in", "ssm_b_re", "ssm_b_im")

    def shard2d(k, a):
        if k in swapped:
            a = jnp.swapaxes(a, -1, -2)
        if k in SMALL:
            return a.reshape((3, CONV_W // N_DEV) if k == "conv_w" else SMALL[k])
        return a[0]

    def result(k, a):
        if k in swapped:
            shape = out_shapes[k]
            return jnp.swapaxes(a.reshape(shape[:-2] + (shape[-1], shape[-2])), -1, -2)
        return a.reshape(out_shapes[k])

    w, m, v = ({k: shard2d(k, a) for k, a in d.items()} for d in (w, m, v))

    shards = {k: w[k].T.astype(bf16) if k in GATHER_TRANSPOSED else w[k].astype(bf16) for k in BIG}
    conv_pad = jnp.pad(w["conv_w"], ((0, 5), (0, LANES - CONV_W // N_DEV)))
    early = ["w_in", "w_kv"]
    gathered = _all_gather([shards[k] for k in early] + [conv_pad], "gather_weights")
    full = {k: a.reshape(-1, a.shape[-1]) for k, a in zip(early, gathered[:-1])}
    late = {k: shards[k] for k in BIG if k not in early}
    conv_full = gathered[-1][:, :3, :CONV_W // N_DEV].transpose(1, 0, 2).reshape(3, CONV_W)
    small = {k: (conv_full if k == "conv_w" else w[k]) for k in SMALL}

    loss, dx, recv, group_sums = _local_step(x[0], mem[0], loss_target[0], full, late, small)

    grads, deltas, new_m, new_v = {}, {}, {}, {}
    tiled = ["w_in", "w_up", "w_down"]
    for k in tiled:
        res = _adamw_update(w[k], m[k], v[k], recv[k], "adamw_" + k, transposed=k in PARTS_TRANSPOSED)
        grads[k], deltas[k], new_m[k], new_v[k] = res
    whole = [k for k in BIG if k not in tiled]
    res = _adamw_whole([w[k] for k in whole], [m[k] for k in whole], [v[k] for k in whole], [recv[k] for k in whole],
                       [k in PARTS_TRANSPOSED for k in whole])
    for d, vals in zip((grads, deltas, new_m, new_v), res):
        d.update(zip(whole, vals))

    widen = lambda k, a: jnp.tile(a, (1, N_DEV)) if k == "conv_w" else a
    res = _adamw_small(small, {k: widen(k, m[k]) for k in SMALL}, {k: widen(k, v[k]) for k in SMALL}, group_sums)
    dev = _slot(_mesh_place())
    for d, small_res in zip((grads, deltas, new_m, new_v), res):
        for k, a in small_res.items():
            if k == "conv_w":
                a = lax.dynamic_slice_in_dim(a, dev * (CONV_W // N_DEV), CONV_W // N_DEV, axis=1)
            d[k] = a

    outs = [loss, dx[None]]
    for d in (grads, deltas, new_m, new_v):
        outs += [result(k, d[k]) for k in WEIGHTS]
    return tuple(outs)
```

```python
import functools
import math

import jax
import jax.numpy as jnp
from jax import lax
from jax.experimental import pallas as pl
from jax.experimental.pallas import tpu as pltpu

f32 = jnp.float32
bf16 = jnp.bfloat16

D_MODEL = 1024
MEM_LEN = 256
GATE_COLS = 3 * D_MODEL
CONV_W = 512
SSM_W = 512
XATTN_W = 512
HEADS = 4
HEAD_DIM = 128
D_FF = 4096
IN_COLS = GATE_COLS + 3 * CONV_W + SSM_W + XATTN_W
SSM_GROUPS = 32
SSM_GROUP = 16
SSM_STATE = 64
N_STATE = SSM_GROUPS * SSM_STATE
ALPHA = 2.0 ** 0.25
LN_EPS = 1e-5
N_DEV = 8

ADAM_LR = 0.001
ADAM_B1 = 0.9
ADAM_B2 = 0.999
ADAM_EPS = 1e-08
ADAM_WD = 0.01
ADAM_STEP = 10

VMEM_LIMIT_V7X = 56 * 2 ** 20
SUBLANES = 8
LANES = 128

TOKEN_TILE = 256
SSM_BLOCK = 512
SSM_SEG = SSM_BLOCK // SUBLANES
LANE_CHUNK = 256
N_HALF = 2
HALF_W = SSM_W // N_HALF
HALF_STATE = N_STATE // N_HALF
HALF_COLS = 2 * HALF_STATE

NT = (((1,), (1,)), ((), ()))
TN = (((0,), (0,)), ((), ()))
NN = (((1,), (0,)), ((), ()))


def _dot(a, b, dims=NN):
    return lax.dot_general(a, b, dims, preferred_element_type=f32)


def _cparams(sem=None):
    return pltpu.CompilerParams(dimension_semantics=sem, vmem_limit_bytes=VMEM_LIMIT_V7X)


def _row_spec(tm, cols, rev_n=None):
    if rev_n is None:
        return pl.BlockSpec((tm, cols), lambda i: (i, 0))
    return pl.BlockSpec((tm, cols), lambda i: (rev_n - 1 - i, 0))


def _col_spec(rows, tm):
    return pl.BlockSpec((rows, tm), lambda i: (0, i))


def _const_spec(shape):
    nd = len(shape)
    return pl.BlockSpec(shape, lambda *_: (0,) * nd, pipeline_mode=pl.Buffered(1))


def _acc_spec(shape):
    nd = len(shape)
    return pl.BlockSpec(shape, lambda *_: (0,) * nd)


def _sds(shape, dtype):
    return jax.ShapeDtypeStruct(shape, dtype)


def _gelu(x):
    c = math.sqrt(2.0 / math.pi)
    return 0.5 * x * (1.0 + jnp.tanh(c * (x + 0.044715 * x * x * x)))


def _gelu_grad(x):
    c = math.sqrt(2.0 / math.pi)
    t = jnp.tanh(c * (x + 0.044715 * x * x * x))
    return 0.5 * (1.0 + t) + 0.5 * x * (1.0 - t * t) * c * (1.0 + 3.0 * 0.044715 * x * x)


def _colsum(a):
    return jnp.sum(a, axis=0, keepdims=True)


def _mesh_place():
    return lax.axis_index("x"), lax.axis_index("y"), lax.axis_index("c")


def _slot(p):
    return 4 * p[0] + 2 * p[1] + p[2]


def _other_devices(me):
    x, y, c = me
    flip = lambda v, d: 1 - v if d else v
    return [(flip(x, dx), flip(y, dy), flip(c, dc)) for dx in (0, 1) for dy in (0, 1) for dc in (0, 1)][1:]


def _all_gather(blocks, name):
    n = len(blocks)

    def body(*refs):
        ins, outs = refs[:n], refs[n:2 * n]
        send_sems, recv_sems, local_sems = refs[2 * n:]
        x, y, c = _mesh_place()
        me, sibling = (x, y, c), (x, y, 1 - c)
        chips = [(1 - x, y), (x, 1 - y), (1 - x, 1 - y)]

        def copy(a, k, block, to, src=None):
            rows = outs[a].at[_slot(block)]
            return pltpu.make_async_remote_copy(
                src_ref=rows if src is None else src, dst_ref=rows,
                send_sem=send_sems.at[a, k], recv_sem=recv_sems.at[a, k],
                device_id=to, device_id_type=pl.DeviceIdType.MESH)

        mine = [pltpu.make_async_copy(ins[a], outs[a].at[_slot(me)], local_sems.at[a]) for a in range(n)]
        for cp in mine:
            cp.start()
        first = []
        for a in range(n):
            first.append(copy(a, 0, me, sibling, src=ins[a]))
            first += [copy(a, 1 + j, me, (*chip, c), src=ins[a]) for j, chip in enumerate(chips)]
        for cp in first:
            cp.start()
        passed = []
        for a in range(n):
            for j, chip in enumerate(chips):
                copy(a, 1 + j, (*chip, c), me).wait_recv()
                fwd = copy(a, 4 + j, (*chip, c), sibling)
                fwd.start()
                passed.append(fwd)
        for a in range(n):
            copy(a, 0, sibling, me).wait_recv()
            for j, chip in enumerate(chips):
                copy(a, 4 + j, (*chip, 1 - c), me).wait_recv()
        for cp in first + passed:
            cp.wait_send()
        for cp in mine:
            cp.wait()

    any_spec = pl.BlockSpec(memory_space=pl.ANY)
    return pl.pallas_call(
        body, name=name,
        out_shape=[_sds((N_DEV,) + b.shape, b.dtype) for b in blocks],
        in_specs=[any_spec] * n, out_specs=[any_spec] * n,
        scratch_shapes=[pltpu.SemaphoreType.DMA((n, 7)), pltpu.SemaphoreType.DMA((n, 7)),
                        pltpu.SemaphoreType.DMA((n,))],
    )(*blocks)


def _side_gather_copies(ins, outs, send_sems, recv_sems, local_sems):
    me = _mesh_place()
    copies = []
    for a, (src, dst) in enumerate(zip(ins, outs)):
        copies.append(pltpu.make_async_copy(src, dst.at[_slot(me)], local_sems.at[a]))
        for k, peer in enumerate(_other_devices(me)):
            copies.append(pltpu.make_async_remote_copy(
                src_ref=src, dst_ref=dst.at[_slot(me)], send_sem=send_sems.at[a, k], recv_sem=recv_sems.at[a, k],
                device_id=peer, device_id_type=pl.DeviceIdType.MESH))
    return copies


def _side_gather_two_level(ins, outs, send_sems, recv_sems, local_sems):
    x, y, c = _mesh_place()
    me, sibling = (x, y, c), (x, y, 1 - c)
    chips = [(1 - x, y), (x, 1 - y), (1 - x, 1 - y)]

    def copy(a, k, block, to, src=None):
        rows = outs[a].at[_slot(block)]
        return pltpu.make_async_remote_copy(
            src_ref=rows if src is None else src, dst_ref=rows, send_sem=send_sems.at[a, k], recv_sem=recv_sems.at[a, k],
            device_id=to, device_id_type=pl.DeviceIdType.MESH)

    n = len(ins)
    mine = [pltpu.make_async_copy(ins[a], outs[a].at[_slot(me)], local_sems.at[a]) for a in range(n)]
    first = [copy(a, 0, me, sibling, src=ins[a]) for a in range(n)]
    first += [copy(a, 1 + j, me, (*chip, c), src=ins[a]) for a in range(n) for j, chip in enumerate(chips)]
    passed = [copy(a, 4 + j, (*chip, c), sibling) for a in range(n) for j, chip in enumerate(chips)]

    def start():
        for cp in mine + first:
            cp.start()

    def forward():
        for a in range(n):
            for j, chip in enumerate(chips):
                copy(a, 1 + j, (*chip, c), me).wait_recv()
        for cp in passed:
            cp.start()

    def finish():
        for a in range(n):
            copy(a, 0, sibling, me).wait_recv()
            for j, chip in enumerate(chips):
                copy(a, 4 + j, (*chip, 1 - c), me).wait_recv()
        for cp in first + passed:
            cp.wait_send()
        for cp in mine:
            cp.wait()

    return start, forward, finish


def _side_gather_specs(blocks):
    n = len(blocks)
    any_spec = pl.BlockSpec(memory_space=pl.ANY)
    return ([any_spec] * n, [_sds((N_DEV,) + b.shape, b.dtype) for b in blocks],
            [pltpu.SemaphoreType.DMA((n, N_DEV - 1)), pltpu.SemaphoreType.DMA((n, N_DEV - 1)),
             pltpu.SemaphoreType.DMA((n,))])


def _kv_proj(mem, w_kv):
    def body(mem_ref, w_ref, kv_ref, kt_ref, memb_ref):
        mb = mem_ref[...].astype(bf16)
        memb_ref[...] = mb
        kv = _dot(mb, w_ref[...]).astype(bf16)
        kv_ref[...] = kv
        kt_ref[...] = kv[:, :XATTN_W].T

    return pl.pallas_call(
        body, name="kv_proj",
        out_shape=[_sds((MEM_LEN, 2 * XATTN_W), bf16), _sds((XATTN_W, MEM_LEN), bf16), _sds((MEM_LEN, D_MODEL), bf16)],
        compiler_params=_cparams(),
    )(mem, w_kv)


def _attention_probs(qb, kv_ref, h):
    kh = kv_ref[:, h * HEAD_DIM:(h + 1) * HEAD_DIM]
    s = _dot(qb[:, h * HEAD_DIM:(h + 1) * HEAD_DIM], kh, NT) * (HEAD_DIM ** -0.5)
    e = jnp.exp(s - jnp.max(s, axis=-1, keepdims=True))
    return e / jnp.sum(e, axis=-1, keepdims=True)


def _in_proj(x, w_in_t, b_gate, conv_w, kv, side_blocks):
    s_len = x.shape[0]
    tm = 2 * TOKEN_TILE
    n = s_len // tm
    ns = len(side_blocks)
    side_in_specs, side_shapes, side_sems = _side_gather_specs(side_blocks)

    def body(*refs):
        (x_ref, win_ref, bg_ref, cw_ref, kv_ref) = refs[:5]
        side_ins = refs[5:5 + ns]
        (xbt_ref, g_ref, cin_ref, u_ref, q_ref, ain_ref, o_ref, aint_ref, ot_ref) = refs[5 + ns:14 + ns]
        side_outs = refs[14 + ns:14 + 2 * ns]
        zs_ref = refs[14 + 2 * ns]
        side_start, side_forward, side_finish = _side_gather_two_level(side_ins, side_outs, *refs[15 + 2 * ns:])
        i = pl.program_id(0)
        pl.when(i == 0)(side_start)
        pl.when(i == (3 * n) // 4)(side_forward)

        xb = x_ref[...].astype(bf16)
        xbt_ref[...] = xb.T
        proj = _dot(xb, win_ref[...], NT)
        g_ref[...] = jax.nn.sigmoid(proj[:, :GATE_COLS] + bg_ref[...]).astype(bf16)
        cin = proj[:, GATE_COLS:GATE_COLS + 3 * CONV_W]
        cin_ref[...] = cin
        u_ref[...] = proj[:, GATE_COLS + 3 * CONV_W:GATE_COLS + 3 * CONV_W + SSM_W]
        qb = proj[:, IN_COLS - XATTN_W:].astype(bf16)
        q_ref[...] = qb

        cb, cc, ch = cin[:, :CONV_W], cin[:, CONV_W:2 * CONV_W], cin[:, 2 * CONV_W:]
        z = cc * ch

        @pl.when(i == 0)
        def _():
            zs_ref[0:8, :] = jnp.zeros((8, CONV_W), f32)

        zs_ref[8:8 + tm, :] = z
        z1 = zs_ref[pl.ds(7, tm), :]
        z2 = zs_ref[pl.ds(6, tm), :]
        cw = cw_ref[...]
        cz = cw[0:1] * z2 + cw[1:2] * z1 + cw[2:3] * z
        zs_ref[0:8, :] = zs_ref[tm:tm + 8, :]
        ain = (cb * cz).astype(bf16)
        ain_ref[...] = ain
        aint_ref[...] = ain.T

        probs = [_attention_probs(qb, kv_ref, h) for h in range(HEADS)]
        outs = [_dot(probs[h].astype(bf16), kv_ref[:, XATTN_W + h * HEAD_DIM:XATTN_W + (h + 1) * HEAD_DIM])
                for h in range(HEADS)]
        ob = jnp.concatenate(outs, axis=1).astype(bf16)
        o_ref[...] = ob
        ot_ref[...] = ob.T

        pl.when(i == n - 1)(side_finish)

    row_cols = [(GATE_COLS, bf16), (3 * CONV_W, f32), (SSM_W, f32), (XATTN_W, bf16), (CONV_W, bf16), (XATTN_W, bf16)]
    t_rows = [D_MODEL, CONV_W, XATTN_W]
    outs = pl.pallas_call(
        body, name="in_proj", grid=(n,),
        in_specs=[_row_spec(tm, D_MODEL), _const_spec((IN_COLS, D_MODEL)), _const_spec((1, GATE_COLS)),
                  _const_spec((3, CONV_W)), _const_spec((MEM_LEN, 2 * XATTN_W))] + side_in_specs,
        out_specs=([_col_spec(t_rows[0], tm)] + [_row_spec(tm, c) for c, _ in row_cols]
                   + [_col_spec(t_rows[1], tm), _col_spec(t_rows[2], tm)] + side_in_specs),
        out_shape=([_sds((t_rows[0], s_len), bf16)] + [_sds((s_len, c), dt) for c, dt in row_cols]
                   + [_sds((t_rows[1], s_len), bf16), _sds((t_rows[2], s_len), bf16)] + side_shapes),
        scratch_shapes=[pltpu.VMEM((tm + 8, CONV_W), f32)] + side_sems,
        compiler_params=_cparams(("arbitrary",)),
    )(x, w_in_t, b_gate, conv_w, kv, *side_blocks)
    return outs[:9], outs[9:]


def _state_cols(chunk, width=LANE_CHUNK):
    half, off = divmod(chunk * width, HALF_STATE)
    lo = half * HALF_COLS + off
    return slice(lo, lo + width), slice(lo + HALF_STATE, lo + HALF_STATE + width)


def _half_cols(half):
    lo = half * HALF_COLS
    return slice(lo, lo + HALF_STATE), slice(lo + HALF_STATE, lo + HALF_COLS)


def _rows_to_segments(src_ref, stage_ref, dst_ref):
    nc = SSM_W // LANES
    for c in range(nc):
        stage_ref[c] = src_ref[:, c * LANES:(c + 1) * LANES]
    for c in range(nc):
        for k in range(SSM_SEG):
            dst_ref[k * SUBLANES:(k + 1) * SUBLANES, c * LANES:(c + 1) * LANES] = (
                stage_ref[c, pl.ds(k, SUBLANES, stride=SSM_SEG), :])


def _rows_from_segments(src_ref, stage_ref, dst_ref):
    nc = SSM_W // LANES
    for c in range(nc):
        for k in range(SSM_SEG):
            stage_ref[c, pl.ds(k, SUBLANES, stride=SSM_SEG), :] = (
                src_ref[k * SUBLANES:(k + 1) * SUBLANES, c * LANES:(c + 1) * LANES])
    for c in range(nc):
        dst_ref[:, c * LANES:(c + 1) * LANES] = stage_ref[c]


def _ssm_scan(s_ref, pw_ref, init_ref, reverse, unroll, width=LANE_CHUNK):
    for chunk in range(N_STATE // width):
        re, im = _state_cols(chunk, width)
        ar = jnp.broadcast_to(pw_ref[0:1, re], (SUBLANES, width))
        ai = jnp.broadcast_to(pw_ref[0:1, im], (SUBLANES, width))
        if reverse:
            ai = -ai

        def step(j, carry, re=re, im=im, ar=ar, ai=ai):
            sr, si = carry
            k = (SSM_SEG - 1 - j) if reverse else j
            r0 = pl.multiple_of(k * SUBLANES, SUBLANES)
            nr = ar * sr - ai * si + s_ref[pl.ds(r0, SUBLANES), re]
            ni = ar * si + ai * sr + s_ref[pl.ds(r0, SUBLANES), im]
            s_ref[pl.ds(r0, SUBLANES), re] = nr
            s_ref[pl.ds(r0, SUBLANES), im] = ni
            return nr, ni

        if init_ref is None:
            init = (jnp.zeros((SUBLANES, width), f32),) * 2
        else:
            init = (init_ref[:, re], init_ref[:, im])
        lax.fori_loop(0, SSM_SEG, step, init, unroll=unroll)


def _ssm_add_carry(s_ref, pw_ref, cm_ref, reverse):
    for chunk in range(N_STATE // LANE_CHUNK):
        re, im = _state_cols(chunk)
        cr, ci = cm_ref[:, re], cm_ref[:, im]
        for k in range(SSM_SEG):
            pk = (SSM_SEG - 1 - k) if reverse else k
            pr = pw_ref[pk:pk + 1, re]
            pi = pw_ref[pk:pk + 1, im]
            if reverse:
                pi = -pi
            rows = slice(k * SUBLANES, (k + 1) * SUBLANES)
            s_ref[rows, re] = s_ref[rows, re] + (pr * cr - pi * ci)
            s_ref[rows, im] = s_ref[rows, im] + (pr * ci + pi * cr)


def _ssm_carries(first_row, s_ref, pw_ref, carry_ref, cm_ref, reverse):
    order = range(SUBLANES - 1, -1, -1) if reverse else range(SUBLANES)
    for half in range(N_HALF):
        re, im = _half_cols(half)
        a_r, a_i = pw_ref[SSM_SEG - 1:SSM_SEG, re], pw_ref[SSM_SEG - 1:SSM_SEG, im]
        if reverse:
            a_i = -a_i
        cr, ci = carry_ref[0:1, re], carry_ref[0:1, im]
        for seg in order:
            cm_ref[seg:seg + 1, re] = cr
            cm_ref[seg:seg + 1, im] = ci
            er = s_ref[first_row + seg:first_row + seg + 1, re]
            ei = s_ref[first_row + seg:first_row + seg + 1, im]
            cr, ci = a_r * cr - a_i * ci + er, a_r * ci + a_i * cr + ei
        carry_ref[0:1, re] = cr
        carry_ref[0:1, im] = ci


def _ssm_fwd(u, b_half, c_half, pw, d_skip, side_blocks):
    s_len = u.shape[0]
    tb = SSM_BLOCK
    n = s_len // tb
    ns = len(side_blocks)
    side_in_specs, side_shapes, side_sems = _side_gather_specs(side_blocks)

    def body(*refs):
        u_ref, b_ref, c_ref, pw_ref, d_ref = refs[:5]
        side_ins = refs[5:5 + ns]
        y_ref, cm_ref = refs[5 + ns:7 + ns]
        side_outs = refs[7 + ns:7 + 2 * ns]
        s_ref, carry_ref, up_ref, yp_ref, stage_ref = refs[7 + 2 * ns:12 + 2 * ns]
        side_start, side_forward, side_finish = _side_gather_two_level(side_ins, side_outs, *refs[12 + 2 * ns:])
        i = pl.program_id(0)

        @pl.when(i == 0)
        def _():
            carry_ref[...] = jnp.zeros_like(carry_ref)
            side_start()

        pl.when(i == (3 * n) // 4)(side_forward)
        _rows_to_segments(u_ref, stage_ref, up_ref)
        u = up_ref[...]
        ub = u.astype(bf16)
        for half in range(N_HALF):
            s_ref[:, half * HALF_COLS:(half + 1) * HALF_COLS] = _dot(ub[:, half * HALF_W:(half + 1) * HALF_W], b_ref[half])
        _ssm_scan(s_ref, pw_ref, None, reverse=False, unroll=4, width=2 * LANE_CHUNK)
        _ssm_carries(tb - SUBLANES, s_ref, pw_ref, carry_ref, cm_ref, reverse=False)
        _ssm_add_carry(s_ref, pw_ref, cm_ref, reverse=False)
        for half in range(N_HALF):
            cols = slice(half * HALF_W, (half + 1) * HALF_W)
            sb = s_ref[:, half * HALF_COLS:(half + 1) * HALF_COLS].astype(bf16)
            yp_ref[:, cols] = _dot(sb, c_ref[half]) + d_ref[:, cols] * u[:, cols]
        _rows_from_segments(yp_ref, stage_ref, y_ref)

        pl.when(i == n - 1)(side_finish)

    outs = pl.pallas_call(
        body, name="ssm_fwd", grid=(n,),
        in_specs=[_row_spec(tb, SSM_W), _const_spec((N_HALF, HALF_W, HALF_COLS)), _const_spec((N_HALF, HALF_COLS, HALF_W)),
                  _const_spec((SSM_SEG, 2 * N_STATE)), _const_spec((1, SSM_W))] + side_in_specs,
        out_specs=[_row_spec(tb, SSM_W), _row_spec(SUBLANES, 2 * N_STATE)] + side_in_specs,
        out_shape=[_sds((s_len, SSM_W), f32), _sds((n * SUBLANES, 2 * N_STATE), f32)] + side_shapes,
        scratch_shapes=[pltpu.VMEM((tb, 2 * N_STATE), f32), pltpu.VMEM((SUBLANES, 2 * N_STATE), f32),
                        pltpu.VMEM((tb, SSM_W), f32), pltpu.VMEM((tb, SSM_W), f32),
                        pltpu.VMEM((SSM_W // LANES, tb, LANES), f32)] + side_sems,
        compiler_params=_cparams(("arbitrary",)),
    )(u, b_half, c_half, pw, d_skip, *side_blocks)
    return outs[0], outs[1], outs[2:]


def _layer_norm_fwd(r, g, b):
    mu = jnp.mean(r, axis=-1, keepdims=True)
    var = jnp.mean(jnp.square(r - mu), axis=-1, keepdims=True)
    rstd = lax.rsqrt(var + LN_EPS)
    xhat = (r - mu) * rstd
    return xhat, rstd, xhat * g + b


def _layer_norm_bwd(dy, xhat, rstd, g):
    dxh = dy * g
    m1 = jnp.mean(dxh, axis=-1, keepdims=True)
    m2 = jnp.mean(dxh * xhat, axis=-1, keepdims=True)
    return rstd * (dxh - m1 - xhat * m2)


def _branch_outputs(ys_ref, ain_ref, o_ref, wglu_ref, wco_ref, wxo_ref):
    ysb = _gelu(ys_ref[...]).astype(bf16)
    glu = _dot(ysb, wglu_ref[...], NT)
    ga, sb = glu[:, :D_MODEL], jax.nn.sigmoid(glu[:, D_MODEL:])
    ya = _dot(ain_ref[...], wco_ref[...], NT)
    yc = _dot(o_ref[...], wxo_ref[...], NT)
    return ysb, ga, sb, ya, ga * sb, yc


def _mid_fwd(y_ssm, g, ain, ob, x, w_glu_t, w_co_t, w_xo_t, w_out, ln1_g, ln1_b):
    s_len = x.shape[0]
    tm = 2 * TOKEN_TILE
    n = s_len // tm

    def body(ys_ref, g_ref, ain_ref, o_ref, x_ref, wglu_ref, wco_ref, wxo_ref, wout_ref, lg_ref, lb_ref,
             ysbt_ref, mb_ref, xhat_ref, rstd_ref):
        ysb, _, _, ya, yb, yc = _branch_outputs(ys_ref, ain_ref, o_ref, wglu_ref, wco_ref, wxo_ref)
        ysbt_ref[...] = ysb.T
        gt = g_ref[...].astype(f32)
        merged = gt[:, :D_MODEL] * ya + gt[:, D_MODEL:2 * D_MODEL] * yb + gt[:, 2 * D_MODEL:] * yc
        mb = merged.astype(bf16)
        mb_ref[...] = mb
        r1 = ALPHA * x_ref[...] + _dot(mb, wout_ref[...])
        xhat, rstd, _ = _layer_norm_fwd(r1, lg_ref[...], lb_ref[...])
        xhat_ref[...] = xhat
        rstd_ref[...] = rstd

    row_cols = [(D_MODEL, bf16), (D_MODEL, f32), (1, f32)]
    return pl.pallas_call(
        body, name="mid_fwd", grid=(n,),
        in_specs=[_row_spec(tm, SSM_W), _row_spec(tm, GATE_COLS), _row_spec(tm, CONV_W), _row_spec(tm, XATTN_W),
                  _row_spec(tm, D_MODEL), _const_spec((2 * D_MODEL, SSM_W)), _const_spec((D_MODEL, CONV_W)),
                  _const_spec((D_MODEL, XATTN_W)), _const_spec((D_MODEL, D_MODEL)),
                  _const_spec((1, D_MODEL)), _const_spec((1, D_MODEL))],
        out_specs=[_col_spec(SSM_W, tm)] + [_row_spec(tm, c) for c, _ in row_cols],
        out_shape=[_sds((SSM_W, s_len), bf16)] + [_sds((s_len, c), dt) for c, dt in row_cols],
        compiler_params=_cparams(("parallel",)),
    )(y_ssm, g, ain, ob, x, w_glu_t, w_co_t, w_xo_t, w_out, ln1_g, ln1_b)


def _mlp_fwd_bwd(xhat1, tgt, ln1_g, ln1_b, w_up_t, b_up, w_down, b_down, ln2_g, ln2_b):
    s_len = xhat1.shape[0]
    tm = TOKEN_TILE
    n = s_len // tm
    fc = 1024
    nfc = D_FF // fc

    def body(xh_ref, t_ref, l1g_ref, l1b_ref, wup_ref, bup_ref, wdn_ref, bdn_ref, l2g_ref, l2b_ref,
             x1bt_ref, hdn_ref, dr2bt_ref, dpre_ref, dx1_ref,
             loss_ref, dl2g_ref, dl2b_ref, dbdn_ref, dbup_ref, rl_ref):
        i = pl.program_id(0)

        @pl.when(i == 0)
        def _():
            loss_ref[...] = jnp.zeros_like(loss_ref)
            dl2g_ref[...] = jnp.zeros_like(dl2g_ref)
            dl2b_ref[...] = jnp.zeros_like(dl2b_ref)
            dbdn_ref[...] = jnp.zeros_like(dbdn_ref)
            dbup_ref[...] = jnp.zeros_like(dbup_ref)

        x1 = xh_ref[...] * l1g_ref[...] + l1b_ref[...]
        x1b = x1.astype(bf16)
        x1bt_ref[...] = x1b.T
        chunks = [slice(c * fc, (c + 1) * fc) for c in range(nfc)]
        pres = [_dot(x1b, wup_ref[cols, :], NT) for cols in chunks]
        hbs = []
        for cols, pre in zip(chunks, pres):
            rl = jnp.maximum(pre + bup_ref[:, cols], 0.0)
            rl_ref[:, cols] = rl
            hb = (rl * rl).astype(bf16)
            hdn_ref[:, cols] = hb
            hbs.append(hb)
        acc = _dot(hbs[0], wdn_ref[chunks[0], :])
        for cols, hb in zip(chunks[1:], hbs[1:]):
            acc = acc + _dot(hb, wdn_ref[cols, :])
        r2 = ALPHA * x1 + acc + bdn_ref[...]
        xhat2, rstd2, y = _layer_norm_fwd(r2, l2g_ref[...], l2b_ref[...])
        err = y - t_ref[...]
        loss_ref[...] += jnp.sum(jnp.sum(err * err, axis=1, keepdims=True), axis=0, keepdims=True) * (0.5 / D_MODEL)
        dy = err * (1.0 / D_MODEL)
        dl2g_ref[...] += _colsum(dy * xhat2)
        dl2b_ref[...] += _colsum(dy)
        dr2 = _layer_norm_bwd(dy, xhat2, rstd2, l2g_ref[...])
        dbdn_ref[...] += _colsum(dr2)
        dr2b = dr2.astype(bf16)
        dr2bt_ref[...] = dr2b.T
        dhs = [_dot(dr2b, wdn_ref[cols, :], NT) for cols in chunks]
        dpbs = []
        for cols, dh in zip(chunks, dhs):
            dpre = dh * (2.0 * rl_ref[:, cols])
            dbup_ref[:, cols] += _colsum(dpre)
            dpb = dpre.astype(bf16)
            dpre_ref[:, cols] = dpb
            dpbs.append(dpb)
        dacc = _dot(dpbs[0], wup_ref[chunks[0], :])
        for cols, dpb in zip(chunks[1:], dpbs[1:]):
            dacc = dacc + _dot(dpb, wup_ref[cols, :])
        dx1_ref[...] = ALPHA * dr2 + dacc

    acc_shapes = [(1, LANES), (1, D_MODEL), (1, D_MODEL), (1, D_MODEL), (1, D_FF)]
    return pl.pallas_call(
        body, name="mlp_fwd_bwd", grid=(n,),
        in_specs=[_row_spec(tm, D_MODEL), _row_spec(tm, D_MODEL), _const_spec((1, D_MODEL)), _const_spec((1, D_MODEL)),
                  _const_spec((D_FF, D_MODEL)), _const_spec((1, D_FF)), _const_spec((D_FF, D_MODEL)),
                  _const_spec((1, D_MODEL)), _const_spec((1, D_MODEL)), _const_spec((1, D_MODEL))],
        out_specs=([_col_spec(D_MODEL, tm), _row_spec(tm, D_FF), _col_spec(D_MODEL, tm), _row_spec(tm, D_FF),
                    _row_spec(tm, D_MODEL)] + [_acc_spec(s) for s in acc_shapes]),
        out_shape=([_sds((D_MODEL, s_len), bf16), _sds((s_len, D_FF), bf16), _sds((D_MODEL, s_len), bf16),
                    _sds((s_len, D_FF), bf16), _sds((s_len, D_MODEL), f32)] + [_sds(s, f32) for s in acc_shapes]),
        scratch_shapes=[pltpu.VMEM((tm, D_FF), f32)],
        compiler_params=_cparams(("arbitrary",)),
    )(xhat1, tgt, ln1_g, ln1_b, w_up_t, b_up, w_down, b_down, ln2_g, ln2_b)


def _mid_bwd(dx1, xhat1, rstd1, g, ain, ob, y_ssm, ln1_g, w_out, w_glu_t, w_co_t, w_xo_t):
    s_len = dx1.shape[0]
    tm = TOKEN_TILE
    n = s_len // tm

    def body(dx1_ref, xh_ref, rs_ref, g_ref, ain_ref, o_ref, ys_ref, lg_ref, wout_ref, wglu_ref, wco_ref, wxo_ref,
             dxp_ref, dr1bt_ref, dgp_ref, dya_ref, dyc_ref, dglu_ref, dyssm_ref,
             dl1g_ref, dl1b_ref, dbg_ref):
        i = pl.program_id(0)

        @pl.when(i == 0)
        def _():
            dl1g_ref[...] = jnp.zeros_like(dl1g_ref)
            dl1b_ref[...] = jnp.zeros_like(dl1b_ref)
            dbg_ref[...] = jnp.zeros_like(dbg_ref)

        dx1 = dx1_ref[...]
        xhat = xh_ref[...]
        dl1g_ref[...] += _colsum(dx1 * xhat)
        dl1b_ref[...] += _colsum(dx1)
        dr1 = _layer_norm_bwd(dx1, xhat, rs_ref[...], lg_ref[...])
        dxp_ref[...] = ALPHA * dr1
        dr1b = dr1.astype(bf16)
        dr1bt_ref[...] = dr1b.T
        dm = _dot(dr1b, wout_ref[...], NT)

        _, ga, sb, ya, yb, yc = _branch_outputs(ys_ref, ain_ref, o_ref, wglu_ref, wco_ref, wxo_ref)
        gt = g_ref[...].astype(f32)
        branch = (ya, yb, yc)
        for j in range(3):
            cols = slice(j * D_MODEL, (j + 1) * D_MODEL)
            gj = gt[:, cols]
            dgp = dm * branch[j] * gj * (1.0 - gj)
            dbg_ref[:, cols] += _colsum(dgp)
            dgp_ref[:, cols] = dgp.astype(bf16)
        dya_ref[...] = (dm * gt[:, :D_MODEL]).astype(bf16)
        dyc_ref[...] = (dm * gt[:, 2 * D_MODEL:]).astype(bf16)
        dyb = dm * gt[:, D_MODEL:2 * D_MODEL]
        dga = (dyb * sb).astype(bf16)
        dgb = (dyb * ga * sb * (1.0 - sb)).astype(bf16)
        dglu_ref[:, :D_MODEL] = dga
        dglu_ref[:, D_MODEL:] = dgb
        dys = _dot(dga, wglu_ref[:D_MODEL, :]) + _dot(dgb, wglu_ref[D_MODEL:, :])
        dyssm_ref[...] = dys * _gelu_grad(ys_ref[...])

    row_cols = [(GATE_COLS, bf16), (D_MODEL, bf16), (D_MODEL, bf16), (2 * D_MODEL, bf16), (SSM_W, f32)]
    acc_shapes = [(1, D_MODEL), (1, D_MODEL), (1, GATE_COLS)]
    return pl.pallas_call(
        body, name="mid_bwd", grid=(n,),
        in_specs=[_row_spec(tm, D_MODEL), _row_spec(tm, D_MODEL), _row_spec(tm, 1), _row_spec(tm, GATE_COLS),
                  _row_spec(tm, CONV_W), _row_spec(tm, XATTN_W), _row_spec(tm, SSM_W),
                  _const_spec((1, D_MODEL)), _const_spec((D_MODEL, D_MODEL)), _const_spec((2 * D_MODEL, SSM_W)),
                  _const_spec((D_MODEL, CONV_W)), _const_spec((D_MODEL, XATTN_W))],
        out_specs=([_row_spec(tm, D_MODEL), _col_spec(D_MODEL, tm)] + [_row_spec(tm, c) for c, _ in row_cols]
                   + [_acc_spec(s) for s in acc_shapes]),
        out_shape=([_sds((s_len, D_MODEL), f32), _sds((D_MODEL, s_len), bf16)]
                   + [_sds((s_len, c), dt) for c, dt in row_cols] + [_sds(s, f32) for s in acc_shapes]),
        compiler_params=_cparams(("arbitrary",)),
    )(dx1, xhat1, rstd1, g, ain, ob, y_ssm, ln1_g, w_out, w_glu_t, w_co_t, w_xo_t)


def _ssm_bwd(u, dy, cm_all, b_half, c_half, pw, d_skip):
    s_len = u.shape[0]
    tb = SSM_BLOCK
    n = s_len // tb

    def body(u_ref, dy_ref, cm_ref, b_ref, c_ref, pw_ref, d_ref,
             du_ref, db_hbm, dc_hbm, da_ref, dd_ref,
             s_ref, g_ref, gcarry_ref, gcm_ref, db_ref, dc_ref, up_ref, dyp_ref, dup_ref, stage_ref):
        i = pl.program_id(0)

        @pl.when(i == 0)
        def _():
            gcarry_ref[...] = jnp.zeros_like(gcarry_ref)
            db_ref[...] = jnp.zeros_like(db_ref)
            dc_ref[...] = jnp.zeros_like(dc_ref)
            da_ref[...] = jnp.zeros_like(da_ref)
            dd_ref[...] = jnp.zeros_like(dd_ref)

        _rows_to_segments(u_ref, stage_ref, up_ref)
        _rows_to_segments(dy_ref, stage_ref, dyp_ref)
        u = up_ref[...]
        ub = u.astype(bf16)
        dy = dyp_ref[...]
        dyb = dy.astype(bf16)
        dd_ref[...] += _colsum(dy * u)

        for half in range(N_HALF):
            s_ref[:, half * HALF_COLS:(half + 1) * HALF_COLS] = _dot(ub[:, half * HALF_W:(half + 1) * HALF_W], b_ref[half])
        _ssm_scan(s_ref, pw_ref, cm_ref, reverse=False, unroll=True)

        for half in range(N_HALF):
            g_ref[:, half * HALF_COLS:(half + 1) * HALF_COLS] = _dot(dyb[:, half * HALF_W:(half + 1) * HALF_W], c_ref[half], NT)
        _ssm_scan(g_ref, pw_ref, None, reverse=True, unroll=True)
        _ssm_carries(0, g_ref, pw_ref, gcarry_ref, gcm_ref, reverse=True)
        _ssm_add_carry(g_ref, pw_ref, gcm_ref, reverse=True)

        for half in range(N_HALF):
            cols = slice(half * HALF_W, (half + 1) * HALF_W)
            scols = slice(half * HALF_COLS, (half + 1) * HALF_COLS)
            gb = g_ref[:, scols].astype(bf16)
            dup_ref[:, cols] = _dot(gb, b_ref[half], NT) + d_ref[:, cols] * dy[:, cols]
            db_ref[half] += _dot(ub[:, cols], gb, TN)
            dc_ref[half] += _dot(s_ref[:, scols].astype(bf16), dyb[:, cols], TN)
        _rows_from_segments(dup_ref, stage_ref, du_ref)

        for chunk in range(N_STATE // LANE_CHUNK):
            re, im = _state_cols(chunk)
            acc_r = da_ref[:, re]
            acc_i = da_ref[:, im]
            for k in range(SSM_SEG):
                rows = slice(k * SUBLANES, (k + 1) * SUBLANES)
                if k == 0:
                    pr, pi = cm_ref[:, re], cm_ref[:, im]
                else:
                    prev = slice((k - 1) * SUBLANES, k * SUBLANES)
                    pr, pi = s_ref[prev, re], s_ref[prev, im]
                gr, gi = g_ref[rows, re], g_ref[rows, im]
                acc_r = acc_r + (gr * pr + gi * pi)
                acc_i = acc_i + (gi * pr - gr * pi)
            da_ref[:, re] = acc_r
            da_ref[:, im] = acc_i

        @pl.when(i == n - 1)
        def _():
            pltpu.sync_copy(db_ref, db_hbm)
            pltpu.sync_copy(dc_ref, dc_hbm)

    rev = functools.partial(_row_spec, rev_n=n)
    any_spec = pl.BlockSpec(memory_space=pl.ANY)
    state_rows = pltpu.VMEM((tb, 2 * N_STATE), f32)
    seg_rows = pltpu.VMEM((SUBLANES, 2 * N_STATE), f32)
    tok_rows = pltpu.VMEM((tb, SSM_W), f32)
    return pl.pallas_call(
        body, name="ssm_bwd", grid=(n,),
        in_specs=[rev(tb, SSM_W), rev(tb, SSM_W), rev(SUBLANES, 2 * N_STATE),
                  _const_spec((N_HALF, HALF_W, HALF_COLS)), _const_spec((N_HALF, HALF_COLS, HALF_W)),
                  _const_spec((SSM_SEG, 2 * N_STATE)), _const_spec((1, SSM_W))],
        out_specs=[rev(tb, SSM_W), any_spec, any_spec, _acc_spec((SUBLANES, 2 * N_STATE)), _acc_spec((1, SSM_W))],
        out_shape=[_sds((s_len, SSM_W), f32), _sds((N_HALF, HALF_W, HALF_COLS), f32),
                   _sds((N_HALF, HALF_COLS, HALF_W), f32), _sds((SUBLANES, 2 * N_STATE), f32), _sds((1, SSM_W), f32)],
        scratch_shapes=[state_rows, state_rows, seg_rows, seg_rows,
                        pltpu.VMEM((N_HALF, HALF_W, HALF_COLS), f32), pltpu.VMEM((N_HALF, HALF_COLS, HALF_W), f32),
                        tok_rows, tok_rows, tok_rows, pltpu.VMEM((SSM_W // LANES, tb, LANES), f32)],
        compiler_params=_cparams(("arbitrary",)),
    )(u, dy, cm_all, b_half, c_half, pw, d_skip)


def _branch_bwd(dya, dyc, cin, q, kv, k_t, conv_w, w_co_t, w_xo_t, side_blocks):
    s_len = dya.shape[0]
    tm = 2 * TOKEN_TILE
    n = s_len // tm
    halo_blocks = tm // 8
    ns = len(side_blocks)
    conv_tile = _sds((8, CONV_W), f32)
    side_in_specs, side_shapes, side_sems = _side_gather_specs(list(side_blocks) + [conv_tile])

    def body(*refs):
        (dya_ref, dyc_ref, cin_ref, cprev_ref, q_ref, kv_ref, cw_ref, wco_ref, wxo_ref, kt_ref) = refs[:10]
        side_ins = refs[10:10 + ns]
        dconv_ref, dq_ref, dkv_ref = refs[10 + ns:13 + ns]
        side_outs = refs[13 + ns:14 + 2 * ns]
        zs_ref, dczs_ref, dcw_ref = refs[14 + 2 * ns:17 + 2 * ns]
        copies = _side_gather_copies(list(side_ins) + [dcw_ref], side_outs, *refs[17 + 2 * ns:])
        side, conv_side = copies[:ns * N_DEV], copies[ns * N_DEV:]
        i = pl.program_id(0)
        tile = n - 1 - i

        @pl.when(i == 0)
        def _():
            dcw_ref[...] = jnp.zeros_like(dcw_ref)
            dkv_ref[...] = jnp.zeros_like(dkv_ref)
            dczs_ref[tm:tm + 8, :] = jnp.zeros((8, CONV_W), f32)
            for cp in side:
                cp.start()

        cin = cin_ref[...]
        cb, cc, ch = cin[:, :CONV_W], cin[:, CONV_W:2 * CONV_W], cin[:, 2 * CONV_W:]
        z = cc * ch
        cprev = cprev_ref[...]
        zprev = cprev[:, CONV_W:2 * CONV_W] * cprev[:, 2 * CONV_W:]
        zs_ref[0:8, :] = jnp.where(tile == 0, 0.0, zprev)
        zs_ref[8:8 + tm, :] = z
        z1 = zs_ref[pl.ds(7, tm), :]
        z2 = zs_ref[pl.ds(6, tm), :]
        cw = cw_ref[...]
        cz = cw[0:1] * z2 + cw[1:2] * z1 + cw[2:3] * z

        dain = _dot(dya_ref[...], wco_ref[...])
        dcb = dain * cz
        dcz = dain * cb
        dczs_ref[0:tm, :] = dcz
        dcz1 = dczs_ref[pl.ds(1, tm), :]
        dcz2 = dczs_ref[pl.ds(2, tm), :]
        dz = cw[2:3] * dcz + cw[1:2] * dcz1 + cw[0:1] * dcz2
        dczs_ref[tm:tm + 8, :] = dczs_ref[0:8, :]
        dcw_ref[0:1, :] += _colsum(dcz * z2)
        dcw_ref[1:2, :] += _colsum(dcz * z1)
        dcw_ref[2:3, :] += _colsum(dcz * z)
        dconv_ref[:, :CONV_W] = dcb.astype(bf16)
        dconv_ref[:, CONV_W:2 * CONV_W] = (dz * ch).astype(bf16)
        dconv_ref[:, 2 * CONV_W:] = (dz * cc).astype(bf16)

        qb = q_ref[...]
        dob = _dot(dyc_ref[...], wxo_ref[...]).astype(bf16)
        kv = kv_ref[...]
        heads = range(HEADS)
        hcs = [slice(h * HEAD_DIM, (h + 1) * HEAD_DIM) for h in heads]
        vcs = [slice(XATTN_W + h * HEAD_DIM, XATTN_W + (h + 1) * HEAD_DIM) for h in heads]
        s_t = [_dot(kv[:, hcs[h]], qb[:, hcs[h]], NT) * (HEAD_DIM ** -0.5) for h in heads]
        dp_t = [_dot(kv[:, vcs[h]], dob[:, hcs[h]], NT) for h in heads]
        e_t = [jnp.exp(s_t[h] - jnp.max(s_t[h], axis=0, keepdims=True)) for h in heads]
        p_t = [e_t[h] / jnp.sum(e_t[h], axis=0, keepdims=True) for h in heads]
        dv = [_dot(p_t[h].astype(bf16), dob[:, hcs[h]]) for h in heads]
        ds_t = [(p_t[h] * (dp_t[h] - jnp.sum(dp_t[h] * p_t[h], axis=0, keepdims=True)) * (HEAD_DIM ** -0.5)).astype(bf16)
                for h in heads]
        dk = [_dot(ds_t[h], qb[:, hcs[h]]) for h in heads]
        dq_t = [_dot(kt_ref[hcs[h], :], ds_t[h]) for h in heads]
        dq_ref[...] = jnp.concatenate(dq_t, axis=0).T.astype(bf16)
        dkv_ref[...] += jnp.concatenate(dk + dv, axis=1)

        @pl.when(i == n - 1)
        def _():
            for cp in conv_side:
                cp.start()
            for cp in side + conv_side:
                cp.wait()

    rev = functools.partial(_row_spec, rev_n=n)
    prev_spec = pl.BlockSpec((8, 3 * CONV_W), lambda i: (jnp.maximum((n - 1 - i) * halo_blocks - 1, 0), 0))
    outs = pl.pallas_call(
        body, name="branch_bwd", grid=(n,),
        in_specs=[rev(tm, D_MODEL), rev(tm, D_MODEL), rev(tm, 3 * CONV_W), prev_spec, rev(tm, XATTN_W),
                  _const_spec((MEM_LEN, 2 * XATTN_W)), _const_spec((3, CONV_W)), _const_spec((D_MODEL, CONV_W)),
                  _const_spec((D_MODEL, XATTN_W)), _const_spec((XATTN_W, MEM_LEN))] + side_in_specs[:ns],
        out_specs=[rev(tm, 3 * CONV_W), rev(tm, XATTN_W), _acc_spec((MEM_LEN, 2 * XATTN_W))] + side_in_specs,
        out_shape=[_sds((s_len, 3 * CONV_W), bf16), _sds((s_len, XATTN_W), bf16),
                   _sds((MEM_LEN, 2 * XATTN_W), f32)] + side_shapes,
        scratch_shapes=[pltpu.VMEM((tm + 8, CONV_W), f32), pltpu.VMEM((tm + 8, CONV_W), f32),
                        pltpu.VMEM((8, CONV_W), f32)] + side_sems,
        compiler_params=_cparams(("arbitrary",)),
    )(dya, dyc, cin, cin, q, kv, conv_w, w_co_t, w_xo_t, k_t, *side_blocks)
    return outs[0], outs[1], outs[2], outs[3:]


def _in_proj_bwd(dgp, dconv, du, dq, dxp, w_in_t):
    s_len = dgp.shape[0]
    tm = 2 * TOKEN_TILE
    n = s_len // tm

    def body(dgp_ref, dconv_ref, du_ref, dq_ref, dxp_ref, win_ref, dx_ref, dproj_ref):
        dproj = jnp.concatenate([dgp_ref[...], dconv_ref[...], du_ref[...].astype(bf16), dq_ref[...]], axis=1)
        dproj_ref[...] = dproj
        dx_ref[...] = dxp_ref[...] + _dot(dproj, win_ref[...])

    return pl.pallas_call(
        body, name="in_proj_bwd", grid=(n,),
        in_specs=[_row_spec(tm, GATE_COLS), _row_spec(tm, 3 * CONV_W), _row_spec(tm, SSM_W), _row_spec(tm, XATTN_W),
                  _row_spec(tm, D_MODEL), _const_spec((IN_COLS, D_MODEL))],
        out_specs=[_row_spec(tm, D_MODEL), _row_spec(tm, IN_COLS)],
        out_shape=[_sds((s_len, D_MODEL), f32), _sds((s_len, IN_COLS), bf16)],
        compiler_params=_cparams(("parallel",)),
    )(dgp, dconv, du, dq, dxp, w_in_t)


N_CHIP = 4
CHIP_STEPS = [(1, 1), (1, 0), (0, 1), (0, 0)]


def _flip(v, d):
    return 1 - v if d else v


def _chip_order():
    x, y, _ = _mesh_place()
    return jnp.stack([2 * _flip(x, dx) + _flip(y, dy) for dx, dy in CHIP_STEPS]).astype(jnp.int32)


def _weight_grads_scatter(problems, name):
    dims = []
    first = 0
    for a_t, b, tm, tt in problems:
        m, s_len = a_t.shape
        w = b.shape[1] // N_DEV
        tm, tt = min(tm, m), min(tt, s_len)
        assert m % tm == 0 and s_len % tt == 0
        nm, nt = m // tm, s_len // tt
        dims.append(dict(m=m, w=w, tm=tm, tt=tt, nm=nm, nt=nt, first=first, steps=N_CHIP * nm * nt))
        first += N_CHIP * nm * nt
    n_prob, total = len(problems), first
    n_scratch = 9

    def place(d, s):
        local = jnp.clip(s - d["first"], 0, d["steps"] - 1)
        return local // (d["nm"] * d["nt"]), (local // d["nt"]) % d["nm"], local % d["nt"]

    def run(d, q, im, t, a_ref, b_ref, recv_ref, acc_ref, send_ref, sib_ref, stash_ref,
            d2d_send, d2d_recv, ici_send, ici_recv, local_sem):
        tm, w, nm, nt = d["tm"], d["w"], d["nm"], d["nt"]
        x, y, c = _mesh_place()
        mesh_id = pl.DeviceIdType.MESH

        @pl.when(t == 0)
        def _():
            acc_ref[...] = jnp.zeros_like(acc_ref)

        acc_ref[...] += _dot(a_ref[...], b_ref[...])

        def to_sibling(qq, imm):
            rows = pl.ds(pl.multiple_of(imm * tm, tm), tm)
            return pltpu.make_async_remote_copy(
                src_ref=send_ref.at[qq, 0, rows, :], dst_ref=sib_ref.at[qq, rows, :],
                send_sem=d2d_send.at[qq], recv_sem=d2d_recv.at[qq, imm],
                device_id=(x, y, 1 - c), device_id_type=mesh_id)

        def finish_tile(qq, imm):
            rows = pl.ds(pl.multiple_of(imm * tm, tm), tm)
            to_sibling(qq, imm).wait_recv()
            both = stash_ref[...] + sib_ref[qq, rows, :].astype(f32)
            send_ref[qq, 1, rows, :] = both.astype(bf16)
            for step, (dx, dy) in enumerate(CHIP_STEPS):
                @pl.when(qq == step)
                def _(step=step, dx=dx, dy=dy):
                    src, dst = send_ref.at[step, 1, rows, :], recv_ref.at[step, rows, :]
                    if dx or dy:
                        pltpu.make_async_remote_copy(
                            src_ref=src, dst_ref=dst, send_sem=ici_send.at[step], recv_sem=ici_recv.at[step],
                            device_id=(_flip(x, dx), _flip(y, dy), c), device_id_type=mesh_id).start()
                    else:
                        pltpu.make_async_copy(src, dst, local_sem).start()

        @pl.when(t == nt - 1)
        def _():
            tile = q * nm + im

            @pl.when(tile > 0)
            def _():
                finish_tile((tile - 1) // nm, (tile - 1) % nm)

            rows = pl.ds(pl.multiple_of(im * tm, tm), tm)
            for core in (0, 1):
                @pl.when(c == core)
                def _(core=core):
                    other = 1 - core
                    send_ref[q, 0, rows, :] = acc_ref[:, other * w:(other + 1) * w].astype(bf16)
                    stash_ref[...] = acc_ref[:, core * w:(core + 1) * w]
            to_sibling(q, im).start()

            @pl.when(tile == N_CHIP * nm - 1)
            def _():
                finish_tile(q, im)
                for step, (dx, dy) in enumerate(CHIP_STEPS):
                    pltpu.make_async_remote_copy(
                        src_ref=send_ref.at[step, 0], dst_ref=sib_ref.at[step],
                        send_sem=d2d_send.at[step], recv_sem=d2d_recv.at[step, 0],
                        device_id=(x, y, 1 - c), device_id_type=mesh_id).wait_send()
                    src, dst = send_ref.at[step, 1], recv_ref.at[step]
                    if dx or dy:
                        pltpu.make_async_remote_copy(
                            src_ref=src, dst_ref=dst, send_sem=ici_send.at[step], recv_sem=ici_recv.at[step],
                            device_id=(_flip(x, dx), _flip(y, dy), c), device_id_type=mesh_id).wait()
                    else:
                        pltpu.make_async_copy(src, dst, local_sem).wait()

    def body(order_ref, *refs):
        del order_ref
        s = pl.program_id(0)
        operands, rest = refs[:2 * n_prob], refs[2 * n_prob:]
        results, scratch = rest[:n_prob], rest[n_prob:]
        for k, d in enumerate(dims):
            @pl.when((s >= d["first"]) & (s < d["first"] + d["steps"]))
            def _(k=k, d=d):
                q, im, t = place(d, s)
                run(d, q, im, t, operands[2 * k], operands[2 * k + 1], results[k],
                    *scratch[n_scratch * k:n_scratch * (k + 1)])

    in_specs, scratch_shapes = [], []
    for d in dims:
        def a_map(s, order, d=d):
            _, im, t = place(d, s)
            return im, t

        def b_map(s, order, d=d):
            q, _, t = place(d, s)
            return t, order[q]

        in_specs += [pl.BlockSpec((d["tm"], d["tt"]), a_map), pl.BlockSpec((d["tt"], 2 * d["w"]), b_map)]
        scratch_shapes += [pltpu.VMEM((d["tm"], 2 * d["w"]), f32), pltpu.VMEM((N_CHIP, 2, d["m"], d["w"]), bf16),
                           pltpu.VMEM((N_CHIP, d["m"], d["w"]), bf16), pltpu.VMEM((d["tm"], d["w"]), f32),
                           pltpu.SemaphoreType.DMA((N_CHIP,)), pltpu.SemaphoreType.DMA((N_CHIP, d["nm"])),
                           pltpu.SemaphoreType.DMA((N_CHIP - 1,)), pltpu.SemaphoreType.DMA((N_CHIP - 1,)),
                           pltpu.SemaphoreType.DMA]
    grid_spec = pltpu.PrefetchScalarGridSpec(
        num_scalar_prefetch=1, grid=(total,), in_specs=in_specs,
        out_specs=[pl.BlockSpec(memory_space=pl.ANY)] * n_prob, scratch_shapes=scratch_shapes)
    return pl.pallas_call(
        body, name=name, grid_spec=grid_spec,
        out_shape=[_sds((N_CHIP, d["m"], d["w"]), bf16) for d in dims],
        compiler_params=_cparams(("arbitrary",)),
    )(_chip_order(), *[op for a_t, b, _, _ in problems for op in (a_t, b)])


def _adamw(w, g, m, v):
    m = ADAM_B1 * m + (1.0 - ADAM_B1) * g
    v = ADAM_B2 * v + (1.0 - ADAM_B2) * jnp.square(g)
    m_hat = m / (1.0 - ADAM_B1 ** ADAM_STEP)
    v_hat = v / (1.0 - ADAM_B2 ** ADAM_STEP)
    delta = -ADAM_LR * (m_hat / (jnp.sqrt(v_hat) + ADAM_EPS) + ADAM_WD * w)
    return delta, m, v


def _sum_parts(p_ref):
    g = p_ref[0].astype(f32)
    for j in range(1, p_ref.shape[0]):
        g = g + p_ref[j].astype(f32)
    return g


def _adamw_update(w, m, v, parts, name, transposed):
    rows, cols = w.shape
    n_parts = parts.shape[0]
    if transposed:
        tc = 256
        steps = cols // tc
        p_spec = pl.BlockSpec((n_parts, tc, rows), lambda i: (0, i, 0))
        spec = pl.BlockSpec((rows, tc), lambda i: (0, i))
    else:
        tr = next(t for t in (256, 128, 64, 32, 16, 8) if rows % t == 0)
        steps = rows // tr
        p_spec = pl.BlockSpec((n_parts, tr, cols), lambda i: (0, i, 0))
        spec = pl.BlockSpec((tr, cols), lambda i: (i, 0))

    def body(w_ref, p_ref, m_ref, v_ref, g_ref, d_ref, nm_ref, nv_ref):
        g = _sum_parts(p_ref)
        if transposed:
            g = g.T
        g_ref[...] = g
        d_ref[...], nm_ref[...], nv_ref[...] = _adamw(w_ref[...], g, m_ref[...], v_ref[...])

    return pl.pallas_call(
        body, name=name, grid=(steps,),
        in_specs=[spec, p_spec, spec, spec], out_specs=[spec] * 4,
        out_shape=[_sds((rows, cols), f32)] * 4,
        compiler_params=_cparams(("parallel",)),
    )(w, parts, m, v)


def _adamw_whole(ws, ms, vs, parts, transposed):
    n = len(ws)

    def body(*refs):
        w_refs, m_refs, v_refs, p_refs = (refs[j * n:(j + 1) * n] for j in range(4))
        out_refs = refs[4 * n:]
        for a in range(n):
            g = _sum_parts(p_refs[a])
            if transposed[a]:
                g = g.T
            d, nm, nv = _adamw(w_refs[a][...], g, m_refs[a][...], v_refs[a][...])
            for j, val in enumerate((g, d, nm, nv)):
                out_refs[j * n + a][...] = val

    res = pl.pallas_call(
        body, name="adamw_small_weights",
        out_shape=[_sds(w.shape, f32) for _ in range(4) for w in ws],
        compiler_params=_cparams(),
    )(*ws, *ms, *vs, *parts)
    return [res[j * n:(j + 1) * n] for j in range(4)]


SMALL_GROUPS = [
    (["b_gate", "ln1_g", "ln1_b", "b_up", "b_down", "ln2_g", "ln2_b", "ssm_d"], 1),
    (["ssm_lam_re", "ssm_lam_im", "ssm_c_re", "ssm_c_im", "ssm_b_re", "ssm_b_im"], 0),
    (["conv_w"], 0),
    (["ssm_log_dt"], 0),
]


def _sum_small(group_parts):
    def body(*refs):
        n = len(refs) // 2
        for p_ref, o_ref in zip(refs[:n], refs[n:]):
            o_ref[...] = _sum_parts(p_ref)

    return pl.pallas_call(
        body, name="sum_small",
        out_shape=[_sds(p.shape[1:], f32) for p in group_parts],
        compiler_params=_cparams(),
    )(*group_parts)


def _adamw_small(ws, ms, vs, group_sums):
    names = [k for group, _ in SMALL_GROUPS for k in group]
    n = len(names)

    def body(*refs):
        w_refs, m_refs, v_refs = (dict(zip(names, refs[j * n:(j + 1) * n])) for j in range(3))
        p_refs = refs[3 * n:3 * n + len(SMALL_GROUPS)]
        out_refs = [dict(zip(names, refs[3 * n + len(SMALL_GROUPS) + j * n:][:n])) for j in range(4)]
        for (group, axis), p_ref in zip(SMALL_GROUPS, p_refs):
            total = p_ref[...]
            off = 0
            for k in group:
                size = SMALL[k][axis]
                g = total[:, off:off + size] if axis == 1 else total[off:off + size, :]
                off += size
                d, nm, nv = _adamw(w_refs[k][...], g, m_refs[k][...], v_refs[k][...])
                for j, val in enumerate((g, d, nm, nv)):
                    out_refs[j][k][...] = val

    res = pl.pallas_call(
        body, name="adamw_small",
        out_shape=[_sds(SMALL[k], f32) for _ in range(4) for k in names],
        compiler_params=_cparams(),
    )(*[ws[k] for k in names], *[ms[k] for k in names], *[vs[k] for k in names], *group_sums)
    return [dict(zip(names, res[j * n:(j + 1) * n])) for j in range(4)]


def _ssm_discretize(lam_re, lam_im, log_dt, b_re, b_im):
    dt = jnp.exp(log_dt)[:, None]
    mag = jnp.exp(lam_re * dt)
    abar_r = mag * jnp.cos(lam_im * dt)
    abar_i = mag * jnp.sin(lam_im * dt)
    den = lam_re * lam_re + lam_im * lam_im
    nr = abar_r - 1.0
    ni = abar_i
    kr = (nr * lam_re + ni * lam_im) / den
    ki = (ni * lam_re - nr * lam_im) / den
    bbar_r = kr[:, None, :] * b_re - ki[:, None, :] * b_im
    bbar_i = kr[:, None, :] * b_im + ki[:, None, :] * b_re
    return abar_r, abar_i, bbar_r, bbar_i


def _state_layout(re, im):
    parts = []
    for half in range(N_HALF):
        cols = slice(half * HALF_STATE, (half + 1) * HALF_STATE)
        parts += [re[..., cols], im[..., cols]]
    return jnp.concatenate(parts, axis=-1)


def _state_unlayout(a):
    re = jnp.concatenate([a[..., _half_cols(h)[0]] for h in range(N_HALF)], axis=-1)
    im = jnp.concatenate([a[..., _half_cols(h)[1]] for h in range(N_HALF)], axis=-1)
    return re, im


def _abar_powers(abar_r, abar_i):
    pr, pi = abar_r.reshape(1, N_STATE), abar_i.reshape(1, N_STATE)
    while pr.shape[0] < SSM_SEG:
        tr, ti = pr[-1:], pi[-1:]
        pr, pi = (jnp.concatenate([pr, pr * tr - pi * ti], axis=0), jnp.concatenate([pi, pr * ti + pi * tr], axis=0))
    return _state_layout(pr, pi)


HALF_GROUPS = SSM_GROUPS // N_HALF


def _half_block_diag(blocks):
    _, r, c = blocks.shape
    eye = jnp.eye(HALF_GROUPS, dtype=blocks.dtype)
    b4 = blocks.reshape(N_HALF, HALF_GROUPS, r, c)
    return jnp.einsum("ngrc,gk->ngrkc", b4, eye).reshape(N_HALF, HALF_GROUPS * r, HALF_GROUPS * c)


def _half_diag_blocks(mat, r, c):
    eye = jnp.eye(HALF_GROUPS, dtype=mat.dtype)
    m5 = mat.reshape(N_HALF, HALF_GROUPS, r, HALF_GROUPS, c)
    return jnp.einsum("ngrkc,gk->ngrc", m5, eye).reshape(SSM_GROUPS, r, c)


BIG = ["w_in", "w_conv_out", "w_glu", "w_kv", "w_xattn_out", "w_out", "w_up", "w_down"]
GATHER_TRANSPOSED = ["w_conv_out", "w_glu", "w_xattn_out", "w_up"]
PARTS_TRANSPOSED = ["w_in", "w_kv", "w_out", "w_down"]
SMALL = {"b_gate": (1, GATE_COLS), "conv_w": (3, CONV_W), "ssm_lam_re": (SSM_GROUPS, SSM_STATE),
         "ssm_lam_im": (SSM_GROUPS, SSM_STATE), "ssm_log_dt": (1, SSM_GROUPS),
         "ssm_b_re": (SSM_W, SSM_STATE), "ssm_b_im": (SSM_W, SSM_STATE),
         "ssm_c_re": (SSM_W, SSM_STATE), "ssm_c_im": (SSM_W, SSM_STATE), "ssm_d": (1, SSM_W),
         "ln1_g": (1, D_MODEL), "ln1_b": (1, D_MODEL), "b_up": (1, D_FF), "b_down": (1, D_MODEL),
         "ln2_g": (1, D_MODEL), "ln2_b": (1, D_MODEL)}
WEIGHTS = ["w_in", "b_gate", "conv_w", "w_conv_out", "ssm_lam_re", "ssm_lam_im", "ssm_log_dt", "ssm_b_re", "ssm_b_im",
           "ssm_c_re", "ssm_c_im", "ssm_d", "w_glu", "w_kv", "w_xattn_out", "w_out", "ln1_g", "ln1_b", "w_up", "b_up",
           "w_down", "b_down", "ln2_g", "ln2_b"]


def _local_step(x, mem, tgt, full, late, small):
    lam_re, lam_im, log_dt = small["ssm_lam_re"], small["ssm_lam_im"], small["ssm_log_dt"].reshape(SSM_GROUPS)
    c_shape = (SSM_GROUPS, SSM_GROUP, SSM_STATE)
    disc, disc_vjp = jax.vjp(_ssm_discretize, lam_re, lam_im, log_dt,
                             small["ssm_b_re"].reshape(c_shape), small["ssm_b_im"].reshape(c_shape))
    abar_r, abar_i, bbar_r, bbar_i = disc
    pw = _abar_powers(abar_r, abar_i)
    c_re, c_im = small["ssm_c_re"].reshape(c_shape), small["ssm_c_im"].reshape(c_shape)
    b_half = jnp.concatenate([_half_block_diag(bbar_r), _half_block_diag(bbar_i)], axis=2).astype(bf16)
    c_half = jnp.concatenate([_half_block_diag(c_re.transpose(0, 2, 1)), -_half_block_diag(c_im.transpose(0, 2, 1))],
                             axis=1).astype(bf16)

    s_len = x.shape[0]
    stack = lambda a: a.reshape(-1, a.shape[-1])
    kv, k_t, memb = _kv_proj(mem, full["w_kv"])
    (xbt, g, cin, u, q, ain, ob, aint, obt), side = _in_proj(
        x, full["w_in"], small["b_gate"], small["conv_w"], kv,
        [late[k] for k in ("w_glu", "w_conv_out", "w_xattn_out", "w_out", "w_up")])
    w_glu_t, w_co_t, w_xo_t, w_out, w_up_t = (stack(a) for a in side)
    y_ssm, cm_all, side = _ssm_fwd(u, b_half, c_half, pw, small["ssm_d"], [late["w_down"]])
    w_down = stack(side[0])
    ysbt, mb, xhat1, rstd1 = _mid_fwd(y_ssm, g, ain, ob, x, w_glu_t, w_co_t, w_xo_t, w_out,
                                      small["ln1_g"], small["ln1_b"])
    (x1bt, hdn, dr2bt, dpre, dx1, loss, dl2g, dl2b, dbdn, dbup) = _mlp_fwd_bwd(
        xhat1, tgt, small["ln1_g"], small["ln1_b"], w_up_t, small["b_up"], w_down,
        small["b_down"], small["ln2_g"], small["ln2_b"])
    (dxp, dr1bt, dgp, dya, dyc, dglu, dyssm, dl1g, dl1b, dbg) = _mid_bwd(
        dx1, xhat1, rstd1, g, ain, ob, y_ssm, small["ln1_g"], w_out, w_glu_t, w_co_t, w_xo_t)
    du, db_half, dc_half, da8, dd = _ssm_bwd(u, dyssm, cm_all, b_half, c_half, pw, small["ssm_d"])
    dabar_r, dabar_i = _state_unlayout(jnp.sum(da8, axis=0))
    dbbar_r = _half_diag_blocks(db_half[:, :, :HALF_STATE], SSM_GROUP, SSM_STATE)
    dbbar_i = _half_diag_blocks(db_half[:, :, HALF_STATE:], SSM_GROUP, SSM_STATE)
    g_shape = (SSM_GROUPS, SSM_STATE)
    dlam_re, dlam_im, dlog_dt, db_re, db_im = disc_vjp(
        (dabar_r.reshape(g_shape), dabar_i.reshape(g_shape), dbbar_r, dbbar_i))
    dc_re = _half_diag_blocks(dc_half[:, :HALF_STATE, :], SSM_STATE, SSM_GROUP).transpose(0, 2, 1)
    dc_im = -_half_diag_blocks(dc_half[:, HALF_STATE:, :], SSM_STATE, SSM_GROUP).transpose(0, 2, 1)

    small_grads = {
        "b_gate": dbg, "ssm_lam_re": dlam_re, "ssm_lam_im": dlam_im, "ssm_log_dt": dlog_dt,
        "ssm_b_re": db_re, "ssm_b_im": db_im, "ssm_c_re": dc_re, "ssm_c_im": dc_im, "ssm_d": dd,
        "ln1_g": dl1g, "ln1_b": dl1b, "b_up": dbup, "b_down": dbdn, "ln2_g": dl2g, "ln2_b": dl2b,
    }
    small_grads = {k: a.reshape(SMALL[k]) for k, a in small_grads.items()}
    groups = [(group, axis) for group, axis in SMALL_GROUPS if group != ["conv_w"]]
    stacks = [jnp.concatenate([small_grads[k] for k in group], axis=axis) if len(group) > 1 else small_grads[group[0]]
              for group, axis in groups]
    n_rowvec = stacks[0].shape[1]
    stacks[0] = jnp.concatenate([stacks[0], loss], axis=1)
    dense = lambda a: a.reshape(-1, LANES) if a.size % LANES == 0 else a
    dconv, dq, dkv, group_parts = _branch_bwd(dya, dyc, cin, q, kv, k_t, small["conv_w"], w_co_t, w_xo_t,
                                              [dense(a) for a in stacks])
    dx, dproj = _in_proj_bwd(dgp, dconv, du, dq, dxp, full["w_in"])
    tm, tt = 512, 2048
    products = {
        "w_down": (dr2bt, hdn, tm, 2 * tt), "w_up": (x1bt, dpre, tm, 2 * tt), "w_out": (dr1bt, mb, tm, s_len),
        "w_glu": (ysbt, dglu, tm, s_len), "w_conv_out": (aint, dya, tm, s_len), "w_xattn_out": (obt, dyc, tm, s_len),
        "w_in": (xbt, dproj, tm, tt),
    }
    recv = {k: _weight_grads_scatter([problem], "d" + k)[0] for k, problem in products.items() if k != "w_xattn_out"}
    recv["w_xattn_out"], recv["w_kv"] = _weight_grads_scatter(
        [products["w_xattn_out"], (dkv.T.astype(bf16), memb, D_MODEL, MEM_LEN)], "dw_xattn_out_kv")
    sums = _sum_small(group_parts)
    group_sums = dict(zip([tuple(group) for group, _ in groups], [s.reshape(a.shape) for s, a in zip(sums, stacks)]))
    group_sums[("conv_w",)] = sums[-1][0:3]
    first = tuple(groups[0][0])
    loss_all = group_sums[first][0, n_rowvec]
    group_sums[first] = group_sums[first][:, :n_rowvec]
    return loss_all, dx, recv, [group_sums[tuple(group)] for group, _ in SMALL_GROUPS]


def kernel(x, mem, w_in, b_gate, conv_w, w_conv_out, ssm_lam_re, ssm_lam_im, ssm_log_dt, ssm_b_re, ssm_b_im, ssm_c_re, ssm_c_im, ssm_d, w_glu, w_kv, w_xattn_out, w_out, ln1_g, ln1_b, w_up, b_up, w_down, b_down, ln2_g, ln2_b, loss_target, m_w_in, m_b_gate, m_conv_w, m_w_conv_out, m_ssm_lam_re, m_ssm_lam_im, m_ssm_log_dt, m_ssm_b_re, m_ssm_b_im, m_ssm_c_re, m_ssm_c_im, m_ssm_d, m_w_glu, m_w_kv, m_w_xattn_out, m_w_out, m_ln1_g, m_ln1_b, m_w_up, m_b_up, m_w_down, m_b_down, m_ln2_g, m_ln2_b, v_w_in, v_b_gate, v_conv_w, v_w_conv_out, v_ssm_lam_re, v_ssm_lam_im, v_ssm_log_dt, v_ssm_b_re, v_ssm_b_im, v_ssm_c_re, v_ssm_c_im, v_ssm_d, v_w_glu, v_w_kv, v_w_xattn_out, v_w_out, v_ln1_g, v_ln1_b, v_w_up, v_b_up, v_w_down, v_b_down, v_ln2_g, v_ln2_b):
    w = dict(w_in=w_in, b_gate=b_gate, conv_w=conv_w, w_conv_out=w_conv_out, ssm_lam_re=ssm_lam_re,
             ssm_lam_im=ssm_lam_im, ssm_log_dt=ssm_log_dt, ssm_b_re=ssm_b_re, ssm_b_im=ssm_b_im, ssm_c_re=ssm_c_re,
             ssm_c_im=ssm_c_im, ssm_d=ssm_d, w_glu=w_glu, w_kv=w_kv, w_xattn_out=w_xattn_out, w_out=w_out,
             ln1_g=ln1_g, ln1_b=ln1_b, w_up=w_up, b_up=b_up, w_down=w_down, b_down=b_down, ln2_g=ln2_g, ln2_b=ln2_b)
    m = dict(w_in=m_w_in, b_gate=m_b_gate, conv_w=m_conv_w, w_conv_out=m_w_conv_out, ssm_lam_re=m_ssm_lam_re,
             ssm_lam_im=m_ssm_lam_im, ssm_log_dt=m_ssm_log_dt, ssm_b_re=m_ssm_b_re, ssm_b_im=m_ssm_b_im,
             ssm_c_re=m_ssm_c_re, ssm_c_im=m_ssm_c_im, ssm_d=m_ssm_d, w_glu=m_w_glu, w_kv=m_w_kv,
             w_xattn_out=m_w_xattn_out, w_out=m_w_out, ln1_g=m_ln1_g, ln1_b=m_ln1_b, w_up=m_w_up, b_up=m_b_up,
             w_down=m_w_down, b_down=m_b_down, ln2_g=m_ln2_g, ln2_b=m_ln2_b)
    v = dict(w_in=v_w_in, b_gate=v_b_gate, conv_w=v_conv_w, w_conv_out=v_w_conv_out, ssm_lam_re=v_ssm_lam_re,
             ssm_lam_im=v_ssm_lam_im, ssm_log_dt=v_ssm_log_dt, ssm_b_re=v_ssm_b_re, ssm_b_im=v_ssm_b_im,
             ssm_c_re=v_ssm_c_re, ssm_c_im=v_ssm_c_im, ssm_d=v_ssm_d, w_glu=v_w_glu, w_kv=v_w_kv,
             w_xattn_out=v_w_xattn_out, w_out=v_w_out, ln1_g=v_ln1_g, ln1_b=v_ln1_b, w_up=v_w_up, b_up=v_b_up,
             w_down=v_w_down, b_down=v_b_down, ln2_g=v_ln2_g, ln2_b=v_ln2_b)
    out_shapes = {k: a.shape for k, a in w.items()}
    swapped = ("w_in", "ssm_b_re", "ssm_b_im")

    def shard2d(k, a):
        if k in swapped:
            a = jnp.swapaxes(a, -1, -2)
        if k in SMALL:
            return a.reshape((3, CONV_W // N_DEV) if k == "conv_w" else SMALL[k])
        return a[0]

    def result(k, a):
        if k in swapped:
            shape = out_shapes[k]
            return jnp.swapaxes(a.reshape(shape[:-2] + (shape[-1], shape[-2])), -1, -2)
        return a.reshape(out_shapes[k])

    w, m, v = ({k: shard2d(k, a) for k, a in d.items()} for d in (w, m, v))

    shards = {k: w[k].T.astype(bf16) if k in GATHER_TRANSPOSED else w[k].astype(bf16) for k in BIG}
    conv_pad = jnp.pad(w["conv_w"], ((0, 5), (0, LANES - CONV_W // N_DEV)))
    early = ["w_in", "w_kv"]
    gathered = _all_gather([shards[k] for k in early] + [conv_pad], "gather_weights")
    full = {k: a.reshape(-1, a.shape[-1]) for k, a in zip(early, gathered[:-1])}
    late = {k: shards[k] for k in BIG if k not in early}
    conv_full = gathered[-1][:, :3, :CONV_W // N_DEV].transpose(1, 0, 2).reshape(3, CONV_W)
    small = {k: (conv_full if k == "conv_w" else w[k]) for k in SMALL}

    loss, dx, recv, group_sums = _local_step(x[0], mem[0], loss_target[0], full, late, small)

    grads, deltas, new_m, new_v = {}, {}, {}, {}
    tiled = ["w_in", "w_up", "w_down"]
    for k in tiled:
        res = _adamw_update(w[k], m[k], v[k], recv[k], "adamw_" + k, transposed=k in PARTS_TRANSPOSED)
        grads[k], deltas[k], new_m[k], new_v[k] = res
    whole = [k for k in BIG if k not in tiled]
    res = _adamw_whole([w[k] for k in whole], [m[k] for k in whole], [v[k] for k in whole], [recv[k] for k in whole],
                       [k in PARTS_TRANSPOSED for k in whole])
    for d, vals in zip((grads, deltas, new_m, new_v), res):
        d.update(zip(whole, vals))

    widen = lambda k, a: jnp.tile(a, (1, N_DEV)) if k == "conv_w" else a
    res = _adamw_small(small, {k: widen(k, m[k]) for k in SMALL}, {k: widen(k, v[k]) for k in SMALL}, group_sums)
    dev = _slot(_mesh_place())
    for d, small_res in zip((grads, deltas, new_m, new_v), res):
        for k, a in small_res.items():
            if k == "conv_w":
                a = lax.dynamic_slice_in_dim(a, dev * (CONV_W // N_DEV), CONV_W // N_DEV, axis=1)
            d[k] = a

    outs = [loss, dx[None]]
    for d in (grads, deltas, new_m, new_v):
        outs += [result(k, d[k]) for k in WEIGHTS]
    return tuple(outs)
```

```python
import functools
import math

import jax
import jax.numpy as jnp
from jax import lax
from jax.experimental import pallas as pl
from jax.experimental.pallas import tpu as pltpu

f32 = jnp.float32
bf16 = jnp.bfloat16

D_MODEL = 1024
MEM_LEN = 256
GATE_COLS = 3 * D_MODEL
CONV_W = 512
SSM_W = 512
XATTN_W = 512
HEADS = 4
HEAD_DIM = 128
D_FF = 4096
IN_COLS = GATE_COLS + 3 * CONV_W + SSM_W + XATTN_W
SSM_GROUPS = 32
SSM_GROUP = 16
SSM_STATE = 64
N_STATE = SSM_GROUPS * SSM_STATE
ALPHA = 2.0 ** 0.25
LN_EPS = 1e-5
N_DEV = 8

ADAM_LR = 0.001
ADAM_B1 = 0.9
ADAM_B2 = 0.999
ADAM_EPS = 1e-08
ADAM_WD = 0.01
ADAM_STEP = 10

VMEM_LIMIT_V7X = 56 * 2 ** 20
SUBLANES = 8
LANES = 128

TOKEN_TILE = 256
SSM_BLOCK = 512
SSM_SEG = SSM_BLOCK // SUBLANES
LANE_CHUNK = 256
N_HALF = 2
HALF_W = SSM_W // N_HALF
HALF_STATE = N_STATE // N_HALF
HALF_COLS = 2 * HALF_STATE

NT = (((1,), (1,)), ((), ()))
TN = (((0,), (0,)), ((), ()))
NN = (((1,), (0,)), ((), ()))


def _dot(a, b, dims=NN):
    return lax.dot_general(a, b, dims, preferred_element_type=f32)


def _cparams(sem=None):
    return pltpu.CompilerParams(dimension_semantics=sem, vmem_limit_bytes=VMEM_LIMIT_V7X)


def _row_spec(tm, cols, rev_n=None):
    if rev_n is None:
        return pl.BlockSpec((tm, cols), lambda i: (i, 0))
    return pl.BlockSpec((tm, cols), lambda i: (rev_n - 1 - i, 0))


def _col_spec(rows, tm):
    return pl.BlockSpec((rows, tm), lambda i: (0, i))


def _const_spec(shape):
    nd = len(shape)
    return pl.BlockSpec(shape, lambda *_: (0,) * nd, pipeline_mode=pl.Buffered(1))


def _acc_spec(shape):
    nd = len(shape)
    return pl.BlockSpec(shape, lambda *_: (0,) * nd)


def _sds(shape, dtype):
    return jax.ShapeDtypeStruct(shape, dtype)


def _gelu(x):
    c = math.sqrt(2.0 / math.pi)
    return 0.5 * x * (1.0 + jnp.tanh(c * (x + 0.044715 * x * x * x)))


def _gelu_grad(x):
    c = math.sqrt(2.0 / math.pi)
    t = jnp.tanh(c * (x + 0.044715 * x * x * x))
    return 0.5 * (1.0 + t) + 0.5 * x * (1.0 - t * t) * c * (1.0 + 3.0 * 0.044715 * x * x)


def _colsum(a):
    return jnp.sum(a, axis=0, keepdims=True)


def _mesh_place():
    return lax.axis_index("x"), lax.axis_index("y"), lax.axis_index("c")


def _slot(p):
    return 4 * p[0] + 2 * p[1] + p[2]


def _other_devices(me):
    x, y, c = me
    flip = lambda v, d: 1 - v if d else v
    return [(flip(x, dx), flip(y, dy), flip(c, dc)) for dx in (0, 1) for dy in (0, 1) for dc in (0, 1)][1:]


def _all_gather(blocks, name):
    n = len(blocks)

    def body(*refs):
        ins, outs = refs[:n], refs[n:2 * n]
        send_sems, recv_sems, local_sems = refs[2 * n:]
        x, y, c = _mesh_place()
        me, sibling = (x, y, c), (x, y, 1 - c)
        chips = [(1 - x, y), (x, 1 - y), (1 - x, 1 - y)]

        def copy(a, k, block, to, src=None):
            rows = outs[a].at[_slot(block)]
            return pltpu.make_async_remote_copy(
                src_ref=rows if src is None else src, dst_ref=rows,
                send_sem=send_sems.at[a, k], recv_sem=recv_sems.at[a, k],
                device_id=to, device_id_type=pl.DeviceIdType.MESH)

        mine = [pltpu.make_async_copy(ins[a], outs[a].at[_slot(me)], local_sems.at[a]) for a in range(n)]
        for cp in mine:
            cp.start()
        first = []
        for a in range(n):
            first.append(copy(a, 0, me, sibling, src=ins[a]))
            first += [copy(a, 1 + j, me, (*chip, c), src=ins[a]) for j, chip in enumerate(chips)]
        for cp in first:
            cp.start()
        passed = []
        for a in range(n):
            for j, chip in enumerate(chips):
                copy(a, 1 + j, (*chip, c), me).wait_recv()
                fwd = copy(a, 4 + j, (*chip, c), sibling)
                fwd.start()
                passed.append(fwd)
        for a in range(n):
            copy(a, 0, sibling, me).wait_recv()
            for j, chip in enumerate(chips):
                copy(a, 4 + j, (*chip, 1 - c), me).wait_recv()
        for cp in first + passed:
            cp.wait_send()
        for cp in mine:
            cp.wait()

    any_spec = pl.BlockSpec(memory_space=pl.ANY)
    return pl.pallas_call(
        body, name=name,
        out_shape=[_sds((N_DEV,) + b.shape, b.dtype) for b in blocks],
        in_specs=[any_spec] * n, out_specs=[any_spec] * n,
        scratch_shapes=[pltpu.SemaphoreType.DMA((n, 7)), pltpu.SemaphoreType.DMA((n, 7)),
                        pltpu.SemaphoreType.DMA((n,))],
    )(*blocks)


def _side_gather_copies(ins, outs, send_sems, recv_sems, local_sems):
    me = _mesh_place()
    copies = []
    for a, (src, dst) in enumerate(zip(ins, outs)):
        copies.append(pltpu.make_async_copy(src, dst.at[_slot(me)], local_sems.at[a]))
        for k, peer in enumerate(_other_devices(me)):
            copies.append(pltpu.make_async_remote_copy(
                src_ref=src, dst_ref=dst.at[_slot(me)], send_sem=send_sems.at[a, k], recv_sem=recv_sems.at[a, k],
                device_id=peer, device_id_type=pl.DeviceIdType.MESH))
    return copies


def _side_gather_two_level(ins, outs, send_sems, recv_sems, local_sems):
    x, y, c = _mesh_place()
    me, sibling = (x, y, c), (x, y, 1 - c)
    chips = [(1 - x, y), (x, 1 - y), (1 - x, 1 - y)]

    def copy(a, k, block, to, src=None):
        rows = outs[a].at[_slot(block)]
        return pltpu.make_async_remote_copy(
            src_ref=rows if src is None else src, dst_ref=rows, send_sem=send_sems.at[a, k], recv_sem=recv_sems.at[a, k],
            device_id=to, device_id_type=pl.DeviceIdType.MESH)

    n = len(ins)
    mine = [pltpu.make_async_copy(ins[a], outs[a].at[_slot(me)], local_sems.at[a]) for a in range(n)]
    first = [copy(a, 0, me, sibling, src=ins[a]) for a in range(n)]
    first += [copy(a, 1 + j, me, (*chip, c), src=ins[a]) for a in range(n) for j, chip in enumerate(chips)]
    passed = [copy(a, 4 + j, (*chip, c), sibling) for a in range(n) for j, chip in enumerate(chips)]

    def start():
        for cp in mine + first:
            cp.start()

    def forward():
        for a in range(n):
            for j, chip in enumerate(chips):
                copy(a, 1 + j, (*chip, c), me).wait_recv()
        for cp in passed:
            cp.start()

    def finish():
        for a in range(n):
            copy(a, 0, sibling, me).wait_recv()
            for j, chip in enumerate(chips):
                copy(a, 4 + j, (*chip, 1 - c), me).wait_recv()
        for cp in first + passed:
            cp.wait_send()
        for cp in mine:
            cp.wait()

    return start, forward, finish


def _side_gather_specs(blocks):
    n = len(blocks)
    any_spec = pl.BlockSpec(memory_space=pl.ANY)
    return ([any_spec] * n, [_sds((N_DEV,) + b.shape, b.dtype) for b in blocks],
            [pltpu.SemaphoreType.DMA((n, N_DEV - 1)), pltpu.SemaphoreType.DMA((n, N_DEV - 1)),
             pltpu.SemaphoreType.DMA((n,))])


def _kv_proj(mem, w_kv):
    def body(mem_ref, w_ref, kv_ref, kt_ref, memb_ref):
        mb = mem_ref[...].astype(bf16)
        memb_ref[...] = mb
        kv = _dot(mb, w_ref[...]).astype(bf16)
        kv_ref[...] = kv
        kt_ref[...] = kv[:, :XATTN_W].T

    return pl.pallas_call(
        body, name="kv_proj",
        out_shape=[_sds((MEM_LEN, 2 * XATTN_W), bf16), _sds((XATTN_W, MEM_LEN), bf16), _sds((MEM_LEN, D_MODEL), bf16)],
        compiler_params=_cparams(),
    )(mem, w_kv)


def _attention_probs(qb, kv_ref, h):
    kh = kv_ref[:, h * HEAD_DIM:(h + 1) * HEAD_DIM]
    s = _dot(qb[:, h * HEAD_DIM:(h + 1) * HEAD_DIM], kh, NT) * (HEAD_DIM ** -0.5)
    e = jnp.exp(s - jnp.max(s, axis=-1, keepdims=True))
    return e / jnp.sum(e, axis=-1, keepdims=True)


def _in_proj(x, w_in_t, b_gate, conv_w, kv, side_blocks):
    s_len = x.shape[0]
    tm = 2 * TOKEN_TILE
    n = s_len // tm
    ns = len(side_blocks)
    side_in_specs, side_shapes, side_sems = _side_gather_specs(side_blocks)

    def body(*refs):
        (x_ref, win_ref, bg_ref, cw_ref, kv_ref) = refs[:5]
        side_ins = refs[5:5 + ns]
        (xbt_ref, g_ref, cin_ref, u_ref, q_ref, ain_ref, o_ref, aint_ref, ot_ref) = refs[5 + ns:14 + ns]
        side_outs = refs[14 + ns:14 + 2 * ns]
        zs_ref = refs[14 + 2 * ns]
        side_start, side_forward, side_finish = _side_gather_two_level(side_ins, side_outs, *refs[15 + 2 * ns:])
        i = pl.program_id(0)
        pl.when(i == 0)(side_start)
        pl.when(i == (3 * n) // 4)(side_forward)

        xb = x_ref[...].astype(bf16)
        xbt_ref[...] = xb.T
        proj = _dot(xb, win_ref[...], NT)
        g_ref[...] = jax.nn.sigmoid(proj[:, :GATE_COLS] + bg_ref[...]).astype(bf16)
        cin = proj[:, GATE_COLS:GATE_COLS + 3 * CONV_W]
        cin_ref[...] = cin
        u_ref[...] = proj[:, GATE_COLS + 3 * CONV_W:GATE_COLS + 3 * CONV_W + SSM_W]
        qb = proj[:, IN_COLS - XATTN_W:].astype(bf16)
        q_ref[...] = qb

        cb, cc, ch = cin[:, :CONV_W], cin[:, CONV_W:2 * CONV_W], cin[:, 2 * CONV_W:]
        z = cc * ch

        @pl.when(i == 0)
        def _():
            zs_ref[0:8, :] = jnp.zeros((8, CONV_W), f32)

        zs_ref[8:8 + tm, :] = z
        z1 = zs_ref[pl.ds(7, tm), :]
        z2 = zs_ref[pl.ds(6, tm), :]
        cw = cw_ref[...]
        cz = cw[0:1] * z2 + cw[1:2] * z1 + cw[2:3] * z
        zs_ref[0:8, :] = zs_ref[tm:tm + 8, :]
        ain = (cb * cz).astype(bf16)
        ain_ref[...] = ain
        aint_ref[...] = ain.T

        probs = [_attention_probs(qb, kv_ref, h) for h in range(HEADS)]
        outs = [_dot(probs[h].astype(bf16), kv_ref[:, XATTN_W + h * HEAD_DIM:XATTN_W + (h + 1) * HEAD_DIM])
                for h in range(HEADS)]
        ob = jnp.concatenate(outs, axis=1).astype(bf16)
        o_ref[...] = ob
        ot_ref[...] = ob.T

        pl.when(i == n - 1)(side_finish)

    row_cols = [(GATE_COLS, bf16), (3 * CONV_W, f32), (SSM_W, f32), (XATTN_W, bf16), (CONV_W, bf16), (XATTN_W, bf16)]
    t_rows = [D_MODEL, CONV_W, XATTN_W]
    outs = pl.pallas_call(
        body, name="in_proj", grid=(n,),
        in_specs=[_row_spec(tm, D_MODEL), _const_spec((IN_COLS, D_MODEL)), _const_spec((1, GATE_COLS)),
                  _const_spec((3, CONV_W)), _const_spec((MEM_LEN, 2 * XATTN_W))] + side_in_specs,
        out_specs=([_col_spec(t_rows[0], tm)] + [_row_spec(tm, c) for c, _ in row_cols]
                   + [_col_spec(t_rows[1], tm), _col_spec(t_rows[2], tm)] + side_in_specs),
        out_shape=([_sds((t_rows[0], s_len), bf16)] + [_sds((s_len, c), dt) for c, dt in row_cols]
                   + [_sds((t_rows[1], s_len), bf16), _sds((t_rows[2], s_len), bf16)] + side_shapes),
        scratch_shapes=[pltpu.VMEM((tm + 8, CONV_W), f32)] + side_sems,
        compiler_params=_cparams(("arbitrary",)),
    )(x, w_in_t, b_gate, conv_w, kv, *side_blocks)
    return outs[:9], outs[9:]


def _state_cols(chunk, width=LANE_CHUNK):
    half, off = divmod(chunk * width, HALF_STATE)
    lo = half * HALF_COLS + off
    return slice(lo, lo + width), slice(lo + HALF_STATE, lo + HALF_STATE + width)


def _half_cols(half):
    lo = half * HALF_COLS
    return slice(lo, lo + HALF_STATE), slice(lo + HALF_STATE, lo + HALF_COLS)


def _rows_to_segments(src_ref, stage_ref, dst_ref):
    nc = SSM_W // LANES
    for c in range(nc):
        stage_ref[c] = src_ref[:, c * LANES:(c + 1) * LANES]
    for c in range(nc):
        for k in range(SSM_SEG):
            dst_ref[k * SUBLANES:(k + 1) * SUBLANES, c * LANES:(c + 1) * LANES] = (
                stage_ref[c, pl.ds(k, SUBLANES, stride=SSM_SEG), :])


def _rows_from_segments(src_ref, stage_ref, dst_ref):
    nc = SSM_W // LANES
    for c in range(nc):
        for k in range(SSM_SEG):
            stage_ref[c, pl.ds(k, SUBLANES, stride=SSM_SEG), :] = (
                src_ref[k * SUBLANES:(k + 1) * SUBLANES, c * LANES:(c + 1) * LANES])
    for c in range(nc):
        dst_ref[:, c * LANES:(c + 1) * LANES] = stage_ref[c]


def _ssm_scan(s_ref, pw_ref, init_ref, reverse, unroll, width=LANE_CHUNK):
    for chunk in range(N_STATE // width):
        re, im = _state_cols(chunk, width)
        ar = jnp.broadcast_to(pw_ref[0:1, re], (SUBLANES, width))
        ai = jnp.broadcast_to(pw_ref[0:1, im], (SUBLANES, width))
        if reverse:
            ai = -ai

        def step(j, carry, re=re, im=im, ar=ar, ai=ai):
            sr, si = carry
            k = (SSM_SEG - 1 - j) if reverse else j
            r0 = pl.multiple_of(k * SUBLANES, SUBLANES)
            nr = ar * sr - ai * si + s_ref[pl.ds(r0, SUBLANES), re]
            ni = ar * si + ai * sr + s_ref[pl.ds(r0, SUBLANES), im]
            s_ref[pl.ds(r0, SUBLANES), re] = nr
            s_ref[pl.ds(r0, SUBLANES), im] = ni
            return nr, ni

        if init_ref is None:
            init = (jnp.zeros((SUBLANES, width), f32),) * 2
        else:
            init = (init_ref[:, re], init_ref[:, im])
        lax.fori_loop(0, SSM_SEG, step, init, unroll=unroll)


def _ssm_add_carry(s_ref, pw_ref, cm_ref, reverse):
    for chunk in range(N_STATE // LANE_CHUNK):
        re, im = _state_cols(chunk)
        cr, ci = cm_ref[:, re], cm_ref[:, im]
        for k in range(SSM_SEG):
            pk = (SSM_SEG - 1 - k) if reverse else k
            pr = pw_ref[pk:pk + 1, re]
            pi = pw_ref[pk:pk + 1, im]
            if reverse:
                pi = -pi
            rows = slice(k * SUBLANES, (k + 1) * SUBLANES)
            s_ref[rows, re] = s_ref[rows, re] + (pr * cr - pi * ci)
            s_ref[rows, im] = s_ref[rows, im] + (pr * ci + pi * cr)


def _ssm_carries(first_row, s_ref, pw_ref, carry_ref, cm_ref, reverse):
    order = range(SUBLANES - 1, -1, -1) if reverse else range(SUBLANES)
    for half in range(N_HALF):
        re, im = _half_cols(half)
        a_r, a_i = pw_ref[SSM_SEG - 1:SSM_SEG, re], pw_ref[SSM_SEG - 1:SSM_SEG, im]
        if reverse:
            a_i = -a_i
        cr, ci = carry_ref[0:1, re], carry_ref[0:1, im]
        for seg in order:
            cm_ref[seg:seg + 1, re] = cr
            cm_ref[seg:seg + 1, im] = ci
            er = s_ref[first_row + seg:first_row + seg + 1, re]
            ei = s_ref[first_row + seg:first_row + seg + 1, im]
            cr, ci = a_r * cr - a_i * ci + er, a_r * ci + a_i * cr + ei
        carry_ref[0:1, re] = cr
        carry_ref[0:1, im] = ci


def _ssm_fwd(u, b_half, c_half, pw, d_skip, side_blocks):
    s_len = u.shape[0]
    tb = SSM_BLOCK
    n = s_len // tb
    ns = len(side_blocks)
    side_in_specs, side_shapes, side_sems = _side_gather_specs(side_blocks)

    def body(*refs):
        u_ref, b_ref, c_ref, pw_ref, d_ref = refs[:5]
        side_ins = refs[5:5 + ns]
        y_ref, cm_ref = refs[5 + ns:7 + ns]
        side_outs = refs[7 + ns:7 + 2 * ns]
        s_ref, carry_ref, up_ref, yp_ref, stage_ref = refs[7 + 2 * ns:12 + 2 * ns]
        side_start, side_forward, side_finish = _side_gather_two_level(side_ins, side_outs, *refs[12 + 2 * ns:])
        i = pl.program_id(0)

        @pl.when(i == 0)
        def _():
            carry_ref[...] = jnp.zeros_like(carry_ref)
            side_start()

        pl.when(i == (3 * n) // 4)(side_forward)
        _rows_to_segments(u_ref, stage_ref, up_ref)
        u = up_ref[...]
        ub = u.astype(bf16)
        for half in range(N_HALF):
            s_ref[:, half * HALF_COLS:(half + 1) * HALF_COLS] = _dot(ub[:, half * HALF_W:(half + 1) * HALF_W], b_ref[half])
        _ssm_scan(s_ref, pw_ref, None, reverse=False, unroll=4, width=2 * LANE_CHUNK)
        _ssm_carries(tb - SUBLANES, s_ref, pw_ref, carry_ref, cm_ref, reverse=False)
        _ssm_add_carry(s_ref, pw_ref, cm_ref, reverse=False)
        for half in range(N_HALF):
            cols = slice(half * HALF_W, (half + 1) * HALF_W)
            sb = s_ref[:, half * HALF_COLS:(half + 1) * HALF_COLS].astype(bf16)
            yp_ref[:, cols] = _dot(sb, c_ref[half]) + d_ref[:, cols] * u[:, cols]
        _rows_from_segments(yp_ref, stage_ref, y_ref)

        pl.when(i == n - 1)(side_finish)

    outs = pl.pallas_call(
        body, name="ssm_fwd", grid=(n,),
        in_specs=[_row_spec(tb, SSM_W), _const_spec((N_HALF, HALF_W, HALF_COLS)), _const_spec((N_HALF, HALF_COLS, HALF_W)),
                  _const_spec((SSM_SEG, 2 * N_STATE)), _const_spec((1, SSM_W))] + side_in_specs,
        out_specs=[_row_spec(tb, SSM_W), _row_spec(SUBLANES, 2 * N_STATE)] + side_in_specs,
        out_shape=[_sds((s_len, SSM_W), f32), _sds((n * SUBLANES, 2 * N_STATE), f32)] + side_shapes,
        scratch_shapes=[pltpu.VMEM((tb, 2 * N_STATE), f32), pltpu.VMEM((SUBLANES, 2 * N_STATE), f32),
                        pltpu.VMEM((tb, SSM_W), f32), pltpu.VMEM((tb, SSM_W), f32),
                        pltpu.VMEM((SSM_W // LANES, tb, LANES), f32)] + side_sems,
        compiler_params=_cparams(("arbitrary",)),
    )(u, b_half, c_half, pw, d_skip, *side_blocks)
    return outs[0], outs[1], outs[2:]


def _layer_norm_fwd(r, g, b):
    mu = jnp.mean(r, axis=-1, keepdims=True)
    var = jnp.mean(jnp.square(r - mu), axis=-1, keepdims=True)
    rstd = lax.rsqrt(var + LN_EPS)
    xhat = (r - mu) * rstd
    return xhat, rstd, xhat * g + b


def _layer_norm_bwd(dy, xhat, rstd, g):
    dxh = dy * g
    m1 = jnp.mean(dxh, axis=-1, keepdims=True)
    m2 = jnp.mean(dxh * xhat, axis=-1, keepdims=True)
    return rstd * (dxh - m1 - xhat * m2)


def _branch_outputs(ys_ref, ain_ref, o_ref, wglu_ref, wco_ref, wxo_ref):
    ysb = _gelu(ys_ref[...]).astype(bf16)
    glu = _dot(ysb, wglu_ref[...], NT)
    ga, sb = glu[:, :D_MODEL], jax.nn.sigmoid(glu[:, D_MODEL:])
    ya = _dot(ain_ref[...], wco_ref[...], NT)
    yc = _dot(o_ref[...], wxo_ref[...], NT)
    return ysb, ga, sb, ya, ga * sb, yc


def _mid_fwd(y_ssm, g, ain, ob, x, w_glu_t, w_co_t, w_xo_t, w_out, ln1_g, ln1_b):
    s_len = x.shape[0]
    tm = 2 * TOKEN_TILE
    n = s_len // tm

    def body(ys_ref, g_ref, ain_ref, o_ref, x_ref, wglu_ref, wco_ref, wxo_ref, wout_ref, lg_ref, lb_ref,
             ysbt_ref, mb_ref, xhat_ref, rstd_ref):
        ysb, _, _, ya, yb, yc = _branch_outputs(ys_ref, ain_ref, o_ref, wglu_ref, wco_ref, wxo_ref)
        ysbt_ref[...] = ysb.T
        gt = g_ref[...].astype(f32)
        merged = gt[:, :D_MODEL] * ya + gt[:, D_MODEL:2 * D_MODEL] * yb + gt[:, 2 * D_MODEL:] * yc
        mb = merged.astype(bf16)
        mb_ref[...] = mb
        r1 = ALPHA * x_ref[...] + _dot(mb, wout_ref[...])
        xhat, rstd, _ = _layer_norm_fwd(r1, lg_ref[...], lb_ref[...])
        xhat_ref[...] = xhat
        rstd_ref[...] = rstd

    row_cols = [(D_MODEL, bf16), (D_MODEL, f32), (1, f32)]
    return pl.pallas_call(
        body, name="mid_fwd", grid=(n,),
        in_specs=[_row_spec(tm, SSM_W), _row_spec(tm, GATE_COLS), _row_spec(tm, CONV_W), _row_spec(tm, XATTN_W),
                  _row_spec(tm, D_MODEL), _const_spec((2 * D_MODEL, SSM_W)), _const_spec((D_MODEL, CONV_W)),
                  _const_spec((D_MODEL, XATTN_W)), _const_spec((D_MODEL, D_MODEL)),
                  _const_spec((1, D_MODEL)), _const_spec((1, D_MODEL))],
        out_specs=[_col_spec(SSM_W, tm)] + [_row_spec(tm, c) for c, _ in row_cols],
        out_shape=[_sds((SSM_W, s_len), bf16)] + [_sds((s_len, c), dt) for c, dt in row_cols],
        compiler_params=_cparams(("parallel",)),
    )(y_ssm, g, ain, ob, x, w_glu_t, w_co_t, w_xo_t, w_out, ln1_g, ln1_b)


def _mlp_fwd_bwd(xhat1, tgt, ln1_g, ln1_b, w_up_t, b_up, w_down, b_down, ln2_g, ln2_b):
    s_len = xhat1.shape[0]
    tm = TOKEN_TILE
    n = s_len // tm
    fc = 1024
    nfc = D_FF // fc

    def body(xh_ref, t_ref, l1g_ref, l1b_ref, wup_ref, bup_ref, wdn_ref, bdn_ref, l2g_ref, l2b_ref,
             x1bt_ref, hdn_ref, dr2bt_ref, dpre_ref, dx1_ref,
             loss_ref, dl2g_ref, dl2b_ref, dbdn_ref, dbup_ref, rl_ref):
        i = pl.program_id(0)

        @pl.when(i == 0)
        def _():
            loss_ref[...] = jnp.zeros_like(loss_ref)
            dl2g_ref[...] = jnp.zeros_like(dl2g_ref)
            dl2b_ref[...] = jnp.zeros_like(dl2b_ref)
            dbdn_ref[...] = jnp.zeros_like(dbdn_ref)
            dbup_ref[...] = jnp.zeros_like(dbup_ref)

        x1 = xh_ref[...] * l1g_ref[...] + l1b_ref[...]
        x1b = x1.astype(bf16)
        x1bt_ref[...] = x1b.T
        chunks = [slice(c * fc, (c + 1) * fc) for c in range(nfc)]
        pres = [_dot(x1b, wup_ref[cols, :], NT) for cols in chunks]
        hbs = []
        for cols, pre in zip(chunks, pres):
            rl = jnp.maximum(pre + bup_ref[:, cols], 0.0)
            rl_ref[:, cols] = rl
            hb = (rl * rl).astype(bf16)
            hdn_ref[:, cols] = hb
            hbs.append(hb)
        acc = _dot(hbs[0], wdn_ref[chunks[0], :])
        for cols, hb in zip(chunks[1:], hbs[1:]):
            acc = acc + _dot(hb, wdn_ref[cols, :])
        r2 = ALPHA * x1 + acc + bdn_ref[...]
        xhat2, rstd2, y = _layer_norm_fwd(r2, l2g_ref[...], l2b_ref[...])
        err = y - t_ref[...]
        loss_ref[...] += jnp.sum(jnp.sum(err * err, axis=1, keepdims=True), axis=0, keepdims=True) * (0.5 / D_MODEL)
        dy = err * (1.0 / D_MODEL)
        dl2g_ref[...] += _colsum(dy * xhat2)
        dl2b_ref[...] += _colsum(dy)
        dr2 = _layer_norm_bwd(dy, xhat2, rstd2, l2g_ref[...])
        dbdn_ref[...] += _colsum(dr2)
        dr2b = dr2.astype(bf16)
        dr2bt_ref[...] = dr2b.T
        dhs = [_dot(dr2b, wdn_ref[cols, :], NT) for cols in chunks]
        dpbs = []
        for cols, dh in zip(chunks, dhs):
            dpre = dh * (2.0 * rl_ref[:, cols])
            dbup_ref[:, cols] += _colsum(dpre)
            dpb = dpre.astype(bf16)
            dpre_ref[:, cols] = dpb
            dpbs.append(dpb)
        dacc = _dot(dpbs[0], wup_ref[chunks[0], :])
        for cols, dpb in zip(chunks[1:], dpbs[1:]):
            dacc = dacc + _dot(dpb, wup_ref[cols, :])
        dx1_ref[...] = ALPHA * dr2 + dacc

    acc_shapes = [(1, LANES), (1, D_MODEL), (1, D_MODEL), (1, D_MODEL), (1, D_FF)]
    return pl.pallas_call(
        body, name="mlp_fwd_bwd", grid=(n,),
        in_specs=[_row_spec(tm, D_MODEL), _row_spec(tm, D_MODEL), _const_spec((1, D_MODEL)), _const_spec((1, D_MODEL)),
                  _const_spec((D_FF, D_MODEL)), _const_spec((1, D_FF)), _const_spec((D_FF, D_MODEL)),
                  _const_spec((1, D_MODEL)), _const_spec((1, D_MODEL)), _const_spec((1, D_MODEL))],
        out_specs=([_col_spec(D_MODEL, tm), _row_spec(tm, D_FF), _col_spec(D_MODEL, tm), _row_spec(tm, D_FF),
                    _row_spec(tm, D_MODEL)] + [_acc_spec(s) for s in acc_shapes]),
        out_shape=([_sds((D_MODEL, s_len), bf16), _sds((s_len, D_FF), bf16), _sds((D_MODEL, s_len), bf16),
                    _sds((s_len, D_FF), bf16), _sds((s_len, D_MODEL), f32)] + [_sds(s, f32) for s in acc_shapes]),
        scratch_shapes=[pltpu.VMEM((tm, D_FF), f32)],
        compiler_params=_cparams(("arbitrary",)),
    )(xhat1, tgt, ln1_g, ln1_b, w_up_t, b_up, w_down, b_down, ln2_g, ln2_b)


def _mid_bwd(dx1, xhat1, rstd1, g, ain, ob, y_ssm, ln1_g, w_out, w_glu_t, w_co_t, w_xo_t):
    s_len = dx1.shape[0]
    tm = TOKEN_TILE
    n = s_len // tm

    def body(dx1_ref, xh_ref, rs_ref, g_ref, ain_ref, o_ref, ys_ref, lg_ref, wout_ref, wglu_ref, wco_ref, wxo_ref,
             dxp_ref, dr1bt_ref, dgp_ref, dya_ref, dyc_ref, dglu_ref, dyssm_ref,
             dl1g_ref, dl1b_ref, dbg_ref):
        i = pl.program_id(0)

        @pl.when(i == 0)
        def _():
            dl1g_ref[...] = jnp.zeros_like(dl1g_ref)
            dl1b_ref[...] = jnp.zeros_like(dl1b_ref)
            dbg_ref[...] = jnp.zeros_like(dbg_ref)

        dx1 = dx1_ref[...]
        xhat = xh_ref[...]
        dl1g_ref[...] += _colsum(dx1 * xhat)
        dl1b_ref[...] += _colsum(dx1)
        dr1 = _layer_norm_bwd(dx1, xhat, rs_ref[...], lg_ref[...])
        dxp_ref[...] = ALPHA * dr1
        dr1b = dr1.astype(bf16)
        dr1bt_ref[...] = dr1b.T
        dm = _dot(dr1b, wout_ref[...], NT)

        _, ga, sb, ya, yb, yc = _branch_outputs(ys_ref, ain_ref, o_ref, wglu_ref, wco_ref, wxo_ref)
        gt = g_ref[...].astype(f32)
        branch = (ya, yb, yc)
        for j in range(3):
            cols = slice(j * D_MODEL, (j + 1) * D_MODEL)
            gj = gt[:, cols]
            dgp = dm * branch[j] * gj * (1.0 - gj)
            dbg_ref[:, cols] += _colsum(dgp)
            dgp_ref[:, cols] = dgp.astype(bf16)
        dya_ref[...] = (dm * gt[:, :D_MODEL]).astype(bf16)
        dyc_ref[...] = (dm * gt[:, 2 * D_MODEL:]).astype(bf16)
        dyb = dm * gt[:, D_MODEL:2 * D_MODEL]
        dga = (dyb * sb).astype(bf16)
        dgb = (dyb * ga * sb * (1.0 - sb)).astype(bf16)
        dglu_ref[:, :D_MODEL] = dga
        dglu_ref[:, D_MODEL:] = dgb
        dys = _dot(dga, wglu_ref[:D_MODEL, :]) + _dot(dgb, wglu_ref[D_MODEL:, :])
        dyssm_ref[...] = dys * _gelu_grad(ys_ref[...])

    row_cols = [(GATE_COLS, bf16), (D_MODEL, bf16), (D_MODEL, bf16), (2 * D_MODEL, bf16), (SSM_W, f32)]
    acc_shapes = [(1, D_MODEL), (1, D_MODEL), (1, GATE_COLS)]
    return pl.pallas_call(
        body, name="mid_bwd", grid=(n,),
        in_specs=[_row_spec(tm, D_MODEL), _row_spec(tm, D_MODEL), _row_spec(tm, 1), _row_spec(tm, GATE_COLS),
                  _row_spec(tm, CONV_W), _row_spec(tm, XATTN_W), _row_spec(tm, SSM_W),
                  _const_spec((1, D_MODEL)), _const_spec((D_MODEL, D_MODEL)), _const_spec((2 * D_MODEL, SSM_W)),
                  _const_spec((D_MODEL, CONV_W)), _const_spec((D_MODEL, XATTN_W))],
        out_specs=([_row_spec(tm, D_MODEL), _col_spec(D_MODEL, tm)] + [_row_spec(tm, c) for c, _ in row_cols]
                   + [_acc_spec(s) for s in acc_shapes]),
        out_shape=([_sds((s_len, D_MODEL), f32), _sds((D_MODEL, s_len), bf16)]
                   + [_sds((s_len, c), dt) for c, dt in row_cols] + [_sds(s, f32) for s in acc_shapes]),
        compiler_params=_cparams(("arbitrary",)),
    )(dx1, xhat1, rstd1, g, ain, ob, y_ssm, ln1_g, w_out, w_glu_t, w_co_t, w_xo_t)


def _ssm_bwd(u, dy, cm_all, b_half, c_half, pw, d_skip):
    s_len = u.shape[0]
    tb = SSM_BLOCK
    n = s_len // tb

    def body(u_ref, dy_ref, cm_ref, b_ref, c_ref, pw_ref, d_ref,
             du_ref, db_hbm, dc_hbm, da_ref, dd_ref,
             s_ref, g_ref, gcarry_ref, gcm_ref, db_ref, dc_ref, up_ref, dyp_ref, dup_ref, stage_ref):
        i = pl.program_id(0)

        @pl.when(i == 0)
        def _():
            gcarry_ref[...] = jnp.zeros_like(gcarry_ref)
            db_ref[...] = jnp.zeros_like(db_ref)
            dc_ref[...] = jnp.zeros_like(dc_ref)
            da_ref[...] = jnp.zeros_like(da_ref)
            dd_ref[...] = jnp.zeros_like(dd_ref)

        _rows_to_segments(u_ref, stage_ref, up_ref)
        _rows_to_segments(dy_ref, stage_ref, dyp_ref)
        u = up_ref[...]
        ub = u.astype(bf16)
        dy = dyp_ref[...]
        dyb = dy.astype(bf16)
        dd_ref[...] += _colsum(dy * u)

        for half in range(N_HALF):
            s_ref[:, half * HALF_COLS:(half + 1) * HALF_COLS] = _dot(ub[:, half * HALF_W:(half + 1) * HALF_W], b_ref[half])
        _ssm_scan(s_ref, pw_ref, cm_ref, reverse=False, unroll=True)

        for half in range(N_HALF):
            g_ref[:, half * HALF_COLS:(half + 1) * HALF_COLS] = _dot(dyb[:, half * HALF_W:(half + 1) * HALF_W], c_ref[half], NT)
        _ssm_scan(g_ref, pw_ref, None, reverse=True, unroll=True)
        _ssm_carries(0, g_ref, pw_ref, gcarry_ref, gcm_ref, reverse=True)
        _ssm_add_carry(g_ref, pw_ref, gcm_ref, reverse=True)

        for half in range(N_HALF):
            cols = slice(half * HALF_W, (half + 1) * HALF_W)
            scols = slice(half * HALF_COLS, (half + 1) * HALF_COLS)
            gb = g_ref[:, scols].astype(bf16)
            dup_ref[:, cols] = _dot(gb, b_ref[half], NT) + d_ref[:, cols] * dy[:, cols]
            db_ref[half] += _dot(ub[:, cols], gb, TN)
            dc_ref[half] += _dot(s_ref[:, scols].astype(bf16), dyb[:, cols], TN)
        _rows_from_segments(dup_ref, stage_ref, du_ref)

        for chunk in range(N_STATE // LANE_CHUNK):
            re, im = _state_cols(chunk)
            acc_r = da_ref[:, re]
            acc_i = da_ref[:, im]
            for k in range(SSM_SEG):
                rows = slice(k * SUBLANES, (k + 1) * SUBLANES)
                if k == 0:
                    pr, pi = cm_ref[:, re], cm_ref[:, im]
                else:
                    prev = slice((k - 1) * SUBLANES, k * SUBLANES)
                    pr, pi = s_ref[prev, re], s_ref[prev, im]
                gr, gi = g_ref[rows, re], g_ref[rows, im]
                acc_r = acc_r + (gr * pr + gi * pi)
                acc_i = acc_i + (gi * pr - gr * pi)
            da_ref[:, re] = acc_r
            da_ref[:, im] = acc_i

        @pl.when(i == n - 1)
        def _():
            pltpu.sync_copy(db_ref, db_hbm)
            pltpu.sync_copy(dc_ref, dc_hbm)

    rev = functools.partial(_row_spec, rev_n=n)
    any_spec = pl.BlockSpec(memory_space=pl.ANY)
    state_rows = pltpu.VMEM((tb, 2 * N_STATE), f32)
    seg_rows = pltpu.VMEM((SUBLANES, 2 * N_STATE), f32)
    tok_rows = pltpu.VMEM((tb, SSM_W), f32)
    return pl.pallas_call(
        body, name="ssm_bwd", grid=(n,),
        in_specs=[rev(tb, SSM_W), rev(tb, SSM_W), rev(SUBLANES, 2 * N_STATE),
                  _const_spec((N_HALF, HALF_W, HALF_COLS)), _const_spec((N_HALF, HALF_COLS, HALF_W)),
                  _const_spec((SSM_SEG, 2 * N_STATE)), _const_spec((1, SSM_W))],
        out_specs=[rev(tb, SSM_W), any_spec, any_spec, _acc_spec((SUBLANES, 2 * N_STATE)), _acc_spec((1, SSM_W))],
        out_shape=[_sds((s_len, SSM_W), f32), _sds((N_HALF, HALF_W, HALF_COLS), f32),
                   _sds((N_HALF, HALF_COLS, HALF_W), f32), _sds((SUBLANES, 2 * N_STATE), f32), _sds((1, SSM_W), f32)],
        scratch_shapes=[state_rows, state_rows, seg_rows, seg_rows,
                        pltpu.VMEM((N_HALF, HALF_W, HALF_COLS), f32), pltpu.VMEM((N_HALF, HALF_COLS, HALF_W), f32),
                        tok_rows, tok_rows, tok_rows, pltpu.VMEM((SSM_W // LANES, tb, LANES), f32)],
        compiler_params=_cparams(("arbitrary",)),
    )(u, dy, cm_all, b_half, c_half, pw, d_skip)


def _branch_bwd(dya, dyc, cin, q, kv, k_t, conv_w, w_co_t, w_xo_t, side_blocks):
    s_len = dya.shape[0]
    tm = 2 * TOKEN_TILE
    n = s_len // tm
    halo_blocks = tm // 8
    ns = len(side_blocks)
    conv_tile = _sds((8, CONV_W), f32)
    side_in_specs, side_shapes, side_sems = _side_gather_specs(list(side_blocks) + [conv_tile])

    def body(*refs):
        (dya_ref, dyc_ref, cin_ref, cprev_ref, q_ref, kv_ref, cw_ref, wco_ref, wxo_ref, kt_ref) = refs[:10]
        side_ins = refs[10:10 + ns]
        dconv_ref, dq_ref, dkv_ref = refs[10 + ns:13 + ns]
        side_outs = refs[13 + ns:14 + 2 * ns]
        zs_ref, dczs_ref, dcw_ref = refs[14 + 2 * ns:17 + 2 * ns]
        copies = _side_gather_copies(list(side_ins) + [dcw_ref], side_outs, *refs[17 + 2 * ns:])
        side, conv_side = copies[:ns * N_DEV], copies[ns * N_DEV:]
        i = pl.program_id(0)
        tile = n - 1 - i

        @pl.when(i == 0)
        def _():
            dcw_ref[...] = jnp.zeros_like(dcw_ref)
            dkv_ref[...] = jnp.zeros_like(dkv_ref)
            dczs_ref[tm:tm + 8, :] = jnp.zeros((8, CONV_W), f32)
            for cp in side:
                cp.start()

        cin = cin_ref[...]
        cb, cc, ch = cin[:, :CONV_W], cin[:, CONV_W:2 * CONV_W], cin[:, 2 * CONV_W:]
        z = cc * ch
        cprev = cprev_ref[...]
        zprev = cprev[:, CONV_W:2 * CONV_W] * cprev[:, 2 * CONV_W:]
        zs_ref[0:8, :] = jnp.where(tile == 0, 0.0, zprev)
        zs_ref[8:8 + tm, :] = z
        z1 = zs_ref[pl.ds(7, tm), :]
        z2 = zs_ref[pl.ds(6, tm), :]
        cw = cw_ref[...]
        cz = cw[0:1] * z2 + cw[1:2] * z1 + cw[2:3] * z

        dain = _dot(dya_ref[...], wco_ref[...])
        dcb = dain * cz
        dcz = dain * cb
        dczs_ref[0:tm, :] = dcz
        dcz1 = dczs_ref[pl.ds(1, tm), :]
        dcz2 = dczs_ref[pl.ds(2, tm), :]
        dz = cw[2:3] * dcz + cw[1:2] * dcz1 + cw[0:1] * dcz2
        dczs_ref[tm:tm + 8, :] = dczs_ref[0:8, :]
        dcw_ref[0:1, :] += _colsum(dcz * z2)
        dcw_ref[1:2, :] += _colsum(dcz * z1)
        dcw_ref[2:3, :] += _colsum(dcz * z)
        dconv_ref[:, :CONV_W] = dcb.astype(bf16)
        dconv_ref[:, CONV_W:2 * CONV_W] = (dz * ch).astype(bf16)
        dconv_ref[:, 2 * CONV_W:] = (dz * cc).astype(bf16)

        qb = q_ref[...]
        dob = _dot(dyc_ref[...], wxo_ref[...]).astype(bf16)
        kv = kv_ref[...]
        heads = range(HEADS)
        hcs = [slice(h * HEAD_DIM, (h + 1) * HEAD_DIM) for h in heads]
        vcs = [slice(XATTN_W + h * HEAD_DIM, XATTN_W + (h + 1) * HEAD_DIM) for h in heads]
        s_t = [_dot(kv[:, hcs[h]], qb[:, hcs[h]], NT) * (HEAD_DIM ** -0.5) for h in heads]
        dp_t = [_dot(kv[:, vcs[h]], dob[:, hcs[h]], NT) for h in heads]
        e_t = [jnp.exp(s_t[h] - jnp.max(s_t[h], axis=0, keepdims=True)) for h in heads]
        p_t = [e_t[h] / jnp.sum(e_t[h], axis=0, keepdims=True) for h in heads]
        dv = [_dot(p_t[h].astype(bf16), dob[:, hcs[h]]) for h in heads]
        ds_t = [(p_t[h] * (dp_t[h] - jnp.sum(dp_t[h] * p_t[h], axis=0, keepdims=True)) * (HEAD_DIM ** -0.5)).astype(bf16)
                for h in heads]
        dk = [_dot(ds_t[h], qb[:, hcs[h]]) for h in heads]
        dq_t = [_dot(kt_ref[hcs[h], :], ds_t[h]) for h in heads]
        dq_ref[...] = jnp.concatenate(dq_t, axis=0).T.astype(bf16)
        dkv_ref[...] += jnp.concatenate(dk + dv, axis=1)

        @pl.when(i == n - 1)
        def _():
            for cp in conv_side:
                cp.start()
            for cp in side + conv_side:
                cp.wait()

    rev = functools.partial(_row_spec, rev_n=n)
    prev_spec = pl.BlockSpec((8, 3 * CONV_W), lambda i: (jnp.maximum((n - 1 - i) * halo_blocks - 1, 0), 0))
    outs = pl.pallas_call(
        body, name="branch_bwd", grid=(n,),
        in_specs=[rev(tm, D_MODEL), rev(tm, D_MODEL), rev(tm, 3 * CONV_W), prev_spec, rev(tm, XATTN_W),
                  _const_spec((MEM_LEN, 2 * XATTN_W)), _const_spec((3, CONV_W)), _const_spec((D_MODEL, CONV_W)),
                  _const_spec((D_MODEL, XATTN_W)), _const_spec((XATTN_W, MEM_LEN))] + side_in_specs[:ns],
        out_specs=[rev(tm, 3 * CONV_W), rev(tm, XATTN_W), _acc_spec((MEM_LEN, 2 * XATTN_W))] + side_in_specs,
        out_shape=[_sds((s_len, 3 * CONV_W), bf16), _sds((s_len, XATTN_W), bf16),
                   _sds((MEM_LEN, 2 * XATTN_W), f32)] + side_shapes,
        scratch_shapes=[pltpu.VMEM((tm + 8, CONV_W), f32), pltpu.VMEM((tm + 8, CONV_W), f32),
                        pltpu.VMEM((8, CONV_W), f32)] + side_sems,
        compiler_params=_cparams(("arbitrary",)),
    )(dya, dyc, cin, cin, q, kv, conv_w, w_co_t, w_xo_t, k_t, *side_blocks)
    return outs[0], outs[1], outs[2], outs[3:]


def _in_proj_bwd(dgp, dconv, du, dq, dxp, w_in_t):
    s_len = dgp.shape[0]
    tm = 2 * TOKEN_TILE
    n = s_len // tm

    def body(dgp_ref, dconv_ref, du_ref, dq_ref, dxp_ref, win_ref, dx_ref, dproj_ref):
        dproj = jnp.concatenate([dgp_ref[...], dconv_ref[...], du_ref[...].astype(bf16), dq_ref[...]], axis=1)
        dproj_ref[...] = dproj
        dx_ref[...] = dxp_ref[...] + _dot(dproj, win_ref[...])

    return pl.pallas_call(
        body, name="in_proj_bwd", grid=(n,),
        in_specs=[_row_spec(tm, GATE_COLS), _row_spec(tm, 3 * CONV_W), _row_spec(tm, SSM_W), _row_spec(tm, XATTN_W),
                  _row_spec(tm, D_MODEL), _const_spec((IN_COLS, D_MODEL))],
        out_specs=[_row_spec(tm, D_MODEL), _row_spec(tm, IN_COLS)],
        out_shape=[_sds((s_len, D_MODEL), f32), _sds((s_len, IN_COLS), bf16)],
        compiler_params=_cparams(("parallel",)),
    )(dgp, dconv, du, dq, dxp, w_in_t)


N_CHIP = 4
CHIP_STEPS = [(1, 1), (1, 0), (0, 1), (0, 0)]


def _flip(v, d):
    return 1 - v if d else v


def _chip_order():
    x, y, _ = _mesh_place()
    return jnp.stack([2 * _flip(x, dx) + _flip(y, dy) for dx, dy in CHIP_STEPS]).astype(jnp.int32)


def _weight_grads_scatter(problems, name):
    dims = []
    first = 0
    for a_t, b, tm, tt in problems:
        m, s_len = a_t.shape
        w = b.shape[1] // N_DEV
        tm, tt = min(tm, m), min(tt, s_len)
        assert m % tm == 0 and s_len % tt == 0
        nm, nt = m // tm, s_len // tt
        dims.append(dict(m=m, w=w, tm=tm, tt=tt, nm=nm, nt=nt, first=first, steps=N_CHIP * nm * nt))
        first += N_CHIP * nm * nt
    n_prob, total = len(problems), first
    n_scratch = 9

    def place(d, s):
        local = jnp.clip(s - d["first"], 0, d["steps"] - 1)
        return local // (d["nm"] * d["nt"]), (local // d["nt"]) % d["nm"], local % d["nt"]

    def run(d, q, im, t, a_ref, b_ref, recv_ref, acc_ref, send_ref, sib_ref, stash_ref,
            d2d_send, d2d_recv, ici_send, ici_recv, local_sem):
        tm, w, nm, nt = d["tm"], d["w"], d["nm"], d["nt"]
        x, y, c = _mesh_place()
        mesh_id = pl.DeviceIdType.MESH

        @pl.when(t == 0)
        def _():
            acc_ref[...] = jnp.zeros_like(acc_ref)

        acc_ref[...] += _dot(a_ref[...], b_ref[...])

        def to_sibling(qq, imm):
            rows = pl.ds(pl.multiple_of(imm * tm, tm), tm)
            return pltpu.make_async_remote_copy(
                src_ref=send_ref.at[qq, 0, rows, :], dst_ref=sib_ref.at[qq, rows, :],
                send_sem=d2d_send.at[qq], recv_sem=d2d_recv.at[qq, imm],
                device_id=(x, y, 1 - c), device_id_type=mesh_id)

        def finish_tile(qq, imm):
            rows = pl.ds(pl.multiple_of(imm * tm, tm), tm)
            to_sibling(qq, imm).wait_recv()
            both = stash_ref[...] + sib_ref[qq, rows, :].astype(f32)
            send_ref[qq, 1, rows, :] = both.astype(bf16)
            for step, (dx, dy) in enumerate(CHIP_STEPS):
                @pl.when(qq == step)
                def _(step=step, dx=dx, dy=dy):
                    src, dst = send_ref.at[step, 1, rows, :], recv_ref.at[step, rows, :]
                    if dx or dy:
                        pltpu.make_async_remote_copy(
                            src_ref=src, dst_ref=dst, send_sem=ici_send.at[step], recv_sem=ici_recv.at[step],
                            device_id=(_flip(x, dx), _flip(y, dy), c), device_id_type=mesh_id).start()
                    else:
                        pltpu.make_async_copy(src, dst, local_sem).start()

        @pl.when(t == nt - 1)
        def _():
            tile = q * nm + im

            @pl.when(tile > 0)
            def _():
                finish_tile((tile - 1) // nm, (tile - 1) % nm)

            rows = pl.ds(pl.multiple_of(im * tm, tm), tm)
            for core in (0, 1):
                @pl.when(c == core)
                def _(core=core):
                    other = 1 - core
                    send_ref[q, 0, rows, :] = acc_ref[:, other * w:(other + 1) * w].astype(bf16)
                    stash_ref[...] = acc_ref[:, core * w:(core + 1) * w]
            to_sibling(q, im).start()

            @pl.when(tile == N_CHIP * nm - 1)
            def _():
                finish_tile(q, im)
                for step, (dx, dy) in enumerate(CHIP_STEPS):
                    pltpu.make_async_remote_copy(
                        src_ref=send_ref.at[step, 0], dst_ref=sib_ref.at[step],
                        send_sem=d2d_send.at[step], recv_sem=d2d_recv.at[step, 0],
                        device_id=(x, y, 1 - c), device_id_type=mesh_id).wait_send()
                    src, dst = send_ref.at[step, 1], recv_ref.at[step]
                    if dx or dy:
                        pltpu.make_async_remote_copy(
                            src_ref=src, dst_ref=dst, send_sem=ici_send.at[step], recv_sem=ici_recv.at[step],
                            device_id=(_flip(x, dx), _flip(y, dy), c), device_id_type=mesh_id).wait()
                    else:
                        pltpu.make_async_copy(src, dst, local_sem).wait()

    def body(order_ref, *refs):
        del order_ref
        s = pl.program_id(0)
        operands, rest = refs[:2 * n_prob], refs[2 * n_prob:]
        results, scratch = rest[:n_prob], rest[n_prob:]
        for k, d in enumerate(dims):
            @pl.when((s >= d["first"]) & (s < d["first"] + d["steps"]))
            def _(k=k, d=d):
                q, im, t = place(d, s)
                run(d, q, im, t, operands[2 * k], operands[2 * k + 1], results[k],
                    *scratch[n_scratch * k:n_scratch * (k + 1)])

    in_specs, scratch_shapes = [], []
    for d in dims:
        def a_map(s, order, d=d):
            _, im, t = place(d, s)
            return im, t

        def b_map(s, order, d=d):
            q, _, t = place(d, s)
            return t, order[q]

        in_specs += [pl.BlockSpec((d["tm"], d["tt"]), a_map), pl.BlockSpec((d["tt"], 2 * d["w"]), b_map)]
        scratch_shapes += [pltpu.VMEM((d["tm"], 2 * d["w"]), f32), pltpu.VMEM((N_CHIP, 2, d["m"], d["w"]), bf16),
                           pltpu.VMEM((N_CHIP, d["m"], d["w"]), bf16), pltpu.VMEM((d["tm"], d["w"]), f32),
                           pltpu.SemaphoreType.DMA((N_CHIP,)), pltpu.SemaphoreType.DMA((N_CHIP, d["nm"])),
                           pltpu.SemaphoreType.DMA((N_CHIP - 1,)), pltpu.SemaphoreType.DMA((N_CHIP - 1,)),
                           pltpu.SemaphoreType.DMA]
    grid_spec = pltpu.PrefetchScalarGridSpec(
        num_scalar_prefetch=1, grid=(total,), in_specs=in_specs,
        out_specs=[pl.BlockSpec(memory_space=pl.ANY)] * n_prob, scratch_shapes=scratch_shapes)
    return pl.pallas_call(
        body, name=name, grid_spec=grid_spec,
        out_shape=[_sds((N_CHIP, d["m"], d["w"]), bf16) for d in dims],
        compiler_params=_cparams(("arbitrary",)),
    )(_chip_order(), *[op for a_t, b, _, _ in problems for op in (a_t, b)])


def _adamw(w, g, m, v):
    m = ADAM_B1 * m + (1.0 - ADAM_B1) * g
    v = ADAM_B2 * v + (1.0 - ADAM_B2) * jnp.square(g)
    m_hat = m / (1.0 - ADAM_B1 ** ADAM_STEP)
    v_hat = v / (1.0 - ADAM_B2 ** ADAM_STEP)
    delta = -ADAM_LR * (m_hat / (jnp.sqrt(v_hat) + ADAM_EPS) + ADAM_WD * w)
    return delta, m, v


def _sum_parts(p_ref):
    g = p_ref[0].astype(f32)
    for j in range(1, p_ref.shape[0]):
        g = g + p_ref[j].astype(f32)
    return g


def _adamw_update(w, m, v, parts, name, transposed):
    rows, cols = w.shape
    n_parts = parts.shape[0]
    if transposed:
        tc = 256
        steps = cols // tc
        p_spec = pl.BlockSpec((n_parts, tc, rows), lambda i: (0, i, 0))
        spec = pl.BlockSpec((rows, tc), lambda i: (0, i))
    else:
        tr = next(t for t in (256, 128, 64, 32, 16, 8) if rows % t == 0)
        steps = rows // tr
        p_spec = pl.BlockSpec((n_parts, tr, cols), lambda i: (0, i, 0))
        spec = pl.BlockSpec((tr, cols), lambda i: (i, 0))

    def body(w_ref, p_ref, m_ref, v_ref, g_ref, d_ref, nm_ref, nv_ref):
        g = _sum_parts(p_ref)
        if transposed:
            g = g.T
        g_ref[...] = g
        d_ref[...], nm_ref[...], nv_ref[...] = _adamw(w_ref[...], g, m_ref[...], v_ref[...])

    return pl.pallas_call(
        body, name=name, grid=(steps,),
        in_specs=[spec, p_spec, spec, spec], out_specs=[spec] * 4,
        out_shape=[_sds((rows, cols), f32)] * 4,
        compiler_params=_cparams(("parallel",)),
    )(w, parts, m, v)


def _adamw_whole(ws, ms, vs, parts, transposed):
    n = len(ws)

    def body(*refs):
        w_refs, m_refs, v_refs, p_refs = (refs[j * n:(j + 1) * n] for j in range(4))
        out_refs = refs[4 * n:]
        for a in range(n):
            g = _sum_parts(p_refs[a])
            if transposed[a]:
                g = g.T
            d, nm, nv = _adamw(w_refs[a][...], g, m_refs[a][...], v_refs[a][...])
            for j, val in enumerate((g, d, nm, nv)):
                out_refs[j * n + a][...] = val

    res = pl.pallas_call(
        body, name="adamw_small_weights",
        out_shape=[_sds(w.shape, f32) for _ in range(4) for w in ws],
        compiler_params=_cparams(),
    )(*ws, *ms, *vs, *parts)
    return [res[j * n:(j + 1) * n] for j in range(4)]


SMALL_GROUPS = [
    (["b_gate", "ln1_g", "ln1_b", "b_up", "b_down", "ln2_g", "ln2_b", "ssm_d"], 1),
    (["ssm_lam_re", "ssm_lam_im", "ssm_c_re", "ssm_c_im", "ssm_b_re", "ssm_b_im"], 0),
    (["conv_w"], 0),
    (["ssm_log_dt"], 0),
]


def _sum_small(group_parts):
    def body(*refs):
        n = len(refs) // 2
        for p_ref, o_ref in zip(refs[:n], refs[n:]):
            o_ref[...] = _sum_parts(p_ref)

    return pl.pallas_call(
        body, name="sum_small",
        out_shape=[_sds(p.shape[1:], f32) for p in group_parts],
        compiler_params=_cparams(),
    )(*group_parts)


def _adamw_small(ws, ms, vs, group_sums):
    names = [k for group, _ in SMALL_GROUPS for k in group]
    n = len(names)

    def body(*refs):
        w_refs, m_refs, v_refs = (dict(zip(names, refs[j * n:(j + 1) * n])) for j in range(3))
        p_refs = refs[3 * n:3 * n + len(SMALL_GROUPS)]
        out_refs = [dict(zip(names, refs[3 * n + len(SMALL_GROUPS) + j * n:][:n])) for j in range(4)]
        for (group, axis), p_ref in zip(SMALL_GROUPS, p_refs):
            total = p_ref[...]
            off = 0
            for k in group:
                size = SMALL[k][axis]
                g = total[:, off:off + size] if axis == 1 else total[off:off + size, :]
                off += size
                d, nm, nv = _adamw(w_refs[k][...], g, m_refs[k][...], v_refs[k][...])
                for j, val in enumerate((g, d, nm, nv)):
                    out_refs[j][k][...] = val

    res = pl.pallas_call(
        body, name="adamw_small",
        out_shape=[_sds(SMALL[k], f32) for _ in range(4) for k in names],
        compiler_params=_cparams(),
    )(*[ws[k] for k in names], *[ms[k] for k in names], *[vs[k] for k in names], *group_sums)
    return [dict(zip(names, res[j * n:(j + 1) * n])) for j in range(4)]


def _ssm_discretize(lam_re, lam_im, log_dt, b_re, b_im):
    dt = jnp.exp(log_dt)[:, None]
    mag = jnp.exp(lam_re * dt)
    abar_r = mag * jnp.cos(lam_im * dt)
    abar_i = mag * jnp.sin(lam_im * dt)
    den = lam_re * lam_re + lam_im * lam_im
    nr = abar_r - 1.0
    ni = abar_i
    kr = (nr * lam_re + ni * lam_im) / den
    ki = (ni * lam_re - nr * lam_im) / den
    bbar_r = kr[:, None, :] * b_re - ki[:, None, :] * b_im
    bbar_i = kr[:, None, :] * b_im + ki[:, None, :] * b_re
    return abar_r, abar_i, bbar_r, bbar_i


def _state_layout(re, im):
    parts = []
    for half in range(N_HALF):
        cols = slice(half * HALF_STATE, (half + 1) * HALF_STATE)
        parts += [re[..., cols], im[..., cols]]
    return jnp.concatenate(parts, axis=-1)


def _state_unlayout(a):
    re = jnp.concatenate([a[..., _half_cols(h)[0]] for h in range(N_HALF)], axis=-1)
    im = jnp.concatenate([a[..., _half_cols(h)[1]] for h in range(N_HALF)], axis=-1)
    return re, im


def _abar_powers(abar_r, abar_i):
    pr, pi = abar_r.reshape(1, N_STATE), abar_i.reshape(1, N_STATE)
    while pr.shape[0] < SSM_SEG:
        tr, ti = pr[-1:], pi[-1:]
        pr, pi = (jnp.concatenate([pr, pr * tr - pi * ti], axis=0), jnp.concatenate([pi, pr * ti + pi * tr], axis=0))
    return _state_layout(pr, pi)


HALF_GROUPS = SSM_GROUPS // N_HALF


def _half_block_diag(blocks):
    _, r, c = blocks.shape
    eye = jnp.eye(HALF_GROUPS, dtype=blocks.dtype)
    b4 = blocks.reshape(N_HALF, HALF_GROUPS, r, c)
    return jnp.einsum("ngrc,gk->ngrkc", b4, eye).reshape(N_HALF, HALF_GROUPS * r, HALF_GROUPS * c)


def _half_diag_blocks(mat, r, c):
    eye = jnp.eye(HALF_GROUPS, dtype=mat.dtype)
    m5 = mat.reshape(N_HALF, HALF_GROUPS, r, HALF_GROUPS, c)
    return jnp.einsum("ngrkc,gk->ngrc", m5, eye).reshape(SSM_GROUPS, r, c)


BIG = ["w_in", "w_conv_out", "w_glu", "w_kv", "w_xattn_out", "w_out", "w_up", "w_down"]
GATHER_TRANSPOSED = ["w_conv_out", "w_glu", "w_xattn_out", "w_up"]
PARTS_TRANSPOSED = ["w_in", "w_kv", "w_out", "w_down"]
SMALL = {"b_gate": (1, GATE_COLS), "conv_w": (3, CONV_W), "ssm_lam_re": (SSM_GROUPS, SSM_STATE),
         "ssm_lam_im": (SSM_GROUPS, SSM_STATE), "ssm_log_dt": (1, SSM_GROUPS),
         "ssm_b_re": (SSM_W, SSM_STATE), "ssm_b_im": (SSM_W, SSM_STATE),
         "ssm_c_re": (SSM_W, SSM_STATE), "ssm_c_im": (SSM_W, SSM_STATE), "ssm_d": (1, SSM_W),
         "ln1_g": (1, D_MODEL), "ln1_b": (1, D_MODEL), "b_up": (1, D_FF), "b_down": (1, D_MODEL),
         "ln2_g": (1, D_MODEL), "ln2_b": (1, D_MODEL)}
WEIGHTS = ["w_in", "b_gate", "conv_w", "w_conv_out", "ssm_lam_re", "ssm_lam_im", "ssm_log_dt", "ssm_b_re", "ssm_b_im",
           "ssm_c_re", "ssm_c_im", "ssm_d", "w_glu", "w_kv", "w_xattn_out", "w_out", "ln1_g", "ln1_b", "w_up", "b_up",
           "w_down", "b_down", "ln2_g", "ln2_b"]


def _local_step(x, mem, tgt, full, late, small):
    lam_re, lam_im, log_dt = small["ssm_lam_re"], small["ssm_lam_im"], small["ssm_log_dt"].reshape(SSM_GROUPS)
    c_shape = (SSM_GROUPS, SSM_GROUP, SSM_STATE)
    disc, disc_vjp = jax.vjp(_ssm_discretize, lam_re, lam_im, log_dt,
                             small["ssm_b_re"].reshape(c_shape), small["ssm_b_im"].reshape(c_shape))
    abar_r, abar_i, bbar_r, bbar_i = disc
    pw = _abar_powers(abar_r, abar_i)
    c_re, c_im = small["ssm_c_re"].reshape(c_shape), small["ssm_c_im"].reshape(c_shape)
    b_half = jnp.concatenate([_half_block_diag(bbar_r), _half_block_diag(bbar_i)], axis=2).astype(bf16)
    c_half = jnp.concatenate([_half_block_diag(c_re.transpose(0, 2, 1)), -_half_block_diag(c_im.transpose(0, 2, 1))],
                             axis=1).astype(bf16)

    s_len = x.shape[0]
    stack = lambda a: a.reshape(-1, a.shape[-1])
    kv, k_t, memb = _kv_proj(mem, full["w_kv"])
    (xbt, g, cin, u, q, ain, ob, aint, obt), side = _in_proj(
        x, full["w_in"], small["b_gate"], small["conv_w"], kv,
        [late[k] for k in ("w_glu", "w_conv_out", "w_xattn_out", "w_out", "w_up")])
    w_glu_t, w_co_t, w_xo_t, w_out, w_up_t = (stack(a) for a in side)
    y_ssm, cm_all, side = _ssm_fwd(u, b_half, c_half, pw, small["ssm_d"], [late["w_down"]])
    w_down = stack(side[0])
    ysbt, mb, xhat1, rstd1 = _mid_fwd(y_ssm, g, ain, ob, x, w_glu_t, w_co_t, w_xo_t, w_out,
                                      small["ln1_g"], small["ln1_b"])
    (x1bt, hdn, dr2bt, dpre, dx1, loss, dl2g, dl2b, dbdn, dbup) = _mlp_fwd_bwd(
        xhat1, tgt, small["ln1_g"], small["ln1_b"], w_up_t, small["b_up"], w_down,
        small["b_down"], small["ln2_g"], small["ln2_b"])
    (dxp, dr1bt, dgp, dya, dyc, dglu, dyssm, dl1g, dl1b, dbg) = _mid_bwd(
        dx1, xhat1, rstd1, g, ain, ob, y_ssm, small["ln1_g"], w_out, w_glu_t, w_co_t, w_xo_t)
    du, db_half, dc_half, da8, dd = _ssm_bwd(u, dyssm, cm_all, b_half, c_half, pw, small["ssm_d"])
    dabar_r, dabar_i = _state_unlayout(jnp.sum(da8, axis=0))
    dbbar_r = _half_diag_blocks(db_half[:, :, :HALF_STATE], SSM_GROUP, SSM_STATE)
    dbbar_i = _half_diag_blocks(db_half[:, :, HALF_STATE:], SSM_GROUP, SSM_STATE)
    g_shape = (SSM_GROUPS, SSM_STATE)
    dlam_re, dlam_im, dlog_dt, db_re, db_im = disc_vjp(
        (dabar_r.reshape(g_shape), dabar_i.reshape(g_shape), dbbar_r, dbbar_i))
    dc_re = _half_diag_blocks(dc_half[:, :HALF_STATE, :], SSM_STATE, SSM_GROUP).transpose(0, 2, 1)
    dc_im = -_half_diag_blocks(dc_half[:, HALF_STATE:, :], SSM_STATE, SSM_GROUP).transpose(0, 2, 1)

    small_grads = {
        "b_gate": dbg, "ssm_lam_re": dlam_re, "ssm_lam_im": dlam_im, "ssm_log_dt": dlog_dt,
        "ssm_b_re": db_re, "ssm_b_im": db_im, "ssm_c_re": dc_re, "ssm_c_im": dc_im, "ssm_d": dd,
        "ln1_g": dl1g, "ln1_b": dl1b, "b_up": dbup, "b_down": dbdn, "ln2_g": dl2g, "ln2_b": dl2b,
    }
    small_grads = {k: a.reshape(SMALL[k]) for k, a in small_grads.items()}
    groups = [(group, axis) for group, axis in SMALL_GROUPS if group != ["conv_w"]]
    stacks = [jnp.concatenate([small_grads[k] for k in group], axis=axis) if len(group) > 1 else small_grads[group[0]]
              for group, axis in groups]
    n_rowvec = stacks[0].shape[1]
    stacks[0] = jnp.concatenate([stacks[0], loss], axis=1)
    dense = lambda a: a.reshape(-1, LANES) if a.size % LANES == 0 else a
    dconv, dq, dkv, group_parts = _branch_bwd(dya, dyc, cin, q, kv, k_t, small["conv_w"], w_co_t, w_xo_t,
                                              [dense(a) for a in stacks])
    dx, dproj = _in_proj_bwd(dgp, dconv, du, dq, dxp, full["w_in"])
    tm, tt = 512, 2048
    calls = {
        "dw_down": {"w_down": (dr2bt, hdn, tm, 2 * tt)},
        "dw_up": {"w_up": (x1bt, dpre, tm, 2 * tt)},
        "dw_in": {"w_in": (xbt, dproj, tm, tt)},
        "dw_out_glu": {"w_out": (dr1bt, mb, tm, 2 * tt), "w_glu": (ysbt, dglu, tm, 2 * tt)},
        "dw_conv_xattn_kv": {"w_conv_out": (aint, dya, tm, 2 * tt), "w_xattn_out": (obt, dyc, tm, 2 * tt),
                             "w_kv": (dkv.T.astype(bf16), memb, D_MODEL, MEM_LEN)},
    }
    recv = {}
    for call_name, problems in calls.items():
        recv.update(zip(problems, _weight_grads_scatter(list(problems.values()), call_name)))
    sums = _sum_small(group_parts)
    group_sums = dict(zip([tuple(group) for group, _ in groups], [s.reshape(a.shape) for s, a in zip(sums, stacks)]))
    group_sums[("conv_w",)] = sums[-1][0:3]
    first = tuple(groups[0][0])
    loss_all = group_sums[first][0, n_rowvec]
    group_sums[first] = group_sums[first][:, :n_rowvec]
    return loss_all, dx, recv, [group_sums[tuple(group)] for group, _ in SMALL_GROUPS]


def kernel(x, mem, w_in, b_gate, conv_w, w_conv_out, ssm_lam_re, ssm_lam_im, ssm_log_dt, ssm_b_re, ssm_b_im, ssm_c_re, ssm_c_im, ssm_d, w_glu, w_kv, w_xattn_out, w_out, ln1_g, ln1_b, w_up, b_up, w_down, b_down, ln2_g, ln2_b, loss_target, m_w_in, m_b_gate, m_conv_w, m_w_conv_out, m_ssm_lam_re, m_ssm_lam_im, m_ssm_log_dt, m_ssm_b_re, m_ssm_b_im, m_ssm_c_re, m_ssm_c_im, m_ssm_d, m_w_glu, m_w_kv, m_w_xattn_out, m_w_out, m_ln1_g, m_ln1_b, m_w_up, m_b_up, m_w_down, m_b_down, m_ln2_g, m_ln2_b, v_w_in, v_b_gate, v_conv_w, v_w_conv_out, v_ssm_lam_re, v_ssm_lam_im, v_ssm_log_dt, v_ssm_b_re, v_ssm_b_im, v_ssm_c_re, v_ssm_c_im, v_ssm_d, v_w_glu, v_w_kv, v_w_xattn_out, v_w_out, v_ln1_g, v_ln1_b, v_w_up, v_b_up, v_w_down, v_b_down, v_ln2_g, v_ln2_b):
    w = dict(w_in=w_in, b_gate=b_gate, conv_w=conv_w, w_conv_out=w_conv_out, ssm_lam_re=ssm_lam_re,
             ssm_lam_im=ssm_lam_im, ssm_log_dt=ssm_log_dt, ssm_b_re=ssm_b_re, ssm_b_im=ssm_b_im, ssm_c_re=ssm_c_re,
             ssm_c_im=ssm_c_im, ssm_d=ssm_d, w_glu=w_glu, w_kv=w_kv, w_xattn_out=w_xattn_out, w_out=w_out,
             ln1_g=ln1_g, ln1_b=ln1_b, w_up=w_up, b_up=b_up, w_down=w_down, b_down=b_down, ln2_g=ln2_g, ln2_b=ln2_b)
    m = dict(w_in=m_w_in, b_gate=m_b_gate, conv_w=m_conv_w, w_conv_out=m_w_conv_out, ssm_lam_re=m_ssm_lam_re,
             ssm_lam_im=m_ssm_lam_im, ssm_log_dt=m_ssm_log_dt, ssm_b_re=m_ssm_b_re, ssm_b_im=m_ssm_b_im,
             ssm_c_re=m_ssm_c_re, ssm_c_im=m_ssm_c_im, ssm_d=m_ssm_d, w_glu=m_w_glu, w_kv=m_w_kv,
             w_xattn_out=m_w_xattn_out, w_out=m_w_out, ln1_g=m_ln1_g, ln1_b=m_ln1_b, w_up=m_w_up, b_up=m_b_up,
             w_down=m_w_down, b_down=m_b_down, ln2_g=m_ln2_g, ln2_b=m_ln2_b)
    v = dict(w_in=v_w_in, b_gate=v_b_gate, conv_w=v_conv_w, w_conv_out=v_w_conv_out, ssm_lam_re=v_ssm_lam_re,
             ssm_lam_im=v_ssm_lam_im, ssm_log_dt=v_ssm_log_dt, ssm_b_re=v_ssm_b_re, ssm_b_im=v_ssm_b_im,
             ssm_c_re=v_ssm_c_re, ssm_c_im=v_ssm_c_im, ssm_d=v_ssm_d, w_glu=v_w_glu, w_kv=v_w_kv,
             w_xattn_out=v_w_xattn_out, w_out=v_w_out, ln1_g=v_ln1_g, ln1_b=v_ln1_b, w_up=v_w_up, b_up=v_b_up,
             w_down=v_w_down, b_down=v_b_down, ln2_g=v_ln2_g, ln2_b=v_ln2_b)
    out_shapes = {k: a.shape for k, a in w.items()}
    swapped = ("w_in", "ssm_b_re", "ssm_b_im")

    def shard2d(k, a):
        if k in swapped:
            a = jnp.swapaxes(a, -1, -2)
        if k in SMALL:
            return a.reshape((3, CONV_W // N_DEV) if k == "conv_w" else SMALL[k])
        return a[0]

    def result(k, a):
        if k in swapped:
            shape = out_shapes[k]
            return jnp.swapaxes(a.reshape(shape[:-2] + (shape[-1], shape[-2])), -1, -2)
        return a.reshape(out_shapes[k])

    w, m, v = ({k: shard2d(k, a) for k, a in d.items()} for d in (w, m, v))

    shards = {k: w[k].T.astype(bf16) if k in GATHER_TRANSPOSED else w[k].astype(bf16) for k in BIG}
    conv_pad = jnp.pad(w["conv_w"], ((0, 5), (0, LANES - CONV_W // N_DEV)))
    early = ["w_in", "w_kv"]
    gathered = _all_gather([shards[k] for k in early] + [conv_pad], "gather_weights")
    full = {k: a.reshape(-1, a.shape[-1]) for k, a in zip(early, gathered[:-1])}
    late = {k: shards[k] for k in BIG if k not in early}
    conv_full = gathered[-1][:, :3, :CONV_W // N_DEV].transpose(1, 0, 2).reshape(3, CONV_W)
    small = {k: (conv_full if k == "conv_w" else w[k]) for k in SMALL}

    loss, dx, recv, group_sums = _local_step(x[0], mem[0], loss_target[0], full, late, small)

    grads, deltas, new_m, new_v = {}, {}, {}, {}
    tiled = ["w_in", "w_up", "w_down"]
    for k in tiled:
        res = _adamw_update(w[k], m[k], v[k], recv[k], "adamw_" + k, transposed=k in PARTS_TRANSPOSED)
        grads[k], deltas[k], new_m[k], new_v[k] = res
    whole = [k for k in BIG if k not in tiled]
    res = _adamw_whole([w[k] for k in whole], [m[k] for k in whole], [v[k] for k in whole], [recv[k] for k in whole],
                       [k in PARTS_TRANSPOSED for k in whole])
    for d, vals in zip((grads, deltas, new_m, new_v), res):
        d.update(zip(whole, vals))

    widen = lambda k, a: jnp.tile(a, (1, N_DEV)) if k == "conv_w" else a
    res = _adamw_small(small, {k: widen(k, m[k]) for k in SMALL}, {k: widen(k, v[k]) for k in SMALL}, group_sums)
    dev = _slot(_mesh_place())
    for d, small_res in zip((grads, deltas, new_m, new_v), res):
        for k, a in small_res.items():
            if k == "conv_w":
                a = lax.dynamic_slice_in_dim(a, dev * (CONV_W // N_DEV), CONV_W // N_DEV, axis=1)
            d[k] = a

    outs = [loss, dx[None]]
    for d in (grads, deltas, new_m, new_v):
        outs += [result(k, d[k]) for k in WEIGHTS]
    return tuple(outs)
```

```python
import functools
import math

import jax
import jax.numpy as jnp
from jax import lax
from jax.experimental import pallas as pl
from jax.experimental.pallas import tpu as pltpu

f32 = jnp.float32
bf16 = jnp.bfloat16

D_MODEL = 1024
MEM_LEN = 256
GATE_COLS = 3 * D_MODEL
CONV_W = 512
SSM_W = 512
XATTN_W = 512
HEADS = 4
HEAD_DIM = 128
D_FF = 4096
IN_COLS = GATE_COLS + 3 * CONV_W + SSM_W + XATTN_W
SSM_GROUPS = 32
SSM_GROUP = 16
SSM_STATE = 64
N_STATE = SSM_GROUPS * SSM_STATE
ALPHA = 2.0 ** 0.25
LN_EPS = 1e-5
N_DEV = 8

ADAM_LR = 0.001
ADAM_B1 = 0.9
ADAM_B2 = 0.999
ADAM_EPS = 1e-08
ADAM_WD = 0.01
ADAM_STEP = 10

VMEM_LIMIT_V7X = 56 * 2 ** 20
SUBLANES = 8
LANES = 128

TOKEN_TILE = 256
SSM_BLOCK = 512
SSM_SEG = SSM_BLOCK // SUBLANES
LANE_CHUNK = 256
N_HALF = 2
HALF_W = SSM_W // N_HALF
HALF_STATE = N_STATE // N_HALF
HALF_COLS = 2 * HALF_STATE

NT = (((1,), (1,)), ((), ()))
TN = (((0,), (0,)), ((), ()))
NN = (((1,), (0,)), ((), ()))


def _dot(a, b, dims=NN):
    return lax.dot_general(a, b, dims, preferred_element_type=f32)


def _cparams(sem=None):
    return pltpu.CompilerParams(dimension_semantics=sem, vmem_limit_bytes=VMEM_LIMIT_V7X)


def _row_spec(tm, cols, rev_n=None):
    if rev_n is None:
        return pl.BlockSpec((tm, cols), lambda i: (i, 0))
    return pl.BlockSpec((tm, cols), lambda i: (rev_n - 1 - i, 0))


def _col_spec(rows, tm):
    return pl.BlockSpec((rows, tm), lambda i: (0, i))


def _const_spec(shape):
    nd = len(shape)
    return pl.BlockSpec(shape, lambda *_: (0,) * nd, pipeline_mode=pl.Buffered(1))


def _acc_spec(shape):
    nd = len(shape)
    return pl.BlockSpec(shape, lambda *_: (0,) * nd)


def _sds(shape, dtype):
    return jax.ShapeDtypeStruct(shape, dtype)


def _gelu(x):
    c = math.sqrt(2.0 / math.pi)
    return 0.5 * x * (1.0 + jnp.tanh(c * (x + 0.044715 * x * x * x)))


def _gelu_grad(x):
    c = math.sqrt(2.0 / math.pi)
    t = jnp.tanh(c * (x + 0.044715 * x * x * x))
    return 0.5 * (1.0 + t) + 0.5 * x * (1.0 - t * t) * c * (1.0 + 3.0 * 0.044715 * x * x)


def _colsum(a):
    return jnp.sum(a, axis=0, keepdims=True)


def _mesh_place():
    return lax.axis_index("x"), lax.axis_index("y"), lax.axis_index("c")


def _slot(p):
    return 4 * p[0] + 2 * p[1] + p[2]


def _other_devices(me):
    x, y, c = me
    flip = lambda v, d: 1 - v if d else v
    return [(flip(x, dx), flip(y, dy), flip(c, dc)) for dx in (0, 1) for dy in (0, 1) for dc in (0, 1)][1:]


def _all_gather(blocks, name):
    n = len(blocks)

    def body(*refs):
        ins, outs = refs[:n], refs[n:2 * n]
        send_sems, recv_sems, local_sems = refs[2 * n:]
        x, y, c = _mesh_place()
        me, sibling = (x, y, c), (x, y, 1 - c)
        chips = [(1 - x, y), (x, 1 - y), (1 - x, 1 - y)]

        def copy(a, k, block, to, src=None):
            rows = outs[a].at[_slot(block)]
            return pltpu.make_async_remote_copy(
                src_ref=rows if src is None else src, dst_ref=rows,
                send_sem=send_sems.at[a, k], recv_sem=recv_sems.at[a, k],
                device_id=to, device_id_type=pl.DeviceIdType.MESH)

        mine = [pltpu.make_async_copy(ins[a], outs[a].at[_slot(me)], local_sems.at[a]) for a in range(n)]
        for cp in mine:
            cp.start()
        first = []
        for a in range(n):
            first.append(copy(a, 0, me, sibling, src=ins[a]))
            first += [copy(a, 1 + j, me, (*chip, c), src=ins[a]) for j, chip in enumerate(chips)]
        for cp in first:
            cp.start()
        passed = []
        for a in range(n):
            for j, chip in enumerate(chips):
                copy(a, 1 + j, (*chip, c), me).wait_recv()
                fwd = copy(a, 4 + j, (*chip, c), sibling)
                fwd.start()
                passed.append(fwd)
        for a in range(n):
            copy(a, 0, sibling, me).wait_recv()
            for j, chip in enumerate(chips):
                copy(a, 4 + j, (*chip, 1 - c), me).wait_recv()
        for cp in first + passed:
            cp.wait_send()
        for cp in mine:
            cp.wait()

    any_spec = pl.BlockSpec(memory_space=pl.ANY)
    return pl.pallas_call(
        body, name=name,
        out_shape=[_sds((N_DEV,) + b.shape, b.dtype) for b in blocks],
        in_specs=[any_spec] * n, out_specs=[any_spec] * n,
        scratch_shapes=[pltpu.SemaphoreType.DMA((n, 7)), pltpu.SemaphoreType.DMA((n, 7)),
                        pltpu.SemaphoreType.DMA((n,))],
    )(*blocks)


def _side_gather_copies(ins, outs, send_sems, recv_sems, local_sems):
    me = _mesh_place()
    copies = []
    for a, (src, dst) in enumerate(zip(ins, outs)):
        copies.append(pltpu.make_async_copy(src, dst.at[_slot(me)], local_sems.at[a]))
        for k, peer in enumerate(_other_devices(me)):
            copies.append(pltpu.make_async_remote_copy(
                src_ref=src, dst_ref=dst.at[_slot(me)], send_sem=send_sems.at[a, k], recv_sem=recv_sems.at[a, k],
                device_id=peer, device_id_type=pl.DeviceIdType.MESH))
    return copies


def _side_gather_two_level(ins, outs, send_sems, recv_sems, local_sems):
    x, y, c = _mesh_place()
    me, sibling = (x, y, c), (x, y, 1 - c)
    chips = [(1 - x, y), (x, 1 - y), (1 - x, 1 - y)]

    def copy(a, k, block, to, src=None):
        rows = outs[a].at[_slot(block)]
        return pltpu.make_async_remote_copy(
            src_ref=rows if src is None else src, dst_ref=rows, send_sem=send_sems.at[a, k], recv_sem=recv_sems.at[a, k],
            device_id=to, device_id_type=pl.DeviceIdType.MESH)

    n = len(ins)
    mine = [pltpu.make_async_copy(ins[a], outs[a].at[_slot(me)], local_sems.at[a]) for a in range(n)]
    first = [copy(a, 0, me, sibling, src=ins[a]) for a in range(n)]
    first += [copy(a, 1 + j, me, (*chip, c), src=ins[a]) for a in range(n) for j, chip in enumerate(chips)]
    passed = [copy(a, 4 + j, (*chip, c), sibling) for a in range(n) for j, chip in enumerate(chips)]

    def start():
        for cp in mine + first:
            cp.start()

    def forward():
        for a in range(n):
            for j, chip in enumerate(chips):
                copy(a, 1 + j, (*chip, c), me).wait_recv()
        for cp in passed:
            cp.start()

    def finish():
        for a in range(n):
            copy(a, 0, sibling, me).wait_recv()
            for j, chip in enumerate(chips):
                copy(a, 4 + j, (*chip, 1 - c), me).wait_recv()
        for cp in first + passed:
            cp.wait_send()
        for cp in mine:
            cp.wait()

    return start, forward, finish


def _side_gather_specs(blocks):
    n = len(blocks)
    any_spec = pl.BlockSpec(memory_space=pl.ANY)
    return ([any_spec] * n, [_sds((N_DEV,) + b.shape, b.dtype) for b in blocks],
            [pltpu.SemaphoreType.DMA((n, N_DEV - 1)), pltpu.SemaphoreType.DMA((n, N_DEV - 1)),
             pltpu.SemaphoreType.DMA((n,))])


def _kv_proj(mem, w_kv):
    def body(mem_ref, w_ref, kv_ref, kt_ref, memb_ref):
        mb = mem_ref[...].astype(bf16)
        memb_ref[...] = mb
        kv = _dot(mb, w_ref[...]).astype(bf16)
        kv_ref[...] = kv
        kt_ref[...] = kv[:, :XATTN_W].T

    return pl.pallas_call(
        body, name="kv_proj",
        out_shape=[_sds((MEM_LEN, 2 * XATTN_W), bf16), _sds((XATTN_W, MEM_LEN), bf16), _sds((MEM_LEN, D_MODEL), bf16)],
        compiler_params=_cparams(),
    )(mem, w_kv)


def _attention_probs(qb, kv_ref, h):
    kh = kv_ref[:, h * HEAD_DIM:(h + 1) * HEAD_DIM]
    s = _dot(qb[:, h * HEAD_DIM:(h + 1) * HEAD_DIM], kh, NT) * (HEAD_DIM ** -0.5)
    e = jnp.exp(s - jnp.max(s, axis=-1, keepdims=True))
    return e / jnp.sum(e, axis=-1, keepdims=True)


def _in_proj(x, w_in_t, b_gate, conv_w, kv, side_blocks):
    s_len = x.shape[0]
    tm = 2 * TOKEN_TILE
    n = s_len // tm
    ns = len(side_blocks)
    side_in_specs, side_shapes, side_sems = _side_gather_specs(side_blocks)

    def body(*refs):
        (x_ref, win_ref, bg_ref, cw_ref, kv_ref) = refs[:5]
        side_ins = refs[5:5 + ns]
        (xbt_ref, g_ref, cin_ref, u_ref, q_ref, ain_ref, o_ref, aint_ref, ot_ref) = refs[5 + ns:14 + ns]
        side_outs = refs[14 + ns:14 + 2 * ns]
        zs_ref = refs[14 + 2 * ns]
        side_start, side_forward, side_finish = _side_gather_two_level(side_ins, side_outs, *refs[15 + 2 * ns:])
        i = pl.program_id(0)
        pl.when(i == 0)(side_start)
        pl.when(i == (3 * n) // 4)(side_forward)

        xb = x_ref[...].astype(bf16)
        xbt_ref[...] = xb.T
        proj = _dot(xb, win_ref[...], NT)
        g_ref[...] = jax.nn.sigmoid(proj[:, :GATE_COLS] + bg_ref[...]).astype(bf16)
        cin = proj[:, GATE_COLS:GATE_COLS + 3 * CONV_W]
        cin_ref[...] = cin
        u_ref[...] = proj[:, GATE_COLS + 3 * CONV_W:GATE_COLS + 3 * CONV_W + SSM_W]
        qb = proj[:, IN_COLS - XATTN_W:].astype(bf16)
        q_ref[...] = qb

        cb, cc, ch = cin[:, :CONV_W], cin[:, CONV_W:2 * CONV_W], cin[:, 2 * CONV_W:]
        z = cc * ch

        @pl.when(i == 0)
        def _():
            zs_ref[0:8, :] = jnp.zeros((8, CONV_W), f32)

        zs_ref[8:8 + tm, :] = z
        z1 = zs_ref[pl.ds(7, tm), :]
        z2 = zs_ref[pl.ds(6, tm), :]
        cw = cw_ref[...]
        cz = cw[0:1] * z2 + cw[1:2] * z1 + cw[2:3] * z
        zs_ref[0:8, :] = zs_ref[tm:tm + 8, :]
        ain = (cb * cz).astype(bf16)
        ain_ref[...] = ain
        aint_ref[...] = ain.T

        probs = [_attention_probs(qb, kv_ref, h) for h in range(HEADS)]
        outs = [_dot(probs[h].astype(bf16), kv_ref[:, XATTN_W + h * HEAD_DIM:XATTN_W + (h + 1) * HEAD_DIM])
                for h in range(HEADS)]
        ob = jnp.concatenate(outs, axis=1).astype(bf16)
        o_ref[...] = ob
        ot_ref[...] = ob.T

        pl.when(i == n - 1)(side_finish)

    row_cols = [(GATE_COLS, bf16), (3 * CONV_W, f32), (SSM_W, f32), (XATTN_W, bf16), (CONV_W, bf16), (XATTN_W, bf16)]
    t_rows = [D_MODEL, CONV_W, XATTN_W]
    outs = pl.pallas_call(
        body, name="in_proj", grid=(n,),
        in_specs=[_row_spec(tm, D_MODEL), _const_spec((IN_COLS, D_MODEL)), _const_spec((1, GATE_COLS)),
                  _const_spec((3, CONV_W)), _const_spec((MEM_LEN, 2 * XATTN_W))] + side_in_specs,
        out_specs=([_col_spec(t_rows[0], tm)] + [_row_spec(tm, c) for c, _ in row_cols]
                   + [_col_spec(t_rows[1], tm), _col_spec(t_rows[2], tm)] + side_in_specs),
        out_shape=([_sds((t_rows[0], s_len), bf16)] + [_sds((s_len, c), dt) for c, dt in row_cols]
                   + [_sds((t_rows[1], s_len), bf16), _sds((t_rows[2], s_len), bf16)] + side_shapes),
        scratch_shapes=[pltpu.VMEM((tm + 8, CONV_W), f32)] + side_sems,
        compiler_params=_cparams(("arbitrary",)),
    )(x, w_in_t, b_gate, conv_w, kv, *side_blocks)
    return outs[:9], outs[9:]


def _state_cols(chunk, width=LANE_CHUNK):
    half, off = divmod(chunk * width, HALF_STATE)
    lo = half * HALF_COLS + off
    return slice(lo, lo + width), slice(lo + HALF_STATE, lo + HALF_STATE + width)


def _half_cols(half):
    lo = half * HALF_COLS
    return slice(lo, lo + HALF_STATE), slice(lo + HALF_STATE, lo + HALF_COLS)


def _rows_to_segments(src_ref, stage_ref, dst_ref):
    nc = SSM_W // LANES
    for c in range(nc):
        stage_ref[c] = src_ref[:, c * LANES:(c + 1) * LANES]
    for c in range(nc):
        for k in range(SSM_SEG):
            dst_ref[k * SUBLANES:(k + 1) * SUBLANES, c * LANES:(c + 1) * LANES] = (
                stage_ref[c, pl.ds(k, SUBLANES, stride=SSM_SEG), :])


def _rows_from_segments(src_ref, stage_ref, dst_ref):
    nc = SSM_W // LANES
    for c in range(nc):
        for k in range(SSM_SEG):
            stage_ref[c, pl.ds(k, SUBLANES, stride=SSM_SEG), :] = (
                src_ref[k * SUBLANES:(k + 1) * SUBLANES, c * LANES:(c + 1) * LANES])
    for c in range(nc):
        dst_ref[:, c * LANES:(c + 1) * LANES] = stage_ref[c]


def _ssm_scan(s_ref, pw_ref, init_ref, reverse, unroll, width=LANE_CHUNK):
    for chunk in range(N_STATE // width):
        re, im = _state_cols(chunk, width)
        ar = jnp.broadcast_to(pw_ref[0:1, re], (SUBLANES, width))
        ai = jnp.broadcast_to(pw_ref[0:1, im], (SUBLANES, width))
        if reverse:
            ai = -ai

        def step(j, carry, re=re, im=im, ar=ar, ai=ai):
            sr, si = carry
            k = (SSM_SEG - 1 - j) if reverse else j
            r0 = pl.multiple_of(k * SUBLANES, SUBLANES)
            nr = ar * sr - ai * si + s_ref[pl.ds(r0, SUBLANES), re]
            ni = ar * si + ai * sr + s_ref[pl.ds(r0, SUBLANES), im]
            s_ref[pl.ds(r0, SUBLANES), re] = nr
            s_ref[pl.ds(r0, SUBLANES), im] = ni
            return nr, ni

        if init_ref is None:
            init = (jnp.zeros((SUBLANES, width), f32),) * 2
        else:
            init = (init_ref[:, re], init_ref[:, im])
        lax.fori_loop(0, SSM_SEG, step, init, unroll=unroll)


def _ssm_add_carry(s_ref, pw_ref, cm_ref, reverse):
    for chunk in range(N_STATE // LANE_CHUNK):
        re, im = _state_cols(chunk)
        cr, ci = cm_ref[:, re], cm_ref[:, im]
        for k in range(SSM_SEG):
            pk = (SSM_SEG - 1 - k) if reverse else k
            pr = pw_ref[pk:pk + 1, re]
            pi = pw_ref[pk:pk + 1, im]
            if reverse:
                pi = -pi
            rows = slice(k * SUBLANES, (k + 1) * SUBLANES)
            s_ref[rows, re] = s_ref[rows, re] + (pr * cr - pi * ci)
            s_ref[rows, im] = s_ref[rows, im] + (pr * ci + pi * cr)


def _ssm_carries(first_row, s_ref, pw_ref, carry_ref, cm_ref, reverse):
    order = range(SUBLANES - 1, -1, -1) if reverse else range(SUBLANES)
    for half in range(N_HALF):
        re, im = _half_cols(half)
        a_r, a_i = pw_ref[SSM_SEG - 1:SSM_SEG, re], pw_ref[SSM_SEG - 1:SSM_SEG, im]
        if reverse:
            a_i = -a_i
        cr, ci = carry_ref[0:1, re], carry_ref[0:1, im]
        for seg in order:
            cm_ref[seg:seg + 1, re] = cr
            cm_ref[seg:seg + 1, im] = ci
            er = s_ref[first_row + seg:first_row + seg + 1, re]
            ei = s_ref[first_row + seg:first_row + seg + 1, im]
            cr, ci = a_r * cr - a_i * ci + er, a_r * ci + a_i * cr + ei
        carry_ref[0:1, re] = cr
        carry_ref[0:1, im] = ci


def _ssm_fwd(u, b_half, c_half, pw, d_skip, side_blocks):
    s_len = u.shape[0]
    tb = SSM_BLOCK
    n = s_len // tb
    ns = len(side_blocks)
    side_in_specs, side_shapes, side_sems = _side_gather_specs(side_blocks)

    def body(*refs):
        u_ref, b_ref, c_ref, pw_ref, d_ref = refs[:5]
        side_ins = refs[5:5 + ns]
        y_ref, cm_ref = refs[5 + ns:7 + ns]
        side_outs = refs[7 + ns:7 + 2 * ns]
        s_ref, carry_ref, up_ref, yp_ref, stage_ref = refs[7 + 2 * ns:12 + 2 * ns]
        side = _side_gather_copies(side_ins, side_outs, *refs[12 + 2 * ns:])
        i = pl.program_id(0)

        @pl.when(i == 0)
        def _():
            carry_ref[...] = jnp.zeros_like(carry_ref)
            for cp in side:
                cp.start()

        _rows_to_segments(u_ref, stage_ref, up_ref)
        u = up_ref[...]
        ub = u.astype(bf16)
        for half in range(N_HALF):
            s_ref[:, half * HALF_COLS:(half + 1) * HALF_COLS] = _dot(ub[:, half * HALF_W:(half + 1) * HALF_W], b_ref[half])
        _ssm_scan(s_ref, pw_ref, None, reverse=False, unroll=4, width=2 * LANE_CHUNK)
        _ssm_carries(tb - SUBLANES, s_ref, pw_ref, carry_ref, cm_ref, reverse=False)
        _ssm_add_carry(s_ref, pw_ref, cm_ref, reverse=False)
        for half in range(N_HALF):
            cols = slice(half * HALF_W, (half + 1) * HALF_W)
            sb = s_ref[:, half * HALF_COLS:(half + 1) * HALF_COLS].astype(bf16)
            yp_ref[:, cols] = _dot(sb, c_ref[half]) + d_ref[:, cols] * u[:, cols]
        _rows_from_segments(yp_ref, stage_ref, y_ref)

        @pl.when(i == n - 1)
        def _():
            for cp in side:
                cp.wait()

    outs = pl.pallas_call(
        body, name="ssm_fwd", grid=(n,),
        in_specs=[_row_spec(tb, SSM_W), _const_spec((N_HALF, HALF_W, HALF_COLS)), _const_spec((N_HALF, HALF_COLS, HALF_W)),
                  _const_spec((SSM_SEG, 2 * N_STATE)), _const_spec((1, SSM_W))] + side_in_specs,
        out_specs=[_row_spec(tb, SSM_W), _row_spec(SUBLANES, 2 * N_STATE)] + side_in_specs,
        out_shape=[_sds((s_len, SSM_W), f32), _sds((n * SUBLANES, 2 * N_STATE), f32)] + side_shapes,
        scratch_shapes=[pltpu.VMEM((tb, 2 * N_STATE), f32), pltpu.VMEM((SUBLANES, 2 * N_STATE), f32),
                        pltpu.VMEM((tb, SSM_W), f32), pltpu.VMEM((tb, SSM_W), f32),
                        pltpu.VMEM((SSM_W // LANES, tb, LANES), f32)] + side_sems,
        compiler_params=_cparams(("arbitrary",)),
    )(u, b_half, c_half, pw, d_skip, *side_blocks)
    return outs[0], outs[1], outs[2:]


def _layer_norm_fwd(r, g, b):
    mu = jnp.mean(r, axis=-1, keepdims=True)
    var = jnp.mean(jnp.square(r - mu), axis=-1, keepdims=True)
    rstd = lax.rsqrt(var + LN_EPS)
    xhat = (r - mu) * rstd
    return xhat, rstd, xhat * g + b


def _layer_norm_bwd(dy, xhat, rstd, g):
    dxh = dy * g
    m1 = jnp.mean(dxh, axis=-1, keepdims=True)
    m2 = jnp.mean(dxh * xhat, axis=-1, keepdims=True)
    return rstd * (dxh - m1 - xhat * m2)


def _branch_outputs(ys_ref, ain_ref, o_ref, wglu_ref, wco_ref, wxo_ref):
    ysb = _gelu(ys_ref[...]).astype(bf16)
    glu = _dot(ysb, wglu_ref[...], NT)
    ga, sb = glu[:, :D_MODEL], jax.nn.sigmoid(glu[:, D_MODEL:])
    ya = _dot(ain_ref[...], wco_ref[...], NT)
    yc = _dot(o_ref[...], wxo_ref[...], NT)
    return ysb, ga, sb, ya, ga * sb, yc


def _mid_fwd(y_ssm, g, ain, ob, x, w_glu_t, w_co_t, w_xo_t, w_out, ln1_g, ln1_b):
    s_len = x.shape[0]
    tm = 2 * TOKEN_TILE
    n = s_len // tm

    def body(ys_ref, g_ref, ain_ref, o_ref, x_ref, wglu_ref, wco_ref, wxo_ref, wout_ref, lg_ref, lb_ref,
             ysbt_ref, mb_ref, xhat_ref, rstd_ref):
        ysb, _, _, ya, yb, yc = _branch_outputs(ys_ref, ain_ref, o_ref, wglu_ref, wco_ref, wxo_ref)
        ysbt_ref[...] = ysb.T
        gt = g_ref[...].astype(f32)
        merged = gt[:, :D_MODEL] * ya + gt[:, D_MODEL:2 * D_MODEL] * yb + gt[:, 2 * D_MODEL:] * yc
        mb = merged.astype(bf16)
        mb_ref[...] = mb
        r1 = ALPHA * x_ref[...] + _dot(mb, wout_ref[...])
        xhat, rstd, _ = _layer_norm_fwd(r1, lg_ref[...], lb_ref[...])
        xhat_ref[...] = xhat
        rstd_ref[...] = rstd

    row_cols = [(D_MODEL, bf16), (D_MODEL, f32), (1, f32)]
    return pl.pallas_call(
        body, name="mid_fwd", grid=(n,),
        in_specs=[_row_spec(tm, SSM_W), _row_spec(tm, GATE_COLS), _row_spec(tm, CONV_W), _row_spec(tm, XATTN_W),
                  _row_spec(tm, D_MODEL), _const_spec((2 * D_MODEL, SSM_W)), _const_spec((D_MODEL, CONV_W)),
                  _const_spec((D_MODEL, XATTN_W)), _const_spec((D_MODEL, D_MODEL)),
                  _const_spec((1, D_MODEL)), _const_spec((1, D_MODEL))],
        out_specs=[_col_spec(SSM_W, tm)] + [_row_spec(tm, c) for c, _ in row_cols],
        out_shape=[_sds((SSM_W, s_len), bf16)] + [_sds((s_len, c), dt) for c, dt in row_cols],
        compiler_params=_cparams(("parallel",)),
    )(y_ssm, g, ain, ob, x, w_glu_t, w_co_t, w_xo_t, w_out, ln1_g, ln1_b)


def _mlp_fwd_bwd(xhat1, tgt, ln1_g, ln1_b, w_up_t, b_up, w_down, b_down, ln2_g, ln2_b):
    s_len = xhat1.shape[0]
    tm = TOKEN_TILE
    n = s_len // tm
    fc = 1024
    nfc = D_FF // fc

    def body(xh_ref, t_ref, l1g_ref, l1b_ref, wup_ref, bup_ref, wdn_ref, bdn_ref, l2g_ref, l2b_ref,
             x1bt_ref, hdn_ref, dr2bt_ref, dpre_ref, dx1_ref,
             loss_ref, dl2g_ref, dl2b_ref, dbdn_ref, dbup_ref, rl_ref):
        i = pl.program_id(0)

        @pl.when(i == 0)
        def _():
            loss_ref[...] = jnp.zeros_like(loss_ref)
            dl2g_ref[...] = jnp.zeros_like(dl2g_ref)
            dl2b_ref[...] = jnp.zeros_like(dl2b_ref)
            dbdn_ref[...] = jnp.zeros_like(dbdn_ref)
            dbup_ref[...] = jnp.zeros_like(dbup_ref)

        x1 = xh_ref[...] * l1g_ref[...] + l1b_ref[...]
        x1b = x1.astype(bf16)
        x1bt_ref[...] = x1b.T
        chunks = [slice(c * fc, (c + 1) * fc) for c in range(nfc)]
        pres = [_dot(x1b, wup_ref[cols, :], NT) for cols in chunks]
        hbs = []
        for cols, pre in zip(chunks, pres):
            rl = jnp.maximum(pre + bup_ref[:, cols], 0.0)
            rl_ref[:, cols] = rl
            hb = (rl * rl).astype(bf16)
            hdn_ref[:, cols] = hb
            hbs.append(hb)
        acc = _dot(hbs[0], wdn_ref[chunks[0], :])
        for cols, hb in zip(chunks[1:], hbs[1:]):
            acc = acc + _dot(hb, wdn_ref[cols, :])
        r2 = ALPHA * x1 + acc + bdn_ref[...]
        xhat2, rstd2, y = _layer_norm_fwd(r2, l2g_ref[...], l2b_ref[...])
        err = y - t_ref[...]
        loss_ref[...] += jnp.sum(jnp.sum(err * err, axis=1, keepdims=True), axis=0, keepdims=True) * (0.5 / D_MODEL)
        dy = err * (1.0 / D_MODEL)
        dl2g_ref[...] += _colsum(dy * xhat2)
        dl2b_ref[...] += _colsum(dy)
        dr2 = _layer_norm_bwd(dy, xhat2, rstd2, l2g_ref[...])
        dbdn_ref[...] += _colsum(dr2)
        dr2b = dr2.astype(bf16)
        dr2bt_ref[...] = dr2b.T
        dhs = [_dot(dr2b, wdn_ref[cols, :], NT) for cols in chunks]
        dpbs = []
        for cols, dh in zip(chunks, dhs):
            dpre = dh * (2.0 * rl_ref[:, cols])
            dbup_ref[:, cols] += _colsum(dpre)
            dpb = dpre.astype(bf16)
            dpre_ref[:, cols] = dpb
            dpbs.append(dpb)
        dacc = _dot(dpbs[0], wup_ref[chunks[0], :])
        for cols, dpb in zip(chunks[1:], dpbs[1:]):
            dacc = dacc + _dot(dpb, wup_ref[cols, :])
        dx1_ref[...] = ALPHA * dr2 + dacc

    acc_shapes = [(1, LANES), (1, D_MODEL), (1, D_MODEL), (1, D_MODEL), (1, D_FF)]
    return pl.pallas_call(
        body, name="mlp_fwd_bwd", grid=(n,),
        in_specs=[_row_spec(tm, D_MODEL), _row_spec(tm, D_MODEL), _const_spec((1, D_MODEL)), _const_spec((1, D_MODEL)),
                  _const_spec((D_FF, D_MODEL)), _const_spec((1, D_FF)), _const_spec((D_FF, D_MODEL)),
                  _const_spec((1, D_MODEL)), _const_spec((1, D_MODEL)), _const_spec((1, D_MODEL))],
        out_specs=([_col_spec(D_MODEL, tm), _row_spec(tm, D_FF), _col_spec(D_MODEL, tm), _row_spec(tm, D_FF),
                    _row_spec(tm, D_MODEL)] + [_acc_spec(s) for s in acc_shapes]),
        out_shape=([_sds((D_MODEL, s_len), bf16), _sds((s_len, D_FF), bf16), _sds((D_MODEL, s_len), bf16),
                    _sds((s_len, D_FF), bf16), _sds((s_len, D_MODEL), f32)] + [_sds(s, f32) for s in acc_shapes]),
        scratch_shapes=[pltpu.VMEM((tm, D_FF), f32)],
        compiler_params=_cparams(("arbitrary",)),
    )(xhat1, tgt, ln1_g, ln1_b, w_up_t, b_up, w_down, b_down, ln2_g, ln2_b)


def _mid_bwd(dx1, xhat1, rstd1, g, ain, ob, y_ssm, ln1_g, w_out, w_glu_t, w_co_t, w_xo_t):
    s_len = dx1.shape[0]
    tm = TOKEN_TILE
    n = s_len // tm

    def body(dx1_ref, xh_ref, rs_ref, g_ref, ain_ref, o_ref, ys_ref, lg_ref, wout_ref, wglu_ref, wco_ref, wxo_ref,
             dxp_ref, dr1bt_ref, dgp_ref, dya_ref, dyc_ref, dglu_ref, dyssm_ref,
             dl1g_ref, dl1b_ref, dbg_ref):
        i = pl.program_id(0)

        @pl.when(i == 0)
        def _():
            dl1g_ref[...] = jnp.zeros_like(dl1g_ref)
            dl1b_ref[...] = jnp.zeros_like(dl1b_ref)
            dbg_ref[...] = jnp.zeros_like(dbg_ref)

        dx1 = dx1_ref[...]
        xhat = xh_ref[...]
        dl1g_ref[...] += _colsum(dx1 * xhat)
        dl1b_ref[...] += _colsum(dx1)
        dr1 = _layer_norm_bwd(dx1, xhat, rs_ref[...], lg_ref[...])
        dxp_ref[...] = ALPHA * dr1
        dr1b = dr1.astype(bf16)
        dr1bt_ref[...] = dr1b.T
        dm = _dot(dr1b, wout_ref[...], NT)

        _, ga, sb, ya, yb, yc = _branch_outputs(ys_ref, ain_ref, o_ref, wglu_ref, wco_ref, wxo_ref)
        gt = g_ref[...].astype(f32)
        branch = (ya, yb, yc)
        for j in range(3):
            cols = slice(j * D_MODEL, (j + 1) * D_MODEL)
            gj = gt[:, cols]
            dgp = dm * branch[j] * gj * (1.0 - gj)
            dbg_ref[:, cols] += _colsum(dgp)
            dgp_ref[:, cols] = dgp.astype(bf16)
        dya_ref[...] = (dm * gt[:, :D_MODEL]).astype(bf16)
        dyc_ref[...] = (dm * gt[:, 2 * D_MODEL:]).astype(bf16)
        dyb = dm * gt[:, D_MODEL:2 * D_MODEL]
        dga = (dyb * sb).astype(bf16)
        dgb = (dyb * ga * sb * (1.0 - sb)).astype(bf16)
        dglu_ref[:, :D_MODEL] = dga
        dglu_ref[:, D_MODEL:] = dgb
        dys = _dot(dga, wglu_ref[:D_MODEL, :]) + _dot(dgb, wglu_ref[D_MODEL:, :])
        dyssm_ref[...] = dys * _gelu_grad(ys_ref[...])

    row_cols = [(GATE_COLS, bf16), (D_MODEL, bf16), (D_MODEL, bf16), (2 * D_MODEL, bf16), (SSM_W, f32)]
    acc_shapes = [(1, D_MODEL), (1, D_MODEL), (1, GATE_COLS)]
    return pl.pallas_call(
        body, name="mid_bwd", grid=(n,),
        in_specs=[_row_spec(tm, D_MODEL), _row_spec(tm, D_MODEL), _row_spec(tm, 1), _row_spec(tm, GATE_COLS),
                  _row_spec(tm, CONV_W), _row_spec(tm, XATTN_W), _row_spec(tm, SSM_W),
                  _const_spec((1, D_MODEL)), _const_spec((D_MODEL, D_MODEL)), _const_spec((2 * D_MODEL, SSM_W)),
                  _const_spec((D_MODEL, CONV_W)), _const_spec((D_MODEL, XATTN_W))],
        out_specs=([_row_spec(tm, D_MODEL), _col_spec(D_MODEL, tm)] + [_row_spec(tm, c) for c, _ in row_cols]
                   + [_acc_spec(s) for s in acc_shapes]),
        out_shape=([_sds((s_len, D_MODEL), f32), _sds((D_MODEL, s_len), bf16)]
                   + [_sds((s_len, c), dt) for c, dt in row_cols] + [_sds(s, f32) for s in acc_shapes]),
        compiler_params=_cparams(("arbitrary",)),
    )(dx1, xhat1, rstd1, g, ain, ob, y_ssm, ln1_g, w_out, w_glu_t, w_co_t, w_xo_t)


def _ssm_bwd(u, dy, cm_all, b_half, c_half, pw, d_skip):
    s_len = u.shape[0]
    tb = SSM_BLOCK
    n = s_len // tb

    def body(u_ref, dy_ref, cm_ref, b_ref, c_ref, pw_ref, d_ref,
             du_ref, db_hbm, dc_hbm, da_ref, dd_ref,
             s_ref, g_ref, gcarry_ref, gcm_ref, db_ref, dc_ref, up_ref, dyp_ref, dup_ref, stage_ref):
        i = pl.program_id(0)

        @pl.when(i == 0)
        def _():
            gcarry_ref[...] = jnp.zeros_like(gcarry_ref)
            db_ref[...] = jnp.zeros_like(db_ref)
            dc_ref[...] = jnp.zeros_like(dc_ref)
            da_ref[...] = jnp.zeros_like(da_ref)
            dd_ref[...] = jnp.zeros_like(dd_ref)

        _rows_to_segments(u_ref, stage_ref, up_ref)
        _rows_to_segments(dy_ref, stage_ref, dyp_ref)
        u = up_ref[...]
        ub = u.astype(bf16)
        dy = dyp_ref[...]
        dyb = dy.astype(bf16)
        dd_ref[...] += _colsum(dy * u)

        for half in range(N_HALF):
            s_ref[:, half * HALF_COLS:(half + 1) * HALF_COLS] = _dot(ub[:, half * HALF_W:(half + 1) * HALF_W], b_ref[half])
        _ssm_scan(s_ref, pw_ref, cm_ref, reverse=False, unroll=True)

        for half in range(N_HALF):
            g_ref[:, half * HALF_COLS:(half + 1) * HALF_COLS] = _dot(dyb[:, half * HALF_W:(half + 1) * HALF_W], c_ref[half], NT)
        _ssm_scan(g_ref, pw_ref, None, reverse=True, unroll=True)
        _ssm_carries(0, g_ref, pw_ref, gcarry_ref, gcm_ref, reverse=True)
        _ssm_add_carry(g_ref, pw_ref, gcm_ref, reverse=True)

        for half in range(N_HALF):
            cols = slice(half * HALF_W, (half + 1) * HALF_W)
            scols = slice(half * HALF_COLS, (half + 1) * HALF_COLS)
            gb = g_ref[:, scols].astype(bf16)
            dup_ref[:, cols] = _dot(gb, b_ref[half], NT) + d_ref[:, cols] * dy[:, cols]
            db_ref[half] += _dot(ub[:, cols], gb, TN)
            dc_ref[half] += _dot(s_ref[:, scols].astype(bf16), dyb[:, cols], TN)
        _rows_from_segments(dup_ref, stage_ref, du_ref)

        for chunk in range(N_STATE // LANE_CHUNK):
            re, im = _state_cols(chunk)
            acc_r = da_ref[:, re]
            acc_i = da_ref[:, im]
            for k in range(SSM_SEG):
                rows = slice(k * SUBLANES, (k + 1) * SUBLANES)
                if k == 0:
                    pr, pi = cm_ref[:, re], cm_ref[:, im]
                else:
                    prev = slice((k - 1) * SUBLANES, k * SUBLANES)
                    pr, pi = s_ref[prev, re], s_ref[prev, im]
                gr, gi = g_ref[rows, re], g_ref[rows, im]
                acc_r = acc_r + (gr * pr + gi * pi)
                acc_i = acc_i + (gi * pr - gr * pi)
            da_ref[:, re] = acc_r
            da_ref[:, im] = acc_i

        @pl.when(i == n - 1)
        def _():
            pltpu.sync_copy(db_ref, db_hbm)
            pltpu.sync_copy(dc_ref, dc_hbm)

    rev = functools.partial(_row_spec, rev_n=n)
    any_spec = pl.BlockSpec(memory_space=pl.ANY)
    state_rows = pltpu.VMEM((tb, 2 * N_STATE), f32)
    seg_rows = pltpu.VMEM((SUBLANES, 2 * N_STATE), f32)
    tok_rows = pltpu.VMEM((tb, SSM_W), f32)
    return pl.pallas_call(
        body, name="ssm_bwd", grid=(n,),
        in_specs=[rev(tb, SSM_W), rev(tb, SSM_W), rev(SUBLANES, 2 * N_STATE),
                  _const_spec((N_HALF, HALF_W, HALF_COLS)), _const_spec((N_HALF, HALF_COLS, HALF_W)),
                  _const_spec((SSM_SEG, 2 * N_STATE)), _const_spec((1, SSM_W))],
        out_specs=[rev(tb, SSM_W), any_spec, any_spec, _acc_spec((SUBLANES, 2 * N_STATE)), _acc_spec((1, SSM_W))],
        out_shape=[_sds((s_len, SSM_W), f32), _sds((N_HALF, HALF_W, HALF_COLS), f32),
                   _sds((N_HALF, HALF_COLS, HALF_W), f32), _sds((SUBLANES, 2 * N_STATE), f32), _sds((1, SSM_W), f32)],
        scratch_shapes=[state_rows, state_rows, seg_rows, seg_rows,
                        pltpu.VMEM((N_HALF, HALF_W, HALF_COLS), f32), pltpu.VMEM((N_HALF, HALF_COLS, HALF_W), f32),
                        tok_rows, tok_rows, tok_rows, pltpu.VMEM((SSM_W // LANES, tb, LANES), f32)],
        compiler_params=_cparams(("arbitrary",)),
    )(u, dy, cm_all, b_half, c_half, pw, d_skip)


def _branch_bwd(dya, dyc, cin, q, kv, k_t, conv_w, w_co_t, w_xo_t, side_blocks):
    s_len = dya.shape[0]
    tm = 2 * TOKEN_TILE
    n = s_len // tm
    halo_blocks = tm // 8
    ns = len(side_blocks)
    conv_tile = _sds((8, CONV_W), f32)
    side_in_specs, side_shapes, side_sems = _side_gather_specs(list(side_blocks) + [conv_tile])

    def body(*refs):
        (dya_ref, dyc_ref, cin_ref, cprev_ref, q_ref, kv_ref, cw_ref, wco_ref, wxo_ref, kt_ref) = refs[:10]
        side_ins = refs[10:10 + ns]
        dconv_ref, dq_ref, dkv_ref = refs[10 + ns:13 + ns]
        side_outs = refs[13 + ns:14 + 2 * ns]
        zs_ref, dczs_ref, dcw_ref = refs[14 + 2 * ns:17 + 2 * ns]
        copies = _side_gather_copies(list(side_ins) + [dcw_ref], side_outs, *refs[17 + 2 * ns:])
        side, conv_side = copies[:ns * N_DEV], copies[ns * N_DEV:]
        i = pl.program_id(0)
        tile = n - 1 - i

        @pl.when(i == 0)
        def _():
            dcw_ref[...] = jnp.zeros_like(dcw_ref)
            dkv_ref[...] = jnp.zeros_like(dkv_ref)
            dczs_ref[tm:tm + 8, :] = jnp.zeros((8, CONV_W), f32)
            for cp in side:
                cp.start()

        cin = cin_ref[...]
        cb, cc, ch = cin[:, :CONV_W], cin[:, CONV_W:2 * CONV_W], cin[:, 2 * CONV_W:]
        z = cc * ch
        cprev = cprev_ref[...]
        zprev = cprev[:, CONV_W:2 * CONV_W] * cprev[:, 2 * CONV_W:]
        zs_ref[0:8, :] = jnp.where(tile == 0, 0.0, zprev)
        zs_ref[8:8 + tm, :] = z
        z1 = zs_ref[pl.ds(7, tm), :]
        z2 = zs_ref[pl.ds(6, tm), :]
        cw = cw_ref[...]
        cz = cw[0:1] * z2 + cw[1:2] * z1 + cw[2:3] * z

        dain = _dot(dya_ref[...], wco_ref[...])
        dcb = dain * cz
        dcz = dain * cb
        dczs_ref[0:tm, :] = dcz
        dcz1 = dczs_ref[pl.ds(1, tm), :]
        dcz2 = dczs_ref[pl.ds(2, tm), :]
        dz = cw[2:3] * dcz + cw[1:2] * dcz1 + cw[0:1] * dcz2
        dczs_ref[tm:tm + 8, :] = dczs_ref[0:8, :]
        dcw_ref[0:1, :] += _colsum(dcz * z2)
        dcw_ref[1:2, :] += _colsum(dcz * z1)
        dcw_ref[2:3, :] += _colsum(dcz * z)
        dconv_ref[:, :CONV_W] = dcb.astype(bf16)
        dconv_ref[:, CONV_W:2 * CONV_W] = (dz * ch).astype(bf16)
        dconv_ref[:, 2 * CONV_W:] = (dz * cc).astype(bf16)

        qb = q_ref[...]
        dob = _dot(dyc_ref[...], wxo_ref[...]).astype(bf16)
        kv = kv_ref[...]
        heads = range(HEADS)
        hcs = [slice(h * HEAD_DIM, (h + 1) * HEAD_DIM) for h in heads]
        vcs = [slice(XATTN_W + h * HEAD_DIM, XATTN_W + (h + 1) * HEAD_DIM) for h in heads]
        s_t = [_dot(kv[:, hcs[h]], qb[:, hcs[h]], NT) * (HEAD_DIM ** -0.5) for h in heads]
        dp_t = [_dot(kv[:, vcs[h]], dob[:, hcs[h]], NT) for h in heads]
        e_t = [jnp.exp(s_t[h] - jnp.max(s_t[h], axis=0, keepdims=True)) for h in heads]
        p_t = [e_t[h] / jnp.sum(e_t[h], axis=0, keepdims=True) for h in heads]
        dv = [_dot(p_t[h].astype(bf16), dob[:, hcs[h]]) for h in heads]
        ds_t = [(p_t[h] * (dp_t[h] - jnp.sum(dp_t[h] * p_t[h], axis=0, keepdims=True)) * (HEAD_DIM ** -0.5)).astype(bf16)
                for h in heads]
        dk = [_dot(ds_t[h], qb[:, hcs[h]]) for h in heads]
        dq_t = [_dot(kt_ref[hcs[h], :], ds_t[h]) for h in heads]
        dq_ref[...] = jnp.concatenate(dq_t, axis=0).T.astype(bf16)
        dkv_ref[...] += jnp.concatenate(dk + dv, axis=1)

        @pl.when(i == n - 1)
        def _():
            for cp in conv_side:
                cp.start()
            for cp in side + conv_side:
                cp.wait()

    rev = functools.partial(_row_spec, rev_n=n)
    prev_spec = pl.BlockSpec((8, 3 * CONV_W), lambda i: (jnp.maximum((n - 1 - i) * halo_blocks - 1, 0), 0))
    outs = pl.pallas_call(
        body, name="branch_bwd", grid=(n,),
        in_specs=[rev(tm, D_MODEL), rev(tm, D_MODEL), rev(tm, 3 * CONV_W), prev_spec, rev(tm, XATTN_W),
                  _const_spec((MEM_LEN, 2 * XATTN_W)), _const_spec((3, CONV_W)), _const_spec((D_MODEL, CONV_W)),
                  _const_spec((D_MODEL, XATTN_W)), _const_spec((XATTN_W, MEM_LEN))] + side_in_specs[:ns],
        out_specs=[rev(tm, 3 * CONV_W), rev(tm, XATTN_W), _acc_spec((MEM_LEN, 2 * XATTN_W))] + side_in_specs,
        out_shape=[_sds((s_len, 3 * CONV_W), bf16), _sds((s_len, XATTN_W), bf16),
                   _sds((MEM_LEN, 2 * XATTN_W), f32)] + side_shapes,
        scratch_shapes=[pltpu.VMEM((tm + 8, CONV_W), f32), pltpu.VMEM((tm + 8, CONV_W), f32),
                        pltpu.VMEM((8, CONV_W), f32)] + side_sems,
        compiler_params=_cparams(("arbitrary",)),
    )(dya, dyc, cin, cin, q, kv, conv_w, w_co_t, w_xo_t, k_t, *side_blocks)
    return outs[0], outs[1], outs[2], outs[3:]


def _in_proj_bwd(dgp, dconv, du, dq, dxp, w_in_t):
    s_len = dgp.shape[0]
    tm = 2 * TOKEN_TILE
    n = s_len // tm

    def body(dgp_ref, dconv_ref, du_ref, dq_ref, dxp_ref, win_ref, dx_ref, dproj_ref):
        dproj = jnp.concatenate([dgp_ref[...], dconv_ref[...], du_ref[...].astype(bf16), dq_ref[...]], axis=1)
        dproj_ref[...] = dproj
        dx_ref[...] = dxp_ref[...] + _dot(dproj, win_ref[...])

    return pl.pallas_call(
        body, name="in_proj_bwd", grid=(n,),
        in_specs=[_row_spec(tm, GATE_COLS), _row_spec(tm, 3 * CONV_W), _row_spec(tm, SSM_W), _row_spec(tm, XATTN_W),
                  _row_spec(tm, D_MODEL), _const_spec((IN_COLS, D_MODEL))],
        out_specs=[_row_spec(tm, D_MODEL), _row_spec(tm, IN_COLS)],
        out_shape=[_sds((s_len, D_MODEL), f32), _sds((s_len, IN_COLS), bf16)],
        compiler_params=_cparams(("parallel",)),
    )(dgp, dconv, du, dq, dxp, w_in_t)


N_CHIP = 4
CHIP_STEPS = [(1, 1), (1, 0), (0, 1), (0, 0)]


def _flip(v, d):
    return 1 - v if d else v


def _chip_order():
    x, y, _ = _mesh_place()
    return jnp.stack([2 * _flip(x, dx) + _flip(y, dy) for dx, dy in CHIP_STEPS]).astype(jnp.int32)


def _weight_grads_scatter(problems, name):
    dims = []
    first = 0
    for a_t, b, tm, tt in problems:
        m, s_len = a_t.shape
        w = b.shape[1] // N_DEV
        tm, tt = min(tm, m), min(tt, s_len)
        assert m % tm == 0 and s_len % tt == 0
        nm, nt = m // tm, s_len // tt
        dims.append(dict(m=m, w=w, tm=tm, tt=tt, nm=nm, nt=nt, first=first, steps=N_CHIP * nm * nt))
        first += N_CHIP * nm * nt
    n_prob, total = len(problems), first
    n_scratch = 9

    def place(d, s):
        local = jnp.clip(s - d["first"], 0, d["steps"] - 1)
        return local // (d["nm"] * d["nt"]), (local // d["nt"]) % d["nm"], local % d["nt"]

    def run(d, q, im, t, a_ref, b_ref, recv_ref, acc_ref, send_ref, sib_ref, stash_ref,
            d2d_send, d2d_recv, ici_send, ici_recv, local_sem):
        tm, w, nm, nt = d["tm"], d["w"], d["nm"], d["nt"]
        x, y, c = _mesh_place()
        mesh_id = pl.DeviceIdType.MESH

        @pl.when(t == 0)
        def _():
            acc_ref[...] = jnp.zeros_like(acc_ref)

        acc_ref[...] += _dot(a_ref[...], b_ref[...])

        def to_sibling(qq, imm):
            rows = pl.ds(pl.multiple_of(imm * tm, tm), tm)
            return pltpu.make_async_remote_copy(
                src_ref=send_ref.at[qq, 0, rows, :], dst_ref=sib_ref.at[qq, rows, :],
                send_sem=d2d_send.at[qq], recv_sem=d2d_recv.at[qq, imm],
                device_id=(x, y, 1 - c), device_id_type=mesh_id)

        def finish_tile(qq, imm):
            rows = pl.ds(pl.multiple_of(imm * tm, tm), tm)
            to_sibling(qq, imm).wait_recv()
            both = stash_ref[...] + sib_ref[qq, rows, :].astype(f32)
            send_ref[qq, 1, rows, :] = both.astype(bf16)
            for step, (dx, dy) in enumerate(CHIP_STEPS):
                @pl.when(qq == step)
                def _(step=step, dx=dx, dy=dy):
                    src, dst = send_ref.at[step, 1, rows, :], recv_ref.at[step, rows, :]
                    if dx or dy:
                        pltpu.make_async_remote_copy(
                            src_ref=src, dst_ref=dst, send_sem=ici_send.at[step], recv_sem=ici_recv.at[step],
                            device_id=(_flip(x, dx), _flip(y, dy), c), device_id_type=mesh_id).start()
                    else:
                        pltpu.make_async_copy(src, dst, local_sem).start()

        @pl.when(t == nt - 1)
        def _():
            tile = q * nm + im

            @pl.when(tile > 0)
            def _():
                finish_tile((tile - 1) // nm, (tile - 1) % nm)

            rows = pl.ds(pl.multiple_of(im * tm, tm), tm)
            for core in (0, 1):
                @pl.when(c == core)
                def _(core=core):
                    other = 1 - core
                    send_ref[q, 0, rows, :] = acc_ref[:, other * w:(other + 1) * w].astype(bf16)
                    stash_ref[...] = acc_ref[:, core * w:(core + 1) * w]
            to_sibling(q, im).start()

            @pl.when(tile == N_CHIP * nm - 1)
            def _():
                finish_tile(q, im)
                for step, (dx, dy) in enumerate(CHIP_STEPS):
                    pltpu.make_async_remote_copy(
                        src_ref=send_ref.at[step, 0], dst_ref=sib_ref.at[step],
                        send_sem=d2d_send.at[step], recv_sem=d2d_recv.at[step, 0],
                        device_id=(x, y, 1 - c), device_id_type=mesh_id).wait_send()
                    src, dst = send_ref.at[step, 1], recv_ref.at[step]
                    if dx or dy:
                        pltpu.make_async_remote_copy(
                            src_ref=src, dst_ref=dst, send_sem=ici_send.at[step], recv_sem=ici_recv.at[step],
                            device_id=(_flip(x, dx), _flip(y, dy), c), device_id_type=mesh_id).wait()
                    else:
                        pltpu.make_async_copy(src, dst, local_sem).wait()

    def body(order_ref, *refs):
        del order_ref
        s = pl.program_id(0)
        operands, rest = refs[:2 * n_prob], refs[2 * n_prob:]
        results, scratch = rest[:n_prob], rest[n_prob:]
        for k, d in enumerate(dims):
            @pl.when((s >= d["first"]) & (s < d["first"] + d["steps"]))
            def _(k=k, d=d):
                q, im, t = place(d, s)
                run(d, q, im, t, operands[2 * k], operands[2 * k + 1], results[k],
                    *scratch[n_scratch * k:n_scratch * (k + 1)])

    in_specs, scratch_shapes = [], []
    for d in dims:
        def a_map(s, order, d=d):
            _, im, t = place(d, s)
            return im, t

        def b_map(s, order, d=d):
            q, _, t = place(d, s)
            return t, order[q]

        in_specs += [pl.BlockSpec((d["tm"], d["tt"]), a_map), pl.BlockSpec((d["tt"], 2 * d["w"]), b_map)]
        scratch_shapes += [pltpu.VMEM((d["tm"], 2 * d["w"]), f32), pltpu.VMEM((N_CHIP, 2, d["m"], d["w"]), bf16),
                           pltpu.VMEM((N_CHIP, d["m"], d["w"]), bf16), pltpu.VMEM((d["tm"], d["w"]), f32),
                           pltpu.SemaphoreType.DMA((N_CHIP,)), pltpu.SemaphoreType.DMA((N_CHIP, d["nm"])),
                           pltpu.SemaphoreType.DMA((N_CHIP - 1,)), pltpu.SemaphoreType.DMA((N_CHIP - 1,)),
                           pltpu.SemaphoreType.DMA]
    grid_spec = pltpu.PrefetchScalarGridSpec(
        num_scalar_prefetch=1, grid=(total,), in_specs=in_specs,
        out_specs=[pl.BlockSpec(memory_space=pl.ANY)] * n_prob, scratch_shapes=scratch_shapes)
    return pl.pallas_call(
        body, name=name, grid_spec=grid_spec,
        out_shape=[_sds((N_CHIP, d["m"], d["w"]), bf16) for d in dims],
        compiler_params=_cparams(("arbitrary",)),
    )(_chip_order(), *[op for a_t, b, _, _ in problems for op in (a_t, b)])


def _adamw(w, g, m, v):
    m = ADAM_B1 * m + (1.0 - ADAM_B1) * g
    v = ADAM_B2 * v + (1.0 - ADAM_B2) * jnp.square(g)
    m_hat = m / (1.0 - ADAM_B1 ** ADAM_STEP)
    v_hat = v / (1.0 - ADAM_B2 ** ADAM_STEP)
    delta = -ADAM_LR * (m_hat / (jnp.sqrt(v_hat) + ADAM_EPS) + ADAM_WD * w)
    return delta, m, v


def _sum_parts(p_ref):
    g = p_ref[0].astype(f32)
    for j in range(1, p_ref.shape[0]):
        g = g + p_ref[j].astype(f32)
    return g


def _adamw_update(w, m, v, parts, name, transposed):
    rows, cols = w.shape
    n_parts = parts.shape[0]
    if transposed:
        tc = 256
        steps = cols // tc
        p_spec = pl.BlockSpec((n_parts, tc, rows), lambda i: (0, i, 0))
        spec = pl.BlockSpec((rows, tc), lambda i: (0, i))
    else:
        tr = next(t for t in (256, 128, 64, 32, 16, 8) if rows % t == 0)
        steps = rows // tr
        p_spec = pl.BlockSpec((n_parts, tr, cols), lambda i: (0, i, 0))
        spec = pl.BlockSpec((tr, cols), lambda i: (i, 0))

    def body(w_ref, p_ref, m_ref, v_ref, g_ref, d_ref, nm_ref, nv_ref):
        g = _sum_parts(p_ref)
        if transposed:
            g = g.T
        g_ref[...] = g
        d_ref[...], nm_ref[...], nv_ref[...] = _adamw(w_ref[...], g, m_ref[...], v_ref[...])

    return pl.pallas_call(
        body, name=name, grid=(steps,),
        in_specs=[spec, p_spec, spec, spec], out_specs=[spec] * 4,
        out_shape=[_sds((rows, cols), f32)] * 4,
        compiler_params=_cparams(("parallel",)),
    )(w, parts, m, v)


def _adamw_whole(ws, ms, vs, parts, transposed):
    n = len(ws)

    def body(*refs):
        w_refs, m_refs, v_refs, p_refs = (refs[j * n:(j + 1) * n] for j in range(4))
        out_refs = refs[4 * n:]
        for a in range(n):
            g = _sum_parts(p_refs[a])
            if transposed[a]:
                g = g.T
            d, nm, nv = _adamw(w_refs[a][...], g, m_refs[a][...], v_refs[a][...])
            for j, val in enumerate((g, d, nm, nv)):
                out_refs[j * n + a][...] = val

    res = pl.pallas_call(
        body, name="adamw_small_weights",
        out_shape=[_sds(w.shape, f32) for _ in range(4) for w in ws],
        compiler_params=_cparams(),
    )(*ws, *ms, *vs, *parts)
    return [res[j * n:(j + 1) * n] for j in range(4)]


SMALL_GROUPS = [
    (["b_gate", "ln1_g", "ln1_b", "b_up", "b_down", "ln2_g", "ln2_b", "ssm_d"], 1),
    (["ssm_lam_re", "ssm_lam_im", "ssm_c_re", "ssm_c_im", "ssm_b_re", "ssm_b_im"], 0),
    (["conv_w"], 0),
    (["ssm_log_dt"], 0),
]


def _sum_small(group_parts):
    def body(*refs):
        n = len(refs) // 2
        for p_ref, o_ref in zip(refs[:n], refs[n:]):
            o_ref[...] = _sum_parts(p_ref)

    return pl.pallas_call(
        body, name="sum_small",
        out_shape=[_sds(p.shape[1:], f32) for p in group_parts],
        compiler_params=_cparams(),
    )(*group_parts)


def _adamw_small(ws, ms, vs, group_sums):
    names = [k for group, _ in SMALL_GROUPS for k in group]
    n = len(names)

    def body(*refs):
        w_refs, m_refs, v_refs = (dict(zip(names, refs[j * n:(j + 1) * n])) for j in range(3))
        p_refs = refs[3 * n:3 * n + len(SMALL_GROUPS)]
        out_refs = [dict(zip(names, refs[3 * n + len(SMALL_GROUPS) + j * n:][:n])) for j in range(4)]
        for (group, axis), p_ref in zip(SMALL_GROUPS, p_refs):
            total = p_ref[...]
            off = 0
            for k in group:
                size = SMALL[k][axis]
                g = total[:, off:off + size] if axis == 1 else total[off:off + size, :]
                off += size
                d, nm, nv = _adamw(w_refs[k][...], g, m_refs[k][...], v_refs[k][...])
                for j, val in enumerate((g, d, nm, nv)):
                    out_refs[j][k][...] = val

    res = pl.pallas_call(
        body, name="adamw_small",
        out_shape=[_sds(SMALL[k], f32) for _ in range(4) for k in names],
        compiler_params=_cparams(),
    )(*[ws[k] for k in names], *[ms[k] for k in names], *[vs[k] for k in names], *group_sums)
    return [dict(zip(names, res[j * n:(j + 1) * n])) for j in range(4)]


def _ssm_discretize(lam_re, lam_im, log_dt, b_re, b_im):
    dt = jnp.exp(log_dt)[:, None]
    mag = jnp.exp(lam_re * dt)
    abar_r = mag * jnp.cos(lam_im * dt)
    abar_i = mag * jnp.sin(lam_im * dt)
    den = lam_re * lam_re + lam_im * lam_im
    nr = abar_r - 1.0
    ni = abar_i
    kr = (nr * lam_re + ni * lam_im) / den
    ki = (ni * lam_re - nr * lam_im) / den
    bbar_r = kr[:, None, :] * b_re - ki[:, None, :] * b_im
    bbar_i = kr[:, None, :] * b_im + ki[:, None, :] * b_re
    return abar_r, abar_i, bbar_r, bbar_i


def _state_layout(re, im):
    parts = []
    for half in range(N_HALF):
        cols = slice(half * HALF_STATE, (half + 1) * HALF_STATE)
        parts += [re[..., cols], im[..., cols]]
    return jnp.concatenate(parts, axis=-1)


def _state_unlayout(a):
    re = jnp.concatenate([a[..., _half_cols(h)[0]] for h in range(N_HALF)], axis=-1)
    im = jnp.concatenate([a[..., _half_cols(h)[1]] for h in range(N_HALF)], axis=-1)
    return re, im


def _abar_powers(abar_r, abar_i):
    pr, pi = abar_r.reshape(1, N_STATE), abar_i.reshape(1, N_STATE)
    while pr.shape[0] < SSM_SEG:
        tr, ti = pr[-1:], pi[-1:]
        pr, pi = (jnp.concatenate([pr, pr * tr - pi * ti], axis=0), jnp.concatenate([pi, pr * ti + pi * tr], axis=0))
    return _state_layout(pr, pi)


HALF_GROUPS = SSM_GROUPS // N_HALF


def _half_block_diag(blocks):
    _, r, c = blocks.shape
    eye = jnp.eye(HALF_GROUPS, dtype=blocks.dtype)
    b4 = blocks.reshape(N_HALF, HALF_GROUPS, r, c)
    return jnp.einsum("ngrc,gk->ngrkc", b4, eye).reshape(N_HALF, HALF_GROUPS * r, HALF_GROUPS * c)


def _half_diag_blocks(mat, r, c):
    eye = jnp.eye(HALF_GROUPS, dtype=mat.dtype)
    m5 = mat.reshape(N_HALF, HALF_GROUPS, r, HALF_GROUPS, c)
    return jnp.einsum("ngrkc,gk->ngrc", m5, eye).reshape(SSM_GROUPS, r, c)


BIG = ["w_in", "w_conv_out", "w_glu", "w_kv", "w_xattn_out", "w_out", "w_up", "w_down"]
GATHER_TRANSPOSED = ["w_conv_out", "w_glu", "w_xattn_out", "w_up"]
PARTS_TRANSPOSED = ["w_in", "w_kv", "w_out", "w_down"]
SMALL = {"b_gate": (1, GATE_COLS), "conv_w": (3, CONV_W), "ssm_lam_re": (SSM_GROUPS, SSM_STATE),
         "ssm_lam_im": (SSM_GROUPS, SSM_STATE), "ssm_log_dt": (1, SSM_GROUPS),
         "ssm_b_re": (SSM_W, SSM_STATE), "ssm_b_im": (SSM_W, SSM_STATE),
         "ssm_c_re": (SSM_W, SSM_STATE), "ssm_c_im": (SSM_W, SSM_STATE), "ssm_d": (1, SSM_W),
         "ln1_g": (1, D_MODEL), "ln1_b": (1, D_MODEL), "b_up": (1, D_FF), "b_down": (1, D_MODEL),
         "ln2_g": (1, D_MODEL), "ln2_b": (1, D_MODEL)}
WEIGHTS = ["w_in", "b_gate", "conv_w", "w_conv_out", "ssm_lam_re", "ssm_lam_im", "ssm_log_dt", "ssm_b_re", "ssm_b_im",
           "ssm_c_re", "ssm_c_im", "ssm_d", "w_glu", "w_kv", "w_xattn_out", "w_out", "ln1_g", "ln1_b", "w_up", "b_up",
           "w_down", "b_down", "ln2_g", "ln2_b"]


def _local_step(x, mem, tgt, full, late, small):
    lam_re, lam_im, log_dt = small["ssm_lam_re"], small["ssm_lam_im"], small["ssm_log_dt"].reshape(SSM_GROUPS)
    c_shape = (SSM_GROUPS, SSM_GROUP, SSM_STATE)
    disc, disc_vjp = jax.vjp(_ssm_discretize, lam_re, lam_im, log_dt,
                             small["ssm_b_re"].reshape(c_shape), small["ssm_b_im"].reshape(c_shape))
    abar_r, abar_i, bbar_r, bbar_i = disc
    pw = _abar_powers(abar_r, abar_i)
    c_re, c_im = small["ssm_c_re"].reshape(c_shape), small["ssm_c_im"].reshape(c_shape)
    b_half = jnp.concatenate([_half_block_diag(bbar_r), _half_block_diag(bbar_i)], axis=2).astype(bf16)
    c_half = jnp.concatenate([_half_block_diag(c_re.transpose(0, 2, 1)), -_half_block_diag(c_im.transpose(0, 2, 1))],
                             axis=1).astype(bf16)

    s_len = x.shape[0]
    stack = lambda a: a.reshape(-1, a.shape[-1])
    kv, k_t, memb = _kv_proj(mem, full["w_kv"])
    (xbt, g, cin, u, q, ain, ob, aint, obt), side = _in_proj(
        x, full["w_in"], small["b_gate"], small["conv_w"], kv,
        [late[k] for k in ("w_glu", "w_conv_out", "w_xattn_out", "w_out", "w_up")])
    w_glu_t, w_co_t, w_xo_t, w_out, w_up_t = (stack(a) for a in side)
    y_ssm, cm_all, side = _ssm_fwd(u, b_half, c_half, pw, small["ssm_d"], [late["w_down"]])
    w_down = stack(side[0])
    ysbt, mb, xhat1, rstd1 = _mid_fwd(y_ssm, g, ain, ob, x, w_glu_t, w_co_t, w_xo_t, w_out,
                                      small["ln1_g"], small["ln1_b"])
    (x1bt, hdn, dr2bt, dpre, dx1, loss, dl2g, dl2b, dbdn, dbup) = _mlp_fwd_bwd(
        xhat1, tgt, small["ln1_g"], small["ln1_b"], w_up_t, small["b_up"], w_down,
        small["b_down"], small["ln2_g"], small["ln2_b"])
    (dxp, dr1bt, dgp, dya, dyc, dglu, dyssm, dl1g, dl1b, dbg) = _mid_bwd(
        dx1, xhat1, rstd1, g, ain, ob, y_ssm, small["ln1_g"], w_out, w_glu_t, w_co_t, w_xo_t)
    du, db_half, dc_half, da8, dd = _ssm_bwd(u, dyssm, cm_all, b_half, c_half, pw, small["ssm_d"])
    dabar_r, dabar_i = _state_unlayout(jnp.sum(da8, axis=0))
    dbbar_r = _half_diag_blocks(db_half[:, :, :HALF_STATE], SSM_GROUP, SSM_STATE)
    dbbar_i = _half_diag_blocks(db_half[:, :, HALF_STATE:], SSM_GROUP, SSM_STATE)
    g_shape = (SSM_GROUPS, SSM_STATE)
    dlam_re, dlam_im, dlog_dt, db_re, db_im = disc_vjp(
        (dabar_r.reshape(g_shape), dabar_i.reshape(g_shape), dbbar_r, dbbar_i))
    dc_re = _half_diag_blocks(dc_half[:, :HALF_STATE, :], SSM_STATE, SSM_GROUP).transpose(0, 2, 1)
    dc_im = -_half_diag_blocks(dc_half[:, HALF_STATE:, :], SSM_STATE, SSM_GROUP).transpose(0, 2, 1)

    small_grads = {
        "b_gate": dbg, "ssm_lam_re": dlam_re, "ssm_lam_im": dlam_im, "ssm_log_dt": dlog_dt,
        "ssm_b_re": db_re, "ssm_b_im": db_im, "ssm_c_re": dc_re, "ssm_c_im": dc_im, "ssm_d": dd,
        "ln1_g": dl1g, "ln1_b": dl1b, "b_up": dbup, "b_down": dbdn, "ln2_g": dl2g, "ln2_b": dl2b,
    }
    small_grads = {k: a.reshape(SMALL[k]) for k, a in small_grads.items()}
    groups = [(group, axis) for group, axis in SMALL_GROUPS if group != ["conv_w"]]
    stacks = [jnp.concatenate([small_grads[k] for k in group], axis=axis) if len(group) > 1 else small_grads[group[0]]
              for group, axis in groups]
    n_rowvec = stacks[0].shape[1]
    stacks[0] = jnp.concatenate([stacks[0], loss], axis=1)
    dense = lambda a: a.reshape(-1, LANES) if a.size % LANES == 0 else a
    dconv, dq, dkv, group_parts = _branch_bwd(dya, dyc, cin, q, kv, k_t, small["conv_w"], w_co_t, w_xo_t,
                                              [dense(a) for a in stacks])
    dx, dproj = _in_proj_bwd(dgp, dconv, du, dq, dxp, full["w_in"])
    tm, tt = 512, 2048
    products = {
        "w_down": (dr2bt, hdn, tm, 2 * tt), "w_up": (x1bt, dpre, tm, 2 * tt), "w_out": (dr1bt, mb, D_MODEL, s_len),
        "w_glu": (ysbt, dglu, tm, s_len), "w_conv_out": (aint, dya, tm, s_len), "w_xattn_out": (obt, dyc, tm, s_len),
        "w_in": (xbt, dproj, tm, tt),
    }
    recv = {k: _weight_grads_scatter([problem], "d" + k)[0] for k, problem in products.items() if k != "w_xattn_out"}
    recv["w_xattn_out"], recv["w_kv"] = _weight_grads_scatter(
        [products["w_xattn_out"], (dkv.T.astype(bf16), memb, D_MODEL, MEM_LEN)], "dw_xattn_out_kv")
    sums = _sum_small(group_parts)
    group_sums = dict(zip([tuple(group) for group, _ in groups], [s.reshape(a.shape) for s, a in zip(sums, stacks)]))
    group_sums[("conv_w",)] = sums[-1][0:3]
    first = tuple(groups[0][0])
    loss_all = group_sums[first][0, n_rowvec]
    group_sums[first] = group_sums[first][:, :n_rowvec]
    return loss_all, dx, recv, [group_sums[tuple(group)] for group, _ in SMALL_GROUPS]


def kernel(x, mem, w_in, b_gate, conv_w, w_conv_out, ssm_lam_re, ssm_lam_im, ssm_log_dt, ssm_b_re, ssm_b_im, ssm_c_re, ssm_c_im, ssm_d, w_glu, w_kv, w_xattn_out, w_out, ln1_g, ln1_b, w_up, b_up, w_down, b_down, ln2_g, ln2_b, loss_target, m_w_in, m_b_gate, m_conv_w, m_w_conv_out, m_ssm_lam_re, m_ssm_lam_im, m_ssm_log_dt, m_ssm_b_re, m_ssm_b_im, m_ssm_c_re, m_ssm_c_im, m_ssm_d, m_w_glu, m_w_kv, m_w_xattn_out, m_w_out, m_ln1_g, m_ln1_b, m_w_up, m_b_up, m_w_down, m_b_down, m_ln2_g, m_ln2_b, v_w_in, v_b_gate, v_conv_w, v_w_conv_out, v_ssm_lam_re, v_ssm_lam_im, v_ssm_log_dt, v_ssm_b_re, v_ssm_b_im, v_ssm_c_re, v_ssm_c_im, v_ssm_d, v_w_glu, v_w_kv, v_w_xattn_out, v_w_out, v_ln1_g, v_ln1_b, v_w_up, v_b_up, v_w_down, v_b_down, v_ln2_g, v_ln2_b):
    w = dict(w_in=w_in, b_gate=b_gate, conv_w=conv_w, w_conv_out=w_conv_out, ssm_lam_re=ssm_lam_re,
             ssm_lam_im=ssm_lam_im, ssm_log_dt=ssm_log_dt, ssm_b_re=ssm_b_re, ssm_b_im=ssm_b_im, ssm_c_re=ssm_c_re,
             ssm_c_im=ssm_c_im, ssm_d=ssm_d, w_glu=w_glu, w_kv=w_kv, w_xattn_out=w_xattn_out, w_out=w_out,
             ln1_g=ln1_g, ln1_b=ln1_b, w_up=w_up, b_up=b_up, w_down=w_down, b_down=b_down, ln2_g=ln2_g, ln2_b=ln2_b)
    m = dict(w_in=m_w_in, b_gate=m_b_gate, conv_w=m_conv_w, w_conv_out=m_w_conv_out, ssm_lam_re=m_ssm_lam_re,
             ssm_lam_im=m_ssm_lam_im, ssm_log_dt=m_ssm_log_dt, ssm_b_re=m_ssm_b_re, ssm_b_im=m_ssm_b_im,
             ssm_c_re=m_ssm_c_re, ssm_c_im=m_ssm_c_im, ssm_d=m_ssm_d, w_glu=m_w_glu, w_kv=m_w_kv,
             w_xattn_out=m_w_xattn_out, w_out=m_w_out, ln1_g=m_ln1_g, ln1_b=m_ln1_b, w_up=m_w_up, b_up=m_b_up,
             w_down=m_w_down, b_down=m_b_down, ln2_g=m_ln2_g, ln2_b=m_ln2_b)
    v = dict(w_in=v_w_in, b_gate=v_b_gate, conv_w=v_conv_w, w_conv_out=v_w_conv_out, ssm_lam_re=v_ssm_lam_re,
             ssm_lam_im=v_ssm_lam_im, ssm_log_dt=v_ssm_log_dt, ssm_b_re=v_ssm_b_re, ssm_b_im=v_ssm_b_im,
             ssm_c_re=v_ssm_c_re, ssm_c_im=v_ssm_c_im, ssm_d=v_ssm_d, w_glu=v_w_glu, w_kv=v_w_kv,
             w_xattn_out=v_w_xattn_out, w_out=v_w_out, ln1_g=v_ln1_g, ln1_b=v_ln1_b, w_up=v_w_up, b_up=v_b_up,
             w_down=v_w_down, b_down=v_b_down, ln2_g=v_ln2_g, ln2_b=v_ln2_b)
    out_shapes = {k: a.shape for k, a in w.items()}
    swapped = ("w_in", "ssm_b_re", "ssm_b_im")

    def shard2d(k, a):
        if k in swapped:
            a = jnp.swapaxes(a, -1, -2)
        if k in SMALL:
            return a.reshape((3, CONV_W // N_DEV) if k == "conv_w" else SMALL[k])
        return a[0]

    def result(k, a):
        if k in swapped:
            shape = out_shapes[k]
            return jnp.swapaxes(a.reshape(shape[:-2] + (shape[-1], shape[-2])), -1, -2)
        return a.reshape(out_shapes[k])

    w, m, v = ({k: shard2d(k, a) for k, a in d.items()} for d in (w, m, v))

    shards = {k: w[k].T.astype(bf16) if k in GATHER_TRANSPOSED else w[k].astype(bf16) for k in BIG}
    conv_pad = jnp.pad(w["conv_w"], ((0, 5), (0, LANES - CONV_W // N_DEV)))
    early = ["w_in", "w_kv"]
    gathered = _all_gather([shards[k] for k in early] + [conv_pad], "gather_weights")
    full = {k: a.reshape(-1, a.shape[-1]) for k, a in zip(early, gathered[:-1])}
    late = {k: shards[k] for k in BIG if k not in early}
    conv_full = gathered[-1][:, :3, :CONV_W // N_DEV].transpose(1, 0, 2).reshape(3, CONV_W)
    small = {k: (conv_full if k == "conv_w" else w[k]) for k in SMALL}

    loss, dx, recv, group_sums = _local_step(x[0], mem[0], loss_target[0], full, late, small)

    grads, deltas, new_m, new_v = {}, {}, {}, {}
    tiled = ["w_in", "w_up", "w_down"]
    for k in tiled:
        res = _adamw_update(w[k], m[k], v[k], recv[k], "adamw_" + k, transposed=k in PARTS_TRANSPOSED)
        grads[k], deltas[k], new_m[k], new_v[k] = res
    whole = [k for k in BIG if k not in tiled]
    res = _adamw_whole([w[k] for k in whole], [m[k] for k in whole], [v[k] for k in whole], [recv[k] for k in whole],
                       [k in PARTS_TRANSPOSED for k in whole])
    for d, vals in zip((grads, deltas, new_m, new_v), res):
        d.update(zip(whole, vals))

    widen = lambda k, a: jnp.tile(a, (1, N_DEV)) if k == "conv_w" else a
    res = _adamw_small(small, {k: widen(k, m[k]) for k in SMALL}, {k: widen(k, v[k]) for k in SMALL}, group_sums)
    dev = _slot(_mesh_place())
    for d, small_res in zip((grads, deltas, new_m, new_v), res):
        for k, a in small_res.items():
            if k == "conv_w":
                a = lax.dynamic_slice_in_dim(a, dev * (CONV_W // N_DEV), CONV_W // N_DEV, axis=1)
            d[k] = a

    outs = [loss, dx[None]]
    for d in (grads, deltas, new_m, new_v):
        outs += [result(k, d[k]) for k in WEIGHTS]
    return tuple(outs)
```

```python
import functools
import math

import jax
import jax.numpy as jnp
from jax import lax
from jax.experimental import pallas as pl
from jax.experimental.pallas import tpu as pltpu

f32 = jnp.float32
bf16 = jnp.bfloat16

D_MODEL = 1024
MEM_LEN = 256
GATE_COLS = 3 * D_MODEL
CONV_W = 512
SSM_W = 512
XATTN_W = 512
HEADS = 4
HEAD_DIM = 128
D_FF = 4096
IN_COLS = GATE_COLS + 3 * CONV_W + SSM_W + XATTN_W
SSM_GROUPS = 32
SSM_GROUP = 16
SSM_STATE = 64
N_STATE = SSM_GROUPS * SSM_STATE
ALPHA = 2.0 ** 0.25
LN_EPS = 1e-5
N_DEV = 8

ADAM_LR = 0.001
ADAM_B1 = 0.9
ADAM_B2 = 0.999
ADAM_EPS = 1e-08
ADAM_WD = 0.01
ADAM_STEP = 10

VMEM_LIMIT_V7X = 56 * 2 ** 20
SUBLANES = 8
LANES = 128

TOKEN_TILE = 256
SSM_BLOCK = 512
SSM_SEG = SSM_BLOCK // SUBLANES
LANE_CHUNK = 256
N_HALF = 2
HALF_W = SSM_W // N_HALF
HALF_STATE = N_STATE // N_HALF
HALF_COLS = 2 * HALF_STATE

NT = (((1,), (1,)), ((), ()))
TN = (((0,), (0,)), ((), ()))
NN = (((1,), (0,)), ((), ()))


def _dot(a, b, dims=NN):
    return lax.dot_general(a, b, dims, preferred_element_type=f32)


def _cparams(sem=None):
    return pltpu.CompilerParams(dimension_semantics=sem, vmem_limit_bytes=VMEM_LIMIT_V7X)


def _row_spec(tm, cols, rev_n=None):
    if rev_n is None:
        return pl.BlockSpec((tm, cols), lambda i: (i, 0))
    return pl.BlockSpec((tm, cols), lambda i: (rev_n - 1 - i, 0))


def _col_spec(rows, tm):
    return pl.BlockSpec((rows, tm), lambda i: (0, i))


def _const_spec(shape):
    nd = len(shape)
    return pl.BlockSpec(shape, lambda *_: (0,) * nd, pipeline_mode=pl.Buffered(1))


def _acc_spec(shape):
    nd = len(shape)
    return pl.BlockSpec(shape, lambda *_: (0,) * nd)


def _sds(shape, dtype):
    return jax.ShapeDtypeStruct(shape, dtype)


def _gelu(x):
    c = math.sqrt(2.0 / math.pi)
    return 0.5 * x * (1.0 + jnp.tanh(c * (x + 0.044715 * x * x * x)))


def _gelu_grad(x):
    c = math.sqrt(2.0 / math.pi)
    t = jnp.tanh(c * (x + 0.044715 * x * x * x))
    return 0.5 * (1.0 + t) + 0.5 * x * (1.0 - t * t) * c * (1.0 + 3.0 * 0.044715 * x * x)


def _colsum(a):
    return jnp.sum(a, axis=0, keepdims=True)


def _mesh_place():
    return lax.axis_index("x"), lax.axis_index("y"), lax.axis_index("c")


def _slot(p):
    return 4 * p[0] + 2 * p[1] + p[2]


def _other_devices(me):
    x, y, c = me
    flip = lambda v, d: 1 - v if d else v
    return [(flip(x, dx), flip(y, dy), flip(c, dc)) for dx in (0, 1) for dy in (0, 1) for dc in (0, 1)][1:]


def _all_gather(blocks, name):
    n = len(blocks)

    def body(*refs):
        ins, outs = refs[:n], refs[n:2 * n]
        send_sems, recv_sems, local_sems = refs[2 * n:]
        x, y, c = _mesh_place()
        me, sibling = (x, y, c), (x, y, 1 - c)
        chips = [(1 - x, y), (x, 1 - y), (1 - x, 1 - y)]

        def copy(a, k, block, to, src=None):
            rows = outs[a].at[_slot(block)]
            return pltpu.make_async_remote_copy(
                src_ref=rows if src is None else src, dst_ref=rows,
                send_sem=send_sems.at[a, k], recv_sem=recv_sems.at[a, k],
                device_id=to, device_id_type=pl.DeviceIdType.MESH)

        mine = [pltpu.make_async_copy(ins[a], outs[a].at[_slot(me)], local_sems.at[a]) for a in range(n)]
        for cp in mine:
            cp.start()
        first = []
        for a in range(n):
            first.append(copy(a, 0, me, sibling, src=ins[a]))
            first += [copy(a, 1 + j, me, (*chip, c), src=ins[a]) for j, chip in enumerate(chips)]
        for cp in first:
            cp.start()
        passed = []
        for a in range(n):
            for j, chip in enumerate(chips):
                copy(a, 1 + j, (*chip, c), me).wait_recv()
                fwd = copy(a, 4 + j, (*chip, c), sibling)
                fwd.start()
                passed.append(fwd)
        for a in range(n):
            copy(a, 0, sibling, me).wait_recv()
            for j, chip in enumerate(chips):
                copy(a, 4 + j, (*chip, 1 - c), me).wait_recv()
        for cp in first + passed:
            cp.wait_send()
        for cp in mine:
            cp.wait()

    any_spec = pl.BlockSpec(memory_space=pl.ANY)
    return pl.pallas_call(
        body, name=name,
        out_shape=[_sds((N_DEV,) + b.shape, b.dtype) for b in blocks],
        in_specs=[any_spec] * n, out_specs=[any_spec] * n,
        scratch_shapes=[pltpu.SemaphoreType.DMA((n, 7)), pltpu.SemaphoreType.DMA((n, 7)),
                        pltpu.SemaphoreType.DMA((n,))],
    )(*blocks)


def _side_gather_copies(ins, outs, send_sems, recv_sems, local_sems):
    me = _mesh_place()
    copies = []
    for a, (src, dst) in enumerate(zip(ins, outs)):
        copies.append(pltpu.make_async_copy(src, dst.at[_slot(me)], local_sems.at[a]))
        for k, peer in enumerate(_other_devices(me)):
            copies.append(pltpu.make_async_remote_copy(
                src_ref=src, dst_ref=dst.at[_slot(me)], send_sem=send_sems.at[a, k], recv_sem=recv_sems.at[a, k],
                device_id=peer, device_id_type=pl.DeviceIdType.MESH))
    return copies


def _side_gather_two_level(ins, outs, send_sems, recv_sems, local_sems):
    x, y, c = _mesh_place()
    me, sibling = (x, y, c), (x, y, 1 - c)
    chips = [(1 - x, y), (x, 1 - y), (1 - x, 1 - y)]

    def copy(a, k, block, to, src=None):
        rows = outs[a].at[_slot(block)]
        return pltpu.make_async_remote_copy(
            src_ref=rows if src is None else src, dst_ref=rows, send_sem=send_sems.at[a, k], recv_sem=recv_sems.at[a, k],
            device_id=to, device_id_type=pl.DeviceIdType.MESH)

    n = len(ins)
    mine = [pltpu.make_async_copy(ins[a], outs[a].at[_slot(me)], local_sems.at[a]) for a in range(n)]
    first = [copy(a, 0, me, sibling, src=ins[a]) for a in range(n)]
    first += [copy(a, 1 + j, me, (*chip, c), src=ins[a]) for a in range(n) for j, chip in enumerate(chips)]
    passed = [copy(a, 4 + j, (*chip, c), sibling) for a in range(n) for j, chip in enumerate(chips)]

    def start():
        for cp in mine + first:
            cp.start()

    def forward():
        for a in range(n):
            for j, chip in enumerate(chips):
                copy(a, 1 + j, (*chip, c), me).wait_recv()
        for cp in passed:
            cp.start()

    def finish():
        for a in range(n):
            copy(a, 0, sibling, me).wait_recv()
            for j, chip in enumerate(chips):
                copy(a, 4 + j, (*chip, 1 - c), me).wait_recv()
        for cp in first + passed:
            cp.wait_send()
        for cp in mine:
            cp.wait()

    return start, forward, finish


def _side_gather_specs(blocks):
    n = len(blocks)
    any_spec = pl.BlockSpec(memory_space=pl.ANY)
    return ([any_spec] * n, [_sds((N_DEV,) + b.shape, b.dtype) for b in blocks],
            [pltpu.SemaphoreType.DMA((n, N_DEV - 1)), pltpu.SemaphoreType.DMA((n, N_DEV - 1)),
             pltpu.SemaphoreType.DMA((n,))])


def _kv_proj(mem, w_kv):
    def body(mem_ref, w_ref, kv_ref, kt_ref, memb_ref):
        mb = mem_ref[...].astype(bf16)
        memb_ref[...] = mb
        kv = _dot(mb, w_ref[...]).astype(bf16)
        kv_ref[...] = kv
        kt_ref[...] = kv[:, :XATTN_W].T

    return pl.pallas_call(
        body, name="kv_proj",
        out_shape=[_sds((MEM_LEN, 2 * XATTN_W), bf16), _sds((XATTN_W, MEM_LEN), bf16), _sds((MEM_LEN, D_MODEL), bf16)],
        compiler_params=_cparams(),
    )(mem, w_kv)


def _attention_probs(qb, kv_ref, h):
    kh = kv_ref[:, h * HEAD_DIM:(h + 1) * HEAD_DIM]
    s = _dot(qb[:, h * HEAD_DIM:(h + 1) * HEAD_DIM], kh, NT) * (HEAD_DIM ** -0.5)
    e = jnp.exp(s - jnp.max(s, axis=-1, keepdims=True))
    return e / jnp.sum(e, axis=-1, keepdims=True)


def _in_proj(x, w_in_t, b_gate, conv_w, kv, side_blocks):
    s_len = x.shape[0]
    tm = 2 * TOKEN_TILE
    n = s_len // tm
    ns = len(side_blocks)
    side_in_specs, side_shapes, side_sems = _side_gather_specs(side_blocks)

    def body(*refs):
        (x_ref, win_ref, bg_ref, cw_ref, kv_ref) = refs[:5]
        side_ins = refs[5:5 + ns]
        (xbt_ref, g_ref, cin_ref, u_ref, q_ref, ain_ref, o_ref, aint_ref, ot_ref) = refs[5 + ns:14 + ns]
        side_outs = refs[14 + ns:14 + 2 * ns]
        zs_ref = refs[14 + 2 * ns]
        side_start, side_forward, side_finish = _side_gather_two_level(side_ins, side_outs, *refs[15 + 2 * ns:])
        i = pl.program_id(0)
        pl.when(i == 0)(side_start)
        pl.when(i == (3 * n) // 4)(side_forward)

        xb = x_ref[...].astype(bf16)
        xbt_ref[...] = xb.T
        proj = _dot(xb, win_ref[...], NT)
        g_ref[...] = jax.nn.sigmoid(proj[:, :GATE_COLS] + bg_ref[...]).astype(bf16)
        cin = proj[:, GATE_COLS:GATE_COLS + 3 * CONV_W]
        cin_ref[...] = cin
        u_ref[...] = proj[:, GATE_COLS + 3 * CONV_W:GATE_COLS + 3 * CONV_W + SSM_W]
        qb = proj[:, IN_COLS - XATTN_W:].astype(bf16)
        q_ref[...] = qb

        cb, cc, ch = cin[:, :CONV_W], cin[:, CONV_W:2 * CONV_W], cin[:, 2 * CONV_W:]
        z = cc * ch

        @pl.when(i == 0)
        def _():
            zs_ref[0:8, :] = jnp.zeros((8, CONV_W), f32)

        zs_ref[8:8 + tm, :] = z
        z1 = zs_ref[pl.ds(7, tm), :]
        z2 = zs_ref[pl.ds(6, tm), :]
        cw = cw_ref[...]
        cz = cw[0:1] * z2 + cw[1:2] * z1 + cw[2:3] * z
        zs_ref[0:8, :] = zs_ref[tm:tm + 8, :]
        ain = (cb * cz).astype(bf16)
        ain_ref[...] = ain
        aint_ref[...] = ain.T

        probs = [_attention_probs(qb, kv_ref, h) for h in range(HEADS)]
        outs = [_dot(probs[h].astype(bf16), kv_ref[:, XATTN_W + h * HEAD_DIM:XATTN_W + (h + 1) * HEAD_DIM])
                for h in range(HEADS)]
        ob = jnp.concatenate(outs, axis=1).astype(bf16)
        o_ref[...] = ob
        ot_ref[...] = ob.T

        pl.when(i == n - 1)(side_finish)

    row_cols = [(GATE_COLS, bf16), (3 * CONV_W, f32), (SSM_W, f32), (XATTN_W, bf16), (CONV_W, bf16), (XATTN_W, bf16)]
    t_rows = [D_MODEL, CONV_W, XATTN_W]
    outs = pl.pallas_call(
        body, name="in_proj", grid=(n,),
        in_specs=[_row_spec(tm, D_MODEL), _const_spec((IN_COLS, D_MODEL)), _const_spec((1, GATE_COLS)),
                  _const_spec((3, CONV_W)), _const_spec((MEM_LEN, 2 * XATTN_W))] + side_in_specs,
        out_specs=([_col_spec(t_rows[0], tm)] + [_row_spec(tm, c) for c, _ in row_cols]
                   + [_col_spec(t_rows[1], tm), _col_spec(t_rows[2], tm)] + side_in_specs),
        out_shape=([_sds((t_rows[0], s_len), bf16)] + [_sds((s_len, c), dt) for c, dt in row_cols]
                   + [_sds((t_rows[1], s_len), bf16), _sds((t_rows[2], s_len), bf16)] + side_shapes),
        scratch_shapes=[pltpu.VMEM((tm + 8, CONV_W), f32)] + side_sems,
        compiler_params=_cparams(("arbitrary",)),
    )(x, w_in_t, b_gate, conv_w, kv, *side_blocks)
    return outs[:9], outs[9:]


def _state_cols(chunk, width=LANE_CHUNK):
    half, off = divmod(chunk * width, HALF_STATE)
    lo = half * HALF_COLS + off
    return slice(lo, lo + width), slice(lo + HALF_STATE, lo + HALF_STATE + width)


def _half_cols(half):
    lo = half * HALF_COLS
    return slice(lo, lo + HALF_STATE), slice(lo + HALF_STATE, lo + HALF_COLS)


def _rows_to_segments(src_ref, stage_ref, dst_ref):
    nc = SSM_W // LANES
    for c in range(nc):
        stage_ref[c] = src_ref[:, c * LANES:(c + 1) * LANES]
    for c in range(nc):
        for k in range(SSM_SEG):
            dst_ref[k * SUBLANES:(k + 1) * SUBLANES, c * LANES:(c + 1) * LANES] = (
                stage_ref[c, pl.ds(k, SUBLANES, stride=SSM_SEG), :])


def _rows_from_segments(src_ref, stage_ref, dst_ref):
    nc = SSM_W // LANES
    for c in range(nc):
        for k in range(SSM_SEG):
            stage_ref[c, pl.ds(k, SUBLANES, stride=SSM_SEG), :] = (
                src_ref[k * SUBLANES:(k + 1) * SUBLANES, c * LANES:(c + 1) * LANES])
    for c in range(nc):
        dst_ref[:, c * LANES:(c + 1) * LANES] = stage_ref[c]


def _ssm_scan(s_ref, pw_ref, init_ref, reverse, unroll, width=LANE_CHUNK):
    for chunk in range(N_STATE // width):
        re, im = _state_cols(chunk, width)
        ar = jnp.broadcast_to(pw_ref[0:1, re], (SUBLANES, width))
        ai = jnp.broadcast_to(pw_ref[0:1, im], (SUBLANES, width))
        if reverse:
            ai = -ai

        def step(j, carry, re=re, im=im, ar=ar, ai=ai):
            sr, si = carry
            k = (SSM_SEG - 1 - j) if reverse else j
            r0 = pl.multiple_of(k * SUBLANES, SUBLANES)
            nr = ar * sr - ai * si + s_ref[pl.ds(r0, SUBLANES), re]
            ni = ar * si + ai * sr + s_ref[pl.ds(r0, SUBLANES), im]
            s_ref[pl.ds(r0, SUBLANES), re] = nr
            s_ref[pl.ds(r0, SUBLANES), im] = ni
            return nr, ni

        if init_ref is None:
            init = (jnp.zeros((SUBLANES, width), f32),) * 2
        else:
            init = (init_ref[:, re], init_ref[:, im])
        lax.fori_loop(0, SSM_SEG, step, init, unroll=unroll)


def _ssm_add_carry(s_ref, pw_ref, cm_ref, reverse):
    for chunk in range(N_STATE // LANE_CHUNK):
        re, im = _state_cols(chunk)
        cr, ci = cm_ref[:, re], cm_ref[:, im]
        for k in range(SSM_SEG):
            pk = (SSM_SEG - 1 - k) if reverse else k
            pr = pw_ref[pk:pk + 1, re]
            pi = pw_ref[pk:pk + 1, im]
            if reverse:
                pi = -pi
            rows = slice(k * SUBLANES, (k + 1) * SUBLANES)
            s_ref[rows, re] = s_ref[rows, re] + (pr * cr - pi * ci)
            s_ref[rows, im] = s_ref[rows, im] + (pr * ci + pi * cr)


def _ssm_carries(first_row, s_ref, pw_ref, carry_ref, cm_ref, reverse):
    order = range(SUBLANES - 1, -1, -1) if reverse else range(SUBLANES)
    for half in range(N_HALF):
        re, im = _half_cols(half)
        a_r, a_i = pw_ref[SSM_SEG - 1:SSM_SEG, re], pw_ref[SSM_SEG - 1:SSM_SEG, im]
        if reverse:
            a_i = -a_i
        cr, ci = carry_ref[0:1, re], carry_ref[0:1, im]
        for seg in order:
            cm_ref[seg:seg + 1, re] = cr
            cm_ref[seg:seg + 1, im] = ci
            er = s_ref[first_row + seg:first_row + seg + 1, re]
            ei = s_ref[first_row + seg:first_row + seg + 1, im]
            cr, ci = a_r * cr - a_i * ci + er, a_r * ci + a_i * cr + ei
        carry_ref[0:1, re] = cr
        carry_ref[0:1, im] = ci


def _ssm_fwd(u, b_half, c_half, pw, d_skip, side_blocks):
    s_len = u.shape[0]
    tb = SSM_BLOCK
    n = s_len // tb
    ns = len(side_blocks)
    side_in_specs, side_shapes, side_sems = _side_gather_specs(side_blocks)

    def body(*refs):
        u_ref, b_ref, c_ref, pw_ref, d_ref = refs[:5]
        side_ins = refs[5:5 + ns]
        y_ref, cm_ref = refs[5 + ns:7 + ns]
        side_outs = refs[7 + ns:7 + 2 * ns]
        s_ref, carry_ref, up_ref, yp_ref, stage_ref = refs[7 + 2 * ns:12 + 2 * ns]
        side = _side_gather_copies(side_ins, side_outs, *refs[12 + 2 * ns:])
        i = pl.program_id(0)

        @pl.when(i == 0)
        def _():
            carry_ref[...] = jnp.zeros_like(carry_ref)
            for cp in side:
                cp.start()

        _rows_to_segments(u_ref, stage_ref, up_ref)
        u = up_ref[...]
        ub = u.astype(bf16)
        for half in range(N_HALF):
            s_ref[:, half * HALF_COLS:(half + 1) * HALF_COLS] = _dot(ub[:, half * HALF_W:(half + 1) * HALF_W], b_ref[half])
        _ssm_scan(s_ref, pw_ref, None, reverse=False, unroll=4, width=2 * LANE_CHUNK)
        _ssm_carries(tb - SUBLANES, s_ref, pw_ref, carry_ref, cm_ref, reverse=False)
        _ssm_add_carry(s_ref, pw_ref, cm_ref, reverse=False)
        for half in range(N_HALF):
            cols = slice(half * HALF_W, (half + 1) * HALF_W)
            sb = s_ref[:, half * HALF_COLS:(half + 1) * HALF_COLS].astype(bf16)
            yp_ref[:, cols] = _dot(sb, c_ref[half]) + d_ref[:, cols] * u[:, cols]
        _rows_from_segments(yp_ref, stage_ref, y_ref)

        @pl.when(i == n - 1)
        def _():
            for cp in side:
                cp.wait()

    outs = pl.pallas_call(
        body, name="ssm_fwd", grid=(n,),
        in_specs=[_row_spec(tb, SSM_W), _const_spec((N_HALF, HALF_W, HALF_COLS)), _const_spec((N_HALF, HALF_COLS, HALF_W)),
                  _const_spec((SSM_SEG, 2 * N_STATE)), _const_spec((1, SSM_W))] + side_in_specs,
        out_specs=[_row_spec(tb, SSM_W), _row_spec(SUBLANES, 2 * N_STATE)] + side_in_specs,
        out_shape=[_sds((s_len, SSM_W), f32), _sds((n * SUBLANES, 2 * N_STATE), f32)] + side_shapes,
        scratch_shapes=[pltpu.VMEM((tb, 2 * N_STATE), f32), pltpu.VMEM((SUBLANES, 2 * N_STATE), f32),
                        pltpu.VMEM((tb, SSM_W), f32), pltpu.VMEM((tb, SSM_W), f32),
                        pltpu.VMEM((SSM_W // LANES, tb, LANES), f32)] + side_sems,
        compiler_params=_cparams(("arbitrary",)),
    )(u, b_half, c_half, pw, d_skip, *side_blocks)
    return outs[0], outs[1], outs[2:]


def _layer_norm_fwd(r, g, b):
    mu = jnp.mean(r, axis=-1, keepdims=True)
    var = jnp.mean(jnp.square(r - mu), axis=-1, keepdims=True)
    rstd = lax.rsqrt(var + LN_EPS)
    xhat = (r - mu) * rstd
    return xhat, rstd, xhat * g + b


def _layer_norm_bwd(dy, xhat, rstd, g):
    dxh = dy * g
    m1 = jnp.mean(dxh, axis=-1, keepdims=True)
    m2 = jnp.mean(dxh * xhat, axis=-1, keepdims=True)
    return rstd * (dxh - m1 - xhat * m2)


def _branch_outputs(ys_ref, ain_ref, o_ref, wglu_ref, wco_ref, wxo_ref):
    ysb = _gelu(ys_ref[...]).astype(bf16)
    glu = _dot(ysb, wglu_ref[...], NT)
    ga, sb = glu[:, :D_MODEL], jax.nn.sigmoid(glu[:, D_MODEL:])
    ya = _dot(ain_ref[...], wco_ref[...], NT)
    yc = _dot(o_ref[...], wxo_ref[...], NT)
    return ysb, ga, sb, ya, ga * sb, yc


def _mid_fwd(y_ssm, g, ain, ob, x, w_glu_t, w_co_t, w_xo_t, w_out, ln1_g, ln1_b):
    s_len = x.shape[0]
    tm = 2 * TOKEN_TILE
    n = s_len // tm

    def body(ys_ref, g_ref, ain_ref, o_ref, x_ref, wglu_ref, wco_ref, wxo_ref, wout_ref, lg_ref, lb_ref,
             ysbt_ref, mb_ref, xhat_ref, rstd_ref):
        ysb, _, _, ya, yb, yc = _branch_outputs(ys_ref, ain_ref, o_ref, wglu_ref, wco_ref, wxo_ref)
        ysbt_ref[...] = ysb.T
        gt = g_ref[...].astype(f32)
        merged = gt[:, :D_MODEL] * ya + gt[:, D_MODEL:2 * D_MODEL] * yb + gt[:, 2 * D_MODEL:] * yc
        mb = merged.astype(bf16)
        mb_ref[...] = mb
        r1 = ALPHA * x_ref[...] + _dot(mb, wout_ref[...])
        xhat, rstd, _ = _layer_norm_fwd(r1, lg_ref[...], lb_ref[...])
        xhat_ref[...] = xhat
        rstd_ref[...] = rstd

    row_cols = [(D_MODEL, bf16), (D_MODEL, f32), (1, f32)]
    return pl.pallas_call(
        body, name="mid_fwd", grid=(n,),
        in_specs=[_row_spec(tm, SSM_W), _row_spec(tm, GATE_COLS), _row_spec(tm, CONV_W), _row_spec(tm, XATTN_W),
                  _row_spec(tm, D_MODEL), _const_spec((2 * D_MODEL, SSM_W)), _const_spec((D_MODEL, CONV_W)),
                  _const_spec((D_MODEL, XATTN_W)), _const_spec((D_MODEL, D_MODEL)),
                  _const_spec((1, D_MODEL)), _const_spec((1, D_MODEL))],
        out_specs=[_col_spec(SSM_W, tm)] + [_row_spec(tm, c) for c, _ in row_cols],
        out_shape=[_sds((SSM_W, s_len), bf16)] + [_sds((s_len, c), dt) for c, dt in row_cols],
        compiler_params=_cparams(("parallel",)),
    )(y_ssm, g, ain, ob, x, w_glu_t, w_co_t, w_xo_t, w_out, ln1_g, ln1_b)


def _mlp_fwd_bwd(xhat1, tgt, ln1_g, ln1_b, w_up_t, b_up, w_down, b_down, ln2_g, ln2_b):
    s_len = xhat1.shape[0]
    tm = TOKEN_TILE
    n = s_len // tm
    fc = 1024
    nfc = D_FF // fc

    def body(xh_ref, t_ref, l1g_ref, l1b_ref, wup_ref, bup_ref, wdn_ref, bdn_ref, l2g_ref, l2b_ref,
             x1bt_ref, hdn_ref, dr2bt_ref, dpre_ref, dx1_ref,
             loss_ref, dl2g_ref, dl2b_ref, dbdn_ref, dbup_ref, rl_ref):
        i = pl.program_id(0)

        @pl.when(i == 0)
        def _():
            loss_ref[...] = jnp.zeros_like(loss_ref)
            dl2g_ref[...] = jnp.zeros_like(dl2g_ref)
            dl2b_ref[...] = jnp.zeros_like(dl2b_ref)
            dbdn_ref[...] = jnp.zeros_like(dbdn_ref)
            dbup_ref[...] = jnp.zeros_like(dbup_ref)

        x1 = xh_ref[...] * l1g_ref[...] + l1b_ref[...]
        x1b = x1.astype(bf16)
        x1bt_ref[...] = x1b.T
        chunks = [slice(c * fc, (c + 1) * fc) for c in range(nfc)]
        pres = [_dot(x1b, wup_ref[cols, :], NT) for cols in chunks]
        hbs = []
        for cols, pre in zip(chunks, pres):
            rl = jnp.maximum(pre + bup_ref[:, cols], 0.0)
            rl_ref[:, cols] = rl
            hb = (rl * rl).astype(bf16)
            hdn_ref[:, cols] = hb
            hbs.append(hb)
        acc = _dot(hbs[0], wdn_ref[chunks[0], :])
        for cols, hb in zip(chunks[1:], hbs[1:]):
            acc = acc + _dot(hb, wdn_ref[cols, :])
        r2 = ALPHA * x1 + acc + bdn_ref[...]
        xhat2, rstd2, y = _layer_norm_fwd(r2, l2g_ref[...], l2b_ref[...])
        err = y - t_ref[...]
        loss_ref[...] += jnp.sum(jnp.sum(err * err, axis=1, keepdims=True), axis=0, keepdims=True) * (0.5 / D_MODEL)
        dy = err * (1.0 / D_MODEL)
        dl2g_ref[...] += _colsum(dy * xhat2)
        dl2b_ref[...] += _colsum(dy)
        dr2 = _layer_norm_bwd(dy, xhat2, rstd2, l2g_ref[...])
        dbdn_ref[...] += _colsum(dr2)
        dr2b = dr2.astype(bf16)
        dr2bt_ref[...] = dr2b.T
        dhs = [_dot(dr2b, wdn_ref[cols, :], NT) for cols in chunks]
        dpbs = []
        for cols, dh in zip(chunks, dhs):
            dpre = dh * (2.0 * rl_ref[:, cols])
            dbup_ref[:, cols] += _colsum(dpre)
            dpb = dpre.astype(bf16)
            dpre_ref[:, cols] = dpb
            dpbs.append(dpb)
        dacc = _dot(dpbs[0], wup_ref[chunks[0], :])
        for cols, dpb in zip(chunks[1:], dpbs[1:]):
            dacc = dacc + _dot(dpb, wup_ref[cols, :])
        dx1_ref[...] = ALPHA * dr2 + dacc

    acc_shapes = [(1, LANES), (1, D_MODEL), (1, D_MODEL), (1, D_MODEL), (1, D_FF)]
    return pl.pallas_call(
        body, name="mlp_fwd_bwd", grid=(n,),
        in_specs=[_row_spec(tm, D_MODEL), _row_spec(tm, D_MODEL), _const_spec((1, D_MODEL)), _const_spec((1, D_MODEL)),
                  _const_spec((D_FF, D_MODEL)), _const_spec((1, D_FF)), _const_spec((D_FF, D_MODEL)),
                  _const_spec((1, D_MODEL)), _const_spec((1, D_MODEL)), _const_spec((1, D_MODEL))],
        out_specs=([_col_spec(D_MODEL, tm), _row_spec(tm, D_FF), _col_spec(D_MODEL, tm), _row_spec(tm, D_FF),
                    _row_spec(tm, D_MODEL)] + [_acc_spec(s) for s in acc_shapes]),
        out_shape=([_sds((D_MODEL, s_len), bf16), _sds((s_len, D_FF), bf16), _sds((D_MODEL, s_len), bf16),
                    _sds((s_len, D_FF), bf16), _sds((s_len, D_MODEL), f32)] + [_sds(s, f32) for s in acc_shapes]),
        scratch_shapes=[pltpu.VMEM((tm, D_FF), f32)],
        compiler_params=_cparams(("arbitrary",)),
    )(xhat1, tgt, ln1_g, ln1_b, w_up_t, b_up, w_down, b_down, ln2_g, ln2_b)


def _mid_bwd(dx1, xhat1, rstd1, g, ain, ob, y_ssm, ln1_g, w_out, w_glu_t, w_co_t, w_xo_t):
    s_len = dx1.shape[0]
    tm = TOKEN_TILE
    n = s_len // tm

    def body(dx1_ref, xh_ref, rs_ref, g_ref, ain_ref, o_ref, ys_ref, lg_ref, wout_ref, wglu_ref, wco_ref, wxo_ref,
             dxp_ref, dr1bt_ref, dgp_ref, dya_ref, dyc_ref, dglu_ref, dyssm_ref,
             dl1g_ref, dl1b_ref, dbg_ref):
        i = pl.program_id(0)

        @pl.when(i == 0)
        def _():
            dl1g_ref[...] = jnp.zeros_like(dl1g_ref)
            dl1b_ref[...] = jnp.zeros_like(dl1b_ref)
            dbg_ref[...] = jnp.zeros_like(dbg_ref)

        dx1 = dx1_ref[...]
        xhat = xh_ref[...]
        dl1g_ref[...] += _colsum(dx1 * xhat)
        dl1b_ref[...] += _colsum(dx1)
        dr1 = _layer_norm_bwd(dx1, xhat, rs_ref[...], lg_ref[...])
        dxp_ref[...] = ALPHA * dr1
        dr1b = dr1.astype(bf16)
        dr1bt_ref[...] = dr1b.T
        dm = _dot(dr1b, wout_ref[...], NT)

        _, ga, sb, ya, yb, yc = _branch_outputs(ys_ref, ain_ref, o_ref, wglu_ref, wco_ref, wxo_ref)
        gt = g_ref[...].astype(f32)
        branch = (ya, yb, yc)
        for j in range(3):
            cols = slice(j * D_MODEL, (j + 1) * D_MODEL)
            gj = gt[:, cols]
            dgp = dm * branch[j] * gj * (1.0 - gj)
            dbg_ref[:, cols] += _colsum(dgp)
            dgp_ref[:, cols] = dgp.astype(bf16)
        dya_ref[...] = (dm * gt[:, :D_MODEL]).astype(bf16)
        dyc_ref[...] = (dm * gt[:, 2 * D_MODEL:]).astype(bf16)
        dyb = dm * gt[:, D_MODEL:2 * D_MODEL]
        dga = (dyb * sb).astype(bf16)
        dgb = (dyb * ga * sb * (1.0 - sb)).astype(bf16)
        dglu_ref[:, :D_MODEL] = dga
        dglu_ref[:, D_MODEL:] = dgb
        dys = _dot(dga, wglu_ref[:D_MODEL, :]) + _dot(dgb, wglu_ref[D_MODEL:, :])
        dyssm_ref[...] = dys * _gelu_grad(ys_ref[...])

    row_cols = [(GATE_COLS, bf16), (D_MODEL, bf16), (D_MODEL, bf16), (2 * D_MODEL, bf16), (SSM_W, f32)]
    acc_shapes = [(1, D_MODEL), (1, D_MODEL), (1, GATE_COLS)]
    return pl.pallas_call(
        body, name="mid_bwd", grid=(n,),
        in_specs=[_row_spec(tm, D_MODEL), _row_spec(tm, D_MODEL), _row_spec(tm, 1), _row_spec(tm, GATE_COLS),
                  _row_spec(tm, CONV_W), _row_spec(tm, XATTN_W), _row_spec(tm, SSM_W),
                  _const_spec((1, D_MODEL)), _const_spec((D_MODEL, D_MODEL)), _const_spec((2 * D_MODEL, SSM_W)),
                  _const_spec((D_MODEL, CONV_W)), _const_spec((D_MODEL, XATTN_W))],
        out_specs=([_row_spec(tm, D_MODEL), _col_spec(D_MODEL, tm)] + [_row_spec(tm, c) for c, _ in row_cols]
                   + [_acc_spec(s) for s in acc_shapes]),
        out_shape=([_sds((s_len, D_MODEL), f32), _sds((D_MODEL, s_len), bf16)]
                   + [_sds((s_len, c), dt) for c, dt in row_cols] + [_sds(s, f32) for s in acc_shapes]),
        compiler_params=_cparams(("arbitrary",)),
    )(dx1, xhat1, rstd1, g, ain, ob, y_ssm, ln1_g, w_out, w_glu_t, w_co_t, w_xo_t)


def _ssm_bwd(u, dy, cm_all, b_half, c_half, pw, d_skip):
    s_len = u.shape[0]
    tb = SSM_BLOCK
    n = s_len // tb

    def body(u_ref, dy_ref, cm_ref, b_ref, c_ref, pw_ref, d_ref,
             du_ref, db_hbm, dc_hbm, da_ref, dd_ref,
             s_ref, g_ref, gcarry_ref, gcm_ref, db_ref, dc_ref, up_ref, dyp_ref, dup_ref, stage_ref):
        i = pl.program_id(0)

        @pl.when(i == 0)
        def _():
            gcarry_ref[...] = jnp.zeros_like(gcarry_ref)
            db_ref[...] = jnp.zeros_like(db_ref)
            dc_ref[...] = jnp.zeros_like(dc_ref)
            da_ref[...] = jnp.zeros_like(da_ref)
            dd_ref[...] = jnp.zeros_like(dd_ref)

        _rows_to_segments(u_ref, stage_ref, up_ref)
        _rows_to_segments(dy_ref, stage_ref, dyp_ref)
        u = up_ref[...]
        ub = u.astype(bf16)
        dy = dyp_ref[...]
        dyb = dy.astype(bf16)
        dd_ref[...] += _colsum(dy * u)

        for half in range(N_HALF):
            s_ref[:, half * HALF_COLS:(half + 1) * HALF_COLS] = _dot(ub[:, half * HALF_W:(half + 1) * HALF_W], b_ref[half])
        _ssm_scan(s_ref, pw_ref, cm_ref, reverse=False, unroll=True)

        for half in range(N_HALF):
            g_ref[:, half * HALF_COLS:(half + 1) * HALF_COLS] = _dot(dyb[:, half * HALF_W:(half + 1) * HALF_W], c_ref[half], NT)
        _ssm_scan(g_ref, pw_ref, None, reverse=True, unroll=True)
        _ssm_carries(0, g_ref, pw_ref, gcarry_ref, gcm_ref, reverse=True)
        _ssm_add_carry(g_ref, pw_ref, gcm_ref, reverse=True)

        for half in range(N_HALF):
            cols = slice(half * HALF_W, (half + 1) * HALF_W)
            scols = slice(half * HALF_COLS, (half + 1) * HALF_COLS)
            gb = g_ref[:, scols].astype(bf16)
            dup_ref[:, cols] = _dot(gb, b_ref[half], NT) + d_ref[:, cols] * dy[:, cols]
            db_ref[half] += _dot(ub[:, cols], gb, TN)
            dc_ref[half] += _dot(s_ref[:, scols].astype(bf16), dyb[:, cols], TN)
        _rows_from_segments(dup_ref, stage_ref, du_ref)

        for chunk in range(N_STATE // LANE_CHUNK):
            re, im = _state_cols(chunk)
            acc_r = da_ref[:, re]
            acc_i = da_ref[:, im]
            for k in range(SSM_SEG):
                rows = slice(k * SUBLANES, (k + 1) * SUBLANES)
                if k == 0:
                    pr, pi = cm_ref[:, re], cm_ref[:, im]
                else:
                    prev = slice((k - 1) * SUBLANES, k * SUBLANES)
                    pr, pi = s_ref[prev, re], s_ref[prev, im]
                gr, gi = g_ref[rows, re], g_ref[rows, im]
                acc_r = acc_r + (gr * pr + gi * pi)
                acc_i = acc_i + (gi * pr - gr * pi)
            da_ref[:, re] = acc_r
            da_ref[:, im] = acc_i

        @pl.when(i == n - 1)
        def _():
            pltpu.sync_copy(db_ref, db_hbm)
            pltpu.sync_copy(dc_ref, dc_hbm)

    rev = functools.partial(_row_spec, rev_n=n)
    any_spec = pl.BlockSpec(memory_space=pl.ANY)
    state_rows = pltpu.VMEM((tb, 2 * N_STATE), f32)
    seg_rows = pltpu.VMEM((SUBLANES, 2 * N_STATE), f32)
    tok_rows = pltpu.VMEM((tb, SSM_W), f32)
    return pl.pallas_call(
        body, name="ssm_bwd", grid=(n,),
        in_specs=[rev(tb, SSM_W), rev(tb, SSM_W), rev(SUBLANES, 2 * N_STATE),
                  _const_spec((N_HALF, HALF_W, HALF_COLS)), _const_spec((N_HALF, HALF_COLS, HALF_W)),
                  _const_spec((SSM_SEG, 2 * N_STATE)), _const_spec((1, SSM_W))],
        out_specs=[rev(tb, SSM_W), any_spec, any_spec, _acc_spec((SUBLANES, 2 * N_STATE)), _acc_spec((1, SSM_W))],
        out_shape=[_sds((s_len, SSM_W), f32), _sds((N_HALF, HALF_W, HALF_COLS), f32),
                   _sds((N_HALF, HALF_COLS, HALF_W), f32), _sds((SUBLANES, 2 * N_STATE), f32), _sds((1, SSM_W), f32)],
        scratch_shapes=[state_rows, state_rows, seg_rows, seg_rows,
                        pltpu.VMEM((N_HALF, HALF_W, HALF_COLS), f32), pltpu.VMEM((N_HALF, HALF_COLS, HALF_W), f32),
                        tok_rows, tok_rows, tok_rows, pltpu.VMEM((SSM_W // LANES, tb, LANES), f32)],
        compiler_params=_cparams(("arbitrary",)),
    )(u, dy, cm_all, b_half, c_half, pw, d_skip)


def _branch_bwd(dya, dyc, cin, q, kv, k_t, conv_w, w_co_t, w_xo_t, side_blocks):
    s_len = dya.shape[0]
    tm = 2 * TOKEN_TILE
    n = s_len // tm
    halo_blocks = tm // 8
    ns = len(side_blocks)
    conv_tile = _sds((8, CONV_W), f32)
    side_in_specs, side_shapes, side_sems = _side_gather_specs(list(side_blocks) + [conv_tile])

    def body(*refs):
        (dya_ref, dyc_ref, cin_ref, cprev_ref, q_ref, kv_ref, cw_ref, wco_ref, wxo_ref, kt_ref) = refs[:10]
        side_ins = refs[10:10 + ns]
        dconv_ref, dq_ref, dkv_ref = refs[10 + ns:13 + ns]
        side_outs = refs[13 + ns:14 + 2 * ns]
        zs_ref, dczs_ref, dcw_ref = refs[14 + 2 * ns:17 + 2 * ns]
        copies = _side_gather_copies(list(side_ins) + [dcw_ref], side_outs, *refs[17 + 2 * ns:])
        side, conv_side = copies[:ns * N_DEV], copies[ns * N_DEV:]
        i = pl.program_id(0)
        tile = n - 1 - i

        @pl.when(i == 0)
        def _():
            dcw_ref[...] = jnp.zeros_like(dcw_ref)
            dkv_ref[...] = jnp.zeros_like(dkv_ref)
            dczs_ref[tm:tm + 8, :] = jnp.zeros((8, CONV_W), f32)
            for cp in side:
                cp.start()

        cin = cin_ref[...]
        cb, cc, ch = cin[:, :CONV_W], cin[:, CONV_W:2 * CONV_W], cin[:, 2 * CONV_W:]
        z = cc * ch
        cprev = cprev_ref[...]
        zprev = cprev[:, CONV_W:2 * CONV_W] * cprev[:, 2 * CONV_W:]
        zs_ref[0:8, :] = jnp.where(tile == 0, 0.0, zprev)
        zs_ref[8:8 + tm, :] = z
        z1 = zs_ref[pl.ds(7, tm), :]
        z2 = zs_ref[pl.ds(6, tm), :]
        cw = cw_ref[...]
        cz = cw[0:1] * z2 + cw[1:2] * z1 + cw[2:3] * z

        dain = _dot(dya_ref[...], wco_ref[...])
        dcb = dain * cz
        dcz = dain * cb
        dczs_ref[0:tm, :] = dcz
        dcz1 = dczs_ref[pl.ds(1, tm), :]
        dcz2 = dczs_ref[pl.ds(2, tm), :]
        dz = cw[2:3] * dcz + cw[1:2] * dcz1 + cw[0:1] * dcz2
        dczs_ref[tm:tm + 8, :] = dczs_ref[0:8, :]
        dcw_ref[0:1, :] += _colsum(dcz * z2)
        dcw_ref[1:2, :] += _colsum(dcz * z1)
        dcw_ref[2:3, :] += _colsum(dcz * z)
        dconv_ref[:, :CONV_W] = dcb.astype(bf16)
        dconv_ref[:, CONV_W:2 * CONV_W] = (dz * ch).astype(bf16)
        dconv_ref[:, 2 * CONV_W:] = (dz * cc).astype(bf16)

        qb = q_ref[...]
        dob = _dot(dyc_ref[...], wxo_ref[...]).astype(bf16)
        kv = kv_ref[...]
        heads = range(HEADS)
        hcs = [slice(h * HEAD_DIM, (h + 1) * HEAD_DIM) for h in heads]
        vcs = [slice(XATTN_W + h * HEAD_DIM, XATTN_W + (h + 1) * HEAD_DIM) for h in heads]
        s_t = [_dot(kv[:, hcs[h]], qb[:, hcs[h]], NT) * (HEAD_DIM ** -0.5) for h in heads]
        dp_t = [_dot(kv[:, vcs[h]], dob[:, hcs[h]], NT) for h in heads]
        e_t = [jnp.exp(s_t[h] - jnp.max(s_t[h], axis=0, keepdims=True)) for h in heads]
        p_t = [e_t[h] / jnp.sum(e_t[h], axis=0, keepdims=True) for h in heads]
        dv = [_dot(p_t[h].astype(bf16), dob[:, hcs[h]]) for h in heads]
        ds_t = [(p_t[h] * (dp_t[h] - jnp.sum(dp_t[h] * p_t[h], axis=0, keepdims=True)) * (HEAD_DIM ** -0.5)).astype(bf16)
                for h in heads]
        dk = [_dot(ds_t[h], qb[:, hcs[h]]) for h in heads]
        dq_t = [_dot(kt_ref[hcs[h], :], ds_t[h]) for h in heads]
        dq_ref[...] = jnp.concatenate(dq_t, axis=0).T.astype(bf16)
        dkv_ref[...] += jnp.concatenate(dk + dv, axis=1)

        @pl.when(i == n - 1)
        def _():
            for cp in conv_side:
                cp.start()
            for cp in side + conv_side:
                cp.wait()

    rev = functools.partial(_row_spec, rev_n=n)
    prev_spec = pl.BlockSpec((8, 3 * CONV_W), lambda i: (jnp.maximum((n - 1 - i) * halo_blocks - 1, 0), 0))
    outs = pl.pallas_call(
        body, name="branch_bwd", grid=(n,),
        in_specs=[rev(tm, D_MODEL), rev(tm, D_MODEL), rev(tm, 3 * CONV_W), prev_spec, rev(tm, XATTN_W),
                  _const_spec((MEM_LEN, 2 * XATTN_W)), _const_spec((3, CONV_W)), _const_spec((D_MODEL, CONV_W)),
                  _const_spec((D_MODEL, XATTN_W)), _const_spec((XATTN_W, MEM_LEN))] + side_in_specs[:ns],
        out_specs=[rev(tm, 3 * CONV_W), rev(tm, XATTN_W), _acc_spec((MEM_LEN, 2 * XATTN_W))] + side_in_specs,
        out_shape=[_sds((s_len, 3 * CONV_W), bf16), _sds((s_len, XATTN_W), bf16),
                   _sds((MEM_LEN, 2 * XATTN_W), f32)] + side_shapes,
        scratch_shapes=[pltpu.VMEM((tm + 8, CONV_W), f32), pltpu.VMEM((tm + 8, CONV_W), f32),
                        pltpu.VMEM((8, CONV_W), f32)] + side_sems,
        compiler_params=_cparams(("arbitrary",)),
    )(dya, dyc, cin, cin, q, kv, conv_w, w_co_t, w_xo_t, k_t, *side_blocks)
    return outs[0], outs[1], outs[2], outs[3:]


def _in_proj_bwd(dgp, dconv, du, dq, dxp, w_in_t):
    s_len = dgp.shape[0]
    tm = 2 * TOKEN_TILE
    n = s_len // tm

    def body(dgp_ref, dconv_ref, du_ref, dq_ref, dxp_ref, win_ref, dx_ref, dproj_ref):
        dproj = jnp.concatenate([dgp_ref[...], dconv_ref[...], du_ref[...].astype(bf16), dq_ref[...]], axis=1)
        dproj_ref[...] = dproj
        dx_ref[...] = dxp_ref[...] + _dot(dproj, win_ref[...])

    return pl.pallas_call(
        body, name="in_proj_bwd", grid=(n,),
        in_specs=[_row_spec(tm, GATE_COLS), _row_spec(tm, 3 * CONV_W), _row_spec(tm, SSM_W), _row_spec(tm, XATTN_W),
                  _row_spec(tm, D_MODEL), _const_spec((IN_COLS, D_MODEL))],
        out_specs=[_row_spec(tm, D_MODEL), _row_spec(tm, IN_COLS)],
        out_shape=[_sds((s_len, D_MODEL), f32), _sds((s_len, IN_COLS), bf16)],
        compiler_params=_cparams(("parallel",)),
    )(dgp, dconv, du, dq, dxp, w_in_t)


N_CHIP = 4
CHIP_STEPS = [(1, 1), (1, 0), (0, 1), (0, 0)]


def _flip(v, d):
    return 1 - v if d else v


def _chip_order():
    x, y, _ = _mesh_place()
    return jnp.stack([2 * _flip(x, dx) + _flip(y, dy) for dx, dy in CHIP_STEPS]).astype(jnp.int32)


def _weight_grads_scatter(problems, name):
    dims = []
    first = 0
    for a_t, b, tm, tt in problems:
        m, s_len = a_t.shape
        w = b.shape[1] // N_DEV
        tm, tt = min(tm, m), min(tt, s_len)
        assert m % tm == 0 and s_len % tt == 0
        nm, nt = m // tm, s_len // tt
        dims.append(dict(m=m, w=w, tm=tm, tt=tt, nm=nm, nt=nt, first=first, steps=N_CHIP * nm * nt))
        first += N_CHIP * nm * nt
    n_prob, total = len(problems), first
    n_scratch = 9

    def place(d, s):
        local = jnp.clip(s - d["first"], 0, d["steps"] - 1)
        return local // (d["nm"] * d["nt"]), (local // d["nt"]) % d["nm"], local % d["nt"]

    def run(d, q, im, t, a_ref, b_ref, recv_ref, acc_ref, send_ref, sib_ref, stash_ref,
            d2d_send, d2d_recv, ici_send, ici_recv, local_sem):
        tm, w, nm, nt = d["tm"], d["w"], d["nm"], d["nt"]
        x, y, c = _mesh_place()
        mesh_id = pl.DeviceIdType.MESH

        @pl.when(t == 0)
        def _():
            acc_ref[...] = jnp.zeros_like(acc_ref)

        acc_ref[...] += _dot(a_ref[...], b_ref[...])

        def to_sibling(qq, imm):
            rows = pl.ds(pl.multiple_of(imm * tm, tm), tm)
            return pltpu.make_async_remote_copy(
                src_ref=send_ref.at[qq, 0, rows, :], dst_ref=sib_ref.at[qq, rows, :],
                send_sem=d2d_send.at[qq], recv_sem=d2d_recv.at[qq, imm],
                device_id=(x, y, 1 - c), device_id_type=mesh_id)

        def finish_tile(qq, imm):
            rows = pl.ds(pl.multiple_of(imm * tm, tm), tm)
            to_sibling(qq, imm).wait_recv()
            both = stash_ref[...] + sib_ref[qq, rows, :].astype(f32)
            send_ref[qq, 1, rows, :] = both.astype(bf16)
            for step, (dx, dy) in enumerate(CHIP_STEPS):
                @pl.when(qq == step)
                def _(step=step, dx=dx, dy=dy):
                    src, dst = send_ref.at[step, 1, rows, :], recv_ref.at[step, rows, :]
                    if dx or dy:
                        pltpu.make_async_remote_copy(
                            src_ref=src, dst_ref=dst, send_sem=ici_send.at[step], recv_sem=ici_recv.at[step],
                            device_id=(_flip(x, dx), _flip(y, dy), c), device_id_type=mesh_id).start()
                    else:
                        pltpu.make_async_copy(src, dst, local_sem).start()

        @pl.when(t == nt - 1)
        def _():
            tile = q * nm + im

            @pl.when(tile > 0)
            def _():
                finish_tile((tile - 1) // nm, (tile - 1) % nm)

            rows = pl.ds(pl.multiple_of(im * tm, tm), tm)
            for core in (0, 1):
                @pl.when(c == core)
                def _(core=core):
                    other = 1 - core
                    send_ref[q, 0, rows, :] = acc_ref[:, other * w:(other + 1) * w].astype(bf16)
                    stash_ref[...] = acc_ref[:, core * w:(core + 1) * w]
            to_sibling(q, im).start()

            @pl.when(tile == N_CHIP * nm - 1)
            def _():
                finish_tile(q, im)
                for step, (dx, dy) in enumerate(CHIP_STEPS):
                    pltpu.make_async_remote_copy(
                        src_ref=send_ref.at[step, 0], dst_ref=sib_ref.at[step],
                        send_sem=d2d_send.at[step], recv_sem=d2d_recv.at[step, 0],
                        device_id=(x, y, 1 - c), device_id_type=mesh_id).wait_send()
                    src, dst = send_ref.at[step, 1], recv_ref.at[step]
                    if dx or dy:
                        pltpu.make_async_remote_copy(
                            src_ref=src, dst_ref=dst, send_sem=ici_send.at[step], recv_sem=ici_recv.at[step],
                            device_id=(_flip(x, dx), _flip(y, dy), c), device_id_type=mesh_id).wait()
                    else:
                        pltpu.make_async_copy(src, dst, local_sem).wait()

    def body(order_ref, *refs):
        del order_ref
        s = pl.program_id(0)
        operands, rest = refs[:2 * n_prob], refs[2 * n_prob:]
        results, scratch = rest[:n_prob], rest[n_prob:]
        for k, d in enumerate(dims):
            @pl.when((s >= d["first"]) & (s < d["first"] + d["steps"]))
            def _(k=k, d=d):
                q, im, t = place(d, s)
                run(d, q, im, t, operands[2 * k], operands[2 * k + 1], results[k],
                    *scratch[n_scratch * k:n_scratch * (k + 1)])

    in_specs, scratch_shapes = [], []
    for d in dims:
        def a_map(s, order, d=d):
            _, im, t = place(d, s)
            return im, t

        def b_map(s, order, d=d):
            q, _, t = place(d, s)
            return t, order[q]

        in_specs += [pl.BlockSpec((d["tm"], d["tt"]), a_map), pl.BlockSpec((d["tt"], 2 * d["w"]), b_map)]
        scratch_shapes += [pltpu.VMEM((d["tm"], 2 * d["w"]), f32), pltpu.VMEM((N_CHIP, 2, d["m"], d["w"]), bf16),
                           pltpu.VMEM((N_CHIP, d["m"], d["w"]), bf16), pltpu.VMEM((d["tm"], d["w"]), f32),
                           pltpu.SemaphoreType.DMA((N_CHIP,)), pltpu.SemaphoreType.DMA((N_CHIP, d["nm"])),
                           pltpu.SemaphoreType.DMA((N_CHIP - 1,)), pltpu.SemaphoreType.DMA((N_CHIP - 1,)),
                           pltpu.SemaphoreType.DMA]
    grid_spec = pltpu.PrefetchScalarGridSpec(
        num_scalar_prefetch=1, grid=(total,), in_specs=in_specs,
        out_specs=[pl.BlockSpec(memory_space=pl.ANY)] * n_prob, scratch_shapes=scratch_shapes)
    return pl.pallas_call(
        body, name=name, grid_spec=grid_spec,
        out_shape=[_sds((N_CHIP, d["m"], d["w"]), bf16) for d in dims],
        compiler_params=_cparams(("arbitrary",)),
    )(_chip_order(), *[op for a_t, b, _, _ in problems for op in (a_t, b)])


def _adamw(w, g, m, v):
    m = ADAM_B1 * m + (1.0 - ADAM_B1) * g
    v = ADAM_B2 * v + (1.0 - ADAM_B2) * jnp.square(g)
    m_hat = m / (1.0 - ADAM_B1 ** ADAM_STEP)
    v_hat = v / (1.0 - ADAM_B2 ** ADAM_STEP)
    delta = -ADAM_LR * (m_hat / (jnp.sqrt(v_hat) + ADAM_EPS) + ADAM_WD * w)
    return delta, m, v


def _sum_parts(p_ref):
    g = p_ref[0].astype(f32)
    for j in range(1, p_ref.shape[0]):
        g = g + p_ref[j].astype(f32)
    return g


def _adamw_update(w, m, v, parts, name, transposed):
    rows, cols = w.shape
    n_parts = parts.shape[0]
    if transposed:
        tc = 256
        steps = cols // tc
        p_spec = pl.BlockSpec((n_parts, tc, rows), lambda i: (0, i, 0))
        spec = pl.BlockSpec((rows, tc), lambda i: (0, i))
    else:
        tr = next(t for t in (256, 128, 64, 32, 16, 8) if rows % t == 0)
        steps = rows // tr
        p_spec = pl.BlockSpec((n_parts, tr, cols), lambda i: (0, i, 0))
        spec = pl.BlockSpec((tr, cols), lambda i: (i, 0))

    def body(w_ref, p_ref, m_ref, v_ref, g_ref, d_ref, nm_ref, nv_ref):
        g = _sum_parts(p_ref)
        if transposed:
            g = g.T
        g_ref[...] = g
        d_ref[...], nm_ref[...], nv_ref[...] = _adamw(w_ref[...], g, m_ref[...], v_ref[...])

    return pl.pallas_call(
        body, name=name, grid=(steps,),
        in_specs=[spec, p_spec, spec, spec], out_specs=[spec] * 4,
        out_shape=[_sds((rows, cols), f32)] * 4,
        compiler_params=_cparams(("parallel",)),
    )(w, parts, m, v)


def _adamw_whole(ws, ms, vs, parts, transposed):
    n = len(ws)

    def body(*refs):
        w_refs, m_refs, v_refs, p_refs = (refs[j * n:(j + 1) * n] for j in range(4))
        out_refs = refs[4 * n:]
        for a in range(n):
            g = _sum_parts(p_refs[a])
            if transposed[a]:
                g = g.T
            d, nm, nv = _adamw(w_refs[a][...], g, m_refs[a][...], v_refs[a][...])
            for j, val in enumerate((g, d, nm, nv)):
                out_refs[j * n + a][...] = val

    res = pl.pallas_call(
        body, name="adamw_small_weights",
        out_shape=[_sds(w.shape, f32) for _ in range(4) for w in ws],
        compiler_params=_cparams(),
    )(*ws, *ms, *vs, *parts)
    return [res[j * n:(j + 1) * n] for j in range(4)]


SMALL_GROUPS = [
    (["b_gate", "ln1_g", "ln1_b", "b_up", "b_down", "ln2_g", "ln2_b", "ssm_d"], 1),
    (["ssm_lam_re", "ssm_lam_im", "ssm_c_re", "ssm_c_im", "ssm_b_re", "ssm_b_im"], 0),
    (["conv_w"], 0),
    (["ssm_log_dt"], 0),
]


def _sum_small(group_parts):
    def body(*refs):
        n = len(refs) // 2
        for p_ref, o_ref in zip(refs[:n], refs[n:]):
            o_ref[...] = _sum_parts(p_ref)

    return pl.pallas_call(
        body, name="sum_small",
        out_shape=[_sds(p.shape[1:], f32) for p in group_parts],
        compiler_params=_cparams(),
    )(*group_parts)


def _adamw_small(ws, ms, vs, group_sums):
    names = [k for group, _ in SMALL_GROUPS for k in group]
    n = len(names)

    def body(*refs):
        w_refs, m_refs, v_refs = (dict(zip(names, refs[j * n:(j + 1) * n])) for j in range(3))
        p_refs = refs[3 * n:3 * n + len(SMALL_GROUPS)]
        out_refs = [dict(zip(names, refs[3 * n + len(SMALL_GROUPS) + j * n:][:n])) for j in range(4)]
        for (group, axis), p_ref in zip(SMALL_GROUPS, p_refs):
            total = p_ref[...]
            off = 0
            for k in group:
                size = SMALL[k][axis]
                g = total[:, off:off + size] if axis == 1 else total[off:off + size, :]
                off += size
                d, nm, nv = _adamw(w_refs[k][...], g, m_refs[k][...], v_refs[k][...])
                for j, val in enumerate((g, d, nm, nv)):
                    out_refs[j][k][...] = val

    res = pl.pallas_call(
        body, name="adamw_small",
        out_shape=[_sds(SMALL[k], f32) for _ in range(4) for k in names],
        compiler_params=_cparams(),
    )(*[ws[k] for k in names], *[ms[k] for k in names], *[vs[k] for k in names], *group_sums)
    return [dict(zip(names, res[j * n:(j + 1) * n])) for j in range(4)]


def _ssm_discretize(lam_re, lam_im, log_dt, b_re, b_im):
    dt = jnp.exp(log_dt)[:, None]
    mag = jnp.exp(lam_re * dt)
    abar_r = mag * jnp.cos(lam_im * dt)
    abar_i = mag * jnp.sin(lam_im * dt)
    den = lam_re * lam_re + lam_im * lam_im
    nr = abar_r - 1.0
    ni = abar_i
    kr = (nr * lam_re + ni * lam_im) / den
    ki = (ni * lam_re - nr * lam_im) / den
    bbar_r = kr[:, None, :] * b_re - ki[:, None, :] * b_im
    bbar_i = kr[:, None, :] * b_im + ki[:, None, :] * b_re
    return abar_r, abar_i, bbar_r, bbar_i


def _state_layout(re, im):
    parts = []
    for half in range(N_HALF):
        cols = slice(half * HALF_STATE, (half + 1) * HALF_STATE)
        parts += [re[..., cols], im[..., cols]]
    return jnp.concatenate(parts, axis=-1)


def _state_unlayout(a):
    re = jnp.concatenate([a[..., _half_cols(h)[0]] for h in range(N_HALF)], axis=-1)
    im = jnp.concatenate([a[..., _half_cols(h)[1]] for h in range(N_HALF)], axis=-1)
    return re, im


def _abar_powers(abar_r, abar_i):
    pr, pi = abar_r.reshape(1, N_STATE), abar_i.reshape(1, N_STATE)
    while pr.shape[0] < SSM_SEG:
        tr, ti = pr[-1:], pi[-1:]
        pr, pi = (jnp.concatenate([pr, pr * tr - pi * ti], axis=0), jnp.concatenate([pi, pr * ti + pi * tr], axis=0))
    return _state_layout(pr, pi)


HALF_GROUPS = SSM_GROUPS // N_HALF


def _half_block_diag(blocks):
    _, r, c = blocks.shape
    eye = jnp.eye(HALF_GROUPS, dtype=blocks.dtype)
    b4 = blocks.reshape(N_HALF, HALF_GROUPS, r, c)
    return jnp.einsum("ngrc,gk->ngrkc", b4, eye).reshape(N_HALF, HALF_GROUPS * r, HALF_GROUPS * c)


def _half_diag_blocks(mat, r, c):
    eye = jnp.eye(HALF_GROUPS, dtype=mat.dtype)
    m5 = mat.reshape(N_HALF, HALF_GROUPS, r, HALF_GROUPS, c)
    return jnp.einsum("ngrkc,gk->ngrc", m5, eye).reshape(SSM_GROUPS, r, c)


BIG = ["w_in", "w_conv_out", "w_glu", "w_kv", "w_xattn_out", "w_out", "w_up", "w_down"]
GATHER_TRANSPOSED = ["w_conv_out", "w_glu", "w_xattn_out", "w_up"]
PARTS_TRANSPOSED = ["w_in", "w_kv", "w_out", "w_down"]
SMALL = {"b_gate": (1, GATE_COLS), "conv_w": (3, CONV_W), "ssm_lam_re": (SSM_GROUPS, SSM_STATE),
         "ssm_lam_im": (SSM_GROUPS, SSM_STATE), "ssm_log_dt": (1, SSM_GROUPS),
         "ssm_b_re": (SSM_W, SSM_STATE), "ssm_b_im": (SSM_W, SSM_STATE),
         "ssm_c_re": (SSM_W, SSM_STATE), "ssm_c_im": (SSM_W, SSM_STATE), "ssm_d": (1, SSM_W),
         "ln1_g": (1, D_MODEL), "ln1_b": (1, D_MODEL), "b_up": (1, D_FF), "b_down": (1, D_MODEL),
         "ln2_g": (1, D_MODEL), "ln2_b": (1, D_MODEL)}
WEIGHTS = ["w_in", "b_gate", "conv_w", "w_conv_out", "ssm_lam_re", "ssm_lam_im", "ssm_log_dt", "ssm_b_re", "ssm_b_im",
           "ssm_c_re", "ssm_c_im", "ssm_d", "w_glu", "w_kv", "w_xattn_out", "w_out", "ln1_g", "ln1_b", "w_up", "b_up",
           "w_down", "b_down", "ln2_g", "ln2_b"]


def _local_step(x, mem, tgt, full, late, small):
    lam_re, lam_im, log_dt = small["ssm_lam_re"], small["ssm_lam_im"], small["ssm_log_dt"].reshape(SSM_GROUPS)
    c_shape = (SSM_GROUPS, SSM_GROUP, SSM_STATE)
    disc, disc_vjp = jax.vjp(_ssm_discretize, lam_re, lam_im, log_dt,
                             small["ssm_b_re"].reshape(c_shape), small["ssm_b_im"].reshape(c_shape))
    abar_r, abar_i, bbar_r, bbar_i = disc
    pw = _abar_powers(abar_r, abar_i)
    c_re, c_im = small["ssm_c_re"].reshape(c_shape), small["ssm_c_im"].reshape(c_shape)
    b_half = jnp.concatenate([_half_block_diag(bbar_r), _half_block_diag(bbar_i)], axis=2).astype(bf16)
    c_half = jnp.concatenate([_half_block_diag(c_re.transpose(0, 2, 1)), -_half_block_diag(c_im.transpose(0, 2, 1))],
                             axis=1).astype(bf16)

    s_len = x.shape[0]
    stack = lambda a: a.reshape(-1, a.shape[-1])
    kv, k_t, memb = _kv_proj(mem, full["w_kv"])
    (xbt, g, cin, u, q, ain, ob, aint, obt), side = _in_proj(
        x, full["w_in"], small["b_gate"], small["conv_w"], kv,
        [late[k] for k in ("w_glu", "w_conv_out", "w_xattn_out", "w_out", "w_up")])
    w_glu_t, w_co_t, w_xo_t, w_out, w_up_t = (stack(a) for a in side)
    y_ssm, cm_all, side = _ssm_fwd(u, b_half, c_half, pw, small["ssm_d"], [late["w_down"]])
    w_down = stack(side[0])
    ysbt, mb, xhat1, rstd1 = _mid_fwd(y_ssm, g, ain, ob, x, w_glu_t, w_co_t, w_xo_t, w_out,
                                      small["ln1_g"], small["ln1_b"])
    (x1bt, hdn, dr2bt, dpre, dx1, loss, dl2g, dl2b, dbdn, dbup) = _mlp_fwd_bwd(
        xhat1, tgt, small["ln1_g"], small["ln1_b"], w_up_t, small["b_up"], w_down,
        small["b_down"], small["ln2_g"], small["ln2_b"])
    (dxp, dr1bt, dgp, dya, dyc, dglu, dyssm, dl1g, dl1b, dbg) = _mid_bwd(
        dx1, xhat1, rstd1, g, ain, ob, y_ssm, small["ln1_g"], w_out, w_glu_t, w_co_t, w_xo_t)
    du, db_half, dc_half, da8, dd = _ssm_bwd(u, dyssm, cm_all, b_half, c_half, pw, small["ssm_d"])
    dabar_r, dabar_i = _state_unlayout(jnp.sum(da8, axis=0))
    dbbar_r = _half_diag_blocks(db_half[:, :, :HALF_STATE], SSM_GROUP, SSM_STATE)
    dbbar_i = _half_diag_blocks(db_half[:, :, HALF_STATE:], SSM_GROUP, SSM_STATE)
    g_shape = (SSM_GROUPS, SSM_STATE)
    dlam_re, dlam_im, dlog_dt, db_re, db_im = disc_vjp(
        (dabar_r.reshape(g_shape), dabar_i.reshape(g_shape), dbbar_r, dbbar_i))
    dc_re = _half_diag_blocks(dc_half[:, :HALF_STATE, :], SSM_STATE, SSM_GROUP).transpose(0, 2, 1)
    dc_im = -_half_diag_blocks(dc_half[:, HALF_STATE:, :], SSM_STATE, SSM_GROUP).transpose(0, 2, 1)

    small_grads = {
        "b_gate": dbg, "ssm_lam_re": dlam_re, "ssm_lam_im": dlam_im, "ssm_log_dt": dlog_dt,
        "ssm_b_re": db_re, "ssm_b_im": db_im, "ssm_c_re": dc_re, "ssm_c_im": dc_im, "ssm_d": dd,
        "ln1_g": dl1g, "ln1_b": dl1b, "b_up": dbup, "b_down": dbdn, "ln2_g": dl2g, "ln2_b": dl2b,
    }
    small_grads = {k: a.reshape(SMALL[k]) for k, a in small_grads.items()}
    groups = [(group, axis) for group, axis in SMALL_GROUPS if group != ["conv_w"]]
    stacks = [jnp.concatenate([small_grads[k] for k in group], axis=axis) if len(group) > 1 else small_grads[group[0]]
              for group, axis in groups]
    n_rowvec = stacks[0].shape[1]
    stacks[0] = jnp.concatenate([stacks[0], loss], axis=1)
    dense = lambda a: a.reshape(-1, LANES) if a.size % LANES == 0 else a
    dconv, dq, dkv, group_parts = _branch_bwd(dya, dyc, cin, q, kv, k_t, small["conv_w"], w_co_t, w_xo_t,
                                              [dense(a) for a in stacks])
    dx, dproj = _in_proj_bwd(dgp, dconv, du, dq, dxp, full["w_in"])
    tm, tt = 512, 2048
    products = {
        "w_down": (dr2bt, hdn, tm, 2 * tt), "w_up": (x1bt, dpre, tm, 2 * tt), "w_out": (dr1bt, mb, D_MODEL, s_len),
        "w_glu": (ysbt, dglu, tm, s_len), "w_conv_out": (aint, dya, tm, s_len), "w_xattn_out": (obt, dyc, tm, s_len),
        "w_in": (xbt, dproj, D_MODEL, tt),
    }
    recv = {k: _weight_grads_scatter([problem], "d" + k)[0] for k, problem in products.items() if k != "w_xattn_out"}
    recv["w_xattn_out"], recv["w_kv"] = _weight_grads_scatter(
        [products["w_xattn_out"], (dkv.T.astype(bf16), memb, D_MODEL, MEM_LEN)], "dw_xattn_out_kv")
    sums = _sum_small(group_parts)
    group_sums = dict(zip([tuple(group) for group, _ in groups], [s.reshape(a.shape) for s, a in zip(sums, stacks)]))
    group_sums[("conv_w",)] = sums[-1][0:3]
    first = tuple(groups[0][0])
    loss_all = group_sums[first][0, n_rowvec]
    group_sums[first] = group_sums[first][:, :n_rowvec]
    return loss_all, dx, recv, [group_sums[tuple(group)] for group, _ in SMALL_GROUPS]


def kernel(x, mem, w_in, b_gate, conv_w, w_conv_out, ssm_lam_re, ssm_lam_im, ssm_log_dt, ssm_b_re, ssm_b_im, ssm_c_re, ssm_c_im, ssm_d, w_glu, w_kv, w_xattn_out, w_out, ln1_g, ln1_b, w_up, b_up, w_down, b_down, ln2_g, ln2_b, loss_target, m_w_in, m_b_gate, m_conv_w, m_w_conv_out, m_ssm_lam_re, m_ssm_lam_im, m_ssm_log_dt, m_ssm_b_re, m_ssm_b_im, m_ssm_c_re, m_ssm_c_im, m_ssm_d, m_w_glu, m_w_kv, m_w_xattn_out, m_w_out, m_ln1_g, m_ln1_b, m_w_up, m_b_up, m_w_down, m_b_down, m_ln2_g, m_ln2_b, v_w_in, v_b_gate, v_conv_w, v_w_conv_out, v_ssm_lam_re, v_ssm_lam_im, v_ssm_log_dt, v_ssm_b_re, v_ssm_b_im, v_ssm_c_re, v_ssm_c_im, v_ssm_d, v_w_glu, v_w_kv, v_w_xattn_out, v_w_out, v_ln1_g, v_ln1_b, v_w_up, v_b_up, v_w_down, v_b_down, v_ln2_g, v_ln2_b):
    w = dict(w_in=w_in, b_gate=b_gate, conv_w=conv_w, w_conv_out=w_conv_out, ssm_lam_re=ssm_lam_re,
             ssm_lam_im=ssm_lam_im, ssm_log_dt=ssm_log_dt, ssm_b_re=ssm_b_re, ssm_b_im=ssm_b_im, ssm_c_re=ssm_c_re,
             ssm_c_im=ssm_c_im, ssm_d=ssm_d, w_glu=w_glu, w_kv=w_kv, w_xattn_out=w_xattn_out, w_out=w_out,
             ln1_g=ln1_g, ln1_b=ln1_b, w_up=w_up, b_up=b_up, w_down=w_down, b_down=b_down, ln2_g=ln2_g, ln2_b=ln2_b)
    m = dict(w_in=m_w_in, b_gate=m_b_gate, conv_w=m_conv_w, w_conv_out=m_w_conv_out, ssm_lam_re=m_ssm_lam_re,
             ssm_lam_im=m_ssm_lam_im, ssm_log_dt=m_ssm_log_dt, ssm_b_re=m_ssm_b_re, ssm_b_im=m_ssm_b_im,
             ssm_c_re=m_ssm_c_re, ssm_c_im=m_ssm_c_im, ssm_d=m_ssm_d, w_glu=m_w_glu, w_kv=m_w_kv,
             w_xattn_out=m_w_xattn_out, w_out=m_w_out, ln1_g=m_ln1_g, ln1_b=m_ln1_b, w_up=m_w_up, b_up=m_b_up,
             w_down=m_w_down, b_down=m_b_down, ln2_g=m_ln2_g, ln2_b=m_ln2_b)
    v = dict(w_in=v_w_in, b_gate=v_b_gate, conv_w=v_conv_w, w_conv_out=v_w_conv_out, ssm_lam_re=v_ssm_lam_re,
             ssm_lam_im=v_ssm_lam_im, ssm_log_dt=v_ssm_log_dt, ssm_b_re=v_ssm_b_re, ssm_b_im=v_ssm_b_im,
             ssm_c_re=v_ssm_c_re, ssm_c_im=v_ssm_c_im, ssm_d=v_ssm_d, w_glu=v_w_glu, w_kv=v_w_kv,
             w_xattn_out=v_w_xattn_out, w_out=v_w_out, ln1_g=v_ln1_g, ln1_b=v_ln1_b, w_up=v_w_up, b_up=v_b_up,
             w_down=v_w_down, b_down=v_b_down, ln2_g=v_ln2_g, ln2_b=v_ln2_b)
    out_shapes = {k: a.shape for k, a in w.items()}
    swapped = ("w_in", "ssm_b_re", "ssm_b_im")

    def shard2d(k, a):
        if k in swapped:
            a = jnp.swapaxes(a, -1, -2)
        if k in SMALL:
            return a.reshape((3, CONV_W // N_DEV) if k == "conv_w" else SMALL[k])
        return a[0]

    def result(k, a):
        if k in swapped:
            shape = out_shapes[k]
            return jnp.swapaxes(a.reshape(shape[:-2] + (shape[-1], shape[-2])), -1, -2)
        return a.reshape(out_shapes[k])

    w, m, v = ({k: shard2d(k, a) for k, a in d.items()} for d in (w, m, v))

    shards = {k: w[k].T.astype(bf16) if k in GATHER_TRANSPOSED else w[k].astype(bf16) for k in BIG}
    conv_pad = jnp.pad(w["conv_w"], ((0, 5), (0, LANES - CONV_W // N_DEV)))
    early = ["w_in", "w_kv"]
    gathered = _all_gather([shards[k] for k in early] + [conv_pad], "gather_weights")
    full = {k: a.reshape(-1, a.shape[-1]) for k, a in zip(early, gathered[:-1])}
    late = {k: shards[k] for k in BIG if k not in early}
    conv_full = gathered[-1][:, :3, :CONV_W // N_DEV].transpose(1, 0, 2).reshape(3, CONV_W)
    small = {k: (conv_full if k == "conv_w" else w[k]) for k in SMALL}

    loss, dx, recv, group_sums = _local_step(x[0], mem[0], loss_target[0], full, late, small)

    grads, deltas, new_m, new_v = {}, {}, {}, {}
    tiled = ["w_in", "w_up", "w_down"]
    for k in tiled:
        res = _adamw_update(w[k], m[k], v[k], recv[k], "adamw_" + k, transposed=k in PARTS_TRANSPOSED)
        grads[k], deltas[k], new_m[k], new_v[k] = res
    whole = [k for k in BIG if k not in tiled]
    res = _adamw_whole([w[k] for k in whole], [m[k] for k in whole], [v[k] for k in whole], [recv[k] for k in whole],
                       [k in PARTS_TRANSPOSED for k in whole])
    for d, vals in zip((grads, deltas, new_m, new_v), res):
        d.update(zip(whole, vals))

    widen = lambda k, a: jnp.tile(a, (1, N_DEV)) if k == "conv_w" else a
    res = _adamw_small(small, {k: widen(k, m[k]) for k in SMALL}, {k: widen(k, v[k]) for k in SMALL}, group_sums)
    dev = _slot(_mesh_place())
    for d, small_res in zip((grads, deltas, new_m, new_v), res):
        for k, a in small_res.items():
            if k == "conv_w":
                a = lax.dynamic_slice_in_dim(a, dev * (CONV_W // N_DEV), CONV_W // N_DEV, axis=1)
            d[k] = a

    outs = [loss, dx[None]]
    for d in (grads, deltas, new_m, new_v):
        outs += [result(k, d[k]) for k in WEIGHTS]
    return tuple(outs)
```
